```python
import math
import jax
import jax.numpy as jnp
from jax import lax
import numpy as np


D_MODEL = 1024
BATCH = 8
SEQ = 4096
DEPTH = 2

CTX_LEN = 256
GRID_W = 64
N_EVEN = (DEPTH + 1) // 2
N_ODD = DEPTH // 2
EPS = 1e-6

MLA_HEADS = 8
QK_NOPE = 64
QK_ROPE = 32
QK_HEAD = QK_NOPE + QK_ROPE
V_HEAD = 64
Q_LORA = 256
KV_LORA = 128
MLA_WIDTH = MLA_HEADS * V_HEAD
ROPE_BASE = 10000.0
ATTN_BLOCK = 128

S5_WIDTH = 512
S5_GROUP = 16
S5_GROUPS = S5_WIDTH // S5_GROUP
S5_STATE = 64
S5_MIN_STEP = 1e-3
S5_MAX_STEP = 1e-1

EVEN_IN = Q_LORA + KV_LORA + QK_ROPE + S5_WIDTH
EVEN_OUT = MLA_WIDTH + S5_WIDTH

HY_WIDTH = D_MODEL
FILT_EMB = 33
FILT_BANDS = (FILT_EMB - 1) // 2
FILT_HIDDEN = 64
SHORT_CONV = 3
HY_MIN_DECAY = -math.log(1e-2) / 1.5
HY_MAX_DECAY = -math.log(1e-2) / 0.3

N_EXPERTS = 64
TOP_K = 6
EXPERT_FF = 256
SHARED_FF = 256
ROUTE_SCALE = 2.5
MOE_BLOCK = 256

kernel_name = 'hybrid_mla_s5_hyena_moe_prefix_dit'


def rmsnorm(x, g=None):
    xf = x.astype(jnp.float32)
    y = xf * lax.rsqrt(jnp.mean(xf * xf, axis=-1, keepdims=True) + EPS)
    if g is not None:
        y = y * g.astype(jnp.float32)
    return y.astype(x.dtype)


def modulate(h, shift, scale):
    return h * (1.0 + scale) + shift


def axial_rope_tables(rows):
    t = jnp.arange(rows * GRID_W)
    row = (t // GRID_W).astype(jnp.float32)
    col = (t % GRID_W).astype(jnp.float32)
    n_freq = QK_ROPE // 4
    inv = ROPE_BASE ** (-jnp.arange(n_freq, dtype=jnp.float32) / n_freq)
    ang = jnp.concatenate([row[:, None] * inv, col[:, None] * inv], axis=-1)
    return jnp.cos(ang), jnp.sin(ang)


def apply_rope(x, cos, sin):
    xf = x.astype(jnp.float32).reshape(x.shape[:-1] + (QK_ROPE // 2, 2))
    xr, xi = xf[..., 0], xf[..., 1]
    cs, sn = cos[:, None, :], sin[:, None, :]
    out = jnp.stack([xr * cs - xi * sn, xr * sn + xi * cs], axis=-1)
    return out.reshape(x.shape).astype(x.dtype)


def rope_tail(t, cos, sin):
    return jnp.concatenate([t[..., :QK_NOPE], apply_rope(t[..., QK_NOPE:], cos, sin)], axis=-1)


def mla_keys_values(part, kv_norm, w_ukv, k_qk, cos, sin):
    B, L, _ = part.shape
    c_kv = rmsnorm(part[..., :KV_LORA], kv_norm)
    kv = (c_kv @ w_ukv).reshape(B, L, MLA_HEADS, QK_NOPE + V_HEAD)
    k_r = jnp.broadcast_to(part[..., None, KV_LORA:KV_LORA + QK_ROPE], (B, L, MLA_HEADS, QK_ROPE))
    k = rmsnorm(jnp.concatenate([kv[..., :QK_NOPE], k_r], axis=-1), k_qk)
    if cos is not None:
        k = rope_tail(k, cos, sin)
    return k, kv[..., QK_NOPE:]


def attend_blocked(q, k, v):
    B, L, H, Dk = q.shape
    nb = L // ATTN_BLOCK
    qb = q.reshape(B, nb, ATTN_BLOCK, H, Dk).transpose(1, 0, 2, 3, 4)
    scale = QK_HEAD ** -0.5

    def one_block(q_blk):
        s = jnp.einsum('bqhd,bkhd->bhqk', q_blk, k, preferred_element_type=jnp.float32) * scale
        p = jax.nn.softmax(s, axis=-1)
        return jnp.einsum('bhqk,bkhd->bqhd', p.astype(v.dtype), v)

    o = lax.map(one_block, qb)
    return o.transpose(1, 0, 2, 3, 4).reshape(B, L, H * v.shape[-1])


def _ssm_combine(left, right):
    a_l, b_l = left
    a_r, b_r = right
    return a_r * a_l, a_r * b_l + b_r


def s5_scan(u, lam_bar, b_bar, s0=None):
    bu = jnp.einsum('blgh,gph->blgp', u.astype(jnp.complex64), b_bar)
    if s0 is not None:
        bu = bu.at[:, 0].add(lam_bar * s0)
    a = jnp.broadcast_to(lam_bar, bu.shape)
    _, states = lax.associative_scan(_ssm_combine, (a, bu), axis=1)
    return states


def s5_mixer(u, u_ctx, lam_re, lam_im, log_step, b_re, b_im, c_re, c_im, d_skip, glu_w, glu_b):
    B, L, _ = u.shape
    Lc = u_ctx.shape[1]
    ug = u.astype(jnp.float32).reshape(B, L, S5_GROUPS, S5_GROUP)
    ugc = u_ctx.astype(jnp.float32).reshape(B, Lc, S5_GROUPS, S5_GROUP)
    y = ug * d_skip.astype(jnp.float32).reshape(S5_GROUPS, S5_GROUP)
    for direction in range(2):
        lam = lax.complex(jnp.minimum(lam_re[direction].astype(jnp.float32), -1e-4),
                          lam_im[direction].astype(jnp.float32))
        step = jnp.exp(log_step[direction].astype(jnp.float32))[:, None]
        lam_bar = jnp.exp(lam * step)
        b_mat = lax.complex(b_re[direction].astype(jnp.float32), b_im[direction].astype(jnp.float32))
        b_bar = ((lam_bar - 1.0) / lam)[..., None] * b_mat
        c_mat = lax.complex(c_re[direction].astype(jnp.float32), c_im[direction].astype(jnp.float32))
        if direction == 0:
            uc, ul = ugc, ug
        else:
            uc, ul = ugc[:, ::-1], ug[:, ::-1]
        ctx_final = s5_scan(uc, lam_bar, b_bar)[:, -1]
        states = s5_scan(ul, lam_bar, b_bar, ctx_final)
        yd = jnp.einsum('blgp,ghp->blgh', states, c_mat).real
        y = y + (yd if direction == 0 else yd[:, ::-1])
    y = y.reshape(B, L, S5_WIDTH)
    z = jax.nn.gelu(y)
    out = z * jax.nn.sigmoid(z @ glu_w.astype(jnp.float32) + glu_b.astype(jnp.float32))
    return out.astype(u.dtype)


def even_mixer(h, h_ctx, cos, sin, w_in, q_norm, w_uq, kv_norm, w_ukv, q_qk, k_qk,
               lam_re, lam_im, log_step, b_re, b_im, c_re, c_im, d_skip, glu_w, glu_b, w_out):
    B, L, _ = h.shape
    proj = h @ w_in
    proj_ctx = h_ctx @ w_in[:, Q_LORA:]
    q = (rmsnorm(proj[..., :Q_LORA], q_norm) @ w_uq).reshape(B, L, MLA_HEADS, QK_HEAD)
    q = rope_tail(rmsnorm(q, q_qk), cos, sin)
    k_lat, v_lat = mla_keys_values(proj[..., Q_LORA:], kv_norm, w_ukv, k_qk, cos, sin)
    k_ctx, v_ctx = mla_keys_values(proj_ctx, kv_norm, w_ukv, k_qk, None, None)
    attn = attend_blocked(q, jnp.concatenate([k_lat, k_ctx], axis=1),
                          jnp.concatenate([v_lat, v_ctx], axis=1))
    s5 = s5_mixer(proj[..., Q_LORA + KV_LORA + QK_ROPE:], proj_ctx[..., KV_LORA + QK_ROPE:],
                  lam_re, lam_im, log_step, b_re, b_im, c_re, c_im, d_skip, glu_w, glu_b)
    return jnp.concatenate([attn, s5], axis=-1) @ w_out


def hyena_filter(L, w1, b1, w2, b2, w3, freq):
    f32 = jnp.float32
    t = jnp.linspace(0.0, 1.0, L, dtype=f32)[:, None]
    ang = 2.0 * math.pi * jnp.arange(L, dtype=f32)[:, None] / L
    bands = jnp.linspace(1e-4, FILT_BANDS - 1, FILT_BANDS, dtype=f32)
    z = jnp.concatenate([t, jnp.cos(bands * ang), -jnp.sin(bands * ang)], axis=-1)
    fr = freq.astype(f32)
    hid = jnp.sin(fr * (z @ w1.astype(f32) + b1.astype(f32)))
    hid = jnp.sin(fr * (hid @ w2.astype(f32) + b2.astype(f32)))
    hf = (hid @ w3.astype(f32)).reshape(L, 2, HY_WIDTH)
    deltas = jnp.linspace(HY_MIN_DECAY, HY_MAX_DECAY, HY_WIDTH, dtype=f32)
    hf = hf * jnp.exp(-t * deltas)[:, None, :]
    hf = hf * lax.rsqrt(jnp.sum(hf * hf, axis=(0, 1), keepdims=True) + EPS)
    return hf[:, 0], hf[:, 1]


def bidir_long_conv(z, h_fwd, h_bwd):
    B, L, Dh = z.shape
    zeros = jnp.zeros((L, Dh), jnp.float32)
    circ = (jnp.concatenate([h_fwd, zeros], axis=0)
            + jnp.concatenate([h_bwd[:1], zeros, h_bwd[:0:-1]], axis=0))
    zf = jnp.fft.rfft(z.astype(jnp.float32), n=2 * L, axis=1)
    cf = jnp.fft.rfft(circ, axis=0)
    return jnp.fft.irfft(zf * cf[None], n=2 * L, axis=1)[:, :L]


def hyena_mixer(h, w_in, conv_w, conv_b, f_w1, f_b1, f_w2, f_b2, f_w3, f_freq, bias, w_out):
    B, L, _ = h.shape
    p = h @ w_in
    pad = SHORT_CONV // 2
    pp = jnp.pad(p, ((0, 0), (pad, pad), (0, 0)))
    p = sum(pp[:, j:j + L] * conv_w[j] for j in range(SHORT_CONV)) + conv_b
    x0, x1, v = jnp.split(p, 3, axis=-1)
    h_fwd, h_bwd = hyena_filter(L, f_w1, f_b1, f_w2, f_b2, f_w3, f_freq)
    z = (v * x1).astype(jnp.float32)
    z = bidir_long_conv(z, h_fwd, h_bwd) + bias.astype(jnp.float32) * z
    y = x0 * z.astype(x0.dtype)
    return y @ w_out


def moe(h, router_w, router_b, w_gate, w_up, w_down, sh_gate, sh_up, sh_down):
    B, L, Dm = h.shape
    xf = h.reshape(B * L, Dm)
    T = B * L
    TK = T * TOP_K
    scores = jax.nn.sigmoid(jnp.einsum('td,de->te', xf, router_w, preferred_element_type=jnp.float32))
    _, idx = lax.top_k(scores + router_b.astype(jnp.float32), TOP_K)
    sel = jnp.take_along_axis(scores, idx, axis=-1)
    wts = sel / jnp.sum(sel, axis=-1, keepdims=True) * ROUTE_SCALE
    flat_e = idx.reshape(TK)
    order = jnp.argsort(flat_e)
    sorted_e = flat_e[order]
    counts = jnp.bincount(flat_e, length=N_EXPERTS)
    padded = (counts + MOE_BLOCK - 1) // MOE_BLOCK * MOE_BLOCK
    pad_end = jnp.cumsum(padded)
    pad_start = pad_end - padded
    start = jnp.cumsum(counts) - counts
    dest = pad_start[sorted_e] + jnp.arange(TK, dtype=jnp.int32) - start[sorted_e]
    n_blocks = -(-TK // MOE_BLOCK) + N_EXPERTS
    n_rows = n_blocks * MOE_BLOCK
    row_tok = jnp.zeros((n_rows,), jnp.int32).at[dest].set((order // TOP_K).astype(jnp.int32))
    row_w = jnp.zeros((n_rows,), jnp.float32).at[dest].set(wts.reshape(TK)[order])
    block_start = jnp.arange(n_blocks, dtype=jnp.int32) * MOE_BLOCK
    block_e = jnp.minimum(jnp.searchsorted(pad_end, block_start, side='right'), N_EXPERTS - 1)

    def expert_block(args):
        toks, w, e = args
        xb = xf[toks]
        hid = jax.nn.silu(xb @ w_gate[e]) * (xb @ w_up[e])
        return (hid @ w_down[e]) * w[:, None].astype(xb.dtype)

    ys = lax.map(expert_block, (row_tok.reshape(n_blocks, MOE_BLOCK),
                                row_w.reshape(n_blocks, MOE_BLOCK), block_e))
    routed = jnp.zeros_like(xf).at[row_tok].add(ys.reshape(n_rows, Dm))
    shared = (jax.nn.silu(xf @ sh_gate) * (xf @ sh_up)) @ sh_down
    return (routed + shared).reshape(B, L, Dm)


def setup_inputs(seed: int = 0) -> dict:
    key = jax.random.key(seed)
    ks = iter(jax.random.split(key, 64))
    f32 = jnp.float32
    D = D_MODEL

    def nrm(shape, s):
        return jax.random.normal(next(ks), shape, f32) * s

    n_idx = jnp.arange(S5_STATE, dtype=f32)
    s5_shape = (N_EVEN, 2, S5_GROUPS, S5_STATE)
    return {
        'x': nrm((BATCH, SEQ, D), 1.0),
        'c': nrm((BATCH, D), 1.0),
        'ctx': nrm((BATCH, CTX_LEN, D), 1.0),
        'c_ctx': nrm((D,), 1.0),
        'ada_w': nrm((DEPTH, D, 6 * D), 0.5 * D ** -0.5),
        'ada_b': nrm((DEPTH, 6 * D), 0.02),
        'ev_w_in': nrm((N_EVEN, D, EVEN_IN), D ** -0.5),
        'mla_q_norm': 1.0 + nrm((N_EVEN, Q_LORA), 0.05),
        'mla_w_uq': nrm((N_EVEN, Q_LORA, MLA_HEADS * QK_HEAD), Q_LORA ** -0.5),
        'mla_kv_norm': 1.0 + nrm((N_EVEN, KV_LORA), 0.05),
        'mla_w_ukv': nrm((N_EVEN, KV_LORA, MLA_HEADS * (QK_NOPE + V_HEAD)), KV_LORA ** -0.5),
        'mla_q_qknorm': 1.0 + nrm((N_EVEN, QK_HEAD), 0.05),
        'mla_k_qknorm': 1.0 + nrm((N_EVEN, QK_HEAD), 0.05),
        's5_lam_re': -0.5 * jnp.exp(nrm(s5_shape, 0.05)),
        's5_lam_im': math.pi * n_idx + nrm(s5_shape, 0.05),
        's5_log_step': jax.random.uniform(next(ks), (N_EVEN, 2, S5_GROUPS), f32,
                                          math.log(S5_MIN_STEP), math.log(S5_MAX_STEP)),
        's5_b_re': nrm((N_EVEN, 2, S5_GROUPS, S5_STATE, S5_GROUP), (2.0 * S5_GROUP) ** -0.5),
        's5_b_im': nrm((N_EVEN, 2, S5_GROUPS, S5_STATE, S5_GROUP), (2.0 * S5_GROUP) ** -0.5),
        's5_c_re': nrm((N_EVEN, 2, S5_GROUPS, S5_GROUP, S5_STATE), 0.5 ** 0.5),
        's5_c_im': nrm((N_EVEN, 2, S5_GROUPS, S5_GROUP, S5_STATE), 0.5 ** 0.5),
        's5_d': nrm((N_EVEN, S5_WIDTH), 1.0),
        's5_glu_w': nrm((N_EVEN, S5_WIDTH, S5_WIDTH), S5_WIDTH ** -0.5),
        's5_glu_b': nrm((N_EVEN, S5_WIDTH), 0.02),
        'ev_w_out': nrm((N_EVEN, EVEN_OUT, D), EVEN_OUT ** -0.5),
        'hy_w_in': nrm((N_ODD, D, 3 * HY_WIDTH), D ** -0.5),
        'hy_conv_w': nrm((N_ODD, SHORT_CONV, 3 * HY_WIDTH), SHORT_CONV ** -0.5),
        'hy_conv_b': nrm((N_ODD, 3 * HY_WIDTH), 0.02),
        'hy_f_w1': nrm((N_ODD, FILT_EMB, FILT_HIDDEN), FILT_EMB ** -0.5),
        'hy_f_b1': nrm((N_ODD, FILT_HIDDEN), 0.02),
        'hy_f_w2': nrm((N_ODD, FILT_HIDDEN, FILT_HIDDEN), FILT_HIDDEN ** -0.5),
        'hy_f_b2': nrm((N_ODD, FILT_HIDDEN), 0.02),
        'hy_f_w3': nrm((N_ODD, FILT_HIDDEN, 2 * HY_WIDTH), FILT_HIDDEN ** -0.5),
        'hy_f_freq': 1.0 + nrm((N_ODD, FILT_HIDDEN), 0.05),
        'hy_bias': nrm((N_ODD, HY_WIDTH), 1.0),
        'hy_w_out': nrm((N_ODD, HY_WIDTH, D), HY_WIDTH ** -0.5),
        'moe_router_w': nrm((DEPTH, D, N_EXPERTS), D ** -0.5),
        'moe_router_b': nrm((DEPTH, N_EXPERTS), 0.01),
        'moe_w_gate': nrm((DEPTH, N_EXPERTS, D, EXPERT_FF), D ** -0.5),
        'moe_w_up': nrm((DEPTH, N_EXPERTS, D, EXPERT_FF), D ** -0.5),
        'moe_w_down': nrm((DEPTH, N_EXPERTS, EXPERT_FF, D), EXPERT_FF ** -0.5),
        'moe_sh_gate': nrm((DEPTH, D, SHARED_FF), D ** -0.5),
        'moe_sh_up': nrm((DEPTH, D, SHARED_FF), D ** -0.5),
        'moe_sh_down': nrm((DEPTH, SHARED_FF, D), SHARED_FF ** -0.5),
    }


def reference(x, c, ctx, c_ctx, ada_w, ada_b,
              ev_w_in, mla_q_norm, mla_w_uq, mla_kv_norm, mla_w_ukv, mla_q_qknorm, mla_k_qknorm,
              s5_lam_re, s5_lam_im, s5_log_step, s5_b_re, s5_b_im, s5_c_re, s5_c_im, s5_d,
              s5_glu_w, s5_glu_b, ev_w_out,
              hy_w_in, hy_conv_w, hy_conv_b, hy_f_w1, hy_f_b1, hy_f_w2, hy_f_b2, hy_f_w3, hy_f_freq,
              hy_bias, hy_w_out,
              moe_router_w, moe_router_b, moe_w_gate, moe_w_up, moe_w_down,
              moe_sh_gate, moe_sh_up, moe_sh_down):
    B, L, D = x.shape
    rows = L // GRID_W
    cos, sin = axial_rope_tables(rows)
    sc = jax.nn.silu(c)
    sc_ctx = jax.nn.silu(c_ctx)
    for li in range(DEPTH):
        mod = sc @ ada_w[li] + ada_b[li]
        sh1, sc1, g1, sh2, sc2, g2 = jnp.split(mod[:, None, :], 6, axis=-1)
        h = modulate(rmsnorm(x), sh1, sc1)
        if li % 2 == 0:
            e = li // 2
            mod_ctx = sc_ctx @ ada_w[li][:, :2 * D] + ada_b[li][:2 * D]
            h_ctx = modulate(rmsnorm(ctx), mod_ctx[:D], mod_ctx[D:])
            mix = even_mixer(h, h_ctx, cos, sin, ev_w_in[e], mla_q_norm[e], mla_w_uq[e], mla_kv_norm[e],
                             mla_w_ukv[e], mla_q_qknorm[e], mla_k_qknorm[e],
                             s5_lam_re[e], s5_lam_im[e], s5_log_step[e], s5_b_re[e], s5_b_im[e],
                             s5_c_re[e], s5_c_im[e], s5_d[e], s5_glu_w[e], s5_glu_b[e], ev_w_out[e])
        else:
            o = li // 2
            mix = hyena_mixer(h, hy_w_in[o], hy_conv_w[o], hy_conv_b[o], hy_f_w1[o], hy_f_b1[o],
                              hy_f_w2[o], hy_f_b2[o], hy_f_w3[o], hy_f_freq[o], hy_bias[o], hy_w_out[o])
        x = x + g1 * mix
        h = modulate(rmsnorm(x), sh2, sc2)
        x = x + g2 * moe(h, moe_router_w[li], moe_router_b[li], moe_w_gate[li], moe_w_up[li],
                         moe_w_down[li], moe_sh_gate[li], moe_sh_up[li], moe_sh_down[li])
    return x
```

```python
import functools
import math

import numpy as np
import jax
import jax.numpy as jnp
from jax import lax
from jax.experimental import pallas as pl
from jax.experimental.pallas import tpu as pltpu

F32 = jnp.float32
BF16 = jnp.bfloat16

D_MODEL = 1024
BATCH = 8
SEQ = 4096
CTX_LEN = 256
KV_LEN = SEQ + CTX_LEN
GRID_W = 64
EPS = 1e-6

MLA_HEADS = 8
QK_NOPE = 64
QK_ROPE = 32
QK_HEAD = QK_NOPE + QK_ROPE
V_HEAD = 64
Q_LORA = 256
KV_LORA = 128
MLA_WIDTH = MLA_HEADS * V_HEAD
ROPE_BASE = 10000.0
HEAD_PAD = 128

S5_WIDTH = 512
S5_GROUP = 16
S5_GROUPS = S5_WIDTH // S5_GROUP
S5_STATE = 64
S5_CHUNK = 32
S5_NCHUNK = KV_LEN // S5_CHUNK
S5_NCHUNK_LAT = SEQ // S5_CHUNK
S5_NCHUNK_CTX = CTX_LEN // S5_CHUNK

HY_WIDTH = D_MODEL
FILT_EMB = 33
FILT_BANDS = (FILT_EMB - 1) // 2
SHORT_CONV = 3
HY_MIN_DECAY = -math.log(1e-2) / 1.5
HY_MAX_DECAY = -math.log(1e-2) / 0.3
DFT_N = 2 * SEQ
DFT_TF = 256
DFT_NF = SEQ // DFT_TF
HY_CT = 512

N_EXPERTS = 64
TOP_K = 6
EXPERT_FF = 256
ROUTE_SCALE = 2.5
MOE_BLOCK = 256

V7X_VMEM_BYTES = 64 * 1024 * 1024
VMEM_LIMIT = V7X_VMEM_BYTES - 8 * 1024 * 1024


def _params(semantics):
    return pltpu.CompilerParams(dimension_semantics=semantics, vmem_limit_bytes=VMEM_LIMIT)


def _norm_mod(x, shift, scale):
    ms = jnp.mean(x * x, axis=-1, keepdims=True)
    return x * lax.rsqrt(ms + EPS) * (1.0 + scale) + shift


def _rms(x, gain, n):
    ms = jnp.sum(x * x, axis=-1, keepdims=True) * (1.0 / n)
    return x * lax.rsqrt(ms + EPS) * gain


A0_TL = 256
A0_NT = SEQ // A0_TL


def _rope_perm():
    return np.concatenate([np.arange(0, QK_ROPE, 2), np.arange(1, QK_ROPE, 2)])


def _rope_tables():
    t = np.arange(SEQ)
    row = (t // GRID_W).astype(np.float64)
    col = (t % GRID_W).astype(np.float64)
    n_freq = QK_ROPE // 4
    inv = ROPE_BASE ** (-np.arange(n_freq, dtype=np.float64) / n_freq)
    ang = np.concatenate([row[:, None] * inv, col[:, None] * inv], axis=-1)
    cos, sin = np.cos(ang), np.sin(ang)
    half = QK_ROPE // 2
    a = np.zeros((KV_LEN, HEAD_PAD))
    bp = np.zeros((KV_LEN, HEAD_PAD))
    bm = np.zeros((KV_LEN, HEAD_PAD))
    a[:, :QK_HEAD] = 1.0
    a[:SEQ, QK_NOPE:QK_NOPE + half] = cos
    a[:SEQ, QK_NOPE + half:QK_HEAD] = cos
    bp[:SEQ, QK_NOPE + half:QK_HEAD] = sin
    bm[:SEQ, QK_NOPE:QK_NOPE + half] = -sin
    return a, bp, bm


def _a0_kernel(x_ref, ctx_ref, sh_ref, sc_ref, shc_ref, scc_ref, win_ref, qn_ref, wuq_ref, kvn_ref,
               wk_ref, wuv_ref, qg_ref, kg_ref, ka_ref, kp_ref, km_ref, qa_ref, qp_ref, qm_ref,
               q_ref, k_ref, v_ref, u_ref):
    j = pl.program_id(1)
    is_ctx = j == A0_NT
    xin = jnp.where(is_ctx, ctx_ref[0], x_ref[0])
    shift = jnp.where(is_ctx, shc_ref[...], sh_ref[0])
    scale = jnp.where(is_ctx, scc_ref[...], sc_ref[0])
    h = _norm_mod(xin, shift, scale).astype(BF16)
    proj = jnp.dot(h, win_ref[...], preferred_element_type=F32)
    u_ref[0] = proj[:, 512:]

    c_kv = _rms(proj[:, Q_LORA:Q_LORA + KV_LORA], kvn_ref[...], KV_LORA).astype(BF16)
    v_ref[0] = jnp.dot(c_kv, wuv_ref[...], preferred_element_type=F32).astype(BF16)
    kin = jnp.concatenate([c_kv, proj[:, 384:512].astype(BF16)], axis=1)
    kf = jnp.dot(kin, wk_ref[...], preferred_element_type=F32)
    ka, kp, km = ka_ref[...], kp_ref[...], km_ref[...]
    for hd in range(MLA_HEADS):
        sl = slice(hd * HEAD_PAD, (hd + 1) * HEAD_PAD)
        kh = _rms(kf[:, sl], kg_ref[:, sl], QK_HEAD)
        kh = kh * ka + pltpu.roll(kh, 16, 1) * kp + pltpu.roll(kh, HEAD_PAD - 16, 1) * km
        k_ref[0, :, sl] = kh.astype(BF16)

    @pl.when(j < A0_NT)
    def _():
        ql = _rms(proj[:, :Q_LORA], qn_ref[...], Q_LORA).astype(BF16)
        qf = jnp.dot(ql, wuq_ref[...], preferred_element_type=F32)
        qa, qp, qm = qa_ref[...], qp_ref[...], qm_ref[...]
        for hd in range(MLA_HEADS):
            sl = slice(hd * HEAD_PAD, (hd + 1) * HEAD_PAD)
            qh = _rms(qf[:, sl], qg_ref[:, sl], QK_HEAD)
            qh = qh * qa + pltpu.roll(qh, 16, 1) * qp + pltpu.roll(qh, HEAD_PAD - 16, 1) * qm
            q_ref[0, :, sl] = qh.astype(BF16)


def _a0_call(x, ctx, sh, sc, shc, scc, w):
    nt = A0_NT
    lat = lambda b, j: (b, jnp.minimum(j, nt - 1), 0)
    full2 = lambda b, j: (0, 0)
    per_b = lambda b, j: (b, 0, 0)
    tab = pl.BlockSpec((A0_TL, HEAD_PAD), lambda b, j: (j, 0))
    in_specs = [
        pl.BlockSpec((1, A0_TL, D_MODEL), lat),
        pl.BlockSpec((1, CTX_LEN, D_MODEL), per_b),
        pl.BlockSpec((1, 1, D_MODEL), per_b),
        pl.BlockSpec((1, 1, D_MODEL), per_b),
        pl.BlockSpec((1, D_MODEL), full2),
        pl.BlockSpec((1, D_MODEL), full2),
        pl.BlockSpec((D_MODEL, 1024), full2),
        pl.BlockSpec((1, Q_LORA), full2),
        pl.BlockSpec((Q_LORA, MLA_HEADS * HEAD_PAD), full2),
        pl.BlockSpec((1, KV_LORA), full2),
        pl.BlockSpec((2 * KV_LORA, MLA_HEADS * HEAD_PAD), full2),
        pl.BlockSpec((KV_LORA, MLA_WIDTH), full2),
        pl.BlockSpec((1, MLA_HEADS * HEAD_PAD), full2),
        pl.BlockSpec((1, MLA_HEADS * HEAD_PAD), full2),
        tab, tab, tab, tab, tab, tab,
    ]
    out_specs = [
        pl.BlockSpec((1, A0_TL, MLA_HEADS * HEAD_PAD), lat),
        pl.BlockSpec((1, A0_TL, MLA_HEADS * HEAD_PAD), lambda b, j: (b, j, 0)),
        pl.BlockSpec((1, A0_TL, MLA_WIDTH), lambda b, j: (b, j, 0)),
        pl.BlockSpec((1, A0_TL, S5_WIDTH), lambda b, j: (b, j, 0)),
    ]
    out_shape = [
        jax.ShapeDtypeStruct((BATCH, SEQ, MLA_HEADS * HEAD_PAD), BF16),
        jax.ShapeDtypeStruct((BATCH, KV_LEN, MLA_HEADS * HEAD_PAD), BF16),
        jax.ShapeDtypeStruct((BATCH, KV_LEN, MLA_WIDTH), BF16),
        jax.ShapeDtypeStruct((BATCH, KV_LEN, S5_WIDTH), F32),
    ]
    return pl.pallas_call(
        _a0_kernel, grid=(BATCH, nt + 1), in_specs=in_specs, out_specs=out_specs, out_shape=out_shape,
        compiler_params=_params(("parallel", "arbitrary")), name="even_in_proj",
    )(x, ctx, sh, sc, shc, scc, *w)


def _a0_weights(w_in, q_norm, w_uq, kv_norm, w_ukv, q_qk, k_qk):
    perm = _rope_perm()
    kr0 = Q_LORA + KV_LORA
    w_cat = jnp.concatenate([
        w_in[:, :kr0], w_in[:, kr0:kr0 + QK_ROPE][:, perm],
        jnp.zeros((D_MODEL, HEAD_PAD - QK_ROPE), F32), w_in[:, kr0 + QK_ROPE:]], axis=1).astype(BF16)
    pad = HEAD_PAD - QK_HEAD

    def head_gain(g):
        gh = jnp.concatenate([g[:QK_NOPE], g[QK_NOPE:][perm], jnp.zeros((pad,), F32)])
        return jnp.tile(gh, MLA_HEADS)[None, :]

    uq = w_uq.reshape(Q_LORA, MLA_HEADS, QK_HEAD)
    uq = jnp.concatenate([uq[..., :QK_NOPE], uq[..., QK_NOPE:][..., perm],
                          jnp.zeros((Q_LORA, MLA_HEADS, pad), F32)], axis=-1)
    uq = uq.reshape(Q_LORA, MLA_HEADS * HEAD_PAD).astype(BF16)
    ukv = w_ukv.reshape(KV_LORA, MLA_HEADS, QK_NOPE + V_HEAD)
    uk = jnp.concatenate([ukv[..., :QK_NOPE], jnp.zeros((KV_LORA, MLA_HEADS, HEAD_PAD - QK_NOPE), F32)], axis=-1)
    uk = uk.reshape(KV_LORA, MLA_HEADS * HEAD_PAD)
    place = np.zeros((KV_LORA, MLA_HEADS, HEAD_PAD), np.float32)
    for i in range(QK_ROPE):
        place[i, :, QK_NOPE + i] = 1.0
    wk = jnp.concatenate([uk, jnp.asarray(place.reshape(KV_LORA, MLA_HEADS * HEAD_PAD))], axis=0).astype(BF16)
    wuv = ukv[..., QK_NOPE:].reshape(KV_LORA, MLA_WIDTH).astype(BF16)
    a, bp, bm = _rope_tables()
    qs = QK_HEAD ** -0.5
    tabs = [jnp.asarray(t, F32) for t in (a, bp, bm, a * qs, bp * qs, bm * qs)]
    return [w_cat, q_norm[None, :], uq, kv_norm[None, :], wk, wuv, head_gain(q_qk), head_gain(k_qk)] + tabs


ATT_TQ = 256
HEADS_PER_STEP = 2


def _attn_kernel(q_ref, k_ref, v_ref, o_ref):
    v = v_ref[0]
    lane_head = lax.broadcasted_iota(jnp.int32, v.shape, 1) // V_HEAD
    acc = None
    for hh in range(HEADS_PER_STEP):
        sl = slice(hh * HEAD_PAD, (hh + 1) * HEAD_PAD)
        s = lax.dot_general(q_ref[0, :, sl], k_ref[0, :, sl], (((1,), (1,)), ((), ())),
                            preferred_element_type=F32)
        m = jnp.max(s, axis=-1, keepdims=True)
        p = jnp.exp(s - m)
        l = jnp.sum(p, axis=-1, keepdims=True)
        vh = jnp.where(lane_head == hh, v, jnp.zeros_like(v))
        o = jnp.dot(p.astype(BF16), vh, preferred_element_type=F32) * (1.0 / l)
        acc = o if acc is None else acc + o
    o_ref[0] = acc.astype(BF16)


def _attn_call(q, k, v):
    wq = HEADS_PER_STEP * HEAD_PAD
    wv = HEADS_PER_STEP * V_HEAD
    return pl.pallas_call(
        _attn_kernel, grid=(BATCH, MLA_HEADS // HEADS_PER_STEP, SEQ // ATT_TQ),
        in_specs=[pl.BlockSpec((1, ATT_TQ, wq), lambda b, h, i: (b, i, h)),
                  pl.BlockSpec((1, KV_LEN, wq), lambda b, h, i: (b, 0, h)),
                  pl.BlockSpec((1, KV_LEN, wv), lambda b, h, i: (b, 0, h))],
        out_specs=pl.BlockSpec((1, ATT_TQ, wv), lambda b, h, i: (b, i, h)),
        out_shape=jax.ShapeDtypeStruct((BATCH, SEQ, MLA_WIDTH), BF16),
        compiler_params=_params(("parallel", "parallel", "arbitrary")), name="mla_attention",
    )(q, k, v)


S5_ROWS = S5_NCHUNK * BATCH
S5_ROWS_LAT = S5_NCHUNK_LAT * BATCH
S5_COLS = S5_CHUNK * S5_GROUP
S5_SW = 2 * S5_STATE


def _s5_kernel(u_ref, t_ref, mb_ref, mc_ref, coef_ref, y_ref, x_sc, sp_sc):
    u = u_ref[0]
    x_sc[...] = jnp.dot(u, mb_ref[0], preferred_element_type=F32)
    cf = coef_ref[0]
    af, bfm, bfp, ab, bbm, bbp = [cf[i * 8:(i + 1) * 8] for i in range(6)]

    def body(i, carry):
        sf, sfw, sb, sbw = carry
        cfw = jnp.where(i < S5_NCHUNK_CTX, i + S5_NCHUNK_LAT, i - S5_NCHUNK_CTX)
        rf = pl.multiple_of(cfw * BATCH, BATCH)
        rb = pl.multiple_of((S5_NCHUNK - 1 - i) * BATCH, BATCH)
        sp_sc[pl.ds(rf, BATCH), 0:S5_SW] = sf
        sp_sc[pl.ds(rb, BATCH), S5_SW:2 * S5_SW] = sb
        xf = x_sc[pl.ds(rf, BATCH), 0:S5_SW]
        xfw = x_sc[pl.ds(rf, BATCH), S5_SW:2 * S5_SW]
        xb = x_sc[pl.ds(rb, BATCH), 2 * S5_SW:3 * S5_SW]
        xbw = x_sc[pl.ds(rb, BATCH), 3 * S5_SW:4 * S5_SW]
        return (sf * af + sfw * bfm + xf, sfw * af + sf * bfp + xfw,
                sb * ab + sbw * bbm + xb, sbw * ab + sb * bbp + xbw)

    z = jnp.zeros((BATCH, S5_SW), F32)
    lax.fori_loop(0, S5_NCHUNK, body, (z, z, z, z))
    y = jnp.dot(u[:S5_ROWS_LAT], t_ref[0], preferred_element_type=F32)
    y = y + jnp.dot(sp_sc[0:S5_ROWS_LAT, :].astype(BF16), mc_ref[0], preferred_element_type=F32)
    y_ref[0] = y


def _s5_call(ug, t, mb, mc, coef):
    g3 = lambda g: (g, 0, 0)
    return pl.pallas_call(
        _s5_kernel, grid=(S5_GROUPS,),
        in_specs=[pl.BlockSpec((1, S5_ROWS, S5_COLS), g3),
                  pl.BlockSpec((1, S5_COLS, S5_COLS), g3),
                  pl.BlockSpec((1, S5_COLS, 4 * S5_SW), g3),
                  pl.BlockSpec((1, 2 * S5_SW, S5_COLS), g3),
                  pl.BlockSpec((1, 6 * 8, S5_SW), g3)],
        out_specs=pl.BlockSpec((1, S5_ROWS_LAT, S5_COLS), g3),
        out_shape=jax.ShapeDtypeStruct((S5_GROUPS, S5_ROWS_LAT, S5_COLS), F32),
        scratch_shapes=[pltpu.VMEM((S5_ROWS, 4 * S5_SW), F32), pltpu.VMEM((S5_ROWS, 2 * S5_SW), F32)],
        compiler_params=_params(("parallel",)), name="s5_chunked_scan",
    )(ug, t, mb, mc, coef)


def _s5_weights(lam_re, lam_im, log_step, b_re, b_im, c_re, c_im):
    q = S5_CHUNK
    hi = lax.Precision.HIGHEST
    t_blocks, mbs, mcs, coefs = [], [], [], []
    sig = jnp.arange(q)
    for d in range(2):
        lr = jnp.minimum(lam_re[d], -1e-4)
        li = lam_im[d]
        step = jnp.exp(log_step[d])[:, None]
        jj = jnp.arange(q + 1, dtype=F32)[:, None, None]
        mag = jnp.exp(lr * step * jj)
        ph = li * step * jj
        pr, pi = mag * jnp.cos(ph), mag * jnp.sin(ph)
        nr, ni = pr[1] - 1.0, pi[1]
        den = lr * lr + li * li
        fr, fi = (nr * lr + ni * li) / den, (ni * lr - nr * li) / den
        br = fr[..., None] * b_re[d] - fi[..., None] * b_im[d]
        bi = fr[..., None] * b_im[d] + fi[..., None] * b_re[d]
        cr, ci = c_re[d], c_im[d]
        cpr = cr[None] * pr[:, :, None, :] - ci[None] * pi[:, :, None, :]
        cpi = cr[None] * pi[:, :, None, :] + ci[None] * pr[:, :, None, :]
        kern = (jnp.einsum('jghp,gpk->jghk', cpr[:q], br, precision=hi)
                - jnp.einsum('jghp,gpk->jghk', cpi[:q], bi, precision=hi))
        lag = (sig[None, :] - sig[:, None]) if d == 0 else (sig[:, None] - sig[None, :])
        kt = kern.transpose(1, 0, 3, 2)
        tb = jnp.where((lag >= 0)[None, :, :, None, None], kt[:, jnp.clip(lag, 0, q - 1)], 0.0)
        t_blocks.append(tb.transpose(0, 1, 3, 2, 4).reshape(S5_GROUPS, S5_COLS, S5_COLS))
        pw = (q - 1 - sig) if d == 0 else sig
        xr = pr[pw][..., None] * br[None] - pi[pw][..., None] * bi[None]
        xi = pr[pw][..., None] * bi[None] + pi[pw][..., None] * br[None]
        xr = xr.transpose(1, 0, 3, 2).reshape(S5_GROUPS, S5_COLS, S5_STATE)
        xi = xi.transpose(1, 0, 3, 2).reshape(S5_GROUPS, S5_COLS, S5_STATE)
        mbs += [xr, xi, xi, xr]
        po = (sig + 1) if d == 0 else (q - sig)
        mr = cpr[po].transpose(1, 3, 0, 2).reshape(S5_GROUPS, S5_STATE, S5_COLS)
        mi = -cpi[po].transpose(1, 3, 0, 2).reshape(S5_GROUPS, S5_STATE, S5_COLS)
        mcs += [mr, mi]
        are, aim = pr[q], pi[q]
        rows = [jnp.concatenate([are, are], -1), jnp.concatenate([-aim, aim], -1), jnp.concatenate([aim, -aim], -1)]
        coefs += [jnp.broadcast_to(r[:, None, :], (S5_GROUPS, 8, S5_SW)) for r in rows]
    t = (t_blocks[0] + t_blocks[1]).astype(BF16)
    mb = jnp.concatenate(mbs, axis=-1).astype(BF16)
    mc = jnp.concatenate(mcs, axis=1).astype(BF16)
    coef = jnp.concatenate(coefs, axis=1)
    return t, mb, mc, coef


A1_TL = 512


def _a1_kernel(att_ref, ys_ref, u_ref, d_ref, gw_ref, gb_ref, woa_ref, wos_ref, x_ref, g_ref, o_ref):
    y = u_ref[0] * d_ref[...] + ys_ref[0]
    z = jax.nn.gelu(y)
    gate = jax.nn.sigmoid(jnp.dot(z.astype(BF16), gw_ref[...], preferred_element_type=F32) + gb_ref[...])
    s5 = (z * gate).astype(BF16)
    mix = jnp.dot(att_ref[0], woa_ref[...], preferred_element_type=F32)
    mix = mix + jnp.dot(s5, wos_ref[...], preferred_element_type=F32)
    o_ref[0] = x_ref[0] + g_ref[0] * mix


def _a1_call(att, ys, u, d, gw, gb, woa, wos, x, g1):
    t3 = lambda b, j: (b, j, 0)
    full2 = lambda b, j: (0, 0)
    return pl.pallas_call(
        _a1_kernel, grid=(BATCH, SEQ // A1_TL),
        in_specs=[pl.BlockSpec((1, A1_TL, MLA_WIDTH), t3),
                  pl.BlockSpec((1, A1_TL, S5_WIDTH), t3),
                  pl.BlockSpec((1, A1_TL, S5_WIDTH), t3),
                  pl.BlockSpec((1, S5_WIDTH), full2),
                  pl.BlockSpec((S5_WIDTH, S5_WIDTH), full2),
                  pl.BlockSpec((1, S5_WIDTH), full2),
                  pl.BlockSpec((MLA_WIDTH, D_MODEL), full2),
                  pl.BlockSpec((S5_WIDTH, D_MODEL), full2),
                  pl.BlockSpec((1, A1_TL, D_MODEL), t3),
                  pl.BlockSpec((1, 1, D_MODEL), lambda b, j: (b, 0, 0))],
        out_specs=pl.BlockSpec((1, A1_TL, D_MODEL), t3),
        out_shape=jax.ShapeDtypeStruct((BATCH, SEQ, D_MODEL), F32),
        compiler_params=_params(("parallel", "arbitrary")), name="even_out_proj",
    )(att, ys, u, d, gw, gb, woa, wos, x, g1)


MOE_TL = 512


def _moe_in_kernel(x_ref, sh_ref, sc_ref, rw_ref, sg_ref, su_ref, sd_ref, h_ref, s_ref, o_ref):
    h = _norm_mod(x_ref[0], sh_ref[0], sc_ref[0])
    logits = jnp.dot(h, rw_ref[...], preferred_element_type=F32, precision=lax.Precision.HIGHEST)
    s_ref[0] = jax.nn.sigmoid(logits)
    hb = h.astype(BF16)
    h_ref[0] = hb
    hid = jax.nn.silu(jnp.dot(hb, sg_ref[...], preferred_element_type=F32))
    hid = hid * jnp.dot(hb, su_ref[...], preferred_element_type=F32)
    o_ref[0] = jnp.dot(hid.astype(BF16), sd_ref[...], preferred_element_type=F32)


def _moe_in_call(x, sh, sc, rw, sg, su, sd):
    t3 = lambda b, j: (b, j, 0)
    full2 = lambda b, j: (0, 0)
    per_b = lambda b, j: (b, 0, 0)
    ff = sg.shape[1]
    return pl.pallas_call(
        _moe_in_kernel, grid=(BATCH, SEQ // MOE_TL),
        in_specs=[pl.BlockSpec((1, MOE_TL, D_MODEL), t3),
                  pl.BlockSpec((1, 1, D_MODEL), per_b),
                  pl.BlockSpec((1, 1, D_MODEL), per_b),
                  pl.BlockSpec((D_MODEL, N_EXPERTS), full2),
                  pl.BlockSpec((D_MODEL, ff), full2),
                  pl.BlockSpec((D_MODEL, ff), full2),
                  pl.BlockSpec((ff, D_MODEL), full2)],
        out_specs=[pl.BlockSpec((1, MOE_TL, D_MODEL), t3),
                   pl.BlockSpec((1, MOE_TL, N_EXPERTS), t3),
                   pl.BlockSpec((1, MOE_TL, D_MODEL), t3)],
        out_shape=[jax.ShapeDtypeStruct((BATCH, SEQ, D_MODEL), BF16),
                   jax.ShapeDtypeStruct((BATCH, SEQ, N_EXPERTS), F32),
                   jax.ShapeDtypeStruct((BATCH, SEQ, D_MODEL), F32)],
        compiler_params=_params(("parallel", "arbitrary")), name="moe_router_shared",
    )(x, sh, sc, rw, sg, su, sd)


def _expert_kernel(be_ref, x_ref, wg_ref, wu_ref, wd_ref, rw_ref, o_ref):
    del be_ref
    xb = x_ref[...]
    hid = jax.nn.silu(jnp.dot(xb, wg_ref[0], preferred_element_type=F32))
    hid = hid * jnp.dot(xb, wu_ref[0], preferred_element_type=F32)
    y = jnp.dot(hid.astype(BF16), wd_ref[0], preferred_element_type=F32)
    o_ref[...] = (y * rw_ref[...]).astype(BF16)


def _expert_call(block_e, xb, wg, wu, wd, row_w):
    n_rows = xb.shape[0]
    n_blocks = n_rows // MOE_BLOCK
    grid_spec = pltpu.PrefetchScalarGridSpec(
        num_scalar_prefetch=1, grid=(n_blocks,),
        in_specs=[pl.BlockSpec((MOE_BLOCK, D_MODEL), lambda i, be: (i, 0)),
                  pl.BlockSpec((1, D_MODEL, EXPERT_FF), lambda i, be: (be[i], 0, 0)),
                  pl.BlockSpec((1, D_MODEL, EXPERT_FF), lambda i, be: (be[i], 0, 0)),
                  pl.BlockSpec((1, EXPERT_FF, D_MODEL), lambda i, be: (be[i], 0, 0)),
                  pl.BlockSpec((MOE_BLOCK, 1), lambda i, be: (i, 0))],
        out_specs=pl.BlockSpec((MOE_BLOCK, D_MODEL), lambda i, be: (i, 0)))
    return pl.pallas_call(
        _expert_kernel, grid_spec=grid_spec,
        out_shape=jax.ShapeDtypeStruct((n_rows, D_MODEL), BF16),
        compiler_params=_params(("arbitrary",)), name="moe_experts",
    )(block_e, xb, wg, wu, wd, row_w)


def _moe(x, sh, sc, g2, router_w, router_b, w_gate, w_up, w_down, sh_gate, sh_up, sh_down):
    T = BATCH * SEQ
    TK = T * TOP_K
    hb, scores, shared = _moe_in_call(x, sh, sc, router_w, sh_gate.astype(BF16), sh_up.astype(BF16),
                                      sh_down.astype(BF16))
    hb = hb.reshape(T, D_MODEL)
    scores = scores.reshape(T, N_EXPERTS)
    _, idx = lax.top_k(scores + router_b.astype(F32), TOP_K)
    sel = jnp.take_along_axis(scores, idx, axis=-1)
    wts = sel / jnp.sum(sel, axis=-1, keepdims=True) * ROUTE_SCALE
    flat_e = idx.reshape(TK)
    order = jnp.argsort(flat_e)
    sorted_e = flat_e[order]
    counts = jnp.bincount(flat_e, length=N_EXPERTS)
    padded = (counts + MOE_BLOCK - 1) // MOE_BLOCK * MOE_BLOCK
    pad_end = jnp.cumsum(padded)
    pad_start = pad_end - padded
    start = jnp.cumsum(counts) - counts
    dest = pad_start[sorted_e] + jnp.arange(TK, dtype=jnp.int32) - start[sorted_e]
    n_blocks = -(-TK // MOE_BLOCK) + N_EXPERTS
    n_rows = n_blocks * MOE_BLOCK
    row_tok = jnp.zeros((n_rows,), jnp.int32).at[dest].set((order // TOP_K).astype(jnp.int32))
    row_w = jnp.zeros((n_rows,), F32).at[dest].set(wts.reshape(TK)[order])
    block_start = jnp.arange(n_blocks, dtype=jnp.int32) * MOE_BLOCK
    block_e = jnp.minimum(jnp.searchsorted(pad_end, block_start, side='right'), N_EXPERTS - 1).astype(jnp.int32)
    xb = hb[row_tok]
    ys = _expert_call(block_e, xb, w_gate.astype(BF16), w_up.astype(BF16), w_down.astype(BF16), row_w[:, None])
    pos = jnp.zeros((TK,), jnp.int32).at[order].set(dest.astype(jnp.int32)).reshape(T, TOP_K)
    routed = jnp.sum(ys[pos].astype(F32), axis=1)
    return x + g2 * (routed.reshape(x.shape) + shared)


HY_TL = 512
HALO = 8


def _hy_in_kernel(x_ref, xp_ref, xn_ref, sh_ref, sc_ref, w_ref, cw_ref, cb_ref, z_ref, x0_ref, h_sc):
    j = pl.program_id(1)
    shift, scale = sh_ref[0], sc_ref[0]
    keep_prev = (j > 0).astype(F32)
    keep_next = (j < SEQ // HY_TL - 1).astype(F32)
    h_sc[0:HALO, :] = _norm_mod(xp_ref[0], shift, scale) * keep_prev
    h_sc[HALO:HALO + HY_TL, :] = _norm_mod(x_ref[0], shift, scale)
    h_sc[HALO + HY_TL:, :] = _norm_mod(xn_ref[0], shift, scale) * keep_next
    hcat = h_sc[...].astype(BF16)
    outs = []
    for part in range(3):
        sl = slice(part * HY_WIDTH, (part + 1) * HY_WIDTH)
        p = jnp.dot(hcat, w_ref[:, sl], preferred_element_type=F32)
        o = (p[HALO - 1:HALO - 1 + HY_TL] * cw_ref[0:1, sl] + p[HALO:HALO + HY_TL] * cw_ref[1:2, sl]
             + p[HALO + 1:HALO + 1 + HY_TL] * cw_ref[2:3, sl] + cb_ref[:, sl])
        outs.append(o)
    x0_ref[0] = outs[0].astype(BF16)
    z_ref[0] = (outs[2] * outs[1]).astype(BF16)


def _hy_in_call(x, sh, sc, w, cw, cb):
    nb8 = HY_TL // HALO
    t3 = lambda b, j: (b, j, 0)
    full2 = lambda b, j: (0, 0)
    per_b = lambda b, j: (b, 0, 0)
    return pl.pallas_call(
        _hy_in_kernel, grid=(BATCH, SEQ // HY_TL),
        in_specs=[pl.BlockSpec((1, HY_TL, D_MODEL), t3),
                  pl.BlockSpec((1, HALO, D_MODEL), lambda b, j: (b, jnp.maximum(j * nb8 - 1, 0), 0)),
                  pl.BlockSpec((1, HALO, D_MODEL), lambda b, j: (b, jnp.minimum((j + 1) * nb8, SEQ // HALO - 1), 0)),
                  pl.BlockSpec((1, 1, D_MODEL), per_b),
                  pl.BlockSpec((1, 1, D_MODEL), per_b),
                  pl.BlockSpec((D_MODEL, 3 * HY_WIDTH), full2),
                  pl.BlockSpec((SHORT_CONV, 3 * HY_WIDTH), full2),
                  pl.BlockSpec((1, 3 * HY_WIDTH), full2)],
        out_specs=[pl.BlockSpec((1, HY_TL, HY_WIDTH), t3), pl.BlockSpec((1, HY_TL, HY_WIDTH), t3)],
        out_shape=[jax.ShapeDtypeStruct((BATCH, SEQ, HY_WIDTH), BF16),
                   jax.ShapeDtypeStruct((BATCH, SEQ, HY_WIDTH), BF16)],
        scratch_shapes=[pltpu.VMEM((HY_TL + 2 * HALO, D_MODEL), F32)],
        compiler_params=_params(("parallel", "arbitrary")), name="hyena_in_proj",
    )(x, x, x, sh, sc, w, cw, cb)


def _dft_matrices():
    n = DFT_N
    k = np.arange(SEQ, dtype=np.int64)[:, None]
    a = np.arange(64, dtype=np.int64)[None, :]
    ang_a = 2.0 * np.pi * ((k * a * 64) % n) / n
    ang_c = 2.0 * np.pi * ((k * a) % n) / n
    ca, sa, cc, sc = [jnp.asarray(v, F32) for v in (np.cos(ang_a), np.sin(ang_a), np.cos(ang_c), np.sin(ang_c))]
    f_re = (ca[:, :, None] * cc[:, None, :] - sa[:, :, None] * sc[:, None, :]).reshape(SEQ, SEQ)
    f_im = -(sa[:, :, None] * cc[:, None, :] + ca[:, :, None] * sc[:, None, :]).reshape(SEQ, SEQ)
    alt = jnp.asarray(1.0 - 2.0 * (np.arange(SEQ) % 2), F32)
    f_im = f_im.at[0, :].set(alt)
    fwd = jnp.concatenate([f_re.reshape(DFT_NF, DFT_TF, SEQ), f_im.reshape(DFT_NF, DFT_TF, SEQ)], axis=1)
    g_re = f_re * (2.0 / n)
    g_re = g_re.at[:, 0].set(1.0 / n)
    g_im = f_im.T * (2.0 / n)
    g_im = g_im.at[:, 0].set(alt / n)
    inv = jnp.concatenate([g_re.reshape(SEQ, DFT_NF, DFT_TF), g_im.reshape(SEQ, DFT_NF, DFT_TF)], axis=2)
    inv = inv.transpose(1, 0, 2)
    return fwd.astype(BF16), inv.astype(BF16)


def _spec_kernel(f_ref, h_ref, p_ref, pn_ref, q_ref):
    spec = jnp.dot(f_ref[0], h_ref[...], preferred_element_type=F32)
    re, im = spec[:DFT_TF], spec[DFT_TF:]
    c_re = re[:, :HY_WIDTH] + re[:, HY_WIDTH:]
    c_im = im[:, :HY_WIDTH] - im[:, HY_WIDTH:]
    nyq = im[:, :HY_WIDTH] + im[:, HY_WIDTH:]
    row = lax.broadcasted_iota(jnp.int32, c_re.shape, 0) + pl.program_id(0) * DFT_TF
    first = row == 0
    p_ref[...] = c_re
    pn_ref[...] = jnp.where(first, nyq, c_re)
    q_ref[...] = jnp.where(first, 0.0, c_im)


def _spec_call(fwd, hfb):
    out = jax.ShapeDtypeStruct((SEQ, HY_WIDTH), F32)
    ospec = pl.BlockSpec((DFT_TF, HY_WIDTH), lambda f: (f, 0))
    return pl.pallas_call(
        _spec_kernel, grid=(DFT_NF,),
        in_specs=[pl.BlockSpec((1, 2 * DFT_TF, SEQ), lambda f: (f, 0, 0)),
                  pl.BlockSpec((SEQ, 2 * HY_WIDTH), lambda f: (0, 0))],
        out_specs=[ospec, ospec, ospec], out_shape=[out, out, out],
        compiler_params=_params(("arbitrary",)), name="hyena_filter_spectrum",
    )(fwd, hfb)


def _conv_kernel(z_ref, f_ref, g_ref, p_ref, pn_ref, q_ref, y_ref, acc):
    f = pl.program_id(2)
    spec = jnp.dot(f_ref[0], z_ref[0], preferred_element_type=F32)
    zr, zi = spec[:DFT_TF], spec[DFT_TF:]
    yr = zr * p_ref[...] - zi * q_ref[...]
    yi = zr * q_ref[...] + zi * pn_ref[...]
    ycat = jnp.concatenate([yr, yi], axis=0).astype(BF16)
    part = jnp.dot(g_ref[0], ycat, preferred_element_type=F32)

    @pl.when(f == 0)
    def _():
        acc[...] = part

    @pl.when(f > 0)
    def _():
        acc[...] += part

    @pl.when(f == DFT_NF - 1)
    def _():
        y_ref[0] = acc[...].astype(BF16)


def _conv_call(z, fwd, inv, p, pn, q):
    sspec = pl.BlockSpec((DFT_TF, HY_CT), lambda b, c, f: (f, c))
    return pl.pallas_call(
        _conv_kernel, grid=(BATCH, HY_WIDTH // HY_CT, DFT_NF),
        in_specs=[pl.BlockSpec((1, SEQ, HY_CT), lambda b, c, f: (b, 0, c)),
                  pl.BlockSpec((1, 2 * DFT_TF, SEQ), lambda b, c, f: (f, 0, 0)),
                  pl.BlockSpec((1, SEQ, 2 * DFT_TF), lambda b, c, f: (f, 0, 0)),
                  sspec, sspec, sspec],
        out_specs=pl.BlockSpec((1, SEQ, HY_CT), lambda b, c, f: (b, 0, c)),
        out_shape=jax.ShapeDtypeStruct((BATCH, SEQ, HY_WIDTH), BF16),
        scratch_shapes=[pltpu.VMEM((SEQ, HY_CT), F32)],
        compiler_params=_params(("parallel", "parallel", "arbitrary")), name="hyena_long_conv",
    )(z, fwd, inv, p, pn, q)


def _hy_out_kernel(y_ref, z_ref, x0_ref, b_ref, w_ref, x_ref, g_ref, o_ref):
    z = z_ref[0].astype(F32)
    gated = x0_ref[0].astype(F32) * (y_ref[0].astype(F32) + b_ref[...] * z)
    mix = jnp.dot(gated.astype(BF16), w_ref[...], preferred_element_type=F32)
    o_ref[0] = x_ref[0] + g_ref[0] * mix


def _hy_out_call(y, z, x0, bias, w, x, g1):
    t3 = lambda b, j: (b, j, 0)
    full2 = lambda b, j: (0, 0)
    return pl.pallas_call(
        _hy_out_kernel, grid=(BATCH, SEQ // HY_TL),
        in_specs=[pl.BlockSpec((1, HY_TL, HY_WIDTH), t3),
                  pl.BlockSpec((1, HY_TL, HY_WIDTH), t3),
                  pl.BlockSpec((1, HY_TL, HY_WIDTH), t3),
                  pl.BlockSpec((1, HY_WIDTH), full2),
                  pl.BlockSpec((HY_WIDTH, D_MODEL), full2),
                  pl.BlockSpec((1, HY_TL, D_MODEL), t3),
                  pl.BlockSpec((1, 1, D_MODEL), lambda b, j: (b, 0, 0))],
        out_specs=pl.BlockSpec((1, HY_TL, D_MODEL), t3),
        out_shape=jax.ShapeDtypeStruct((BATCH, SEQ, D_MODEL), F32),
        compiler_params=_params(("parallel", "arbitrary")), name="hyena_out_proj",
    )(y, z, x0, bias, w, x, g1)


def _hyena_filter(w1, b1, w2, b2, w3, freq):
    hi = lax.Precision.HIGHEST
    Lq = SEQ
    t = jnp.linspace(0.0, 1.0, Lq, dtype=F32)[:, None]
    ang = 2.0 * math.pi * jnp.arange(Lq, dtype=F32)[:, None] / Lq
    bands = jnp.linspace(1e-4, FILT_BANDS - 1, FILT_BANDS, dtype=F32)
    z = jnp.concatenate([t, jnp.cos(bands * ang), -jnp.sin(bands * ang)], axis=-1)
    hid = jnp.sin(freq * (jnp.dot(z, w1, precision=hi) + b1))
    hid = jnp.sin(freq * (jnp.dot(hid, w2, precision=hi) + b2))
    hf = jnp.dot(hid, w3, precision=hi).reshape(Lq, 2, HY_WIDTH)
    deltas = jnp.linspace(HY_MIN_DECAY, HY_MAX_DECAY, HY_WIDTH, dtype=F32)
    hf = hf * jnp.exp(-t * deltas)[:, None, :]
    hf = hf * lax.rsqrt(jnp.sum(hf * hf, axis=(0, 1), keepdims=True) + EPS)
    return hf.reshape(Lq, 2 * HY_WIDTH)


def kernel(x, c, ctx, c_ctx, ada_w, ada_b, ev_w_in, mla_q_norm, mla_w_uq, mla_kv_norm, mla_w_ukv, mla_q_qknorm, mla_k_qknorm, s5_lam_re, s5_lam_im, s5_log_step, s5_b_re, s5_b_im, s5_c_re, s5_c_im, s5_d, s5_glu_w, s5_glu_b, ev_w_out, hy_w_in, hy_conv_w, hy_conv_b, hy_f_w1, hy_f_b1, hy_f_w2, hy_f_b2, hy_f_w3, hy_f_freq, hy_bias, hy_w_out, moe_router_w, moe_router_b, moe_w_gate, moe_w_up, moe_w_down, moe_sh_gate, moe_sh_up, moe_sh_down):
    hi = lax.Precision.HIGHEST
    D = D_MODEL
    sc = jax.nn.silu(c)
    sc_ctx = jax.nn.silu(c_ctx)

    def mods(li):
        mod = jnp.dot(sc, ada_w[li], precision=hi) + ada_b[li]
        return [m[:, None, :] for m in jnp.split(mod, 6, axis=-1)]

    sh1, sc1, g1, sh2, sc2, g2 = mods(0)
    mod_ctx = jnp.dot(sc_ctx, ada_w[0][:, :2 * D], precision=hi) + ada_b[0][:2 * D]
    w0 = _a0_weights(ev_w_in[0], mla_q_norm[0], mla_w_uq[0], mla_kv_norm[0], mla_w_ukv[0],
                     mla_q_qknorm[0], mla_k_qknorm[0])
    q, k, v, u = _a0_call(x, ctx, sh1, sc1, mod_ctx[None, :D], mod_ctx[None, D:], w0)
    att = _attn_call(q, k, v)
    ug = u.astype(BF16).reshape(BATCH, S5_NCHUNK, S5_CHUNK, S5_GROUPS, S5_GROUP)
    ug = ug.transpose(3, 1, 0, 2, 4).reshape(S5_GROUPS, S5_ROWS, S5_COLS)
    ys = _s5_call(ug, *_s5_weights(s5_lam_re[0], s5_lam_im[0], s5_log_step[0], s5_b_re[0], s5_b_im[0],
                                   s5_c_re[0], s5_c_im[0]))
    ys = ys.reshape(S5_GROUPS, S5_NCHUNK_LAT, BATCH, S5_CHUNK, S5_GROUP)
    ys = ys.transpose(2, 1, 3, 0, 4).reshape(BATCH, SEQ, S5_WIDTH)
    wo = ev_w_out[0].astype(BF16)
    x = _a1_call(att, ys, u, s5_d[0][None, :], s5_glu_w[0].astype(BF16), s5_glu_b[0][None, :],
                 wo[:MLA_WIDTH], wo[MLA_WIDTH:], x, g1)
    x = _moe(x, sh2, sc2, g2, moe_router_w[0], moe_router_b[0], moe_w_gate[0], moe_w_up[0], moe_w_down[0],
             moe_sh_gate[0], moe_sh_up[0], moe_sh_down[0])

    sh1, sc1, g1, sh2, sc2, g2 = mods(1)
    z, x0 = _hy_in_call(x, sh1, sc1, hy_w_in[0].astype(BF16), hy_conv_w[0], hy_conv_b[0][None, :])
    fwd, inv = _dft_matrices()
    hfb = _hyena_filter(hy_f_w1[0], hy_f_b1[0], hy_f_w2[0], hy_f_b2[0], hy_f_w3[0], hy_f_freq[0])
    p, pn, qq = _spec_call(fwd, hfb.astype(BF16))
    y = _conv_call(z, fwd, inv, p, pn, qq)
    x = _hy_out_call(y, z, x0, hy_bias[0][None, :], hy_w_out[0].astype(BF16), x, g1)
    x = _moe(x, sh2, sc2, g2, moe_router_w[1], moe_router_b[1], moe_w_gate[1], moe_w_up[1], moe_w_down[1],
             moe_sh_gate[1], moe_sh_up[1], moe_sh_down[1])
    return x
```

```python
import functools
import math

import numpy as np
import jax
import jax.numpy as jnp
from jax import lax
from jax.experimental import pallas as pl
from jax.experimental.pallas import tpu as pltpu

F32 = jnp.float32
BF16 = jnp.bfloat16

D_MODEL = 1024
BATCH = 8
SEQ = 4096
CTX_LEN = 256
KV_LEN = SEQ + CTX_LEN
GRID_W = 64
EPS = 1e-6

MLA_HEADS = 8
QK_NOPE = 64
QK_ROPE = 32
QK_HEAD = QK_NOPE + QK_ROPE
V_HEAD = 64
Q_LORA = 256
KV_LORA = 128
MLA_WIDTH = MLA_HEADS * V_HEAD
ROPE_BASE = 10000.0
HEAD_PAD = 128

S5_WIDTH = 512
S5_GROUP = 16
S5_GROUPS = S5_WIDTH // S5_GROUP
S5_STATE = 64
S5_CHUNK = 32
S5_NCHUNK = KV_LEN // S5_CHUNK
S5_NCHUNK_LAT = SEQ // S5_CHUNK
S5_NCHUNK_CTX = CTX_LEN // S5_CHUNK

HY_WIDTH = D_MODEL
FILT_EMB = 33
FILT_BANDS = (FILT_EMB - 1) // 2
SHORT_CONV = 3
HY_MIN_DECAY = -math.log(1e-2) / 1.5
HY_MAX_DECAY = -math.log(1e-2) / 0.3
DFT_N = 2 * SEQ
DFT_TF = 256
DFT_NF = SEQ // DFT_TF
HY_CT = 512

N_EXPERTS = 64
TOP_K = 6
EXPERT_FF = 256
ROUTE_SCALE = 2.5
MOE_BLOCK = 512

V7X_VMEM_BYTES = 64 * 1024 * 1024
VMEM_LIMIT = V7X_VMEM_BYTES - 8 * 1024 * 1024


def _params(semantics):
    return pltpu.CompilerParams(dimension_semantics=semantics, vmem_limit_bytes=VMEM_LIMIT)


def _norm_mod(x, shift, scale):
    ms = jnp.mean(x * x, axis=-1, keepdims=True)
    return x * lax.rsqrt(ms + EPS) * (1.0 + scale) + shift


def _rms(x, gain, n):
    ms = jnp.sum(x * x, axis=-1, keepdims=True) * (1.0 / n)
    return x * lax.rsqrt(ms + EPS) * gain


A0_TL = 256
A0_NT = SEQ // A0_TL


def _rope_perm():
    return np.concatenate([np.arange(0, QK_ROPE, 2), np.arange(1, QK_ROPE, 2)])


def _rope_tables():
    t = np.arange(SEQ)
    row = (t // GRID_W).astype(np.float64)
    col = (t % GRID_W).astype(np.float64)
    n_freq = QK_ROPE // 4
    inv = ROPE_BASE ** (-np.arange(n_freq, dtype=np.float64) / n_freq)
    ang = np.concatenate([row[:, None] * inv, col[:, None] * inv], axis=-1)
    cos, sin = np.cos(ang), np.sin(ang)
    half = QK_ROPE // 2
    a = np.zeros((KV_LEN, HEAD_PAD))
    bp = np.zeros((KV_LEN, HEAD_PAD))
    bm = np.zeros((KV_LEN, HEAD_PAD))
    a[:, :QK_HEAD] = 1.0
    a[:SEQ, QK_NOPE:QK_NOPE + half] = cos
    a[:SEQ, QK_NOPE + half:QK_HEAD] = cos
    bp[:SEQ, QK_NOPE + half:QK_HEAD] = sin
    bm[:SEQ, QK_NOPE:QK_NOPE + half] = -sin
    return a, bp, bm


def _a0_kernel(x_ref, ctx_ref, sh_ref, sc_ref, shc_ref, scc_ref, win_ref, qn_ref, wuq_ref, kvn_ref,
               wk_ref, wuv_ref, qg_ref, kg_ref, ka_ref, kp_ref, km_ref, qa_ref, qp_ref, qm_ref,
               q_ref, k_ref, v_ref, u_ref):
    j = pl.program_id(1)
    is_ctx = j == A0_NT
    xin = jnp.where(is_ctx, ctx_ref[0], x_ref[0])
    shift = jnp.where(is_ctx, shc_ref[...], sh_ref[0])
    scale = jnp.where(is_ctx, scc_ref[...], sc_ref[0])
    h = _norm_mod(xin, shift, scale).astype(BF16)
    proj = jnp.dot(h, win_ref[...], preferred_element_type=F32)
    u_ref[0] = proj[:, 512:]

    c_kv = _rms(proj[:, Q_LORA:Q_LORA + KV_LORA], kvn_ref[...], KV_LORA).astype(BF16)
    v_ref[0] = jnp.dot(c_kv, wuv_ref[...], preferred_element_type=F32).astype(BF16)
    kin = jnp.concatenate([c_kv, proj[:, 384:512].astype(BF16)], axis=1)
    kf = jnp.dot(kin, wk_ref[...], preferred_element_type=F32)
    ka, kp, km = ka_ref[...], kp_ref[...], km_ref[...]
    for hd in range(MLA_HEADS):
        sl = slice(hd * HEAD_PAD, (hd + 1) * HEAD_PAD)
        kh = _rms(kf[:, sl], kg_ref[:, sl], QK_HEAD)
        kh = kh * ka + pltpu.roll(kh, 16, 1) * kp + pltpu.roll(kh, HEAD_PAD - 16, 1) * km
        k_ref[0, :, sl] = kh.astype(BF16)

    @pl.when(j < A0_NT)
    def _():
        ql = _rms(proj[:, :Q_LORA], qn_ref[...], Q_LORA).astype(BF16)
        qf = jnp.dot(ql, wuq_ref[...], preferred_element_type=F32)
        qa, qp, qm = qa_ref[...], qp_ref[...], qm_ref[...]
        for hd in range(MLA_HEADS):
            sl = slice(hd * HEAD_PAD, (hd + 1) * HEAD_PAD)
            qh = _rms(qf[:, sl], qg_ref[:, sl], QK_HEAD)
            qh = qh * qa + pltpu.roll(qh, 16, 1) * qp + pltpu.roll(qh, HEAD_PAD - 16, 1) * qm
            q_ref[0, :, sl] = qh.astype(BF16)


def _a0_call(x, ctx, sh, sc, shc, scc, w):
    nt = A0_NT
    lat = lambda b, j: (b, jnp.minimum(j, nt - 1), 0)
    full2 = lambda b, j: (0, 0)
    per_b = lambda b, j: (b, 0, 0)
    tab = pl.BlockSpec((A0_TL, HEAD_PAD), lambda b, j: (j, 0))
    in_specs = [
        pl.BlockSpec((1, A0_TL, D_MODEL), lat),
        pl.BlockSpec((1, CTX_LEN, D_MODEL), per_b),
        pl.BlockSpec((1, 1, D_MODEL), per_b),
        pl.BlockSpec((1, 1, D_MODEL), per_b),
        pl.BlockSpec((1, D_MODEL), full2),
        pl.BlockSpec((1, D_MODEL), full2),
        pl.BlockSpec((D_MODEL, 1024), full2),
        pl.BlockSpec((1, Q_LORA), full2),
        pl.BlockSpec((Q_LORA, MLA_HEADS * HEAD_PAD), full2),
        pl.BlockSpec((1, KV_LORA), full2),
        pl.BlockSpec((2 * KV_LORA, MLA_HEADS * HEAD_PAD), full2),
        pl.BlockSpec((KV_LORA, MLA_WIDTH), full2),
        pl.BlockSpec((1, MLA_HEADS * HEAD_PAD), full2),
        pl.BlockSpec((1, MLA_HEADS * HEAD_PAD), full2),
        tab, tab, tab, tab, tab, tab,
    ]
    out_specs = [
        pl.BlockSpec((1, A0_TL, MLA_HEADS * HEAD_PAD), lat),
        pl.BlockSpec((1, A0_TL, MLA_HEADS * HEAD_PAD), lambda b, j: (b, j, 0)),
        pl.BlockSpec((1, A0_TL, MLA_WIDTH), lambda b, j: (b, j, 0)),
        pl.BlockSpec((1, A0_TL, S5_WIDTH), lambda b, j: (b, j, 0)),
    ]
    out_shape = [
        jax.ShapeDtypeStruct((BATCH, SEQ, MLA_HEADS * HEAD_PAD), BF16),
        jax.ShapeDtypeStruct((BATCH, KV_LEN, MLA_HEADS * HEAD_PAD), BF16),
        jax.ShapeDtypeStruct((BATCH, KV_LEN, MLA_WIDTH), BF16),
        jax.ShapeDtypeStruct((BATCH, KV_LEN, S5_WIDTH), F32),
    ]
    return pl.pallas_call(
        _a0_kernel, grid=(BATCH, nt + 1), in_specs=in_specs, out_specs=out_specs, out_shape=out_shape,
        compiler_params=_params(("parallel", "arbitrary")), name="even_in_proj",
    )(x, ctx, sh, sc, shc, scc, *w)


def _a0_weights(w_in, q_norm, w_uq, kv_norm, w_ukv, q_qk, k_qk):
    perm = _rope_perm()
    kr0 = Q_LORA + KV_LORA
    w_cat = jnp.concatenate([
        w_in[:, :kr0], w_in[:, kr0:kr0 + QK_ROPE][:, perm],
        jnp.zeros((D_MODEL, HEAD_PAD - QK_ROPE), F32), w_in[:, kr0 + QK_ROPE:]], axis=1).astype(BF16)
    pad = HEAD_PAD - QK_HEAD

    def head_gain(g):
        gh = jnp.concatenate([g[:QK_NOPE], g[QK_NOPE:][perm], jnp.zeros((pad,), F32)])
        return jnp.tile(gh, MLA_HEADS)[None, :]

    uq = w_uq.reshape(Q_LORA, MLA_HEADS, QK_HEAD)
    uq = jnp.concatenate([uq[..., :QK_NOPE], uq[..., QK_NOPE:][..., perm],
                          jnp.zeros((Q_LORA, MLA_HEADS, pad), F32)], axis=-1)
    uq = uq.reshape(Q_LORA, MLA_HEADS * HEAD_PAD).astype(BF16)
    ukv = w_ukv.reshape(KV_LORA, MLA_HEADS, QK_NOPE + V_HEAD)
    uk = jnp.concatenate([ukv[..., :QK_NOPE], jnp.zeros((KV_LORA, MLA_HEADS, HEAD_PAD - QK_NOPE), F32)], axis=-1)
    uk = uk.reshape(KV_LORA, MLA_HEADS * HEAD_PAD)
    place = np.zeros((KV_LORA, MLA_HEADS, HEAD_PAD), np.float32)
    for i in range(QK_ROPE):
        place[i, :, QK_NOPE + i] = 1.0
    wk = jnp.concatenate([uk, jnp.asarray(place.reshape(KV_LORA, MLA_HEADS * HEAD_PAD))], axis=0).astype(BF16)
    wuv = ukv[..., QK_NOPE:].reshape(KV_LORA, MLA_WIDTH).astype(BF16)
    a, bp, bm = _rope_tables()
    qs = QK_HEAD ** -0.5
    tabs = [jnp.asarray(t, F32) for t in (a, bp, bm, a * qs, bp * qs, bm * qs)]
    return [w_cat, q_norm[None, :], uq, kv_norm[None, :], wk, wuv, head_gain(q_qk), head_gain(k_qk)] + tabs


ATT_TQ = 256
HEADS_PER_STEP = 2


def _attn_kernel(q_ref, k_ref, v_ref, o_ref):
    v = v_ref[0]
    lane_head = lax.broadcasted_iota(jnp.int32, v.shape, 1) // V_HEAD
    acc = None
    for hh in range(HEADS_PER_STEP):
        sl = slice(hh * HEAD_PAD, (hh + 1) * HEAD_PAD)
        s = lax.dot_general(q_ref[0, :, sl], k_ref[0, :, sl], (((1,), (1,)), ((), ())),
                            preferred_element_type=F32)
        m = jnp.max(s, axis=-1, keepdims=True)
        p = jnp.exp(s - m)
        l = jnp.sum(p, axis=-1, keepdims=True)
        vh = jnp.where(lane_head == hh, v, jnp.zeros_like(v))
        o = jnp.dot(p.astype(BF16), vh, preferred_element_type=F32) * (1.0 / l)
        acc = o if acc is None else acc + o
    o_ref[0] = acc.astype(BF16)


def _attn_call(q, k, v):
    wq = HEADS_PER_STEP * HEAD_PAD
    wv = HEADS_PER_STEP * V_HEAD
    return pl.pallas_call(
        _attn_kernel, grid=(BATCH, MLA_HEADS // HEADS_PER_STEP, SEQ // ATT_TQ),
        in_specs=[pl.BlockSpec((1, ATT_TQ, wq), lambda b, h, i: (b, i, h)),
                  pl.BlockSpec((1, KV_LEN, wq), lambda b, h, i: (b, 0, h)),
                  pl.BlockSpec((1, KV_LEN, wv), lambda b, h, i: (b, 0, h))],
        out_specs=pl.BlockSpec((1, ATT_TQ, wv), lambda b, h, i: (b, i, h)),
        out_shape=jax.ShapeDtypeStruct((BATCH, SEQ, MLA_WIDTH), BF16),
        compiler_params=_params(("parallel", "parallel", "arbitrary")), name="mla_attention",
    )(q, k, v)


S5_ROWS = S5_NCHUNK * BATCH
S5_ROWS_LAT = S5_NCHUNK_LAT * BATCH
S5_COLS = S5_CHUNK * S5_GROUP
S5_SW = 2 * S5_STATE


def _s5_kernel(u_ref, t_ref, mb_ref, mc_ref, coef_ref, y_ref, x_sc, sp_sc):
    u = u_ref[0]
    x_sc[...] = jnp.dot(u, mb_ref[0], preferred_element_type=F32)
    cf = coef_ref[0]
    af, bfm, bfp, ab, bbm, bbp = [cf[i * 8:(i + 1) * 8] for i in range(6)]

    def body(i, carry):
        sf, sfw, sb, sbw = carry
        cfw = jnp.where(i < S5_NCHUNK_CTX, i + S5_NCHUNK_LAT, i - S5_NCHUNK_CTX)
        rf = pl.multiple_of(cfw * BATCH, BATCH)
        rb = pl.multiple_of((S5_NCHUNK - 1 - i) * BATCH, BATCH)
        sp_sc[pl.ds(rf, BATCH), 0:S5_SW] = sf
        sp_sc[pl.ds(rb, BATCH), S5_SW:2 * S5_SW] = sb
        xf = x_sc[pl.ds(rf, BATCH), 0:S5_SW]
        xfw = x_sc[pl.ds(rf, BATCH), S5_SW:2 * S5_SW]
        xb = x_sc[pl.ds(rb, BATCH), 2 * S5_SW:3 * S5_SW]
        xbw = x_sc[pl.ds(rb, BATCH), 3 * S5_SW:4 * S5_SW]
        return (sf * af + sfw * bfm + xf, sfw * af + sf * bfp + xfw,
                sb * ab + sbw * bbm + xb, sbw * ab + sb * bbp + xbw)

    z = jnp.zeros((BATCH, S5_SW), F32)
    lax.fori_loop(0, S5_NCHUNK, body, (z, z, z, z))
    y = jnp.dot(u[:S5_ROWS_LAT], t_ref[0], preferred_element_type=F32)
    y = y + jnp.dot(sp_sc[0:S5_ROWS_LAT, :].astype(BF16), mc_ref[0], preferred_element_type=F32)
    y_ref[0] = y


def _s5_call(ug, t, mb, mc, coef):
    g3 = lambda g: (g, 0, 0)
    return pl.pallas_call(
        _s5_kernel, grid=(S5_GROUPS,),
        in_specs=[pl.BlockSpec((1, S5_ROWS, S5_COLS), g3),
                  pl.BlockSpec((1, S5_COLS, S5_COLS), g3),
                  pl.BlockSpec((1, S5_COLS, 4 * S5_SW), g3),
                  pl.BlockSpec((1, 2 * S5_SW, S5_COLS), g3),
                  pl.BlockSpec((1, 6 * 8, S5_SW), g3)],
        out_specs=pl.BlockSpec((1, S5_ROWS_LAT, S5_COLS), g3),
        out_shape=jax.ShapeDtypeStruct((S5_GROUPS, S5_ROWS_LAT, S5_COLS), F32),
        scratch_shapes=[pltpu.VMEM((S5_ROWS, 4 * S5_SW), F32), pltpu.VMEM((S5_ROWS, 2 * S5_SW), F32)],
        compiler_params=_params(("parallel",)), name="s5_chunked_scan",
    )(ug, t, mb, mc, coef)


def _s5_weights(lam_re, lam_im, log_step, b_re, b_im, c_re, c_im):
    q = S5_CHUNK
    hi = lax.Precision.HIGHEST
    t_blocks, mbs, mcs, coefs = [], [], [], []
    sig = jnp.arange(q)
    for d in range(2):
        lr = jnp.minimum(lam_re[d], -1e-4)
        li = lam_im[d]
        step = jnp.exp(log_step[d])[:, None]
        jj = jnp.arange(q + 1, dtype=F32)[:, None, None]
        mag = jnp.exp(lr * step * jj)
        ph = li * step * jj
        pr, pi = mag * jnp.cos(ph), mag * jnp.sin(ph)
        nr, ni = pr[1] - 1.0, pi[1]
        den = lr * lr + li * li
        fr, fi = (nr * lr + ni * li) / den, (ni * lr - nr * li) / den
        br = fr[..., None] * b_re[d] - fi[..., None] * b_im[d]
        bi = fr[..., None] * b_im[d] + fi[..., None] * b_re[d]
        cr, ci = c_re[d], c_im[d]
        cpr = cr[None] * pr[:, :, None, :] - ci[None] * pi[:, :, None, :]
        cpi = cr[None] * pi[:, :, None, :] + ci[None] * pr[:, :, None, :]
        kern = (jnp.einsum('jghp,gpk->jghk', cpr[:q], br, precision=hi)
                - jnp.einsum('jghp,gpk->jghk', cpi[:q], bi, precision=hi))
        lag = (sig[None, :] - sig[:, None]) if d == 0 else (sig[:, None] - sig[None, :])
        kt = kern.transpose(1, 0, 3, 2)
        tb = jnp.where((lag >= 0)[None, :, :, None, None], kt[:, jnp.clip(lag, 0, q - 1)], 0.0)
        t_blocks.append(tb.transpose(0, 1, 3, 2, 4).reshape(S5_GROUPS, S5_COLS, S5_COLS))
        pw = (q - 1 - sig) if d == 0 else sig
        xr = pr[pw][..., None] * br[None] - pi[pw][..., None] * bi[None]
        xi = pr[pw][..., None] * bi[None] + pi[pw][..., None] * br[None]
        xr = xr.transpose(1, 0, 3, 2).reshape(S5_GROUPS, S5_COLS, S5_STATE)
        xi = xi.transpose(1, 0, 3, 2).reshape(S5_GROUPS, S5_COLS, S5_STATE)
        mbs += [xr, xi, xi, xr]
        po = (sig + 1) if d == 0 else (q - sig)
        mr = cpr[po].transpose(1, 3, 0, 2).reshape(S5_GROUPS, S5_STATE, S5_COLS)
        mi = -cpi[po].transpose(1, 3, 0, 2).reshape(S5_GROUPS, S5_STATE, S5_COLS)
        mcs += [mr, mi]
        are, aim = pr[q], pi[q]
        rows = [jnp.concatenate([are, are], -1), jnp.concatenate([-aim, aim], -1), jnp.concatenate([aim, -aim], -1)]
        coefs += [jnp.broadcast_to(r[:, None, :], (S5_GROUPS, 8, S5_SW)) for r in rows]
    t = (t_blocks[0] + t_blocks[1]).astype(BF16)
    mb = jnp.concatenate(mbs, axis=-1).astype(BF16)
    mc = jnp.concatenate(mcs, axis=1).astype(BF16)
    coef = jnp.concatenate(coefs, axis=1)
    return t, mb, mc, coef


A1_TL = 512


def _a1_kernel(att_ref, ys_ref, u_ref, d_ref, gw_ref, gb_ref, woa_ref, wos_ref, x_ref, g_ref, o_ref):
    y = u_ref[0] * d_ref[...] + ys_ref[0]
    z = jax.nn.gelu(y)
    gate = jax.nn.sigmoid(jnp.dot(z.astype(BF16), gw_ref[...], preferred_element_type=F32) + gb_ref[...])
    s5 = (z * gate).astype(BF16)
    mix = jnp.dot(att_ref[0], woa_ref[...], preferred_element_type=F32)
    mix = mix + jnp.dot(s5, wos_ref[...], preferred_element_type=F32)
    o_ref[0] = x_ref[0] + g_ref[0] * mix


def _a1_call(att, ys, u, d, gw, gb, woa, wos, x, g1):
    t3 = lambda b, j: (b, j, 0)
    full2 = lambda b, j: (0, 0)
    return pl.pallas_call(
        _a1_kernel, grid=(BATCH, SEQ // A1_TL),
        in_specs=[pl.BlockSpec((1, A1_TL, MLA_WIDTH), t3),
                  pl.BlockSpec((1, A1_TL, S5_WIDTH), t3),
                  pl.BlockSpec((1, A1_TL, S5_WIDTH), t3),
                  pl.BlockSpec((1, S5_WIDTH), full2),
                  pl.BlockSpec((S5_WIDTH, S5_WIDTH), full2),
                  pl.BlockSpec((1, S5_WIDTH), full2),
                  pl.BlockSpec((MLA_WIDTH, D_MODEL), full2),
                  pl.BlockSpec((S5_WIDTH, D_MODEL), full2),
                  pl.BlockSpec((1, A1_TL, D_MODEL), t3),
                  pl.BlockSpec((1, 1, D_MODEL), lambda b, j: (b, 0, 0))],
        out_specs=pl.BlockSpec((1, A1_TL, D_MODEL), t3),
        out_shape=jax.ShapeDtypeStruct((BATCH, SEQ, D_MODEL), F32),
        compiler_params=_params(("parallel", "arbitrary")), name="even_out_proj",
    )(att, ys, u, d, gw, gb, woa, wos, x, g1)


MOE_TL = 512
SLOT_PAD = 8


def _slot_columns(cols, lane):
    out = jnp.zeros(lane.shape, cols[0].dtype)
    for k, col in enumerate(cols):
        out = jnp.where(lane == k, col, out)
    return out[:, :SLOT_PAD]


def _moe_in_kernel(x_ref, sh_ref, sc_ref, rw_ref, rb_ref, sg_ref, su_ref, sd_ref, tri_ref,
                   h_ref, idx_ref, wt_ref, rank_ref, cnt_ref, o_ref, run_sc):
    @pl.when((pl.program_id(0) == 0) & (pl.program_id(1) == 0))
    def _():
        run_sc[...] = jnp.zeros_like(run_sc)

    h = _norm_mod(x_ref[0], sh_ref[0], sc_ref[0])
    logits = jnp.dot(h, rw_ref[...], preferred_element_type=F32, precision=lax.Precision.HIGHEST)
    scores = jax.nn.sigmoid(logits)
    hb = h.astype(BF16)
    h_ref[0] = hb
    hid = jax.nn.silu(jnp.dot(hb, sg_ref[...], preferred_element_type=F32))
    hid = hid * jnp.dot(hb, su_ref[...], preferred_element_type=F32)
    o_ref[0] = jnp.dot(hid.astype(BF16), sd_ref[...], preferred_element_type=F32)

    work = scores + rb_ref[...]
    lane = lax.broadcasted_iota(jnp.int32, work.shape, 1)
    hits, ids = [], []
    for _ in range(TOP_K):
        m = jnp.max(work, axis=-1, keepdims=True)
        ik = jnp.min(jnp.where(work == m, lane, N_EXPERTS), axis=-1, keepdims=True)
        hit = lane == ik
        hits.append(hit)
        ids.append(ik)
        work = jnp.where(hit, -jnp.inf, work)
    mask = hits[0]
    for hit in hits[1:]:
        mask = jnp.logical_or(mask, hit)
    maskf = mask.astype(F32)
    before = jnp.dot(tri_ref[...], maskf.astype(BF16), preferred_element_type=F32) + run_sc[...]
    sel = [jnp.sum(jnp.where(hit, scores, 0.0), axis=-1, keepdims=True) for hit in hits]
    denom = sel[0]
    for s in sel[1:]:
        denom = denom + s
    ranks = [jnp.sum(jnp.where(hit, before, 0.0), axis=-1, keepdims=True) for hit in hits]
    lane128 = lax.broadcasted_iota(jnp.int32, (MOE_TL, 128), 1)
    idx_ref[0] = _slot_columns(ids, lane128)
    wt_ref[0] = _slot_columns([s / denom * ROUTE_SCALE for s in sel], lane128)
    rank_ref[0] = _slot_columns([r.astype(jnp.int32) for r in ranks], lane128)
    run_sc[...] += jnp.sum(maskf, axis=0, keepdims=True)
    cnt_ref[...] = run_sc[...]


def _moe_in_call(x, sh, sc, rw, rb, sg, su, sd):
    t3 = lambda b, j: (b, j, 0)
    full2 = lambda b, j: (0, 0)
    per_b = lambda b, j: (b, 0, 0)
    ff = sg.shape[1]
    tri = jnp.asarray(np.tril(np.ones((MOE_TL, MOE_TL), np.float32), -1), BF16)
    slot = jax.ShapeDtypeStruct((BATCH, SEQ, SLOT_PAD), jnp.int32)
    return pl.pallas_call(
        _moe_in_kernel, grid=(BATCH, SEQ // MOE_TL),
        in_specs=[pl.BlockSpec((1, MOE_TL, D_MODEL), t3),
                  pl.BlockSpec((1, 1, D_MODEL), per_b),
                  pl.BlockSpec((1, 1, D_MODEL), per_b),
                  pl.BlockSpec((D_MODEL, N_EXPERTS), full2),
                  pl.BlockSpec((1, N_EXPERTS), full2),
                  pl.BlockSpec((D_MODEL, ff), full2),
                  pl.BlockSpec((D_MODEL, ff), full2),
                  pl.BlockSpec((ff, D_MODEL), full2),
                  pl.BlockSpec((MOE_TL, MOE_TL), full2)],
        out_specs=[pl.BlockSpec((1, MOE_TL, D_MODEL), t3),
                   pl.BlockSpec((1, MOE_TL, SLOT_PAD), t3),
                   pl.BlockSpec((1, MOE_TL, SLOT_PAD), t3),
                   pl.BlockSpec((1, MOE_TL, SLOT_PAD), t3),
                   pl.BlockSpec((1, N_EXPERTS), full2),
                   pl.BlockSpec((1, MOE_TL, D_MODEL), t3)],
        out_shape=[jax.ShapeDtypeStruct((BATCH, SEQ, D_MODEL), BF16),
                   slot,
                   jax.ShapeDtypeStruct((BATCH, SEQ, SLOT_PAD), F32),
                   slot,
                   jax.ShapeDtypeStruct((1, N_EXPERTS), F32),
                   jax.ShapeDtypeStruct((BATCH, SEQ, D_MODEL), F32)],
        scratch_shapes=[pltpu.VMEM((1, N_EXPERTS), F32)],
        compiler_params=_params(("arbitrary", "arbitrary")), name="moe_router_shared",
    )(x, sh, sc, rw, rb, sg, su, sd, tri)


def _expert_kernel(be_ref, x_ref, wg_ref, wu_ref, wd_ref, o_ref):
    del be_ref
    xb = x_ref[...]
    hid = jax.nn.silu(jnp.dot(xb, wg_ref[0], preferred_element_type=F32))
    hid = hid * jnp.dot(xb, wu_ref[0], preferred_element_type=F32)
    o_ref[...] = jnp.dot(hid.astype(BF16), wd_ref[0], preferred_element_type=F32).astype(BF16)


def _expert_call(block_e, xb, wg, wu, wd):
    n_rows = xb.shape[0]
    n_blocks = n_rows // MOE_BLOCK
    grid_spec = pltpu.PrefetchScalarGridSpec(
        num_scalar_prefetch=1, grid=(n_blocks,),
        in_specs=[pl.BlockSpec((MOE_BLOCK, D_MODEL), lambda i, be: (i, 0)),
                  pl.BlockSpec((1, D_MODEL, EXPERT_FF), lambda i, be: (be[i], 0, 0)),
                  pl.BlockSpec((1, D_MODEL, EXPERT_FF), lambda i, be: (be[i], 0, 0)),
                  pl.BlockSpec((1, EXPERT_FF, D_MODEL), lambda i, be: (be[i], 0, 0))],
        out_specs=pl.BlockSpec((MOE_BLOCK, D_MODEL), lambda i, be: (i, 0)))
    return pl.pallas_call(
        _expert_kernel, grid_spec=grid_spec,
        out_shape=jax.ShapeDtypeStruct((n_rows, D_MODEL), BF16),
        compiler_params=_params(("arbitrary",)), name="moe_experts",
    )(block_e, xb, wg, wu, wd)


def _combine_kernel(y_ref, w_ref, s_ref, x_ref, g_ref, o_ref):
    w = w_ref[0]
    acc = s_ref[0]
    for k in range(TOP_K):
        acc = acc + w[:, k:k + 1] * y_ref[0, :, k * D_MODEL:(k + 1) * D_MODEL].astype(F32)
    o_ref[0] = x_ref[0] + g_ref[0] * acc


def _combine_call(yg, wts, shared, x, g2):
    t3 = lambda b, j: (b, j, 0)
    return pl.pallas_call(
        _combine_kernel, grid=(BATCH, SEQ // MOE_TL),
        in_specs=[pl.BlockSpec((1, MOE_TL, TOP_K * D_MODEL), t3),
                  pl.BlockSpec((1, MOE_TL, SLOT_PAD), t3),
                  pl.BlockSpec((1, MOE_TL, D_MODEL), t3),
                  pl.BlockSpec((1, MOE_TL, D_MODEL), t3),
                  pl.BlockSpec((1, 1, D_MODEL), lambda b, j: (b, 0, 0))],
        out_specs=pl.BlockSpec((1, MOE_TL, D_MODEL), t3),
        out_shape=jax.ShapeDtypeStruct((BATCH, SEQ, D_MODEL), F32),
        compiler_params=_params(("parallel", "arbitrary")), name="moe_combine",
    )(yg, wts, shared, x, g2)


def _moe(x, sh, sc, g2, router_w, router_b, w_gate, w_up, w_down, sh_gate, sh_up, sh_down):
    T = BATCH * SEQ
    TK = T * TOP_K
    hb, idx, wts, rank, counts, shared = _moe_in_call(
        x, sh, sc, router_w, router_b[None, :], sh_gate.astype(BF16), sh_up.astype(BF16), sh_down.astype(BF16))
    counts = counts[0].astype(jnp.int32)
    padded = (counts + MOE_BLOCK - 1) // MOE_BLOCK * MOE_BLOCK
    pad_end = jnp.cumsum(padded)
    pad_start = pad_end - padded
    n_blocks = -(-TK // MOE_BLOCK) + N_EXPERTS
    n_rows = n_blocks * MOE_BLOCK
    block_start = jnp.arange(n_blocks, dtype=jnp.int32) * MOE_BLOCK
    block_e = jnp.minimum(jnp.searchsorted(pad_end, block_start, side='right'), N_EXPERTS - 1).astype(jnp.int32)
    idx = idx.reshape(T, SLOT_PAD)[:, :TOP_K]
    dest = pad_start[idx] + rank.reshape(T, SLOT_PAD)[:, :TOP_K]
    rows = jnp.broadcast_to(hb.reshape(T, 1, D_MODEL), (T, TOP_K, D_MODEL))
    xb = jnp.zeros((n_rows, D_MODEL), BF16).at[dest].set(rows, unique_indices=True)
    ys = _expert_call(block_e, xb, w_gate.astype(BF16), w_up.astype(BF16), w_down.astype(BF16))
    yg = ys[dest].reshape(BATCH, SEQ, TOP_K * D_MODEL)
    return _combine_call(yg, wts, shared, x, g2)


HY_TL = 512
HALO = 8


def _hy_in_kernel(x_ref, xp_ref, xn_ref, sh_ref, sc_ref, w_ref, cw_ref, cb_ref, z_ref, x0_ref, h_sc):
    j = pl.program_id(1)
    shift, scale = sh_ref[0], sc_ref[0]
    keep_prev = (j > 0).astype(F32)
    keep_next = (j < SEQ // HY_TL - 1).astype(F32)
    h_sc[0:HALO, :] = _norm_mod(xp_ref[0], shift, scale) * keep_prev
    h_sc[HALO:HALO + HY_TL, :] = _norm_mod(x_ref[0], shift, scale)
    h_sc[HALO + HY_TL:, :] = _norm_mod(xn_ref[0], shift, scale) * keep_next
    hcat = h_sc[...].astype(BF16)
    outs = []
    for part in range(3):
        sl = slice(part * HY_WIDTH, (part + 1) * HY_WIDTH)
        p = jnp.dot(hcat, w_ref[:, sl], preferred_element_type=F32)
        o = (p[HALO - 1:HALO - 1 + HY_TL] * cw_ref[0:1, sl] + p[HALO:HALO + HY_TL] * cw_ref[1:2, sl]
             + p[HALO + 1:HALO + 1 + HY_TL] * cw_ref[2:3, sl] + cb_ref[:, sl])
        outs.append(o)
    x0_ref[0] = outs[0].astype(BF16)
    z_ref[0] = (outs[2] * outs[1]).astype(BF16)


def _hy_in_call(x, sh, sc, w, cw, cb):
    nb8 = HY_TL // HALO
    t3 = lambda b, j: (b, j, 0)
    full2 = lambda b, j: (0, 0)
    per_b = lambda b, j: (b, 0, 0)
    return pl.pallas_call(
        _hy_in_kernel, grid=(BATCH, SEQ // HY_TL),
        in_specs=[pl.BlockSpec((1, HY_TL, D_MODEL), t3),
                  pl.BlockSpec((1, HALO, D_MODEL), lambda b, j: (b, jnp.maximum(j * nb8 - 1, 0), 0)),
                  pl.BlockSpec((1, HALO, D_MODEL), lambda b, j: (b, jnp.minimum((j + 1) * nb8, SEQ // HALO - 1), 0)),
                  pl.BlockSpec((1, 1, D_MODEL), per_b),
                  pl.BlockSpec((1, 1, D_MODEL), per_b),
                  pl.BlockSpec((D_MODEL, 3 * HY_WIDTH), full2),
                  pl.BlockSpec((SHORT_CONV, 3 * HY_WIDTH), full2),
                  pl.BlockSpec((1, 3 * HY_WIDTH), full2)],
        out_specs=[pl.BlockSpec((1, HY_TL, HY_WIDTH), t3), pl.BlockSpec((1, HY_TL, HY_WIDTH), t3)],
        out_shape=[jax.ShapeDtypeStruct((BATCH, SEQ, HY_WIDTH), BF16),
                   jax.ShapeDtypeStruct((BATCH, SEQ, HY_WIDTH), BF16)],
        scratch_shapes=[pltpu.VMEM((HY_TL + 2 * HALO, D_MODEL), F32)],
        compiler_params=_params(("parallel", "arbitrary")), name="hyena_in_proj",
    )(x, x, x, sh, sc, w, cw, cb)


def _dft_matrices():
    n = DFT_N
    k = np.arange(SEQ, dtype=np.int64)[:, None]
    a = np.arange(64, dtype=np.int64)[None, :]
    ang_a = 2.0 * np.pi * ((k * a * 64) % n) / n
    ang_c = 2.0 * np.pi * ((k * a) % n) / n
    ca, sa, cc, sc = [jnp.asarray(v, F32) for v in (np.cos(ang_a), np.sin(ang_a), np.cos(ang_c), np.sin(ang_c))]
    f_re = (ca[:, :, None] * cc[:, None, :] - sa[:, :, None] * sc[:, None, :]).reshape(SEQ, SEQ)
    f_im = -(sa[:, :, None] * cc[:, None, :] + ca[:, :, None] * sc[:, None, :]).reshape(SEQ, SEQ)
    alt = jnp.asarray(1.0 - 2.0 * (np.arange(SEQ) % 2), F32)
    f_im = f_im.at[0, :].set(alt)
    fwd = jnp.concatenate([f_re.reshape(DFT_NF, DFT_TF, SEQ), f_im.reshape(DFT_NF, DFT_TF, SEQ)], axis=1)
    g_re = f_re * (2.0 / n)
    g_re = g_re.at[:, 0].set(1.0 / n)
    g_im = f_im.T * (2.0 / n)
    g_im = g_im.at[:, 0].set(alt / n)
    inv = jnp.concatenate([g_re.reshape(SEQ, DFT_NF, DFT_TF), g_im.reshape(SEQ, DFT_NF, DFT_TF)], axis=2)
    inv = inv.transpose(1, 0, 2)
    return fwd.astype(BF16), inv.astype(BF16)


def _spec_kernel(f_ref, h_ref, p_ref, pn_ref, q_ref):
    spec = jnp.dot(f_ref[0], h_ref[...], preferred_element_type=F32)
    re, im = spec[:DFT_TF], spec[DFT_TF:]
    c_re = re[:, :HY_WIDTH] + re[:, HY_WIDTH:]
    c_im = im[:, :HY_WIDTH] - im[:, HY_WIDTH:]
    nyq = im[:, :HY_WIDTH] + im[:, HY_WIDTH:]
    row = lax.broadcasted_iota(jnp.int32, c_re.shape, 0) + pl.program_id(0) * DFT_TF
    first = row == 0
    p_ref[...] = c_re
    pn_ref[...] = jnp.where(first, nyq, c_re)
    q_ref[...] = jnp.where(first, 0.0, c_im)


def _spec_call(fwd, hfb):
    out = jax.ShapeDtypeStruct((SEQ, HY_WIDTH), F32)
    ospec = pl.BlockSpec((DFT_TF, HY_WIDTH), lambda f: (f, 0))
    return pl.pallas_call(
        _spec_kernel, grid=(DFT_NF,),
        in_specs=[pl.BlockSpec((1, 2 * DFT_TF, SEQ), lambda f: (f, 0, 0)),
                  pl.BlockSpec((SEQ, 2 * HY_WIDTH), lambda f: (0, 0))],
        out_specs=[ospec, ospec, ospec], out_shape=[out, out, out],
        compiler_params=_params(("arbitrary",)), name="hyena_filter_spectrum",
    )(fwd, hfb)


def _conv_kernel(z_ref, f_ref, g_ref, p_ref, pn_ref, q_ref, y_ref, acc):
    f = pl.program_id(2)
    spec = jnp.dot(f_ref[0], z_ref[0], preferred_element_type=F32)
    zr, zi = spec[:DFT_TF], spec[DFT_TF:]
    yr = zr * p_ref[...] - zi * q_ref[...]
    yi = zr * q_ref[...] + zi * pn_ref[...]
    ycat = jnp.concatenate([yr, yi], axis=0).astype(BF16)
    part = jnp.dot(g_ref[0], ycat, preferred_element_type=F32)

    @pl.when(f == 0)
    def _():
        acc[...] = part

    @pl.when(f > 0)
    def _():
        acc[...] += part

    @pl.when(f == DFT_NF - 1)
    def _():
        y_ref[0] = acc[...].astype(BF16)


def _conv_call(z, fwd, inv, p, pn, q):
    sspec = pl.BlockSpec((DFT_TF, HY_CT), lambda b, c, f: (f, c))
    return pl.pallas_call(
        _conv_kernel, grid=(BATCH, HY_WIDTH // HY_CT, DFT_NF),
        in_specs=[pl.BlockSpec((1, SEQ, HY_CT), lambda b, c, f: (b, 0, c)),
                  pl.BlockSpec((1, 2 * DFT_TF, SEQ), lambda b, c, f: (f, 0, 0)),
                  pl.BlockSpec((1, SEQ, 2 * DFT_TF), lambda b, c, f: (f, 0, 0)),
                  sspec, sspec, sspec],
        out_specs=pl.BlockSpec((1, SEQ, HY_CT), lambda b, c, f: (b, 0, c)),
        out_shape=jax.ShapeDtypeStruct((BATCH, SEQ, HY_WIDTH), BF16),
        scratch_shapes=[pltpu.VMEM((SEQ, HY_CT), F32)],
        compiler_params=_params(("parallel", "parallel", "arbitrary")), name="hyena_long_conv",
    )(z, fwd, inv, p, pn, q)


def _hy_out_kernel(y_ref, z_ref, x0_ref, b_ref, w_ref, x_ref, g_ref, o_ref):
    z = z_ref[0].astype(F32)
    gated = x0_ref[0].astype(F32) * (y_ref[0].astype(F32) + b_ref[...] * z)
    mix = jnp.dot(gated.astype(BF16), w_ref[...], preferred_element_type=F32)
    o_ref[0] = x_ref[0] + g_ref[0] * mix


def _hy_out_call(y, z, x0, bias, w, x, g1):
    t3 = lambda b, j: (b, j, 0)
    full2 = lambda b, j: (0, 0)
    return pl.pallas_call(
        _hy_out_kernel, grid=(BATCH, SEQ // HY_TL),
        in_specs=[pl.BlockSpec((1, HY_TL, HY_WIDTH), t3),
                  pl.BlockSpec((1, HY_TL, HY_WIDTH), t3),
                  pl.BlockSpec((1, HY_TL, HY_WIDTH), t3),
                  pl.BlockSpec((1, HY_WIDTH), full2),
                  pl.BlockSpec((HY_WIDTH, D_MODEL), full2),
                  pl.BlockSpec((1, HY_TL, D_MODEL), t3),
                  pl.BlockSpec((1, 1, D_MODEL), lambda b, j: (b, 0, 0))],
        out_specs=pl.BlockSpec((1, HY_TL, D_MODEL), t3),
        out_shape=jax.ShapeDtypeStruct((BATCH, SEQ, D_MODEL), F32),
        compiler_params=_params(("parallel", "arbitrary")), name="hyena_out_proj",
    )(y, z, x0, bias, w, x, g1)


def _hyena_filter(w1, b1, w2, b2, w3, freq):
    hi = lax.Precision.HIGHEST
    Lq = SEQ
    t = jnp.linspace(0.0, 1.0, Lq, dtype=F32)[:, None]
    ang = 2.0 * math.pi * jnp.arange(Lq, dtype=F32)[:, None] / Lq
    bands = jnp.linspace(1e-4, FILT_BANDS - 1, FILT_BANDS, dtype=F32)
    z = jnp.concatenate([t, jnp.cos(bands * ang), -jnp.sin(bands * ang)], axis=-1)
    hid = jnp.sin(freq * (jnp.dot(z, w1, precision=hi) + b1))
    hid = jnp.sin(freq * (jnp.dot(hid, w2, precision=hi) + b2))
    hf = jnp.dot(hid, w3, precision=hi).reshape(Lq, 2, HY_WIDTH)
    deltas = jnp.linspace(HY_MIN_DECAY, HY_MAX_DECAY, HY_WIDTH, dtype=F32)
    hf = hf * jnp.exp(-t * deltas)[:, None, :]
    hf = hf * lax.rsqrt(jnp.sum(hf * hf, axis=(0, 1), keepdims=True) + EPS)
    return hf.reshape(Lq, 2 * HY_WIDTH)


def kernel(x, c, ctx, c_ctx, ada_w, ada_b, ev_w_in, mla_q_norm, mla_w_uq, mla_kv_norm, mla_w_ukv, mla_q_qknorm, mla_k_qknorm, s5_lam_re, s5_lam_im, s5_log_step, s5_b_re, s5_b_im, s5_c_re, s5_c_im, s5_d, s5_glu_w, s5_glu_b, ev_w_out, hy_w_in, hy_conv_w, hy_conv_b, hy_f_w1, hy_f_b1, hy_f_w2, hy_f_b2, hy_f_w3, hy_f_freq, hy_bias, hy_w_out, moe_router_w, moe_router_b, moe_w_gate, moe_w_up, moe_w_down, moe_sh_gate, moe_sh_up, moe_sh_down):
    hi = lax.Precision.HIGHEST
    D = D_MODEL
    sc = jax.nn.silu(c)
    sc_ctx = jax.nn.silu(c_ctx)

    def mods(li):
        mod = jnp.dot(sc, ada_w[li], precision=hi) + ada_b[li]
        return [m[:, None, :] for m in jnp.split(mod, 6, axis=-1)]

    sh1, sc1, g1, sh2, sc2, g2 = mods(0)
    mod_ctx = jnp.dot(sc_ctx, ada_w[0][:, :2 * D], precision=hi) + ada_b[0][:2 * D]
    w0 = _a0_weights(ev_w_in[0], mla_q_norm[0], mla_w_uq[0], mla_kv_norm[0], mla_w_ukv[0],
                     mla_q_qknorm[0], mla_k_qknorm[0])
    q, k, v, u = _a0_call(x, ctx, sh1, sc1, mod_ctx[None, :D], mod_ctx[None, D:], w0)
    att = _attn_call(q, k, v)
    ug = u.astype(BF16).reshape(BATCH, S5_NCHUNK, S5_CHUNK, S5_GROUPS, S5_GROUP)
    ug = ug.transpose(3, 1, 0, 2, 4).reshape(S5_GROUPS, S5_ROWS, S5_COLS)
    ys = _s5_call(ug, *_s5_weights(s5_lam_re[0], s5_lam_im[0], s5_log_step[0], s5_b_re[0], s5_b_im[0],
                                   s5_c_re[0], s5_c_im[0]))
    ys = ys.reshape(S5_GROUPS, S5_NCHUNK_LAT, BATCH, S5_CHUNK, S5_GROUP)
    ys = ys.transpose(2, 1, 3, 0, 4).reshape(BATCH, SEQ, S5_WIDTH)
    wo = ev_w_out[0].astype(BF16)
    x = _a1_call(att, ys, u, s5_d[0][None, :], s5_glu_w[0].astype(BF16), s5_glu_b[0][None, :],
                 wo[:MLA_WIDTH], wo[MLA_WIDTH:], x, g1)
    x = _moe(x, sh2, sc2, g2, moe_router_w[0], moe_router_b[0], moe_w_gate[0], moe_w_up[0], moe_w_down[0],
             moe_sh_gate[0], moe_sh_up[0], moe_sh_down[0])

    sh1, sc1, g1, sh2, sc2, g2 = mods(1)
    z, x0 = _hy_in_call(x, sh1, sc1, hy_w_in[0].astype(BF16), hy_conv_w[0], hy_conv_b[0][None, :])
    fwd, inv = _dft_matrices()
    hfb = _hyena_filter(hy_f_w1[0], hy_f_b1[0], hy_f_w2[0], hy_f_b2[0], hy_f_w3[0], hy_f_freq[0])
    p, pn, qq = _spec_call(fwd, hfb.astype(BF16))
    y = _conv_call(z, fwd, inv, p, pn, qq)
    x = _hy_out_call(y, z, x0, hy_bias[0][None, :], hy_w_out[0].astype(BF16), x, g1)
    x = _moe(x, sh2, sc2, g2, moe_router_w[1], moe_router_b[1], moe_w_gate[1], moe_w_up[1], moe_w_down[1],
             moe_sh_gate[1], moe_sh_up[1], moe_sh_down[1])
    return x
```

```python
import functools
import math

import numpy as np
import jax
import jax.numpy as jnp
from jax import lax
from jax.experimental import pallas as pl
from jax.experimental.pallas import tpu as pltpu
from jax.experimental.pallas import tpu_sc as plsc

F32 = jnp.float32
BF16 = jnp.bfloat16

D_MODEL = 1024
BATCH = 8
SEQ = 4096
CTX_LEN = 256
KV_LEN = SEQ + CTX_LEN
GRID_W = 64
EPS = 1e-6

MLA_HEADS = 8
QK_NOPE = 64
QK_ROPE = 32
QK_HEAD = QK_NOPE + QK_ROPE
V_HEAD = 64
Q_LORA = 256
KV_LORA = 128
MLA_WIDTH = MLA_HEADS * V_HEAD
ROPE_BASE = 10000.0
HEAD_PAD = 128

S5_WIDTH = 512
S5_GROUP = 16
S5_GROUPS = S5_WIDTH // S5_GROUP
S5_STATE = 64
S5_CHUNK = 32
S5_NCHUNK = KV_LEN // S5_CHUNK
S5_NCHUNK_LAT = SEQ // S5_CHUNK
S5_NCHUNK_CTX = CTX_LEN // S5_CHUNK

HY_WIDTH = D_MODEL
FILT_EMB = 33
FILT_BANDS = (FILT_EMB - 1) // 2
SHORT_CONV = 3
HY_MIN_DECAY = -math.log(1e-2) / 1.5
HY_MAX_DECAY = -math.log(1e-2) / 0.3
DFT_N = 2 * SEQ
DFT_TF = 256
DFT_NF = SEQ // DFT_TF
HY_CT = 512

N_EXPERTS = 64
TOP_K = 6
EXPERT_FF = 256
ROUTE_SCALE = 2.5
MOE_BLOCK = 512
ROW_WORDS = D_MODEL // 4
SC_WINDOW = 128

V7X_VMEM_BYTES = 64 * 1024 * 1024
VMEM_LIMIT = V7X_VMEM_BYTES - 8 * 1024 * 1024


def _params(semantics):
    return pltpu.CompilerParams(dimension_semantics=semantics, vmem_limit_bytes=VMEM_LIMIT)


def _norm_mod(x, shift, scale):
    ms = jnp.mean(x * x, axis=-1, keepdims=True)
    return x * lax.rsqrt(ms + EPS) * (1.0 + scale) + shift


def _rms(x, gain, n):
    ms = jnp.sum(x * x, axis=-1, keepdims=True) * (1.0 / n)
    return x * lax.rsqrt(ms + EPS) * gain


A0_TL = 256
A0_NT = SEQ // A0_TL


def _rope_perm():
    return np.concatenate([np.arange(0, QK_ROPE, 2), np.arange(1, QK_ROPE, 2)])


def _rope_tables():
    t = np.arange(SEQ)
    row = (t // GRID_W).astype(np.float64)
    col = (t % GRID_W).astype(np.float64)
    n_freq = QK_ROPE // 4
    inv = ROPE_BASE ** (-np.arange(n_freq, dtype=np.float64) / n_freq)
    ang = np.concatenate([row[:, None] * inv, col[:, None] * inv], axis=-1)
    cos, sin = np.cos(ang), np.sin(ang)
    half = QK_ROPE // 2
    a = np.zeros((KV_LEN, HEAD_PAD))
    bp = np.zeros((KV_LEN, HEAD_PAD))
    bm = np.zeros((KV_LEN, HEAD_PAD))
    a[:, :QK_HEAD] = 1.0
    a[:SEQ, QK_NOPE:QK_NOPE + half] = cos
    a[:SEQ, QK_NOPE + half:QK_HEAD] = cos
    bp[:SEQ, QK_NOPE + half:QK_HEAD] = sin
    bm[:SEQ, QK_NOPE:QK_NOPE + half] = -sin
    return a, bp, bm


def _a0_kernel(x_ref, ctx_ref, sh_ref, sc_ref, shc_ref, scc_ref, win_ref, qn_ref, wuq_ref, kvn_ref,
               wk_ref, wuv_ref, qg_ref, kg_ref, ka_ref, kp_ref, km_ref, qa_ref, qp_ref, qm_ref,
               q_ref, k_ref, v_ref, u_ref):
    j = pl.program_id(1)
    is_ctx = j == A0_NT
    xin = jnp.where(is_ctx, ctx_ref[0], x_ref[0])
    shift = jnp.where(is_ctx, shc_ref[...], sh_ref[0])
    scale = jnp.where(is_ctx, scc_ref[...], sc_ref[0])
    h = _norm_mod(xin, shift, scale).astype(BF16)
    proj = jnp.dot(h, win_ref[...], preferred_element_type=F32)
    u_ref[0] = proj[:, 512:]

    c_kv = _rms(proj[:, Q_LORA:Q_LORA + KV_LORA], kvn_ref[...], KV_LORA).astype(BF16)
    v_ref[0] = jnp.dot(c_kv, wuv_ref[...], preferred_element_type=F32).astype(BF16)
    kin = jnp.concatenate([c_kv, proj[:, 384:512].astype(BF16)], axis=1)
    kf = jnp.dot(kin, wk_ref[...], preferred_element_type=F32)
    ka, kp, km = ka_ref[...], kp_ref[...], km_ref[...]
    for hd in range(MLA_HEADS):
        sl = slice(hd * HEAD_PAD, (hd + 1) * HEAD_PAD)
        kh = _rms(kf[:, sl], kg_ref[:, sl], QK_HEAD)
        kh = kh * ka + pltpu.roll(kh, 16, 1) * kp + pltpu.roll(kh, HEAD_PAD - 16, 1) * km
        k_ref[0, :, sl] = kh.astype(BF16)

    @pl.when(j < A0_NT)
    def _():
        ql = _rms(proj[:, :Q_LORA], qn_ref[...], Q_LORA).astype(BF16)
        qf = jnp.dot(ql, wuq_ref[...], preferred_element_type=F32)
        qa, qp, qm = qa_ref[...], qp_ref[...], qm_ref[...]
        for hd in range(MLA_HEADS):
            sl = slice(hd * HEAD_PAD, (hd + 1) * HEAD_PAD)
            qh = _rms(qf[:, sl], qg_ref[:, sl], QK_HEAD)
            qh = qh * qa + pltpu.roll(qh, 16, 1) * qp + pltpu.roll(qh, HEAD_PAD - 16, 1) * qm
            q_ref[0, :, sl] = qh.astype(BF16)


def _a0_call(x, ctx, sh, sc, shc, scc, w):
    nt = A0_NT
    lat = lambda b, j: (b, jnp.minimum(j, nt - 1), 0)
    full2 = lambda b, j: (0, 0)
    per_b = lambda b, j: (b, 0, 0)
    tab = pl.BlockSpec((A0_TL, HEAD_PAD), lambda b, j: (j, 0))
    in_specs = [
        pl.BlockSpec((1, A0_TL, D_MODEL), lat),
        pl.BlockSpec((1, CTX_LEN, D_MODEL), per_b),
        pl.BlockSpec((1, 1, D_MODEL), per_b),
        pl.BlockSpec((1, 1, D_MODEL), per_b),
        pl.BlockSpec((1, D_MODEL), full2),
        pl.BlockSpec((1, D_MODEL), full2),
        pl.BlockSpec((D_MODEL, 1024), full2),
        pl.BlockSpec((1, Q_LORA), full2),
        pl.BlockSpec((Q_LORA, MLA_HEADS * HEAD_PAD), full2),
        pl.BlockSpec((1, KV_LORA), full2),
        pl.BlockSpec((2 * KV_LORA, MLA_HEADS * HEAD_PAD), full2),
        pl.BlockSpec((KV_LORA, MLA_WIDTH), full2),
        pl.BlockSpec((1, MLA_HEADS * HEAD_PAD), full2),
        pl.BlockSpec((1, MLA_HEADS * HEAD_PAD), full2),
        tab, tab, tab, tab, tab, tab,
    ]
    out_specs = [
        pl.BlockSpec((1, A0_TL, MLA_HEADS * HEAD_PAD), lat),
        pl.BlockSpec((1, A0_TL, MLA_HEADS * HEAD_PAD), lambda b, j: (b, j, 0)),
        pl.BlockSpec((1, A0_TL, MLA_WIDTH), lambda b, j: (b, j, 0)),
        pl.BlockSpec((1, A0_TL, S5_WIDTH), lambda b, j: (b, j, 0)),
    ]
    out_shape = [
        jax.ShapeDtypeStruct((BATCH, SEQ, MLA_HEADS * HEAD_PAD), BF16),
        jax.ShapeDtypeStruct((BATCH, KV_LEN, MLA_HEADS * HEAD_PAD), BF16),
        jax.ShapeDtypeStruct((BATCH, KV_LEN, MLA_WIDTH), BF16),
        jax.ShapeDtypeStruct((BATCH, KV_LEN, S5_WIDTH), F32),
    ]
    return pl.pallas_call(
        _a0_kernel, grid=(BATCH, nt + 1), in_specs=in_specs, out_specs=out_specs, out_shape=out_shape,
        compiler_params=_params(("parallel", "arbitrary")), name="even_in_proj",
    )(x, ctx, sh, sc, shc, scc, *w)


def _a0_weights(w_in, q_norm, w_uq, kv_norm, w_ukv, q_qk, k_qk):
    perm = _rope_perm()
    kr0 = Q_LORA + KV_LORA
    w_cat = jnp.concatenate([
        w_in[:, :kr0], w_in[:, kr0:kr0 + QK_ROPE][:, perm],
        jnp.zeros((D_MODEL, HEAD_PAD - QK_ROPE), F32), w_in[:, kr0 + QK_ROPE:]], axis=1).astype(BF16)
    pad = HEAD_PAD - QK_HEAD

    def head_gain(g):
        gh = jnp.concatenate([g[:QK_NOPE], g[QK_NOPE:][perm], jnp.zeros((pad,), F32)])
        return jnp.tile(gh, MLA_HEADS)[None, :]

    uq = w_uq.reshape(Q_LORA, MLA_HEADS, QK_HEAD)
    uq = jnp.concatenate([uq[..., :QK_NOPE], uq[..., QK_NOPE:][..., perm],
                          jnp.zeros((Q_LORA, MLA_HEADS, pad), F32)], axis=-1)
    uq = uq.reshape(Q_LORA, MLA_HEADS * HEAD_PAD).astype(BF16)
    ukv = w_ukv.reshape(KV_LORA, MLA_HEADS, QK_NOPE + V_HEAD)
    uk = jnp.concatenate([ukv[..., :QK_NOPE], jnp.zeros((KV_LORA, MLA_HEADS, HEAD_PAD - QK_NOPE), F32)], axis=-1)
    uk = uk.reshape(KV_LORA, MLA_HEADS * HEAD_PAD)
    place = np.zeros((KV_LORA, MLA_HEADS, HEAD_PAD), np.float32)
    for i in range(QK_ROPE):
        place[i, :, QK_NOPE + i] = 1.0
    wk = jnp.concatenate([uk, jnp.asarray(place.reshape(KV_LORA, MLA_HEADS * HEAD_PAD))], axis=0).astype(BF16)
    wuv = ukv[..., QK_NOPE:].reshape(KV_LORA, MLA_WIDTH).astype(BF16)
    a, bp, bm = _rope_tables()
    qs = QK_HEAD ** -0.5
    tabs = [jnp.asarray(t, F32) for t in (a, bp, bm, a * qs, bp * qs, bm * qs)]
    return [w_cat, q_norm[None, :], uq, kv_norm[None, :], wk, wuv, head_gain(q_qk), head_gain(k_qk)] + tabs


ATT_TQ = 256
HEADS_PER_STEP = 2


def _attn_kernel(q_ref, k_ref, v_ref, o_ref):
    v = v_ref[0]
    lane_head = lax.broadcasted_iota(jnp.int32, v.shape, 1) // V_HEAD
    acc = None
    for hh in range(HEADS_PER_STEP):
        sl = slice(hh * HEAD_PAD, (hh + 1) * HEAD_PAD)
        s = lax.dot_general(q_ref[0, :, sl], k_ref[0, :, sl], (((1,), (1,)), ((), ())),
                            preferred_element_type=F32)
        m = jnp.max(s, axis=-1, keepdims=True)
        p = jnp.exp(s - m)
        l = jnp.sum(p, axis=-1, keepdims=True)
        vh = jnp.where(lane_head == hh, v, jnp.zeros_like(v))
        o = jnp.dot(p.astype(BF16), vh, preferred_element_type=F32) * (1.0 / l)
        acc = o if acc is None else acc + o
    o_ref[0] = acc.astype(BF16)


def _attn_call(q, k, v):
    wq = HEADS_PER_STEP * HEAD_PAD
    wv = HEADS_PER_STEP * V_HEAD
    return pl.pallas_call(
        _attn_kernel, grid=(BATCH, MLA_HEADS // HEADS_PER_STEP, SEQ // ATT_TQ),
        in_specs=[pl.BlockSpec((1, ATT_TQ, wq), lambda b, h, i: (b, i, h)),
                  pl.BlockSpec((1, KV_LEN, wq), lambda b, h, i: (b, 0, h)),
                  pl.BlockSpec((1, KV_LEN, wv), lambda b, h, i: (b, 0, h))],
        out_specs=pl.BlockSpec((1, ATT_TQ, wv), lambda b, h, i: (b, i, h)),
        out_shape=jax.ShapeDtypeStruct((BATCH, SEQ, MLA_WIDTH), BF16),
        compiler_params=_params(("parallel", "parallel", "arbitrary")), name="mla_attention",
    )(q, k, v)


S5_ROWS = S5_NCHUNK * BATCH
S5_ROWS_LAT = S5_NCHUNK_LAT * BATCH
S5_COLS = S5_CHUNK * S5_GROUP
S5_SW = 2 * S5_STATE


def _s5_kernel(u_ref, t_ref, mb_ref, mc_ref, coef_ref, y_ref, x_sc, sp_sc):
    u = u_ref[0]
    x_sc[...] = jnp.dot(u, mb_ref[0], preferred_element_type=F32)
    cf = coef_ref[0]
    af, bfm, bfp, ab, bbm, bbp = [cf[i * 8:(i + 1) * 8] for i in range(6)]

    def body(i, carry):
        sf, sfw, sb, sbw = carry
        cfw = jnp.where(i < S5_NCHUNK_CTX, i + S5_NCHUNK_LAT, i - S5_NCHUNK_CTX)
        rf = pl.multiple_of(cfw * BATCH, BATCH)
        rb = pl.multiple_of((S5_NCHUNK - 1 - i) * BATCH, BATCH)
        sp_sc[pl.ds(rf, BATCH), 0:S5_SW] = sf
        sp_sc[pl.ds(rb, BATCH), S5_SW:2 * S5_SW] = sb
        xf = x_sc[pl.ds(rf, BATCH), 0:S5_SW]
        xfw = x_sc[pl.ds(rf, BATCH), S5_SW:2 * S5_SW]
        xb = x_sc[pl.ds(rb, BATCH), 2 * S5_SW:3 * S5_SW]
        xbw = x_sc[pl.ds(rb, BATCH), 3 * S5_SW:4 * S5_SW]
        return (sf * af + sfw * bfm + xf, sfw * af + sf * bfp + xfw,
                sb * ab + sbw * bbm + xb, sbw * ab + sb * bbp + xbw)

    z = jnp.zeros((BATCH, S5_SW), F32)
    lax.fori_loop(0, S5_NCHUNK, body, (z, z, z, z))
    y = jnp.dot(u[:S5_ROWS_LAT], t_ref[0], preferred_element_type=F32)
    y = y + jnp.dot(sp_sc[0:S5_ROWS_LAT, :].astype(BF16), mc_ref[0], preferred_element_type=F32)
    y_ref[0] = y


def _s5_call(ug, t, mb, mc, coef):
    g3 = lambda g: (g, 0, 0)
    return pl.pallas_call(
        _s5_kernel, grid=(S5_GROUPS,),
        in_specs=[pl.BlockSpec((1, S5_ROWS, S5_COLS), g3),
                  pl.BlockSpec((1, S5_COLS, S5_COLS), g3),
                  pl.BlockSpec((1, S5_COLS, 4 * S5_SW), g3),
                  pl.BlockSpec((1, 2 * S5_SW, S5_COLS), g3),
                  pl.BlockSpec((1, 6 * 8, S5_SW), g3)],
        out_specs=pl.BlockSpec((1, S5_ROWS_LAT, S5_COLS), g3),
        out_shape=jax.ShapeDtypeStruct((S5_GROUPS, S5_ROWS_LAT, S5_COLS), F32),
        scratch_shapes=[pltpu.VMEM((S5_ROWS, 4 * S5_SW), F32), pltpu.VMEM((S5_ROWS, 2 * S5_SW), F32)],
        compiler_params=_params(("parallel",)), name="s5_chunked_scan",
    )(ug, t, mb, mc, coef)


def _s5_weights(lam_re, lam_im, log_step, b_re, b_im, c_re, c_im):
    q = S5_CHUNK
    hi = lax.Precision.HIGHEST
    t_blocks, mbs, mcs, coefs = [], [], [], []
    sig = jnp.arange(q)
    for d in range(2):
        lr = jnp.minimum(lam_re[d], -1e-4)
        li = lam_im[d]
        step = jnp.exp(log_step[d])[:, None]
        jj = jnp.arange(q + 1, dtype=F32)[:, None, None]
        mag = jnp.exp(lr * step * jj)
        ph = li * step * jj
        pr, pi = mag * jnp.cos(ph), mag * jnp.sin(ph)
        nr, ni = pr[1] - 1.0, pi[1]
        den = lr * lr + li * li
        fr, fi = (nr * lr + ni * li) / den, (ni * lr - nr * li) / den
        br = fr[..., None] * b_re[d] - fi[..., None] * b_im[d]
        bi = fr[..., None] * b_im[d] + fi[..., None] * b_re[d]
        cr, ci = c_re[d], c_im[d]
        cpr = cr[None] * pr[:, :, None, :] - ci[None] * pi[:, :, None, :]
        cpi = cr[None] * pi[:, :, None, :] + ci[None] * pr[:, :, None, :]
        kern = (jnp.einsum('jghp,gpk->jghk', cpr[:q], br, precision=hi)
                - jnp.einsum('jghp,gpk->jghk', cpi[:q], bi, precision=hi))
        lag = (sig[None, :] - sig[:, None]) if d == 0 else (sig[:, None] - sig[None, :])
        kt = kern.transpose(1, 0, 3, 2)
        tb = jnp.where((lag >= 0)[None, :, :, None, None], kt[:, jnp.clip(lag, 0, q - 1)], 0.0)
        t_blocks.append(tb.transpose(0, 1, 3, 2, 4).reshape(S5_GROUPS, S5_COLS, S5_COLS))
        pw = (q - 1 - sig) if d == 0 else sig
        xr = pr[pw][..., None] * br[None] - pi[pw][..., None] * bi[None]
        xi = pr[pw][..., None] * bi[None] + pi[pw][..., None] * br[None]
        xr = xr.transpose(1, 0, 3, 2).reshape(S5_GROUPS, S5_COLS, S5_STATE)
        xi = xi.transpose(1, 0, 3, 2).reshape(S5_GROUPS, S5_COLS, S5_STATE)
        mbs += [xr, xi, xi, xr]
        po = (sig + 1) if d == 0 else (q - sig)
        mr = cpr[po].transpose(1, 3, 0, 2).reshape(S5_GROUPS, S5_STATE, S5_COLS)
        mi = -cpi[po].transpose(1, 3, 0, 2).reshape(S5_GROUPS, S5_STATE, S5_COLS)
        mcs += [mr, mi]
        are, aim = pr[q], pi[q]
        rows = [jnp.concatenate([are, are], -1), jnp.concatenate([-aim, aim], -1), jnp.concatenate([aim, -aim], -1)]
        coefs += [jnp.broadcast_to(r[:, None, :], (S5_GROUPS, 8, S5_SW)) for r in rows]
    t = (t_blocks[0] + t_blocks[1]).astype(BF16)
    mb = jnp.concatenate(mbs, axis=-1).astype(BF16)
    mc = jnp.concatenate(mcs, axis=1).astype(BF16)
    coef = jnp.concatenate(coefs, axis=1)
    return t, mb, mc, coef


A1_TL = 512


def _a1_kernel(att_ref, ys_ref, u_ref, d_ref, gw_ref, gb_ref, woa_ref, wos_ref, x_ref, g_ref, o_ref):
    y = u_ref[0] * d_ref[...] + ys_ref[0]
    z = jax.nn.gelu(y)
    gate = jax.nn.sigmoid(jnp.dot(z.astype(BF16), gw_ref[...], preferred_element_type=F32) + gb_ref[...])
    s5 = (z * gate).astype(BF16)
    mix = jnp.dot(att_ref[0], woa_ref[...], preferred_element_type=F32)
    mix = mix + jnp.dot(s5, wos_ref[...], preferred_element_type=F32)
    o_ref[0] = x_ref[0] + g_ref[0] * mix


def _a1_call(att, ys, u, d, gw, gb, woa, wos, x, g1):
    t3 = lambda b, j: (b, j, 0)
    full2 = lambda b, j: (0, 0)
    return pl.pallas_call(
        _a1_kernel, grid=(BATCH, SEQ // A1_TL),
        in_specs=[pl.BlockSpec((1, A1_TL, MLA_WIDTH), t3),
                  pl.BlockSpec((1, A1_TL, S5_WIDTH), t3),
                  pl.BlockSpec((1, A1_TL, S5_WIDTH), t3),
                  pl.BlockSpec((1, S5_WIDTH), full2),
                  pl.BlockSpec((S5_WIDTH, S5_WIDTH), full2),
                  pl.BlockSpec((1, S5_WIDTH), full2),
                  pl.BlockSpec((MLA_WIDTH, D_MODEL), full2),
                  pl.BlockSpec((S5_WIDTH, D_MODEL), full2),
                  pl.BlockSpec((1, A1_TL, D_MODEL), t3),
                  pl.BlockSpec((1, 1, D_MODEL), lambda b, j: (b, 0, 0))],
        out_specs=pl.BlockSpec((1, A1_TL, D_MODEL), t3),
        out_shape=jax.ShapeDtypeStruct((BATCH, SEQ, D_MODEL), F32),
        compiler_params=_params(("parallel", "arbitrary")), name="even_out_proj",
    )(att, ys, u, d, gw, gb, woa, wos, x, g1)


MOE_TL = 512
SLOT_PAD = 8


def _slot_columns(cols, lane):
    out = jnp.zeros(lane.shape, cols[0].dtype)
    for k, col in enumerate(cols):
        out = jnp.where(lane == k, col, out)
    return out[:, :SLOT_PAD]


def _pack_rows(v):
    halves = []
    for p in range(2):
        base = 2 * p * ROW_WORDS
        a = pltpu.bitcast(v[:, base:base + ROW_WORDS].astype(BF16).astype(F32), jnp.uint32)
        b = pltpu.bitcast(v[:, base + ROW_WORDS:base + 2 * ROW_WORDS].astype(BF16).astype(F32), jnp.uint32)
        halves.append((a >> 16) | b)
    return halves


def _unpack_rows(lo, hi):
    out = []
    for w in (lo, hi):
        out.append(pltpu.bitcast(w << 16, F32))
        out.append(pltpu.bitcast(w & jnp.uint32(0xFFFF0000), F32))
    return out


def _moe_in_kernel(x_ref, sh_ref, sc_ref, rw_ref, rb_ref, sg_ref, su_ref, sd_ref, tri_ref,
                   hlo_ref, hhi_ref, idx_ref, wt_ref, rank_ref, cnt_ref, o_ref, run_sc):
    @pl.when((pl.program_id(0) == 0) & (pl.program_id(1) == 0))
    def _():
        run_sc[...] = jnp.zeros_like(run_sc)

    h = _norm_mod(x_ref[0], sh_ref[0], sc_ref[0])
    logits = jnp.dot(h, rw_ref[...], preferred_element_type=F32, precision=lax.Precision.HIGHEST)
    scores = jax.nn.sigmoid(logits)
    hb = h.astype(BF16)
    hlo_ref[0], hhi_ref[0] = _pack_rows(h)
    hid = jax.nn.silu(jnp.dot(hb, sg_ref[...], preferred_element_type=F32))
    hid = hid * jnp.dot(hb, su_ref[...], preferred_element_type=F32)
    o_ref[0] = jnp.dot(hid.astype(BF16), sd_ref[...], preferred_element_type=F32)

    work = scores + rb_ref[...]
    lane = lax.broadcasted_iota(jnp.int32, work.shape, 1)
    hits, ids = [], []
    for _ in range(TOP_K):
        m = jnp.max(work, axis=-1, keepdims=True)
        ik = jnp.min(jnp.where(work == m, lane, N_EXPERTS), axis=-1, keepdims=True)
        hit = lane == ik
        hits.append(hit)
        ids.append(ik)
        work = jnp.where(hit, -jnp.inf, work)
    mask = hits[0]
    for hit in hits[1:]:
        mask = jnp.logical_or(mask, hit)
    maskf = mask.astype(F32)
    before = jnp.dot(tri_ref[...], maskf.astype(BF16), preferred_element_type=F32) + run_sc[...]
    sel = [jnp.sum(jnp.where(hit, scores, 0.0), axis=-1, keepdims=True) for hit in hits]
    denom = sel[0]
    for s in sel[1:]:
        denom = denom + s
    ranks = [jnp.sum(jnp.where(hit, before, 0.0), axis=-1, keepdims=True) for hit in hits]
    lane128 = lax.broadcasted_iota(jnp.int32, (MOE_TL, 128), 1)
    idx_ref[0] = _slot_columns(ids, lane128)
    wt_ref[0] = _slot_columns([s / denom * ROUTE_SCALE for s in sel], lane128)
    rank_ref[0] = _slot_columns([r.astype(jnp.int32) for r in ranks], lane128)
    run_sc[...] += jnp.sum(maskf, axis=0, keepdims=True)
    cnt_ref[...] = run_sc[...]


def _moe_in_call(x, sh, sc, rw, rb, sg, su, sd):
    t3 = lambda b, j: (b, j, 0)
    full2 = lambda b, j: (0, 0)
    per_b = lambda b, j: (b, 0, 0)
    ff = sg.shape[1]
    tri = jnp.asarray(np.tril(np.ones((MOE_TL, MOE_TL), np.float32), -1), BF16)
    slot = jax.ShapeDtypeStruct((BATCH, SEQ, SLOT_PAD), jnp.int32)
    return pl.pallas_call(
        _moe_in_kernel, grid=(BATCH, SEQ // MOE_TL),
        in_specs=[pl.BlockSpec((1, MOE_TL, D_MODEL), t3),
                  pl.BlockSpec((1, 1, D_MODEL), per_b),
                  pl.BlockSpec((1, 1, D_MODEL), per_b),
                  pl.BlockSpec((D_MODEL, N_EXPERTS), full2),
                  pl.BlockSpec((1, N_EXPERTS), full2),
                  pl.BlockSpec((D_MODEL, ff), full2),
                  pl.BlockSpec((D_MODEL, ff), full2),
                  pl.BlockSpec((ff, D_MODEL), full2),
                  pl.BlockSpec((MOE_TL, MOE_TL), full2)],
        out_specs=[pl.BlockSpec((1, MOE_TL, ROW_WORDS), t3),
                   pl.BlockSpec((1, MOE_TL, ROW_WORDS), t3),
                   pl.BlockSpec((1, MOE_TL, SLOT_PAD), t3),
                   pl.BlockSpec((1, MOE_TL, SLOT_PAD), t3),
                   pl.BlockSpec((1, MOE_TL, SLOT_PAD), t3),
                   pl.BlockSpec((1, N_EXPERTS), full2),
                   pl.BlockSpec((1, MOE_TL, D_MODEL), t3)],
        out_shape=[jax.ShapeDtypeStruct((BATCH, SEQ, ROW_WORDS), jnp.uint32),
                   jax.ShapeDtypeStruct((BATCH, SEQ, ROW_WORDS), jnp.uint32),
                   slot,
                   jax.ShapeDtypeStruct((BATCH, SEQ, SLOT_PAD), F32),
                   slot,
                   jax.ShapeDtypeStruct((1, N_EXPERTS), F32),
                   jax.ShapeDtypeStruct((BATCH, SEQ, D_MODEL), F32)],
        scratch_shapes=[pltpu.VMEM((1, N_EXPERTS), F32)],
        compiler_params=_params(("arbitrary", "arbitrary")), name="moe_router_shared",
    )(x, sh, sc, rw, rb, sg, su, sd, tri)


def _sc_mesh():
    return plsc.VectorSubcoreMesh(core_axis_name="c", subcore_axis_name="s")


def _sc_dispatch(h_words, dest, n_rows):
    n_tok = h_words.shape[0]

    @pl.kernel(out_type=jax.ShapeDtypeStruct((n_rows, ROW_WORDS), jnp.uint32), mesh=_sc_mesh(), scratch_types=[])
    def scatter_rows(h_hbm, i_hbm, o_hbm):
        def body(h_vmem, i_vmem):
            pltpu.sync_copy(h_vmem, o_hbm.at[i_vmem.at[0]])

        pltpu.emit_pipeline(
            body, grid=(TOP_K, n_tok // SC_WINDOW),
            in_specs=[pl.BlockSpec((SC_WINDOW, ROW_WORDS), index_map=lambda k, i: (i, 0)),
                      pl.BlockSpec((1, SC_WINDOW), index_map=lambda k, i: (k, i))],
            out_specs=[],
            core_axis_name=("c", "s"), dimension_semantics=(pltpu.PARALLEL, pltpu.PARALLEL),
        )(h_hbm, i_hbm)

    return scatter_rows(h_words, dest)


def _sc_collect(y_words, dest):
    n_tok = dest.shape[1]

    @pl.kernel(out_type=jax.ShapeDtypeStruct((TOP_K, n_tok, ROW_WORDS), jnp.uint32), mesh=_sc_mesh(),
               scratch_types=[])
    def gather_rows(y_hbm, i_hbm, o_hbm):
        def body(i_vmem, o_vmem):
            pltpu.sync_copy(y_hbm.at[i_vmem.at[0]], o_vmem.at[0])

        pltpu.emit_pipeline(
            body, grid=(TOP_K, n_tok // SC_WINDOW),
            in_specs=[pl.BlockSpec((1, SC_WINDOW), index_map=lambda k, i: (k, i))],
            out_specs=[pl.BlockSpec((1, SC_WINDOW, ROW_WORDS), index_map=lambda k, i: (k, i, 0))],
            core_axis_name=("c", "s"), dimension_semantics=(pltpu.PARALLEL, pltpu.PARALLEL),
        )(i_hbm, o_hbm)

    return gather_rows(y_words, dest)


def _expert_kernel(be_ref, nv_ref, xlo_ref, xhi_ref, wg_ref, wu_ref, wd_ref, ylo_ref, yhi_ref):
    del be_ref
    nv = nv_ref[pl.program_id(0)]
    parts = _unpack_rows(xlo_ref[...], xhi_ref[...])
    xb = jnp.concatenate([p.astype(BF16) for p in parts], axis=1)
    live = lax.broadcasted_iota(jnp.int32, xb.shape, 0) < nv
    xb = jnp.where(live, xb, jnp.zeros_like(xb))
    hid = jax.nn.silu(jnp.dot(xb, wg_ref[0], preferred_element_type=F32))
    hid = hid * jnp.dot(xb, wu_ref[0], preferred_element_type=F32)
    y = jnp.dot(hid.astype(BF16), wd_ref[0], preferred_element_type=F32)
    ylo_ref[...], yhi_ref[...] = _pack_rows(y)


def _expert_call(block_e, n_valid, xlo, xhi, wg, wu, wd):
    n_rows = xlo.shape[0]
    n_blocks = n_rows // MOE_BLOCK
    rows = pl.BlockSpec((MOE_BLOCK, ROW_WORDS), lambda i, be, nv: (i, 0))
    grid_spec = pltpu.PrefetchScalarGridSpec(
        num_scalar_prefetch=2, grid=(n_blocks,),
        in_specs=[rows, rows,
                  pl.BlockSpec((1, D_MODEL, EXPERT_FF), lambda i, be, nv: (be[i], 0, 0)),
                  pl.BlockSpec((1, D_MODEL, EXPERT_FF), lambda i, be, nv: (be[i], 0, 0)),
                  pl.BlockSpec((1, EXPERT_FF, D_MODEL), lambda i, be, nv: (be[i], 0, 0))],
        out_specs=[rows, rows])
    out = jax.ShapeDtypeStruct((n_rows, ROW_WORDS), jnp.uint32)
    return pl.pallas_call(
        _expert_kernel, grid_spec=grid_spec, out_shape=[out, out],
        compiler_params=_params(("arbitrary",)), name="moe_experts",
    )(block_e, n_valid, xlo, xhi, wg, wu, wd)


def _combine_kernel(ylo_ref, yhi_ref, w_ref, s_ref, x_ref, g_ref, o_ref):
    w = w_ref[0]
    acc = [None] * 4
    for k in range(TOP_K):
        wk = w[:, k:k + 1]
        for c, part in enumerate(_unpack_rows(ylo_ref[k], yhi_ref[k])):
            acc[c] = wk * part if acc[c] is None else acc[c] + wk * part
    for c in range(4):
        sl = slice(c * ROW_WORDS, (c + 1) * ROW_WORDS)
        o_ref[0, :, sl] = x_ref[0, :, sl] + g_ref[0, :, sl] * (acc[c] + s_ref[0, :, sl])


def _combine_call(ylo, yhi, wts, shared, x, g2):
    t3 = lambda b, j: (b, j, 0)
    nt = SEQ // MOE_TL
    rows = pl.BlockSpec((TOP_K, MOE_TL, ROW_WORDS), lambda b, j: (0, b * nt + j, 0))
    return pl.pallas_call(
        _combine_kernel, grid=(BATCH, nt),
        in_specs=[rows, rows,
                  pl.BlockSpec((1, MOE_TL, SLOT_PAD), t3),
                  pl.BlockSpec((1, MOE_TL, D_MODEL), t3),
                  pl.BlockSpec((1, MOE_TL, D_MODEL), t3),
                  pl.BlockSpec((1, 1, D_MODEL), lambda b, j: (b, 0, 0))],
        out_specs=pl.BlockSpec((1, MOE_TL, D_MODEL), t3),
        out_shape=jax.ShapeDtypeStruct((BATCH, SEQ, D_MODEL), F32),
        compiler_params=_params(("parallel", "arbitrary")), name="moe_combine",
    )(ylo, yhi, wts, shared, x, g2)


def _moe(x, sh, sc, g2, router_w, router_b, w_gate, w_up, w_down, sh_gate, sh_up, sh_down):
    T = BATCH * SEQ
    TK = T * TOP_K
    hlo, hhi, idx, wts, rank, counts, shared = _moe_in_call(
        x, sh, sc, router_w, router_b[None, :], sh_gate.astype(BF16), sh_up.astype(BF16), sh_down.astype(BF16))
    counts = counts[0].astype(jnp.int32)
    padded = (counts + MOE_BLOCK - 1) // MOE_BLOCK * MOE_BLOCK
    pad_end = jnp.cumsum(padded)
    pad_start = pad_end - padded
    n_blocks = -(-TK // MOE_BLOCK) + N_EXPERTS
    n_rows = n_blocks * MOE_BLOCK
    block_start = jnp.arange(n_blocks, dtype=jnp.int32) * MOE_BLOCK
    block_e = jnp.minimum(jnp.searchsorted(pad_end, block_start, side='right'), N_EXPERTS - 1).astype(jnp.int32)
    n_valid = jnp.clip(counts[block_e] - (block_start - pad_start[block_e]), 0, MOE_BLOCK).astype(jnp.int32)
    idx = idx.reshape(T, SLOT_PAD)[:, :TOP_K]
    dest = pad_start[idx] + rank.reshape(T, SLOT_PAD)[:, :TOP_K]
    dest = dest.T.astype(jnp.int32)
    xlo = _sc_dispatch(hlo.reshape(T, ROW_WORDS), dest, n_rows)
    xhi = _sc_dispatch(hhi.reshape(T, ROW_WORDS), dest, n_rows)
    ylo, yhi = _expert_call(block_e, n_valid, xlo, xhi, w_gate.astype(BF16), w_up.astype(BF16),
                            w_down.astype(BF16))
    return _combine_call(_sc_collect(ylo, dest), _sc_collect(yhi, dest), wts, shared, x, g2)


HY_TL = 512
HALO = 8


def _hy_in_kernel(x_ref, xp_ref, xn_ref, sh_ref, sc_ref, w_ref, cw_ref, cb_ref, z_ref, x0_ref, h_sc):
    j = pl.program_id(1)
    shift, scale = sh_ref[0], sc_ref[0]
    keep_prev = (j > 0).astype(F32)
    keep_next = (j < SEQ // HY_TL - 1).astype(F32)
    h_sc[0:HALO, :] = _norm_mod(xp_ref[0], shift, scale) * keep_prev
    h_sc[HALO:HALO + HY_TL, :] = _norm_mod(x_ref[0], shift, scale)
    h_sc[HALO + HY_TL:, :] = _norm_mod(xn_ref[0], shift, scale) * keep_next
    hcat = h_sc[...].astype(BF16)
    outs = []
    for part in range(3):
        sl = slice(part * HY_WIDTH, (part + 1) * HY_WIDTH)
        p = jnp.dot(hcat, w_ref[:, sl], preferred_element_type=F32)
        o = (p[HALO - 1:HALO - 1 + HY_TL] * cw_ref[0:1, sl] + p[HALO:HALO + HY_TL] * cw_ref[1:2, sl]
             + p[HALO + 1:HALO + 1 + HY_TL] * cw_ref[2:3, sl] + cb_ref[:, sl])
        outs.append(o)
    x0_ref[0] = outs[0].astype(BF16)
    z_ref[0] = (outs[2] * outs[1]).astype(BF16)


def _hy_in_call(x, sh, sc, w, cw, cb):
    nb8 = HY_TL // HALO
    t3 = lambda b, j: (b, j, 0)
    full2 = lambda b, j: (0, 0)
    per_b = lambda b, j: (b, 0, 0)
    return pl.pallas_call(
        _hy_in_kernel, grid=(BATCH, SEQ // HY_TL),
        in_specs=[pl.BlockSpec((1, HY_TL, D_MODEL), t3),
                  pl.BlockSpec((1, HALO, D_MODEL), lambda b, j: (b, jnp.maximum(j * nb8 - 1, 0), 0)),
                  pl.BlockSpec((1, HALO, D_MODEL), lambda b, j: (b, jnp.minimum((j + 1) * nb8, SEQ // HALO - 1), 0)),
                  pl.BlockSpec((1, 1, D_MODEL), per_b),
                  pl.BlockSpec((1, 1, D_MODEL), per_b),
                  pl.BlockSpec((D_MODEL, 3 * HY_WIDTH), full2),
                  pl.BlockSpec((SHORT_CONV, 3 * HY_WIDTH), full2),
                  pl.BlockSpec((1, 3 * HY_WIDTH), full2)],
        out_specs=[pl.BlockSpec((1, HY_TL, HY_WIDTH), t3), pl.BlockSpec((1, HY_TL, HY_WIDTH), t3)],
        out_shape=[jax.ShapeDtypeStruct((BATCH, SEQ, HY_WIDTH), BF16),
                   jax.ShapeDtypeStruct((BATCH, SEQ, HY_WIDTH), BF16)],
        scratch_shapes=[pltpu.VMEM((HY_TL + 2 * HALO, D_MODEL), F32)],
        compiler_params=_params(("parallel", "arbitrary")), name="hyena_in_proj",
    )(x, x, x, sh, sc, w, cw, cb)


def _dft_matrices():
    n = DFT_N
    k = np.arange(SEQ, dtype=np.int64)[:, None]
    a = np.arange(64, dtype=np.int64)[None, :]
    ang_a = 2.0 * np.pi * ((k * a * 64) % n) / n
    ang_c = 2.0 * np.pi * ((k * a) % n) / n
    ca, sa, cc, sc = [jnp.asarray(v, F32) for v in (np.cos(ang_a), np.sin(ang_a), np.cos(ang_c), np.sin(ang_c))]
    f_re = (ca[:, :, None] * cc[:, None, :] - sa[:, :, None] * sc[:, None, :]).reshape(SEQ, SEQ)
    f_im = -(sa[:, :, None] * cc[:, None, :] + ca[:, :, None] * sc[:, None, :]).reshape(SEQ, SEQ)
    alt = jnp.asarray(1.0 - 2.0 * (np.arange(SEQ) % 2), F32)
    f_im = f_im.at[0, :].set(alt)
    fwd = jnp.concatenate([f_re.reshape(DFT_NF, DFT_TF, SEQ), f_im.reshape(DFT_NF, DFT_TF, SEQ)], axis=1)
    g_re = f_re * (2.0 / n)
    g_re = g_re.at[:, 0].set(1.0 / n)
    g_im = f_im.T * (2.0 / n)
    g_im = g_im.at[:, 0].set(alt / n)
    inv = jnp.concatenate([g_re.reshape(SEQ, DFT_NF, DFT_TF), g_im.reshape(SEQ, DFT_NF, DFT_TF)], axis=2)
    inv = inv.transpose(1, 0, 2)
    return fwd.astype(BF16), inv.astype(BF16)


def _spec_kernel(f_ref, h_ref, p_ref, pn_ref, q_ref):
    spec = jnp.dot(f_ref[0], h_ref[...], preferred_element_type=F32)
    re, im = spec[:DFT_TF], spec[DFT_TF:]
    c_re = re[:, :HY_WIDTH] + re[:, HY_WIDTH:]
    c_im = im[:, :HY_WIDTH] - im[:, HY_WIDTH:]
    nyq = im[:, :HY_WIDTH] + im[:, HY_WIDTH:]
    row = lax.broadcasted_iota(jnp.int32, c_re.shape, 0) + pl.program_id(0) * DFT_TF
    first = row == 0
    p_ref[...] = c_re
    pn_ref[...] = jnp.where(first, nyq, c_re)
    q_ref[...] = jnp.where(first, 0.0, c_im)


def _spec_call(fwd, hfb):
    out = jax.ShapeDtypeStruct((SEQ, HY_WIDTH), F32)
    ospec = pl.BlockSpec((DFT_TF, HY_WIDTH), lambda f: (f, 0))
    return pl.pallas_call(
        _spec_kernel, grid=(DFT_NF,),
        in_specs=[pl.BlockSpec((1, 2 * DFT_TF, SEQ), lambda f: (f, 0, 0)),
                  pl.BlockSpec((SEQ, 2 * HY_WIDTH), lambda f: (0, 0))],
        out_specs=[ospec, ospec, ospec], out_shape=[out, out, out],
        compiler_params=_params(("arbitrary",)), name="hyena_filter_spectrum",
    )(fwd, hfb)


def _conv_kernel(z_ref, f_ref, g_ref, p_ref, pn_ref, q_ref, y_ref, acc):
    f = pl.program_id(2)
    spec = jnp.dot(f_ref[0], z_ref[0], preferred_element_type=F32)
    zr, zi = spec[:DFT_TF], spec[DFT_TF:]
    yr = zr * p_ref[...] - zi * q_ref[...]
    yi = zr * q_ref[...] + zi * pn_ref[...]
    ycat = jnp.concatenate([yr, yi], axis=0).astype(BF16)
    part = jnp.dot(g_ref[0], ycat, preferred_element_type=F32)

    @pl.when(f == 0)
    def _():
        acc[...] = part

    @pl.when(f > 0)
    def _():
        acc[...] += part

    @pl.when(f == DFT_NF - 1)
    def _():
        y_ref[0] = acc[...].astype(BF16)


def _conv_call(z, fwd, inv, p, pn, q):
    sspec = pl.BlockSpec((DFT_TF, HY_CT), lambda b, c, f: (f, c))
    return pl.pallas_call(
        _conv_kernel, grid=(BATCH, HY_WIDTH // HY_CT, DFT_NF),
        in_specs=[pl.BlockSpec((1, SEQ, HY_CT), lambda b, c, f: (b, 0, c)),
                  pl.BlockSpec((1, 2 * DFT_TF, SEQ), lambda b, c, f: (f, 0, 0)),
                  pl.BlockSpec((1, SEQ, 2 * DFT_TF), lambda b, c, f: (f, 0, 0)),
                  sspec, sspec, sspec],
        out_specs=pl.BlockSpec((1, SEQ, HY_CT), lambda b, c, f: (b, 0, c)),
        out_shape=jax.ShapeDtypeStruct((BATCH, SEQ, HY_WIDTH), BF16),
        scratch_shapes=[pltpu.VMEM((SEQ, HY_CT), F32)],
        compiler_params=_params(("parallel", "parallel", "arbitrary")), name="hyena_long_conv",
    )(z, fwd, inv, p, pn, q)


def _hy_out_kernel(y_ref, z_ref, x0_ref, b_ref, w_ref, x_ref, g_ref, o_ref):
    z = z_ref[0].astype(F32)
    gated = x0_ref[0].astype(F32) * (y_ref[0].astype(F32) + b_ref[...] * z)
    mix = jnp.dot(gated.astype(BF16), w_ref[...], preferred_element_type=F32)
    o_ref[0] = x_ref[0] + g_ref[0] * mix


def _hy_out_call(y, z, x0, bias, w, x, g1):
    t3 = lambda b, j: (b, j, 0)
    full2 = lambda b, j: (0, 0)
    return pl.pallas_call(
        _hy_out_kernel, grid=(BATCH, SEQ // HY_TL),
        in_specs=[pl.BlockSpec((1, HY_TL, HY_WIDTH), t3),
                  pl.BlockSpec((1, HY_TL, HY_WIDTH), t3),
                  pl.BlockSpec((1, HY_TL, HY_WIDTH), t3),
                  pl.BlockSpec((1, HY_WIDTH), full2),
                  pl.BlockSpec((HY_WIDTH, D_MODEL), full2),
                  pl.BlockSpec((1, HY_TL, D_MODEL), t3),
                  pl.BlockSpec((1, 1, D_MODEL), lambda b, j: (b, 0, 0))],
        out_specs=pl.BlockSpec((1, HY_TL, D_MODEL), t3),
        out_shape=jax.ShapeDtypeStruct((BATCH, SEQ, D_MODEL), F32),
        compiler_params=_params(("parallel", "arbitrary")), name="hyena_out_proj",
    )(y, z, x0, bias, w, x, g1)


def _hyena_filter(w1, b1, w2, b2, w3, freq):
    hi = lax.Precision.HIGHEST
    Lq = SEQ
    t = jnp.linspace(0.0, 1.0, Lq, dtype=F32)[:, None]
    ang = 2.0 * math.pi * jnp.arange(Lq, dtype=F32)[:, None] / Lq
    bands = jnp.linspace(1e-4, FILT_BANDS - 1, FILT_BANDS, dtype=F32)
    z = jnp.concatenate([t, jnp.cos(bands * ang), -jnp.sin(bands * ang)], axis=-1)
    hid = jnp.sin(freq * (jnp.dot(z, w1, precision=hi) + b1))
    hid = jnp.sin(freq * (jnp.dot(hid, w2, precision=hi) + b2))
    hf = jnp.dot(hid, w3, precision=hi).reshape(Lq, 2, HY_WIDTH)
    deltas = jnp.linspace(HY_MIN_DECAY, HY_MAX_DECAY, HY_WIDTH, dtype=F32)
    hf = hf * jnp.exp(-t * deltas)[:, None, :]
    hf = hf * lax.rsqrt(jnp.sum(hf * hf, axis=(0, 1), keepdims=True) + EPS)
    return hf.reshape(Lq, 2 * HY_WIDTH)


def kernel(x, c, ctx, c_ctx, ada_w, ada_b, ev_w_in, mla_q_norm, mla_w_uq, mla_kv_norm, mla_w_ukv, mla_q_qknorm, mla_k_qknorm, s5_lam_re, s5_lam_im, s5_log_step, s5_b_re, s5_b_im, s5_c_re, s5_c_im, s5_d, s5_glu_w, s5_glu_b, ev_w_out, hy_w_in, hy_conv_w, hy_conv_b, hy_f_w1, hy_f_b1, hy_f_w2, hy_f_b2, hy_f_w3, hy_f_freq, hy_bias, hy_w_out, moe_router_w, moe_router_b, moe_w_gate, moe_w_up, moe_w_down, moe_sh_gate, moe_sh_up, moe_sh_down):
    hi = lax.Precision.HIGHEST
    D = D_MODEL
    sc = jax.nn.silu(c)
    sc_ctx = jax.nn.silu(c_ctx)

    def mods(li):
        mod = jnp.dot(sc, ada_w[li], precision=hi) + ada_b[li]
        return [m[:, None, :] for m in jnp.split(mod, 6, axis=-1)]

    sh1, sc1, g1, sh2, sc2, g2 = mods(0)
    mod_ctx = jnp.dot(sc_ctx, ada_w[0][:, :2 * D], precision=hi) + ada_b[0][:2 * D]
    w0 = _a0_weights(ev_w_in[0], mla_q_norm[0], mla_w_uq[0], mla_kv_norm[0], mla_w_ukv[0],
                     mla_q_qknorm[0], mla_k_qknorm[0])
    q, k, v, u = _a0_call(x, ctx, sh1, sc1, mod_ctx[None, :D], mod_ctx[None, D:], w0)
    att = _attn_call(q, k, v)
    ug = u.astype(BF16).reshape(BATCH, S5_NCHUNK, S5_CHUNK, S5_GROUPS, S5_GROUP)
    ug = ug.transpose(3, 1, 0, 2, 4).reshape(S5_GROUPS, S5_ROWS, S5_COLS)
    ys = _s5_call(ug, *_s5_weights(s5_lam_re[0], s5_lam_im[0], s5_log_step[0], s5_b_re[0], s5_b_im[0],
                                   s5_c_re[0], s5_c_im[0]))
    ys = ys.reshape(S5_GROUPS, S5_NCHUNK_LAT, BATCH, S5_CHUNK, S5_GROUP)
    ys = ys.transpose(2, 1, 3, 0, 4).reshape(BATCH, SEQ, S5_WIDTH)
    wo = ev_w_out[0].astype(BF16)
    x = _a1_call(att, ys, u, s5_d[0][None, :], s5_glu_w[0].astype(BF16), s5_glu_b[0][None, :],
                 wo[:MLA_WIDTH], wo[MLA_WIDTH:], x, g1)
    x = _moe(x, sh2, sc2, g2, moe_router_w[0], moe_router_b[0], moe_w_gate[0], moe_w_up[0], moe_w_down[0],
             moe_sh_gate[0], moe_sh_up[0], moe_sh_down[0])

    sh1, sc1, g1, sh2, sc2, g2 = mods(1)
    z, x0 = _hy_in_call(x, sh1, sc1, hy_w_in[0].astype(BF16), hy_conv_w[0], hy_conv_b[0][None, :])
    fwd, inv = _dft_matrices()
    hfb = _hyena_filter(hy_f_w1[0], hy_f_b1[0], hy_f_w2[0], hy_f_b2[0], hy_f_w3[0], hy_f_freq[0])
    p, pn, qq = _spec_call(fwd, hfb.astype(BF16))
    y = _conv_call(z, fwd, inv, p, pn, qq)
    x = _hy_out_call(y, z, x0, hy_bias[0][None, :], hy_w_out[0].astype(BF16), x, g1)
    x = _moe(x, sh2, sc2, g2, moe_router_w[1], moe_router_b[1], moe_w_gate[1], moe_w_up[1], moe_w_down[1],
             moe_sh_gate[1], moe_sh_up[1], moe_sh_down[1])
    return x
```

```python
import functools
import math

import numpy as np
import jax
import jax.numpy as jnp
from jax import lax
from jax.experimental import pallas as pl
from jax.experimental.pallas import tpu as pltpu
from jax.experimental.pallas import tpu_sc as plsc

F32 = jnp.float32
BF16 = jnp.bfloat16

D_MODEL = 1024
BATCH = 8
SEQ = 4096
CTX_LEN = 256
KV_LEN = SEQ + CTX_LEN
GRID_W = 64
EPS = 1e-6

MLA_HEADS = 8
QK_NOPE = 64
QK_ROPE = 32
QK_HEAD = QK_NOPE + QK_ROPE
V_HEAD = 64
Q_LORA = 256
KV_LORA = 128
MLA_WIDTH = MLA_HEADS * V_HEAD
ROPE_BASE = 10000.0
HEAD_PAD = 128

S5_WIDTH = 512
S5_GROUP = 16
S5_GROUPS = S5_WIDTH // S5_GROUP
S5_STATE = 64
S5_CHUNK = 32
S5_NCHUNK = KV_LEN // S5_CHUNK
S5_NCHUNK_LAT = SEQ // S5_CHUNK
S5_NCHUNK_CTX = CTX_LEN // S5_CHUNK

HY_WIDTH = D_MODEL
FILT_EMB = 33
FILT_BANDS = (FILT_EMB - 1) // 2
SHORT_CONV = 3
HY_MIN_DECAY = -math.log(1e-2) / 1.5
HY_MAX_DECAY = -math.log(1e-2) / 0.3
DFT_N = 2 * SEQ
DFT_TF = 256
DFT_NF = SEQ // DFT_TF
HY_CT = 512

N_EXPERTS = 64
TOP_K = 6
EXPERT_FF = 256
ROUTE_SCALE = 2.5
MOE_BLOCK = 512
ROW_WORDS = D_MODEL // 4
SC_WINDOW = 128

V7X_VMEM_BYTES = 64 * 1024 * 1024
VMEM_LIMIT = V7X_VMEM_BYTES - 8 * 1024 * 1024


def _params(semantics):
    return pltpu.CompilerParams(dimension_semantics=semantics, vmem_limit_bytes=VMEM_LIMIT)


def _norm_mod(x, shift, scale):
    ms = jnp.mean(x * x, axis=-1, keepdims=True)
    return x * lax.rsqrt(ms + EPS) * (1.0 + scale) + shift


def _rms(x, gain, n):
    ms = jnp.sum(x * x, axis=-1, keepdims=True) * (1.0 / n)
    return x * lax.rsqrt(ms + EPS) * gain


A0_TL = 256
A0_NT = SEQ // A0_TL


def _rope_perm():
    return np.concatenate([np.arange(0, QK_ROPE, 2), np.arange(1, QK_ROPE, 2)])


def _rope_tables():
    t = np.arange(SEQ)
    row = (t // GRID_W).astype(np.float64)
    col = (t % GRID_W).astype(np.float64)
    n_freq = QK_ROPE // 4
    inv = ROPE_BASE ** (-np.arange(n_freq, dtype=np.float64) / n_freq)
    ang = np.concatenate([row[:, None] * inv, col[:, None] * inv], axis=-1)
    cos, sin = np.cos(ang), np.sin(ang)
    half = QK_ROPE // 2
    a = np.zeros((KV_LEN, HEAD_PAD))
    bp = np.zeros((KV_LEN, HEAD_PAD))
    bm = np.zeros((KV_LEN, HEAD_PAD))
    a[:, :QK_HEAD] = 1.0
    a[:SEQ, QK_NOPE:QK_NOPE + half] = cos
    a[:SEQ, QK_NOPE + half:QK_HEAD] = cos
    bp[:SEQ, QK_NOPE + half:QK_HEAD] = sin
    bm[:SEQ, QK_NOPE:QK_NOPE + half] = -sin
    return a, bp, bm


def _a0_kernel(x_ref, ctx_ref, sh_ref, sc_ref, shc_ref, scc_ref, win_ref, qn_ref, wuq_ref, kvn_ref,
               wk_ref, wuv_ref, qg_ref, kg_ref, ka_ref, kp_ref, km_ref, qa_ref, qp_ref, qm_ref,
               q_ref, k_ref, v_ref, u_ref):
    j = pl.program_id(1)
    is_ctx = j == A0_NT
    xin = jnp.where(is_ctx, ctx_ref[0], x_ref[0])
    shift = jnp.where(is_ctx, shc_ref[...], sh_ref[0])
    scale = jnp.where(is_ctx, scc_ref[...], sc_ref[0])
    h = _norm_mod(xin, shift, scale).astype(BF16)
    proj = jnp.dot(h, win_ref[...], preferred_element_type=F32)
    u_ref[0] = proj[:, 512:]

    c_kv = _rms(proj[:, Q_LORA:Q_LORA + KV_LORA], kvn_ref[...], KV_LORA).astype(BF16)
    v_ref[0] = jnp.dot(c_kv, wuv_ref[...], preferred_element_type=F32).astype(BF16)
    kin = jnp.concatenate([c_kv, proj[:, 384:512].astype(BF16)], axis=1)
    kf = jnp.dot(kin, wk_ref[...], preferred_element_type=F32)
    ka, kp, km = ka_ref[...], kp_ref[...], km_ref[...]
    for hd in range(MLA_HEADS):
        sl = slice(hd * HEAD_PAD, (hd + 1) * HEAD_PAD)
        kh = _rms(kf[:, sl], kg_ref[:, sl], QK_HEAD)
        kh = kh * ka + pltpu.roll(kh, 16, 1) * kp + pltpu.roll(kh, HEAD_PAD - 16, 1) * km
        k_ref[0, :, sl] = kh.astype(BF16)

    @pl.when(j < A0_NT)
    def _():
        ql = _rms(proj[:, :Q_LORA], qn_ref[...], Q_LORA).astype(BF16)
        qf = jnp.dot(ql, wuq_ref[...], preferred_element_type=F32)
        qa, qp, qm = qa_ref[...], qp_ref[...], qm_ref[...]
        for hd in range(MLA_HEADS):
            sl = slice(hd * HEAD_PAD, (hd + 1) * HEAD_PAD)
            qh = _rms(qf[:, sl], qg_ref[:, sl], QK_HEAD)
            qh = qh * qa + pltpu.roll(qh, 16, 1) * qp + pltpu.roll(qh, HEAD_PAD - 16, 1) * qm
            q_ref[0, :, sl] = qh.astype(BF16)


def _a0_call(x, ctx, sh, sc, shc, scc, w):
    nt = A0_NT
    lat = lambda b, j: (b, jnp.minimum(j, nt - 1), 0)
    full2 = lambda b, j: (0, 0)
    per_b = lambda b, j: (b, 0, 0)
    tab = pl.BlockSpec((A0_TL, HEAD_PAD), lambda b, j: (j, 0))
    in_specs = [
        pl.BlockSpec((1, A0_TL, D_MODEL), lat),
        pl.BlockSpec((1, CTX_LEN, D_MODEL), per_b),
        pl.BlockSpec((1, 1, D_MODEL), per_b),
        pl.BlockSpec((1, 1, D_MODEL), per_b),
        pl.BlockSpec((1, D_MODEL), full2),
        pl.BlockSpec((1, D_MODEL), full2),
        pl.BlockSpec((D_MODEL, 1024), full2),
        pl.BlockSpec((1, Q_LORA), full2),
        pl.BlockSpec((Q_LORA, MLA_HEADS * HEAD_PAD), full2),
        pl.BlockSpec((1, KV_LORA), full2),
        pl.BlockSpec((2 * KV_LORA, MLA_HEADS * HEAD_PAD), full2),
        pl.BlockSpec((KV_LORA, MLA_WIDTH), full2),
        pl.BlockSpec((1, MLA_HEADS * HEAD_PAD), full2),
        pl.BlockSpec((1, MLA_HEADS * HEAD_PAD), full2),
        tab, tab, tab, tab, tab, tab,
    ]
    out_specs = [
        pl.BlockSpec((1, A0_TL, MLA_HEADS * HEAD_PAD), lat),
        pl.BlockSpec((1, A0_TL, MLA_HEADS * HEAD_PAD), lambda b, j: (b, j, 0)),
        pl.BlockSpec((1, A0_TL, MLA_WIDTH), lambda b, j: (b, j, 0)),
        pl.BlockSpec((1, A0_TL, S5_WIDTH), lambda b, j: (b, j, 0)),
    ]
    out_shape = [
        jax.ShapeDtypeStruct((BATCH, SEQ, MLA_HEADS * HEAD_PAD), BF16),
        jax.ShapeDtypeStruct((BATCH, KV_LEN, MLA_HEADS * HEAD_PAD), BF16),
        jax.ShapeDtypeStruct((BATCH, KV_LEN, MLA_WIDTH), BF16),
        jax.ShapeDtypeStruct((BATCH, KV_LEN, S5_WIDTH), F32),
    ]
    return pl.pallas_call(
        _a0_kernel, grid=(BATCH, nt + 1), in_specs=in_specs, out_specs=out_specs, out_shape=out_shape,
        compiler_params=_params(("parallel", "arbitrary")), name="even_in_proj",
    )(x, ctx, sh, sc, shc, scc, *w)


def _a0_weights(w_in, q_norm, w_uq, kv_norm, w_ukv, q_qk, k_qk):
    perm = _rope_perm()
    kr0 = Q_LORA + KV_LORA
    w_cat = jnp.concatenate([
        w_in[:, :kr0], w_in[:, kr0:kr0 + QK_ROPE][:, perm],
        jnp.zeros((D_MODEL, HEAD_PAD - QK_ROPE), F32), w_in[:, kr0 + QK_ROPE:]], axis=1).astype(BF16)
    pad = HEAD_PAD - QK_HEAD

    def head_gain(g):
        gh = jnp.concatenate([g[:QK_NOPE], g[QK_NOPE:][perm], jnp.zeros((pad,), F32)])
        return jnp.tile(gh, MLA_HEADS)[None, :]

    uq = w_uq.reshape(Q_LORA, MLA_HEADS, QK_HEAD)
    uq = jnp.concatenate([uq[..., :QK_NOPE], uq[..., QK_NOPE:][..., perm],
                          jnp.zeros((Q_LORA, MLA_HEADS, pad), F32)], axis=-1)
    uq = uq.reshape(Q_LORA, MLA_HEADS * HEAD_PAD).astype(BF16)
    ukv = w_ukv.reshape(KV_LORA, MLA_HEADS, QK_NOPE + V_HEAD)
    uk = jnp.concatenate([ukv[..., :QK_NOPE], jnp.zeros((KV_LORA, MLA_HEADS, HEAD_PAD - QK_NOPE), F32)], axis=-1)
    uk = uk.reshape(KV_LORA, MLA_HEADS * HEAD_PAD)
    place = np.zeros((KV_LORA, MLA_HEADS, HEAD_PAD), np.float32)
    for i in range(QK_ROPE):
        place[i, :, QK_NOPE + i] = 1.0
    wk = jnp.concatenate([uk, jnp.asarray(place.reshape(KV_LORA, MLA_HEADS * HEAD_PAD))], axis=0).astype(BF16)
    wuv = ukv[..., QK_NOPE:].reshape(KV_LORA, MLA_WIDTH).astype(BF16)
    a, bp, bm = _rope_tables()
    qs = QK_HEAD ** -0.5
    tabs = [jnp.asarray(t, F32) for t in (a, bp, bm, a * qs, bp * qs, bm * qs)]
    return [w_cat, q_norm[None, :], uq, kv_norm[None, :], wk, wuv, head_gain(q_qk), head_gain(k_qk)] + tabs


ATT_TQ = 256
HEADS_PER_STEP = 2


def _attn_kernel(q_ref, k_ref, v_ref, o_ref):
    v = v_ref[0]
    lane_head = lax.broadcasted_iota(jnp.int32, v.shape, 1) // V_HEAD
    acc = None
    for hh in range(HEADS_PER_STEP):
        sl = slice(hh * HEAD_PAD, (hh + 1) * HEAD_PAD)
        s = lax.dot_general(q_ref[0, :, sl], k_ref[0, :, sl], (((1,), (1,)), ((), ())),
                            preferred_element_type=F32)
        m = jnp.max(s, axis=-1, keepdims=True)
        p = jnp.exp(s - m)
        l = jnp.sum(p, axis=-1, keepdims=True)
        vh = jnp.where(lane_head == hh, v, jnp.zeros_like(v))
        o = jnp.dot(p.astype(BF16), vh, preferred_element_type=F32) * (1.0 / l)
        acc = o if acc is None else acc + o
    o_ref[0] = acc.astype(BF16)


def _attn_call(q, k, v):
    wq = HEADS_PER_STEP * HEAD_PAD
    wv = HEADS_PER_STEP * V_HEAD
    return pl.pallas_call(
        _attn_kernel, grid=(BATCH, MLA_HEADS // HEADS_PER_STEP, SEQ // ATT_TQ),
        in_specs=[pl.BlockSpec((1, ATT_TQ, wq), lambda b, h, i: (b, i, h)),
                  pl.BlockSpec((1, KV_LEN, wq), lambda b, h, i: (b, 0, h)),
                  pl.BlockSpec((1, KV_LEN, wv), lambda b, h, i: (b, 0, h))],
        out_specs=pl.BlockSpec((1, ATT_TQ, wv), lambda b, h, i: (b, i, h)),
        out_shape=jax.ShapeDtypeStruct((BATCH, SEQ, MLA_WIDTH), BF16),
        compiler_params=_params(("parallel", "parallel", "arbitrary")), name="mla_attention",
    )(q, k, v)


S5_ROWS = S5_NCHUNK * BATCH
S5_ROWS_LAT = S5_NCHUNK_LAT * BATCH
S5_COLS = S5_CHUNK * S5_GROUP
S5_SW = 2 * S5_STATE


def _s5_kernel(u_ref, t_ref, mb_ref, mc_ref, coef_ref, y_ref, x_sc, sp_sc):
    u = u_ref[0]
    x_sc[...] = jnp.dot(u, mb_ref[0], preferred_element_type=F32)
    cf = coef_ref[0]
    af, bfm, bfp, ab, bbm, bbp = [cf[i * 8:(i + 1) * 8] for i in range(6)]

    def body(i, carry):
        sf, sfw, sb, sbw = carry
        cfw = jnp.where(i < S5_NCHUNK_CTX, i + S5_NCHUNK_LAT, i - S5_NCHUNK_CTX)
        rf = pl.multiple_of(cfw * BATCH, BATCH)
        rb = pl.multiple_of((S5_NCHUNK - 1 - i) * BATCH, BATCH)
        sp_sc[pl.ds(rf, BATCH), 0:S5_SW] = sf
        sp_sc[pl.ds(rb, BATCH), S5_SW:2 * S5_SW] = sb
        xf = x_sc[pl.ds(rf, BATCH), 0:S5_SW]
        xfw = x_sc[pl.ds(rf, BATCH), S5_SW:2 * S5_SW]
        xb = x_sc[pl.ds(rb, BATCH), 2 * S5_SW:3 * S5_SW]
        xbw = x_sc[pl.ds(rb, BATCH), 3 * S5_SW:4 * S5_SW]
        return (sf * af + sfw * bfm + xf, sfw * af + sf * bfp + xfw,
                sb * ab + sbw * bbm + xb, sbw * ab + sb * bbp + xbw)

    z = jnp.zeros((BATCH, S5_SW), F32)
    lax.fori_loop(0, S5_NCHUNK, body, (z, z, z, z))
    y = jnp.dot(u[:S5_ROWS_LAT], t_ref[0], preferred_element_type=F32)
    y = y + jnp.dot(sp_sc[0:S5_ROWS_LAT, :].astype(BF16), mc_ref[0], preferred_element_type=F32)
    y_ref[0] = y


def _s5_call(ug, t, mb, mc, coef):
    g3 = lambda g: (g, 0, 0)
    return pl.pallas_call(
        _s5_kernel, grid=(S5_GROUPS,),
        in_specs=[pl.BlockSpec((1, S5_ROWS, S5_COLS), g3),
                  pl.BlockSpec((1, S5_COLS, S5_COLS), g3),
                  pl.BlockSpec((1, S5_COLS, 4 * S5_SW), g3),
                  pl.BlockSpec((1, 2 * S5_SW, S5_COLS), g3),
                  pl.BlockSpec((1, 6 * 8, S5_SW), g3)],
        out_specs=pl.BlockSpec((1, S5_ROWS_LAT, S5_COLS), g3),
        out_shape=jax.ShapeDtypeStruct((S5_GROUPS, S5_ROWS_LAT, S5_COLS), F32),
        scratch_shapes=[pltpu.VMEM((S5_ROWS, 4 * S5_SW), F32), pltpu.VMEM((S5_ROWS, 2 * S5_SW), F32)],
        compiler_params=_params(("parallel",)), name="s5_chunked_scan",
    )(ug, t, mb, mc, coef)


def _s5_weights(lam_re, lam_im, log_step, b_re, b_im, c_re, c_im):
    q = S5_CHUNK
    hi = lax.Precision.HIGHEST
    t_blocks, mbs, mcs, coefs = [], [], [], []
    sig = jnp.arange(q)
    for d in range(2):
        lr = jnp.minimum(lam_re[d], -1e-4)
        li = lam_im[d]
        step = jnp.exp(log_step[d])[:, None]
        jj = jnp.arange(q + 1, dtype=F32)[:, None, None]
        mag = jnp.exp(lr * step * jj)
        ph = li * step * jj
        pr, pi = mag * jnp.cos(ph), mag * jnp.sin(ph)
        nr, ni = pr[1] - 1.0, pi[1]
        den = lr * lr + li * li
        fr, fi = (nr * lr + ni * li) / den, (ni * lr - nr * li) / den
        br = fr[..., None] * b_re[d] - fi[..., None] * b_im[d]
        bi = fr[..., None] * b_im[d] + fi[..., None] * b_re[d]
        cr, ci = c_re[d], c_im[d]
        cpr = cr[None] * pr[:, :, None, :] - ci[None] * pi[:, :, None, :]
        cpi = cr[None] * pi[:, :, None, :] + ci[None] * pr[:, :, None, :]
        kern = (jnp.einsum('jghp,gpk->jghk', cpr[:q], br, precision=hi)
                - jnp.einsum('jghp,gpk->jghk', cpi[:q], bi, precision=hi))
        lag = (sig[None, :] - sig[:, None]) if d == 0 else (sig[:, None] - sig[None, :])
        kt = kern.transpose(1, 0, 3, 2)
        tb = jnp.where((lag >= 0)[None, :, :, None, None], kt[:, jnp.clip(lag, 0, q - 1)], 0.0)
        t_blocks.append(tb.transpose(0, 1, 3, 2, 4).reshape(S5_GROUPS, S5_COLS, S5_COLS))
        pw = (q - 1 - sig) if d == 0 else sig
        xr = pr[pw][..., None] * br[None] - pi[pw][..., None] * bi[None]
        xi = pr[pw][..., None] * bi[None] + pi[pw][..., None] * br[None]
        xr = xr.transpose(1, 0, 3, 2).reshape(S5_GROUPS, S5_COLS, S5_STATE)
        xi = xi.transpose(1, 0, 3, 2).reshape(S5_GROUPS, S5_COLS, S5_STATE)
        mbs += [xr, xi, xi, xr]
        po = (sig + 1) if d == 0 else (q - sig)
        mr = cpr[po].transpose(1, 3, 0, 2).reshape(S5_GROUPS, S5_STATE, S5_COLS)
        mi = -cpi[po].transpose(1, 3, 0, 2).reshape(S5_GROUPS, S5_STATE, S5_COLS)
        mcs += [mr, mi]
        are, aim = pr[q], pi[q]
        rows = [jnp.concatenate([are, are], -1), jnp.concatenate([-aim, aim], -1), jnp.concatenate([aim, -aim], -1)]
        coefs += [jnp.broadcast_to(r[:, None, :], (S5_GROUPS, 8, S5_SW)) for r in rows]
    t = (t_blocks[0] + t_blocks[1]).astype(BF16)
    mb = jnp.concatenate(mbs, axis=-1).astype(BF16)
    mc = jnp.concatenate(mcs, axis=1).astype(BF16)
    coef = jnp.concatenate(coefs, axis=1)
    return t, mb, mc, coef


A1_TL = 512


def _a1_kernel(att_ref, ys_ref, u_ref, d_ref, gw_ref, gb_ref, woa_ref, wos_ref, x_ref, g_ref, o_ref):
    y = u_ref[0] * d_ref[...] + ys_ref[0]
    z = jax.nn.gelu(y)
    gate = jax.nn.sigmoid(jnp.dot(z.astype(BF16), gw_ref[...], preferred_element_type=F32) + gb_ref[...])
    s5 = (z * gate).astype(BF16)
    mix = jnp.dot(att_ref[0], woa_ref[...], preferred_element_type=F32)
    mix = mix + jnp.dot(s5, wos_ref[...], preferred_element_type=F32)
    o_ref[0] = x_ref[0] + g_ref[0] * mix


def _a1_call(att, ys, u, d, gw, gb, woa, wos, x, g1):
    t3 = lambda b, j: (b, j, 0)
    full2 = lambda b, j: (0, 0)
    return pl.pallas_call(
        _a1_kernel, grid=(BATCH, SEQ // A1_TL),
        in_specs=[pl.BlockSpec((1, A1_TL, MLA_WIDTH), t3),
                  pl.BlockSpec((1, A1_TL, S5_WIDTH), t3),
                  pl.BlockSpec((1, A1_TL, S5_WIDTH), t3),
                  pl.BlockSpec((1, S5_WIDTH), full2),
                  pl.BlockSpec((S5_WIDTH, S5_WIDTH), full2),
                  pl.BlockSpec((1, S5_WIDTH), full2),
                  pl.BlockSpec((MLA_WIDTH, D_MODEL), full2),
                  pl.BlockSpec((S5_WIDTH, D_MODEL), full2),
                  pl.BlockSpec((1, A1_TL, D_MODEL), t3),
                  pl.BlockSpec((1, 1, D_MODEL), lambda b, j: (b, 0, 0))],
        out_specs=pl.BlockSpec((1, A1_TL, D_MODEL), t3),
        out_shape=jax.ShapeDtypeStruct((BATCH, SEQ, D_MODEL), F32),
        compiler_params=_params(("parallel", "arbitrary")), name="even_out_proj",
    )(att, ys, u, d, gw, gb, woa, wos, x, g1)


MOE_TL = 512
SLOT_PAD = 8


def _slot_columns(cols, lane):
    out = jnp.zeros(lane.shape, cols[0].dtype)
    for k, col in enumerate(cols):
        out = jnp.where(lane == k, col, out)
    return out[:, :SLOT_PAD]


def _pack_rows(v):
    halves = []
    for p in range(2):
        base = 2 * p * ROW_WORDS
        a = pltpu.bitcast(v[:, base:base + ROW_WORDS].astype(BF16).astype(F32), jnp.uint32)
        b = pltpu.bitcast(v[:, base + ROW_WORDS:base + 2 * ROW_WORDS].astype(BF16).astype(F32), jnp.uint32)
        halves.append((a >> 16) | b)
    return halves


def _unpack_rows(lo, hi):
    out = []
    for w in (lo, hi):
        out.append(pltpu.bitcast(w << 16, F32))
        out.append(pltpu.bitcast(w & jnp.uint32(0xFFFF0000), F32))
    return out


def _moe_in_kernel(x_ref, sh_ref, sc_ref, rw_ref, rb_ref, sg_ref, su_ref, sd_ref, tri_ref,
                   hlo_ref, hhi_ref, idx_ref, wt_ref, rank_ref, cnt_ref, o_ref, run_sc):
    @pl.when((pl.program_id(0) == 0) & (pl.program_id(1) == 0))
    def _():
        run_sc[...] = jnp.zeros_like(run_sc)

    h = _norm_mod(x_ref[0], sh_ref[0], sc_ref[0])
    hb = h.astype(BF16)
    h_lo = (h - hb.astype(F32)).astype(BF16)
    rw = rw_ref[...]
    rw_hi = rw.astype(BF16)
    rw_lo = (rw - rw_hi.astype(F32)).astype(BF16)
    logits = jnp.dot(jnp.concatenate([hb, hb, h_lo], axis=1), jnp.concatenate([rw_hi, rw_lo, rw_hi], axis=0),
                     preferred_element_type=F32)
    scores = jax.nn.sigmoid(logits)
    hlo_ref[0], hhi_ref[0] = _pack_rows(h)
    hid = jax.nn.silu(jnp.dot(hb, sg_ref[...].astype(BF16), preferred_element_type=F32))
    hid = hid * jnp.dot(hb, su_ref[...].astype(BF16), preferred_element_type=F32)
    o_ref[0] = jnp.dot(hid.astype(BF16), sd_ref[...].astype(BF16), preferred_element_type=F32)

    work = scores + rb_ref[...]
    lane = lax.broadcasted_iota(jnp.int32, work.shape, 1)
    hits, ids = [], []
    for _ in range(TOP_K):
        m = jnp.max(work, axis=-1, keepdims=True)
        ik = jnp.min(jnp.where(work == m, lane, N_EXPERTS), axis=-1, keepdims=True)
        hit = lane == ik
        hits.append(hit)
        ids.append(ik)
        work = jnp.where(hit, -jnp.inf, work)
    mask = hits[0]
    for hit in hits[1:]:
        mask = jnp.logical_or(mask, hit)
    maskf = mask.astype(F32)
    before = jnp.dot(tri_ref[...], maskf.astype(BF16), preferred_element_type=F32) + run_sc[...]
    sel = [jnp.sum(jnp.where(hit, scores, 0.0), axis=-1, keepdims=True) for hit in hits]
    denom = sel[0]
    for s in sel[1:]:
        denom = denom + s
    ranks = [jnp.sum(jnp.where(hit, before, 0.0), axis=-1, keepdims=True) for hit in hits]
    lane128 = lax.broadcasted_iota(jnp.int32, (MOE_TL, 128), 1)
    idx_ref[0] = _slot_columns(ids, lane128)
    wt_ref[0] = _slot_columns([s / denom * ROUTE_SCALE for s in sel], lane128)
    rank_ref[0] = _slot_columns([r.astype(jnp.int32) for r in ranks], lane128)
    run_sc[...] += jnp.sum(maskf, axis=0, keepdims=True)
    cnt_ref[...] = run_sc[...]


def _moe_in_call(x, sh, sc, rw, rb, sg, su, sd):
    t3 = lambda b, j: (b, j, 0)
    full2 = lambda b, j: (0, 0)
    per_b = lambda b, j: (b, 0, 0)
    ff = sg.shape[1]
    tri = jnp.asarray(np.tril(np.ones((MOE_TL, MOE_TL), np.float32), -1), BF16)
    slot = jax.ShapeDtypeStruct((BATCH, SEQ, SLOT_PAD), jnp.int32)
    return pl.pallas_call(
        _moe_in_kernel, grid=(BATCH, SEQ // MOE_TL),
        in_specs=[pl.BlockSpec((1, MOE_TL, D_MODEL), t3),
                  pl.BlockSpec((1, 1, D_MODEL), per_b),
                  pl.BlockSpec((1, 1, D_MODEL), per_b),
                  pl.BlockSpec((D_MODEL, N_EXPERTS), full2),
                  pl.BlockSpec((1, N_EXPERTS), full2),
                  pl.BlockSpec((D_MODEL, ff), full2),
                  pl.BlockSpec((D_MODEL, ff), full2),
                  pl.BlockSpec((ff, D_MODEL), full2),
                  pl.BlockSpec((MOE_TL, MOE_TL), full2)],
        out_specs=[pl.BlockSpec((1, MOE_TL, ROW_WORDS), t3),
                   pl.BlockSpec((1, MOE_TL, ROW_WORDS), t3),
                   pl.BlockSpec((1, MOE_TL, SLOT_PAD), t3),
                   pl.BlockSpec((1, MOE_TL, SLOT_PAD), t3),
                   pl.BlockSpec((1, MOE_TL, SLOT_PAD), t3),
                   pl.BlockSpec((1, N_EXPERTS), full2),
                   pl.BlockSpec((1, MOE_TL, D_MODEL), t3)],
        out_shape=[jax.ShapeDtypeStruct((BATCH, SEQ, ROW_WORDS), jnp.uint32),
                   jax.ShapeDtypeStruct((BATCH, SEQ, ROW_WORDS), jnp.uint32),
                   slot,
                   jax.ShapeDtypeStruct((BATCH, SEQ, SLOT_PAD), F32),
                   slot,
                   jax.ShapeDtypeStruct((1, N_EXPERTS), F32),
                   jax.ShapeDtypeStruct((BATCH, SEQ, D_MODEL), F32)],
        scratch_shapes=[pltpu.VMEM((1, N_EXPERTS), F32)],
        compiler_params=_params(("arbitrary", "arbitrary")), name="moe_router_shared",
    )(x, sh, sc, rw, rb, sg, su, sd, tri)


def _sc_mesh():
    return plsc.VectorSubcoreMesh(core_axis_name="c", subcore_axis_name="s")


def _sc_dispatch(h_words, dest, n_rows):
    n_tok = h_words.shape[0]

    @pl.kernel(out_type=jax.ShapeDtypeStruct((n_rows, ROW_WORDS), jnp.uint32), mesh=_sc_mesh(), scratch_types=[])
    def scatter_rows(h_hbm, i_hbm, o_hbm):
        def body(h_vmem, i_vmem):
            pltpu.sync_copy(h_vmem, o_hbm.at[i_vmem.at[0]])

        pltpu.emit_pipeline(
            body, grid=(TOP_K, n_tok // SC_WINDOW),
            in_specs=[pl.BlockSpec((SC_WINDOW, ROW_WORDS), index_map=lambda k, i: (i, 0)),
                      pl.BlockSpec((1, SC_WINDOW), index_map=lambda k, i: (k, i))],
            out_specs=[],
            core_axis_name=("c", "s"), dimension_semantics=(pltpu.PARALLEL, pltpu.PARALLEL),
        )(h_hbm, i_hbm)

    return scatter_rows(h_words, dest)


def _sc_collect(y_words, dest):
    n_tok = dest.shape[1]

    @pl.kernel(out_type=jax.ShapeDtypeStruct((TOP_K, n_tok, ROW_WORDS), jnp.uint32), mesh=_sc_mesh(),
               scratch_types=[])
    def gather_rows(y_hbm, i_hbm, o_hbm):
        def body(i_vmem, o_vmem):
            pltpu.sync_copy(y_hbm.at[i_vmem.at[0]], o_vmem.at[0])

        pltpu.emit_pipeline(
            body, grid=(TOP_K, n_tok // SC_WINDOW),
            in_specs=[pl.BlockSpec((1, SC_WINDOW), index_map=lambda k, i: (k, i))],
            out_specs=[pl.BlockSpec((1, SC_WINDOW, ROW_WORDS), index_map=lambda k, i: (k, i, 0))],
            core_axis_name=("c", "s"), dimension_semantics=(pltpu.PARALLEL, pltpu.PARALLEL),
        )(i_hbm, o_hbm)

    return gather_rows(y_words, dest)


def _expert_kernel(be_ref, nv_ref, xlo_ref, xhi_ref, wg_ref, wu_ref, wd_ref, ylo_ref, yhi_ref):
    del be_ref
    nv = nv_ref[pl.program_id(0)]

    @pl.when(nv > 0)
    def _():
        parts = _unpack_rows(xlo_ref[...], xhi_ref[...])
        xb = jnp.concatenate([p.astype(BF16) for p in parts], axis=1)
        live = lax.broadcasted_iota(jnp.int32, xb.shape, 0) < nv
        xb = jnp.where(live, xb, jnp.zeros_like(xb))
        hid = jax.nn.silu(jnp.dot(xb, wg_ref[0, 0].astype(BF16), preferred_element_type=F32))
        hid = hid * jnp.dot(xb, wu_ref[0, 0].astype(BF16), preferred_element_type=F32)
        y = jnp.dot(hid.astype(BF16), wd_ref[0, 0].astype(BF16), preferred_element_type=F32)
        ylo_ref[...], yhi_ref[...] = _pack_rows(y)

    @pl.when(nv == 0)
    def _():
        ylo_ref[...] = jnp.zeros_like(ylo_ref)
        yhi_ref[...] = jnp.zeros_like(yhi_ref)


def _expert_call(block_e, n_valid, xlo, xhi, wg, wu, wd, li):
    n_rows = xlo.shape[0]
    n_blocks = n_rows // MOE_BLOCK
    rows = pl.BlockSpec((MOE_BLOCK, ROW_WORDS), lambda i, be, nv: (i, 0))
    grid_spec = pltpu.PrefetchScalarGridSpec(
        num_scalar_prefetch=2, grid=(n_blocks,),
        in_specs=[rows, rows,
                  pl.BlockSpec((1, 1, D_MODEL, EXPERT_FF), lambda i, be, nv: (li, be[i], 0, 0)),
                  pl.BlockSpec((1, 1, D_MODEL, EXPERT_FF), lambda i, be, nv: (li, be[i], 0, 0)),
                  pl.BlockSpec((1, 1, EXPERT_FF, D_MODEL), lambda i, be, nv: (li, be[i], 0, 0))],
        out_specs=[rows, rows])
    out = jax.ShapeDtypeStruct((n_rows, ROW_WORDS), jnp.uint32)
    return pl.pallas_call(
        _expert_kernel, grid_spec=grid_spec, out_shape=[out, out],
        compiler_params=_params(("arbitrary",)), name="moe_experts",
    )(block_e, n_valid, xlo, xhi, wg, wu, wd)


def _combine_kernel(ylo_ref, yhi_ref, w_ref, s_ref, x_ref, g_ref, o_ref):
    w = w_ref[0]
    acc = [None] * 4
    for k in range(TOP_K):
        wk = w[:, k:k + 1]
        for c, part in enumerate(_unpack_rows(ylo_ref[k], yhi_ref[k])):
            acc[c] = wk * part if acc[c] is None else acc[c] + wk * part
    for c in range(4):
        sl = slice(c * ROW_WORDS, (c + 1) * ROW_WORDS)
        o_ref[0, :, sl] = x_ref[0, :, sl] + g_ref[0, :, sl] * (acc[c] + s_ref[0, :, sl])


def _combine_call(ylo, yhi, wts, shared, x, g2):
    t3 = lambda b, j: (b, j, 0)
    nt = SEQ // MOE_TL
    rows = pl.BlockSpec((TOP_K, MOE_TL, ROW_WORDS), lambda b, j: (0, b * nt + j, 0))
    return pl.pallas_call(
        _combine_kernel, grid=(BATCH, nt),
        in_specs=[rows, rows,
                  pl.BlockSpec((1, MOE_TL, SLOT_PAD), t3),
                  pl.BlockSpec((1, MOE_TL, D_MODEL), t3),
                  pl.BlockSpec((1, MOE_TL, D_MODEL), t3),
                  pl.BlockSpec((1, 1, D_MODEL), lambda b, j: (b, 0, 0))],
        out_specs=pl.BlockSpec((1, MOE_TL, D_MODEL), t3),
        out_shape=jax.ShapeDtypeStruct((BATCH, SEQ, D_MODEL), F32),
        compiler_params=_params(("parallel", "arbitrary")), name="moe_combine",
    )(ylo, yhi, wts, shared, x, g2)


def _moe(x, sh, sc, g2, router_w, router_b, w_gate, w_up, w_down, sh_gate, sh_up, sh_down, li):
    T = BATCH * SEQ
    TK = T * TOP_K
    hlo, hhi, idx, wts, rank, counts, shared = _moe_in_call(
        x, sh, sc, router_w, router_b[None, :], sh_gate, sh_up, sh_down)
    counts = counts[0].astype(jnp.int32)
    padded = (counts + MOE_BLOCK - 1) // MOE_BLOCK * MOE_BLOCK
    pad_end = jnp.cumsum(padded)
    pad_start = pad_end - padded
    n_blocks = -(-TK // MOE_BLOCK) + N_EXPERTS
    n_rows = n_blocks * MOE_BLOCK
    block_start = jnp.arange(n_blocks, dtype=jnp.int32) * MOE_BLOCK
    block_e = jnp.sum((block_start[:, None] >= pad_end[None, :]).astype(jnp.int32), axis=1)
    block_e = jnp.minimum(block_e, N_EXPERTS - 1)
    n_valid = jnp.clip(counts[block_e] - (block_start - pad_start[block_e]), 0, MOE_BLOCK).astype(jnp.int32)
    idx = idx.reshape(T, SLOT_PAD)[:, :TOP_K]
    dest = pad_start[idx] + rank.reshape(T, SLOT_PAD)[:, :TOP_K]
    dest = dest.T.astype(jnp.int32)
    xlo = _sc_dispatch(hlo.reshape(T, ROW_WORDS), dest, n_rows)
    xhi = _sc_dispatch(hhi.reshape(T, ROW_WORDS), dest, n_rows)
    ylo, yhi = _expert_call(block_e, n_valid, xlo, xhi, w_gate, w_up, w_down, li)
    return _combine_call(_sc_collect(ylo, dest), _sc_collect(yhi, dest), wts, shared, x, g2)


HY_TL = 512
HALO = 8


def _hy_in_kernel(x_ref, xp_ref, xn_ref, sh_ref, sc_ref, w_ref, cw_ref, cb_ref, z_ref, x0_ref, h_sc):
    j = pl.program_id(1)
    shift, scale = sh_ref[0], sc_ref[0]
    keep_prev = (j > 0).astype(F32)
    keep_next = (j < SEQ // HY_TL - 1).astype(F32)
    h_sc[0:HALO, :] = _norm_mod(xp_ref[0], shift, scale) * keep_prev
    h_sc[HALO:HALO + HY_TL, :] = _norm_mod(x_ref[0], shift, scale)
    h_sc[HALO + HY_TL:, :] = _norm_mod(xn_ref[0], shift, scale) * keep_next
    hcat = h_sc[...].astype(BF16)
    outs = []
    for part in range(3):
        sl = slice(part * HY_WIDTH, (part + 1) * HY_WIDTH)
        p = jnp.dot(hcat, w_ref[:, sl], preferred_element_type=F32)
        o = (p[HALO - 1:HALO - 1 + HY_TL] * cw_ref[0:1, sl] + p[HALO:HALO + HY_TL] * cw_ref[1:2, sl]
             + p[HALO + 1:HALO + 1 + HY_TL] * cw_ref[2:3, sl] + cb_ref[:, sl])
        outs.append(o)
    x0_ref[0] = outs[0].astype(BF16)
    z_ref[0] = (outs[2] * outs[1]).astype(BF16)


def _hy_in_call(x, sh, sc, w, cw, cb):
    nb8 = HY_TL // HALO
    t3 = lambda b, j: (b, j, 0)
    full2 = lambda b, j: (0, 0)
    per_b = lambda b, j: (b, 0, 0)
    return pl.pallas_call(
        _hy_in_kernel, grid=(BATCH, SEQ // HY_TL),
        in_specs=[pl.BlockSpec((1, HY_TL, D_MODEL), t3),
                  pl.BlockSpec((1, HALO, D_MODEL), lambda b, j: (b, jnp.maximum(j * nb8 - 1, 0), 0)),
                  pl.BlockSpec((1, HALO, D_MODEL), lambda b, j: (b, jnp.minimum((j + 1) * nb8, SEQ // HALO - 1), 0)),
                  pl.BlockSpec((1, 1, D_MODEL), per_b),
                  pl.BlockSpec((1, 1, D_MODEL), per_b),
                  pl.BlockSpec((D_MODEL, 3 * HY_WIDTH), full2),
                  pl.BlockSpec((SHORT_CONV, 3 * HY_WIDTH), full2),
                  pl.BlockSpec((1, 3 * HY_WIDTH), full2)],
        out_specs=[pl.BlockSpec((1, HY_TL, HY_WIDTH), t3), pl.BlockSpec((1, HY_TL, HY_WIDTH), t3)],
        out_shape=[jax.ShapeDtypeStruct((BATCH, SEQ, HY_WIDTH), BF16),
                   jax.ShapeDtypeStruct((BATCH, SEQ, HY_WIDTH), BF16)],
        scratch_shapes=[pltpu.VMEM((HY_TL + 2 * HALO, D_MODEL), F32)],
        compiler_params=_params(("parallel", "arbitrary")), name="hyena_in_proj",
    )(x, x, x, sh, sc, w, cw, cb)


def _dft_matrices():
    n = DFT_N
    k = np.arange(SEQ, dtype=np.int64)[:, None]
    a = np.arange(64, dtype=np.int64)[None, :]
    ang_a = 2.0 * np.pi * ((k * a * 64) % n) / n
    ang_c = 2.0 * np.pi * ((k * a) % n) / n
    ca, sa, cc, sc = [jnp.asarray(v, F32) for v in (np.cos(ang_a), np.sin(ang_a), np.cos(ang_c), np.sin(ang_c))]
    f_re = (ca[:, :, None] * cc[:, None, :] - sa[:, :, None] * sc[:, None, :]).reshape(SEQ, SEQ)
    f_im = -(sa[:, :, None] * cc[:, None, :] + ca[:, :, None] * sc[:, None, :]).reshape(SEQ, SEQ)
    alt = jnp.asarray(1.0 - 2.0 * (np.arange(SEQ) % 2), F32)
    f_im = f_im.at[0, :].set(alt)
    fwd = jnp.concatenate([f_re.reshape(DFT_NF, DFT_TF, SEQ), f_im.reshape(DFT_NF, DFT_TF, SEQ)], axis=1)
    g_re = f_re * (2.0 / n)
    g_re = g_re.at[:, 0].set(1.0 / n)
    g_im = f_im.T * (2.0 / n)
    g_im = g_im.at[:, 0].set(alt / n)
    inv = jnp.concatenate([g_re.reshape(SEQ, DFT_NF, DFT_TF), g_im.reshape(SEQ, DFT_NF, DFT_TF)], axis=2)
    inv = inv.transpose(1, 0, 2)
    return fwd.astype(BF16), inv.astype(BF16)


def _spec_kernel(f_ref, h_ref, p_ref, pn_ref, q_ref):
    spec = jnp.dot(f_ref[0], h_ref[...], preferred_element_type=F32)
    re, im = spec[:DFT_TF], spec[DFT_TF:]
    c_re = re[:, :HY_WIDTH] + re[:, HY_WIDTH:]
    c_im = im[:, :HY_WIDTH] - im[:, HY_WIDTH:]
    nyq = im[:, :HY_WIDTH] + im[:, HY_WIDTH:]
    row = lax.broadcasted_iota(jnp.int32, c_re.shape, 0) + pl.program_id(0) * DFT_TF
    first = row == 0
    p_ref[...] = c_re
    pn_ref[...] = jnp.where(first, nyq, c_re)
    q_ref[...] = jnp.where(first, 0.0, c_im)


def _spec_call(fwd, hfb):
    out = jax.ShapeDtypeStruct((SEQ, HY_WIDTH), F32)
    ospec = pl.BlockSpec((DFT_TF, HY_WIDTH), lambda f: (f, 0))
    return pl.pallas_call(
        _spec_kernel, grid=(DFT_NF,),
        in_specs=[pl.BlockSpec((1, 2 * DFT_TF, SEQ), lambda f: (f, 0, 0)),
                  pl.BlockSpec((SEQ, 2 * HY_WIDTH), lambda f: (0, 0))],
        out_specs=[ospec, ospec, ospec], out_shape=[out, out, out],
        compiler_params=_params(("arbitrary",)), name="hyena_filter_spectrum",
    )(fwd, hfb)


def _conv_kernel(z_ref, f_ref, g_ref, p_ref, pn_ref, q_ref, y_ref, acc):
    f = pl.program_id(2)
    spec = jnp.dot(f_ref[0], z_ref[0], preferred_element_type=F32)
    zr, zi = spec[:DFT_TF], spec[DFT_TF:]
    yr = zr * p_ref[...] - zi * q_ref[...]
    yi = zr * q_ref[...] + zi * pn_ref[...]
    ycat = jnp.concatenate([yr, yi], axis=0).astype(BF16)
    part = jnp.dot(g_ref[0], ycat, preferred_element_type=F32)

    @pl.when(f == 0)
    def _():
        acc[...] = part

    @pl.when(f > 0)
    def _():
        acc[...] += part

    @pl.when(f == DFT_NF - 1)
    def _():
        y_ref[0] = acc[...].astype(BF16)


def _conv_call(z, fwd, inv, p, pn, q):
    sspec = pl.BlockSpec((DFT_TF, HY_CT), lambda b, c, f: (f, c))
    return pl.pallas_call(
        _conv_kernel, grid=(BATCH, HY_WIDTH // HY_CT, DFT_NF),
        in_specs=[pl.BlockSpec((1, SEQ, HY_CT), lambda b, c, f: (b, 0, c)),
                  pl.BlockSpec((1, 2 * DFT_TF, SEQ), lambda b, c, f: (f, 0, 0)),
                  pl.BlockSpec((1, SEQ, 2 * DFT_TF), lambda b, c, f: (f, 0, 0)),
                  sspec, sspec, sspec],
        out_specs=pl.BlockSpec((1, SEQ, HY_CT), lambda b, c, f: (b, 0, c)),
        out_shape=jax.ShapeDtypeStruct((BATCH, SEQ, HY_WIDTH), BF16),
        scratch_shapes=[pltpu.VMEM((SEQ, HY_CT), F32)],
        compiler_params=_params(("parallel", "parallel", "arbitrary")), name="hyena_long_conv",
    )(z, fwd, inv, p, pn, q)


def _hy_out_kernel(y_ref, z_ref, x0_ref, b_ref, w_ref, x_ref, g_ref, o_ref):
    z = z_ref[0].astype(F32)
    gated = x0_ref[0].astype(F32) * (y_ref[0].astype(F32) + b_ref[...] * z)
    mix = jnp.dot(gated.astype(BF16), w_ref[...], preferred_element_type=F32)
    o_ref[0] = x_ref[0] + g_ref[0] * mix


def _hy_out_call(y, z, x0, bias, w, x, g1):
    t3 = lambda b, j: (b, j, 0)
    full2 = lambda b, j: (0, 0)
    return pl.pallas_call(
        _hy_out_kernel, grid=(BATCH, SEQ // HY_TL),
        in_specs=[pl.BlockSpec((1, HY_TL, HY_WIDTH), t3),
                  pl.BlockSpec((1, HY_TL, HY_WIDTH), t3),
                  pl.BlockSpec((1, HY_TL, HY_WIDTH), t3),
                  pl.BlockSpec((1, HY_WIDTH), full2),
                  pl.BlockSpec((HY_WIDTH, D_MODEL), full2),
                  pl.BlockSpec((1, HY_TL, D_MODEL), t3),
                  pl.BlockSpec((1, 1, D_MODEL), lambda b, j: (b, 0, 0))],
        out_specs=pl.BlockSpec((1, HY_TL, D_MODEL), t3),
        out_shape=jax.ShapeDtypeStruct((BATCH, SEQ, D_MODEL), F32),
        compiler_params=_params(("parallel", "arbitrary")), name="hyena_out_proj",
    )(y, z, x0, bias, w, x, g1)


def _hyena_filter(w1, b1, w2, b2, w3, freq):
    hi = lax.Precision.HIGHEST
    Lq = SEQ
    t = jnp.linspace(0.0, 1.0, Lq, dtype=F32)[:, None]
    ang = 2.0 * math.pi * jnp.arange(Lq, dtype=F32)[:, None] / Lq
    bands = jnp.linspace(1e-4, FILT_BANDS - 1, FILT_BANDS, dtype=F32)
    z = jnp.concatenate([t, jnp.cos(bands * ang), -jnp.sin(bands * ang)], axis=-1)
    hid = jnp.sin(freq * (jnp.dot(z, w1, precision=hi) + b1))
    hid = jnp.sin(freq * (jnp.dot(hid, w2, precision=hi) + b2))
    hf = jnp.dot(hid, w3, precision=hi).reshape(Lq, 2, HY_WIDTH)
    deltas = jnp.linspace(HY_MIN_DECAY, HY_MAX_DECAY, HY_WIDTH, dtype=F32)
    hf = hf * jnp.exp(-t * deltas)[:, None, :]
    hf = hf * lax.rsqrt(jnp.sum(hf * hf, axis=(0, 1), keepdims=True) + EPS)
    return hf.reshape(Lq, 2 * HY_WIDTH)


def kernel(x, c, ctx, c_ctx, ada_w, ada_b, ev_w_in, mla_q_norm, mla_w_uq, mla_kv_norm, mla_w_ukv, mla_q_qknorm, mla_k_qknorm, s5_lam_re, s5_lam_im, s5_log_step, s5_b_re, s5_b_im, s5_c_re, s5_c_im, s5_d, s5_glu_w, s5_glu_b, ev_w_out, hy_w_in, hy_conv_w, hy_conv_b, hy_f_w1, hy_f_b1, hy_f_w2, hy_f_b2, hy_f_w3, hy_f_freq, hy_bias, hy_w_out, moe_router_w, moe_router_b, moe_w_gate, moe_w_up, moe_w_down, moe_sh_gate, moe_sh_up, moe_sh_down):
    hi = lax.Precision.HIGHEST
    D = D_MODEL
    sc = jax.nn.silu(c)
    sc_ctx = jax.nn.silu(c_ctx)

    def mods(li):
        mod = jnp.dot(sc, ada_w[li], precision=hi) + ada_b[li]
        return [m[:, None, :] for m in jnp.split(mod, 6, axis=-1)]

    sh1, sc1, g1, sh2, sc2, g2 = mods(0)
    mod_ctx = jnp.dot(sc_ctx, ada_w[0][:, :2 * D], precision=hi) + ada_b[0][:2 * D]
    w0 = _a0_weights(ev_w_in[0], mla_q_norm[0], mla_w_uq[0], mla_kv_norm[0], mla_w_ukv[0],
                     mla_q_qknorm[0], mla_k_qknorm[0])
    q, k, v, u = _a0_call(x, ctx, sh1, sc1, mod_ctx[None, :D], mod_ctx[None, D:], w0)
    att = _attn_call(q, k, v)
    ug = u.astype(BF16).reshape(BATCH, S5_NCHUNK, S5_CHUNK, S5_GROUPS, S5_GROUP)
    ug = ug.transpose(3, 1, 0, 2, 4).reshape(S5_GROUPS, S5_ROWS, S5_COLS)
    ys = _s5_call(ug, *_s5_weights(s5_lam_re[0], s5_lam_im[0], s5_log_step[0], s5_b_re[0], s5_b_im[0],
                                   s5_c_re[0], s5_c_im[0]))
    ys = ys.reshape(S5_GROUPS, S5_NCHUNK_LAT, BATCH, S5_CHUNK, S5_GROUP)
    ys = ys.transpose(2, 1, 3, 0, 4).reshape(BATCH, SEQ, S5_WIDTH)
    wo = ev_w_out[0].astype(BF16)
    x = _a1_call(att, ys, u, s5_d[0][None, :], s5_glu_w[0].astype(BF16), s5_glu_b[0][None, :],
                 wo[:MLA_WIDTH], wo[MLA_WIDTH:], x, g1)
    x = _moe(x, sh2, sc2, g2, moe_router_w[0], moe_router_b[0], moe_w_gate, moe_w_up, moe_w_down,
             moe_sh_gate[0], moe_sh_up[0], moe_sh_down[0], 0)

    sh1, sc1, g1, sh2, sc2, g2 = mods(1)
    z, x0 = _hy_in_call(x, sh1, sc1, hy_w_in[0].astype(BF16), hy_conv_w[0], hy_conv_b[0][None, :])
    fwd, inv = _dft_matrices()
    hfb = _hyena_filter(hy_f_w1[0], hy_f_b1[0], hy_f_w2[0], hy_f_b2[0], hy_f_w3[0], hy_f_freq[0])
    p, pn, qq = _spec_call(fwd, hfb.astype(BF16))
    y = _conv_call(z, fwd, inv, p, pn, qq)
    x = _hy_out_call(y, z, x0, hy_bias[0][None, :], hy_w_out[0].astype(BF16), x, g1)
    x = _moe(x, sh2, sc2, g2, moe_router_w[1], moe_router_b[1], moe_w_gate, moe_w_up, moe_w_down,
             moe_sh_gate[1], moe_sh_up[1], moe_sh_down[1], 1)
    return x
```

```python
import functools
import math

import numpy as np
import jax
import jax.numpy as jnp
from jax import lax
from jax.experimental import pallas as pl
from jax.experimental.pallas import tpu as pltpu
from jax.experimental.pallas import tpu_sc as plsc

F32 = jnp.float32
BF16 = jnp.bfloat16

D_MODEL = 1024
BATCH = 8
SEQ = 4096
CTX_LEN = 256
KV_LEN = SEQ + CTX_LEN
GRID_W = 64
EPS = 1e-6

MLA_HEADS = 8
QK_NOPE = 64
QK_ROPE = 32
QK_HEAD = QK_NOPE + QK_ROPE
V_HEAD = 64
Q_LORA = 256
KV_LORA = 128
MLA_WIDTH = MLA_HEADS * V_HEAD
ROPE_BASE = 10000.0
HEAD_PAD = 128

S5_WIDTH = 512
S5_GROUP = 16
S5_GROUPS = S5_WIDTH // S5_GROUP
S5_STATE = 64
S5_CHUNK = 32
S5_NCHUNK = KV_LEN // S5_CHUNK
S5_NCHUNK_LAT = SEQ // S5_CHUNK
S5_NCHUNK_CTX = CTX_LEN // S5_CHUNK

HY_WIDTH = D_MODEL
FILT_EMB = 33
FILT_BANDS = (FILT_EMB - 1) // 2
SHORT_CONV = 3
HY_MIN_DECAY = -math.log(1e-2) / 1.5
HY_MAX_DECAY = -math.log(1e-2) / 0.3
DFT_N = 2 * SEQ
FFT_N1 = 16
FFT_N2 = DFT_N // FFT_N1
FFT_NA = FFT_N1 // 2
FFT_NK = FFT_N1 // 2 + 1
HY_CT = 256

N_EXPERTS = 64
TOP_K = 6
EXPERT_FF = 256
ROUTE_SCALE = 2.5
MOE_BLOCK = 512
ROW_WORDS = D_MODEL // 4
SC_WINDOW = 128

V7X_VMEM_BYTES = 64 * 1024 * 1024
VMEM_LIMIT = V7X_VMEM_BYTES - 8 * 1024 * 1024


def _params(semantics):
    return pltpu.CompilerParams(dimension_semantics=semantics, vmem_limit_bytes=VMEM_LIMIT)


def _norm_mod(x, shift, scale):
    ms = jnp.mean(x * x, axis=-1, keepdims=True)
    return x * lax.rsqrt(ms + EPS) * (1.0 + scale) + shift


def _rms(x, gain, n):
    ms = jnp.sum(x * x, axis=-1, keepdims=True) * (1.0 / n)
    return x * lax.rsqrt(ms + EPS) * gain


A0_TL = 256
A0_NT = SEQ // A0_TL


def _rope_perm():
    return np.concatenate([np.arange(0, QK_ROPE, 2), np.arange(1, QK_ROPE, 2)])


def _rope_tables():
    t = np.arange(SEQ)
    row = (t // GRID_W).astype(np.float64)
    col = (t % GRID_W).astype(np.float64)
    n_freq = QK_ROPE // 4
    inv = ROPE_BASE ** (-np.arange(n_freq, dtype=np.float64) / n_freq)
    ang = np.concatenate([row[:, None] * inv, col[:, None] * inv], axis=-1)
    cos, sin = np.cos(ang), np.sin(ang)
    half = QK_ROPE // 2
    a = np.zeros((KV_LEN, HEAD_PAD))
    bp = np.zeros((KV_LEN, HEAD_PAD))
    bm = np.zeros((KV_LEN, HEAD_PAD))
    a[:, :QK_HEAD] = 1.0
    a[:SEQ, QK_NOPE:QK_NOPE + half] = cos
    a[:SEQ, QK_NOPE + half:QK_HEAD] = cos
    bp[:SEQ, QK_NOPE + half:QK_HEAD] = sin
    bm[:SEQ, QK_NOPE:QK_NOPE + half] = -sin
    return a, bp, bm


def _a0_kernel(x_ref, ctx_ref, sh_ref, sc_ref, shc_ref, scc_ref, win_ref, qn_ref, wuq_ref, kvn_ref,
               wk_ref, wuv_ref, qg_ref, kg_ref, ka_ref, kp_ref, km_ref, qa_ref, qp_ref, qm_ref,
               q_ref, k_ref, v_ref, u_ref):
    j = pl.program_id(1)
    is_ctx = j == A0_NT
    xin = jnp.where(is_ctx, ctx_ref[0], x_ref[0])
    shift = jnp.where(is_ctx, shc_ref[...], sh_ref[0])
    scale = jnp.where(is_ctx, scc_ref[...], sc_ref[0])
    h = _norm_mod(xin, shift, scale).astype(BF16)
    proj = jnp.dot(h, win_ref[...], preferred_element_type=F32)
    u_ref[0] = proj[:, 512:]

    c_kv = _rms(proj[:, Q_LORA:Q_LORA + KV_LORA], kvn_ref[...], KV_LORA).astype(BF16)
    v_ref[0] = jnp.dot(c_kv, wuv_ref[...], preferred_element_type=F32).astype(BF16)
    kin = jnp.concatenate([c_kv, proj[:, 384:512].astype(BF16)], axis=1)
    kf = jnp.dot(kin, wk_ref[...], preferred_element_type=F32)
    ka, kp, km = ka_ref[...], kp_ref[...], km_ref[...]
    for hd in range(MLA_HEADS):
        sl = slice(hd * HEAD_PAD, (hd + 1) * HEAD_PAD)
        kh = _rms(kf[:, sl], kg_ref[:, sl], QK_HEAD)
        kh = kh * ka + pltpu.roll(kh, 16, 1) * kp + pltpu.roll(kh, HEAD_PAD - 16, 1) * km
        k_ref[0, :, sl] = kh.astype(BF16)

    @pl.when(j < A0_NT)
    def _():
        ql = _rms(proj[:, :Q_LORA], qn_ref[...], Q_LORA).astype(BF16)
        qf = jnp.dot(ql, wuq_ref[...], preferred_element_type=F32)
        qa, qp, qm = qa_ref[...], qp_ref[...], qm_ref[...]
        for hd in range(MLA_HEADS):
            sl = slice(hd * HEAD_PAD, (hd + 1) * HEAD_PAD)
            qh = _rms(qf[:, sl], qg_ref[:, sl], QK_HEAD)
            qh = qh * qa + pltpu.roll(qh, 16, 1) * qp + pltpu.roll(qh, HEAD_PAD - 16, 1) * qm
            q_ref[0, :, sl] = qh.astype(BF16)


def _a0_call(x, ctx, sh, sc, shc, scc, w):
    nt = A0_NT
    lat = lambda b, j: (b, jnp.minimum(j, nt - 1), 0)
    full2 = lambda b, j: (0, 0)
    per_b = lambda b, j: (b, 0, 0)
    tab = pl.BlockSpec((A0_TL, HEAD_PAD), lambda b, j: (j, 0))
    in_specs = [
        pl.BlockSpec((1, A0_TL, D_MODEL), lat),
        pl.BlockSpec((1, CTX_LEN, D_MODEL), per_b),
        pl.BlockSpec((1, 1, D_MODEL), per_b),
        pl.BlockSpec((1, 1, D_MODEL), per_b),
        pl.BlockSpec((1, D_MODEL), full2),
        pl.BlockSpec((1, D_MODEL), full2),
        pl.BlockSpec((D_MODEL, 1024), full2),
        pl.BlockSpec((1, Q_LORA), full2),
        pl.BlockSpec((Q_LORA, MLA_HEADS * HEAD_PAD), full2),
        pl.BlockSpec((1, KV_LORA), full2),
        pl.BlockSpec((2 * KV_LORA, MLA_HEADS * HEAD_PAD), full2),
        pl.BlockSpec((KV_LORA, MLA_WIDTH), full2),
        pl.BlockSpec((1, MLA_HEADS * HEAD_PAD), full2),
        pl.BlockSpec((1, MLA_HEADS * HEAD_PAD), full2),
        tab, tab, tab, tab, tab, tab,
    ]
    out_specs = [
        pl.BlockSpec((1, A0_TL, MLA_HEADS * HEAD_PAD), lat),
        pl.BlockSpec((1, A0_TL, MLA_HEADS * HEAD_PAD), lambda b, j: (b, j, 0)),
        pl.BlockSpec((1, A0_TL, MLA_WIDTH), lambda b, j: (b, j, 0)),
        pl.BlockSpec((1, A0_TL, S5_WIDTH), lambda b, j: (b, j, 0)),
    ]
    out_shape = [
        jax.ShapeDtypeStruct((BATCH, SEQ, MLA_HEADS * HEAD_PAD), BF16),
        jax.ShapeDtypeStruct((BATCH, KV_LEN, MLA_HEADS * HEAD_PAD), BF16),
        jax.ShapeDtypeStruct((BATCH, KV_LEN, MLA_WIDTH), BF16),
        jax.ShapeDtypeStruct((BATCH, KV_LEN, S5_WIDTH), F32),
    ]
    return pl.pallas_call(
        _a0_kernel, grid=(BATCH, nt + 1), in_specs=in_specs, out_specs=out_specs, out_shape=out_shape,
        compiler_params=_params(("parallel", "arbitrary")), name="even_in_proj",
    )(x, ctx, sh, sc, shc, scc, *w)


def _a0_weights(w_in, q_norm, w_uq, kv_norm, w_ukv, q_qk, k_qk):
    perm = _rope_perm()
    kr0 = Q_LORA + KV_LORA
    w_cat = jnp.concatenate([
        w_in[:, :kr0], w_in[:, kr0:kr0 + QK_ROPE][:, perm],
        jnp.zeros((D_MODEL, HEAD_PAD - QK_ROPE), F32), w_in[:, kr0 + QK_ROPE:]], axis=1).astype(BF16)
    pad = HEAD_PAD - QK_HEAD

    def head_gain(g):
        gh = jnp.concatenate([g[:QK_NOPE], g[QK_NOPE:][perm], jnp.zeros((pad,), F32)])
        return jnp.tile(gh, MLA_HEADS)[None, :]

    uq = w_uq.reshape(Q_LORA, MLA_HEADS, QK_HEAD)
    uq = jnp.concatenate([uq[..., :QK_NOPE], uq[..., QK_NOPE:][..., perm],
                          jnp.zeros((Q_LORA, MLA_HEADS, pad), F32)], axis=-1)
    uq = uq.reshape(Q_LORA, MLA_HEADS * HEAD_PAD).astype(BF16)
    ukv = w_ukv.reshape(KV_LORA, MLA_HEADS, QK_NOPE + V_HEAD)
    uk = jnp.concatenate([ukv[..., :QK_NOPE], jnp.zeros((KV_LORA, MLA_HEADS, HEAD_PAD - QK_NOPE), F32)], axis=-1)
    uk = uk.reshape(KV_LORA, MLA_HEADS * HEAD_PAD)
    place = np.zeros((KV_LORA, MLA_HEADS, HEAD_PAD), np.float32)
    for i in range(QK_ROPE):
        place[i, :, QK_NOPE + i] = 1.0
    wk = jnp.concatenate([uk, jnp.asarray(place.reshape(KV_LORA, MLA_HEADS * HEAD_PAD))], axis=0).astype(BF16)
    wuv = ukv[..., QK_NOPE:].reshape(KV_LORA, MLA_WIDTH).astype(BF16)
    a, bp, bm = _rope_tables()
    qs = QK_HEAD ** -0.5
    tabs = [jnp.asarray(t, F32) for t in (a, bp, bm, a * qs, bp * qs, bm * qs)]
    return [w_cat, q_norm[None, :], uq, kv_norm[None, :], wk, wuv, head_gain(q_qk), head_gain(k_qk)] + tabs


ATT_TQ = 256
HEADS_PER_STEP = 2


def _attn_kernel(q_ref, k_ref, v_ref, o_ref):
    v = v_ref[0]
    lane_head = lax.broadcasted_iota(jnp.int32, v.shape, 1) // V_HEAD
    acc = None
    for hh in range(HEADS_PER_STEP):
        sl = slice(hh * HEAD_PAD, (hh + 1) * HEAD_PAD)
        s = lax.dot_general(q_ref[0, :, sl], k_ref[0, :, sl], (((1,), (1,)), ((), ())),
                            preferred_element_type=F32)
        m = jnp.max(s, axis=-1, keepdims=True)
        p = jnp.exp(s - m)
        l = jnp.sum(p, axis=-1, keepdims=True)
        vh = jnp.where(lane_head == hh, v, jnp.zeros_like(v))
        o = jnp.dot(p.astype(BF16), vh, preferred_element_type=F32) * (1.0 / l)
        acc = o if acc is None else acc + o
    o_ref[0] = acc.astype(BF16)


def _attn_call(q, k, v):
    wq = HEADS_PER_STEP * HEAD_PAD
    wv = HEADS_PER_STEP * V_HEAD
    return pl.pallas_call(
        _attn_kernel, grid=(BATCH, MLA_HEADS // HEADS_PER_STEP, SEQ // ATT_TQ),
        in_specs=[pl.BlockSpec((1, ATT_TQ, wq), lambda b, h, i: (b, i, h)),
                  pl.BlockSpec((1, KV_LEN, wq), lambda b, h, i: (b, 0, h)),
                  pl.BlockSpec((1, KV_LEN, wv), lambda b, h, i: (b, 0, h))],
        out_specs=pl.BlockSpec((1, ATT_TQ, wv), lambda b, h, i: (b, i, h)),
        out_shape=jax.ShapeDtypeStruct((BATCH, SEQ, MLA_WIDTH), BF16),
        compiler_params=_params(("parallel", "parallel", "arbitrary")), name="mla_attention",
    )(q, k, v)


S5_ROWS = S5_NCHUNK * BATCH
S5_ROWS_LAT = S5_NCHUNK_LAT * BATCH
S5_COLS = S5_CHUNK * S5_GROUP
S5_SW = 2 * S5_STATE


def _s5_kernel(u_ref, t_ref, mb_ref, mc_ref, coef_ref, y_ref, x_sc, sp_sc):
    u = u_ref[0]
    x_sc[...] = jnp.dot(u, mb_ref[0], preferred_element_type=F32)
    cf = coef_ref[0]
    af, bfm, bfp, ab, bbm, bbp = [cf[i * 8:(i + 1) * 8] for i in range(6)]

    def body(i, carry):
        sf, sfw, sb, sbw = carry
        cfw = jnp.where(i < S5_NCHUNK_CTX, i + S5_NCHUNK_LAT, i - S5_NCHUNK_CTX)
        rf = pl.multiple_of(cfw * BATCH, BATCH)
        rb = pl.multiple_of((S5_NCHUNK - 1 - i) * BATCH, BATCH)
        sp_sc[pl.ds(rf, BATCH), 0:S5_SW] = sf
        sp_sc[pl.ds(rb, BATCH), S5_SW:2 * S5_SW] = sb
        xf = x_sc[pl.ds(rf, BATCH), 0:S5_SW]
        xfw = x_sc[pl.ds(rf, BATCH), S5_SW:2 * S5_SW]
        xb = x_sc[pl.ds(rb, BATCH), 2 * S5_SW:3 * S5_SW]
        xbw = x_sc[pl.ds(rb, BATCH), 3 * S5_SW:4 * S5_SW]
        return (sf * af + sfw * bfm + xf, sfw * af + sf * bfp + xfw,
                sb * ab + sbw * bbm + xb, sbw * ab + sb * bbp + xbw)

    z = jnp.zeros((BATCH, S5_SW), F32)
    lax.fori_loop(0, S5_NCHUNK, body, (z, z, z, z))
    y = jnp.dot(u[:S5_ROWS_LAT], t_ref[0], preferred_element_type=F32)
    y = y + jnp.dot(sp_sc[0:S5_ROWS_LAT, :].astype(BF16), mc_ref[0], preferred_element_type=F32)
    y_ref[0] = y


def _s5_call(ug, t, mb, mc, coef):
    g3 = lambda g: (g, 0, 0)
    return pl.pallas_call(
        _s5_kernel, grid=(S5_GROUPS,),
        in_specs=[pl.BlockSpec((1, S5_ROWS, S5_COLS), g3),
                  pl.BlockSpec((1, S5_COLS, S5_COLS), g3),
                  pl.BlockSpec((1, S5_COLS, 4 * S5_SW), g3),
                  pl.BlockSpec((1, 2 * S5_SW, S5_COLS), g3),
                  pl.BlockSpec((1, 6 * 8, S5_SW), g3)],
        out_specs=pl.BlockSpec((1, S5_ROWS_LAT, S5_COLS), g3),
        out_shape=jax.ShapeDtypeStruct((S5_GROUPS, S5_ROWS_LAT, S5_COLS), F32),
        scratch_shapes=[pltpu.VMEM((S5_ROWS, 4 * S5_SW), F32), pltpu.VMEM((S5_ROWS, 2 * S5_SW), F32)],
        compiler_params=_params(("parallel",)), name="s5_chunked_scan",
    )(ug, t, mb, mc, coef)


def _s5_weights(lam_re, lam_im, log_step, b_re, b_im, c_re, c_im):
    q = S5_CHUNK
    hi = lax.Precision.HIGHEST
    t_blocks, mbs, mcs, coefs = [], [], [], []
    sig = jnp.arange(q)
    for d in range(2):
        lr = jnp.minimum(lam_re[d], -1e-4)
        li = lam_im[d]
        step = jnp.exp(log_step[d])[:, None]
        jj = jnp.arange(q + 1, dtype=F32)[:, None, None]
        mag = jnp.exp(lr * step * jj)
        ph = li * step * jj
        pr, pi = mag * jnp.cos(ph), mag * jnp.sin(ph)
        nr, ni = pr[1] - 1.0, pi[1]
        den = lr * lr + li * li
        fr, fi = (nr * lr + ni * li) / den, (ni * lr - nr * li) / den
        br = fr[..., None] * b_re[d] - fi[..., None] * b_im[d]
        bi = fr[..., None] * b_im[d] + fi[..., None] * b_re[d]
        cr, ci = c_re[d], c_im[d]
        cpr = cr[None] * pr[:, :, None, :] - ci[None] * pi[:, :, None, :]
        cpi = cr[None] * pi[:, :, None, :] + ci[None] * pr[:, :, None, :]
        kern = (jnp.einsum('jghp,gpk->jghk', cpr[:q], br, precision=hi)
                - jnp.einsum('jghp,gpk->jghk', cpi[:q], bi, precision=hi))
        lag = (sig[None, :] - sig[:, None]) if d == 0 else (sig[:, None] - sig[None, :])
        kt = kern.transpose(1, 0, 3, 2)
        tb = jnp.where((lag >= 0)[None, :, :, None, None], kt[:, jnp.clip(lag, 0, q - 1)], 0.0)
        t_blocks.append(tb.transpose(0, 1, 3, 2, 4).reshape(S5_GROUPS, S5_COLS, S5_COLS))
        pw = (q - 1 - sig) if d == 0 else sig
        xr = pr[pw][..., None] * br[None] - pi[pw][..., None] * bi[None]
        xi = pr[pw][..., None] * bi[None] + pi[pw][..., None] * br[None]
        xr = xr.transpose(1, 0, 3, 2).reshape(S5_GROUPS, S5_COLS, S5_STATE)
        xi = xi.transpose(1, 0, 3, 2).reshape(S5_GROUPS, S5_COLS, S5_STATE)
        mbs += [xr, xi, xi, xr]
        po = (sig + 1) if d == 0 else (q - sig)
        mr = cpr[po].transpose(1, 3, 0, 2).reshape(S5_GROUPS, S5_STATE, S5_COLS)
        mi = -cpi[po].transpose(1, 3, 0, 2).reshape(S5_GROUPS, S5_STATE, S5_COLS)
        mcs += [mr, mi]
        are, aim = pr[q], pi[q]
        rows = [jnp.concatenate([are, are], -1), jnp.concatenate([-aim, aim], -1), jnp.concatenate([aim, -aim], -1)]
        coefs += [jnp.broadcast_to(r[:, None, :], (S5_GROUPS, 8, S5_SW)) for r in rows]
    t = (t_blocks[0] + t_blocks[1]).astype(BF16)
    mb = jnp.concatenate(mbs, axis=-1).astype(BF16)
    mc = jnp.concatenate(mcs, axis=1).astype(BF16)
    coef = jnp.concatenate(coefs, axis=1)
    return t, mb, mc, coef


A1_TL = 512


def _a1_kernel(att_ref, ys_ref, u_ref, d_ref, gw_ref, gb_ref, woa_ref, wos_ref, x_ref, g_ref, o_ref):
    y = u_ref[0] * d_ref[...] + ys_ref[0]
    z = jax.nn.gelu(y)
    gate = jax.nn.sigmoid(jnp.dot(z.astype(BF16), gw_ref[...], preferred_element_type=F32) + gb_ref[...])
    s5 = (z * gate).astype(BF16)
    mix = jnp.dot(att_ref[0], woa_ref[...], preferred_element_type=F32)
    mix = mix + jnp.dot(s5, wos_ref[...], preferred_element_type=F32)
    o_ref[0] = x_ref[0] + g_ref[0] * mix


def _a1_call(att, ys, u, d, gw, gb, woa, wos, x, g1):
    t3 = lambda b, j: (b, j, 0)
    full2 = lambda b, j: (0, 0)
    return pl.pallas_call(
        _a1_kernel, grid=(BATCH, SEQ // A1_TL),
        in_specs=[pl.BlockSpec((1, A1_TL, MLA_WIDTH), t3),
                  pl.BlockSpec((1, A1_TL, S5_WIDTH), t3),
                  pl.BlockSpec((1, A1_TL, S5_WIDTH), t3),
                  pl.BlockSpec((1, S5_WIDTH), full2),
                  pl.BlockSpec((S5_WIDTH, S5_WIDTH), full2),
                  pl.BlockSpec((1, S5_WIDTH), full2),
                  pl.BlockSpec((MLA_WIDTH, D_MODEL), full2),
                  pl.BlockSpec((S5_WIDTH, D_MODEL), full2),
                  pl.BlockSpec((1, A1_TL, D_MODEL), t3),
                  pl.BlockSpec((1, 1, D_MODEL), lambda b, j: (b, 0, 0))],
        out_specs=pl.BlockSpec((1, A1_TL, D_MODEL), t3),
        out_shape=jax.ShapeDtypeStruct((BATCH, SEQ, D_MODEL), F32),
        compiler_params=_params(("parallel", "arbitrary")), name="even_out_proj",
    )(att, ys, u, d, gw, gb, woa, wos, x, g1)


MOE_TL = 512
SLOT_PAD = 8


def _slot_columns(cols, lane):
    out = jnp.zeros(lane.shape, cols[0].dtype)
    for k, col in enumerate(cols):
        out = jnp.where(lane == k, col, out)
    return out[:, :SLOT_PAD]


def _pack_rows(v):
    halves = []
    for p in range(2):
        base = 2 * p * ROW_WORDS
        a = pltpu.bitcast(v[:, base:base + ROW_WORDS].astype(BF16).astype(F32), jnp.uint32)
        b = pltpu.bitcast(v[:, base + ROW_WORDS:base + 2 * ROW_WORDS].astype(BF16).astype(F32), jnp.uint32)
        halves.append((a >> 16) | b)
    return halves


def _unpack_rows(lo, hi):
    out = []
    for w in (lo, hi):
        out.append(pltpu.bitcast(w << 16, F32))
        out.append(pltpu.bitcast(w & jnp.uint32(0xFFFF0000), F32))
    return out


def _moe_in_kernel(x_ref, sh_ref, sc_ref, rw_ref, rb_ref, sg_ref, su_ref, sd_ref, tri_ref,
                   hlo_ref, hhi_ref, idx_ref, wt_ref, rank_ref, cnt_ref, o_ref, run_sc):
    @pl.when((pl.program_id(0) == 0) & (pl.program_id(1) == 0))
    def _():
        run_sc[...] = jnp.zeros_like(run_sc)

    h = _norm_mod(x_ref[0], sh_ref[0], sc_ref[0])
    hb = h.astype(BF16)
    h_lo = (h - hb.astype(F32)).astype(BF16)
    rw = rw_ref[...]
    rw_hi = rw.astype(BF16)
    rw_lo = (rw - rw_hi.astype(F32)).astype(BF16)
    logits = jnp.dot(jnp.concatenate([hb, hb, h_lo], axis=1), jnp.concatenate([rw_hi, rw_lo, rw_hi], axis=0),
                     preferred_element_type=F32)
    scores = jax.nn.sigmoid(logits)
    hlo_ref[0], hhi_ref[0] = _pack_rows(h)
    hid = jax.nn.silu(jnp.dot(hb, sg_ref[...].astype(BF16), preferred_element_type=F32))
    hid = hid * jnp.dot(hb, su_ref[...].astype(BF16), preferred_element_type=F32)
    o_ref[0] = jnp.dot(hid.astype(BF16), sd_ref[...].astype(BF16), preferred_element_type=F32)

    work = scores + rb_ref[...]
    lane = lax.broadcasted_iota(jnp.int32, work.shape, 1)
    hits, ids = [], []
    for _ in range(TOP_K):
        m = jnp.max(work, axis=-1, keepdims=True)
        ik = jnp.min(jnp.where(work == m, lane, N_EXPERTS), axis=-1, keepdims=True)
        hit = lane == ik
        hits.append(hit)
        ids.append(ik)
        work = jnp.where(hit, -jnp.inf, work)
    mask = hits[0]
    for hit in hits[1:]:
        mask = jnp.logical_or(mask, hit)
    maskf = mask.astype(F32)
    before = jnp.dot(tri_ref[...], maskf.astype(BF16), preferred_element_type=F32) + run_sc[...]
    sel = [jnp.sum(jnp.where(hit, scores, 0.0), axis=-1, keepdims=True) for hit in hits]
    denom = sel[0]
    for s in sel[1:]:
        denom = denom + s
    ranks = [jnp.sum(jnp.where(hit, before, 0.0), axis=-1, keepdims=True) for hit in hits]
    lane128 = lax.broadcasted_iota(jnp.int32, (MOE_TL, 128), 1)
    idx_ref[0] = _slot_columns(ids, lane128)
    wt_ref[0] = _slot_columns([s / denom * ROUTE_SCALE for s in sel], lane128)
    rank_ref[0] = _slot_columns([r.astype(jnp.int32) for r in ranks], lane128)
    run_sc[...] += jnp.sum(maskf, axis=0, keepdims=True)
    cnt_ref[...] = run_sc[...]


def _moe_in_call(x, sh, sc, rw, rb, sg, su, sd):
    t3 = lambda b, j: (b, j, 0)
    full2 = lambda b, j: (0, 0)
    per_b = lambda b, j: (b, 0, 0)
    ff = sg.shape[1]
    tri = jnp.asarray(np.tril(np.ones((MOE_TL, MOE_TL), np.float32), -1), BF16)
    slot = jax.ShapeDtypeStruct((BATCH, SEQ, SLOT_PAD), jnp.int32)
    return pl.pallas_call(
        _moe_in_kernel, grid=(BATCH, SEQ // MOE_TL),
        in_specs=[pl.BlockSpec((1, MOE_TL, D_MODEL), t3),
                  pl.BlockSpec((1, 1, D_MODEL), per_b),
                  pl.BlockSpec((1, 1, D_MODEL), per_b),
                  pl.BlockSpec((D_MODEL, N_EXPERTS), full2),
                  pl.BlockSpec((1, N_EXPERTS), full2),
                  pl.BlockSpec((D_MODEL, ff), full2),
                  pl.BlockSpec((D_MODEL, ff), full2),
                  pl.BlockSpec((ff, D_MODEL), full2),
                  pl.BlockSpec((MOE_TL, MOE_TL), full2)],
        out_specs=[pl.BlockSpec((1, MOE_TL, ROW_WORDS), t3),
                   pl.BlockSpec((1, MOE_TL, ROW_WORDS), t3),
                   pl.BlockSpec((1, MOE_TL, SLOT_PAD), t3),
                   pl.BlockSpec((1, MOE_TL, SLOT_PAD), t3),
                   pl.BlockSpec((1, MOE_TL, SLOT_PAD), t3),
                   pl.BlockSpec((1, N_EXPERTS), full2),
                   pl.BlockSpec((1, MOE_TL, D_MODEL), t3)],
        out_shape=[jax.ShapeDtypeStruct((BATCH, SEQ, ROW_WORDS), jnp.uint32),
                   jax.ShapeDtypeStruct((BATCH, SEQ, ROW_WORDS), jnp.uint32),
                   slot,
                   jax.ShapeDtypeStruct((BATCH, SEQ, SLOT_PAD), F32),
                   slot,
                   jax.ShapeDtypeStruct((1, N_EXPERTS), F32),
                   jax.ShapeDtypeStruct((BATCH, SEQ, D_MODEL), F32)],
        scratch_shapes=[pltpu.VMEM((1, N_EXPERTS), F32)],
        compiler_params=_params(("arbitrary", "arbitrary")), name="moe_router_shared",
    )(x, sh, sc, rw, rb, sg, su, sd, tri)


def _sc_mesh():
    return plsc.VectorSubcoreMesh(core_axis_name="c", subcore_axis_name="s")


def _sc_dispatch(h_words, dest, n_rows):
    n_tok = h_words.shape[0]

    @pl.kernel(out_type=jax.ShapeDtypeStruct((n_rows, ROW_WORDS), jnp.uint32), mesh=_sc_mesh(), scratch_types=[])
    def scatter_rows(h_hbm, i_hbm, o_hbm):
        def body(h_vmem, i_vmem):
            pltpu.sync_copy(h_vmem, o_hbm.at[i_vmem.at[0]])

        pltpu.emit_pipeline(
            body, grid=(TOP_K, n_tok // SC_WINDOW),
            in_specs=[pl.BlockSpec((SC_WINDOW, ROW_WORDS), index_map=lambda k, i: (i, 0)),
                      pl.BlockSpec((1, SC_WINDOW), index_map=lambda k, i: (k, i))],
            out_specs=[],
            core_axis_name=("c", "s"), dimension_semantics=(pltpu.PARALLEL, pltpu.PARALLEL),
        )(h_hbm, i_hbm)

    return scatter_rows(h_words, dest)


def _sc_collect(y_words, dest):
    n_tok = dest.shape[1]

    @pl.kernel(out_type=jax.ShapeDtypeStruct((TOP_K, n_tok, ROW_WORDS), jnp.uint32), mesh=_sc_mesh(),
               scratch_types=[])
    def gather_rows(y_hbm, i_hbm, o_hbm):
        def body(i_vmem, o_vmem):
            pltpu.sync_copy(y_hbm.at[i_vmem.at[0]], o_vmem.at[0])

        pltpu.emit_pipeline(
            body, grid=(TOP_K, n_tok // SC_WINDOW),
            in_specs=[pl.BlockSpec((1, SC_WINDOW), index_map=lambda k, i: (k, i))],
            out_specs=[pl.BlockSpec((1, SC_WINDOW, ROW_WORDS), index_map=lambda k, i: (k, i, 0))],
            core_axis_name=("c", "s"), dimension_semantics=(pltpu.PARALLEL, pltpu.PARALLEL),
        )(i_hbm, o_hbm)

    return gather_rows(y_words, dest)


def _expert_kernel(be_ref, nv_ref, xlo_ref, xhi_ref, wg_ref, wu_ref, wd_ref, ylo_ref, yhi_ref):
    del be_ref
    nv = nv_ref[pl.program_id(0)]

    @pl.when(nv > 0)
    def _():
        parts = _unpack_rows(xlo_ref[...], xhi_ref[...])
        xb = jnp.concatenate([p.astype(BF16) for p in parts], axis=1)
        live = lax.broadcasted_iota(jnp.int32, xb.shape, 0) < nv
        xb = jnp.where(live, xb, jnp.zeros_like(xb))
        hid = jax.nn.silu(jnp.dot(xb, wg_ref[0, 0].astype(BF16), preferred_element_type=F32))
        hid = hid * jnp.dot(xb, wu_ref[0, 0].astype(BF16), preferred_element_type=F32)
        y = jnp.dot(hid.astype(BF16), wd_ref[0, 0].astype(BF16), preferred_element_type=F32)
        ylo_ref[...], yhi_ref[...] = _pack_rows(y)

    @pl.when(nv == 0)
    def _():
        ylo_ref[...] = jnp.zeros_like(ylo_ref)
        yhi_ref[...] = jnp.zeros_like(yhi_ref)


def _expert_call(block_e, n_valid, xlo, xhi, wg, wu, wd, li):
    n_rows = xlo.shape[0]
    n_blocks = n_rows // MOE_BLOCK
    rows = pl.BlockSpec((MOE_BLOCK, ROW_WORDS), lambda i, be, nv: (i, 0))
    grid_spec = pltpu.PrefetchScalarGridSpec(
        num_scalar_prefetch=2, grid=(n_blocks,),
        in_specs=[rows, rows,
                  pl.BlockSpec((1, 1, D_MODEL, EXPERT_FF), lambda i, be, nv: (li, be[i], 0, 0)),
                  pl.BlockSpec((1, 1, D_MODEL, EXPERT_FF), lambda i, be, nv: (li, be[i], 0, 0)),
                  pl.BlockSpec((1, 1, EXPERT_FF, D_MODEL), lambda i, be, nv: (li, be[i], 0, 0))],
        out_specs=[rows, rows])
    out = jax.ShapeDtypeStruct((n_rows, ROW_WORDS), jnp.uint32)
    return pl.pallas_call(
        _expert_kernel, grid_spec=grid_spec, out_shape=[out, out],
        compiler_params=_params(("arbitrary",)), name="moe_experts",
    )(block_e, n_valid, xlo, xhi, wg, wu, wd)


def _combine_kernel(ylo_ref, yhi_ref, w_ref, s_ref, x_ref, g_ref, o_ref):
    w = w_ref[0]
    acc = [None] * 4
    for k in range(TOP_K):
        wk = w[:, k:k + 1]
        for c, part in enumerate(_unpack_rows(ylo_ref[k], yhi_ref[k])):
            acc[c] = wk * part if acc[c] is None else acc[c] + wk * part
    for c in range(4):
        sl = slice(c * ROW_WORDS, (c + 1) * ROW_WORDS)
        o_ref[0, :, sl] = x_ref[0, :, sl] + g_ref[0, :, sl] * (acc[c] + s_ref[0, :, sl])


def _combine_call(ylo, yhi, wts, shared, x, g2):
    t3 = lambda b, j: (b, j, 0)
    nt = SEQ // MOE_TL
    rows = pl.BlockSpec((TOP_K, MOE_TL, ROW_WORDS), lambda b, j: (0, b * nt + j, 0))
    return pl.pallas_call(
        _combine_kernel, grid=(BATCH, nt),
        in_specs=[rows, rows,
                  pl.BlockSpec((1, MOE_TL, SLOT_PAD), t3),
                  pl.BlockSpec((1, MOE_TL, D_MODEL), t3),
                  pl.BlockSpec((1, MOE_TL, D_MODEL), t3),
                  pl.BlockSpec((1, 1, D_MODEL), lambda b, j: (b, 0, 0))],
        out_specs=pl.BlockSpec((1, MOE_TL, D_MODEL), t3),
        out_shape=jax.ShapeDtypeStruct((BATCH, SEQ, D_MODEL), F32),
        compiler_params=_params(("parallel", "arbitrary")), name="moe_combine",
    )(ylo, yhi, wts, shared, x, g2)


def _moe(x, sh, sc, g2, router_w, router_b, w_gate, w_up, w_down, sh_gate, sh_up, sh_down, li):
    T = BATCH * SEQ
    TK = T * TOP_K
    hlo, hhi, idx, wts, rank, counts, shared = _moe_in_call(
        x, sh, sc, router_w, router_b[None, :], sh_gate, sh_up, sh_down)
    counts = counts[0].astype(jnp.int32)
    padded = (counts + MOE_BLOCK - 1) // MOE_BLOCK * MOE_BLOCK
    pad_end = jnp.cumsum(padded)
    pad_start = pad_end - padded
    n_blocks = -(-TK // MOE_BLOCK) + N_EXPERTS
    n_rows = n_blocks * MOE_BLOCK
    block_start = jnp.arange(n_blocks, dtype=jnp.int32) * MOE_BLOCK
    block_e = jnp.sum((block_start[:, None] >= pad_end[None, :]).astype(jnp.int32), axis=1)
    block_e = jnp.minimum(block_e, N_EXPERTS - 1)
    n_valid = jnp.clip(counts[block_e] - (block_start - pad_start[block_e]), 0, MOE_BLOCK).astype(jnp.int32)
    idx = idx.reshape(T, SLOT_PAD)[:, :TOP_K]
    dest = pad_start[idx] + rank.reshape(T, SLOT_PAD)[:, :TOP_K]
    dest = dest.T.astype(jnp.int32)
    xlo = _sc_dispatch(hlo.reshape(T, ROW_WORDS), dest, n_rows)
    xhi = _sc_dispatch(hhi.reshape(T, ROW_WORDS), dest, n_rows)
    ylo, yhi = _expert_call(block_e, n_valid, xlo, xhi, w_gate, w_up, w_down, li)
    return _combine_call(_sc_collect(ylo, dest), _sc_collect(yhi, dest), wts, shared, x, g2)


HY_TL = 512
HALO = 8


def _hy_in_kernel(x_ref, xp_ref, xn_ref, sh_ref, sc_ref, w_ref, cw_ref, cb_ref, z_ref, x0_ref, h_sc):
    j = pl.program_id(1)
    shift, scale = sh_ref[0], sc_ref[0]
    keep_prev = (j > 0).astype(F32)
    keep_next = (j < SEQ // HY_TL - 1).astype(F32)
    h_sc[0:HALO, :] = _norm_mod(xp_ref[0], shift, scale) * keep_prev
    h_sc[HALO:HALO + HY_TL, :] = _norm_mod(x_ref[0], shift, scale)
    h_sc[HALO + HY_TL:, :] = _norm_mod(xn_ref[0], shift, scale) * keep_next
    hcat = h_sc[...].astype(BF16)
    outs = []
    for part in range(3):
        sl = slice(part * HY_WIDTH, (part + 1) * HY_WIDTH)
        p = jnp.dot(hcat, w_ref[:, sl], preferred_element_type=F32)
        o = (p[HALO - 1:HALO - 1 + HY_TL] * cw_ref[0:1, sl] + p[HALO:HALO + HY_TL] * cw_ref[1:2, sl]
             + p[HALO + 1:HALO + 1 + HY_TL] * cw_ref[2:3, sl] + cb_ref[:, sl])
        outs.append(o)
    x0_ref[0] = outs[0].astype(BF16)
    z_ref[0] = (outs[2] * outs[1]).astype(BF16)


def _hy_in_call(x, sh, sc, w, cw, cb):
    nb8 = HY_TL // HALO
    t3 = lambda b, j: (b, j, 0)
    full2 = lambda b, j: (0, 0)
    per_b = lambda b, j: (b, 0, 0)
    return pl.pallas_call(
        _hy_in_kernel, grid=(BATCH, SEQ // HY_TL),
        in_specs=[pl.BlockSpec((1, HY_TL, D_MODEL), t3),
                  pl.BlockSpec((1, HALO, D_MODEL), lambda b, j: (b, jnp.maximum(j * nb8 - 1, 0), 0)),
                  pl.BlockSpec((1, HALO, D_MODEL), lambda b, j: (b, jnp.minimum((j + 1) * nb8, SEQ // HALO - 1), 0)),
                  pl.BlockSpec((1, 1, D_MODEL), per_b),
                  pl.BlockSpec((1, 1, D_MODEL), per_b),
                  pl.BlockSpec((D_MODEL, 3 * HY_WIDTH), full2),
                  pl.BlockSpec((SHORT_CONV, 3 * HY_WIDTH), full2),
                  pl.BlockSpec((1, 3 * HY_WIDTH), full2)],
        out_specs=[pl.BlockSpec((1, HY_TL, HY_WIDTH), t3), pl.BlockSpec((1, HY_TL, HY_WIDTH), t3)],
        out_shape=[jax.ShapeDtypeStruct((BATCH, SEQ, HY_WIDTH), BF16),
                   jax.ShapeDtypeStruct((BATCH, SEQ, HY_WIDTH), BF16)],
        scratch_shapes=[pltpu.VMEM((HY_TL + 2 * HALO, D_MODEL), F32)],
        compiler_params=_params(("parallel", "arbitrary")), name="hyena_in_proj",
    )(x, x, x, sh, sc, w, cw, cb)


def _fft_tables():
    c = np.arange(FFT_N2, dtype=np.int64)
    ang = 2.0 * np.pi * ((c[:, None] * c[None, :]) % FFT_N2) / FFT_N2
    sr, si = np.cos(ang), -np.sin(ang)
    m = np.block([[sr, -si], [si, sr]])
    k1 = np.arange(FFT_NK, dtype=np.int64)
    ang_t = 2.0 * np.pi * (k1[:, None] * c[None, :]) / DFT_N
    lanes = np.ones((1, 1, 128))
    tr = np.cos(ang_t)[:, :, None] * lanes
    ti = -np.sin(ang_t)[:, :, None] * lanes
    return jnp.asarray(m, BF16), jnp.asarray(tr, F32), jnp.asarray(ti, F32)


def _lin(acc, coef, val):
    if abs(coef) < 1e-12:
        return acc
    term = val if coef == 1.0 else (-val if coef == -1.0 else coef * val)
    return term if acc is None else acc + term


def _twiddle(tr_ref, ti_ref, k1, width):
    reps = width // 128
    tr, ti = tr_ref[k1], ti_ref[k1]
    return jnp.concatenate([tr] * reps, axis=1), jnp.concatenate([ti] * reps, axis=1)


def _class_spectrum(block, k1, m_ref, tr_ref, ti_ref, width):
    yr = yi = None
    for a in range(FFT_NA):
        th = 2.0 * math.pi * ((a * k1) % FFT_N1) / FFT_N1
        za = block(a)
        yr = _lin(yr, round(math.cos(th), 15), za)
        yi = _lin(yi, round(-math.sin(th), 15), za)
    if k1 > 0:
        tr, ti = _twiddle(tr_ref, ti_ref, k1, width)
        yr, yi = (yr * tr, yr * ti) if yi is None else (yr * tr - yi * ti, yr * ti + yi * tr)
    if yi is None:
        x = jnp.dot(m_ref[:, :FFT_N2], yr.astype(BF16), preferred_element_type=F32)
    else:
        x = jnp.dot(m_ref[...], jnp.concatenate([yr, yi], axis=0).astype(BF16), preferred_element_type=F32)
    return x[:FFT_N2], x[FFT_N2:]


def _class_inverse(yr, yi, k1, m_ref, tr_ref, ti_ref, acc_ref, width):
    v = jnp.dot(m_ref[...], jnp.concatenate([yr, -yi], axis=0).astype(BF16), preferred_element_type=F32)
    ur, ui = v[:FFT_N2], -v[FFT_N2:]
    if k1 > 0:
        tr, ti = _twiddle(tr_ref, ti_ref, k1, width)
        ur, ui = ur * tr + ui * ti, ui * tr - ur * ti
    scale = (1.0 if k1 in (0, FFT_N1 // 2) else 2.0) / DFT_N
    for a in range(FFT_NA):
        th = 2.0 * math.pi * ((a * k1) % FFT_N1) / FFT_N1
        term = _lin(None, round(math.cos(th), 15) * scale, ur)
        term = _lin(term, round(-math.sin(th), 15) * scale, ui)
        rows = slice(a * FFT_N2, (a + 1) * FFT_N2)
        if k1 == 0:
            acc_ref[rows, :] = term
        else:
            acc_ref[rows, :] += term


def _spec_kernel(hf_ref, hb_ref, m_ref, tr_ref, ti_ref, c_ref):
    for k1 in range(FFT_NK):
        fr, fi = _class_spectrum(lambda a: hf_ref[a * FFT_N2:(a + 1) * FFT_N2, :], k1, m_ref, tr_ref, ti_ref, HY_CT)
        br, bi = _class_spectrum(lambda a: hb_ref[a * FFT_N2:(a + 1) * FFT_N2, :], k1, m_ref, tr_ref, ti_ref, HY_CT)
        c_ref[k1, :FFT_N2, :] = (fr + br).astype(BF16)
        c_ref[k1, FFT_N2:, :] = (fi - bi).astype(BF16)


def _fft_table_specs(ngrid):
    z = (0,) * 2
    z3 = (0,) * 3
    if ngrid == 1:
        return [pl.BlockSpec((2 * FFT_N2, 2 * FFT_N2), lambda c: z),
                pl.BlockSpec((FFT_NK, FFT_N2, 128), lambda c: z3),
                pl.BlockSpec((FFT_NK, FFT_N2, 128), lambda c: z3)]
    return [pl.BlockSpec((2 * FFT_N2, 2 * FFT_N2), lambda b, c: z),
            pl.BlockSpec((FFT_NK, FFT_N2, 128), lambda b, c: z3),
            pl.BlockSpec((FFT_NK, FFT_N2, 128), lambda b, c: z3)]


def _spec_call(hfb, m, tr, ti):
    nct = HY_WIDTH // HY_CT
    return pl.pallas_call(
        _spec_kernel, grid=(nct,),
        in_specs=[pl.BlockSpec((SEQ, HY_CT), lambda c: (0, c)),
                  pl.BlockSpec((SEQ, HY_CT), lambda c: (0, c + nct))] + _fft_table_specs(1),
        out_specs=pl.BlockSpec((FFT_NK, 2 * FFT_N2, HY_CT), lambda c: (0, 0, c)),
        out_shape=jax.ShapeDtypeStruct((FFT_NK, 2 * FFT_N2, HY_WIDTH), BF16),
        compiler_params=_params(("arbitrary",)), name="hyena_filter_spectrum",
    )(hfb, hfb, m, tr, ti)


def _conv_kernel(z_ref, c_ref, m_ref, tr_ref, ti_ref, y_ref, acc):
    for k1 in range(FFT_NK):
        xr, xi = _class_spectrum(lambda a: z_ref[0, a * FFT_N2:(a + 1) * FFT_N2, :].astype(F32), k1,
                                 m_ref, tr_ref, ti_ref, HY_CT)
        cr = c_ref[k1, :FFT_N2, :].astype(F32)
        ci = c_ref[k1, FFT_N2:, :].astype(F32)
        _class_inverse(xr * cr - xi * ci, xr * ci + xi * cr, k1, m_ref, tr_ref, ti_ref, acc, HY_CT)
    y_ref[0] = acc[...].astype(BF16)


def _conv_call(z, spec, m, tr, ti):
    return pl.pallas_call(
        _conv_kernel, grid=(HY_WIDTH // HY_CT, BATCH),
        in_specs=[pl.BlockSpec((1, SEQ, HY_CT), lambda c, b: (b, 0, c)),
                  pl.BlockSpec((FFT_NK, 2 * FFT_N2, HY_CT), lambda c, b: (0, 0, c))] + _fft_table_specs(2),
        out_specs=pl.BlockSpec((1, SEQ, HY_CT), lambda c, b: (b, 0, c)),
        out_shape=jax.ShapeDtypeStruct((BATCH, SEQ, HY_WIDTH), BF16),
        scratch_shapes=[pltpu.VMEM((SEQ, HY_CT), F32)],
        compiler_params=_params(("parallel", "arbitrary")), name="hyena_long_conv",
    )(z, spec, m, tr, ti)


def _hy_out_kernel(y_ref, z_ref, x0_ref, b_ref, w_ref, x_ref, g_ref, o_ref):
    z = z_ref[0].astype(F32)
    gated = x0_ref[0].astype(F32) * (y_ref[0].astype(F32) + b_ref[...] * z)
    mix = jnp.dot(gated.astype(BF16), w_ref[...], preferred_element_type=F32)
    o_ref[0] = x_ref[0] + g_ref[0] * mix


def _hy_out_call(y, z, x0, bias, w, x, g1):
    t3 = lambda b, j: (b, j, 0)
    full2 = lambda b, j: (0, 0)
    return pl.pallas_call(
        _hy_out_kernel, grid=(BATCH, SEQ // HY_TL),
        in_specs=[pl.BlockSpec((1, HY_TL, HY_WIDTH), t3),
                  pl.BlockSpec((1, HY_TL, HY_WIDTH), t3),
                  pl.BlockSpec((1, HY_TL, HY_WIDTH), t3),
                  pl.BlockSpec((1, HY_WIDTH), full2),
                  pl.BlockSpec((HY_WIDTH, D_MODEL), full2),
                  pl.BlockSpec((1, HY_TL, D_MODEL), t3),
                  pl.BlockSpec((1, 1, D_MODEL), lambda b, j: (b, 0, 0))],
        out_specs=pl.BlockSpec((1, HY_TL, D_MODEL), t3),
        out_shape=jax.ShapeDtypeStruct((BATCH, SEQ, D_MODEL), F32),
        compiler_params=_params(("parallel", "arbitrary")), name="hyena_out_proj",
    )(y, z, x0, bias, w, x, g1)


def _hyena_filter(w1, b1, w2, b2, w3, freq):
    hi = lax.Precision.HIGHEST
    Lq = SEQ
    t = jnp.linspace(0.0, 1.0, Lq, dtype=F32)[:, None]
    ang = 2.0 * math.pi * jnp.arange(Lq, dtype=F32)[:, None] / Lq
    bands = jnp.linspace(1e-4, FILT_BANDS - 1, FILT_BANDS, dtype=F32)
    z = jnp.concatenate([t, jnp.cos(bands * ang), -jnp.sin(bands * ang)], axis=-1)
    hid = jnp.sin(freq * (jnp.dot(z, w1, precision=hi) + b1))
    hid = jnp.sin(freq * (jnp.dot(hid, w2, precision=hi) + b2))
    hf = jnp.dot(hid, w3, precision=hi).reshape(Lq, 2, HY_WIDTH)
    deltas = jnp.linspace(HY_MIN_DECAY, HY_MAX_DECAY, HY_WIDTH, dtype=F32)
    hf = hf * jnp.exp(-t * deltas)[:, None, :]
    hf = hf * lax.rsqrt(jnp.sum(hf * hf, axis=(0, 1), keepdims=True) + EPS)
    return hf.reshape(Lq, 2 * HY_WIDTH)


def kernel(x, c, ctx, c_ctx, ada_w, ada_b, ev_w_in, mla_q_norm, mla_w_uq, mla_kv_norm, mla_w_ukv, mla_q_qknorm, mla_k_qknorm, s5_lam_re, s5_lam_im, s5_log_step, s5_b_re, s5_b_im, s5_c_re, s5_c_im, s5_d, s5_glu_w, s5_glu_b, ev_w_out, hy_w_in, hy_conv_w, hy_conv_b, hy_f_w1, hy_f_b1, hy_f_w2, hy_f_b2, hy_f_w3, hy_f_freq, hy_bias, hy_w_out, moe_router_w, moe_router_b, moe_w_gate, moe_w_up, moe_w_down, moe_sh_gate, moe_sh_up, moe_sh_down):
    hi = lax.Precision.HIGHEST
    D = D_MODEL
    sc = jax.nn.silu(c)
    sc_ctx = jax.nn.silu(c_ctx)

    def mods(li):
        mod = jnp.dot(sc, ada_w[li], precision=hi) + ada_b[li]
        return [m[:, None, :] for m in jnp.split(mod, 6, axis=-1)]

    sh1, sc1, g1, sh2, sc2, g2 = mods(0)
    mod_ctx = jnp.dot(sc_ctx, ada_w[0][:, :2 * D], precision=hi) + ada_b[0][:2 * D]
    w0 = _a0_weights(ev_w_in[0], mla_q_norm[0], mla_w_uq[0], mla_kv_norm[0], mla_w_ukv[0],
                     mla_q_qknorm[0], mla_k_qknorm[0])
    q, k, v, u = _a0_call(x, ctx, sh1, sc1, mod_ctx[None, :D], mod_ctx[None, D:], w0)
    att = _attn_call(q, k, v)
    ug = u.astype(BF16).reshape(BATCH, S5_NCHUNK, S5_CHUNK, S5_GROUPS, S5_GROUP)
    ug = ug.transpose(3, 1, 0, 2, 4).reshape(S5_GROUPS, S5_ROWS, S5_COLS)
    ys = _s5_call(ug, *_s5_weights(s5_lam_re[0], s5_lam_im[0], s5_log_step[0], s5_b_re[0], s5_b_im[0],
                                   s5_c_re[0], s5_c_im[0]))
    ys = ys.reshape(S5_GROUPS, S5_NCHUNK_LAT, BATCH, S5_CHUNK, S5_GROUP)
    ys = ys.transpose(2, 1, 3, 0, 4).reshape(BATCH, SEQ, S5_WIDTH)
    wo = ev_w_out[0].astype(BF16)
    x = _a1_call(att, ys, u, s5_d[0][None, :], s5_glu_w[0].astype(BF16), s5_glu_b[0][None, :],
                 wo[:MLA_WIDTH], wo[MLA_WIDTH:], x, g1)
    x = _moe(x, sh2, sc2, g2, moe_router_w[0], moe_router_b[0], moe_w_gate, moe_w_up, moe_w_down,
             moe_sh_gate[0], moe_sh_up[0], moe_sh_down[0], 0)

    sh1, sc1, g1, sh2, sc2, g2 = mods(1)
    z, x0 = _hy_in_call(x, sh1, sc1, hy_w_in[0].astype(BF16), hy_conv_w[0], hy_conv_b[0][None, :])
    fft_tabs = _fft_tables()
    hfb = _hyena_filter(hy_f_w1[0], hy_f_b1[0], hy_f_w2[0], hy_f_b2[0], hy_f_w3[0], hy_f_freq[0])
    y = _conv_call(z, _spec_call(hfb, *fft_tabs), *fft_tabs)
    x = _hy_out_call(y, z, x0, hy_bias[0][None, :], hy_w_out[0].astype(BF16), x, g1)
    x = _moe(x, sh2, sc2, g2, moe_router_w[1], moe_router_b[1], moe_w_gate, moe_w_up, moe_w_down,
             moe_sh_gate[1], moe_sh_up[1], moe_sh_down[1], 1)
    return x
```

```python
import functools
import math

import numpy as np
import jax
import jax.numpy as jnp
from jax import lax
from jax.experimental import pallas as pl
from jax.experimental.pallas import tpu as pltpu
from jax.experimental.pallas import tpu_sc as plsc

F32 = jnp.float32
BF16 = jnp.bfloat16

D_MODEL = 1024
BATCH = 8
SEQ = 4096
CTX_LEN = 256
KV_LEN = SEQ + CTX_LEN
GRID_W = 64
EPS = 1e-6

MLA_HEADS = 8
QK_NOPE = 64
QK_ROPE = 32
QK_HEAD = QK_NOPE + QK_ROPE
V_HEAD = 64
Q_LORA = 256
KV_LORA = 128
MLA_WIDTH = MLA_HEADS * V_HEAD
ROPE_BASE = 10000.0
HEAD_PAD = 128

S5_WIDTH = 512
S5_GROUP = 16
S5_GROUPS = S5_WIDTH // S5_GROUP
S5_STATE = 64
S5_CHUNK = 32
S5_NCHUNK = KV_LEN // S5_CHUNK
S5_NCHUNK_LAT = SEQ // S5_CHUNK
S5_NCHUNK_CTX = CTX_LEN // S5_CHUNK

HY_WIDTH = D_MODEL
FILT_EMB = 33
FILT_BANDS = (FILT_EMB - 1) // 2
SHORT_CONV = 3
HY_MIN_DECAY = -math.log(1e-2) / 1.5
HY_MAX_DECAY = -math.log(1e-2) / 0.3
DFT_N = 2 * SEQ
FFT_N1 = 16
FFT_N2 = DFT_N // FFT_N1
FFT_NA = FFT_N1 // 2
FFT_NK = FFT_N1 // 2 + 1
HY_CT = 256

N_EXPERTS = 64
TOP_K = 6
EXPERT_FF = 256
ROUTE_SCALE = 2.5
MOE_BLOCK = 512
ROW_WORDS = D_MODEL // 4
SC_WINDOW = 128

V7X_VMEM_BYTES = 64 * 1024 * 1024
VMEM_LIMIT = V7X_VMEM_BYTES - 8 * 1024 * 1024


def _params(semantics):
    return pltpu.CompilerParams(dimension_semantics=semantics, vmem_limit_bytes=VMEM_LIMIT)


def _norm_mod(x, shift, scale):
    ms = jnp.mean(x * x, axis=-1, keepdims=True)
    return x * lax.rsqrt(ms + EPS) * (1.0 + scale) + shift


def _rms(x, gain, n):
    ms = jnp.sum(x * x, axis=-1, keepdims=True) * (1.0 / n)
    return x * lax.rsqrt(ms + EPS) * gain


A0_TL = 256
A0_NT = SEQ // A0_TL


def _rope_perm():
    return np.concatenate([np.arange(0, QK_ROPE, 2), np.arange(1, QK_ROPE, 2)])


def _rope_tables():
    t = np.arange(SEQ)
    row = (t // GRID_W).astype(np.float64)
    col = (t % GRID_W).astype(np.float64)
    n_freq = QK_ROPE // 4
    inv = ROPE_BASE ** (-np.arange(n_freq, dtype=np.float64) / n_freq)
    ang = np.concatenate([row[:, None] * inv, col[:, None] * inv], axis=-1)
    cos, sin = np.cos(ang), np.sin(ang)
    half = QK_ROPE // 2
    a = np.zeros((KV_LEN, HEAD_PAD))
    bp = np.zeros((KV_LEN, HEAD_PAD))
    bm = np.zeros((KV_LEN, HEAD_PAD))
    a[:, :QK_HEAD] = 1.0
    a[:SEQ, QK_NOPE:QK_NOPE + half] = cos
    a[:SEQ, QK_NOPE + half:QK_HEAD] = cos
    bp[:SEQ, QK_NOPE + half:QK_HEAD] = sin
    bm[:SEQ, QK_NOPE:QK_NOPE + half] = -sin
    return a, bp, bm


def _a0_kernel(x_ref, ctx_ref, sh_ref, sc_ref, shc_ref, scc_ref, win_ref, qn_ref, wuq_ref, kvn_ref,
               wk_ref, wuv_ref, qg_ref, kg_ref, ka_ref, kp_ref, km_ref, qa_ref, qp_ref, qm_ref,
               q_ref, k_ref, v_ref, u_ref):
    j = pl.program_id(1)
    is_ctx = j == A0_NT
    xin = jnp.where(is_ctx, ctx_ref[0], x_ref[0])
    shift = jnp.where(is_ctx, shc_ref[...], sh_ref[0])
    scale = jnp.where(is_ctx, scc_ref[...], sc_ref[0])
    h = _norm_mod(xin, shift, scale).astype(BF16)
    proj = jnp.dot(h, win_ref[...], preferred_element_type=F32)
    u_ref[0] = proj[:, 512:].astype(BF16)

    c_kv = _rms(proj[:, Q_LORA:Q_LORA + KV_LORA], kvn_ref[...], KV_LORA).astype(BF16)
    v_ref[0] = jnp.dot(c_kv, wuv_ref[...], preferred_element_type=F32).astype(BF16)
    kin = jnp.concatenate([c_kv, proj[:, 384:512].astype(BF16)], axis=1)
    kf = jnp.dot(kin, wk_ref[...], preferred_element_type=F32)
    ka, kp, km = ka_ref[...], kp_ref[...], km_ref[...]
    for hd in range(MLA_HEADS):
        sl = slice(hd * HEAD_PAD, (hd + 1) * HEAD_PAD)
        kh = _rms(kf[:, sl], kg_ref[:, sl], QK_HEAD)
        kh = kh * ka + pltpu.roll(kh, 16, 1) * kp + pltpu.roll(kh, HEAD_PAD - 16, 1) * km
        k_ref[0, :, sl] = kh.astype(BF16)

    @pl.when(j < A0_NT)
    def _():
        ql = _rms(proj[:, :Q_LORA], qn_ref[...], Q_LORA).astype(BF16)
        qf = jnp.dot(ql, wuq_ref[...], preferred_element_type=F32)
        qa, qp, qm = qa_ref[...], qp_ref[...], qm_ref[...]
        for hd in range(MLA_HEADS):
            sl = slice(hd * HEAD_PAD, (hd + 1) * HEAD_PAD)
            qh = _rms(qf[:, sl], qg_ref[:, sl], QK_HEAD)
            qh = qh * qa + pltpu.roll(qh, 16, 1) * qp + pltpu.roll(qh, HEAD_PAD - 16, 1) * qm
            q_ref[0, :, sl] = qh.astype(BF16)


def _a0_call(x, ctx, sh, sc, shc, scc, w):
    nt = A0_NT
    lat = lambda b, j: (b, jnp.minimum(j, nt - 1), 0)
    full2 = lambda b, j: (0, 0)
    per_b = lambda b, j: (b, 0, 0)
    tab = pl.BlockSpec((A0_TL, HEAD_PAD), lambda b, j: (j, 0))
    in_specs = [
        pl.BlockSpec((1, A0_TL, D_MODEL), lat),
        pl.BlockSpec((1, CTX_LEN, D_MODEL), per_b),
        pl.BlockSpec((1, 1, D_MODEL), per_b),
        pl.BlockSpec((1, 1, D_MODEL), per_b),
        pl.BlockSpec((1, D_MODEL), full2),
        pl.BlockSpec((1, D_MODEL), full2),
        pl.BlockSpec((D_MODEL, 1024), full2),
        pl.BlockSpec((1, Q_LORA), full2),
        pl.BlockSpec((Q_LORA, MLA_HEADS * HEAD_PAD), full2),
        pl.BlockSpec((1, KV_LORA), full2),
        pl.BlockSpec((2 * KV_LORA, MLA_HEADS * HEAD_PAD), full2),
        pl.BlockSpec((KV_LORA, MLA_WIDTH), full2),
        pl.BlockSpec((1, MLA_HEADS * HEAD_PAD), full2),
        pl.BlockSpec((1, MLA_HEADS * HEAD_PAD), full2),
        tab, tab, tab, tab, tab, tab,
    ]
    out_specs = [
        pl.BlockSpec((1, A0_TL, MLA_HEADS * HEAD_PAD), lat),
        pl.BlockSpec((1, A0_TL, MLA_HEADS * HEAD_PAD), lambda b, j: (b, j, 0)),
        pl.BlockSpec((1, A0_TL, MLA_WIDTH), lambda b, j: (b, j, 0)),
        pl.BlockSpec((1, A0_TL, S5_WIDTH), lambda b, j: (b, j, 0)),
    ]
    out_shape = [
        jax.ShapeDtypeStruct((BATCH, SEQ, MLA_HEADS * HEAD_PAD), BF16),
        jax.ShapeDtypeStruct((BATCH, KV_LEN, MLA_HEADS * HEAD_PAD), BF16),
        jax.ShapeDtypeStruct((BATCH, KV_LEN, MLA_WIDTH), BF16),
        jax.ShapeDtypeStruct((BATCH, KV_LEN, S5_WIDTH), BF16),
    ]
    return pl.pallas_call(
        _a0_kernel, grid=(BATCH, nt + 1), in_specs=in_specs, out_specs=out_specs, out_shape=out_shape,
        compiler_params=_params(("parallel", "arbitrary")), name="even_in_proj",
    )(x, ctx, sh, sc, shc, scc, *w)


def _a0_weights(w_in, q_norm, w_uq, kv_norm, w_ukv, q_qk, k_qk):
    perm = _rope_perm()
    kr0 = Q_LORA + KV_LORA
    w_cat = jnp.concatenate([
        w_in[:, :kr0], w_in[:, kr0:kr0 + QK_ROPE][:, perm],
        jnp.zeros((D_MODEL, HEAD_PAD - QK_ROPE), F32), w_in[:, kr0 + QK_ROPE:]], axis=1).astype(BF16)
    pad = HEAD_PAD - QK_HEAD

    def head_gain(g):
        gh = jnp.concatenate([g[:QK_NOPE], g[QK_NOPE:][perm], jnp.zeros((pad,), F32)])
        return jnp.tile(gh, MLA_HEADS)[None, :]

    uq = w_uq.reshape(Q_LORA, MLA_HEADS, QK_HEAD)
    uq = jnp.concatenate([uq[..., :QK_NOPE], uq[..., QK_NOPE:][..., perm],
                          jnp.zeros((Q_LORA, MLA_HEADS, pad), F32)], axis=-1)
    uq = uq.reshape(Q_LORA, MLA_HEADS * HEAD_PAD).astype(BF16)
    ukv = w_ukv.reshape(KV_LORA, MLA_HEADS, QK_NOPE + V_HEAD)
    uk = jnp.concatenate([ukv[..., :QK_NOPE], jnp.zeros((KV_LORA, MLA_HEADS, HEAD_PAD - QK_NOPE), F32)], axis=-1)
    uk = uk.reshape(KV_LORA, MLA_HEADS * HEAD_PAD)
    place = np.zeros((KV_LORA, MLA_HEADS, HEAD_PAD), np.float32)
    for i in range(QK_ROPE):
        place[i, :, QK_NOPE + i] = 1.0
    wk = jnp.concatenate([uk, jnp.asarray(place.reshape(KV_LORA, MLA_HEADS * HEAD_PAD))], axis=0).astype(BF16)
    wuv = ukv[..., QK_NOPE:].reshape(KV_LORA, MLA_WIDTH).astype(BF16)
    a, bp, bm = _rope_tables()
    qs = QK_HEAD ** -0.5
    tabs = [jnp.asarray(t, F32) for t in (a, bp, bm, a * qs, bp * qs, bm * qs)]
    return [w_cat, q_norm[None, :], uq, kv_norm[None, :], wk, wuv, head_gain(q_qk), head_gain(k_qk)] + tabs


ATT_TQ = 256
HEADS_PER_STEP = 2


def _attn_kernel(q_ref, k_ref, v_ref, o_ref):
    v = v_ref[0]
    lane_head = lax.broadcasted_iota(jnp.int32, v.shape, 1) // V_HEAD
    acc = None
    for hh in range(HEADS_PER_STEP):
        sl = slice(hh * HEAD_PAD, (hh + 1) * HEAD_PAD)
        s = lax.dot_general(q_ref[0, :, sl], k_ref[0, :, sl], (((1,), (1,)), ((), ())),
                            preferred_element_type=F32)
        m = jnp.max(s, axis=-1, keepdims=True)
        p = jnp.exp(s - m)
        l = jnp.sum(p, axis=-1, keepdims=True)
        vh = jnp.where(lane_head == hh, v, jnp.zeros_like(v))
        o = jnp.dot(p.astype(BF16), vh, preferred_element_type=F32) * (1.0 / l)
        acc = o if acc is None else acc + o
    o_ref[0] = acc.astype(BF16)


def _attn_call(q, k, v):
    wq = HEADS_PER_STEP * HEAD_PAD
    wv = HEADS_PER_STEP * V_HEAD
    return pl.pallas_call(
        _attn_kernel, grid=(BATCH, MLA_HEADS // HEADS_PER_STEP, SEQ // ATT_TQ),
        in_specs=[pl.BlockSpec((1, ATT_TQ, wq), lambda b, h, i: (b, i, h)),
                  pl.BlockSpec((1, KV_LEN, wq), lambda b, h, i: (b, 0, h)),
                  pl.BlockSpec((1, KV_LEN, wv), lambda b, h, i: (b, 0, h))],
        out_specs=pl.BlockSpec((1, ATT_TQ, wv), lambda b, h, i: (b, i, h)),
        out_shape=jax.ShapeDtypeStruct((BATCH, SEQ, MLA_WIDTH), BF16),
        compiler_params=_params(("parallel", "parallel", "arbitrary")), name="mla_attention",
    )(q, k, v)


S5_ROWS = S5_NCHUNK * BATCH
S5_ROWS_LAT = S5_NCHUNK_LAT * BATCH
S5_COLS = S5_CHUNK * S5_GROUP
S5_SW = 2 * S5_STATE


def _s5_kernel(u_ref, t_ref, mb_ref, mc_ref, coef_ref, y_ref, x_sc, sp_sc):
    u = u_ref[0]
    x_sc[...] = jnp.dot(u, mb_ref[0], preferred_element_type=F32)
    cf = coef_ref[0]
    af, bfm, bfp, ab, bbm, bbp = [cf[i * 8:(i + 1) * 8] for i in range(6)]

    def body(i, carry):
        sf, sfw, sb, sbw = carry
        cfw = jnp.where(i < S5_NCHUNK_CTX, i + S5_NCHUNK_LAT, i - S5_NCHUNK_CTX)
        rf = pl.multiple_of(cfw * BATCH, BATCH)
        rb = pl.multiple_of((S5_NCHUNK - 1 - i) * BATCH, BATCH)
        sp_sc[pl.ds(rf, BATCH), 0:S5_SW] = sf
        sp_sc[pl.ds(rb, BATCH), S5_SW:2 * S5_SW] = sb
        xf = x_sc[pl.ds(rf, BATCH), 0:S5_SW]
        xfw = x_sc[pl.ds(rf, BATCH), S5_SW:2 * S5_SW]
        xb = x_sc[pl.ds(rb, BATCH), 2 * S5_SW:3 * S5_SW]
        xbw = x_sc[pl.ds(rb, BATCH), 3 * S5_SW:4 * S5_SW]
        return (sf * af + sfw * bfm + xf, sfw * af + sf * bfp + xfw,
                sb * ab + sbw * bbm + xb, sbw * ab + sb * bbp + xbw)

    z = jnp.zeros((BATCH, S5_SW), F32)
    lax.fori_loop(0, S5_NCHUNK, body, (z, z, z, z))
    y = jnp.dot(u[:S5_ROWS_LAT], t_ref[0], preferred_element_type=F32)
    y = y + jnp.dot(sp_sc[0:S5_ROWS_LAT, :].astype(BF16), mc_ref[0], preferred_element_type=F32)
    y_ref[0] = y.astype(BF16)


def _s5_call(ug, t, mb, mc, coef):
    g3 = lambda g: (g, 0, 0)
    return pl.pallas_call(
        _s5_kernel, grid=(S5_GROUPS,),
        in_specs=[pl.BlockSpec((1, S5_ROWS, S5_COLS), g3),
                  pl.BlockSpec((1, S5_COLS, S5_COLS), g3),
                  pl.BlockSpec((1, S5_COLS, 4 * S5_SW), g3),
                  pl.BlockSpec((1, 2 * S5_SW, S5_COLS), g3),
                  pl.BlockSpec((1, 6 * 8, S5_SW), g3)],
        out_specs=pl.BlockSpec((1, S5_ROWS_LAT, S5_COLS), g3),
        out_shape=jax.ShapeDtypeStruct((S5_GROUPS, S5_ROWS_LAT, S5_COLS), BF16),
        scratch_shapes=[pltpu.VMEM((S5_ROWS, 4 * S5_SW), F32), pltpu.VMEM((S5_ROWS, 2 * S5_SW), F32)],
        compiler_params=_params(("parallel",)), name="s5_chunked_scan",
    )(ug, t, mb, mc, coef)


def _s5_weights(lam_re, lam_im, log_step, b_re, b_im, c_re, c_im):
    q = S5_CHUNK
    hi = lax.Precision.HIGHEST
    t_blocks, mbs, mcs, coefs = [], [], [], []
    sig = jnp.arange(q)
    for d in range(2):
        lr = jnp.minimum(lam_re[d], -1e-4)
        li = lam_im[d]
        step = jnp.exp(log_step[d])[:, None]
        jj = jnp.arange(q + 1, dtype=F32)[:, None, None]
        mag = jnp.exp(lr * step * jj)
        ph = li * step * jj
        pr, pi = mag * jnp.cos(ph), mag * jnp.sin(ph)
        nr, ni = pr[1] - 1.0, pi[1]
        den = lr * lr + li * li
        fr, fi = (nr * lr + ni * li) / den, (ni * lr - nr * li) / den
        br = fr[..., None] * b_re[d] - fi[..., None] * b_im[d]
        bi = fr[..., None] * b_im[d] + fi[..., None] * b_re[d]
        cr, ci = c_re[d], c_im[d]
        cpr = cr[None] * pr[:, :, None, :] - ci[None] * pi[:, :, None, :]
        cpi = cr[None] * pi[:, :, None, :] + ci[None] * pr[:, :, None, :]
        kern = (jnp.einsum('jghp,gpk->jghk', cpr[:q], br, precision=hi)
                - jnp.einsum('jghp,gpk->jghk', cpi[:q], bi, precision=hi))
        lag = (sig[None, :] - sig[:, None]) if d == 0 else (sig[:, None] - sig[None, :])
        kt = kern.transpose(1, 0, 3, 2)
        tb = jnp.where((lag >= 0)[None, :, :, None, None], kt[:, jnp.clip(lag, 0, q - 1)], 0.0)
        t_blocks.append(tb.transpose(0, 1, 3, 2, 4).reshape(S5_GROUPS, S5_COLS, S5_COLS))
        pw = (q - 1 - sig) if d == 0 else sig
        xr = pr[pw][..., None] * br[None] - pi[pw][..., None] * bi[None]
        xi = pr[pw][..., None] * bi[None] + pi[pw][..., None] * br[None]
        xr = xr.transpose(1, 0, 3, 2).reshape(S5_GROUPS, S5_COLS, S5_STATE)
        xi = xi.transpose(1, 0, 3, 2).reshape(S5_GROUPS, S5_COLS, S5_STATE)
        mbs += [xr, xi, xi, xr]
        po = (sig + 1) if d == 0 else (q - sig)
        mr = cpr[po].transpose(1, 3, 0, 2).reshape(S5_GROUPS, S5_STATE, S5_COLS)
        mi = -cpi[po].transpose(1, 3, 0, 2).reshape(S5_GROUPS, S5_STATE, S5_COLS)
        mcs += [mr, mi]
        are, aim = pr[q], pi[q]
        rows = [jnp.concatenate([are, are], -1), jnp.concatenate([-aim, aim], -1), jnp.concatenate([aim, -aim], -1)]
        coefs += [jnp.broadcast_to(r[:, None, :], (S5_GROUPS, 8, S5_SW)) for r in rows]
    t = (t_blocks[0] + t_blocks[1]).astype(BF16)
    mb = jnp.concatenate(mbs, axis=-1).astype(BF16)
    mc = jnp.concatenate(mcs, axis=1).astype(BF16)
    coef = jnp.concatenate(coefs, axis=1)
    return t, mb, mc, coef


A1_TL = 512


def _a1_kernel(att_ref, ys_ref, u_ref, d_ref, gw_ref, gb_ref, woa_ref, wos_ref, x_ref, g_ref, o_ref):
    y = u_ref[0].astype(F32) * d_ref[...] + ys_ref[0].astype(F32)
    z = jax.nn.gelu(y)
    gate = jax.nn.sigmoid(jnp.dot(z.astype(BF16), gw_ref[...], preferred_element_type=F32) + gb_ref[...])
    s5 = (z * gate).astype(BF16)
    mix = jnp.dot(att_ref[0], woa_ref[...], preferred_element_type=F32)
    mix = mix + jnp.dot(s5, wos_ref[...], preferred_element_type=F32)
    o_ref[0] = x_ref[0] + g_ref[0] * mix


def _a1_call(att, ys, u, d, gw, gb, woa, wos, x, g1):
    t3 = lambda b, j: (b, j, 0)
    full2 = lambda b, j: (0, 0)
    return pl.pallas_call(
        _a1_kernel, grid=(BATCH, SEQ // A1_TL),
        in_specs=[pl.BlockSpec((1, A1_TL, MLA_WIDTH), t3),
                  pl.BlockSpec((1, A1_TL, S5_WIDTH), t3),
                  pl.BlockSpec((1, A1_TL, S5_WIDTH), t3),
                  pl.BlockSpec((1, S5_WIDTH), full2),
                  pl.BlockSpec((S5_WIDTH, S5_WIDTH), full2),
                  pl.BlockSpec((1, S5_WIDTH), full2),
                  pl.BlockSpec((MLA_WIDTH, D_MODEL), full2),
                  pl.BlockSpec((S5_WIDTH, D_MODEL), full2),
                  pl.BlockSpec((1, A1_TL, D_MODEL), t3),
                  pl.BlockSpec((1, 1, D_MODEL), lambda b, j: (b, 0, 0))],
        out_specs=pl.BlockSpec((1, A1_TL, D_MODEL), t3),
        out_shape=jax.ShapeDtypeStruct((BATCH, SEQ, D_MODEL), F32),
        compiler_params=_params(("parallel", "arbitrary")), name="even_out_proj",
    )(att, ys, u, d, gw, gb, woa, wos, x, g1)


MOE_TL = 512
SLOT_PAD = 8


def _slot_columns(cols, lane):
    out = jnp.zeros(lane.shape, cols[0].dtype)
    for k, col in enumerate(cols):
        out = jnp.where(lane == k, col, out)
    return out[:, :SLOT_PAD]


def _pack_rows(v):
    halves = []
    for p in range(2):
        base = 2 * p * ROW_WORDS
        a = pltpu.bitcast(v[:, base:base + ROW_WORDS].astype(BF16).astype(F32), jnp.uint32)
        b = pltpu.bitcast(v[:, base + ROW_WORDS:base + 2 * ROW_WORDS].astype(BF16).astype(F32), jnp.uint32)
        halves.append((a >> 16) | b)
    return halves


def _unpack_rows(lo, hi):
    out = []
    for w in (lo, hi):
        out.append(pltpu.bitcast(w << 16, F32))
        out.append(pltpu.bitcast(w & jnp.uint32(0xFFFF0000), F32))
    return out


def _moe_in_kernel(x_ref, sh_ref, sc_ref, rw_ref, rb_ref, sg_ref, su_ref, sd_ref, tri_ref,
                   hlo_ref, hhi_ref, idx_ref, wt_ref, rank_ref, cnt_ref, o_ref, run_sc):
    @pl.when((pl.program_id(0) == 0) & (pl.program_id(1) == 0))
    def _():
        run_sc[...] = jnp.zeros_like(run_sc)

    h = _norm_mod(x_ref[0], sh_ref[0], sc_ref[0])
    hb = h.astype(BF16)
    h_lo = (h - hb.astype(F32)).astype(BF16)
    rw = rw_ref[...]
    rw_hi = rw.astype(BF16)
    rw_lo = (rw - rw_hi.astype(F32)).astype(BF16)
    logits = jnp.dot(jnp.concatenate([hb, hb, h_lo], axis=1), jnp.concatenate([rw_hi, rw_lo, rw_hi], axis=0),
                     preferred_element_type=F32)
    scores = jax.nn.sigmoid(logits)
    hlo_ref[0], hhi_ref[0] = _pack_rows(h)
    hid = jax.nn.silu(jnp.dot(hb, sg_ref[...].astype(BF16), preferred_element_type=F32))
    hid = hid * jnp.dot(hb, su_ref[...].astype(BF16), preferred_element_type=F32)
    o_ref[0] = jnp.dot(hid.astype(BF16), sd_ref[...].astype(BF16), preferred_element_type=F32)

    work = scores + rb_ref[...]
    lane = lax.broadcasted_iota(jnp.int32, work.shape, 1)
    hits, ids = [], []
    for _ in range(TOP_K):
        m = jnp.max(work, axis=-1, keepdims=True)
        ik = jnp.min(jnp.where(work == m, lane, N_EXPERTS), axis=-1, keepdims=True)
        hit = lane == ik
        hits.append(hit)
        ids.append(ik)
        work = jnp.where(hit, -jnp.inf, work)
    mask = hits[0]
    for hit in hits[1:]:
        mask = jnp.logical_or(mask, hit)
    maskf = mask.astype(F32)
    before = jnp.dot(tri_ref[...], maskf.astype(BF16), preferred_element_type=F32) + run_sc[...]
    sel = [jnp.sum(jnp.where(hit, scores, 0.0), axis=-1, keepdims=True) for hit in hits]
    denom = sel[0]
    for s in sel[1:]:
        denom = denom + s
    ranks = [jnp.sum(jnp.where(hit, before, 0.0), axis=-1, keepdims=True) for hit in hits]
    lane128 = lax.broadcasted_iota(jnp.int32, (MOE_TL, 128), 1)
    idx_ref[0] = _slot_columns(ids, lane128)
    wt_ref[0] = _slot_columns([s / denom * ROUTE_SCALE for s in sel], lane128)
    rank_ref[0] = _slot_columns([r.astype(jnp.int32) for r in ranks], lane128)
    run_sc[...] += jnp.sum(maskf, axis=0, keepdims=True)
    cnt_ref[...] = run_sc[...]


def _moe_in_call(x, sh, sc, rw, rb, sg, su, sd):
    t3 = lambda b, j: (b, j, 0)
    full2 = lambda b, j: (0, 0)
    per_b = lambda b, j: (b, 0, 0)
    ff = sg.shape[1]
    tri = jnp.asarray(np.tril(np.ones((MOE_TL, MOE_TL), np.float32), -1), BF16)
    slot = jax.ShapeDtypeStruct((BATCH, SEQ, SLOT_PAD), jnp.int32)
    return pl.pallas_call(
        _moe_in_kernel, grid=(BATCH, SEQ // MOE_TL),
        in_specs=[pl.BlockSpec((1, MOE_TL, D_MODEL), t3),
                  pl.BlockSpec((1, 1, D_MODEL), per_b),
                  pl.BlockSpec((1, 1, D_MODEL), per_b),
                  pl.BlockSpec((D_MODEL, N_EXPERTS), full2),
                  pl.BlockSpec((1, N_EXPERTS), full2),
                  pl.BlockSpec((D_MODEL, ff), full2),
                  pl.BlockSpec((D_MODEL, ff), full2),
                  pl.BlockSpec((ff, D_MODEL), full2),
                  pl.BlockSpec((MOE_TL, MOE_TL), full2)],
        out_specs=[pl.BlockSpec((1, MOE_TL, ROW_WORDS), t3),
                   pl.BlockSpec((1, MOE_TL, ROW_WORDS), t3),
                   pl.BlockSpec((1, MOE_TL, SLOT_PAD), t3),
                   pl.BlockSpec((1, MOE_TL, SLOT_PAD), t3),
                   pl.BlockSpec((1, MOE_TL, SLOT_PAD), t3),
                   pl.BlockSpec((1, N_EXPERTS), full2),
                   pl.BlockSpec((1, MOE_TL, D_MODEL), t3)],
        out_shape=[jax.ShapeDtypeStruct((BATCH, SEQ, ROW_WORDS), jnp.uint32),
                   jax.ShapeDtypeStruct((BATCH, SEQ, ROW_WORDS), jnp.uint32),
                   slot,
                   jax.ShapeDtypeStruct((BATCH, SEQ, SLOT_PAD), F32),
                   slot,
                   jax.ShapeDtypeStruct((1, N_EXPERTS), F32),
                   jax.ShapeDtypeStruct((BATCH, SEQ, D_MODEL), F32)],
        scratch_shapes=[pltpu.VMEM((1, N_EXPERTS), F32)],
        compiler_params=_params(("arbitrary", "arbitrary")), name="moe_router_shared",
    )(x, sh, sc, rw, rb, sg, su, sd, tri)


def _sc_mesh():
    return plsc.VectorSubcoreMesh(core_axis_name="c", subcore_axis_name="s")


def _sc_dispatch(h_words, dest, n_rows):
    n_tok = h_words.shape[0]

    @pl.kernel(out_type=jax.ShapeDtypeStruct((n_rows, ROW_WORDS), jnp.uint32), mesh=_sc_mesh(), scratch_types=[])
    def scatter_rows(h_hbm, i_hbm, o_hbm):
        def body(h_vmem, i_vmem):
            for k in range(TOP_K):
                pltpu.sync_copy(h_vmem, o_hbm.at[i_vmem.at[k]])

        pltpu.emit_pipeline(
            body, grid=(n_tok // SC_WINDOW,),
            in_specs=[pl.BlockSpec((SC_WINDOW, ROW_WORDS), index_map=lambda i: (i, 0)),
                      pl.BlockSpec((SLOT_PAD, SC_WINDOW), index_map=lambda i: (0, i))],
            out_specs=[],
            core_axis_name=("c", "s"), dimension_semantics=(pltpu.PARALLEL,),
        )(h_hbm, i_hbm)

    return scatter_rows(h_words, dest)


def _sc_collect(y_words, dest):
    n_tok = dest.shape[1]

    @pl.kernel(out_type=jax.ShapeDtypeStruct((TOP_K, n_tok, ROW_WORDS), jnp.uint32), mesh=_sc_mesh(),
               scratch_types=[])
    def gather_rows(y_hbm, i_hbm, o_hbm):
        def body(i_vmem, o_vmem):
            pltpu.sync_copy(y_hbm.at[i_vmem.at[0]], o_vmem.at[0])

        pltpu.emit_pipeline(
            body, grid=(TOP_K, n_tok // SC_WINDOW),
            in_specs=[pl.BlockSpec((1, SC_WINDOW), index_map=lambda k, i: (k, i))],
            out_specs=[pl.BlockSpec((1, SC_WINDOW, ROW_WORDS), index_map=lambda k, i: (k, i, 0))],
            core_axis_name=("c", "s"), dimension_semantics=(pltpu.PARALLEL, pltpu.PARALLEL),
        )(i_hbm, o_hbm)

    return gather_rows(y_words, dest)


def _expert_kernel(be_ref, nv_ref, xlo_ref, xhi_ref, wg_ref, wu_ref, wd_ref, ylo_ref, yhi_ref,
                   wg_sc, wu_sc, wd_sc):
    i = pl.program_id(0)
    nv = nv_ref[i]

    @pl.when(jnp.logical_or(i == 0, be_ref[i] != be_ref[jnp.maximum(i - 1, 0)]))
    def _():
        wg_sc[...] = wg_ref[0, 0].astype(BF16)
        wu_sc[...] = wu_ref[0, 0].astype(BF16)
        wd_sc[...] = wd_ref[0, 0].astype(BF16)

    @pl.when(nv > 0)
    def _():
        parts = _unpack_rows(xlo_ref[...], xhi_ref[...])
        xb = jnp.concatenate([p.astype(BF16) for p in parts], axis=1)
        live = lax.broadcasted_iota(jnp.int32, xb.shape, 0) < nv
        xb = jnp.where(live, xb, jnp.zeros_like(xb))
        hid = jax.nn.silu(jnp.dot(xb, wg_sc[...], preferred_element_type=F32))
        hid = hid * jnp.dot(xb, wu_sc[...], preferred_element_type=F32)
        y = jnp.dot(hid.astype(BF16), wd_sc[...], preferred_element_type=F32)
        ylo_ref[...], yhi_ref[...] = _pack_rows(y)

    @pl.when(nv == 0)
    def _():
        ylo_ref[...] = jnp.zeros_like(ylo_ref)
        yhi_ref[...] = jnp.zeros_like(yhi_ref)


def _expert_call(block_e, n_valid, xlo, xhi, wg, wu, wd, li):
    n_rows = xlo.shape[0]
    n_blocks = n_rows // MOE_BLOCK
    rows = pl.BlockSpec((MOE_BLOCK, ROW_WORDS), lambda i, be, nv: (i, 0))
    grid_spec = pltpu.PrefetchScalarGridSpec(
        num_scalar_prefetch=2, grid=(n_blocks,),
        in_specs=[rows, rows,
                  pl.BlockSpec((1, 1, D_MODEL, EXPERT_FF), lambda i, be, nv: (li, be[i], 0, 0)),
                  pl.BlockSpec((1, 1, D_MODEL, EXPERT_FF), lambda i, be, nv: (li, be[i], 0, 0)),
                  pl.BlockSpec((1, 1, EXPERT_FF, D_MODEL), lambda i, be, nv: (li, be[i], 0, 0))],
        out_specs=[rows, rows],
        scratch_shapes=[pltpu.VMEM((D_MODEL, EXPERT_FF), BF16), pltpu.VMEM((D_MODEL, EXPERT_FF), BF16),
                        pltpu.VMEM((EXPERT_FF, D_MODEL), BF16)])
    out = jax.ShapeDtypeStruct((n_rows, ROW_WORDS), jnp.uint32)
    return pl.pallas_call(
        _expert_kernel, grid_spec=grid_spec, out_shape=[out, out],
        compiler_params=_params(("arbitrary",)), name="moe_experts",
    )(block_e, n_valid, xlo, xhi, wg, wu, wd)


def _combine_kernel(ylo_ref, yhi_ref, w_ref, s_ref, x_ref, g_ref, o_ref):
    w = w_ref[0]
    acc = [None] * 4
    for k in range(TOP_K):
        wk = w[:, k:k + 1]
        for c, part in enumerate(_unpack_rows(ylo_ref[k], yhi_ref[k])):
            acc[c] = wk * part if acc[c] is None else acc[c] + wk * part
    for c in range(4):
        sl = slice(c * ROW_WORDS, (c + 1) * ROW_WORDS)
        o_ref[0, :, sl] = x_ref[0, :, sl] + g_ref[0, :, sl] * (acc[c] + s_ref[0, :, sl])


def _combine_call(ylo, yhi, wts, shared, x, g2):
    t3 = lambda b, j: (b, j, 0)
    nt = SEQ // MOE_TL
    rows = pl.BlockSpec((TOP_K, MOE_TL, ROW_WORDS), lambda b, j: (0, b * nt + j, 0))
    return pl.pallas_call(
        _combine_kernel, grid=(BATCH, nt),
        in_specs=[rows, rows,
                  pl.BlockSpec((1, MOE_TL, SLOT_PAD), t3),
                  pl.BlockSpec((1, MOE_TL, D_MODEL), t3),
                  pl.BlockSpec((1, MOE_TL, D_MODEL), t3),
                  pl.BlockSpec((1, 1, D_MODEL), lambda b, j: (b, 0, 0))],
        out_specs=pl.BlockSpec((1, MOE_TL, D_MODEL), t3),
        out_shape=jax.ShapeDtypeStruct((BATCH, SEQ, D_MODEL), F32),
        compiler_params=_params(("parallel", "arbitrary")), name="moe_combine",
    )(ylo, yhi, wts, shared, x, g2)


def _moe(x, sh, sc, g2, router_w, router_b, w_gate, w_up, w_down, sh_gate, sh_up, sh_down, li):
    T = BATCH * SEQ
    TK = T * TOP_K
    hlo, hhi, idx, wts, rank, counts, shared = _moe_in_call(
        x, sh, sc, router_w, router_b[None, :], sh_gate, sh_up, sh_down)
    counts = counts[0].astype(jnp.int32)
    padded = (counts + MOE_BLOCK - 1) // MOE_BLOCK * MOE_BLOCK
    pad_end = jnp.cumsum(padded)
    pad_start = pad_end - padded
    n_blocks = -(-TK // MOE_BLOCK) + N_EXPERTS
    n_rows = n_blocks * MOE_BLOCK
    block_start = jnp.arange(n_blocks, dtype=jnp.int32) * MOE_BLOCK
    block_e = jnp.sum((block_start[:, None] >= pad_end[None, :]).astype(jnp.int32), axis=1)
    block_e = jnp.minimum(block_e, N_EXPERTS - 1)
    n_valid = jnp.clip(counts[block_e] - (block_start - pad_start[block_e]), 0, MOE_BLOCK).astype(jnp.int32)
    dest = (pad_start[idx.reshape(T, SLOT_PAD)] + rank.reshape(T, SLOT_PAD)).T.astype(jnp.int32)
    xlo = _sc_dispatch(hlo.reshape(T, ROW_WORDS), dest, n_rows)
    xhi = _sc_dispatch(hhi.reshape(T, ROW_WORDS), dest, n_rows)
    ylo, yhi = _expert_call(block_e, n_valid, xlo, xhi, w_gate, w_up, w_down, li)
    return _combine_call(_sc_collect(ylo, dest), _sc_collect(yhi, dest), wts, shared, x, g2)


HY_TL = 512
HALO = 8


def _hy_in_kernel(x_ref, xp_ref, xn_ref, sh_ref, sc_ref, w_ref, cw_ref, cb_ref, z_ref, x0_ref, h_sc):
    j = pl.program_id(1)
    shift, scale = sh_ref[0], sc_ref[0]
    keep_prev = (j > 0).astype(F32)
    keep_next = (j < SEQ // HY_TL - 1).astype(F32)
    h_sc[0:HALO, :] = _norm_mod(xp_ref[0], shift, scale) * keep_prev
    h_sc[HALO:HALO + HY_TL, :] = _norm_mod(x_ref[0], shift, scale)
    h_sc[HALO + HY_TL:, :] = _norm_mod(xn_ref[0], shift, scale) * keep_next
    hcat = h_sc[...].astype(BF16)
    outs = []
    for part in range(3):
        sl = slice(part * HY_WIDTH, (part + 1) * HY_WIDTH)
        p = jnp.dot(hcat, w_ref[:, sl], preferred_element_type=F32)
        o = (p[HALO - 1:HALO - 1 + HY_TL] * cw_ref[0:1, sl] + p[HALO:HALO + HY_TL] * cw_ref[1:2, sl]
             + p[HALO + 1:HALO + 1 + HY_TL] * cw_ref[2:3, sl] + cb_ref[:, sl])
        outs.append(o)
    x0_ref[0] = outs[0].astype(BF16)
    z_ref[0] = (outs[2] * outs[1]).astype(BF16)


def _hy_in_call(x, sh, sc, w, cw, cb):
    nb8 = HY_TL // HALO
    t3 = lambda b, j: (b, j, 0)
    full2 = lambda b, j: (0, 0)
    per_b = lambda b, j: (b, 0, 0)
    return pl.pallas_call(
        _hy_in_kernel, grid=(BATCH, SEQ // HY_TL),
        in_specs=[pl.BlockSpec((1, HY_TL, D_MODEL), t3),
                  pl.BlockSpec((1, HALO, D_MODEL), lambda b, j: (b, jnp.maximum(j * nb8 - 1, 0), 0)),
                  pl.BlockSpec((1, HALO, D_MODEL), lambda b, j: (b, jnp.minimum((j + 1) * nb8, SEQ // HALO - 1), 0)),
                  pl.BlockSpec((1, 1, D_MODEL), per_b),
                  pl.BlockSpec((1, 1, D_MODEL), per_b),
                  pl.BlockSpec((D_MODEL, 3 * HY_WIDTH), full2),
                  pl.BlockSpec((SHORT_CONV, 3 * HY_WIDTH), full2),
                  pl.BlockSpec((1, 3 * HY_WIDTH), full2)],
        out_specs=[pl.BlockSpec((1, HY_TL, HY_WIDTH), t3), pl.BlockSpec((1, HY_TL, HY_WIDTH), t3)],
        out_shape=[jax.ShapeDtypeStruct((BATCH, SEQ, HY_WIDTH), BF16),
                   jax.ShapeDtypeStruct((BATCH, SEQ, HY_WIDTH), BF16)],
        scratch_shapes=[pltpu.VMEM((HY_TL + 2 * HALO, D_MODEL), F32)],
        compiler_params=_params(("parallel", "arbitrary")), name="hyena_in_proj",
    )(x, x, x, sh, sc, w, cw, cb)


def _fft_tables():
    c = np.arange(FFT_N2, dtype=np.int64)
    ang = 2.0 * np.pi * ((c[:, None] * c[None, :]) % FFT_N2) / FFT_N2
    sr, si = np.cos(ang), -np.sin(ang)
    m = np.block([[sr, -si], [si, sr]])
    k1 = np.arange(FFT_NK, dtype=np.int64)
    ang_t = 2.0 * np.pi * (k1[:, None] * c[None, :]) / DFT_N
    lanes = np.ones((1, 1, 128))
    tr = np.cos(ang_t)[:, :, None] * lanes
    ti = -np.sin(ang_t)[:, :, None] * lanes
    return jnp.asarray(m, BF16), jnp.asarray(tr, F32), jnp.asarray(ti, F32)


def _lin(acc, coef, val):
    if abs(coef) < 1e-12:
        return acc
    term = val if coef == 1.0 else (-val if coef == -1.0 else coef * val)
    return term if acc is None else acc + term


def _twiddle(tr_ref, ti_ref, k1, width):
    reps = width // 128
    tr, ti = tr_ref[k1], ti_ref[k1]
    return jnp.concatenate([tr] * reps, axis=1), jnp.concatenate([ti] * reps, axis=1)


def _class_spectrum(block, k1, m_ref, tr_ref, ti_ref, width):
    yr = yi = None
    for a in range(FFT_NA):
        th = 2.0 * math.pi * ((a * k1) % FFT_N1) / FFT_N1
        za = block(a)
        yr = _lin(yr, round(math.cos(th), 15), za)
        yi = _lin(yi, round(-math.sin(th), 15), za)
    if k1 > 0:
        tr, ti = _twiddle(tr_ref, ti_ref, k1, width)
        yr, yi = (yr * tr, yr * ti) if yi is None else (yr * tr - yi * ti, yr * ti + yi * tr)
    if yi is None:
        x = jnp.dot(m_ref[:, :FFT_N2], yr.astype(BF16), preferred_element_type=F32)
    else:
        x = jnp.dot(m_ref[...], jnp.concatenate([yr, yi], axis=0).astype(BF16), preferred_element_type=F32)
    return x[:FFT_N2], x[FFT_N2:]


def _class_inverse(yr, yi, k1, m_ref, tr_ref, ti_ref, acc_ref, width):
    v = jnp.dot(m_ref[...], jnp.concatenate([yr, -yi], axis=0).astype(BF16), preferred_element_type=F32)
    ur, ui = v[:FFT_N2], -v[FFT_N2:]
    if k1 > 0:
        tr, ti = _twiddle(tr_ref, ti_ref, k1, width)
        ur, ui = ur * tr + ui * ti, ui * tr - ur * ti
    scale = (1.0 if k1 in (0, FFT_N1 // 2) else 2.0) / DFT_N
    for a in range(FFT_NA):
        th = 2.0 * math.pi * ((a * k1) % FFT_N1) / FFT_N1
        term = _lin(None, round(math.cos(th), 15) * scale, ur)
        term = _lin(term, round(-math.sin(th), 15) * scale, ui)
        rows = slice(a * FFT_N2, (a + 1) * FFT_N2)
        if k1 == 0:
            acc_ref[rows, :] = term
        else:
            acc_ref[rows, :] += term


def _spec_kernel(hf_ref, hb_ref, m_ref, tr_ref, ti_ref, c_ref):
    for k1 in range(FFT_NK):
        fr, fi = _class_spectrum(lambda a: hf_ref[a * FFT_N2:(a + 1) * FFT_N2, :], k1, m_ref, tr_ref, ti_ref, HY_CT)
        br, bi = _class_spectrum(lambda a: hb_ref[a * FFT_N2:(a + 1) * FFT_N2, :], k1, m_ref, tr_ref, ti_ref, HY_CT)
        c_ref[k1, :FFT_N2, :] = (fr + br).astype(BF16)
        c_ref[k1, FFT_N2:, :] = (fi - bi).astype(BF16)


def _fft_table_specs(ngrid):
    z = (0,) * 2
    z3 = (0,) * 3
    if ngrid == 1:
        return [pl.BlockSpec((2 * FFT_N2, 2 * FFT_N2), lambda c: z),
                pl.BlockSpec((FFT_NK, FFT_N2, 128), lambda c: z3),
                pl.BlockSpec((FFT_NK, FFT_N2, 128), lambda c: z3)]
    return [pl.BlockSpec((2 * FFT_N2, 2 * FFT_N2), lambda b, c: z),
            pl.BlockSpec((FFT_NK, FFT_N2, 128), lambda b, c: z3),
            pl.BlockSpec((FFT_NK, FFT_N2, 128), lambda b, c: z3)]


def _spec_call(hfb, m, tr, ti):
    nct = HY_WIDTH // HY_CT
    return pl.pallas_call(
        _spec_kernel, grid=(nct,),
        in_specs=[pl.BlockSpec((SEQ, HY_CT), lambda c: (0, c)),
                  pl.BlockSpec((SEQ, HY_CT), lambda c: (0, c + nct))] + _fft_table_specs(1),
        out_specs=pl.BlockSpec((FFT_NK, 2 * FFT_N2, HY_CT), lambda c: (0, 0, c)),
        out_shape=jax.ShapeDtypeStruct((FFT_NK, 2 * FFT_N2, HY_WIDTH), BF16),
        compiler_params=_params(("arbitrary",)), name="hyena_filter_spectrum",
    )(hfb, hfb, m, tr, ti)


def _conv_kernel(z_ref, c_ref, m_ref, tr_ref, ti_ref, y_ref, acc):
    for k1 in range(FFT_NK):
        xr, xi = _class_spectrum(lambda a: z_ref[0, a * FFT_N2:(a + 1) * FFT_N2, :].astype(F32), k1,
                                 m_ref, tr_ref, ti_ref, HY_CT)
        cr = c_ref[k1, :FFT_N2, :].astype(F32)
        ci = c_ref[k1, FFT_N2:, :].astype(F32)
        _class_inverse(xr * cr - xi * ci, xr * ci + xi * cr, k1, m_ref, tr_ref, ti_ref, acc, HY_CT)
    y_ref[0] = acc[...].astype(BF16)


def _conv_call(z, spec, m, tr, ti):
    return pl.pallas_call(
        _conv_kernel, grid=(HY_WIDTH // HY_CT, BATCH),
        in_specs=[pl.BlockSpec((1, SEQ, HY_CT), lambda c, b: (b, 0, c)),
                  pl.BlockSpec((FFT_NK, 2 * FFT_N2, HY_CT), lambda c, b: (0, 0, c))] + _fft_table_specs(2),
        out_specs=pl.BlockSpec((1, SEQ, HY_CT), lambda c, b: (b, 0, c)),
        out_shape=jax.ShapeDtypeStruct((BATCH, SEQ, HY_WIDTH), BF16),
        scratch_shapes=[pltpu.VMEM((SEQ, HY_CT), F32)],
        compiler_params=_params(("parallel", "arbitrary")), name="hyena_long_conv",
    )(z, spec, m, tr, ti)


def _hy_out_kernel(y_ref, z_ref, x0_ref, b_ref, w_ref, x_ref, g_ref, o_ref):
    z = z_ref[0].astype(F32)
    gated = x0_ref[0].astype(F32) * (y_ref[0].astype(F32) + b_ref[...] * z)
    mix = jnp.dot(gated.astype(BF16), w_ref[...], preferred_element_type=F32)
    o_ref[0] = x_ref[0] + g_ref[0] * mix


def _hy_out_call(y, z, x0, bias, w, x, g1):
    t3 = lambda b, j: (b, j, 0)
    full2 = lambda b, j: (0, 0)
    return pl.pallas_call(
        _hy_out_kernel, grid=(BATCH, SEQ // HY_TL),
        in_specs=[pl.BlockSpec((1, HY_TL, HY_WIDTH), t3),
                  pl.BlockSpec((1, HY_TL, HY_WIDTH), t3),
                  pl.BlockSpec((1, HY_TL, HY_WIDTH), t3),
                  pl.BlockSpec((1, HY_WIDTH), full2),
                  pl.BlockSpec((HY_WIDTH, D_MODEL), full2),
                  pl.BlockSpec((1, HY_TL, D_MODEL), t3),
                  pl.BlockSpec((1, 1, D_MODEL), lambda b, j: (b, 0, 0))],
        out_specs=pl.BlockSpec((1, HY_TL, D_MODEL), t3),
        out_shape=jax.ShapeDtypeStruct((BATCH, SEQ, D_MODEL), F32),
        compiler_params=_params(("parallel", "arbitrary")), name="hyena_out_proj",
    )(y, z, x0, bias, w, x, g1)


def _hyena_filter(w1, b1, w2, b2, w3, freq):
    hi = lax.Precision.HIGHEST
    Lq = SEQ
    t = jnp.linspace(0.0, 1.0, Lq, dtype=F32)[:, None]
    ang = 2.0 * math.pi * jnp.arange(Lq, dtype=F32)[:, None] / Lq
    bands = jnp.linspace(1e-4, FILT_BANDS - 1, FILT_BANDS, dtype=F32)
    z = jnp.concatenate([t, jnp.cos(bands * ang), -jnp.sin(bands * ang)], axis=-1)
    hid = jnp.sin(freq * (jnp.dot(z, w1, precision=hi) + b1))
    hid = jnp.sin(freq * (jnp.dot(hid, w2, precision=hi) + b2))
    hf = jnp.dot(hid, w3, precision=hi).reshape(Lq, 2, HY_WIDTH)
    deltas = jnp.linspace(HY_MIN_DECAY, HY_MAX_DECAY, HY_WIDTH, dtype=F32)
    hf = hf * jnp.exp(-t * deltas)[:, None, :]
    hf = hf * lax.rsqrt(jnp.sum(hf * hf, axis=(0, 1), keepdims=True) + EPS)
    return hf.reshape(Lq, 2 * HY_WIDTH)


def kernel(x, c, ctx, c_ctx, ada_w, ada_b, ev_w_in, mla_q_norm, mla_w_uq, mla_kv_norm, mla_w_ukv, mla_q_qknorm, mla_k_qknorm, s5_lam_re, s5_lam_im, s5_log_step, s5_b_re, s5_b_im, s5_c_re, s5_c_im, s5_d, s5_glu_w, s5_glu_b, ev_w_out, hy_w_in, hy_conv_w, hy_conv_b, hy_f_w1, hy_f_b1, hy_f_w2, hy_f_b2, hy_f_w3, hy_f_freq, hy_bias, hy_w_out, moe_router_w, moe_router_b, moe_w_gate, moe_w_up, moe_w_down, moe_sh_gate, moe_sh_up, moe_sh_down):
    hi = lax.Precision.HIGHEST
    D = D_MODEL
    sc = jax.nn.silu(c)
    sc_ctx = jax.nn.silu(c_ctx)

    def mods(li):
        mod = jnp.dot(sc, ada_w[li], precision=hi) + ada_b[li]
        return [m[:, None, :] for m in jnp.split(mod, 6, axis=-1)]

    sh1, sc1, g1, sh2, sc2, g2 = mods(0)
    mod_ctx = jnp.dot(sc_ctx, ada_w[0][:, :2 * D], precision=hi) + ada_b[0][:2 * D]
    w0 = _a0_weights(ev_w_in[0], mla_q_norm[0], mla_w_uq[0], mla_kv_norm[0], mla_w_ukv[0],
                     mla_q_qknorm[0], mla_k_qknorm[0])
    q, k, v, u = _a0_call(x, ctx, sh1, sc1, mod_ctx[None, :D], mod_ctx[None, D:], w0)
    att = _attn_call(q, k, v)
    ug = u.reshape(BATCH, S5_NCHUNK, S5_CHUNK, S5_GROUPS, S5_GROUP)
    ug = ug.transpose(3, 1, 0, 2, 4).reshape(S5_GROUPS, S5_ROWS, S5_COLS)
    ys = _s5_call(ug, *_s5_weights(s5_lam_re[0], s5_lam_im[0], s5_log_step[0], s5_b_re[0], s5_b_im[0],
                                   s5_c_re[0], s5_c_im[0]))
    ys = ys.reshape(S5_GROUPS, S5_NCHUNK_LAT, BATCH, S5_CHUNK, S5_GROUP)
    ys = ys.transpose(2, 1, 3, 0, 4).reshape(BATCH, SEQ, S5_WIDTH)
    wo = ev_w_out[0].astype(BF16)
    x = _a1_call(att, ys, u, s5_d[0][None, :], s5_glu_w[0].astype(BF16), s5_glu_b[0][None, :],
                 wo[:MLA_WIDTH], wo[MLA_WIDTH:], x, g1)
    x = _moe(x, sh2, sc2, g2, moe_router_w[0], moe_router_b[0], moe_w_gate, moe_w_up, moe_w_down,
             moe_sh_gate[0], moe_sh_up[0], moe_sh_down[0], 0)

    sh1, sc1, g1, sh2, sc2, g2 = mods(1)
    z, x0 = _hy_in_call(x, sh1, sc1, hy_w_in[0].astype(BF16), hy_conv_w[0], hy_conv_b[0][None, :])
    fft_tabs = _fft_tables()
    hfb = _hyena_filter(hy_f_w1[0], hy_f_b1[0], hy_f_w2[0], hy_f_b2[0], hy_f_w3[0], hy_f_freq[0])
    y = _conv_call(z, _spec_call(hfb, *fft_tabs), *fft_tabs)
    x = _hy_out_call(y, z, x0, hy_bias[0][None, :], hy_w_out[0].astype(BF16), x, g1)
    x = _moe(x, sh2, sc2, g2, moe_router_w[1], moe_router_b[1], moe_w_gate, moe_w_up, moe_w_down,
             moe_sh_gate[1], moe_sh_up[1], moe_sh_down[1], 1)
    return x
```

```python
import functools
import math

import numpy as np
import jax
import jax.numpy as jnp
from jax import lax
from jax.experimental import pallas as pl
from jax.experimental.pallas import tpu as pltpu
from jax.experimental.pallas import tpu_sc as plsc

F32 = jnp.float32
BF16 = jnp.bfloat16

D_MODEL = 1024
BATCH = 8
SEQ = 4096
CTX_LEN = 256
KV_LEN = SEQ + CTX_LEN
GRID_W = 64
EPS = 1e-6

MLA_HEADS = 8
QK_NOPE = 64
QK_ROPE = 32
QK_HEAD = QK_NOPE + QK_ROPE
V_HEAD = 64
Q_LORA = 256
KV_LORA = 128
MLA_WIDTH = MLA_HEADS * V_HEAD
ROPE_BASE = 10000.0
HEAD_PAD = 128

S5_WIDTH = 512
S5_GROUP = 16
S5_GROUPS = S5_WIDTH // S5_GROUP
S5_STATE = 64
S5_CHUNK = 32
S5_NCHUNK = KV_LEN // S5_CHUNK
S5_NCHUNK_LAT = SEQ // S5_CHUNK
S5_NCHUNK_CTX = CTX_LEN // S5_CHUNK

HY_WIDTH = D_MODEL
FILT_EMB = 33
FILT_BANDS = (FILT_EMB - 1) // 2
SHORT_CONV = 3
HY_MIN_DECAY = -math.log(1e-2) / 1.5
HY_MAX_DECAY = -math.log(1e-2) / 0.3
DFT_N = 2 * SEQ
FFT_N1 = 16
FFT_N2 = DFT_N // FFT_N1
FFT_NA = FFT_N1 // 2
FFT_NK = FFT_N1 // 2 + 1
HY_CT = 256

N_EXPERTS = 64
TOP_K = 6
EXPERT_FF = 256
ROUTE_SCALE = 2.5
MOE_BLOCK = 512
ROW_WORDS = D_MODEL // 4
SC_WINDOW = 128

V7X_VMEM_BYTES = 64 * 1024 * 1024
VMEM_LIMIT = V7X_VMEM_BYTES - 8 * 1024 * 1024


def _params(semantics):
    return pltpu.CompilerParams(dimension_semantics=semantics, vmem_limit_bytes=VMEM_LIMIT)


def _norm_mod(x, shift, scale):
    ms = jnp.mean(x * x, axis=-1, keepdims=True)
    return x * lax.rsqrt(ms + EPS) * (1.0 + scale) + shift


def _rms(x, gain, n):
    ms = jnp.sum(x * x, axis=-1, keepdims=True) * (1.0 / n)
    return x * lax.rsqrt(ms + EPS) * gain


A0_TL = 256
A0_NT = SEQ // A0_TL


def _rope_perm():
    return np.concatenate([np.arange(0, QK_ROPE, 2), np.arange(1, QK_ROPE, 2)])


def _rope_tables():
    t = np.arange(SEQ)
    row = (t // GRID_W).astype(np.float64)
    col = (t % GRID_W).astype(np.float64)
    n_freq = QK_ROPE // 4
    inv = ROPE_BASE ** (-np.arange(n_freq, dtype=np.float64) / n_freq)
    ang = np.concatenate([row[:, None] * inv, col[:, None] * inv], axis=-1)
    cos, sin = np.cos(ang), np.sin(ang)
    half = QK_ROPE // 2
    a = np.zeros((KV_LEN, HEAD_PAD))
    bp = np.zeros((KV_LEN, HEAD_PAD))
    bm = np.zeros((KV_LEN, HEAD_PAD))
    a[:, :QK_HEAD] = 1.0
    a[:SEQ, QK_NOPE:QK_NOPE + half] = cos
    a[:SEQ, QK_NOPE + half:QK_HEAD] = cos
    bp[:SEQ, QK_NOPE + half:QK_HEAD] = sin
    bm[:SEQ, QK_NOPE:QK_NOPE + half] = -sin
    return a, bp, bm


def _a0_kernel(x_ref, ctx_ref, sh_ref, sc_ref, shc_ref, scc_ref, win_ref, qn_ref, wuq_ref, kvn_ref,
               wk_ref, wuv_ref, qg_ref, kg_ref, ka_ref, kp_ref, km_ref, qa_ref, qp_ref, qm_ref,
               q_ref, k_ref, v_ref, u_ref):
    j = pl.program_id(1)
    is_ctx = j == A0_NT
    xin = jnp.where(is_ctx, ctx_ref[0], x_ref[0])
    shift = jnp.where(is_ctx, shc_ref[...], sh_ref[0])
    scale = jnp.where(is_ctx, scc_ref[...], sc_ref[0])
    h = _norm_mod(xin, shift, scale).astype(BF16)
    proj = jnp.dot(h, win_ref[...], preferred_element_type=F32)
    u_ref[0] = proj[:, 512:].astype(BF16)

    c_kv = _rms(proj[:, Q_LORA:Q_LORA + KV_LORA], kvn_ref[...], KV_LORA).astype(BF16)
    lane = lax.broadcasted_iota(jnp.int32, (1, MLA_HEADS * HEAD_PAD), 1)
    ones_lane = (lane % HEAD_PAD == V_HEAD).astype(F32)
    v_ref[0] = (jnp.dot(c_kv, wuv_ref[...], preferred_element_type=F32) + ones_lane).astype(BF16)
    kin = jnp.concatenate([c_kv, proj[:, 384:512].astype(BF16)], axis=1)
    kf = jnp.dot(kin, wk_ref[...], preferred_element_type=F32)
    ka, kp, km = ka_ref[...], kp_ref[...], km_ref[...]
    for hd in range(MLA_HEADS):
        sl = slice(hd * HEAD_PAD, (hd + 1) * HEAD_PAD)
        kh = _rms(kf[:, sl], kg_ref[:, sl], QK_HEAD)
        kh = kh * ka + pltpu.roll(kh, 16, 1) * kp + pltpu.roll(kh, HEAD_PAD - 16, 1) * km
        k_ref[0, :, sl] = kh.astype(BF16)

    @pl.when(j < A0_NT)
    def _():
        ql = _rms(proj[:, :Q_LORA], qn_ref[...], Q_LORA).astype(BF16)
        qf = jnp.dot(ql, wuq_ref[...], preferred_element_type=F32)
        qa, qp, qm = qa_ref[...], qp_ref[...], qm_ref[...]
        for hd in range(MLA_HEADS):
            sl = slice(hd * HEAD_PAD, (hd + 1) * HEAD_PAD)
            qh = _rms(qf[:, sl], qg_ref[:, sl], QK_HEAD)
            qh = qh * qa + pltpu.roll(qh, 16, 1) * qp + pltpu.roll(qh, HEAD_PAD - 16, 1) * qm
            q_ref[0, :, sl] = qh.astype(BF16)


def _a0_call(x, ctx, sh, sc, shc, scc, w):
    nt = A0_NT
    lat = lambda b, j: (b, jnp.minimum(j, nt - 1), 0)
    full2 = lambda b, j: (0, 0)
    per_b = lambda b, j: (b, 0, 0)
    tab = pl.BlockSpec((A0_TL, HEAD_PAD), lambda b, j: (j, 0))
    in_specs = [
        pl.BlockSpec((1, A0_TL, D_MODEL), lat),
        pl.BlockSpec((1, CTX_LEN, D_MODEL), per_b),
        pl.BlockSpec((1, 1, D_MODEL), per_b),
        pl.BlockSpec((1, 1, D_MODEL), per_b),
        pl.BlockSpec((1, D_MODEL), full2),
        pl.BlockSpec((1, D_MODEL), full2),
        pl.BlockSpec((D_MODEL, 1024), full2),
        pl.BlockSpec((1, Q_LORA), full2),
        pl.BlockSpec((Q_LORA, MLA_HEADS * HEAD_PAD), full2),
        pl.BlockSpec((1, KV_LORA), full2),
        pl.BlockSpec((2 * KV_LORA, MLA_HEADS * HEAD_PAD), full2),
        pl.BlockSpec((KV_LORA, MLA_HEADS * HEAD_PAD), full2),
        pl.BlockSpec((1, MLA_HEADS * HEAD_PAD), full2),
        pl.BlockSpec((1, MLA_HEADS * HEAD_PAD), full2),
        tab, tab, tab, tab, tab, tab,
    ]
    out_specs = [
        pl.BlockSpec((1, A0_TL, MLA_HEADS * HEAD_PAD), lat),
        pl.BlockSpec((1, A0_TL, MLA_HEADS * HEAD_PAD), lambda b, j: (b, j, 0)),
        pl.BlockSpec((1, A0_TL, MLA_HEADS * HEAD_PAD), lambda b, j: (b, j, 0)),
        pl.BlockSpec((1, A0_TL, S5_WIDTH), lambda b, j: (b, j, 0)),
    ]
    out_shape = [
        jax.ShapeDtypeStruct((BATCH, SEQ, MLA_HEADS * HEAD_PAD), BF16),
        jax.ShapeDtypeStruct((BATCH, KV_LEN, MLA_HEADS * HEAD_PAD), BF16),
        jax.ShapeDtypeStruct((BATCH, KV_LEN, MLA_HEADS * HEAD_PAD), BF16),
        jax.ShapeDtypeStruct((BATCH, KV_LEN, S5_WIDTH), BF16),
    ]
    return pl.pallas_call(
        _a0_kernel, grid=(BATCH, nt + 1), in_specs=in_specs, out_specs=out_specs, out_shape=out_shape,
        compiler_params=_params(("parallel", "arbitrary")), name="even_in_proj",
    )(x, ctx, sh, sc, shc, scc, *w)


def _a0_weights(w_in, q_norm, w_uq, kv_norm, w_ukv, q_qk, k_qk):
    perm = _rope_perm()
    kr0 = Q_LORA + KV_LORA
    w_cat = jnp.concatenate([
        w_in[:, :kr0], w_in[:, kr0:kr0 + QK_ROPE][:, perm],
        jnp.zeros((D_MODEL, HEAD_PAD - QK_ROPE), F32), w_in[:, kr0 + QK_ROPE:]], axis=1).astype(BF16)
    pad = HEAD_PAD - QK_HEAD

    def head_gain(g):
        gh = jnp.concatenate([g[:QK_NOPE], g[QK_NOPE:][perm], jnp.zeros((pad,), F32)])
        return jnp.tile(gh, MLA_HEADS)[None, :]

    uq = w_uq.reshape(Q_LORA, MLA_HEADS, QK_HEAD)
    uq = jnp.concatenate([uq[..., :QK_NOPE], uq[..., QK_NOPE:][..., perm],
                          jnp.zeros((Q_LORA, MLA_HEADS, pad), F32)], axis=-1)
    uq = uq.reshape(Q_LORA, MLA_HEADS * HEAD_PAD).astype(BF16)
    ukv = w_ukv.reshape(KV_LORA, MLA_HEADS, QK_NOPE + V_HEAD)
    uk = jnp.concatenate([ukv[..., :QK_NOPE], jnp.zeros((KV_LORA, MLA_HEADS, HEAD_PAD - QK_NOPE), F32)], axis=-1)
    uk = uk.reshape(KV_LORA, MLA_HEADS * HEAD_PAD)
    place = np.zeros((KV_LORA, MLA_HEADS, HEAD_PAD), np.float32)
    for i in range(QK_ROPE):
        place[i, :, QK_NOPE + i] = 1.0
    wk = jnp.concatenate([uk, jnp.asarray(place.reshape(KV_LORA, MLA_HEADS * HEAD_PAD))], axis=0).astype(BF16)
    wuv = jnp.concatenate([ukv[..., QK_NOPE:], jnp.zeros((KV_LORA, MLA_HEADS, HEAD_PAD - V_HEAD), F32)], axis=-1)
    wuv = wuv.reshape(KV_LORA, MLA_HEADS * HEAD_PAD).astype(BF16)
    a, bp, bm = _rope_tables()
    qs = QK_HEAD ** -0.5 * math.log2(math.e)
    tabs = [jnp.asarray(t, F32) for t in (a, bp, bm, a * qs, bp * qs, bm * qs)]
    return [w_cat, q_norm[None, :], uq, kv_norm[None, :], wk, wuv, head_gain(q_qk), head_gain(k_qk)] + tabs


ATT_TQ = 256
HEADS_PER_STEP = 4


def _attn_kernel(q_ref, k_ref, v_ref, o_ref):
    for hh in range(HEADS_PER_STEP):
        sl = slice(hh * HEAD_PAD, (hh + 1) * HEAD_PAD)
        s = lax.dot_general(q_ref[0, :, sl], k_ref[0, :, sl], (((1,), (1,)), ((), ())),
                            preferred_element_type=F32)
        m = jnp.max(s, axis=-1, keepdims=True)
        p = jnp.exp2(s - m).astype(BF16)
        acc = jnp.dot(p, v_ref[0, :, sl], preferred_element_type=F32)
        o_ref[0, :, sl] = (acc * (1.0 / acc[:, V_HEAD:V_HEAD + 1])).astype(BF16)


def _attn_call(q, k, v):
    wq = HEADS_PER_STEP * HEAD_PAD
    return pl.pallas_call(
        _attn_kernel, grid=(BATCH, MLA_HEADS // HEADS_PER_STEP, SEQ // ATT_TQ),
        in_specs=[pl.BlockSpec((1, ATT_TQ, wq), lambda b, h, i: (b, i, h)),
                  pl.BlockSpec((1, KV_LEN, wq), lambda b, h, i: (b, 0, h)),
                  pl.BlockSpec((1, KV_LEN, wq), lambda b, h, i: (b, 0, h))],
        out_specs=pl.BlockSpec((1, ATT_TQ, wq), lambda b, h, i: (b, i, h)),
        out_shape=jax.ShapeDtypeStruct((BATCH, SEQ, MLA_HEADS * HEAD_PAD), BF16),
        compiler_params=_params(("parallel", "parallel", "arbitrary")), name="mla_attention",
    )(q, k, v)


S5_ROWS = S5_NCHUNK * BATCH
S5_ROWS_LAT = S5_NCHUNK_LAT * BATCH
S5_COLS = S5_CHUNK * S5_GROUP
S5_SW = 2 * S5_STATE


def _s5_kernel(u_ref, t_ref, mb_ref, mc_ref, coef_ref, y_ref, x_sc, sp_sc):
    u = u_ref[0]
    x_sc[...] = jnp.dot(u, mb_ref[0], preferred_element_type=F32)
    cf = coef_ref[0]
    af, bfm, bfp, ab, bbm, bbp = [cf[i * 8:(i + 1) * 8] for i in range(6)]

    def body(i, carry):
        sf, sfw, sb, sbw = carry
        cfw = jnp.where(i < S5_NCHUNK_CTX, i + S5_NCHUNK_LAT, i - S5_NCHUNK_CTX)
        rf = pl.multiple_of(cfw * BATCH, BATCH)
        rb = pl.multiple_of((S5_NCHUNK - 1 - i) * BATCH, BATCH)
        sp_sc[pl.ds(rf, BATCH), 0:S5_SW] = sf
        sp_sc[pl.ds(rb, BATCH), S5_SW:2 * S5_SW] = sb
        xf = x_sc[pl.ds(rf, BATCH), 0:S5_SW]
        xfw = x_sc[pl.ds(rf, BATCH), S5_SW:2 * S5_SW]
        xb = x_sc[pl.ds(rb, BATCH), 2 * S5_SW:3 * S5_SW]
        xbw = x_sc[pl.ds(rb, BATCH), 3 * S5_SW:4 * S5_SW]
        return (sf * af + sfw * bfm + xf, sfw * af + sf * bfp + xfw,
                sb * ab + sbw * bbm + xb, sbw * ab + sb * bbp + xbw)

    z = jnp.zeros((BATCH, S5_SW), F32)
    lax.fori_loop(0, S5_NCHUNK, body, (z, z, z, z))
    y = jnp.dot(u[:S5_ROWS_LAT], t_ref[0], preferred_element_type=F32)
    y = y + jnp.dot(sp_sc[0:S5_ROWS_LAT, :].astype(BF16), mc_ref[0], preferred_element_type=F32)
    y_ref[0] = y.astype(BF16)


def _s5_call(ug, t, mb, mc, coef):
    g3 = lambda g: (g, 0, 0)
    return pl.pallas_call(
        _s5_kernel, grid=(S5_GROUPS,),
        in_specs=[pl.BlockSpec((1, S5_ROWS, S5_COLS), g3),
                  pl.BlockSpec((1, S5_COLS, S5_COLS), g3),
                  pl.BlockSpec((1, S5_COLS, 4 * S5_SW), g3),
                  pl.BlockSpec((1, 2 * S5_SW, S5_COLS), g3),
                  pl.BlockSpec((1, 6 * 8, S5_SW), g3)],
        out_specs=pl.BlockSpec((1, S5_ROWS_LAT, S5_COLS), g3),
        out_shape=jax.ShapeDtypeStruct((S5_GROUPS, S5_ROWS_LAT, S5_COLS), BF16),
        scratch_shapes=[pltpu.VMEM((S5_ROWS, 4 * S5_SW), F32), pltpu.VMEM((S5_ROWS, 2 * S5_SW), F32)],
        compiler_params=_params(("parallel",)), name="s5_chunked_scan",
    )(ug, t, mb, mc, coef)


def _s5_weights(lam_re, lam_im, log_step, b_re, b_im, c_re, c_im):
    q = S5_CHUNK
    hi = lax.Precision.HIGHEST
    t_blocks, mbs, mcs, coefs = [], [], [], []
    sig = jnp.arange(q)
    for d in range(2):
        lr = jnp.minimum(lam_re[d], -1e-4)
        li = lam_im[d]
        step = jnp.exp(log_step[d])[:, None]
        jj = jnp.arange(q + 1, dtype=F32)[:, None, None]
        mag = jnp.exp(lr * step * jj)
        ph = li * step * jj
        pr, pi = mag * jnp.cos(ph), mag * jnp.sin(ph)
        nr, ni = pr[1] - 1.0, pi[1]
        den = lr * lr + li * li
        fr, fi = (nr * lr + ni * li) / den, (ni * lr - nr * li) / den
        br = fr[..., None] * b_re[d] - fi[..., None] * b_im[d]
        bi = fr[..., None] * b_im[d] + fi[..., None] * b_re[d]
        cr, ci = c_re[d], c_im[d]
        cpr = cr[None] * pr[:, :, None, :] - ci[None] * pi[:, :, None, :]
        cpi = cr[None] * pi[:, :, None, :] + ci[None] * pr[:, :, None, :]
        kern = (jnp.einsum('jghp,gpk->jghk', cpr[:q], br, precision=hi)
                - jnp.einsum('jghp,gpk->jghk', cpi[:q], bi, precision=hi))
        lag = (sig[None, :] - sig[:, None]) if d == 0 else (sig[:, None] - sig[None, :])
        kt = kern.transpose(1, 0, 3, 2)
        tb = jnp.where((lag >= 0)[None, :, :, None, None], kt[:, jnp.clip(lag, 0, q - 1)], 0.0)
        t_blocks.append(tb.transpose(0, 1, 3, 2, 4).reshape(S5_GROUPS, S5_COLS, S5_COLS))
        pw = (q - 1 - sig) if d == 0 else sig
        xr = pr[pw][..., None] * br[None] - pi[pw][..., None] * bi[None]
        xi = pr[pw][..., None] * bi[None] + pi[pw][..., None] * br[None]
        xr = xr.transpose(1, 0, 3, 2).reshape(S5_GROUPS, S5_COLS, S5_STATE)
        xi = xi.transpose(1, 0, 3, 2).reshape(S5_GROUPS, S5_COLS, S5_STATE)
        mbs += [xr, xi, xi, xr]
        po = (sig + 1) if d == 0 else (q - sig)
        mr = cpr[po].transpose(1, 3, 0, 2).reshape(S5_GROUPS, S5_STATE, S5_COLS)
        mi = -cpi[po].transpose(1, 3, 0, 2).reshape(S5_GROUPS, S5_STATE, S5_COLS)
        mcs += [mr, mi]
        are, aim = pr[q], pi[q]
        rows = [jnp.concatenate([are, are], -1), jnp.concatenate([-aim, aim], -1), jnp.concatenate([aim, -aim], -1)]
        coefs += [jnp.broadcast_to(r[:, None, :], (S5_GROUPS, 8, S5_SW)) for r in rows]
    t = (t_blocks[0] + t_blocks[1]).astype(BF16)
    mb = jnp.concatenate(mbs, axis=-1).astype(BF16)
    mc = jnp.concatenate(mcs, axis=1).astype(BF16)
    coef = jnp.concatenate(coefs, axis=1)
    return t, mb, mc, coef


A1_TL = 512


def _a1_kernel(att_ref, ys_ref, u_ref, d_ref, gw_ref, gb_ref, woa_ref, wos_ref, x_ref, g_ref, o_ref):
    y = u_ref[0].astype(F32) * d_ref[...] + ys_ref[0].astype(F32)
    z = jax.nn.gelu(y)
    gate = jax.nn.sigmoid(jnp.dot(z.astype(BF16), gw_ref[...], preferred_element_type=F32) + gb_ref[...])
    s5 = (z * gate).astype(BF16)
    mix = jnp.dot(att_ref[0], woa_ref[...], preferred_element_type=F32)
    mix = mix + jnp.dot(s5, wos_ref[...], preferred_element_type=F32)
    o_ref[0] = x_ref[0] + g_ref[0] * mix


def _a1_call(att, ys, u, d, gw, gb, woa, wos, x, g1):
    t3 = lambda b, j: (b, j, 0)
    full2 = lambda b, j: (0, 0)
    return pl.pallas_call(
        _a1_kernel, grid=(BATCH, SEQ // A1_TL),
        in_specs=[pl.BlockSpec((1, A1_TL, MLA_HEADS * HEAD_PAD), t3),
                  pl.BlockSpec((1, A1_TL, S5_WIDTH), t3),
                  pl.BlockSpec((1, A1_TL, S5_WIDTH), t3),
                  pl.BlockSpec((1, S5_WIDTH), full2),
                  pl.BlockSpec((S5_WIDTH, S5_WIDTH), full2),
                  pl.BlockSpec((1, S5_WIDTH), full2),
                  pl.BlockSpec((MLA_HEADS * HEAD_PAD, D_MODEL), full2),
                  pl.BlockSpec((S5_WIDTH, D_MODEL), full2),
                  pl.BlockSpec((1, A1_TL, D_MODEL), t3),
                  pl.BlockSpec((1, 1, D_MODEL), lambda b, j: (b, 0, 0))],
        out_specs=pl.BlockSpec((1, A1_TL, D_MODEL), t3),
        out_shape=jax.ShapeDtypeStruct((BATCH, SEQ, D_MODEL), F32),
        compiler_params=_params(("parallel", "arbitrary")), name="even_out_proj",
    )(att, ys, u, d, gw, gb, woa, wos, x, g1)


MOE_TL = 512
SLOT_PAD = 8


def _slot_columns(cols, lane):
    out = jnp.zeros(lane.shape, cols[0].dtype)
    for k, col in enumerate(cols):
        out = jnp.where(lane == k, col, out)
    return out[:, :SLOT_PAD]


def _pack_rows(v):
    halves = []
    for p in range(2):
        base = 2 * p * ROW_WORDS
        a = pltpu.bitcast(v[:, base:base + ROW_WORDS].astype(BF16).astype(F32), jnp.uint32)
        b = pltpu.bitcast(v[:, base + ROW_WORDS:base + 2 * ROW_WORDS].astype(BF16).astype(F32), jnp.uint32)
        halves.append((a >> 16) | b)
    return halves


def _unpack_rows(lo, hi):
    out = []
    for w in (lo, hi):
        out.append(pltpu.bitcast(w << 16, F32))
        out.append(pltpu.bitcast(w & jnp.uint32(0xFFFF0000), F32))
    return out


def _moe_in_kernel(x_ref, sh_ref, sc_ref, rw_ref, rb_ref, sg_ref, su_ref, sd_ref, tri_ref,
                   hlo_ref, hhi_ref, idx_ref, wt_ref, rank_ref, cnt_ref, o_ref, run_sc):
    @pl.when((pl.program_id(0) == 0) & (pl.program_id(1) == 0))
    def _():
        run_sc[...] = jnp.zeros_like(run_sc)

    h = _norm_mod(x_ref[0], sh_ref[0], sc_ref[0])
    hb = h.astype(BF16)
    h_lo = (h - hb.astype(F32)).astype(BF16)
    rw = rw_ref[...]
    rw_hi = rw.astype(BF16)
    rw_lo = (rw - rw_hi.astype(F32)).astype(BF16)
    logits = jnp.dot(jnp.concatenate([hb, hb, h_lo], axis=1), jnp.concatenate([rw_hi, rw_lo, rw_hi], axis=0),
                     preferred_element_type=F32)
    scores = jax.nn.sigmoid(logits)
    hlo_ref[0], hhi_ref[0] = _pack_rows(h)
    hid = jax.nn.silu(jnp.dot(hb, sg_ref[...].astype(BF16), preferred_element_type=F32))
    hid = hid * jnp.dot(hb, su_ref[...].astype(BF16), preferred_element_type=F32)
    o_ref[0] = jnp.dot(hid.astype(BF16), sd_ref[...].astype(BF16), preferred_element_type=F32)

    work = scores + rb_ref[...]
    lane = lax.broadcasted_iota(jnp.int32, work.shape, 1)
    hits, ids = [], []
    for _ in range(TOP_K):
        m = jnp.max(work, axis=-1, keepdims=True)
        ik = jnp.min(jnp.where(work == m, lane, N_EXPERTS), axis=-1, keepdims=True)
        hit = lane == ik
        hits.append(hit)
        ids.append(ik)
        work = jnp.where(hit, -jnp.inf, work)
    mask = hits[0]
    for hit in hits[1:]:
        mask = jnp.logical_or(mask, hit)
    maskf = mask.astype(F32)
    before = jnp.dot(tri_ref[...], maskf.astype(BF16), preferred_element_type=F32) + run_sc[...]
    sel = [jnp.sum(jnp.where(hit, scores, 0.0), axis=-1, keepdims=True) for hit in hits]
    denom = sel[0]
    for s in sel[1:]:
        denom = denom + s
    ranks = [jnp.sum(jnp.where(hit, before, 0.0), axis=-1, keepdims=True) for hit in hits]
    lane128 = lax.broadcasted_iota(jnp.int32, (MOE_TL, 128), 1)
    idx_ref[0] = _slot_columns(ids, lane128)
    wt_ref[0] = _slot_columns([s / denom * ROUTE_SCALE for s in sel], lane128)
    rank_ref[0] = _slot_columns([r.astype(jnp.int32) for r in ranks], lane128)
    run_sc[...] += jnp.sum(maskf, axis=0, keepdims=True)
    cnt_ref[...] = run_sc[...]


def _moe_in_call(x, sh, sc, rw, rb, sg, su, sd):
    t3 = lambda b, j: (b, j, 0)
    full2 = lambda b, j: (0, 0)
    per_b = lambda b, j: (b, 0, 0)
    ff = sg.shape[1]
    tri = jnp.asarray(np.tril(np.ones((MOE_TL, MOE_TL), np.float32), -1), BF16)
    slot = jax.ShapeDtypeStruct((BATCH, SEQ, SLOT_PAD), jnp.int32)
    return pl.pallas_call(
        _moe_in_kernel, grid=(BATCH, SEQ // MOE_TL),
        in_specs=[pl.BlockSpec((1, MOE_TL, D_MODEL), t3),
                  pl.BlockSpec((1, 1, D_MODEL), per_b),
                  pl.BlockSpec((1, 1, D_MODEL), per_b),
                  pl.BlockSpec((D_MODEL, N_EXPERTS), full2),
                  pl.BlockSpec((1, N_EXPERTS), full2),
                  pl.BlockSpec((D_MODEL, ff), full2),
                  pl.BlockSpec((D_MODEL, ff), full2),
                  pl.BlockSpec((ff, D_MODEL), full2),
                  pl.BlockSpec((MOE_TL, MOE_TL), full2)],
        out_specs=[pl.BlockSpec((1, MOE_TL, ROW_WORDS), t3),
                   pl.BlockSpec((1, MOE_TL, ROW_WORDS), t3),
                   pl.BlockSpec((1, MOE_TL, SLOT_PAD), t3),
                   pl.BlockSpec((1, MOE_TL, SLOT_PAD), t3),
                   pl.BlockSpec((1, MOE_TL, SLOT_PAD), t3),
                   pl.BlockSpec((1, N_EXPERTS), full2),
                   pl.BlockSpec((1, MOE_TL, D_MODEL), t3)],
        out_shape=[jax.ShapeDtypeStruct((BATCH, SEQ, ROW_WORDS), jnp.uint32),
                   jax.ShapeDtypeStruct((BATCH, SEQ, ROW_WORDS), jnp.uint32),
                   slot,
                   jax.ShapeDtypeStruct((BATCH, SEQ, SLOT_PAD), F32),
                   slot,
                   jax.ShapeDtypeStruct((1, N_EXPERTS), F32),
                   jax.ShapeDtypeStruct((BATCH, SEQ, D_MODEL), F32)],
        scratch_shapes=[pltpu.VMEM((1, N_EXPERTS), F32)],
        compiler_params=_params(("arbitrary", "arbitrary")), name="moe_router_shared",
    )(x, sh, sc, rw, rb, sg, su, sd, tri)


def _sc_mesh():
    return plsc.VectorSubcoreMesh(core_axis_name="c", subcore_axis_name="s")


def _sc_dispatch(h_words, dest, n_rows):
    n_tok = h_words.shape[0]

    @pl.kernel(out_type=jax.ShapeDtypeStruct((n_rows, ROW_WORDS), jnp.uint32), mesh=_sc_mesh(), scratch_types=[])
    def scatter_rows(h_hbm, i_hbm, o_hbm):
        def body(h_vmem, i_vmem):
            for k in range(TOP_K):
                pltpu.sync_copy(h_vmem, o_hbm.at[i_vmem.at[k]])

        pltpu.emit_pipeline(
            body, grid=(n_tok // SC_WINDOW,),
            in_specs=[pl.BlockSpec((SC_WINDOW, ROW_WORDS), index_map=lambda i: (i, 0)),
                      pl.BlockSpec((SLOT_PAD, SC_WINDOW), index_map=lambda i: (0, i))],
            out_specs=[],
            core_axis_name=("c", "s"), dimension_semantics=(pltpu.PARALLEL,),
        )(h_hbm, i_hbm)

    return scatter_rows(h_words, dest)


def _sc_collect(y_words, dest):
    n_tok = dest.shape[1]

    @pl.kernel(out_type=jax.ShapeDtypeStruct((TOP_K, n_tok, ROW_WORDS), jnp.uint32), mesh=_sc_mesh(),
               scratch_types=[])
    def gather_rows(y_hbm, i_hbm, o_hbm):
        def body(i_vmem, o_vmem):
            pltpu.sync_copy(y_hbm.at[i_vmem.at[0]], o_vmem.at[0])

        pltpu.emit_pipeline(
            body, grid=(TOP_K, n_tok // SC_WINDOW),
            in_specs=[pl.BlockSpec((1, SC_WINDOW), index_map=lambda k, i: (k, i))],
            out_specs=[pl.BlockSpec((1, SC_WINDOW, ROW_WORDS), index_map=lambda k, i: (k, i, 0))],
            core_axis_name=("c", "s"), dimension_semantics=(pltpu.PARALLEL, pltpu.PARALLEL),
        )(i_hbm, o_hbm)

    return gather_rows(y_words, dest)


def _expert_kernel(be_ref, nv_ref, xlo_ref, xhi_ref, wg_ref, wu_ref, wd_ref, ylo_ref, yhi_ref,
                   wg_sc, wu_sc, wd_sc):
    i = pl.program_id(0)
    nv = nv_ref[i]

    @pl.when(jnp.logical_or(i == 0, be_ref[i] != be_ref[jnp.maximum(i - 1, 0)]))
    def _():
        wg_sc[...] = wg_ref[0, 0].astype(BF16)
        wu_sc[...] = wu_ref[0, 0].astype(BF16)
        wd_sc[...] = wd_ref[0, 0].astype(BF16)

    @pl.when(nv > 0)
    def _():
        parts = _unpack_rows(xlo_ref[...], xhi_ref[...])
        xb = jnp.concatenate([p.astype(BF16) for p in parts], axis=1)
        live = lax.broadcasted_iota(jnp.int32, xb.shape, 0) < nv
        xb = jnp.where(live, xb, jnp.zeros_like(xb))
        hid = jax.nn.silu(jnp.dot(xb, wg_sc[...], preferred_element_type=F32))
        hid = hid * jnp.dot(xb, wu_sc[...], preferred_element_type=F32)
        y = jnp.dot(hid.astype(BF16), wd_sc[...], preferred_element_type=F32)
        ylo_ref[...], yhi_ref[...] = _pack_rows(y)

    @pl.when(nv == 0)
    def _():
        ylo_ref[...] = jnp.zeros_like(ylo_ref)
        yhi_ref[...] = jnp.zeros_like(yhi_ref)


def _expert_call(block_e, n_valid, xlo, xhi, wg, wu, wd, li):
    n_rows = xlo.shape[0]
    n_blocks = n_rows // MOE_BLOCK
    rows = pl.BlockSpec((MOE_BLOCK, ROW_WORDS), lambda i, be, nv: (i, 0))
    grid_spec = pltpu.PrefetchScalarGridSpec(
        num_scalar_prefetch=2, grid=(n_blocks,),
        in_specs=[rows, rows,
                  pl.BlockSpec((1, 1, D_MODEL, EXPERT_FF), lambda i, be, nv: (li, be[i], 0, 0)),
                  pl.BlockSpec((1, 1, D_MODEL, EXPERT_FF), lambda i, be, nv: (li, be[i], 0, 0)),
                  pl.BlockSpec((1, 1, EXPERT_FF, D_MODEL), lambda i, be, nv: (li, be[i], 0, 0))],
        out_specs=[rows, rows],
        scratch_shapes=[pltpu.VMEM((D_MODEL, EXPERT_FF), BF16), pltpu.VMEM((D_MODEL, EXPERT_FF), BF16),
                        pltpu.VMEM((EXPERT_FF, D_MODEL), BF16)])
    out = jax.ShapeDtypeStruct((n_rows, ROW_WORDS), jnp.uint32)
    return pl.pallas_call(
        _expert_kernel, grid_spec=grid_spec, out_shape=[out, out],
        compiler_params=_params(("arbitrary",)), name="moe_experts",
    )(block_e, n_valid, xlo, xhi, wg, wu, wd)


def _combine_kernel(ylo_ref, yhi_ref, w_ref, s_ref, x_ref, g_ref, o_ref):
    w = w_ref[0]
    acc = [None] * 4
    for k in range(TOP_K):
        wk = w[:, k:k + 1]
        for c, part in enumerate(_unpack_rows(ylo_ref[k], yhi_ref[k])):
            acc[c] = wk * part if acc[c] is None else acc[c] + wk * part
    for c in range(4):
        sl = slice(c * ROW_WORDS, (c + 1) * ROW_WORDS)
        o_ref[0, :, sl] = x_ref[0, :, sl] + g_ref[0, :, sl] * (acc[c] + s_ref[0, :, sl])


def _combine_call(ylo, yhi, wts, shared, x, g2):
    t3 = lambda b, j: (b, j, 0)
    nt = SEQ // MOE_TL
    rows = pl.BlockSpec((TOP_K, MOE_TL, ROW_WORDS), lambda b, j: (0, b * nt + j, 0))
    return pl.pallas_call(
        _combine_kernel, grid=(BATCH, nt),
        in_specs=[rows, rows,
                  pl.BlockSpec((1, MOE_TL, SLOT_PAD), t3),
                  pl.BlockSpec((1, MOE_TL, D_MODEL), t3),
                  pl.BlockSpec((1, MOE_TL, D_MODEL), t3),
                  pl.BlockSpec((1, 1, D_MODEL), lambda b, j: (b, 0, 0))],
        out_specs=pl.BlockSpec((1, MOE_TL, D_MODEL), t3),
        out_shape=jax.ShapeDtypeStruct((BATCH, SEQ, D_MODEL), F32),
        compiler_params=_params(("parallel", "arbitrary")), name="moe_combine",
    )(ylo, yhi, wts, shared, x, g2)


def _moe(x, sh, sc, g2, router_w, router_b, w_gate, w_up, w_down, sh_gate, sh_up, sh_down, li):
    T = BATCH * SEQ
    TK = T * TOP_K
    hlo, hhi, idx, wts, rank, counts, shared = _moe_in_call(
        x, sh, sc, router_w, router_b[None, :], sh_gate, sh_up, sh_down)
    counts = counts[0].astype(jnp.int32)
    padded = (counts + MOE_BLOCK - 1) // MOE_BLOCK * MOE_BLOCK
    pad_end = jnp.cumsum(padded)
    pad_start = pad_end - padded
    n_blocks = -(-TK // MOE_BLOCK) + N_EXPERTS
    n_rows = n_blocks * MOE_BLOCK
    block_start = jnp.arange(n_blocks, dtype=jnp.int32) * MOE_BLOCK
    block_e = jnp.sum((block_start[:, None] >= pad_end[None, :]).astype(jnp.int32), axis=1)
    block_e = jnp.minimum(block_e, N_EXPERTS - 1)
    n_valid = jnp.clip(counts[block_e] - (block_start - pad_start[block_e]), 0, MOE_BLOCK).astype(jnp.int32)
    dest = (pad_start[idx.reshape(T, SLOT_PAD)] + rank.reshape(T, SLOT_PAD)).T.astype(jnp.int32)
    xlo = _sc_dispatch(hlo.reshape(T, ROW_WORDS), dest, n_rows)
    xhi = _sc_dispatch(hhi.reshape(T, ROW_WORDS), dest, n_rows)
    ylo, yhi = _expert_call(block_e, n_valid, xlo, xhi, w_gate, w_up, w_down, li)
    return _combine_call(_sc_collect(ylo, dest), _sc_collect(yhi, dest), wts, shared, x, g2)


HY_TL = 512
HALO = 8


def _hy_in_kernel(x_ref, xp_ref, xn_ref, sh_ref, sc_ref, w_ref, cw_ref, cb_ref, z_ref, x0_ref, h_sc):
    j = pl.program_id(1)
    shift, scale = sh_ref[0], sc_ref[0]
    keep_prev = (j > 0).astype(F32)
    keep_next = (j < SEQ // HY_TL - 1).astype(F32)
    h_sc[0:HALO, :] = _norm_mod(xp_ref[0], shift, scale) * keep_prev
    h_sc[HALO:HALO + HY_TL, :] = _norm_mod(x_ref[0], shift, scale)
    h_sc[HALO + HY_TL:, :] = _norm_mod(xn_ref[0], shift, scale) * keep_next
    hcat = h_sc[...].astype(BF16)
    outs = []
    for part in range(3):
        sl = slice(part * HY_WIDTH, (part + 1) * HY_WIDTH)
        p = jnp.dot(hcat, w_ref[:, sl], preferred_element_type=F32)
        o = (p[HALO - 1:HALO - 1 + HY_TL] * cw_ref[0:1, sl] + p[HALO:HALO + HY_TL] * cw_ref[1:2, sl]
             + p[HALO + 1:HALO + 1 + HY_TL] * cw_ref[2:3, sl] + cb_ref[:, sl])
        outs.append(o)
    x0_ref[0] = outs[0].astype(BF16)
    z_ref[0] = (outs[2] * outs[1]).astype(BF16)


def _hy_in_call(x, sh, sc, w, cw, cb):
    nb8 = HY_TL // HALO
    t3 = lambda b, j: (b, j, 0)
    full2 = lambda b, j: (0, 0)
    per_b = lambda b, j: (b, 0, 0)
    return pl.pallas_call(
        _hy_in_kernel, grid=(BATCH, SEQ // HY_TL),
        in_specs=[pl.BlockSpec((1, HY_TL, D_MODEL), t3),
                  pl.BlockSpec((1, HALO, D_MODEL), lambda b, j: (b, jnp.maximum(j * nb8 - 1, 0), 0)),
                  pl.BlockSpec((1, HALO, D_MODEL), lambda b, j: (b, jnp.minimum((j + 1) * nb8, SEQ // HALO - 1), 0)),
                  pl.BlockSpec((1, 1, D_MODEL), per_b),
                  pl.BlockSpec((1, 1, D_MODEL), per_b),
                  pl.BlockSpec((D_MODEL, 3 * HY_WIDTH), full2),
                  pl.BlockSpec((SHORT_CONV, 3 * HY_WIDTH), full2),
                  pl.BlockSpec((1, 3 * HY_WIDTH), full2)],
        out_specs=[pl.BlockSpec((1, HY_TL, HY_WIDTH), t3), pl.BlockSpec((1, HY_TL, HY_WIDTH), t3)],
        out_shape=[jax.ShapeDtypeStruct((BATCH, SEQ, HY_WIDTH), BF16),
                   jax.ShapeDtypeStruct((BATCH, SEQ, HY_WIDTH), BF16)],
        scratch_shapes=[pltpu.VMEM((HY_TL + 2 * HALO, D_MODEL), F32)],
        compiler_params=_params(("parallel", "arbitrary")), name="hyena_in_proj",
    )(x, x, x, sh, sc, w, cw, cb)


def _fft_tables():
    c = np.arange(FFT_N2, dtype=np.int64)
    ang = 2.0 * np.pi * ((c[:, None] * c[None, :]) % FFT_N2) / FFT_N2
    sr, si = np.cos(ang), -np.sin(ang)
    m = np.block([[sr, -si], [si, sr]])
    k1 = np.arange(FFT_NK, dtype=np.int64)
    ang_t = 2.0 * np.pi * (k1[:, None] * c[None, :]) / DFT_N
    lanes = np.ones((1, 1, 128))
    tr = np.cos(ang_t)[:, :, None] * lanes
    ti = -np.sin(ang_t)[:, :, None] * lanes
    return jnp.asarray(m, BF16), jnp.asarray(tr, F32), jnp.asarray(ti, F32)


def _lin(acc, coef, val):
    if abs(coef) < 1e-12:
        return acc
    term = val if coef == 1.0 else (-val if coef == -1.0 else coef * val)
    return term if acc is None else acc + term


def _twiddle(tr_ref, ti_ref, k1, width):
    reps = width // 128
    tr, ti = tr_ref[k1], ti_ref[k1]
    return jnp.concatenate([tr] * reps, axis=1), jnp.concatenate([ti] * reps, axis=1)


def _class_spectrum(block, k1, m_ref, tr_ref, ti_ref, width):
    yr = yi = None
    for a in range(FFT_NA):
        th = 2.0 * math.pi * ((a * k1) % FFT_N1) / FFT_N1
        za = block(a)
        yr = _lin(yr, round(math.cos(th), 15), za)
        yi = _lin(yi, round(-math.sin(th), 15), za)
    if k1 > 0:
        tr, ti = _twiddle(tr_ref, ti_ref, k1, width)
        yr, yi = (yr * tr, yr * ti) if yi is None else (yr * tr - yi * ti, yr * ti + yi * tr)
    if yi is None:
        x = jnp.dot(m_ref[:, :FFT_N2], yr.astype(BF16), preferred_element_type=F32)
    else:
        x = jnp.dot(m_ref[...], jnp.concatenate([yr, yi], axis=0).astype(BF16), preferred_element_type=F32)
    return x[:FFT_N2], x[FFT_N2:]


def _class_inverse(yr, yi, k1, m_ref, tr_ref, ti_ref, acc_ref, width):
    v = jnp.dot(m_ref[...], jnp.concatenate([yr, -yi], axis=0).astype(BF16), preferred_element_type=F32)
    ur, ui = v[:FFT_N2], -v[FFT_N2:]
    if k1 > 0:
        tr, ti = _twiddle(tr_ref, ti_ref, k1, width)
        ur, ui = ur * tr + ui * ti, ui * tr - ur * ti
    scale = (1.0 if k1 in (0, FFT_N1 // 2) else 2.0) / DFT_N
    for a in range(FFT_NA):
        th = 2.0 * math.pi * ((a * k1) % FFT_N1) / FFT_N1
        term = _lin(None, round(math.cos(th), 15) * scale, ur)
        term = _lin(term, round(-math.sin(th), 15) * scale, ui)
        rows = slice(a * FFT_N2, (a + 1) * FFT_N2)
        if k1 == 0:
            acc_ref[rows, :] = term
        else:
            acc_ref[rows, :] += term


def _spec_kernel(hf_ref, hb_ref, m_ref, tr_ref, ti_ref, c_ref):
    for k1 in range(FFT_NK):
        fr, fi = _class_spectrum(lambda a: hf_ref[a * FFT_N2:(a + 1) * FFT_N2, :], k1, m_ref, tr_ref, ti_ref, HY_CT)
        br, bi = _class_spectrum(lambda a: hb_ref[a * FFT_N2:(a + 1) * FFT_N2, :], k1, m_ref, tr_ref, ti_ref, HY_CT)
        c_ref[k1, :FFT_N2, :] = (fr + br).astype(BF16)
        c_ref[k1, FFT_N2:, :] = (fi - bi).astype(BF16)


def _fft_table_specs(ngrid):
    z = (0,) * 2
    z3 = (0,) * 3
    if ngrid == 1:
        return [pl.BlockSpec((2 * FFT_N2, 2 * FFT_N2), lambda c: z),
                pl.BlockSpec((FFT_NK, FFT_N2, 128), lambda c: z3),
                pl.BlockSpec((FFT_NK, FFT_N2, 128), lambda c: z3)]
    return [pl.BlockSpec((2 * FFT_N2, 2 * FFT_N2), lambda b, c: z),
            pl.BlockSpec((FFT_NK, FFT_N2, 128), lambda b, c: z3),
            pl.BlockSpec((FFT_NK, FFT_N2, 128), lambda b, c: z3)]


def _spec_call(hfb, m, tr, ti):
    nct = HY_WIDTH // HY_CT
    return pl.pallas_call(
        _spec_kernel, grid=(nct,),
        in_specs=[pl.BlockSpec((SEQ, HY_CT), lambda c: (0, c)),
                  pl.BlockSpec((SEQ, HY_CT), lambda c: (0, c + nct))] + _fft_table_specs(1),
        out_specs=pl.BlockSpec((FFT_NK, 2 * FFT_N2, HY_CT), lambda c: (0, 0, c)),
        out_shape=jax.ShapeDtypeStruct((FFT_NK, 2 * FFT_N2, HY_WIDTH), BF16),
        compiler_params=_params(("arbitrary",)), name="hyena_filter_spectrum",
    )(hfb, hfb, m, tr, ti)


def _conv_kernel(z_ref, c_ref, m_ref, tr_ref, ti_ref, y_ref, acc):
    for k1 in range(FFT_NK):
        xr, xi = _class_spectrum(lambda a: z_ref[0, a * FFT_N2:(a + 1) * FFT_N2, :].astype(F32), k1,
                                 m_ref, tr_ref, ti_ref, HY_CT)
        cr = c_ref[k1, :FFT_N2, :].astype(F32)
        ci = c_ref[k1, FFT_N2:, :].astype(F32)
        _class_inverse(xr * cr - xi * ci, xr * ci + xi * cr, k1, m_ref, tr_ref, ti_ref, acc, HY_CT)
    y_ref[0] = acc[...].astype(BF16)


def _conv_call(z, spec, m, tr, ti):
    return pl.pallas_call(
        _conv_kernel, grid=(HY_WIDTH // HY_CT, BATCH),
        in_specs=[pl.BlockSpec((1, SEQ, HY_CT), lambda c, b: (b, 0, c)),
                  pl.BlockSpec((FFT_NK, 2 * FFT_N2, HY_CT), lambda c, b: (0, 0, c))] + _fft_table_specs(2),
        out_specs=pl.BlockSpec((1, SEQ, HY_CT), lambda c, b: (b, 0, c)),
        out_shape=jax.ShapeDtypeStruct((BATCH, SEQ, HY_WIDTH), BF16),
        scratch_shapes=[pltpu.VMEM((SEQ, HY_CT), F32)],
        compiler_params=_params(("parallel", "arbitrary")), name="hyena_long_conv",
    )(z, spec, m, tr, ti)


def _hy_out_kernel(y_ref, z_ref, x0_ref, b_ref, w_ref, x_ref, g_ref, o_ref):
    z = z_ref[0].astype(F32)
    gated = x0_ref[0].astype(F32) * (y_ref[0].astype(F32) + b_ref[...] * z)
    mix = jnp.dot(gated.astype(BF16), w_ref[...], preferred_element_type=F32)
    o_ref[0] = x_ref[0] + g_ref[0] * mix


def _hy_out_call(y, z, x0, bias, w, x, g1):
    t3 = lambda b, j: (b, j, 0)
    full2 = lambda b, j: (0, 0)
    return pl.pallas_call(
        _hy_out_kernel, grid=(BATCH, SEQ // HY_TL),
        in_specs=[pl.BlockSpec((1, HY_TL, HY_WIDTH), t3),
                  pl.BlockSpec((1, HY_TL, HY_WIDTH), t3),
                  pl.BlockSpec((1, HY_TL, HY_WIDTH), t3),
                  pl.BlockSpec((1, HY_WIDTH), full2),
                  pl.BlockSpec((HY_WIDTH, D_MODEL), full2),
                  pl.BlockSpec((1, HY_TL, D_MODEL), t3),
                  pl.BlockSpec((1, 1, D_MODEL), lambda b, j: (b, 0, 0))],
        out_specs=pl.BlockSpec((1, HY_TL, D_MODEL), t3),
        out_shape=jax.ShapeDtypeStruct((BATCH, SEQ, D_MODEL), F32),
        compiler_params=_params(("parallel", "arbitrary")), name="hyena_out_proj",
    )(y, z, x0, bias, w, x, g1)


def _hyena_filter(w1, b1, w2, b2, w3, freq):
    hi = lax.Precision.HIGHEST
    Lq = SEQ
    t = jnp.linspace(0.0, 1.0, Lq, dtype=F32)[:, None]
    ang = 2.0 * math.pi * jnp.arange(Lq, dtype=F32)[:, None] / Lq
    bands = jnp.linspace(1e-4, FILT_BANDS - 1, FILT_BANDS, dtype=F32)
    z = jnp.concatenate([t, jnp.cos(bands * ang), -jnp.sin(bands * ang)], axis=-1)
    hid = jnp.sin(freq * (jnp.dot(z, w1, precision=hi) + b1))
    hid = jnp.sin(freq * (jnp.dot(hid, w2, precision=hi) + b2))
    hf = jnp.dot(hid, w3, precision=hi).reshape(Lq, 2, HY_WIDTH)
    deltas = jnp.linspace(HY_MIN_DECAY, HY_MAX_DECAY, HY_WIDTH, dtype=F32)
    hf = hf * jnp.exp(-t * deltas)[:, None, :]
    hf = hf * lax.rsqrt(jnp.sum(hf * hf, axis=(0, 1), keepdims=True) + EPS)
    return hf.reshape(Lq, 2 * HY_WIDTH)


def kernel(x, c, ctx, c_ctx, ada_w, ada_b, ev_w_in, mla_q_norm, mla_w_uq, mla_kv_norm, mla_w_ukv, mla_q_qknorm, mla_k_qknorm, s5_lam_re, s5_lam_im, s5_log_step, s5_b_re, s5_b_im, s5_c_re, s5_c_im, s5_d, s5_glu_w, s5_glu_b, ev_w_out, hy_w_in, hy_conv_w, hy_conv_b, hy_f_w1, hy_f_b1, hy_f_w2, hy_f_b2, hy_f_w3, hy_f_freq, hy_bias, hy_w_out, moe_router_w, moe_router_b, moe_w_gate, moe_w_up, moe_w_down, moe_sh_gate, moe_sh_up, moe_sh_down):
    hi = lax.Precision.HIGHEST
    D = D_MODEL
    sc = jax.nn.silu(c)
    sc_ctx = jax.nn.silu(c_ctx)

    def mods(li):
        mod = jnp.dot(sc, ada_w[li], precision=hi) + ada_b[li]
        return [m[:, None, :] for m in jnp.split(mod, 6, axis=-1)]

    sh1, sc1, g1, sh2, sc2, g2 = mods(0)
    mod_ctx = jnp.dot(sc_ctx, ada_w[0][:, :2 * D], precision=hi) + ada_b[0][:2 * D]
    w0 = _a0_weights(ev_w_in[0], mla_q_norm[0], mla_w_uq[0], mla_kv_norm[0], mla_w_ukv[0],
                     mla_q_qknorm[0], mla_k_qknorm[0])
    q, k, v, u = _a0_call(x, ctx, sh1, sc1, mod_ctx[None, :D], mod_ctx[None, D:], w0)
    att = _attn_call(q, k, v)
    ug = u.reshape(BATCH, S5_NCHUNK, S5_CHUNK, S5_GROUPS, S5_GROUP)
    ug = ug.transpose(3, 1, 0, 2, 4).reshape(S5_GROUPS, S5_ROWS, S5_COLS)
    ys = _s5_call(ug, *_s5_weights(s5_lam_re[0], s5_lam_im[0], s5_log_step[0], s5_b_re[0], s5_b_im[0],
                                   s5_c_re[0], s5_c_im[0]))
    ys = ys.reshape(S5_GROUPS, S5_NCHUNK_LAT, BATCH, S5_CHUNK, S5_GROUP)
    ys = ys.transpose(2, 1, 3, 0, 4).reshape(BATCH, SEQ, S5_WIDTH)
    wo = ev_w_out[0].astype(BF16)
    wo_att = jnp.concatenate([wo[:MLA_WIDTH].reshape(MLA_HEADS, V_HEAD, D_MODEL),
                              jnp.zeros((MLA_HEADS, HEAD_PAD - V_HEAD, D_MODEL), BF16)], axis=1)
    x = _a1_call(att, ys, u, s5_d[0][None, :], s5_glu_w[0].astype(BF16), s5_glu_b[0][None, :],
                 wo_att.reshape(MLA_HEADS * HEAD_PAD, D_MODEL), wo[MLA_WIDTH:], x, g1)
    x = _moe(x, sh2, sc2, g2, moe_router_w[0], moe_router_b[0], moe_w_gate, moe_w_up, moe_w_down,
             moe_sh_gate[0], moe_sh_up[0], moe_sh_down[0], 0)

    sh1, sc1, g1, sh2, sc2, g2 = mods(1)
    z, x0 = _hy_in_call(x, sh1, sc1, hy_w_in[0].astype(BF16), hy_conv_w[0], hy_conv_b[0][None, :])
    fft_tabs = _fft_tables()
    hfb = _hyena_filter(hy_f_w1[0], hy_f_b1[0], hy_f_w2[0], hy_f_b2[0], hy_f_w3[0], hy_f_freq[0])
    y = _conv_call(z, _spec_call(hfb, *fft_tabs), *fft_tabs)
    x = _hy_out_call(y, z, x0, hy_bias[0][None, :], hy_w_out[0].astype(BF16), x, g1)
    x = _moe(x, sh2, sc2, g2, moe_router_w[1], moe_router_b[1], moe_w_gate, moe_w_up, moe_w_down,
             moe_sh_gate[1], moe_sh_up[1], moe_sh_down[1], 1)
    return x
```

```python
import functools
import math

import numpy as np
import jax
import jax.numpy as jnp
from jax import lax
from jax.experimental import pallas as pl
from jax.experimental.pallas import tpu as pltpu
from jax.experimental.pallas import tpu_sc as plsc

F32 = jnp.float32
BF16 = jnp.bfloat16

D_MODEL = 1024
BATCH = 8
SEQ = 4096
CTX_LEN = 256
KV_LEN = SEQ + CTX_LEN
GRID_W = 64
EPS = 1e-6

MLA_HEADS = 8
QK_NOPE = 64
QK_ROPE = 32
QK_HEAD = QK_NOPE + QK_ROPE
V_HEAD = 64
Q_LORA = 256
KV_LORA = 128
MLA_WIDTH = MLA_HEADS * V_HEAD
ROPE_BASE = 10000.0
HEAD_PAD = 128

S5_WIDTH = 512
S5_GROUP = 16
S5_GROUPS = S5_WIDTH // S5_GROUP
S5_STATE = 64
S5_CHUNK = 32
S5_NCHUNK = KV_LEN // S5_CHUNK
S5_NCHUNK_LAT = SEQ // S5_CHUNK
S5_NCHUNK_CTX = CTX_LEN // S5_CHUNK

HY_WIDTH = D_MODEL
FILT_EMB = 33
FILT_BANDS = (FILT_EMB - 1) // 2
SHORT_CONV = 3
HY_MIN_DECAY = -math.log(1e-2) / 1.5
HY_MAX_DECAY = -math.log(1e-2) / 0.3
DFT_N = 2 * SEQ
FFT_N1 = 16
FFT_N2 = DFT_N // FFT_N1
FFT_NA = FFT_N1 // 2
FFT_NK = FFT_N1 // 2 + 1
HY_CT = 256

N_EXPERTS = 64
TOP_K = 6
EXPERT_FF = 256
ROUTE_SCALE = 2.5
MOE_BLOCK = 512
ROW_WORDS = D_MODEL // 4
SC_WINDOW = 128

V7X_VMEM_BYTES = 64 * 1024 * 1024
VMEM_LIMIT = V7X_VMEM_BYTES - 8 * 1024 * 1024


def _params(semantics):
    return pltpu.CompilerParams(dimension_semantics=semantics, vmem_limit_bytes=VMEM_LIMIT)


def _norm_mod(x, shift, scale):
    ms = jnp.mean(x * x, axis=-1, keepdims=True)
    return x * lax.rsqrt(ms + EPS) * (1.0 + scale) + shift


def _rms(x, gain, n):
    ms = jnp.sum(x * x, axis=-1, keepdims=True) * (1.0 / n)
    return x * lax.rsqrt(ms + EPS) * gain


A0_TL = 256
A0_NT = SEQ // A0_TL


def _rope_perm():
    return np.concatenate([np.arange(0, QK_ROPE, 2), np.arange(1, QK_ROPE, 2)])


def _rope_tables():
    t = np.arange(SEQ)
    row = (t // GRID_W).astype(np.float64)
    col = (t % GRID_W).astype(np.float64)
    n_freq = QK_ROPE // 4
    inv = ROPE_BASE ** (-np.arange(n_freq, dtype=np.float64) / n_freq)
    ang = np.concatenate([row[:, None] * inv, col[:, None] * inv], axis=-1)
    cos, sin = np.cos(ang), np.sin(ang)
    half = QK_ROPE // 2
    a = np.zeros((KV_LEN, HEAD_PAD))
    bp = np.zeros((KV_LEN, HEAD_PAD))
    bm = np.zeros((KV_LEN, HEAD_PAD))
    a[:, :QK_HEAD] = 1.0
    a[:SEQ, QK_NOPE:QK_NOPE + half] = cos
    a[:SEQ, QK_NOPE + half:QK_HEAD] = cos
    bp[:SEQ, QK_NOPE + half:QK_HEAD] = sin
    bm[:SEQ, QK_NOPE:QK_NOPE + half] = -sin
    return a, bp, bm


def _a0_kernel(x_ref, ctx_ref, sh_ref, sc_ref, shc_ref, scc_ref, win_ref, qn_ref, wuq_ref, kvn_ref,
               wk_ref, wuv_ref, qg_ref, kg_ref, ka_ref, kp_ref, km_ref, qa_ref, qp_ref, qm_ref,
               q_ref, k_ref, v_ref, u_ref):
    j = pl.program_id(1)
    is_ctx = j == A0_NT
    xin = jnp.where(is_ctx, ctx_ref[0], x_ref[0])
    shift = jnp.where(is_ctx, shc_ref[...], sh_ref[0])
    scale = jnp.where(is_ctx, scc_ref[...], sc_ref[0])
    h = _norm_mod(xin, shift, scale).astype(BF16)
    proj = jnp.dot(h, win_ref[...], preferred_element_type=F32)
    u_ref[0] = proj[:, 512:].astype(BF16)

    c_kv = _rms(proj[:, Q_LORA:Q_LORA + KV_LORA], kvn_ref[...], KV_LORA).astype(BF16)
    lane = lax.broadcasted_iota(jnp.int32, (1, MLA_HEADS * HEAD_PAD), 1)
    ones_lane = (lane % HEAD_PAD == V_HEAD).astype(F32)
    v_ref[0] = (jnp.dot(c_kv, wuv_ref[...], preferred_element_type=F32) + ones_lane).astype(BF16)
    kin = jnp.concatenate([c_kv, proj[:, 384:512].astype(BF16)], axis=1)
    kf = jnp.dot(kin, wk_ref[...], preferred_element_type=F32)
    ka, kp, km = ka_ref[...], kp_ref[...], km_ref[...]
    for hd in range(MLA_HEADS):
        sl = slice(hd * HEAD_PAD, (hd + 1) * HEAD_PAD)
        kh = _rms(kf[:, sl], kg_ref[:, sl], QK_HEAD)
        kh = kh * ka + pltpu.roll(kh, 16, 1) * kp + pltpu.roll(kh, HEAD_PAD - 16, 1) * km
        k_ref[0, :, sl] = kh.astype(BF16)

    @pl.when(j < A0_NT)
    def _():
        ql = _rms(proj[:, :Q_LORA], qn_ref[...], Q_LORA).astype(BF16)
        qf = jnp.dot(ql, wuq_ref[...], preferred_element_type=F32)
        qa, qp, qm = qa_ref[...], qp_ref[...], qm_ref[...]
        for hd in range(MLA_HEADS):
            sl = slice(hd * HEAD_PAD, (hd + 1) * HEAD_PAD)
            qh = _rms(qf[:, sl], qg_ref[:, sl], QK_HEAD)
            qh = qh * qa + pltpu.roll(qh, 16, 1) * qp + pltpu.roll(qh, HEAD_PAD - 16, 1) * qm
            q_ref[0, :, sl] = qh.astype(BF16)


def _a0_call(x, ctx, sh, sc, shc, scc, w):
    nt = A0_NT
    lat = lambda b, j: (b, jnp.minimum(j, nt - 1), 0)
    full2 = lambda b, j: (0, 0)
    per_b = lambda b, j: (b, 0, 0)
    tab = pl.BlockSpec((A0_TL, HEAD_PAD), lambda b, j: (j, 0))
    in_specs = [
        pl.BlockSpec((1, A0_TL, D_MODEL), lat),
        pl.BlockSpec((1, CTX_LEN, D_MODEL), per_b),
        pl.BlockSpec((1, 1, D_MODEL), per_b),
        pl.BlockSpec((1, 1, D_MODEL), per_b),
        pl.BlockSpec((1, D_MODEL), full2),
        pl.BlockSpec((1, D_MODEL), full2),
        pl.BlockSpec((D_MODEL, 1024), full2),
        pl.BlockSpec((1, Q_LORA), full2),
        pl.BlockSpec((Q_LORA, MLA_HEADS * HEAD_PAD), full2),
        pl.BlockSpec((1, KV_LORA), full2),
        pl.BlockSpec((2 * KV_LORA, MLA_HEADS * HEAD_PAD), full2),
        pl.BlockSpec((KV_LORA, MLA_HEADS * HEAD_PAD), full2),
        pl.BlockSpec((1, MLA_HEADS * HEAD_PAD), full2),
        pl.BlockSpec((1, MLA_HEADS * HEAD_PAD), full2),
        tab, tab, tab, tab, tab, tab,
    ]
    out_specs = [
        pl.BlockSpec((1, A0_TL, MLA_HEADS * HEAD_PAD), lat),
        pl.BlockSpec((1, A0_TL, MLA_HEADS * HEAD_PAD), lambda b, j: (b, j, 0)),
        pl.BlockSpec((1, A0_TL, MLA_HEADS * HEAD_PAD), lambda b, j: (b, j, 0)),
        pl.BlockSpec((1, A0_TL, S5_WIDTH), lambda b, j: (b, j, 0)),
    ]
    out_shape = [
        jax.ShapeDtypeStruct((BATCH, SEQ, MLA_HEADS * HEAD_PAD), BF16),
        jax.ShapeDtypeStruct((BATCH, KV_LEN, MLA_HEADS * HEAD_PAD), BF16),
        jax.ShapeDtypeStruct((BATCH, KV_LEN, MLA_HEADS * HEAD_PAD), BF16),
        jax.ShapeDtypeStruct((BATCH, KV_LEN, S5_WIDTH), BF16),
    ]
    return pl.pallas_call(
        _a0_kernel, grid=(BATCH, nt + 1), in_specs=in_specs, out_specs=out_specs, out_shape=out_shape,
        compiler_params=_params(("parallel", "arbitrary")), name="even_in_proj",
    )(x, ctx, sh, sc, shc, scc, *w)


def _a0_weights(w_in, q_norm, w_uq, kv_norm, w_ukv, q_qk, k_qk):
    perm = _rope_perm()
    kr0 = Q_LORA + KV_LORA
    w_cat = jnp.concatenate([
        w_in[:, :kr0], w_in[:, kr0:kr0 + QK_ROPE][:, perm],
        jnp.zeros((D_MODEL, HEAD_PAD - QK_ROPE), F32), w_in[:, kr0 + QK_ROPE:]], axis=1).astype(BF16)
    pad = HEAD_PAD - QK_HEAD

    def head_gain(g):
        gh = jnp.concatenate([g[:QK_NOPE], g[QK_NOPE:][perm], jnp.zeros((pad,), F32)])
        return jnp.tile(gh, MLA_HEADS)[None, :]

    uq = w_uq.reshape(Q_LORA, MLA_HEADS, QK_HEAD)
    uq = jnp.concatenate([uq[..., :QK_NOPE], uq[..., QK_NOPE:][..., perm],
                          jnp.zeros((Q_LORA, MLA_HEADS, pad), F32)], axis=-1)
    uq = uq.reshape(Q_LORA, MLA_HEADS * HEAD_PAD).astype(BF16)
    ukv = w_ukv.reshape(KV_LORA, MLA_HEADS, QK_NOPE + V_HEAD)
    uk = jnp.concatenate([ukv[..., :QK_NOPE], jnp.zeros((KV_LORA, MLA_HEADS, HEAD_PAD - QK_NOPE), F32)], axis=-1)
    uk = uk.reshape(KV_LORA, MLA_HEADS * HEAD_PAD)
    place = np.zeros((KV_LORA, MLA_HEADS, HEAD_PAD), np.float32)
    for i in range(QK_ROPE):
        place[i, :, QK_NOPE + i] = 1.0
    wk = jnp.concatenate([uk, jnp.asarray(place.reshape(KV_LORA, MLA_HEADS * HEAD_PAD))], axis=0).astype(BF16)
    wuv = jnp.concatenate([ukv[..., QK_NOPE:], jnp.zeros((KV_LORA, MLA_HEADS, HEAD_PAD - V_HEAD), F32)], axis=-1)
    wuv = wuv.reshape(KV_LORA, MLA_HEADS * HEAD_PAD).astype(BF16)
    a, bp, bm = _rope_tables()
    qs = QK_HEAD ** -0.5 * math.log2(math.e)
    tabs = [jnp.asarray(t, F32) for t in (a, bp, bm, a * qs, bp * qs, bm * qs)]
    return [w_cat, q_norm[None, :], uq, kv_norm[None, :], wk, wuv, head_gain(q_qk), head_gain(k_qk)] + tabs


ATT_TQ = 256
HEADS_PER_STEP = 4


def _attn_kernel(q_ref, k_ref, v_ref, o_ref):
    for hh in range(HEADS_PER_STEP):
        sl = slice(hh * HEAD_PAD, (hh + 1) * HEAD_PAD)
        s = lax.dot_general(q_ref[0, :, sl], k_ref[0, :, sl], (((1,), (1,)), ((), ())),
                            preferred_element_type=F32)
        m = jnp.max(s, axis=-1, keepdims=True)
        p = jnp.exp2(s - m).astype(BF16)
        acc = jnp.dot(p, v_ref[0, :, sl], preferred_element_type=F32)
        o_ref[0, :, sl] = (acc * (1.0 / acc[:, V_HEAD:V_HEAD + 1])).astype(BF16)


def _attn_call(q, k, v):
    wq = HEADS_PER_STEP * HEAD_PAD
    return pl.pallas_call(
        _attn_kernel, grid=(BATCH, MLA_HEADS // HEADS_PER_STEP, SEQ // ATT_TQ),
        in_specs=[pl.BlockSpec((1, ATT_TQ, wq), lambda b, h, i: (b, i, h)),
                  pl.BlockSpec((1, KV_LEN, wq), lambda b, h, i: (b, 0, h)),
                  pl.BlockSpec((1, KV_LEN, wq), lambda b, h, i: (b, 0, h))],
        out_specs=pl.BlockSpec((1, ATT_TQ, wq), lambda b, h, i: (b, i, h)),
        out_shape=jax.ShapeDtypeStruct((BATCH, SEQ, MLA_HEADS * HEAD_PAD), BF16),
        compiler_params=_params(("parallel", "parallel", "arbitrary")), name="mla_attention",
    )(q, k, v)


S5_ROWS = S5_NCHUNK * BATCH
S5_ROWS_LAT = S5_NCHUNK_LAT * BATCH
S5_COLS = S5_CHUNK * S5_GROUP
S5_SW = 2 * S5_STATE


def _s5_kernel(u_ref, r_ref, mb_ref, mc_ref, coef_ref, y_ref, x_sc, sp_sc, t_sc):
    u = u_ref[0]
    lags = r_ref[0]
    for sg in range(S5_CHUNK):
        off = (S5_CHUNK - 1 - sg) * S5_GROUP
        t_sc[sg * S5_GROUP:(sg + 1) * S5_GROUP, :] = lags[:, off:off + S5_COLS].astype(BF16)
    x_sc[...] = jnp.dot(u, mb_ref[0], preferred_element_type=F32)
    cf = coef_ref[0]
    af, bfm, bfp, ab, bbm, bbp = [cf[i * 8:(i + 1) * 8] for i in range(6)]

    def body(i, carry):
        sf, sfw, sb, sbw = carry
        cfw = jnp.where(i < S5_NCHUNK_CTX, i + S5_NCHUNK_LAT, i - S5_NCHUNK_CTX)
        rf = pl.multiple_of(cfw * BATCH, BATCH)
        rb = pl.multiple_of((S5_NCHUNK - 1 - i) * BATCH, BATCH)
        sp_sc[pl.ds(rf, BATCH), 0:S5_SW] = sf
        sp_sc[pl.ds(rb, BATCH), S5_SW:2 * S5_SW] = sb
        xf = x_sc[pl.ds(rf, BATCH), 0:S5_SW]
        xfw = x_sc[pl.ds(rf, BATCH), S5_SW:2 * S5_SW]
        xb = x_sc[pl.ds(rb, BATCH), 2 * S5_SW:3 * S5_SW]
        xbw = x_sc[pl.ds(rb, BATCH), 3 * S5_SW:4 * S5_SW]
        return (sf * af + sfw * bfm + xf, sfw * af + sf * bfp + xfw,
                sb * ab + sbw * bbm + xb, sbw * ab + sb * bbp + xbw)

    z = jnp.zeros((BATCH, S5_SW), F32)
    lax.fori_loop(0, S5_NCHUNK, body, (z, z, z, z))
    y = jnp.dot(u[:S5_ROWS_LAT], t_sc[...], preferred_element_type=F32)
    y = y + jnp.dot(sp_sc[0:S5_ROWS_LAT, :].astype(BF16), mc_ref[0], preferred_element_type=F32)
    y_ref[0] = y.astype(BF16)


def _s5_call(ug, t, mb, mc, coef):
    g3 = lambda g: (g, 0, 0)
    return pl.pallas_call(
        _s5_kernel, grid=(S5_GROUPS,),
        in_specs=[pl.BlockSpec((1, S5_ROWS, S5_COLS), g3),
                  pl.BlockSpec((1, S5_GROUP, 2 * S5_COLS), g3),
                  pl.BlockSpec((1, S5_COLS, 4 * S5_SW), g3),
                  pl.BlockSpec((1, 2 * S5_SW, S5_COLS), g3),
                  pl.BlockSpec((1, 6 * 8, S5_SW), g3)],
        out_specs=pl.BlockSpec((1, S5_ROWS_LAT, S5_COLS), g3),
        out_shape=jax.ShapeDtypeStruct((S5_GROUPS, S5_ROWS_LAT, S5_COLS), BF16),
        scratch_shapes=[pltpu.VMEM((S5_ROWS, 4 * S5_SW), F32), pltpu.VMEM((S5_ROWS, 2 * S5_SW), F32),
                        pltpu.VMEM((S5_COLS, S5_COLS), BF16)],
        compiler_params=_params(("parallel",)), name="s5_chunked_scan",
    )(ug, t, mb, mc, coef)


def _s5_weights(lam_re, lam_im, log_step, b_re, b_im, c_re, c_im):
    q = S5_CHUNK
    hi = lax.Precision.HIGHEST
    t_blocks, mbs, mcs, coefs = [], [], [], []
    sig = jnp.arange(q)
    for d in range(2):
        lr = jnp.minimum(lam_re[d], -1e-4)
        li = lam_im[d]
        step = jnp.exp(log_step[d])[:, None]
        jj = jnp.arange(q + 1, dtype=F32)[:, None, None]
        mag = jnp.exp(lr * step * jj)
        ph = li * step * jj
        pr, pi = mag * jnp.cos(ph), mag * jnp.sin(ph)
        nr, ni = pr[1] - 1.0, pi[1]
        den = lr * lr + li * li
        fr, fi = (nr * lr + ni * li) / den, (ni * lr - nr * li) / den
        br = fr[..., None] * b_re[d] - fi[..., None] * b_im[d]
        bi = fr[..., None] * b_im[d] + fi[..., None] * b_re[d]
        cr, ci = c_re[d], c_im[d]
        cpr = cr[None] * pr[:, :, None, :] - ci[None] * pi[:, :, None, :]
        cpi = cr[None] * pi[:, :, None, :] + ci[None] * pr[:, :, None, :]
        kern = (jnp.einsum('jghp,gpk->jghk', cpr[:q], br, precision=hi)
                - jnp.einsum('jghp,gpk->jghk', cpi[:q], bi, precision=hi))
        kt = kern.transpose(1, 3, 0, 2)
        zero_slots = jnp.zeros((S5_GROUPS, S5_GROUP, q, S5_GROUP), F32)
        if d == 0:
            t_blocks.append(jnp.concatenate([zero_slots[:, :, :q - 1], kt, zero_slots[:, :, :1]], axis=2))
        else:
            t_blocks.append(jnp.concatenate([kt[:, :, ::-1], zero_slots], axis=2))
        pw = (q - 1 - sig) if d == 0 else sig
        xr = pr[pw][..., None] * br[None] - pi[pw][..., None] * bi[None]
        xi = pr[pw][..., None] * bi[None] + pi[pw][..., None] * br[None]
        xr = xr.transpose(1, 0, 3, 2).reshape(S5_GROUPS, S5_COLS, S5_STATE)
        xi = xi.transpose(1, 0, 3, 2).reshape(S5_GROUPS, S5_COLS, S5_STATE)
        mbs += [xr, xi, xi, xr]
        po = (sig + 1) if d == 0 else (q - sig)
        mr = cpr[po].transpose(1, 3, 0, 2).reshape(S5_GROUPS, S5_STATE, S5_COLS)
        mi = -cpi[po].transpose(1, 3, 0, 2).reshape(S5_GROUPS, S5_STATE, S5_COLS)
        mcs += [mr, mi]
        are, aim = pr[q], pi[q]
        rows = [jnp.concatenate([are, are], -1), jnp.concatenate([-aim, aim], -1), jnp.concatenate([aim, -aim], -1)]
        coefs += [jnp.broadcast_to(r[:, None, :], (S5_GROUPS, 8, S5_SW)) for r in rows]
    t = (t_blocks[0] + t_blocks[1]).reshape(S5_GROUPS, S5_GROUP, 2 * S5_COLS)
    mb = jnp.concatenate(mbs, axis=-1).astype(BF16)
    mc = jnp.concatenate(mcs, axis=1).astype(BF16)
    coef = jnp.concatenate(coefs, axis=1)
    return t, mb, mc, coef


A1_TL = 512


def _a1_kernel(att_ref, ys_ref, u_ref, d_ref, gw_ref, gb_ref, woa_ref, wos_ref, x_ref, g_ref, o_ref):
    y = u_ref[0].astype(F32) * d_ref[...] + ys_ref[0].astype(F32)
    z = jax.nn.gelu(y)
    gate = jax.nn.sigmoid(jnp.dot(z.astype(BF16), gw_ref[...], preferred_element_type=F32) + gb_ref[...])
    s5 = (z * gate).astype(BF16)
    mix = jnp.dot(att_ref[0], woa_ref[...], preferred_element_type=F32)
    mix = mix + jnp.dot(s5, wos_ref[...], preferred_element_type=F32)
    o_ref[0] = x_ref[0] + g_ref[0] * mix


def _a1_call(att, ys, u, d, gw, gb, woa, wos, x, g1):
    t3 = lambda b, j: (b, j, 0)
    full2 = lambda b, j: (0, 0)
    return pl.pallas_call(
        _a1_kernel, grid=(BATCH, SEQ // A1_TL),
        in_specs=[pl.BlockSpec((1, A1_TL, MLA_HEADS * HEAD_PAD), t3),
                  pl.BlockSpec((1, A1_TL, S5_WIDTH), t3),
                  pl.BlockSpec((1, A1_TL, S5_WIDTH), t3),
                  pl.BlockSpec((1, S5_WIDTH), full2),
                  pl.BlockSpec((S5_WIDTH, S5_WIDTH), full2),
                  pl.BlockSpec((1, S5_WIDTH), full2),
                  pl.BlockSpec((MLA_HEADS * HEAD_PAD, D_MODEL), full2),
                  pl.BlockSpec((S5_WIDTH, D_MODEL), full2),
                  pl.BlockSpec((1, A1_TL, D_MODEL), t3),
                  pl.BlockSpec((1, 1, D_MODEL), lambda b, j: (b, 0, 0))],
        out_specs=pl.BlockSpec((1, A1_TL, D_MODEL), t3),
        out_shape=jax.ShapeDtypeStruct((BATCH, SEQ, D_MODEL), F32),
        compiler_params=_params(("parallel", "arbitrary")), name="even_out_proj",
    )(att, ys, u, d, gw, gb, woa, wos, x, g1)


MOE_TL = 512
SLOT_PAD = 8


def _slot_columns(cols, lane):
    out = jnp.zeros(lane.shape, cols[0].dtype)
    for k, col in enumerate(cols):
        out = jnp.where(lane == k, col, out)
    return out[:, :SLOT_PAD]


def _pack_rows(v):
    halves = []
    for p in range(2):
        base = 2 * p * ROW_WORDS
        a = pltpu.bitcast(v[:, base:base + ROW_WORDS].astype(BF16).astype(F32), jnp.uint32)
        b = pltpu.bitcast(v[:, base + ROW_WORDS:base + 2 * ROW_WORDS].astype(BF16).astype(F32), jnp.uint32)
        halves.append((a >> 16) | b)
    return halves


def _unpack_rows(lo, hi):
    out = []
    for w in (lo, hi):
        out.append(pltpu.bitcast(w << 16, F32))
        out.append(pltpu.bitcast(w & jnp.uint32(0xFFFF0000), F32))
    return out


def _moe_in_kernel(x_ref, sh_ref, sc_ref, rw_ref, rb_ref, sg_ref, su_ref, sd_ref, tri_ref,
                   hlo_ref, hhi_ref, idx_ref, wt_ref, rank_ref, cnt_ref, o_ref, run_sc):
    @pl.when((pl.program_id(0) == 0) & (pl.program_id(1) == 0))
    def _():
        run_sc[...] = jnp.zeros_like(run_sc)

    h = _norm_mod(x_ref[0], sh_ref[0], sc_ref[0])
    hb = h.astype(BF16)
    h_lo = (h - hb.astype(F32)).astype(BF16)
    rw = rw_ref[...]
    rw_hi = rw.astype(BF16)
    rw_lo = (rw - rw_hi.astype(F32)).astype(BF16)
    logits = jnp.dot(jnp.concatenate([hb, hb, h_lo], axis=1), jnp.concatenate([rw_hi, rw_lo, rw_hi], axis=0),
                     preferred_element_type=F32)
    scores = jax.nn.sigmoid(logits)
    hlo_ref[0], hhi_ref[0] = _pack_rows(h)
    hid = jax.nn.silu(jnp.dot(hb, sg_ref[...].astype(BF16), preferred_element_type=F32))
    hid = hid * jnp.dot(hb, su_ref[...].astype(BF16), preferred_element_type=F32)
    o_ref[0] = jnp.dot(hid.astype(BF16), sd_ref[...].astype(BF16), preferred_element_type=F32)

    work = scores + rb_ref[...]
    lane = lax.broadcasted_iota(jnp.int32, work.shape, 1)
    hits, ids = [], []
    for _ in range(TOP_K):
        m = jnp.max(work, axis=-1, keepdims=True)
        ik = jnp.min(jnp.where(work == m, lane, N_EXPERTS), axis=-1, keepdims=True)
        hit = lane == ik
        hits.append(hit)
        ids.append(ik)
        work = jnp.where(hit, -jnp.inf, work)
    mask = hits[0]
    for hit in hits[1:]:
        mask = jnp.logical_or(mask, hit)
    maskf = mask.astype(F32)
    before = jnp.dot(tri_ref[...], maskf.astype(BF16), preferred_element_type=F32) + run_sc[...]
    sel = [jnp.sum(jnp.where(hit, scores, 0.0), axis=-1, keepdims=True) for hit in hits]
    denom = sel[0]
    for s in sel[1:]:
        denom = denom + s
    ranks = [jnp.sum(jnp.where(hit, before, 0.0), axis=-1, keepdims=True) for hit in hits]
    lane128 = lax.broadcasted_iota(jnp.int32, (MOE_TL, 128), 1)
    idx_ref[0] = _slot_columns(ids, lane128)
    wt_ref[0] = _slot_columns([s / denom * ROUTE_SCALE for s in sel], lane128)
    rank_ref[0] = _slot_columns([r.astype(jnp.int32) for r in ranks], lane128)
    run_sc[...] += jnp.sum(maskf, axis=0, keepdims=True)
    cnt_ref[...] = run_sc[...]


def _moe_in_call(x, sh, sc, rw, rb, sg, su, sd):
    t3 = lambda b, j: (b, j, 0)
    full2 = lambda b, j: (0, 0)
    per_b = lambda b, j: (b, 0, 0)
    ff = sg.shape[1]
    tri = jnp.asarray(np.tril(np.ones((MOE_TL, MOE_TL), np.float32), -1), BF16)
    slot = jax.ShapeDtypeStruct((BATCH, SEQ, SLOT_PAD), jnp.int32)
    return pl.pallas_call(
        _moe_in_kernel, grid=(BATCH, SEQ // MOE_TL),
        in_specs=[pl.BlockSpec((1, MOE_TL, D_MODEL), t3),
                  pl.BlockSpec((1, 1, D_MODEL), per_b),
                  pl.BlockSpec((1, 1, D_MODEL), per_b),
                  pl.BlockSpec((D_MODEL, N_EXPERTS), full2),
                  pl.BlockSpec((1, N_EXPERTS), full2),
                  pl.BlockSpec((D_MODEL, ff), full2),
                  pl.BlockSpec((D_MODEL, ff), full2),
                  pl.BlockSpec((ff, D_MODEL), full2),
                  pl.BlockSpec((MOE_TL, MOE_TL), full2)],
        out_specs=[pl.BlockSpec((1, MOE_TL, ROW_WORDS), t3),
                   pl.BlockSpec((1, MOE_TL, ROW_WORDS), t3),
                   pl.BlockSpec((1, MOE_TL, SLOT_PAD), t3),
                   pl.BlockSpec((1, MOE_TL, SLOT_PAD), t3),
                   pl.BlockSpec((1, MOE_TL, SLOT_PAD), t3),
                   pl.BlockSpec((1, N_EXPERTS), full2),
                   pl.BlockSpec((1, MOE_TL, D_MODEL), t3)],
        out_shape=[jax.ShapeDtypeStruct((BATCH, SEQ, ROW_WORDS), jnp.uint32),
                   jax.ShapeDtypeStruct((BATCH, SEQ, ROW_WORDS), jnp.uint32),
                   slot,
                   jax.ShapeDtypeStruct((BATCH, SEQ, SLOT_PAD), F32),
                   slot,
                   jax.ShapeDtypeStruct((1, N_EXPERTS), F32),
                   jax.ShapeDtypeStruct((BATCH, SEQ, D_MODEL), F32)],
        scratch_shapes=[pltpu.VMEM((1, N_EXPERTS), F32)],
        compiler_params=_params(("arbitrary", "arbitrary")), name="moe_router_shared",
    )(x, sh, sc, rw, rb, sg, su, sd, tri)


def _sc_mesh():
    return plsc.VectorSubcoreMesh(core_axis_name="c", subcore_axis_name="s")


def _sc_dispatch(h_words, dest, n_rows):
    n_tok = h_words.shape[0]

    @pl.kernel(out_type=jax.ShapeDtypeStruct((n_rows, ROW_WORDS), jnp.uint32), mesh=_sc_mesh(), scratch_types=[])
    def scatter_rows(h_hbm, i_hbm, o_hbm):
        def body(h_vmem, i_vmem):
            for k in range(TOP_K):
                pltpu.sync_copy(h_vmem, o_hbm.at[i_vmem.at[k]])

        pltpu.emit_pipeline(
            body, grid=(n_tok // SC_WINDOW,),
            in_specs=[pl.BlockSpec((SC_WINDOW, ROW_WORDS), index_map=lambda i: (i, 0)),
                      pl.BlockSpec((SLOT_PAD, SC_WINDOW), index_map=lambda i: (0, i))],
            out_specs=[],
            core_axis_name=("c", "s"), dimension_semantics=(pltpu.PARALLEL,),
        )(h_hbm, i_hbm)

    return scatter_rows(h_words, dest)


def _sc_collect(y_words, dest):
    n_tok = dest.shape[1]

    @pl.kernel(out_type=jax.ShapeDtypeStruct((TOP_K, n_tok, ROW_WORDS), jnp.uint32), mesh=_sc_mesh(),
               scratch_types=[])
    def gather_rows(y_hbm, i_hbm, o_hbm):
        def body(i_vmem, o_vmem):
            pltpu.sync_copy(y_hbm.at[i_vmem.at[0]], o_vmem.at[0])

        pltpu.emit_pipeline(
            body, grid=(TOP_K, n_tok // SC_WINDOW),
            in_specs=[pl.BlockSpec((1, SC_WINDOW), index_map=lambda k, i: (k, i))],
            out_specs=[pl.BlockSpec((1, SC_WINDOW, ROW_WORDS), index_map=lambda k, i: (k, i, 0))],
            core_axis_name=("c", "s"), dimension_semantics=(pltpu.PARALLEL, pltpu.PARALLEL),
        )(i_hbm, o_hbm)

    return gather_rows(y_words, dest)


def _expert_kernel(be_ref, nv_ref, xlo_ref, xhi_ref, wg_ref, wu_ref, wd_ref, ylo_ref, yhi_ref,
                   wg_sc, wu_sc, wd_sc):
    i = pl.program_id(0)
    nv = nv_ref[i]

    @pl.when(jnp.logical_or(i == 0, be_ref[i] != be_ref[jnp.maximum(i - 1, 0)]))
    def _():
        wg_sc[...] = wg_ref[0, 0].astype(BF16)
        wu_sc[...] = wu_ref[0, 0].astype(BF16)
        wd_sc[...] = wd_ref[0, 0].astype(BF16)

    @pl.when(nv > 0)
    def _():
        parts = _unpack_rows(xlo_ref[...], xhi_ref[...])
        xb = jnp.concatenate([p.astype(BF16) for p in parts], axis=1)
        live = lax.broadcasted_iota(jnp.int32, xb.shape, 0) < nv
        xb = jnp.where(live, xb, jnp.zeros_like(xb))
        hid = jax.nn.silu(jnp.dot(xb, wg_sc[...], preferred_element_type=F32))
        hid = hid * jnp.dot(xb, wu_sc[...], preferred_element_type=F32)
        y = jnp.dot(hid.astype(BF16), wd_sc[...], preferred_element_type=F32)
        ylo_ref[...], yhi_ref[...] = _pack_rows(y)

    @pl.when(nv == 0)
    def _():
        ylo_ref[...] = jnp.zeros_like(ylo_ref)
        yhi_ref[...] = jnp.zeros_like(yhi_ref)


def _expert_call(block_e, n_valid, xlo, xhi, wg, wu, wd, li):
    n_rows = xlo.shape[0]
    n_blocks = n_rows // MOE_BLOCK
    rows = pl.BlockSpec((MOE_BLOCK, ROW_WORDS), lambda i, be, nv: (i, 0))
    grid_spec = pltpu.PrefetchScalarGridSpec(
        num_scalar_prefetch=2, grid=(n_blocks,),
        in_specs=[rows, rows,
                  pl.BlockSpec((1, 1, D_MODEL, EXPERT_FF), lambda i, be, nv: (li, be[i], 0, 0)),
                  pl.BlockSpec((1, 1, D_MODEL, EXPERT_FF), lambda i, be, nv: (li, be[i], 0, 0)),
                  pl.BlockSpec((1, 1, EXPERT_FF, D_MODEL), lambda i, be, nv: (li, be[i], 0, 0))],
        out_specs=[rows, rows],
        scratch_shapes=[pltpu.VMEM((D_MODEL, EXPERT_FF), BF16), pltpu.VMEM((D_MODEL, EXPERT_FF), BF16),
                        pltpu.VMEM((EXPERT_FF, D_MODEL), BF16)])
    out = jax.ShapeDtypeStruct((n_rows, ROW_WORDS), jnp.uint32)
    return pl.pallas_call(
        _expert_kernel, grid_spec=grid_spec, out_shape=[out, out],
        compiler_params=_params(("arbitrary",)), name="moe_experts",
    )(block_e, n_valid, xlo, xhi, wg, wu, wd)


def _combine_kernel(ylo_ref, yhi_ref, w_ref, s_ref, x_ref, g_ref, o_ref):
    w = w_ref[0]
    acc = [None] * 4
    for k in range(TOP_K):
        wk = w[:, k:k + 1]
        for c, part in enumerate(_unpack_rows(ylo_ref[k], yhi_ref[k])):
            acc[c] = wk * part if acc[c] is None else acc[c] + wk * part
    for c in range(4):
        sl = slice(c * ROW_WORDS, (c + 1) * ROW_WORDS)
        o_ref[0, :, sl] = x_ref[0, :, sl] + g_ref[0, :, sl] * (acc[c] + s_ref[0, :, sl])


def _combine_call(ylo, yhi, wts, shared, x, g2):
    t3 = lambda b, j: (b, j, 0)
    nt = SEQ // MOE_TL
    rows = pl.BlockSpec((TOP_K, MOE_TL, ROW_WORDS), lambda b, j: (0, b * nt + j, 0))
    return pl.pallas_call(
        _combine_kernel, grid=(BATCH, nt),
        in_specs=[rows, rows,
                  pl.BlockSpec((1, MOE_TL, SLOT_PAD), t3),
                  pl.BlockSpec((1, MOE_TL, D_MODEL), t3),
                  pl.BlockSpec((1, MOE_TL, D_MODEL), t3),
                  pl.BlockSpec((1, 1, D_MODEL), lambda b, j: (b, 0, 0))],
        out_specs=pl.BlockSpec((1, MOE_TL, D_MODEL), t3),
        out_shape=jax.ShapeDtypeStruct((BATCH, SEQ, D_MODEL), F32),
        compiler_params=_params(("parallel", "arbitrary")), name="moe_combine",
    )(ylo, yhi, wts, shared, x, g2)


def _moe(x, sh, sc, g2, router_w, router_b, w_gate, w_up, w_down, sh_gate, sh_up, sh_down, li):
    T = BATCH * SEQ
    TK = T * TOP_K
    hlo, hhi, idx, wts, rank, counts, shared = _moe_in_call(
        x, sh, sc, router_w, router_b[None, :], sh_gate, sh_up, sh_down)
    counts = counts[0].astype(jnp.int32)
    padded = (counts + MOE_BLOCK - 1) // MOE_BLOCK * MOE_BLOCK
    pad_end = jnp.cumsum(padded)
    pad_start = pad_end - padded
    n_blocks = -(-TK // MOE_BLOCK) + N_EXPERTS
    n_rows = n_blocks * MOE_BLOCK
    block_start = jnp.arange(n_blocks, dtype=jnp.int32) * MOE_BLOCK
    owns = jnp.logical_and(block_start[:, None] >= pad_start[None, :], block_start[:, None] < pad_end[None, :])
    owns = owns.astype(jnp.int32)
    experts = jnp.arange(N_EXPERTS, dtype=jnp.int32)[None, :]
    block_e = jnp.sum(owns * experts, axis=1) + (N_EXPERTS - 1) * (1 - jnp.sum(owns, axis=1))
    n_valid = jnp.sum(owns * (counts[None, :] - (block_start[:, None] - pad_start[None, :])), axis=1)
    n_valid = jnp.clip(n_valid, 0, MOE_BLOCK).astype(jnp.int32)
    dest = (pad_start[idx.reshape(T, SLOT_PAD)] + rank.reshape(T, SLOT_PAD)).T.astype(jnp.int32)
    xlo = _sc_dispatch(hlo.reshape(T, ROW_WORDS), dest, n_rows)
    xhi = _sc_dispatch(hhi.reshape(T, ROW_WORDS), dest, n_rows)
    ylo, yhi = _expert_call(block_e, n_valid, xlo, xhi, w_gate, w_up, w_down, li)
    return _combine_call(_sc_collect(ylo, dest), _sc_collect(yhi, dest), wts, shared, x, g2)


HY_TL = 512
HALO = 8


def _hy_in_kernel(x_ref, xp_ref, xn_ref, sh_ref, sc_ref, w_ref, cw_ref, cb_ref, z_ref, x0_ref, h_sc):
    j = pl.program_id(1)
    shift, scale = sh_ref[0], sc_ref[0]
    keep_prev = (j > 0).astype(F32)
    keep_next = (j < SEQ // HY_TL - 1).astype(F32)
    h_sc[0:HALO, :] = _norm_mod(xp_ref[0], shift, scale) * keep_prev
    h_sc[HALO:HALO + HY_TL, :] = _norm_mod(x_ref[0], shift, scale)
    h_sc[HALO + HY_TL:, :] = _norm_mod(xn_ref[0], shift, scale) * keep_next
    hcat = h_sc[...].astype(BF16)
    outs = []
    for part in range(3):
        sl = slice(part * HY_WIDTH, (part + 1) * HY_WIDTH)
        p = jnp.dot(hcat, w_ref[:, sl], preferred_element_type=F32)
        o = (p[HALO - 1:HALO - 1 + HY_TL] * cw_ref[0:1, sl] + p[HALO:HALO + HY_TL] * cw_ref[1:2, sl]
             + p[HALO + 1:HALO + 1 + HY_TL] * cw_ref[2:3, sl] + cb_ref[:, sl])
        outs.append(o)
    x0_ref[0] = outs[0].astype(BF16)
    z_ref[0] = (outs[2] * outs[1]).astype(BF16)


def _hy_in_call(x, sh, sc, w, cw, cb):
    nb8 = HY_TL // HALO
    t3 = lambda b, j: (b, j, 0)
    full2 = lambda b, j: (0, 0)
    per_b = lambda b, j: (b, 0, 0)
    return pl.pallas_call(
        _hy_in_kernel, grid=(BATCH, SEQ // HY_TL),
        in_specs=[pl.BlockSpec((1, HY_TL, D_MODEL), t3),
                  pl.BlockSpec((1, HALO, D_MODEL), lambda b, j: (b, jnp.maximum(j * nb8 - 1, 0), 0)),
                  pl.BlockSpec((1, HALO, D_MODEL), lambda b, j: (b, jnp.minimum((j + 1) * nb8, SEQ // HALO - 1), 0)),
                  pl.BlockSpec((1, 1, D_MODEL), per_b),
                  pl.BlockSpec((1, 1, D_MODEL), per_b),
                  pl.BlockSpec((D_MODEL, 3 * HY_WIDTH), full2),
                  pl.BlockSpec((SHORT_CONV, 3 * HY_WIDTH), full2),
                  pl.BlockSpec((1, 3 * HY_WIDTH), full2)],
        out_specs=[pl.BlockSpec((1, HY_TL, HY_WIDTH), t3), pl.BlockSpec((1, HY_TL, HY_WIDTH), t3)],
        out_shape=[jax.ShapeDtypeStruct((BATCH, SEQ, HY_WIDTH), BF16),
                   jax.ShapeDtypeStruct((BATCH, SEQ, HY_WIDTH), BF16)],
        scratch_shapes=[pltpu.VMEM((HY_TL + 2 * HALO, D_MODEL), F32)],
        compiler_params=_params(("parallel", "arbitrary")), name="hyena_in_proj",
    )(x, x, x, sh, sc, w, cw, cb)


def _fft_tables():
    c = np.arange(FFT_N2, dtype=np.int64)
    ang = 2.0 * np.pi * ((c[:, None] * c[None, :]) % FFT_N2) / FFT_N2
    sr, si = np.cos(ang), -np.sin(ang)
    m = np.block([[sr, -si], [si, sr]])
    k1 = np.arange(FFT_NK, dtype=np.int64)
    ang_t = 2.0 * np.pi * (k1[:, None] * c[None, :]) / DFT_N
    lanes = np.ones((1, 1, 128))
    tr = np.cos(ang_t)[:, :, None] * lanes
    ti = -np.sin(ang_t)[:, :, None] * lanes
    return jnp.asarray(m, BF16), jnp.asarray(tr, F32), jnp.asarray(ti, F32)


def _lin(acc, coef, val):
    if abs(coef) < 1e-12:
        return acc
    term = val if coef == 1.0 else (-val if coef == -1.0 else coef * val)
    return term if acc is None else acc + term


def _twiddle(tr_ref, ti_ref, k1, width):
    reps = width // 128
    tr, ti = tr_ref[k1], ti_ref[k1]
    return jnp.concatenate([tr] * reps, axis=1), jnp.concatenate([ti] * reps, axis=1)


def _class_spectrum(block, k1, m_ref, tr_ref, ti_ref, width):
    yr = yi = None
    for a in range(FFT_NA):
        th = 2.0 * math.pi * ((a * k1) % FFT_N1) / FFT_N1
        za = block(a)
        yr = _lin(yr, round(math.cos(th), 15), za)
        yi = _lin(yi, round(-math.sin(th), 15), za)
    if k1 > 0:
        tr, ti = _twiddle(tr_ref, ti_ref, k1, width)
        yr, yi = (yr * tr, yr * ti) if yi is None else (yr * tr - yi * ti, yr * ti + yi * tr)
    if yi is None:
        x = jnp.dot(m_ref[:, :FFT_N2], yr.astype(BF16), preferred_element_type=F32)
    else:
        x = jnp.dot(m_ref[...], jnp.concatenate([yr, yi], axis=0).astype(BF16), preferred_element_type=F32)
    return x[:FFT_N2], x[FFT_N2:]


def _class_inverse(yr, yi, k1, m_ref, tr_ref, ti_ref, acc_ref, width):
    v = jnp.dot(m_ref[...], jnp.concatenate([yr, -yi], axis=0).astype(BF16), preferred_element_type=F32)
    ur, ui = v[:FFT_N2], -v[FFT_N2:]
    if k1 > 0:
        tr, ti = _twiddle(tr_ref, ti_ref, k1, width)
        ur, ui = ur * tr + ui * ti, ui * tr - ur * ti
    scale = (1.0 if k1 in (0, FFT_N1 // 2) else 2.0) / DFT_N
    for a in range(FFT_NA):
        th = 2.0 * math.pi * ((a * k1) % FFT_N1) / FFT_N1
        term = _lin(None, round(math.cos(th), 15) * scale, ur)
        term = _lin(term, round(-math.sin(th), 15) * scale, ui)
        rows = slice(a * FFT_N2, (a + 1) * FFT_N2)
        if k1 == 0:
            acc_ref[rows, :] = term
        else:
            acc_ref[rows, :] += term


def _spec_kernel(hf_ref, hb_ref, m_ref, tr_ref, ti_ref, c_ref):
    for k1 in range(FFT_NK):
        fr, fi = _class_spectrum(lambda a: hf_ref[a * FFT_N2:(a + 1) * FFT_N2, :], k1, m_ref, tr_ref, ti_ref, HY_CT)
        br, bi = _class_spectrum(lambda a: hb_ref[a * FFT_N2:(a + 1) * FFT_N2, :], k1, m_ref, tr_ref, ti_ref, HY_CT)
        c_ref[k1, :FFT_N2, :] = (fr + br).astype(BF16)
        c_ref[k1, FFT_N2:, :] = (fi - bi).astype(BF16)


def _fft_table_specs(ngrid):
    z = (0,) * 2
    z3 = (0,) * 3
    if ngrid == 1:
        return [pl.BlockSpec((2 * FFT_N2, 2 * FFT_N2), lambda c: z),
                pl.BlockSpec((FFT_NK, FFT_N2, 128), lambda c: z3),
                pl.BlockSpec((FFT_NK, FFT_N2, 128), lambda c: z3)]
    return [pl.BlockSpec((2 * FFT_N2, 2 * FFT_N2), lambda b, c: z),
            pl.BlockSpec((FFT_NK, FFT_N2, 128), lambda b, c: z3),
            pl.BlockSpec((FFT_NK, FFT_N2, 128), lambda b, c: z3)]


def _spec_call(hfb, m, tr, ti):
    nct = HY_WIDTH // HY_CT
    return pl.pallas_call(
        _spec_kernel, grid=(nct,),
        in_specs=[pl.BlockSpec((SEQ, HY_CT), lambda c: (0, c)),
                  pl.BlockSpec((SEQ, HY_CT), lambda c: (0, c + nct))] + _fft_table_specs(1),
        out_specs=pl.BlockSpec((FFT_NK, 2 * FFT_N2, HY_CT), lambda c: (0, 0, c)),
        out_shape=jax.ShapeDtypeStruct((FFT_NK, 2 * FFT_N2, HY_WIDTH), BF16),
        compiler_params=_params(("arbitrary",)), name="hyena_filter_spectrum",
    )(hfb, hfb, m, tr, ti)


def _conv_kernel(z_ref, c_ref, m_ref, tr_ref, ti_ref, y_ref, acc):
    for k1 in range(FFT_NK):
        xr, xi = _class_spectrum(lambda a: z_ref[0, a * FFT_N2:(a + 1) * FFT_N2, :].astype(F32), k1,
                                 m_ref, tr_ref, ti_ref, HY_CT)
        cr = c_ref[k1, :FFT_N2, :].astype(F32)
        ci = c_ref[k1, FFT_N2:, :].astype(F32)
        _class_inverse(xr * cr - xi * ci, xr * ci + xi * cr, k1, m_ref, tr_ref, ti_ref, acc, HY_CT)
    y_ref[0] = acc[...].astype(BF16)


def _conv_call(z, spec, m, tr, ti):
    return pl.pallas_call(
        _conv_kernel, grid=(HY_WIDTH // HY_CT, BATCH),
        in_specs=[pl.BlockSpec((1, SEQ, HY_CT), lambda c, b: (b, 0, c)),
                  pl.BlockSpec((FFT_NK, 2 * FFT_N2, HY_CT), lambda c, b: (0, 0, c))] + _fft_table_specs(2),
        out_specs=pl.BlockSpec((1, SEQ, HY_CT), lambda c, b: (b, 0, c)),
        out_shape=jax.ShapeDtypeStruct((BATCH, SEQ, HY_WIDTH), BF16),
        scratch_shapes=[pltpu.VMEM((SEQ, HY_CT), F32)],
        compiler_params=_params(("parallel", "arbitrary")), name="hyena_long_conv",
    )(z, spec, m, tr, ti)


def _hy_out_kernel(y_ref, z_ref, x0_ref, b_ref, w_ref, x_ref, g_ref, o_ref):
    z = z_ref[0].astype(F32)
    gated = x0_ref[0].astype(F32) * (y_ref[0].astype(F32) + b_ref[...] * z)
    mix = jnp.dot(gated.astype(BF16), w_ref[...], preferred_element_type=F32)
    o_ref[0] = x_ref[0] + g_ref[0] * mix


def _hy_out_call(y, z, x0, bias, w, x, g1):
    t3 = lambda b, j: (b, j, 0)
    full2 = lambda b, j: (0, 0)
    return pl.pallas_call(
        _hy_out_kernel, grid=(BATCH, SEQ // HY_TL),
        in_specs=[pl.BlockSpec((1, HY_TL, HY_WIDTH), t3),
                  pl.BlockSpec((1, HY_TL, HY_WIDTH), t3),
                  pl.BlockSpec((1, HY_TL, HY_WIDTH), t3),
                  pl.BlockSpec((1, HY_WIDTH), full2),
                  pl.BlockSpec((HY_WIDTH, D_MODEL), full2),
                  pl.BlockSpec((1, HY_TL, D_MODEL), t3),
                  pl.BlockSpec((1, 1, D_MODEL), lambda b, j: (b, 0, 0))],
        out_specs=pl.BlockSpec((1, HY_TL, D_MODEL), t3),
        out_shape=jax.ShapeDtypeStruct((BATCH, SEQ, D_MODEL), F32),
        compiler_params=_params(("parallel", "arbitrary")), name="hyena_out_proj",
    )(y, z, x0, bias, w, x, g1)


def _hyena_filter(w1, b1, w2, b2, w3, freq):
    hi = lax.Precision.HIGHEST
    Lq = SEQ
    t = jnp.linspace(0.0, 1.0, Lq, dtype=F32)[:, None]
    ang = 2.0 * math.pi * jnp.arange(Lq, dtype=F32)[:, None] / Lq
    bands = jnp.linspace(1e-4, FILT_BANDS - 1, FILT_BANDS, dtype=F32)
    z = jnp.concatenate([t, jnp.cos(bands * ang), -jnp.sin(bands * ang)], axis=-1)
    hid = jnp.sin(freq * (jnp.dot(z, w1, precision=hi) + b1))
    hid = jnp.sin(freq * (jnp.dot(hid, w2, precision=hi) + b2))
    deltas = jnp.linspace(HY_MIN_DECAY, HY_MAX_DECAY, HY_WIDTH, dtype=F32)
    hf = jnp.dot(hid, w3, precision=hi) * jnp.exp(-t * jnp.tile(deltas, 2))
    ssq = jnp.sum(hf * hf, axis=0)
    ssq = ssq[:HY_WIDTH] + ssq[HY_WIDTH:]
    return hf * jnp.tile(lax.rsqrt(ssq + EPS), 2)


def kernel(x, c, ctx, c_ctx, ada_w, ada_b, ev_w_in, mla_q_norm, mla_w_uq, mla_kv_norm, mla_w_ukv, mla_q_qknorm, mla_k_qknorm, s5_lam_re, s5_lam_im, s5_log_step, s5_b_re, s5_b_im, s5_c_re, s5_c_im, s5_d, s5_glu_w, s5_glu_b, ev_w_out, hy_w_in, hy_conv_w, hy_conv_b, hy_f_w1, hy_f_b1, hy_f_w2, hy_f_b2, hy_f_w3, hy_f_freq, hy_bias, hy_w_out, moe_router_w, moe_router_b, moe_w_gate, moe_w_up, moe_w_down, moe_sh_gate, moe_sh_up, moe_sh_down):
    hi = lax.Precision.HIGHEST
    D = D_MODEL
    sc = jax.nn.silu(c)
    sc_ctx = jax.nn.silu(c_ctx)

    def mods(li):
        mod = jnp.dot(sc, ada_w[li], precision=hi) + ada_b[li]
        return [m[:, None, :] for m in jnp.split(mod, 6, axis=-1)]

    sh1, sc1, g1, sh2, sc2, g2 = mods(0)
    mod_ctx = jnp.dot(sc_ctx, ada_w[0][:, :2 * D], precision=hi) + ada_b[0][:2 * D]
    w0 = _a0_weights(ev_w_in[0], mla_q_norm[0], mla_w_uq[0], mla_kv_norm[0], mla_w_ukv[0],
                     mla_q_qknorm[0], mla_k_qknorm[0])
    q, k, v, u = _a0_call(x, ctx, sh1, sc1, mod_ctx[None, :D], mod_ctx[None, D:], w0)
    att = _attn_call(q, k, v)
    ug = u.reshape(BATCH, S5_NCHUNK, S5_CHUNK, S5_GROUPS, S5_GROUP)
    ug = ug.transpose(3, 1, 0, 2, 4).reshape(S5_GROUPS, S5_ROWS, S5_COLS)
    ys = _s5_call(ug, *_s5_weights(s5_lam_re[0], s5_lam_im[0], s5_log_step[0], s5_b_re[0], s5_b_im[0],
                                   s5_c_re[0], s5_c_im[0]))
    ys = ys.reshape(S5_GROUPS, S5_NCHUNK_LAT, BATCH, S5_CHUNK, S5_GROUP)
    ys = ys.transpose(2, 1, 3, 0, 4).reshape(BATCH, SEQ, S5_WIDTH)
    wo = ev_w_out[0].astype(BF16)
    wo_att = jnp.concatenate([wo[:MLA_WIDTH].reshape(MLA_HEADS, V_HEAD, D_MODEL),
                              jnp.zeros((MLA_HEADS, HEAD_PAD - V_HEAD, D_MODEL), BF16)], axis=1)
    x = _a1_call(att, ys, u, s5_d[0][None, :], s5_glu_w[0].astype(BF16), s5_glu_b[0][None, :],
                 wo_att.reshape(MLA_HEADS * HEAD_PAD, D_MODEL), wo[MLA_WIDTH:], x, g1)
    x = _moe(x, sh2, sc2, g2, moe_router_w[0], moe_router_b[0], moe_w_gate, moe_w_up, moe_w_down,
             moe_sh_gate[0], moe_sh_up[0], moe_sh_down[0], 0)

    sh1, sc1, g1, sh2, sc2, g2 = mods(1)
    z, x0 = _hy_in_call(x, sh1, sc1, hy_w_in[0].astype(BF16), hy_conv_w[0], hy_conv_b[0][None, :])
    fft_tabs = _fft_tables()
    hfb = _hyena_filter(hy_f_w1[0], hy_f_b1[0], hy_f_w2[0], hy_f_b2[0], hy_f_w3[0], hy_f_freq[0])
    y = _conv_call(z, _spec_call(hfb, *fft_tabs), *fft_tabs)
    x = _hy_out_call(y, z, x0, hy_bias[0][None, :], hy_w_out[0].astype(BF16), x, g1)
    x = _moe(x, sh2, sc2, g2, moe_router_w[1], moe_router_b[1], moe_w_gate, moe_w_up, moe_w_down,
             moe_sh_gate[1], moe_sh_up[1], moe_sh_down[1], 1)
    return x
```

```python
import functools
import math

import numpy as np
import jax
import jax.numpy as jnp
from jax import lax
from jax.experimental import pallas as pl
from jax.experimental.pallas import tpu as pltpu
from jax.experimental.pallas import tpu_sc as plsc

F32 = jnp.float32
BF16 = jnp.bfloat16

D_MODEL = 1024
BATCH = 8
SEQ = 4096
CTX_LEN = 256
KV_LEN = SEQ + CTX_LEN
GRID_W = 64
EPS = 1e-6

MLA_HEADS = 8
QK_NOPE = 64
QK_ROPE = 32
QK_HEAD = QK_NOPE + QK_ROPE
V_HEAD = 64
Q_LORA = 256
KV_LORA = 128
MLA_WIDTH = MLA_HEADS * V_HEAD
ROPE_BASE = 10000.0
HEAD_PAD = 128

S5_WIDTH = 512
S5_GROUP = 16
S5_GROUPS = S5_WIDTH // S5_GROUP
S5_STATE = 64
S5_CHUNK = 32
S5_NCHUNK = KV_LEN // S5_CHUNK
S5_NCHUNK_LAT = SEQ // S5_CHUNK
S5_NCHUNK_CTX = CTX_LEN // S5_CHUNK

HY_WIDTH = D_MODEL
FILT_EMB = 33
FILT_BANDS = (FILT_EMB - 1) // 2
SHORT_CONV = 3
HY_MIN_DECAY = -math.log(1e-2) / 1.5
HY_MAX_DECAY = -math.log(1e-2) / 0.3
DFT_N = 2 * SEQ
FFT_N1 = 16
FFT_N2 = DFT_N // FFT_N1
FFT_NA = FFT_N1 // 2
FFT_NK = FFT_N1 // 2 + 1
HY_CT = 256

N_EXPERTS = 64
TOP_K = 6
EXPERT_FF = 256
ROUTE_SCALE = 2.5
MOE_BLOCK = 1024
ROW_WORDS = D_MODEL // 4
SC_WINDOW = 128

V7X_VMEM_BYTES = 64 * 1024 * 1024
VMEM_LIMIT = V7X_VMEM_BYTES - 8 * 1024 * 1024


def _params(semantics):
    return pltpu.CompilerParams(dimension_semantics=semantics, vmem_limit_bytes=VMEM_LIMIT)


def _norm_mod(x, shift, scale):
    ms = jnp.mean(x * x, axis=-1, keepdims=True)
    return x * lax.rsqrt(ms + EPS) * (1.0 + scale) + shift


def _rms(x, gain, n):
    ms = jnp.sum(x * x, axis=-1, keepdims=True) * (1.0 / n)
    return x * lax.rsqrt(ms + EPS) * gain


A0_TL = 256
A0_NT = SEQ // A0_TL


def _rope_perm():
    return np.concatenate([np.arange(0, QK_ROPE, 2), np.arange(1, QK_ROPE, 2)])


def _rope_tables():
    t = np.arange(SEQ)
    row = (t // GRID_W).astype(np.float64)
    col = (t % GRID_W).astype(np.float64)
    n_freq = QK_ROPE // 4
    inv = ROPE_BASE ** (-np.arange(n_freq, dtype=np.float64) / n_freq)
    ang = np.concatenate([row[:, None] * inv, col[:, None] * inv], axis=-1)
    cos, sin = np.cos(ang), np.sin(ang)
    half = QK_ROPE // 2
    a = np.zeros((KV_LEN, HEAD_PAD))
    bp = np.zeros((KV_LEN, HEAD_PAD))
    bm = np.zeros((KV_LEN, HEAD_PAD))
    a[:, :QK_HEAD] = 1.0
    a[:SEQ, QK_NOPE:QK_NOPE + half] = cos
    a[:SEQ, QK_NOPE + half:QK_HEAD] = cos
    bp[:SEQ, QK_NOPE + half:QK_HEAD] = sin
    bm[:SEQ, QK_NOPE:QK_NOPE + half] = -sin
    return a, bp, bm


def _a0_kernel(x_ref, ctx_ref, sh_ref, sc_ref, shc_ref, scc_ref, win_ref, qn_ref, wuq_ref, kvn_ref,
               wk_ref, wuv_ref, qg_ref, kg_ref, ka_ref, kp_ref, km_ref, qa_ref, qp_ref, qm_ref,
               q_ref, k_ref, v_ref, u_ref):
    j = pl.program_id(1)
    is_ctx = j == A0_NT
    xin = jnp.where(is_ctx, ctx_ref[0], x_ref[0])
    shift = jnp.where(is_ctx, shc_ref[...], sh_ref[0])
    scale = jnp.where(is_ctx, scc_ref[...], sc_ref[0])
    h = _norm_mod(xin, shift, scale).astype(BF16)
    proj = jnp.dot(h, win_ref[...], preferred_element_type=F32)
    u_ref[0] = proj[:, 512:].astype(BF16)

    c_kv = _rms(proj[:, Q_LORA:Q_LORA + KV_LORA], kvn_ref[...], KV_LORA).astype(BF16)
    lane = lax.broadcasted_iota(jnp.int32, (1, MLA_HEADS * HEAD_PAD), 1)
    ones_lane = (lane % HEAD_PAD == V_HEAD).astype(F32)
    v_ref[0] = (jnp.dot(c_kv, wuv_ref[...], preferred_element_type=F32) + ones_lane).astype(BF16)
    kin = jnp.concatenate([c_kv, proj[:, 384:512].astype(BF16)], axis=1)
    kf = jnp.dot(kin, wk_ref[...], preferred_element_type=F32)
    ka, kp, km = ka_ref[...], kp_ref[...], km_ref[...]
    for hd in range(MLA_HEADS):
        sl = slice(hd * HEAD_PAD, (hd + 1) * HEAD_PAD)
        kh = _rms(kf[:, sl], kg_ref[:, sl], QK_HEAD)
        kh = kh * ka + pltpu.roll(kh, 16, 1) * kp + pltpu.roll(kh, HEAD_PAD - 16, 1) * km
        k_ref[0, :, sl] = kh.astype(BF16)

    @pl.when(j < A0_NT)
    def _():
        ql = _rms(proj[:, :Q_LORA], qn_ref[...], Q_LORA).astype(BF16)
        qf = jnp.dot(ql, wuq_ref[...], preferred_element_type=F32)
        qa, qp, qm = qa_ref[...], qp_ref[...], qm_ref[...]
        for hd in range(MLA_HEADS):
            sl = slice(hd * HEAD_PAD, (hd + 1) * HEAD_PAD)
            qh = _rms(qf[:, sl], qg_ref[:, sl], QK_HEAD)
            qh = qh * qa + pltpu.roll(qh, 16, 1) * qp + pltpu.roll(qh, HEAD_PAD - 16, 1) * qm
            q_ref[0, :, sl] = qh.astype(BF16)


def _a0_call(x, ctx, sh, sc, shc, scc, w):
    nt = A0_NT
    lat = lambda b, j: (b, jnp.minimum(j, nt - 1), 0)
    full2 = lambda b, j: (0, 0)
    per_b = lambda b, j: (b, 0, 0)
    tab = pl.BlockSpec((A0_TL, HEAD_PAD), lambda b, j: (j, 0))
    in_specs = [
        pl.BlockSpec((1, A0_TL, D_MODEL), lat),
        pl.BlockSpec((1, CTX_LEN, D_MODEL), per_b),
        pl.BlockSpec((1, 1, D_MODEL), per_b),
        pl.BlockSpec((1, 1, D_MODEL), per_b),
        pl.BlockSpec((1, D_MODEL), full2),
        pl.BlockSpec((1, D_MODEL), full2),
        pl.BlockSpec((D_MODEL, 1024), full2),
        pl.BlockSpec((1, Q_LORA), full2),
        pl.BlockSpec((Q_LORA, MLA_HEADS * HEAD_PAD), full2),
        pl.BlockSpec((1, KV_LORA), full2),
        pl.BlockSpec((2 * KV_LORA, MLA_HEADS * HEAD_PAD), full2),
        pl.BlockSpec((KV_LORA, MLA_HEADS * HEAD_PAD), full2),
        pl.BlockSpec((1, MLA_HEADS * HEAD_PAD), full2),
        pl.BlockSpec((1, MLA_HEADS * HEAD_PAD), full2),
        tab, tab, tab, tab, tab, tab,
    ]
    out_specs = [
        pl.BlockSpec((1, A0_TL, MLA_HEADS * HEAD_PAD), lat),
        pl.BlockSpec((1, A0_TL, MLA_HEADS * HEAD_PAD), lambda b, j: (b, j, 0)),
        pl.BlockSpec((1, A0_TL, MLA_HEADS * HEAD_PAD), lambda b, j: (b, j, 0)),
        pl.BlockSpec((1, A0_TL, S5_WIDTH), lambda b, j: (b, j, 0)),
    ]
    out_shape = [
        jax.ShapeDtypeStruct((BATCH, SEQ, MLA_HEADS * HEAD_PAD), BF16),
        jax.ShapeDtypeStruct((BATCH, KV_LEN, MLA_HEADS * HEAD_PAD), BF16),
        jax.ShapeDtypeStruct((BATCH, KV_LEN, MLA_HEADS * HEAD_PAD), BF16),
        jax.ShapeDtypeStruct((BATCH, KV_LEN, S5_WIDTH), BF16),
    ]
    return pl.pallas_call(
        _a0_kernel, grid=(BATCH, nt + 1), in_specs=in_specs, out_specs=out_specs, out_shape=out_shape,
        compiler_params=_params(("parallel", "arbitrary")), name="even_in_proj",
    )(x, ctx, sh, sc, shc, scc, *w)


def _a0_weights(w_in, q_norm, w_uq, kv_norm, w_ukv, q_qk, k_qk):
    perm = _rope_perm()
    kr0 = Q_LORA + KV_LORA
    w_cat = jnp.concatenate([
        w_in[:, :kr0], w_in[:, kr0:kr0 + QK_ROPE][:, perm],
        jnp.zeros((D_MODEL, HEAD_PAD - QK_ROPE), F32), w_in[:, kr0 + QK_ROPE:]], axis=1).astype(BF16)
    pad = HEAD_PAD - QK_HEAD

    def head_gain(g):
        gh = jnp.concatenate([g[:QK_NOPE], g[QK_NOPE:][perm], jnp.zeros((pad,), F32)])
        return jnp.tile(gh, MLA_HEADS)[None, :]

    uq = w_uq.reshape(Q_LORA, MLA_HEADS, QK_HEAD)
    uq = jnp.concatenate([uq[..., :QK_NOPE], uq[..., QK_NOPE:][..., perm],
                          jnp.zeros((Q_LORA, MLA_HEADS, pad), F32)], axis=-1)
    uq = uq.reshape(Q_LORA, MLA_HEADS * HEAD_PAD).astype(BF16)
    ukv = w_ukv.reshape(KV_LORA, MLA_HEADS, QK_NOPE + V_HEAD)
    uk = jnp.concatenate([ukv[..., :QK_NOPE], jnp.zeros((KV_LORA, MLA_HEADS, HEAD_PAD - QK_NOPE), F32)], axis=-1)
    uk = uk.reshape(KV_LORA, MLA_HEADS * HEAD_PAD)
    place = np.zeros((KV_LORA, MLA_HEADS, HEAD_PAD), np.float32)
    for i in range(QK_ROPE):
        place[i, :, QK_NOPE + i] = 1.0
    wk = jnp.concatenate([uk, jnp.asarray(place.reshape(KV_LORA, MLA_HEADS * HEAD_PAD))], axis=0).astype(BF16)
    wuv = jnp.concatenate([ukv[..., QK_NOPE:], jnp.zeros((KV_LORA, MLA_HEADS, HEAD_PAD - V_HEAD), F32)], axis=-1)
    wuv = wuv.reshape(KV_LORA, MLA_HEADS * HEAD_PAD).astype(BF16)
    a, bp, bm = _rope_tables()
    qs = QK_HEAD ** -0.5 * math.log2(math.e)
    tabs = [jnp.asarray(t, F32) for t in (a, bp, bm, a * qs, bp * qs, bm * qs)]
    return [w_cat, q_norm[None, :], uq, kv_norm[None, :], wk, wuv, head_gain(q_qk), head_gain(k_qk)] + tabs


ATT_TQ = 256
HEADS_PER_STEP = 4


def _attn_kernel(q_ref, k_ref, v_ref, o_ref):
    for hh in range(HEADS_PER_STEP):
        sl = slice(hh * HEAD_PAD, (hh + 1) * HEAD_PAD)
        s = lax.dot_general(q_ref[0, :, sl], k_ref[0, :, sl], (((1,), (1,)), ((), ())),
                            preferred_element_type=F32)
        m = jnp.max(s, axis=-1, keepdims=True)
        p = jnp.exp2(s - m).astype(BF16)
        acc = jnp.dot(p, v_ref[0, :, sl], preferred_element_type=F32)
        o_ref[0, :, sl] = (acc * (1.0 / acc[:, V_HEAD:V_HEAD + 1])).astype(BF16)


def _attn_call(q, k, v):
    wq = HEADS_PER_STEP * HEAD_PAD
    return pl.pallas_call(
        _attn_kernel, grid=(BATCH, MLA_HEADS // HEADS_PER_STEP, SEQ // ATT_TQ),
        in_specs=[pl.BlockSpec((1, ATT_TQ, wq), lambda b, h, i: (b, i, h)),
                  pl.BlockSpec((1, KV_LEN, wq), lambda b, h, i: (b, 0, h)),
                  pl.BlockSpec((1, KV_LEN, wq), lambda b, h, i: (b, 0, h))],
        out_specs=pl.BlockSpec((1, ATT_TQ, wq), lambda b, h, i: (b, i, h)),
        out_shape=jax.ShapeDtypeStruct((BATCH, SEQ, MLA_HEADS * HEAD_PAD), BF16),
        compiler_params=_params(("parallel", "parallel", "arbitrary")), name="mla_attention",
    )(q, k, v)


S5_ROWS = S5_NCHUNK * BATCH
S5_ROWS_LAT = S5_NCHUNK_LAT * BATCH
S5_COLS = S5_CHUNK * S5_GROUP
S5_SW = 2 * S5_STATE


def _s5_kernel(u_ref, r_ref, mb_ref, mc_ref, coef_ref, y_ref, x_sc, sp_sc, t_sc):
    u = u_ref[0]
    lags = r_ref[0]
    for sg in range(S5_CHUNK):
        off = (S5_CHUNK - 1 - sg) * S5_GROUP
        t_sc[sg * S5_GROUP:(sg + 1) * S5_GROUP, :] = lags[:, off:off + S5_COLS].astype(BF16)
    x_sc[...] = jnp.dot(u, mb_ref[0], preferred_element_type=F32)
    cf = coef_ref[0]
    af, bfm, bfp, ab, bbm, bbp = [cf[i * 8:(i + 1) * 8] for i in range(6)]

    def body(i, carry):
        sf, sfw, sb, sbw = carry
        cfw = jnp.where(i < S5_NCHUNK_CTX, i + S5_NCHUNK_LAT, i - S5_NCHUNK_CTX)
        rf = pl.multiple_of(cfw * BATCH, BATCH)
        rb = pl.multiple_of((S5_NCHUNK - 1 - i) * BATCH, BATCH)
        sp_sc[pl.ds(rf, BATCH), 0:S5_SW] = sf
        sp_sc[pl.ds(rb, BATCH), S5_SW:2 * S5_SW] = sb
        xf = x_sc[pl.ds(rf, BATCH), 0:S5_SW]
        xfw = x_sc[pl.ds(rf, BATCH), S5_SW:2 * S5_SW]
        xb = x_sc[pl.ds(rb, BATCH), 2 * S5_SW:3 * S5_SW]
        xbw = x_sc[pl.ds(rb, BATCH), 3 * S5_SW:4 * S5_SW]
        return (sf * af + sfw * bfm + xf, sfw * af + sf * bfp + xfw,
                sb * ab + sbw * bbm + xb, sbw * ab + sb * bbp + xbw)

    z = jnp.zeros((BATCH, S5_SW), F32)
    lax.fori_loop(0, S5_NCHUNK, body, (z, z, z, z))
    y = jnp.dot(u[:S5_ROWS_LAT], t_sc[...], preferred_element_type=F32)
    y = y + jnp.dot(sp_sc[0:S5_ROWS_LAT, :].astype(BF16), mc_ref[0], preferred_element_type=F32)
    y_ref[0] = y.astype(BF16)


def _s5_call(ug, t, mb, mc, coef):
    g3 = lambda g: (g, 0, 0)
    return pl.pallas_call(
        _s5_kernel, grid=(S5_GROUPS,),
        in_specs=[pl.BlockSpec((1, S5_ROWS, S5_COLS), g3),
                  pl.BlockSpec((1, S5_GROUP, 2 * S5_COLS), g3),
                  pl.BlockSpec((1, S5_COLS, 4 * S5_SW), g3),
                  pl.BlockSpec((1, 2 * S5_SW, S5_COLS), g3),
                  pl.BlockSpec((1, 6 * 8, S5_SW), g3)],
        out_specs=pl.BlockSpec((1, S5_ROWS_LAT, S5_COLS), g3),
        out_shape=jax.ShapeDtypeStruct((S5_GROUPS, S5_ROWS_LAT, S5_COLS), BF16),
        scratch_shapes=[pltpu.VMEM((S5_ROWS, 4 * S5_SW), F32), pltpu.VMEM((S5_ROWS, 2 * S5_SW), F32),
                        pltpu.VMEM((S5_COLS, S5_COLS), BF16)],
        compiler_params=_params(("parallel",)), name="s5_chunked_scan",
    )(ug, t, mb, mc, coef)


def _s5_weights(lam_re, lam_im, log_step, b_re, b_im, c_re, c_im):
    q = S5_CHUNK
    hi = lax.Precision.HIGHEST
    t_blocks, mbs, mcs, coefs = [], [], [], []
    sig = jnp.arange(q)
    for d in range(2):
        lr = jnp.minimum(lam_re[d], -1e-4)
        li = lam_im[d]
        step = jnp.exp(log_step[d])[:, None]
        jj = jnp.arange(q + 1, dtype=F32)[:, None, None]
        mag = jnp.exp(lr * step * jj)
        ph = li * step * jj
        pr, pi = mag * jnp.cos(ph), mag * jnp.sin(ph)
        nr, ni = pr[1] - 1.0, pi[1]
        den = lr * lr + li * li
        fr, fi = (nr * lr + ni * li) / den, (ni * lr - nr * li) / den
        br = fr[..., None] * b_re[d] - fi[..., None] * b_im[d]
        bi = fr[..., None] * b_im[d] + fi[..., None] * b_re[d]
        cr, ci = c_re[d], c_im[d]
        cpr = cr[None] * pr[:, :, None, :] - ci[None] * pi[:, :, None, :]
        cpi = cr[None] * pi[:, :, None, :] + ci[None] * pr[:, :, None, :]
        kern = (jnp.einsum('jghp,gpk->jghk', cpr[:q], br, precision=hi)
                - jnp.einsum('jghp,gpk->jghk', cpi[:q], bi, precision=hi))
        kt = kern.transpose(1, 3, 0, 2)
        zero_slots = jnp.zeros((S5_GROUPS, S5_GROUP, q, S5_GROUP), F32)
        if d == 0:
            t_blocks.append(jnp.concatenate([zero_slots[:, :, :q - 1], kt, zero_slots[:, :, :1]], axis=2))
        else:
            t_blocks.append(jnp.concatenate([kt[:, :, ::-1], zero_slots], axis=2))
        pw = (q - 1 - sig) if d == 0 else sig
        xr = pr[pw][..., None] * br[None] - pi[pw][..., None] * bi[None]
        xi = pr[pw][..., None] * bi[None] + pi[pw][..., None] * br[None]
        xr = xr.transpose(1, 0, 3, 2).reshape(S5_GROUPS, S5_COLS, S5_STATE)
        xi = xi.transpose(1, 0, 3, 2).reshape(S5_GROUPS, S5_COLS, S5_STATE)
        mbs += [xr, xi, xi, xr]
        po = (sig + 1) if d == 0 else (q - sig)
        mr = cpr[po].transpose(1, 3, 0, 2).reshape(S5_GROUPS, S5_STATE, S5_COLS)
        mi = -cpi[po].transpose(1, 3, 0, 2).reshape(S5_GROUPS, S5_STATE, S5_COLS)
        mcs += [mr, mi]
        are, aim = pr[q], pi[q]
        rows = [jnp.concatenate([are, are], -1), jnp.concatenate([-aim, aim], -1), jnp.concatenate([aim, -aim], -1)]
        coefs += [jnp.broadcast_to(r[:, None, :], (S5_GROUPS, 8, S5_SW)) for r in rows]
    t = (t_blocks[0] + t_blocks[1]).reshape(S5_GROUPS, S5_GROUP, 2 * S5_COLS)
    mb = jnp.concatenate(mbs, axis=-1).astype(BF16)
    mc = jnp.concatenate(mcs, axis=1).astype(BF16)
    coef = jnp.concatenate(coefs, axis=1)
    return t, mb, mc, coef


A1_TL = 512


def _a1_kernel(att_ref, ys_ref, u_ref, d_ref, gw_ref, gb_ref, woa_ref, wos_ref, x_ref, g_ref, o_ref):
    y = u_ref[0].astype(F32) * d_ref[...] + ys_ref[0].astype(F32)
    z = jax.nn.gelu(y)
    gate = jax.nn.sigmoid(jnp.dot(z.astype(BF16), gw_ref[...], preferred_element_type=F32) + gb_ref[...])
    s5 = (z * gate).astype(BF16)
    mix = jnp.dot(att_ref[0], woa_ref[...], preferred_element_type=F32)
    mix = mix + jnp.dot(s5, wos_ref[...], preferred_element_type=F32)
    o_ref[0] = x_ref[0] + g_ref[0] * mix


def _a1_call(att, ys, u, d, gw, gb, woa, wos, x, g1):
    t3 = lambda b, j: (b, j, 0)
    full2 = lambda b, j: (0, 0)
    return pl.pallas_call(
        _a1_kernel, grid=(BATCH, SEQ // A1_TL),
        in_specs=[pl.BlockSpec((1, A1_TL, MLA_HEADS * HEAD_PAD), t3),
                  pl.BlockSpec((1, A1_TL, S5_WIDTH), t3),
                  pl.BlockSpec((1, A1_TL, S5_WIDTH), t3),
                  pl.BlockSpec((1, S5_WIDTH), full2),
                  pl.BlockSpec((S5_WIDTH, S5_WIDTH), full2),
                  pl.BlockSpec((1, S5_WIDTH), full2),
                  pl.BlockSpec((MLA_HEADS * HEAD_PAD, D_MODEL), full2),
                  pl.BlockSpec((S5_WIDTH, D_MODEL), full2),
                  pl.BlockSpec((1, A1_TL, D_MODEL), t3),
                  pl.BlockSpec((1, 1, D_MODEL), lambda b, j: (b, 0, 0))],
        out_specs=pl.BlockSpec((1, A1_TL, D_MODEL), t3),
        out_shape=jax.ShapeDtypeStruct((BATCH, SEQ, D_MODEL), F32),
        compiler_params=_params(("parallel", "arbitrary")), name="even_out_proj",
    )(att, ys, u, d, gw, gb, woa, wos, x, g1)


MOE_TL = 512
SLOT_PAD = 8


def _slot_columns(cols, lane):
    out = jnp.zeros(lane.shape, cols[0].dtype)
    for k, col in enumerate(cols):
        out = jnp.where(lane == k, col, out)
    return out[:, :SLOT_PAD]


def _pack_rows(v):
    halves = []
    for p in range(2):
        base = 2 * p * ROW_WORDS
        a = pltpu.bitcast(v[:, base:base + ROW_WORDS].astype(BF16).astype(F32), jnp.uint32)
        b = pltpu.bitcast(v[:, base + ROW_WORDS:base + 2 * ROW_WORDS].astype(BF16).astype(F32), jnp.uint32)
        halves.append((a >> 16) | b)
    return halves


def _unpack_rows(lo, hi):
    out = []
    for w in (lo, hi):
        out.append(pltpu.bitcast(w << 16, F32))
        out.append(pltpu.bitcast(w & jnp.uint32(0xFFFF0000), F32))
    return out


def _moe_in_kernel(x_ref, sh_ref, sc_ref, rw_ref, rb_ref, tri_ref,
                   hlo_ref, hhi_ref, idx_ref, wt_ref, rank_ref, cnt_ref, run_sc):
    @pl.when((pl.program_id(0) == 0) & (pl.program_id(1) == 0))
    def _():
        run_sc[...] = jnp.zeros_like(run_sc)

    h = _norm_mod(x_ref[0], sh_ref[0], sc_ref[0])
    hb = h.astype(BF16)
    h_lo = (h - hb.astype(F32)).astype(BF16)
    rw = rw_ref[...]
    rw_hi = rw.astype(BF16)
    rw_lo = (rw - rw_hi.astype(F32)).astype(BF16)
    logits = jnp.dot(jnp.concatenate([hb, hb, h_lo], axis=1), jnp.concatenate([rw_hi, rw_lo, rw_hi], axis=0),
                     preferred_element_type=F32)
    scores = jax.nn.sigmoid(logits)
    hlo_ref[0], hhi_ref[0] = _pack_rows(h)

    work = scores + rb_ref[...]
    lane = lax.broadcasted_iota(jnp.int32, work.shape, 1)
    hits, ids = [], []
    for _ in range(TOP_K):
        m = jnp.max(work, axis=-1, keepdims=True)
        ik = jnp.min(jnp.where(work == m, lane, N_EXPERTS), axis=-1, keepdims=True)
        hit = lane == ik
        hits.append(hit)
        ids.append(ik)
        work = jnp.where(hit, -jnp.inf, work)
    mask = hits[0]
    for hit in hits[1:]:
        mask = jnp.logical_or(mask, hit)
    maskf = mask.astype(F32)
    before = jnp.dot(tri_ref[...], maskf.astype(BF16), preferred_element_type=F32) + run_sc[...]
    sel = [jnp.sum(jnp.where(hit, scores, 0.0), axis=-1, keepdims=True) for hit in hits]
    denom = sel[0]
    for s in sel[1:]:
        denom = denom + s
    ranks = [jnp.sum(jnp.where(hit, before, 0.0), axis=-1, keepdims=True) for hit in hits]
    lane128 = lax.broadcasted_iota(jnp.int32, (MOE_TL, 128), 1)
    idx_ref[0] = _slot_columns(ids, lane128)
    wt_ref[0] = _slot_columns([s / denom * ROUTE_SCALE for s in sel], lane128)
    rank_ref[0] = _slot_columns([r.astype(jnp.int32) for r in ranks], lane128)
    run_sc[...] += jnp.sum(maskf, axis=0, keepdims=True)
    cnt_ref[...] = run_sc[...]


def _moe_in_call(x, sh, sc, rw, rb):
    t3 = lambda b, j: (b, j, 0)
    full2 = lambda b, j: (0, 0)
    per_b = lambda b, j: (b, 0, 0)
    tri = jnp.asarray(np.tril(np.ones((MOE_TL, MOE_TL), np.float32), -1), BF16)
    slot = jax.ShapeDtypeStruct((BATCH, SEQ, SLOT_PAD), jnp.int32)
    return pl.pallas_call(
        _moe_in_kernel, grid=(BATCH, SEQ // MOE_TL),
        in_specs=[pl.BlockSpec((1, MOE_TL, D_MODEL), t3),
                  pl.BlockSpec((1, 1, D_MODEL), per_b),
                  pl.BlockSpec((1, 1, D_MODEL), per_b),
                  pl.BlockSpec((D_MODEL, N_EXPERTS), full2),
                  pl.BlockSpec((1, N_EXPERTS), full2),
                  pl.BlockSpec((MOE_TL, MOE_TL), full2)],
        out_specs=[pl.BlockSpec((1, MOE_TL, ROW_WORDS), t3),
                   pl.BlockSpec((1, MOE_TL, ROW_WORDS), t3),
                   pl.BlockSpec((1, MOE_TL, SLOT_PAD), t3),
                   pl.BlockSpec((1, MOE_TL, SLOT_PAD), t3),
                   pl.BlockSpec((1, MOE_TL, SLOT_PAD), t3),
                   pl.BlockSpec((1, N_EXPERTS), full2)],
        out_shape=[jax.ShapeDtypeStruct((BATCH, SEQ, ROW_WORDS), jnp.uint32),
                   jax.ShapeDtypeStruct((BATCH, SEQ, ROW_WORDS), jnp.uint32),
                   slot,
                   jax.ShapeDtypeStruct((BATCH, SEQ, SLOT_PAD), F32),
                   slot,
                   jax.ShapeDtypeStruct((1, N_EXPERTS), F32)],
        scratch_shapes=[pltpu.VMEM((1, N_EXPERTS), F32)],
        compiler_params=_params(("arbitrary", "arbitrary")), name="moe_router",
    )(x, sh, sc, rw, rb, tri)


def _sc_mesh():
    return plsc.VectorSubcoreMesh(core_axis_name="c", subcore_axis_name="s")


def _sc_dispatch(h_words, dest, n_rows):
    n_tok = h_words.shape[0]

    @pl.kernel(out_type=jax.ShapeDtypeStruct((n_rows, ROW_WORDS), jnp.uint32), mesh=_sc_mesh(), scratch_types=[])
    def scatter_rows(h_hbm, i_hbm, o_hbm):
        def body(h_vmem, i_vmem):
            for k in range(TOP_K):
                pltpu.sync_copy(h_vmem, o_hbm.at[i_vmem.at[k]])

        pltpu.emit_pipeline(
            body, grid=(n_tok // SC_WINDOW,),
            in_specs=[pl.BlockSpec((SC_WINDOW, ROW_WORDS), index_map=lambda i: (i, 0)),
                      pl.BlockSpec((SLOT_PAD, SC_WINDOW), index_map=lambda i: (0, i))],
            out_specs=[],
            core_axis_name=("c", "s"), dimension_semantics=(pltpu.PARALLEL,),
        )(h_hbm, i_hbm)

    return scatter_rows(h_words, dest)


def _sc_collect(y_words, dest):
    n_tok = dest.shape[1]

    @pl.kernel(out_type=jax.ShapeDtypeStruct((TOP_K, n_tok, ROW_WORDS), jnp.uint32), mesh=_sc_mesh(),
               scratch_types=[])
    def gather_rows(y_hbm, i_hbm, o_hbm):
        def body(i_vmem, o_vmem):
            pltpu.sync_copy(y_hbm.at[i_vmem.at[0]], o_vmem.at[0])

        pltpu.emit_pipeline(
            body, grid=(TOP_K, n_tok // SC_WINDOW),
            in_specs=[pl.BlockSpec((1, SC_WINDOW), index_map=lambda k, i: (k, i))],
            out_specs=[pl.BlockSpec((1, SC_WINDOW, ROW_WORDS), index_map=lambda k, i: (k, i, 0))],
            core_axis_name=("c", "s"), dimension_semantics=(pltpu.PARALLEL, pltpu.PARALLEL),
        )(i_hbm, o_hbm)

    return gather_rows(y_words, dest)


def _expert_kernel(be_ref, nv_ref, xlo_ref, xhi_ref, wg_ref, wu_ref, wd_ref, ylo_ref, yhi_ref,
                   wg_sc, wu_sc, wd_sc):
    i = pl.program_id(0)
    nv = nv_ref[i]

    @pl.when(jnp.logical_or(i == 0, be_ref[i] != be_ref[jnp.maximum(i - 1, 0)]))
    def _():
        wg_sc[...] = wg_ref[0, 0].astype(BF16)
        wu_sc[...] = wu_ref[0, 0].astype(BF16)
        wd_sc[...] = wd_ref[0, 0].astype(BF16)

    @pl.when(nv > 0)
    def _():
        parts = _unpack_rows(xlo_ref[...], xhi_ref[...])
        xb = jnp.concatenate([p.astype(BF16) for p in parts], axis=1)
        live = lax.broadcasted_iota(jnp.int32, xb.shape, 0) < nv
        xb = jnp.where(live, xb, jnp.zeros_like(xb))
        hid = jax.nn.silu(jnp.dot(xb, wg_sc[...], preferred_element_type=F32))
        hid = hid * jnp.dot(xb, wu_sc[...], preferred_element_type=F32)
        y = jnp.dot(hid.astype(BF16), wd_sc[...], preferred_element_type=F32)
        ylo_ref[...], yhi_ref[...] = _pack_rows(y)

    @pl.when(nv == 0)
    def _():
        ylo_ref[...] = jnp.zeros_like(ylo_ref)
        yhi_ref[...] = jnp.zeros_like(yhi_ref)


def _expert_call(block_e, n_valid, xlo, xhi, wg, wu, wd, li):
    n_rows = xlo.shape[0]
    n_blocks = n_rows // MOE_BLOCK
    rows = pl.BlockSpec((MOE_BLOCK, ROW_WORDS), lambda i, be, nv: (i, 0))
    grid_spec = pltpu.PrefetchScalarGridSpec(
        num_scalar_prefetch=2, grid=(n_blocks,),
        in_specs=[rows, rows,
                  pl.BlockSpec((1, 1, D_MODEL, EXPERT_FF), lambda i, be, nv: (li, be[i], 0, 0)),
                  pl.BlockSpec((1, 1, D_MODEL, EXPERT_FF), lambda i, be, nv: (li, be[i], 0, 0)),
                  pl.BlockSpec((1, 1, EXPERT_FF, D_MODEL), lambda i, be, nv: (li, be[i], 0, 0))],
        out_specs=[rows, rows],
        scratch_shapes=[pltpu.VMEM((D_MODEL, EXPERT_FF), BF16), pltpu.VMEM((D_MODEL, EXPERT_FF), BF16),
                        pltpu.VMEM((EXPERT_FF, D_MODEL), BF16)])
    out = jax.ShapeDtypeStruct((n_rows, ROW_WORDS), jnp.uint32)
    return pl.pallas_call(
        _expert_kernel, grid_spec=grid_spec, out_shape=[out, out],
        compiler_params=_params(("arbitrary",)), name="moe_experts",
    )(block_e, n_valid, xlo, xhi, wg, wu, wd)


def _combine_kernel(ylo_ref, yhi_ref, w_ref, x_ref, sh_ref, sc_ref, g_ref, sg_ref, su_ref, sd_ref, o_ref):
    hb = _norm_mod(x_ref[0], sh_ref[0], sc_ref[0]).astype(BF16)
    hid = jax.nn.silu(jnp.dot(hb, sg_ref[...].astype(BF16), preferred_element_type=F32))
    hid = hid * jnp.dot(hb, su_ref[...].astype(BF16), preferred_element_type=F32)
    shared = jnp.dot(hid.astype(BF16), sd_ref[...].astype(BF16), preferred_element_type=F32)
    w = w_ref[0]
    acc = [None] * 4
    for k in range(TOP_K):
        wk = w[:, k:k + 1]
        for c, part in enumerate(_unpack_rows(ylo_ref[k], yhi_ref[k])):
            acc[c] = wk * part if acc[c] is None else acc[c] + wk * part
    for c in range(4):
        sl = slice(c * ROW_WORDS, (c + 1) * ROW_WORDS)
        o_ref[0, :, sl] = x_ref[0, :, sl] + g_ref[0, :, sl] * (acc[c] + shared[:, sl])


def _combine_call(ylo, yhi, wts, x, sh, sc, g2, sg, su, sd):
    t3 = lambda b, j: (b, j, 0)
    per_b = lambda b, j: (b, 0, 0)
    full2 = lambda b, j: (0, 0)
    ff = sg.shape[1]
    nt = SEQ // MOE_TL
    rows = pl.BlockSpec((TOP_K, MOE_TL, ROW_WORDS), lambda b, j: (0, b * nt + j, 0))
    return pl.pallas_call(
        _combine_kernel, grid=(BATCH, nt),
        in_specs=[rows, rows,
                  pl.BlockSpec((1, MOE_TL, SLOT_PAD), t3),
                  pl.BlockSpec((1, MOE_TL, D_MODEL), t3),
                  pl.BlockSpec((1, 1, D_MODEL), per_b),
                  pl.BlockSpec((1, 1, D_MODEL), per_b),
                  pl.BlockSpec((1, 1, D_MODEL), per_b),
                  pl.BlockSpec((D_MODEL, ff), full2),
                  pl.BlockSpec((D_MODEL, ff), full2),
                  pl.BlockSpec((ff, D_MODEL), full2)],
        out_specs=pl.BlockSpec((1, MOE_TL, D_MODEL), t3),
        out_shape=jax.ShapeDtypeStruct((BATCH, SEQ, D_MODEL), F32),
        compiler_params=_params(("parallel", "arbitrary")), name="moe_combine_shared",
    )(ylo, yhi, wts, x, sh, sc, g2, sg, su, sd)


def _moe(x, sh, sc, g2, router_w, router_b, w_gate, w_up, w_down, sh_gate, sh_up, sh_down, li):
    T = BATCH * SEQ
    TK = T * TOP_K
    hlo, hhi, idx, wts, rank, counts = _moe_in_call(x, sh, sc, router_w, router_b[None, :])
    counts = counts[0].astype(jnp.int32)
    padded = (counts + MOE_BLOCK - 1) // MOE_BLOCK * MOE_BLOCK
    pad_end = jnp.cumsum(padded)
    pad_start = pad_end - padded
    n_blocks = -(-TK // MOE_BLOCK) + N_EXPERTS
    n_rows = n_blocks * MOE_BLOCK
    block_start = jnp.arange(n_blocks, dtype=jnp.int32) * MOE_BLOCK
    owns = jnp.logical_and(block_start[:, None] >= pad_start[None, :], block_start[:, None] < pad_end[None, :])
    owns = owns.astype(jnp.int32)
    experts = jnp.arange(N_EXPERTS, dtype=jnp.int32)[None, :]
    block_e = jnp.sum(owns * experts, axis=1) + (N_EXPERTS - 1) * (1 - jnp.sum(owns, axis=1))
    n_valid = jnp.sum(owns * (counts[None, :] - (block_start[:, None] - pad_start[None, :])), axis=1)
    n_valid = jnp.clip(n_valid, 0, MOE_BLOCK).astype(jnp.int32)
    dest = (pad_start[idx.reshape(T, SLOT_PAD)] + rank.reshape(T, SLOT_PAD)).T.astype(jnp.int32)
    xlo = _sc_dispatch(hlo.reshape(T, ROW_WORDS), dest, n_rows)
    xhi = _sc_dispatch(hhi.reshape(T, ROW_WORDS), dest, n_rows)
    ylo, yhi = _expert_call(block_e, n_valid, xlo, xhi, w_gate, w_up, w_down, li)
    return _combine_call(_sc_collect(ylo, dest), _sc_collect(yhi, dest), wts, x, sh, sc, g2,
                         sh_gate, sh_up, sh_down)


HY_TL = 512
HALO = 8


def _hy_in_kernel(x_ref, xp_ref, xn_ref, sh_ref, sc_ref, w_ref, cw_ref, cb_ref, z_ref, x0_ref, h_sc):
    j = pl.program_id(1)
    shift, scale = sh_ref[0], sc_ref[0]
    keep_prev = (j > 0).astype(F32)
    keep_next = (j < SEQ // HY_TL - 1).astype(F32)
    h_sc[0:HALO, :] = _norm_mod(xp_ref[0], shift, scale) * keep_prev
    h_sc[HALO:HALO + HY_TL, :] = _norm_mod(x_ref[0], shift, scale)
    h_sc[HALO + HY_TL:, :] = _norm_mod(xn_ref[0], shift, scale) * keep_next
    hcat = h_sc[...].astype(BF16)
    outs = []
    for part in range(3):
        sl = slice(part * HY_WIDTH, (part + 1) * HY_WIDTH)
        p = jnp.dot(hcat, w_ref[:, sl], preferred_element_type=F32)
        o = (p[HALO - 1:HALO - 1 + HY_TL] * cw_ref[0:1, sl] + p[HALO:HALO + HY_TL] * cw_ref[1:2, sl]
             + p[HALO + 1:HALO + 1 + HY_TL] * cw_ref[2:3, sl] + cb_ref[:, sl])
        outs.append(o)
    x0_ref[0] = outs[0].astype(BF16)
    z_ref[0] = (outs[2] * outs[1]).astype(BF16)


def _hy_in_call(x, sh, sc, w, cw, cb):
    nb8 = HY_TL // HALO
    t3 = lambda b, j: (b, j, 0)
    full2 = lambda b, j: (0, 0)
    per_b = lambda b, j: (b, 0, 0)
    return pl.pallas_call(
        _hy_in_kernel, grid=(BATCH, SEQ // HY_TL),
        in_specs=[pl.BlockSpec((1, HY_TL, D_MODEL), t3),
                  pl.BlockSpec((1, HALO, D_MODEL), lambda b, j: (b, jnp.maximum(j * nb8 - 1, 0), 0)),
                  pl.BlockSpec((1, HALO, D_MODEL), lambda b, j: (b, jnp.minimum((j + 1) * nb8, SEQ // HALO - 1), 0)),
                  pl.BlockSpec((1, 1, D_MODEL), per_b),
                  pl.BlockSpec((1, 1, D_MODEL), per_b),
                  pl.BlockSpec((D_MODEL, 3 * HY_WIDTH), full2),
                  pl.BlockSpec((SHORT_CONV, 3 * HY_WIDTH), full2),
                  pl.BlockSpec((1, 3 * HY_WIDTH), full2)],
        out_specs=[pl.BlockSpec((1, HY_TL, HY_WIDTH), t3), pl.BlockSpec((1, HY_TL, HY_WIDTH), t3)],
        out_shape=[jax.ShapeDtypeStruct((BATCH, SEQ, HY_WIDTH), BF16),
                   jax.ShapeDtypeStruct((BATCH, SEQ, HY_WIDTH), BF16)],
        scratch_shapes=[pltpu.VMEM((HY_TL + 2 * HALO, D_MODEL), F32)],
        compiler_params=_params(("parallel", "arbitrary")), name="hyena_in_proj",
    )(x, x, x, sh, sc, w, cw, cb)


def _fft_tables():
    c = np.arange(FFT_N2, dtype=np.int64)
    ang = 2.0 * np.pi * ((c[:, None] * c[None, :]) % FFT_N2) / FFT_N2
    sr, si = np.cos(ang), -np.sin(ang)
    m = np.block([[sr, -si], [si, sr]])
    k1 = np.arange(FFT_NK, dtype=np.int64)
    ang_t = 2.0 * np.pi * (k1[:, None] * c[None, :]) / DFT_N
    lanes = np.ones((1, 1, 128))
    tr = np.cos(ang_t)[:, :, None] * lanes
    ti = -np.sin(ang_t)[:, :, None] * lanes
    return jnp.asarray(m, BF16), jnp.asarray(tr, F32), jnp.asarray(ti, F32)


def _lin(acc, coef, val):
    if abs(coef) < 1e-12:
        return acc
    term = val if coef == 1.0 else (-val if coef == -1.0 else coef * val)
    return term if acc is None else acc + term


def _twiddle(tr_ref, ti_ref, k1, width):
    reps = width // 128
    tr, ti = tr_ref[k1], ti_ref[k1]
    return jnp.concatenate([tr] * reps, axis=1), jnp.concatenate([ti] * reps, axis=1)


def _class_spectrum(block, k1, m_ref, tr_ref, ti_ref, width):
    yr = yi = None
    for a in range(FFT_NA):
        th = 2.0 * math.pi * ((a * k1) % FFT_N1) / FFT_N1
        za = block(a)
        yr = _lin(yr, round(math.cos(th), 15), za)
        yi = _lin(yi, round(-math.sin(th), 15), za)
    if k1 > 0:
        tr, ti = _twiddle(tr_ref, ti_ref, k1, width)
        yr, yi = (yr * tr, yr * ti) if yi is None else (yr * tr - yi * ti, yr * ti + yi * tr)
    if yi is None:
        x = jnp.dot(m_ref[:, :FFT_N2], yr.astype(BF16), preferred_element_type=F32)
    else:
        x = jnp.dot(m_ref[...], jnp.concatenate([yr, yi], axis=0).astype(BF16), preferred_element_type=F32)
    return x[:FFT_N2], x[FFT_N2:]


def _class_inverse(yr, yi, k1, m_ref, tr_ref, ti_ref, acc_ref, width):
    v = jnp.dot(m_ref[...], jnp.concatenate([yr, -yi], axis=0).astype(BF16), preferred_element_type=F32)
    ur, ui = v[:FFT_N2], -v[FFT_N2:]
    if k1 > 0:
        tr, ti = _twiddle(tr_ref, ti_ref, k1, width)
        ur, ui = ur * tr + ui * ti, ui * tr - ur * ti
    scale = (1.0 if k1 in (0, FFT_N1 // 2) else 2.0) / DFT_N
    for a in range(FFT_NA):
        th = 2.0 * math.pi * ((a * k1) % FFT_N1) / FFT_N1
        term = _lin(None, round(math.cos(th), 15) * scale, ur)
        term = _lin(term, round(-math.sin(th), 15) * scale, ui)
        rows = slice(a * FFT_N2, (a + 1) * FFT_N2)
        if k1 == 0:
            acc_ref[rows, :] = term
        else:
            acc_ref[rows, :] += term


def _spec_kernel(hf_ref, hb_ref, m_ref, tr_ref, ti_ref, c_ref):
    for k1 in range(FFT_NK):
        fr, fi = _class_spectrum(lambda a: hf_ref[a * FFT_N2:(a + 1) * FFT_N2, :], k1, m_ref, tr_ref, ti_ref, HY_CT)
        br, bi = _class_spectrum(lambda a: hb_ref[a * FFT_N2:(a + 1) * FFT_N2, :], k1, m_ref, tr_ref, ti_ref, HY_CT)
        c_ref[k1, :FFT_N2, :] = (fr + br).astype(BF16)
        c_ref[k1, FFT_N2:, :] = (fi - bi).astype(BF16)


def _fft_table_specs(ngrid):
    z = (0,) * 2
    z3 = (0,) * 3
    if ngrid == 1:
        return [pl.BlockSpec((2 * FFT_N2, 2 * FFT_N2), lambda c: z),
                pl.BlockSpec((FFT_NK, FFT_N2, 128), lambda c: z3),
                pl.BlockSpec((FFT_NK, FFT_N2, 128), lambda c: z3)]
    return [pl.BlockSpec((2 * FFT_N2, 2 * FFT_N2), lambda b, c: z),
            pl.BlockSpec((FFT_NK, FFT_N2, 128), lambda b, c: z3),
            pl.BlockSpec((FFT_NK, FFT_N2, 128), lambda b, c: z3)]


def _spec_call(hfb, m, tr, ti):
    nct = HY_WIDTH // HY_CT
    return pl.pallas_call(
        _spec_kernel, grid=(nct,),
        in_specs=[pl.BlockSpec((SEQ, HY_CT), lambda c: (0, c)),
                  pl.BlockSpec((SEQ, HY_CT), lambda c: (0, c + nct))] + _fft_table_specs(1),
        out_specs=pl.BlockSpec((FFT_NK, 2 * FFT_N2, HY_CT), lambda c: (0, 0, c)),
        out_shape=jax.ShapeDtypeStruct((FFT_NK, 2 * FFT_N2, HY_WIDTH), BF16),
        compiler_params=_params(("arbitrary",)), name="hyena_filter_spectrum",
    )(hfb, hfb, m, tr, ti)


def _conv_kernel(z_ref, c_ref, m_ref, tr_ref, ti_ref, y_ref, acc):
    for k1 in range(FFT_NK):
        xr, xi = _class_spectrum(lambda a: z_ref[0, a * FFT_N2:(a + 1) * FFT_N2, :].astype(F32), k1,
                                 m_ref, tr_ref, ti_ref, HY_CT)
        cr = c_ref[k1, :FFT_N2, :].astype(F32)
        ci = c_ref[k1, FFT_N2:, :].astype(F32)
        _class_inverse(xr * cr - xi * ci, xr * ci + xi * cr, k1, m_ref, tr_ref, ti_ref, acc, HY_CT)
    y_ref[0] = acc[...].astype(BF16)


def _conv_call(z, spec, m, tr, ti):
    return pl.pallas_call(
        _conv_kernel, grid=(HY_WIDTH // HY_CT, BATCH),
        in_specs=[pl.BlockSpec((1, SEQ, HY_CT), lambda c, b: (b, 0, c)),
                  pl.BlockSpec((FFT_NK, 2 * FFT_N2, HY_CT), lambda c, b: (0, 0, c))] + _fft_table_specs(2),
        out_specs=pl.BlockSpec((1, SEQ, HY_CT), lambda c, b: (b, 0, c)),
        out_shape=jax.ShapeDtypeStruct((BATCH, SEQ, HY_WIDTH), BF16),
        scratch_shapes=[pltpu.VMEM((SEQ, HY_CT), F32)],
        compiler_params=_params(("parallel", "arbitrary")), name="hyena_long_conv",
    )(z, spec, m, tr, ti)


def _hy_out_kernel(y_ref, z_ref, x0_ref, b_ref, w_ref, x_ref, g_ref, o_ref):
    z = z_ref[0].astype(F32)
    gated = x0_ref[0].astype(F32) * (y_ref[0].astype(F32) + b_ref[...] * z)
    mix = jnp.dot(gated.astype(BF16), w_ref[...], preferred_element_type=F32)
    o_ref[0] = x_ref[0] + g_ref[0] * mix


def _hy_out_call(y, z, x0, bias, w, x, g1):
    t3 = lambda b, j: (b, j, 0)
    full2 = lambda b, j: (0, 0)
    return pl.pallas_call(
        _hy_out_kernel, grid=(BATCH, SEQ // HY_TL),
        in_specs=[pl.BlockSpec((1, HY_TL, HY_WIDTH), t3),
                  pl.BlockSpec((1, HY_TL, HY_WIDTH), t3),
                  pl.BlockSpec((1, HY_TL, HY_WIDTH), t3),
                  pl.BlockSpec((1, HY_WIDTH), full2),
                  pl.BlockSpec((HY_WIDTH, D_MODEL), full2),
                  pl.BlockSpec((1, HY_TL, D_MODEL), t3),
                  pl.BlockSpec((1, 1, D_MODEL), lambda b, j: (b, 0, 0))],
        out_specs=pl.BlockSpec((1, HY_TL, D_MODEL), t3),
        out_shape=jax.ShapeDtypeStruct((BATCH, SEQ, D_MODEL), F32),
        compiler_params=_params(("parallel", "arbitrary")), name="hyena_out_proj",
    )(y, z, x0, bias, w, x, g1)


def _hyena_filter(w1, b1, w2, b2, w3, freq):
    hi = lax.Precision.HIGHEST
    Lq = SEQ
    t = jnp.linspace(0.0, 1.0, Lq, dtype=F32)[:, None]
    ang = 2.0 * math.pi * jnp.arange(Lq, dtype=F32)[:, None] / Lq
    bands = jnp.linspace(1e-4, FILT_BANDS - 1, FILT_BANDS, dtype=F32)
    z = jnp.concatenate([t, jnp.cos(bands * ang), -jnp.sin(bands * ang)], axis=-1)
    hid = jnp.sin(freq * (jnp.dot(z, w1, precision=hi) + b1))
    hid = jnp.sin(freq * (jnp.dot(hid, w2, precision=hi) + b2))
    deltas = jnp.linspace(HY_MIN_DECAY, HY_MAX_DECAY, HY_WIDTH, dtype=F32)
    hf = jnp.dot(hid, w3, precision=hi) * jnp.exp(-t * jnp.tile(deltas, 2))
    ssq = jnp.sum(hf * hf, axis=0)
    ssq = ssq[:HY_WIDTH] + ssq[HY_WIDTH:]
    return hf * jnp.tile(lax.rsqrt(ssq + EPS), 2)


def kernel(x, c, ctx, c_ctx, ada_w, ada_b, ev_w_in, mla_q_norm, mla_w_uq, mla_kv_norm, mla_w_ukv, mla_q_qknorm, mla_k_qknorm, s5_lam_re, s5_lam_im, s5_log_step, s5_b_re, s5_b_im, s5_c_re, s5_c_im, s5_d, s5_glu_w, s5_glu_b, ev_w_out, hy_w_in, hy_conv_w, hy_conv_b, hy_f_w1, hy_f_b1, hy_f_w2, hy_f_b2, hy_f_w3, hy_f_freq, hy_bias, hy_w_out, moe_router_w, moe_router_b, moe_w_gate, moe_w_up, moe_w_down, moe_sh_gate, moe_sh_up, moe_sh_down):
    hi = lax.Precision.HIGHEST
    D = D_MODEL
    sc = jax.nn.silu(c)
    sc_ctx = jax.nn.silu(c_ctx)

    def mods(li):
        mod = jnp.dot(sc, ada_w[li], precision=hi) + ada_b[li]
        return [m[:, None, :] for m in jnp.split(mod, 6, axis=-1)]

    sh1, sc1, g1, sh2, sc2, g2 = mods(0)
    mod_ctx = jnp.dot(sc_ctx, ada_w[0][:, :2 * D], precision=hi) + ada_b[0][:2 * D]
    w0 = _a0_weights(ev_w_in[0], mla_q_norm[0], mla_w_uq[0], mla_kv_norm[0], mla_w_ukv[0],
                     mla_q_qknorm[0], mla_k_qknorm[0])
    q, k, v, u = _a0_call(x, ctx, sh1, sc1, mod_ctx[None, :D], mod_ctx[None, D:], w0)
    att = _attn_call(q, k, v)
    ug = u.reshape(BATCH, S5_NCHUNK, S5_CHUNK, S5_GROUPS, S5_GROUP)
    ug = ug.transpose(3, 1, 0, 2, 4).reshape(S5_GROUPS, S5_ROWS, S5_COLS)
    ys = _s5_call(ug, *_s5_weights(s5_lam_re[0], s5_lam_im[0], s5_log_step[0], s5_b_re[0], s5_b_im[0],
                                   s5_c_re[0], s5_c_im[0]))
    ys = ys.reshape(S5_GROUPS, S5_NCHUNK_LAT, BATCH, S5_CHUNK, S5_GROUP)
    ys = ys.transpose(2, 1, 3, 0, 4).reshape(BATCH, SEQ, S5_WIDTH)
    wo = ev_w_out[0].astype(BF16)
    wo_att = jnp.concatenate([wo[:MLA_WIDTH].reshape(MLA_HEADS, V_HEAD, D_MODEL),
                              jnp.zeros((MLA_HEADS, HEAD_PAD - V_HEAD, D_MODEL), BF16)], axis=1)
    x = _a1_call(att, ys, u, s5_d[0][None, :], s5_glu_w[0].astype(BF16), s5_glu_b[0][None, :],
                 wo_att.reshape(MLA_HEADS * HEAD_PAD, D_MODEL), wo[MLA_WIDTH:], x, g1)
    x = _moe(x, sh2, sc2, g2, moe_router_w[0], moe_router_b[0], moe_w_gate, moe_w_up, moe_w_down,
             moe_sh_gate[0], moe_sh_up[0], moe_sh_down[0], 0)

    sh1, sc1, g1, sh2, sc2, g2 = mods(1)
    z, x0 = _hy_in_call(x, sh1, sc1, hy_w_in[0].astype(BF16), hy_conv_w[0], hy_conv_b[0][None, :])
    fft_tabs = _fft_tables()
    hfb = _hyena_filter(hy_f_w1[0], hy_f_b1[0], hy_f_w2[0], hy_f_b2[0], hy_f_w3[0], hy_f_freq[0])
    y = _conv_call(z, _spec_call(hfb, *fft_tabs), *fft_tabs)
    x = _hy_out_call(y, z, x0, hy_bias[0][None, :], hy_w_out[0].astype(BF16), x, g1)
    x = _moe(x, sh2, sc2, g2, moe_router_w[1], moe_router_b[1], moe_w_gate, moe_w_up, moe_w_down,
             moe_sh_gate[1], moe_sh_up[1], moe_sh_down[1], 1)
    return x
```

```python
import functools
import math

import numpy as np
import jax
import jax.numpy as jnp
from jax import lax
from jax.experimental import pallas as pl
from jax.experimental.pallas import tpu as pltpu
from jax.experimental.pallas import tpu_sc as plsc

F32 = jnp.float32
BF16 = jnp.bfloat16

D_MODEL = 1024
BATCH = 8
SEQ = 4096
CTX_LEN = 256
KV_LEN = SEQ + CTX_LEN
GRID_W = 64
EPS = 1e-6

MLA_HEADS = 8
QK_NOPE = 64
QK_ROPE = 32
QK_HEAD = QK_NOPE + QK_ROPE
V_HEAD = 64
Q_LORA = 256
KV_LORA = 128
MLA_WIDTH = MLA_HEADS * V_HEAD
ROPE_BASE = 10000.0
HEAD_PAD = 128

S5_WIDTH = 512
S5_GROUP = 16
S5_GROUPS = S5_WIDTH // S5_GROUP
S5_STATE = 64
S5_CHUNK = 32
S5_NCHUNK = KV_LEN // S5_CHUNK
S5_NCHUNK_LAT = SEQ // S5_CHUNK
S5_NCHUNK_CTX = CTX_LEN // S5_CHUNK

HY_WIDTH = D_MODEL
FILT_EMB = 33
FILT_BANDS = (FILT_EMB - 1) // 2
SHORT_CONV = 3
HY_MIN_DECAY = -math.log(1e-2) / 1.5
HY_MAX_DECAY = -math.log(1e-2) / 0.3
DFT_N = 2 * SEQ
FFT_N1 = 16
FFT_N2 = DFT_N // FFT_N1
FFT_NA = FFT_N1 // 2
FFT_NK = FFT_N1 // 2 + 1
HY_CT = 256

N_EXPERTS = 64
TOP_K = 6
EXPERT_FF = 256
ROUTE_SCALE = 2.5
MOE_BLOCK = 1024
ROW_WORDS = D_MODEL // 4
SC_WINDOW = 128

V7X_VMEM_BYTES = 64 * 1024 * 1024
VMEM_LIMIT = V7X_VMEM_BYTES - 8 * 1024 * 1024


def _params(semantics):
    return pltpu.CompilerParams(dimension_semantics=semantics, vmem_limit_bytes=VMEM_LIMIT)


def _norm_mod(x, shift, scale):
    ms = jnp.mean(x * x, axis=-1, keepdims=True)
    return x * lax.rsqrt(ms + EPS) * (1.0 + scale) + shift


def _rms(x, gain, n):
    ms = jnp.sum(x * x, axis=-1, keepdims=True) * (1.0 / n)
    return x * lax.rsqrt(ms + EPS) * gain


A0_TL = 256
A0_NT = SEQ // A0_TL


def _rope_perm():
    return np.concatenate([np.arange(0, QK_ROPE, 2), np.arange(1, QK_ROPE, 2)])


def _rope_tables():
    t = np.arange(SEQ)
    row = (t // GRID_W).astype(np.float64)
    col = (t % GRID_W).astype(np.float64)
    n_freq = QK_ROPE // 4
    inv = ROPE_BASE ** (-np.arange(n_freq, dtype=np.float64) / n_freq)
    ang = np.concatenate([row[:, None] * inv, col[:, None] * inv], axis=-1)
    cos, sin = np.cos(ang), np.sin(ang)
    half = QK_ROPE // 2
    a = np.zeros((KV_LEN, HEAD_PAD))
    b = np.zeros((KV_LEN, HEAD_PAD))
    a[:, :QK_HEAD] = 1.0
    a[:SEQ, QK_NOPE:QK_NOPE + half] = cos
    a[:SEQ, QK_NOPE + half:QK_HEAD] = cos
    b[:SEQ, QK_NOPE:QK_NOPE + half] = -sin
    b[:SEQ, QK_NOPE + half:QK_HEAD] = sin
    return a, b


def _norm_rope_heads(f, gain_ref, a, b, out_ref):
    width = MLA_HEADS * HEAD_PAD
    for hd in range(MLA_HEADS):
        sl = slice(hd * HEAD_PAD, (hd + 1) * HEAD_PAD)
        x = f[:, sl]
        r = lax.rsqrt(jnp.sum(x * x, axis=-1, keepdims=True) * (1.0 / QK_HEAD) + EPS)
        rot = x * (a * gain_ref[:, sl]) + f[:, width + hd * HEAD_PAD:width + (hd + 1) * HEAD_PAD] * b
        out_ref[0, :, sl] = (rot * r).astype(BF16)


def _a0_kernel(x_ref, ctx_ref, sh_ref, sc_ref, shc_ref, scc_ref, win_ref, qn_ref, wuq_ref, kvn_ref,
               wk_ref, wuv_ref, qg_ref, kg_ref, ka_ref, kb_ref, qa_ref, qb_ref,
               q_ref, k_ref, v_ref, u_ref):
    j = pl.program_id(1)
    is_ctx = j == A0_NT
    xin = jnp.where(is_ctx, ctx_ref[0], x_ref[0])
    shift = jnp.where(is_ctx, shc_ref[...], sh_ref[0])
    scale = jnp.where(is_ctx, scc_ref[...], sc_ref[0])
    h = _norm_mod(xin, shift, scale).astype(BF16)
    proj = jnp.dot(h, win_ref[...], preferred_element_type=F32)
    u_ref[0] = proj[:, 512:].astype(BF16)

    c_kv = _rms(proj[:, Q_LORA:Q_LORA + KV_LORA], kvn_ref[...], KV_LORA).astype(BF16)
    lane = lax.broadcasted_iota(jnp.int32, (1, MLA_HEADS * HEAD_PAD), 1)
    ones_lane = (lane % HEAD_PAD == V_HEAD).astype(F32)
    v_ref[0] = (jnp.dot(c_kv, wuv_ref[...], preferred_element_type=F32) + ones_lane).astype(BF16)
    kin = jnp.concatenate([c_kv, proj[:, 384:512].astype(BF16)], axis=1)
    kf = jnp.dot(kin, wk_ref[...], preferred_element_type=F32)
    _norm_rope_heads(kf, kg_ref, ka_ref[...], kb_ref[...], k_ref)

    @pl.when(j < A0_NT)
    def _():
        ql = _rms(proj[:, :Q_LORA], qn_ref[...], Q_LORA).astype(BF16)
        qf = jnp.dot(ql, wuq_ref[...], preferred_element_type=F32)
        _norm_rope_heads(qf, qg_ref, qa_ref[...], qb_ref[...], q_ref)


def _a0_call(x, ctx, sh, sc, shc, scc, w):
    nt = A0_NT
    lat = lambda b, j: (b, jnp.minimum(j, nt - 1), 0)
    full2 = lambda b, j: (0, 0)
    per_b = lambda b, j: (b, 0, 0)
    tab = pl.BlockSpec((A0_TL, HEAD_PAD), lambda b, j: (j, 0))
    in_specs = [
        pl.BlockSpec((1, A0_TL, D_MODEL), lat),
        pl.BlockSpec((1, CTX_LEN, D_MODEL), per_b),
        pl.BlockSpec((1, 1, D_MODEL), per_b),
        pl.BlockSpec((1, 1, D_MODEL), per_b),
        pl.BlockSpec((1, D_MODEL), full2),
        pl.BlockSpec((1, D_MODEL), full2),
        pl.BlockSpec((D_MODEL, 1024), full2),
        pl.BlockSpec((1, Q_LORA), full2),
        pl.BlockSpec((Q_LORA, 2 * MLA_HEADS * HEAD_PAD), full2),
        pl.BlockSpec((1, KV_LORA), full2),
        pl.BlockSpec((2 * KV_LORA, 2 * MLA_HEADS * HEAD_PAD), full2),
        pl.BlockSpec((KV_LORA, MLA_HEADS * HEAD_PAD), full2),
        pl.BlockSpec((1, MLA_HEADS * HEAD_PAD), full2),
        pl.BlockSpec((1, MLA_HEADS * HEAD_PAD), full2),
        tab, tab, tab, tab,
    ]
    out_specs = [
        pl.BlockSpec((1, A0_TL, MLA_HEADS * HEAD_PAD), lat),
        pl.BlockSpec((1, A0_TL, MLA_HEADS * HEAD_PAD), lambda b, j: (b, j, 0)),
        pl.BlockSpec((1, A0_TL, MLA_HEADS * HEAD_PAD), lambda b, j: (b, j, 0)),
        pl.BlockSpec((1, A0_TL, S5_WIDTH), lambda b, j: (b, j, 0)),
    ]
    out_shape = [
        jax.ShapeDtypeStruct((BATCH, SEQ, MLA_HEADS * HEAD_PAD), BF16),
        jax.ShapeDtypeStruct((BATCH, KV_LEN, MLA_HEADS * HEAD_PAD), BF16),
        jax.ShapeDtypeStruct((BATCH, KV_LEN, MLA_HEADS * HEAD_PAD), BF16),
        jax.ShapeDtypeStruct((BATCH, KV_LEN, S5_WIDTH), BF16),
    ]
    return pl.pallas_call(
        _a0_kernel, grid=(BATCH, nt + 1), in_specs=in_specs, out_specs=out_specs, out_shape=out_shape,
        compiler_params=_params(("parallel", "arbitrary")), name="even_in_proj",
    )(x, ctx, sh, sc, shc, scc, *w)


def _gain_swap(w, gain):
    half = QK_ROPE // 2
    wg = (w * gain).reshape(w.shape[0], MLA_HEADS, HEAD_PAD)
    re, im = wg[..., QK_NOPE:QK_NOPE + half], wg[..., QK_NOPE + half:QK_HEAD]
    out = jnp.concatenate([jnp.zeros_like(wg[..., :QK_NOPE]), im, re, jnp.zeros_like(wg[..., QK_HEAD:])], axis=-1)
    return out.reshape(w.shape)


def _a0_weights(w_in, q_norm, w_uq, kv_norm, w_ukv, q_qk, k_qk):
    perm = _rope_perm()
    kr0 = Q_LORA + KV_LORA
    w_cat = jnp.concatenate([
        w_in[:, :kr0], w_in[:, kr0:kr0 + QK_ROPE][:, perm],
        jnp.zeros((D_MODEL, HEAD_PAD - QK_ROPE), F32), w_in[:, kr0 + QK_ROPE:]], axis=1).astype(BF16)
    pad = HEAD_PAD - QK_HEAD

    def head_gain(g):
        gh = jnp.concatenate([g[:QK_NOPE], g[QK_NOPE:][perm], jnp.zeros((pad,), F32)])
        return jnp.tile(gh, MLA_HEADS)[None, :]

    uq = w_uq.reshape(Q_LORA, MLA_HEADS, QK_HEAD)
    uq = jnp.concatenate([uq[..., :QK_NOPE], uq[..., QK_NOPE:][..., perm],
                          jnp.zeros((Q_LORA, MLA_HEADS, pad), F32)], axis=-1)
    uq = uq.reshape(Q_LORA, MLA_HEADS * HEAD_PAD)
    uq = jnp.concatenate([uq, _gain_swap(uq, head_gain(q_qk))], axis=1).astype(BF16)
    ukv = w_ukv.reshape(KV_LORA, MLA_HEADS, QK_NOPE + V_HEAD)
    uk = jnp.concatenate([ukv[..., :QK_NOPE], jnp.zeros((KV_LORA, MLA_HEADS, HEAD_PAD - QK_NOPE), F32)], axis=-1)
    uk = uk.reshape(KV_LORA, MLA_HEADS * HEAD_PAD)
    place = np.zeros((KV_LORA, MLA_HEADS, HEAD_PAD), np.float32)
    for i in range(QK_ROPE):
        place[i, :, QK_NOPE + i] = 1.0
    wk = jnp.concatenate([uk, jnp.asarray(place.reshape(KV_LORA, MLA_HEADS * HEAD_PAD))], axis=0)
    wk = jnp.concatenate([wk, _gain_swap(wk, head_gain(k_qk))], axis=1).astype(BF16)
    wuv = jnp.concatenate([ukv[..., QK_NOPE:], jnp.zeros((KV_LORA, MLA_HEADS, HEAD_PAD - V_HEAD), F32)], axis=-1)
    wuv = wuv.reshape(KV_LORA, MLA_HEADS * HEAD_PAD).astype(BF16)
    a, b = _rope_tables()
    qs = QK_HEAD ** -0.5 * math.log2(math.e)
    tabs = [jnp.asarray(t, F32) for t in (a, b, a * qs, b * qs)]
    return [w_cat, q_norm[None, :], uq, kv_norm[None, :], wk, wuv, head_gain(q_qk), head_gain(k_qk)] + tabs


ATT_TQ = 256
HEADS_PER_STEP = 4


def _attn_kernel(q_ref, k_ref, v_ref, o_ref):
    for hh in range(HEADS_PER_STEP):
        sl = slice(hh * HEAD_PAD, (hh + 1) * HEAD_PAD)
        s = lax.dot_general(q_ref[0, :, sl], k_ref[0, :, sl], (((1,), (1,)), ((), ())),
                            preferred_element_type=F32)
        m = jnp.max(s, axis=-1, keepdims=True)
        p = jnp.exp2(s - m).astype(BF16)
        acc = jnp.dot(p, v_ref[0, :, sl], preferred_element_type=F32)
        o_ref[0, :, sl] = (acc * (1.0 / acc[:, V_HEAD:V_HEAD + 1])).astype(BF16)


def _attn_call(q, k, v):
    wq = HEADS_PER_STEP * HEAD_PAD
    return pl.pallas_call(
        _attn_kernel, grid=(BATCH, MLA_HEADS // HEADS_PER_STEP, SEQ // ATT_TQ),
        in_specs=[pl.BlockSpec((1, ATT_TQ, wq), lambda b, h, i: (b, i, h)),
                  pl.BlockSpec((1, KV_LEN, wq), lambda b, h, i: (b, 0, h)),
                  pl.BlockSpec((1, KV_LEN, wq), lambda b, h, i: (b, 0, h))],
        out_specs=pl.BlockSpec((1, ATT_TQ, wq), lambda b, h, i: (b, i, h)),
        out_shape=jax.ShapeDtypeStruct((BATCH, SEQ, MLA_HEADS * HEAD_PAD), BF16),
        compiler_params=_params(("parallel", "parallel", "arbitrary")), name="mla_attention",
    )(q, k, v)


S5_ROWS = S5_NCHUNK * BATCH
S5_ROWS_LAT = S5_NCHUNK_LAT * BATCH
S5_COLS = S5_CHUNK * S5_GROUP
S5_SW = 2 * S5_STATE


def _s5_kernel(u_ref, r_ref, mb_ref, mc_ref, coef_ref, y_ref, x_sc, sp_sc, t_sc):
    u = u_ref[0]
    lags = r_ref[0]
    for sg in range(S5_CHUNK):
        off = (S5_CHUNK - 1 - sg) * S5_GROUP
        t_sc[sg * S5_GROUP:(sg + 1) * S5_GROUP, :] = lags[:, off:off + S5_COLS].astype(BF16)
    x_sc[...] = jnp.dot(u, mb_ref[0], preferred_element_type=F32)
    cf = coef_ref[0]
    af, bfm, bfp, ab, bbm, bbp = [cf[i * 8:(i + 1) * 8] for i in range(6)]

    def body(i, carry):
        sf, sfw, sb, sbw = carry
        cfw = jnp.where(i < S5_NCHUNK_CTX, i + S5_NCHUNK_LAT, i - S5_NCHUNK_CTX)
        rf = pl.multiple_of(cfw * BATCH, BATCH)
        rb = pl.multiple_of((S5_NCHUNK - 1 - i) * BATCH, BATCH)
        sp_sc[pl.ds(rf, BATCH), 0:S5_SW] = sf
        sp_sc[pl.ds(rb, BATCH), S5_SW:2 * S5_SW] = sb
        xf = x_sc[pl.ds(rf, BATCH), 0:S5_SW]
        xfw = x_sc[pl.ds(rf, BATCH), S5_SW:2 * S5_SW]
        xb = x_sc[pl.ds(rb, BATCH), 2 * S5_SW:3 * S5_SW]
        xbw = x_sc[pl.ds(rb, BATCH), 3 * S5_SW:4 * S5_SW]
        return (sf * af + sfw * bfm + xf, sfw * af + sf * bfp + xfw,
                sb * ab + sbw * bbm + xb, sbw * ab + sb * bbp + xbw)

    z = jnp.zeros((BATCH, S5_SW), F32)
    lax.fori_loop(0, S5_NCHUNK, body, (z, z, z, z))
    y = jnp.dot(u[:S5_ROWS_LAT], t_sc[...], preferred_element_type=F32)
    y = y + jnp.dot(sp_sc[0:S5_ROWS_LAT, :].astype(BF16), mc_ref[0], preferred_element_type=F32)
    y_ref[0] = y.astype(BF16)


def _s5_call(ug, t, mb, mc, coef):
    g3 = lambda g: (g, 0, 0)
    return pl.pallas_call(
        _s5_kernel, grid=(S5_GROUPS,),
        in_specs=[pl.BlockSpec((1, S5_ROWS, S5_COLS), g3),
                  pl.BlockSpec((1, S5_GROUP, 2 * S5_COLS), g3),
                  pl.BlockSpec((1, S5_COLS, 4 * S5_SW), g3),
                  pl.BlockSpec((1, 2 * S5_SW, S5_COLS), g3),
                  pl.BlockSpec((1, 6 * 8, S5_SW), g3)],
        out_specs=pl.BlockSpec((1, S5_ROWS_LAT, S5_COLS), g3),
        out_shape=jax.ShapeDtypeStruct((S5_GROUPS, S5_ROWS_LAT, S5_COLS), BF16),
        scratch_shapes=[pltpu.VMEM((S5_ROWS, 4 * S5_SW), F32), pltpu.VMEM((S5_ROWS, 2 * S5_SW), F32),
                        pltpu.VMEM((S5_COLS, S5_COLS), BF16)],
        compiler_params=_params(("parallel",)), name="s5_chunked_scan",
    )(ug, t, mb, mc, coef)


def _s5_weights(lam_re, lam_im, log_step, b_re, b_im, c_re, c_im):
    q = S5_CHUNK
    hi = lax.Precision.HIGHEST
    t_blocks, mbs, mcs, coefs = [], [], [], []
    sig = jnp.arange(q)
    for d in range(2):
        lr = jnp.minimum(lam_re[d], -1e-4)
        li = lam_im[d]
        step = jnp.exp(log_step[d])[:, None]
        jj = jnp.arange(q + 1, dtype=F32)[:, None, None]
        mag = jnp.exp(lr * step * jj)
        ph = li * step * jj
        pr, pi = mag * jnp.cos(ph), mag * jnp.sin(ph)
        nr, ni = pr[1] - 1.0, pi[1]
        den = lr * lr + li * li
        fr, fi = (nr * lr + ni * li) / den, (ni * lr - nr * li) / den
        br = fr[..., None] * b_re[d] - fi[..., None] * b_im[d]
        bi = fr[..., None] * b_im[d] + fi[..., None] * b_re[d]
        cr, ci = c_re[d], c_im[d]
        cpr = cr[None] * pr[:, :, None, :] - ci[None] * pi[:, :, None, :]
        cpi = cr[None] * pi[:, :, None, :] + ci[None] * pr[:, :, None, :]
        kern = (jnp.einsum('jghp,gpk->jghk', cpr[:q], br, precision=hi)
                - jnp.einsum('jghp,gpk->jghk', cpi[:q], bi, precision=hi))
        kt = kern.transpose(1, 3, 0, 2)
        zero_slots = jnp.zeros((S5_GROUPS, S5_GROUP, q, S5_GROUP), F32)
        if d == 0:
            t_blocks.append(jnp.concatenate([zero_slots[:, :, :q - 1], kt, zero_slots[:, :, :1]], axis=2))
        else:
            t_blocks.append(jnp.concatenate([kt[:, :, ::-1], zero_slots], axis=2))
        pw = (q - 1 - sig) if d == 0 else sig
        xr = pr[pw][..., None] * br[None] - pi[pw][..., None] * bi[None]
        xi = pr[pw][..., None] * bi[None] + pi[pw][..., None] * br[None]
        xr = xr.transpose(1, 0, 3, 2).reshape(S5_GROUPS, S5_COLS, S5_STATE)
        xi = xi.transpose(1, 0, 3, 2).reshape(S5_GROUPS, S5_COLS, S5_STATE)
        mbs += [xr, xi, xi, xr]
        po = (sig + 1) if d == 0 else (q - sig)
        mr = cpr[po].transpose(1, 3, 0, 2).reshape(S5_GROUPS, S5_STATE, S5_COLS)
        mi = -cpi[po].transpose(1, 3, 0, 2).reshape(S5_GROUPS, S5_STATE, S5_COLS)
        mcs += [mr, mi]
        are, aim = pr[q], pi[q]
        rows = [jnp.concatenate([are, are], -1), jnp.concatenate([-aim, aim], -1), jnp.concatenate([aim, -aim], -1)]
        coefs += [jnp.broadcast_to(r[:, None, :], (S5_GROUPS, 8, S5_SW)) for r in rows]
    t = (t_blocks[0] + t_blocks[1]).reshape(S5_GROUPS, S5_GROUP, 2 * S5_COLS)
    mb = jnp.concatenate(mbs, axis=-1).astype(BF16)
    mc = jnp.concatenate(mcs, axis=1).astype(BF16)
    coef = jnp.concatenate(coefs, axis=1)
    return t, mb, mc, coef


A1_TL = 512


def _a1_kernel(att_ref, ys_ref, u_ref, d_ref, gw_ref, gb_ref, woa_ref, wos_ref, x_ref, g_ref, o_ref):
    y = u_ref[0].astype(F32) * d_ref[...] + ys_ref[0].astype(F32)
    z = jax.nn.gelu(y)
    gate = jax.nn.sigmoid(jnp.dot(z.astype(BF16), gw_ref[...], preferred_element_type=F32) + gb_ref[...])
    s5 = (z * gate).astype(BF16)
    mix = jnp.dot(att_ref[0], woa_ref[...], preferred_element_type=F32)
    mix = mix + jnp.dot(s5, wos_ref[...], preferred_element_type=F32)
    o_ref[0] = x_ref[0] + g_ref[0] * mix


def _a1_call(att, ys, u, d, gw, gb, woa, wos, x, g1):
    t3 = lambda b, j: (b, j, 0)
    full2 = lambda b, j: (0, 0)
    return pl.pallas_call(
        _a1_kernel, grid=(BATCH, SEQ // A1_TL),
        in_specs=[pl.BlockSpec((1, A1_TL, MLA_HEADS * HEAD_PAD), t3),
                  pl.BlockSpec((1, A1_TL, S5_WIDTH), t3),
                  pl.BlockSpec((1, A1_TL, S5_WIDTH), t3),
                  pl.BlockSpec((1, S5_WIDTH), full2),
                  pl.BlockSpec((S5_WIDTH, S5_WIDTH), full2),
                  pl.BlockSpec((1, S5_WIDTH), full2),
                  pl.BlockSpec((MLA_HEADS * HEAD_PAD, D_MODEL), full2),
                  pl.BlockSpec((S5_WIDTH, D_MODEL), full2),
                  pl.BlockSpec((1, A1_TL, D_MODEL), t3),
                  pl.BlockSpec((1, 1, D_MODEL), lambda b, j: (b, 0, 0))],
        out_specs=pl.BlockSpec((1, A1_TL, D_MODEL), t3),
        out_shape=jax.ShapeDtypeStruct((BATCH, SEQ, D_MODEL), F32),
        compiler_params=_params(("parallel", "arbitrary")), name="even_out_proj",
    )(att, ys, u, d, gw, gb, woa, wos, x, g1)


MOE_TL = 512
SLOT_PAD = 8


def _slot_columns(cols, lane):
    out = jnp.zeros(lane.shape, cols[0].dtype)
    for k, col in enumerate(cols):
        out = jnp.where(lane == k, col, out)
    return out[:, :SLOT_PAD]


def _pack_rows(v):
    halves = []
    for p in range(2):
        base = 2 * p * ROW_WORDS
        a = pltpu.bitcast(v[:, base:base + ROW_WORDS].astype(BF16).astype(F32), jnp.uint32)
        b = pltpu.bitcast(v[:, base + ROW_WORDS:base + 2 * ROW_WORDS].astype(BF16).astype(F32), jnp.uint32)
        halves.append((a >> 16) | b)
    return halves


def _unpack_rows(lo, hi):
    out = []
    for w in (lo, hi):
        out.append(pltpu.bitcast(w << 16, F32))
        out.append(pltpu.bitcast(w & jnp.uint32(0xFFFF0000), F32))
    return out


def _moe_in_kernel(x_ref, sh_ref, sc_ref, rw_ref, rb_ref, tri_ref,
                   hlo_ref, hhi_ref, idx_ref, wt_ref, rank_ref, cnt_ref, run_sc):
    @pl.when((pl.program_id(0) == 0) & (pl.program_id(1) == 0))
    def _():
        run_sc[...] = jnp.zeros_like(run_sc)

    h = _norm_mod(x_ref[0], sh_ref[0], sc_ref[0])
    hb = h.astype(BF16)
    h_lo = (h - hb.astype(F32)).astype(BF16)
    rw = rw_ref[...]
    rw_hi = rw.astype(BF16)
    rw_lo = (rw - rw_hi.astype(F32)).astype(BF16)
    logits = jnp.dot(jnp.concatenate([hb, hb, h_lo], axis=1), jnp.concatenate([rw_hi, rw_lo, rw_hi], axis=0),
                     preferred_element_type=F32)
    scores = jax.nn.sigmoid(logits)
    hlo_ref[0], hhi_ref[0] = _pack_rows(h)

    work = scores + rb_ref[...]
    lane = lax.broadcasted_iota(jnp.int32, work.shape, 1)
    hits, ids = [], []
    for _ in range(TOP_K):
        m = jnp.max(work, axis=-1, keepdims=True)
        ik = jnp.min(jnp.where(work == m, lane, N_EXPERTS), axis=-1, keepdims=True)
        hit = lane == ik
        hits.append(hit)
        ids.append(ik)
        work = jnp.where(hit, -jnp.inf, work)
    mask = hits[0]
    for hit in hits[1:]:
        mask = jnp.logical_or(mask, hit)
    maskf = mask.astype(F32)
    before = jnp.dot(tri_ref[...], maskf.astype(BF16), preferred_element_type=F32) + run_sc[...]
    sel = [jnp.sum(jnp.where(hit, scores, 0.0), axis=-1, keepdims=True) for hit in hits]
    denom = sel[0]
    for s in sel[1:]:
        denom = denom + s
    ranks = [jnp.sum(jnp.where(hit, before, 0.0), axis=-1, keepdims=True) for hit in hits]
    lane128 = lax.broadcasted_iota(jnp.int32, (MOE_TL, 128), 1)
    idx_ref[0] = _slot_columns(ids, lane128)
    wt_ref[0] = _slot_columns([s / denom * ROUTE_SCALE for s in sel], lane128)
    rank_ref[0] = _slot_columns([r.astype(jnp.int32) for r in ranks], lane128)
    run_sc[...] += jnp.sum(maskf, axis=0, keepdims=True)
    cnt_ref[...] = run_sc[...]


def _moe_in_call(x, sh, sc, rw, rb):
    t3 = lambda b, j: (b, j, 0)
    full2 = lambda b, j: (0, 0)
    per_b = lambda b, j: (b, 0, 0)
    tri = jnp.asarray(np.tril(np.ones((MOE_TL, MOE_TL), np.float32), -1), BF16)
    slot = jax.ShapeDtypeStruct((BATCH, SEQ, SLOT_PAD), jnp.int32)
    return pl.pallas_call(
        _moe_in_kernel, grid=(BATCH, SEQ // MOE_TL),
        in_specs=[pl.BlockSpec((1, MOE_TL, D_MODEL), t3),
                  pl.BlockSpec((1, 1, D_MODEL), per_b),
                  pl.BlockSpec((1, 1, D_MODEL), per_b),
                  pl.BlockSpec((D_MODEL, N_EXPERTS), full2),
                  pl.BlockSpec((1, N_EXPERTS), full2),
                  pl.BlockSpec((MOE_TL, MOE_TL), full2)],
        out_specs=[pl.BlockSpec((1, MOE_TL, ROW_WORDS), t3),
                   pl.BlockSpec((1, MOE_TL, ROW_WORDS), t3),
                   pl.BlockSpec((1, MOE_TL, SLOT_PAD), t3),
                   pl.BlockSpec((1, MOE_TL, SLOT_PAD), t3),
                   pl.BlockSpec((1, MOE_TL, SLOT_PAD), t3),
                   pl.BlockSpec((1, N_EXPERTS), full2)],
        out_shape=[jax.ShapeDtypeStruct((BATCH, SEQ, ROW_WORDS), jnp.uint32),
                   jax.ShapeDtypeStruct((BATCH, SEQ, ROW_WORDS), jnp.uint32),
                   slot,
                   jax.ShapeDtypeStruct((BATCH, SEQ, SLOT_PAD), F32),
                   slot,
                   jax.ShapeDtypeStruct((1, N_EXPERTS), F32)],
        scratch_shapes=[pltpu.VMEM((1, N_EXPERTS), F32)],
        compiler_params=_params(("arbitrary", "arbitrary")), name="moe_router",
    )(x, sh, sc, rw, rb, tri)


def _sc_mesh():
    return plsc.VectorSubcoreMesh(core_axis_name="c", subcore_axis_name="s")


def _sc_dispatch(h_words, dest, n_rows):
    n_tok = h_words.shape[0]

    @pl.kernel(out_type=jax.ShapeDtypeStruct((n_rows, ROW_WORDS), jnp.uint32), mesh=_sc_mesh(), scratch_types=[])
    def scatter_rows(h_hbm, i_hbm, o_hbm):
        def body(h_vmem, i_vmem):
            for k in range(TOP_K):
                pltpu.sync_copy(h_vmem, o_hbm.at[i_vmem.at[k]])

        pltpu.emit_pipeline(
            body, grid=(n_tok // SC_WINDOW,),
            in_specs=[pl.BlockSpec((SC_WINDOW, ROW_WORDS), index_map=lambda i: (i, 0)),
                      pl.BlockSpec((SLOT_PAD, SC_WINDOW), index_map=lambda i: (0, i))],
            out_specs=[],
            core_axis_name=("c", "s"), dimension_semantics=(pltpu.PARALLEL,),
        )(h_hbm, i_hbm)

    return scatter_rows(h_words, dest)


def _sc_collect(y_words, dest):
    n_tok = dest.shape[1]

    @pl.kernel(out_type=jax.ShapeDtypeStruct((TOP_K, n_tok, ROW_WORDS), jnp.uint32), mesh=_sc_mesh(),
               scratch_types=[])
    def gather_rows(y_hbm, i_hbm, o_hbm):
        def body(i_vmem, o_vmem):
            pltpu.sync_copy(y_hbm.at[i_vmem.at[0]], o_vmem.at[0])

        pltpu.emit_pipeline(
            body, grid=(TOP_K, n_tok // SC_WINDOW),
            in_specs=[pl.BlockSpec((1, SC_WINDOW), index_map=lambda k, i: (k, i))],
            out_specs=[pl.BlockSpec((1, SC_WINDOW, ROW_WORDS), index_map=lambda k, i: (k, i, 0))],
            core_axis_name=("c", "s"), dimension_semantics=(pltpu.PARALLEL, pltpu.PARALLEL),
        )(i_hbm, o_hbm)

    return gather_rows(y_words, dest)


def _expert_kernel(be_ref, nv_ref, xlo_ref, xhi_ref, wg_ref, wu_ref, wd_ref, ylo_ref, yhi_ref,
                   wg_sc, wu_sc, wd_sc):
    i = pl.program_id(0)
    nv = nv_ref[i]

    @pl.when(jnp.logical_or(i == 0, be_ref[i] != be_ref[jnp.maximum(i - 1, 0)]))
    def _():
        wg_sc[...] = wg_ref[0, 0].astype(BF16)
        wu_sc[...] = wu_ref[0, 0].astype(BF16)
        wd_sc[...] = wd_ref[0, 0].astype(BF16)

    @pl.when(nv > 0)
    def _():
        parts = _unpack_rows(xlo_ref[...], xhi_ref[...])
        xb = jnp.concatenate([p.astype(BF16) for p in parts], axis=1)
        live = lax.broadcasted_iota(jnp.int32, xb.shape, 0) < nv
        xb = jnp.where(live, xb, jnp.zeros_like(xb))
        hid = jax.nn.silu(jnp.dot(xb, wg_sc[...], preferred_element_type=F32))
        hid = hid * jnp.dot(xb, wu_sc[...], preferred_element_type=F32)
        y = jnp.dot(hid.astype(BF16), wd_sc[...], preferred_element_type=F32)
        ylo_ref[...], yhi_ref[...] = _pack_rows(y)

    @pl.when(nv == 0)
    def _():
        ylo_ref[...] = jnp.zeros_like(ylo_ref)
        yhi_ref[...] = jnp.zeros_like(yhi_ref)


def _expert_call(block_e, n_valid, xlo, xhi, wg, wu, wd, li):
    n_rows = xlo.shape[0]
    n_blocks = n_rows // MOE_BLOCK
    rows = pl.BlockSpec((MOE_BLOCK, ROW_WORDS), lambda i, be, nv: (i, 0))
    grid_spec = pltpu.PrefetchScalarGridSpec(
        num_scalar_prefetch=2, grid=(n_blocks,),
        in_specs=[rows, rows,
                  pl.BlockSpec((1, 1, D_MODEL, EXPERT_FF), lambda i, be, nv: (li, be[i], 0, 0)),
                  pl.BlockSpec((1, 1, D_MODEL, EXPERT_FF), lambda i, be, nv: (li, be[i], 0, 0)),
                  pl.BlockSpec((1, 1, EXPERT_FF, D_MODEL), lambda i, be, nv: (li, be[i], 0, 0))],
        out_specs=[rows, rows],
        scratch_shapes=[pltpu.VMEM((D_MODEL, EXPERT_FF), BF16), pltpu.VMEM((D_MODEL, EXPERT_FF), BF16),
                        pltpu.VMEM((EXPERT_FF, D_MODEL), BF16)])
    out = jax.ShapeDtypeStruct((n_rows, ROW_WORDS), jnp.uint32)
    return pl.pallas_call(
        _expert_kernel, grid_spec=grid_spec, out_shape=[out, out],
        compiler_params=_params(("arbitrary",)), name="moe_experts",
    )(block_e, n_valid, xlo, xhi, wg, wu, wd)


def _combine_kernel(ylo_ref, yhi_ref, w_ref, x_ref, sh_ref, sc_ref, g_ref, sg_ref, su_ref, sd_ref, o_ref):
    hb = _norm_mod(x_ref[0], sh_ref[0], sc_ref[0]).astype(BF16)
    hid = jax.nn.silu(jnp.dot(hb, sg_ref[...].astype(BF16), preferred_element_type=F32))
    hid = hid * jnp.dot(hb, su_ref[...].astype(BF16), preferred_element_type=F32)
    shared = jnp.dot(hid.astype(BF16), sd_ref[...].astype(BF16), preferred_element_type=F32)
    w = w_ref[0]
    acc = [None] * 4
    for k in range(TOP_K):
        wk = w[:, k:k + 1]
        for c, part in enumerate(_unpack_rows(ylo_ref[k], yhi_ref[k])):
            acc[c] = wk * part if acc[c] is None else acc[c] + wk * part
    for c in range(4):
        sl = slice(c * ROW_WORDS, (c + 1) * ROW_WORDS)
        o_ref[0, :, sl] = x_ref[0, :, sl] + g_ref[0, :, sl] * (acc[c] + shared[:, sl])


def _combine_call(ylo, yhi, wts, x, sh, sc, g2, sg, su, sd):
    t3 = lambda b, j: (b, j, 0)
    per_b = lambda b, j: (b, 0, 0)
    full2 = lambda b, j: (0, 0)
    ff = sg.shape[1]
    nt = SEQ // MOE_TL
    rows = pl.BlockSpec((TOP_K, MOE_TL, ROW_WORDS), lambda b, j: (0, b * nt + j, 0))
    return pl.pallas_call(
        _combine_kernel, grid=(BATCH, nt),
        in_specs=[rows, rows,
                  pl.BlockSpec((1, MOE_TL, SLOT_PAD), t3),
                  pl.BlockSpec((1, MOE_TL, D_MODEL), t3),
                  pl.BlockSpec((1, 1, D_MODEL), per_b),
                  pl.BlockSpec((1, 1, D_MODEL), per_b),
                  pl.BlockSpec((1, 1, D_MODEL), per_b),
                  pl.BlockSpec((D_MODEL, ff), full2),
                  pl.BlockSpec((D_MODEL, ff), full2),
                  pl.BlockSpec((ff, D_MODEL), full2)],
        out_specs=pl.BlockSpec((1, MOE_TL, D_MODEL), t3),
        out_shape=jax.ShapeDtypeStruct((BATCH, SEQ, D_MODEL), F32),
        compiler_params=_params(("parallel", "arbitrary")), name="moe_combine_shared",
    )(ylo, yhi, wts, x, sh, sc, g2, sg, su, sd)


def _moe(x, sh, sc, g2, router_w, router_b, w_gate, w_up, w_down, sh_gate, sh_up, sh_down, li):
    T = BATCH * SEQ
    TK = T * TOP_K
    hlo, hhi, idx, wts, rank, counts = _moe_in_call(x, sh, sc, router_w, router_b[None, :])
    counts = counts[0].astype(jnp.int32)
    padded = (counts + MOE_BLOCK - 1) // MOE_BLOCK * MOE_BLOCK
    pad_end = jnp.cumsum(padded)
    pad_start = pad_end - padded
    n_blocks = -(-TK // MOE_BLOCK) + N_EXPERTS
    n_rows = n_blocks * MOE_BLOCK
    block_start = jnp.arange(n_blocks, dtype=jnp.int32) * MOE_BLOCK
    owns = jnp.logical_and(block_start[:, None] >= pad_start[None, :], block_start[:, None] < pad_end[None, :])
    owns = owns.astype(jnp.int32)
    experts = jnp.arange(N_EXPERTS, dtype=jnp.int32)[None, :]
    block_e = jnp.sum(owns * experts, axis=1) + (N_EXPERTS - 1) * (1 - jnp.sum(owns, axis=1))
    n_valid = jnp.sum(owns * (counts[None, :] - (block_start[:, None] - pad_start[None, :])), axis=1)
    n_valid = jnp.clip(n_valid, 0, MOE_BLOCK).astype(jnp.int32)
    dest = (pad_start[idx.reshape(T, SLOT_PAD)] + rank.reshape(T, SLOT_PAD)).T.astype(jnp.int32)
    xlo = _sc_dispatch(hlo.reshape(T, ROW_WORDS), dest, n_rows)
    xhi = _sc_dispatch(hhi.reshape(T, ROW_WORDS), dest, n_rows)
    ylo, yhi = _expert_call(block_e, n_valid, xlo, xhi, w_gate, w_up, w_down, li)
    return _combine_call(_sc_collect(ylo, dest), _sc_collect(yhi, dest), wts, x, sh, sc, g2,
                         sh_gate, sh_up, sh_down)


HY_TL = 512
HALO = 8


def _hy_in_kernel(x_ref, xp_ref, xn_ref, sh_ref, sc_ref, w_ref, cw_ref, cb_ref, z_ref, x0_ref, h_sc):
    j = pl.program_id(1)
    shift, scale = sh_ref[0], sc_ref[0]
    keep_prev = (j > 0).astype(F32)
    keep_next = (j < SEQ // HY_TL - 1).astype(F32)
    h_sc[0:HALO, :] = _norm_mod(xp_ref[0], shift, scale) * keep_prev
    h_sc[HALO:HALO + HY_TL, :] = _norm_mod(x_ref[0], shift, scale)
    h_sc[HALO + HY_TL:, :] = _norm_mod(xn_ref[0], shift, scale) * keep_next
    hcat = h_sc[...].astype(BF16)
    outs = []
    for part in range(3):
        sl = slice(part * HY_WIDTH, (part + 1) * HY_WIDTH)
        p = jnp.dot(hcat, w_ref[:, sl], preferred_element_type=F32)
        o = (p[HALO - 1:HALO - 1 + HY_TL] * cw_ref[0:1, sl] + p[HALO:HALO + HY_TL] * cw_ref[1:2, sl]
             + p[HALO + 1:HALO + 1 + HY_TL] * cw_ref[2:3, sl] + cb_ref[:, sl])
        outs.append(o)
    x0_ref[0] = outs[0].astype(BF16)
    z_ref[0] = (outs[2] * outs[1]).astype(BF16)


def _hy_in_call(x, sh, sc, w, cw, cb):
    nb8 = HY_TL // HALO
    t3 = lambda b, j: (b, j, 0)
    full2 = lambda b, j: (0, 0)
    per_b = lambda b, j: (b, 0, 0)
    return pl.pallas_call(
        _hy_in_kernel, grid=(BATCH, SEQ // HY_TL),
        in_specs=[pl.BlockSpec((1, HY_TL, D_MODEL), t3),
                  pl.BlockSpec((1, HALO, D_MODEL), lambda b, j: (b, jnp.maximum(j * nb8 - 1, 0), 0)),
                  pl.BlockSpec((1, HALO, D_MODEL), lambda b, j: (b, jnp.minimum((j + 1) * nb8, SEQ // HALO - 1), 0)),
                  pl.BlockSpec((1, 1, D_MODEL), per_b),
                  pl.BlockSpec((1, 1, D_MODEL), per_b),
                  pl.BlockSpec((D_MODEL, 3 * HY_WIDTH), full2),
                  pl.BlockSpec((SHORT_CONV, 3 * HY_WIDTH), full2),
                  pl.BlockSpec((1, 3 * HY_WIDTH), full2)],
        out_specs=[pl.BlockSpec((1, HY_TL, HY_WIDTH), t3), pl.BlockSpec((1, HY_TL, HY_WIDTH), t3)],
        out_shape=[jax.ShapeDtypeStruct((BATCH, SEQ, HY_WIDTH), BF16),
                   jax.ShapeDtypeStruct((BATCH, SEQ, HY_WIDTH), BF16)],
        scratch_shapes=[pltpu.VMEM((HY_TL + 2 * HALO, D_MODEL), F32)],
        compiler_params=_params(("parallel", "arbitrary")), name="hyena_in_proj",
    )(x, x, x, sh, sc, w, cw, cb)


def _fft_tables():
    c = np.arange(FFT_N2, dtype=np.int64)
    ang = 2.0 * np.pi * ((c[:, None] * c[None, :]) % FFT_N2) / FFT_N2
    sr, si = np.cos(ang), -np.sin(ang)
    m = np.block([[sr, -si], [si, sr]])
    k1 = np.arange(FFT_NK, dtype=np.int64)
    ang_t = 2.0 * np.pi * (k1[:, None] * c[None, :]) / DFT_N
    lanes = np.ones((1, 1, 128))
    tr = np.cos(ang_t)[:, :, None] * lanes
    ti = -np.sin(ang_t)[:, :, None] * lanes
    return jnp.asarray(m, BF16), jnp.asarray(tr, F32), jnp.asarray(ti, F32)


def _lin(acc, coef, val):
    if abs(coef) < 1e-12:
        return acc
    term = val if coef == 1.0 else (-val if coef == -1.0 else coef * val)
    return term if acc is None else acc + term


def _twiddle(tr_ref, ti_ref, k1, width):
    reps = width // 128
    tr, ti = tr_ref[k1], ti_ref[k1]
    return jnp.concatenate([tr] * reps, axis=1), jnp.concatenate([ti] * reps, axis=1)


def _class_spectrum(block, k1, m_ref, tr_ref, ti_ref, width):
    yr = yi = None
    for a in range(FFT_NA):
        th = 2.0 * math.pi * ((a * k1) % FFT_N1) / FFT_N1
        za = block(a)
        yr = _lin(yr, round(math.cos(th), 15), za)
        yi = _lin(yi, round(-math.sin(th), 15), za)
    if k1 > 0:
        tr, ti = _twiddle(tr_ref, ti_ref, k1, width)
        yr, yi = (yr * tr, yr * ti) if yi is None else (yr * tr - yi * ti, yr * ti + yi * tr)
    if yi is None:
        x = jnp.dot(m_ref[:, :FFT_N2], yr.astype(BF16), preferred_element_type=F32)
    else:
        x = jnp.dot(m_ref[...], jnp.concatenate([yr, yi], axis=0).astype(BF16), preferred_element_type=F32)
    return x[:FFT_N2], x[FFT_N2:]


def _class_inverse(yr, yi, k1, m_ref, tr_ref, ti_ref, acc_ref, width):
    v = jnp.dot(m_ref[...], jnp.concatenate([yr, -yi], axis=0).astype(BF16), preferred_element_type=F32)
    ur, ui = v[:FFT_N2], -v[FFT_N2:]
    if k1 > 0:
        tr, ti = _twiddle(tr_ref, ti_ref, k1, width)
        ur, ui = ur * tr + ui * ti, ui * tr - ur * ti
    scale = (1.0 if k1 in (0, FFT_N1 // 2) else 2.0) / DFT_N
    for a in range(FFT_NA):
        th = 2.0 * math.pi * ((a * k1) % FFT_N1) / FFT_N1
        term = _lin(None, round(math.cos(th), 15) * scale, ur)
        term = _lin(term, round(-math.sin(th), 15) * scale, ui)
        rows = slice(a * FFT_N2, (a + 1) * FFT_N2)
        if k1 == 0:
            acc_ref[rows, :] = term
        else:
            acc_ref[rows, :] += term


def _spec_kernel(hf_ref, hb_ref, m_ref, tr_ref, ti_ref, c_ref):
    for k1 in range(FFT_NK):
        fr, fi = _class_spectrum(lambda a: hf_ref[a * FFT_N2:(a + 1) * FFT_N2, :], k1, m_ref, tr_ref, ti_ref, HY_CT)
        br, bi = _class_spectrum(lambda a: hb_ref[a * FFT_N2:(a + 1) * FFT_N2, :], k1, m_ref, tr_ref, ti_ref, HY_CT)
        c_ref[k1, :FFT_N2, :] = (fr + br).astype(BF16)
        c_ref[k1, FFT_N2:, :] = (fi - bi).astype(BF16)


def _fft_table_specs(ngrid):
    z = (0,) * 2
    z3 = (0,) * 3
    if ngrid == 1:
        return [pl.BlockSpec((2 * FFT_N2, 2 * FFT_N2), lambda c: z),
                pl.BlockSpec((FFT_NK, FFT_N2, 128), lambda c: z3),
                pl.BlockSpec((FFT_NK, FFT_N2, 128), lambda c: z3)]
    return [pl.BlockSpec((2 * FFT_N2, 2 * FFT_N2), lambda b, c: z),
            pl.BlockSpec((FFT_NK, FFT_N2, 128), lambda b, c: z3),
            pl.BlockSpec((FFT_NK, FFT_N2, 128), lambda b, c: z3)]


def _spec_call(hfb, m, tr, ti):
    nct = HY_WIDTH // HY_CT
    return pl.pallas_call(
        _spec_kernel, grid=(nct,),
        in_specs=[pl.BlockSpec((SEQ, HY_CT), lambda c: (0, c)),
                  pl.BlockSpec((SEQ, HY_CT), lambda c: (0, c + nct))] + _fft_table_specs(1),
        out_specs=pl.BlockSpec((FFT_NK, 2 * FFT_N2, HY_CT), lambda c: (0, 0, c)),
        out_shape=jax.ShapeDtypeStruct((FFT_NK, 2 * FFT_N2, HY_WIDTH), BF16),
        compiler_params=_params(("arbitrary",)), name="hyena_filter_spectrum",
    )(hfb, hfb, m, tr, ti)


def _conv_kernel(z_ref, c_ref, m_ref, tr_ref, ti_ref, y_ref, acc):
    for k1 in range(FFT_NK):
        xr, xi = _class_spectrum(lambda a: z_ref[0, a * FFT_N2:(a + 1) * FFT_N2, :].astype(F32), k1,
                                 m_ref, tr_ref, ti_ref, HY_CT)
        cr = c_ref[k1, :FFT_N2, :].astype(F32)
        ci = c_ref[k1, FFT_N2:, :].astype(F32)
        _class_inverse(xr * cr - xi * ci, xr * ci + xi * cr, k1, m_ref, tr_ref, ti_ref, acc, HY_CT)
    y_ref[0] = acc[...].astype(BF16)


def _conv_call(z, spec, m, tr, ti):
    return pl.pallas_call(
        _conv_kernel, grid=(HY_WIDTH // HY_CT, BATCH),
        in_specs=[pl.BlockSpec((1, SEQ, HY_CT), lambda c, b: (b, 0, c)),
                  pl.BlockSpec((FFT_NK, 2 * FFT_N2, HY_CT), lambda c, b: (0, 0, c))] + _fft_table_specs(2),
        out_specs=pl.BlockSpec((1, SEQ, HY_CT), lambda c, b: (b, 0, c)),
        out_shape=jax.ShapeDtypeStruct((BATCH, SEQ, HY_WIDTH), BF16),
        scratch_shapes=[pltpu.VMEM((SEQ, HY_CT), F32)],
        compiler_params=_params(("parallel", "arbitrary")), name="hyena_long_conv",
    )(z, spec, m, tr, ti)


def _hy_out_kernel(y_ref, z_ref, x0_ref, b_ref, w_ref, x_ref, g_ref, o_ref):
    z = z_ref[0].astype(F32)
    gated = x0_ref[0].astype(F32) * (y_ref[0].astype(F32) + b_ref[...] * z)
    mix = jnp.dot(gated.astype(BF16), w_ref[...], preferred_element_type=F32)
    o_ref[0] = x_ref[0] + g_ref[0] * mix


def _hy_out_call(y, z, x0, bias, w, x, g1):
    t3 = lambda b, j: (b, j, 0)
    full2 = lambda b, j: (0, 0)
    return pl.pallas_call(
        _hy_out_kernel, grid=(BATCH, SEQ // HY_TL),
        in_specs=[pl.BlockSpec((1, HY_TL, HY_WIDTH), t3),
                  pl.BlockSpec((1, HY_TL, HY_WIDTH), t3),
                  pl.BlockSpec((1, HY_TL, HY_WIDTH), t3),
                  pl.BlockSpec((1, HY_WIDTH), full2),
                  pl.BlockSpec((HY_WIDTH, D_MODEL), full2),
                  pl.BlockSpec((1, HY_TL, D_MODEL), t3),
                  pl.BlockSpec((1, 1, D_MODEL), lambda b, j: (b, 0, 0))],
        out_specs=pl.BlockSpec((1, HY_TL, D_MODEL), t3),
        out_shape=jax.ShapeDtypeStruct((BATCH, SEQ, D_MODEL), F32),
        compiler_params=_params(("parallel", "arbitrary")), name="hyena_out_proj",
    )(y, z, x0, bias, w, x, g1)


def _hyena_filter(w1, b1, w2, b2, w3, freq):
    hi = lax.Precision.HIGHEST
    Lq = SEQ
    t = jnp.linspace(0.0, 1.0, Lq, dtype=F32)[:, None]
    ang = 2.0 * math.pi * jnp.arange(Lq, dtype=F32)[:, None] / Lq
    bands = jnp.linspace(1e-4, FILT_BANDS - 1, FILT_BANDS, dtype=F32)
    z = jnp.concatenate([t, jnp.cos(bands * ang), -jnp.sin(bands * ang)], axis=-1)
    hid = jnp.sin(freq * (jnp.dot(z, w1, precision=hi) + b1))
    hid = jnp.sin(freq * (jnp.dot(hid, w2, precision=hi) + b2))
    deltas = jnp.linspace(HY_MIN_DECAY, HY_MAX_DECAY, HY_WIDTH, dtype=F32)
    hf = jnp.dot(hid, w3, precision=hi) * jnp.exp(-t * jnp.tile(deltas, 2))
    ssq = jnp.sum(hf * hf, axis=0)
    ssq = ssq[:HY_WIDTH] + ssq[HY_WIDTH:]
    return hf * jnp.tile(lax.rsqrt(ssq + EPS), 2)


def kernel(x, c, ctx, c_ctx, ada_w, ada_b, ev_w_in, mla_q_norm, mla_w_uq, mla_kv_norm, mla_w_ukv, mla_q_qknorm, mla_k_qknorm, s5_lam_re, s5_lam_im, s5_log_step, s5_b_re, s5_b_im, s5_c_re, s5_c_im, s5_d, s5_glu_w, s5_glu_b, ev_w_out, hy_w_in, hy_conv_w, hy_conv_b, hy_f_w1, hy_f_b1, hy_f_w2, hy_f_b2, hy_f_w3, hy_f_freq, hy_bias, hy_w_out, moe_router_w, moe_router_b, moe_w_gate, moe_w_up, moe_w_down, moe_sh_gate, moe_sh_up, moe_sh_down):
    hi = lax.Precision.HIGHEST
    D = D_MODEL
    sc = jax.nn.silu(c)
    sc_ctx = jax.nn.silu(c_ctx)

    def mods(li):
        mod = jnp.dot(sc, ada_w[li], precision=hi) + ada_b[li]
        return [m[:, None, :] for m in jnp.split(mod, 6, axis=-1)]

    sh1, sc1, g1, sh2, sc2, g2 = mods(0)
    mod_ctx = jnp.dot(sc_ctx, ada_w[0][:, :2 * D], precision=hi) + ada_b[0][:2 * D]
    w0 = _a0_weights(ev_w_in[0], mla_q_norm[0], mla_w_uq[0], mla_kv_norm[0], mla_w_ukv[0],
                     mla_q_qknorm[0], mla_k_qknorm[0])
    q, k, v, u = _a0_call(x, ctx, sh1, sc1, mod_ctx[None, :D], mod_ctx[None, D:], w0)
    att = _attn_call(q, k, v)
    ug = u.reshape(BATCH, S5_NCHUNK, S5_CHUNK, S5_GROUPS, S5_GROUP)
    ug = ug.transpose(3, 1, 0, 2, 4).reshape(S5_GROUPS, S5_ROWS, S5_COLS)
    ys = _s5_call(ug, *_s5_weights(s5_lam_re[0], s5_lam_im[0], s5_log_step[0], s5_b_re[0], s5_b_im[0],
                                   s5_c_re[0], s5_c_im[0]))
    ys = ys.reshape(S5_GROUPS, S5_NCHUNK_LAT, BATCH, S5_CHUNK, S5_GROUP)
    ys = ys.transpose(2, 1, 3, 0, 4).reshape(BATCH, SEQ, S5_WIDTH)
    wo = ev_w_out[0].astype(BF16)
    wo_att = jnp.concatenate([wo[:MLA_WIDTH].reshape(MLA_HEADS, V_HEAD, D_MODEL),
                              jnp.zeros((MLA_HEADS, HEAD_PAD - V_HEAD, D_MODEL), BF16)], axis=1)
    x = _a1_call(att, ys, u, s5_d[0][None, :], s5_glu_w[0].astype(BF16), s5_glu_b[0][None, :],
                 wo_att.reshape(MLA_HEADS * HEAD_PAD, D_MODEL), wo[MLA_WIDTH:], x, g1)
    x = _moe(x, sh2, sc2, g2, moe_router_w[0], moe_router_b[0], moe_w_gate, moe_w_up, moe_w_down,
             moe_sh_gate[0], moe_sh_up[0], moe_sh_down[0], 0)

    sh1, sc1, g1, sh2, sc2, g2 = mods(1)
    z, x0 = _hy_in_call(x, sh1, sc1, hy_w_in[0].astype(BF16), hy_conv_w[0], hy_conv_b[0][None, :])
    fft_tabs = _fft_tables()
    hfb = _hyena_filter(hy_f_w1[0], hy_f_b1[0], hy_f_w2[0], hy_f_b2[0], hy_f_w3[0], hy_f_freq[0])
    y = _conv_call(z, _spec_call(hfb, *fft_tabs), *fft_tabs)
    x = _hy_out_call(y, z, x0, hy_bias[0][None, :], hy_w_out[0].astype(BF16), x, g1)
    x = _moe(x, sh2, sc2, g2, moe_router_w[1], moe_router_b[1], moe_w_gate, moe_w_up, moe_w_down,
             moe_sh_gate[1], moe_sh_up[1], moe_sh_down[1], 1)
    return x
```

```python
import functools
import math

import numpy as np
import jax
import jax.numpy as jnp
from jax import lax
from jax.experimental import pallas as pl
from jax.experimental.pallas import tpu as pltpu
from jax.experimental.pallas import tpu_sc as plsc

F32 = jnp.float32
BF16 = jnp.bfloat16

D_MODEL = 1024
BATCH = 8
SEQ = 4096
CTX_LEN = 256
KV_LEN = SEQ + CTX_LEN
GRID_W = 64
EPS = 1e-6

MLA_HEADS = 8
QK_NOPE = 64
QK_ROPE = 32
QK_HEAD = QK_NOPE + QK_ROPE
V_HEAD = 64
Q_LORA = 256
KV_LORA = 128
MLA_WIDTH = MLA_HEADS * V_HEAD
ROPE_BASE = 10000.0
HEAD_PAD = 128

S5_WIDTH = 512
S5_GROUP = 16
S5_GROUPS = S5_WIDTH // S5_GROUP
S5_STATE = 64
S5_CHUNK = 32
S5_NCHUNK = KV_LEN // S5_CHUNK
S5_NCHUNK_LAT = SEQ // S5_CHUNK
S5_NCHUNK_CTX = CTX_LEN // S5_CHUNK

HY_WIDTH = D_MODEL
FILT_EMB = 33
FILT_BANDS = (FILT_EMB - 1) // 2
SHORT_CONV = 3
HY_MIN_DECAY = -math.log(1e-2) / 1.5
HY_MAX_DECAY = -math.log(1e-2) / 0.3
DFT_N = 2 * SEQ
FFT_N1 = 16
FFT_N2 = DFT_N // FFT_N1
FFT_NA = FFT_N1 // 2
FFT_NK = FFT_N1 // 2 + 1
HY_CT = 256

N_EXPERTS = 64
TOP_K = 6
EXPERT_FF = 256
ROUTE_SCALE = 2.5
MOE_BLOCK = 512
MOE_GROUP_B = 4
ROW_WORDS = D_MODEL // 4
SC_WINDOW = 128

V7X_VMEM_BYTES = 64 * 1024 * 1024
VMEM_LIMIT = V7X_VMEM_BYTES - 8 * 1024 * 1024


def _params(semantics):
    return pltpu.CompilerParams(dimension_semantics=semantics, vmem_limit_bytes=VMEM_LIMIT)


def _norm_mod(x, shift, scale):
    ms = jnp.mean(x * x, axis=-1, keepdims=True)
    return x * lax.rsqrt(ms + EPS) * (1.0 + scale) + shift


def _rms(x, gain, n):
    ms = jnp.sum(x * x, axis=-1, keepdims=True) * (1.0 / n)
    return x * lax.rsqrt(ms + EPS) * gain


A0_TL = 256
A0_NT = SEQ // A0_TL


def _rope_perm():
    return np.concatenate([np.arange(0, QK_ROPE, 2), np.arange(1, QK_ROPE, 2)])


def _rope_tables():
    t = np.arange(SEQ)
    row = (t // GRID_W).astype(np.float64)
    col = (t % GRID_W).astype(np.float64)
    n_freq = QK_ROPE // 4
    inv = ROPE_BASE ** (-np.arange(n_freq, dtype=np.float64) / n_freq)
    ang = np.concatenate([row[:, None] * inv, col[:, None] * inv], axis=-1)
    cos, sin = np.cos(ang), np.sin(ang)
    half = QK_ROPE // 2
    a = np.zeros((KV_LEN, HEAD_PAD))
    b = np.zeros((KV_LEN, HEAD_PAD))
    a[:, :QK_HEAD] = 1.0
    a[:SEQ, QK_NOPE:QK_NOPE + half] = cos
    a[:SEQ, QK_NOPE + half:QK_HEAD] = cos
    b[:SEQ, QK_NOPE:QK_NOPE + half] = -sin
    b[:SEQ, QK_NOPE + half:QK_HEAD] = sin
    return a, b


def _norm_rope_heads(f, gain_ref, a, b, out_ref):
    width = MLA_HEADS * HEAD_PAD
    for hd in range(MLA_HEADS):
        sl = slice(hd * HEAD_PAD, (hd + 1) * HEAD_PAD)
        x = f[:, sl]
        r = lax.rsqrt(jnp.sum(x * x, axis=-1, keepdims=True) * (1.0 / QK_HEAD) + EPS)
        rot = x * (a * gain_ref[:, sl]) + f[:, width + hd * HEAD_PAD:width + (hd + 1) * HEAD_PAD] * b
        out_ref[0, :, sl] = (rot * r).astype(BF16)


def _a0_kernel(x_ref, ctx_ref, sh_ref, sc_ref, shc_ref, scc_ref, win_ref, qn_ref, wuq_ref, kvn_ref,
               wk_ref, wuv_ref, qg_ref, kg_ref, ka_ref, kb_ref, qa_ref, qb_ref,
               q_ref, k_ref, v_ref, u_ref):
    j = pl.program_id(1)
    is_ctx = j == A0_NT
    xin = jnp.where(is_ctx, ctx_ref[0], x_ref[0])
    shift = jnp.where(is_ctx, shc_ref[...], sh_ref[0])
    scale = jnp.where(is_ctx, scc_ref[...], sc_ref[0])
    h = _norm_mod(xin, shift, scale).astype(BF16)
    proj = jnp.dot(h, win_ref[...], preferred_element_type=F32)
    u_ref[0] = proj[:, 512:].astype(BF16)

    c_kv = _rms(proj[:, Q_LORA:Q_LORA + KV_LORA], kvn_ref[...], KV_LORA).astype(BF16)
    lane = lax.broadcasted_iota(jnp.int32, (1, MLA_HEADS * HEAD_PAD), 1)
    ones_lane = (lane % HEAD_PAD == V_HEAD).astype(F32)
    v_ref[0] = (jnp.dot(c_kv, wuv_ref[...], preferred_element_type=F32) + ones_lane).astype(BF16)
    kin = jnp.concatenate([c_kv, proj[:, 384:512].astype(BF16)], axis=1)
    kf = jnp.dot(kin, wk_ref[...], preferred_element_type=F32)
    _norm_rope_heads(kf, kg_ref, ka_ref[...], kb_ref[...], k_ref)

    @pl.when(j < A0_NT)
    def _():
        ql = _rms(proj[:, :Q_LORA], qn_ref[...], Q_LORA).astype(BF16)
        qf = jnp.dot(ql, wuq_ref[...], preferred_element_type=F32)
        _norm_rope_heads(qf, qg_ref, qa_ref[...], qb_ref[...], q_ref)


def _a0_call(x, ctx, sh, sc, shc, scc, w):
    nt = A0_NT
    lat = lambda b, j: (b, jnp.minimum(j, nt - 1), 0)
    full2 = lambda b, j: (0, 0)
    per_b = lambda b, j: (b, 0, 0)
    tab = pl.BlockSpec((A0_TL, HEAD_PAD), lambda b, j: (j, 0))
    in_specs = [
        pl.BlockSpec((1, A0_TL, D_MODEL), lat),
        pl.BlockSpec((1, CTX_LEN, D_MODEL), per_b),
        pl.BlockSpec((1, 1, D_MODEL), per_b),
        pl.BlockSpec((1, 1, D_MODEL), per_b),
        pl.BlockSpec((1, D_MODEL), full2),
        pl.BlockSpec((1, D_MODEL), full2),
        pl.BlockSpec((D_MODEL, 1024), full2),
        pl.BlockSpec((1, Q_LORA), full2),
        pl.BlockSpec((Q_LORA, 2 * MLA_HEADS * HEAD_PAD), full2),
        pl.BlockSpec((1, KV_LORA), full2),
        pl.BlockSpec((2 * KV_LORA, 2 * MLA_HEADS * HEAD_PAD), full2),
        pl.BlockSpec((KV_LORA, MLA_HEADS * HEAD_PAD), full2),
        pl.BlockSpec((1, MLA_HEADS * HEAD_PAD), full2),
        pl.BlockSpec((1, MLA_HEADS * HEAD_PAD), full2),
        tab, tab, tab, tab,
    ]
    out_specs = [
        pl.BlockSpec((1, A0_TL, MLA_HEADS * HEAD_PAD), lat),
        pl.BlockSpec((1, A0_TL, MLA_HEADS * HEAD_PAD), lambda b, j: (b, j, 0)),
        pl.BlockSpec((1, A0_TL, MLA_HEADS * HEAD_PAD), lambda b, j: (b, j, 0)),
        pl.BlockSpec((1, A0_TL, S5_WIDTH), lambda b, j: (b, j, 0)),
    ]
    out_shape = [
        jax.ShapeDtypeStruct((BATCH, SEQ, MLA_HEADS * HEAD_PAD), BF16),
        jax.ShapeDtypeStruct((BATCH, KV_LEN, MLA_HEADS * HEAD_PAD), BF16),
        jax.ShapeDtypeStruct((BATCH, KV_LEN, MLA_HEADS * HEAD_PAD), BF16),
        jax.ShapeDtypeStruct((BATCH, KV_LEN, S5_WIDTH), BF16),
    ]
    return pl.pallas_call(
        _a0_kernel, grid=(BATCH, nt + 1), in_specs=in_specs, out_specs=out_specs, out_shape=out_shape,
        compiler_params=_params(("parallel", "arbitrary")), name="even_in_proj",
    )(x, ctx, sh, sc, shc, scc, *w)


def _gain_swap(w, gain):
    half = QK_ROPE // 2
    wg = (w * gain).reshape(w.shape[0], MLA_HEADS, HEAD_PAD)
    re, im = wg[..., QK_NOPE:QK_NOPE + half], wg[..., QK_NOPE + half:QK_HEAD]
    out = jnp.concatenate([jnp.zeros_like(wg[..., :QK_NOPE]), im, re, jnp.zeros_like(wg[..., QK_HEAD:])], axis=-1)
    return out.reshape(w.shape)


def _a0_weights(w_in, q_norm, w_uq, kv_norm, w_ukv, q_qk, k_qk):
    perm = _rope_perm()
    kr0 = Q_LORA + KV_LORA
    w_cat = jnp.concatenate([
        w_in[:, :kr0], w_in[:, kr0:kr0 + QK_ROPE][:, perm],
        jnp.zeros((D_MODEL, HEAD_PAD - QK_ROPE), F32), w_in[:, kr0 + QK_ROPE:]], axis=1).astype(BF16)
    pad = HEAD_PAD - QK_HEAD

    def head_gain(g):
        gh = jnp.concatenate([g[:QK_NOPE], g[QK_NOPE:][perm], jnp.zeros((pad,), F32)])
        return jnp.tile(gh, MLA_HEADS)[None, :]

    uq = w_uq.reshape(Q_LORA, MLA_HEADS, QK_HEAD)
    uq = jnp.concatenate([uq[..., :QK_NOPE], uq[..., QK_NOPE:][..., perm],
                          jnp.zeros((Q_LORA, MLA_HEADS, pad), F32)], axis=-1)
    uq = uq.reshape(Q_LORA, MLA_HEADS * HEAD_PAD)
    uq = jnp.concatenate([uq, _gain_swap(uq, head_gain(q_qk))], axis=1).astype(BF16)
    ukv = w_ukv.reshape(KV_LORA, MLA_HEADS, QK_NOPE + V_HEAD)
    uk = jnp.concatenate([ukv[..., :QK_NOPE], jnp.zeros((KV_LORA, MLA_HEADS, HEAD_PAD - QK_NOPE), F32)], axis=-1)
    uk = uk.reshape(KV_LORA, MLA_HEADS * HEAD_PAD)
    place = np.zeros((KV_LORA, MLA_HEADS, HEAD_PAD), np.float32)
    for i in range(QK_ROPE):
        place[i, :, QK_NOPE + i] = 1.0
    wk = jnp.concatenate([uk, jnp.asarray(place.reshape(KV_LORA, MLA_HEADS * HEAD_PAD))], axis=0)
    wk = jnp.concatenate([wk, _gain_swap(wk, head_gain(k_qk))], axis=1).astype(BF16)
    wuv = jnp.concatenate([ukv[..., QK_NOPE:], jnp.zeros((KV_LORA, MLA_HEADS, HEAD_PAD - V_HEAD), F32)], axis=-1)
    wuv = wuv.reshape(KV_LORA, MLA_HEADS * HEAD_PAD).astype(BF16)
    a, b = _rope_tables()
    qs = QK_HEAD ** -0.5 * math.log2(math.e)
    tabs = [jnp.asarray(t, F32) for t in (a, b, a * qs, b * qs)]
    return [w_cat, q_norm[None, :], uq, kv_norm[None, :], wk, wuv, head_gain(q_qk), head_gain(k_qk)] + tabs


ATT_TQ = 256
HEADS_PER_STEP = 4


def _attn_kernel(q_ref, k_ref, v_ref, o_ref):
    for hh in range(HEADS_PER_STEP):
        sl = slice(hh * HEAD_PAD, (hh + 1) * HEAD_PAD)
        s = lax.dot_general(q_ref[0, :, sl], k_ref[0, :, sl], (((1,), (1,)), ((), ())),
                            preferred_element_type=F32)
        m = jnp.max(s, axis=-1, keepdims=True)
        p = jnp.exp2(s - m).astype(BF16)
        acc = jnp.dot(p, v_ref[0, :, sl], preferred_element_type=F32)
        o_ref[0, :, sl] = (acc * (1.0 / acc[:, V_HEAD:V_HEAD + 1])).astype(BF16)


def _attn_call(q, k, v):
    wq = HEADS_PER_STEP * HEAD_PAD
    return pl.pallas_call(
        _attn_kernel, grid=(BATCH, MLA_HEADS // HEADS_PER_STEP, SEQ // ATT_TQ),
        in_specs=[pl.BlockSpec((1, ATT_TQ, wq), lambda b, h, i: (b, i, h)),
                  pl.BlockSpec((1, KV_LEN, wq), lambda b, h, i: (b, 0, h)),
                  pl.BlockSpec((1, KV_LEN, wq), lambda b, h, i: (b, 0, h))],
        out_specs=pl.BlockSpec((1, ATT_TQ, wq), lambda b, h, i: (b, i, h)),
        out_shape=jax.ShapeDtypeStruct((BATCH, SEQ, MLA_HEADS * HEAD_PAD), BF16),
        compiler_params=_params(("parallel", "parallel", "arbitrary")), name="mla_attention",
    )(q, k, v)


S5_ROWS = S5_NCHUNK * BATCH
S5_ROWS_LAT = S5_NCHUNK_LAT * BATCH
S5_COLS = S5_CHUNK * S5_GROUP
S5_SW = 2 * S5_STATE


def _s5_kernel(u_ref, r_ref, mb_ref, mc_ref, coef_ref, y_ref, x_sc, sp_sc, t_sc):
    u = u_ref[0]
    lags = r_ref[0]
    for sg in range(S5_CHUNK):
        off = (S5_CHUNK - 1 - sg) * S5_GROUP
        t_sc[sg * S5_GROUP:(sg + 1) * S5_GROUP, :] = lags[:, off:off + S5_COLS].astype(BF16)
    x_sc[...] = jnp.dot(u, mb_ref[0], preferred_element_type=F32)
    cf = coef_ref[0]
    af, bfm, bfp, ab, bbm, bbp = [cf[i * 8:(i + 1) * 8] for i in range(6)]

    def body(i, carry):
        sf, sfw, sb, sbw = carry
        cfw = jnp.where(i < S5_NCHUNK_CTX, i + S5_NCHUNK_LAT, i - S5_NCHUNK_CTX)
        rf = pl.multiple_of(cfw * BATCH, BATCH)
        rb = pl.multiple_of((S5_NCHUNK - 1 - i) * BATCH, BATCH)
        sp_sc[pl.ds(rf, BATCH), 0:S5_SW] = sf
        sp_sc[pl.ds(rb, BATCH), S5_SW:2 * S5_SW] = sb
        xf = x_sc[pl.ds(rf, BATCH), 0:S5_SW]
        xfw = x_sc[pl.ds(rf, BATCH), S5_SW:2 * S5_SW]
        xb = x_sc[pl.ds(rb, BATCH), 2 * S5_SW:3 * S5_SW]
        xbw = x_sc[pl.ds(rb, BATCH), 3 * S5_SW:4 * S5_SW]
        return (sf * af + sfw * bfm + xf, sfw * af + sf * bfp + xfw,
                sb * ab + sbw * bbm + xb, sbw * ab + sb * bbp + xbw)

    z = jnp.zeros((BATCH, S5_SW), F32)
    lax.fori_loop(0, S5_NCHUNK, body, (z, z, z, z))
    y = jnp.dot(u[:S5_ROWS_LAT], t_sc[...], preferred_element_type=F32)
    y = y + jnp.dot(sp_sc[0:S5_ROWS_LAT, :].astype(BF16), mc_ref[0], preferred_element_type=F32)
    y_ref[0] = y.astype(BF16)


def _s5_call(ug, t, mb, mc, coef):
    g3 = lambda g: (g, 0, 0)
    return pl.pallas_call(
        _s5_kernel, grid=(S5_GROUPS,),
        in_specs=[pl.BlockSpec((1, S5_ROWS, S5_COLS), g3),
                  pl.BlockSpec((1, S5_GROUP, 2 * S5_COLS), g3),
                  pl.BlockSpec((1, S5_COLS, 4 * S5_SW), g3),
                  pl.BlockSpec((1, 2 * S5_SW, S5_COLS), g3),
                  pl.BlockSpec((1, 6 * 8, S5_SW), g3)],
        out_specs=pl.BlockSpec((1, S5_ROWS_LAT, S5_COLS), g3),
        out_shape=jax.ShapeDtypeStruct((S5_GROUPS, S5_ROWS_LAT, S5_COLS), BF16),
        scratch_shapes=[pltpu.VMEM((S5_ROWS, 4 * S5_SW), F32), pltpu.VMEM((S5_ROWS, 2 * S5_SW), F32),
                        pltpu.VMEM((S5_COLS, S5_COLS), BF16)],
        compiler_params=_params(("parallel",)), name="s5_chunked_scan",
    )(ug, t, mb, mc, coef)


def _s5_weights(lam_re, lam_im, log_step, b_re, b_im, c_re, c_im):
    q = S5_CHUNK
    hi = lax.Precision.HIGHEST
    t_blocks, mbs, mcs, coefs = [], [], [], []
    sig = jnp.arange(q)
    for d in range(2):
        lr = jnp.minimum(lam_re[d], -1e-4)
        li = lam_im[d]
        step = jnp.exp(log_step[d])[:, None]
        jj = jnp.arange(q + 1, dtype=F32)[:, None, None]
        mag = jnp.exp(lr * step * jj)
        ph = li * step * jj
        pr, pi = mag * jnp.cos(ph), mag * jnp.sin(ph)
        nr, ni = pr[1] - 1.0, pi[1]
        den = lr * lr + li * li
        fr, fi = (nr * lr + ni * li) / den, (ni * lr - nr * li) / den
        br = fr[..., None] * b_re[d] - fi[..., None] * b_im[d]
        bi = fr[..., None] * b_im[d] + fi[..., None] * b_re[d]
        cr, ci = c_re[d], c_im[d]
        cpr = cr[None] * pr[:, :, None, :] - ci[None] * pi[:, :, None, :]
        cpi = cr[None] * pi[:, :, None, :] + ci[None] * pr[:, :, None, :]
        kern = (jnp.einsum('jghp,gpk->jghk', cpr[:q], br, precision=hi)
                - jnp.einsum('jghp,gpk->jghk', cpi[:q], bi, precision=hi))
        kt = kern.transpose(1, 3, 0, 2)
        zero_slots = jnp.zeros((S5_GROUPS, S5_GROUP, q, S5_GROUP), F32)
        if d == 0:
            t_blocks.append(jnp.concatenate([zero_slots[:, :, :q - 1], kt, zero_slots[:, :, :1]], axis=2))
        else:
            t_blocks.append(jnp.concatenate([kt[:, :, ::-1], zero_slots], axis=2))
        pw = (q - 1 - sig) if d == 0 else sig
        xr = pr[pw][..., None] * br[None] - pi[pw][..., None] * bi[None]
        xi = pr[pw][..., None] * bi[None] + pi[pw][..., None] * br[None]
        xr = xr.transpose(1, 0, 3, 2).reshape(S5_GROUPS, S5_COLS, S5_STATE)
        xi = xi.transpose(1, 0, 3, 2).reshape(S5_GROUPS, S5_COLS, S5_STATE)
        mbs += [xr, xi, xi, xr]
        po = (sig + 1) if d == 0 else (q - sig)
        mr = cpr[po].transpose(1, 3, 0, 2).reshape(S5_GROUPS, S5_STATE, S5_COLS)
        mi = -cpi[po].transpose(1, 3, 0, 2).reshape(S5_GROUPS, S5_STATE, S5_COLS)
        mcs += [mr, mi]
        are, aim = pr[q], pi[q]
        rows = [jnp.concatenate([are, are], -1), jnp.concatenate([-aim, aim], -1), jnp.concatenate([aim, -aim], -1)]
        coefs += [jnp.broadcast_to(r[:, None, :], (S5_GROUPS, 8, S5_SW)) for r in rows]
    t = (t_blocks[0] + t_blocks[1]).reshape(S5_GROUPS, S5_GROUP, 2 * S5_COLS)
    mb = jnp.concatenate(mbs, axis=-1).astype(BF16)
    mc = jnp.concatenate(mcs, axis=1).astype(BF16)
    coef = jnp.concatenate(coefs, axis=1)
    return t, mb, mc, coef


A1_TL = 512


def _a1_kernel(att_ref, ys_ref, u_ref, d_ref, gw_ref, gb_ref, woa_ref, wos_ref, x_ref, g_ref, o_ref):
    y = u_ref[0].astype(F32) * d_ref[...] + ys_ref[0].astype(F32)
    z = jax.nn.gelu(y)
    gate = jax.nn.sigmoid(jnp.dot(z.astype(BF16), gw_ref[...], preferred_element_type=F32) + gb_ref[...])
    s5 = (z * gate).astype(BF16)
    mix = jnp.dot(att_ref[0], woa_ref[...], preferred_element_type=F32)
    mix = mix + jnp.dot(s5, wos_ref[...], preferred_element_type=F32)
    o_ref[0] = x_ref[0] + g_ref[0] * mix


def _a1_call(att, ys, u, d, gw, gb, woa, wos, x, g1):
    t3 = lambda b, j: (b, j, 0)
    full2 = lambda b, j: (0, 0)
    return pl.pallas_call(
        _a1_kernel, grid=(BATCH, SEQ // A1_TL),
        in_specs=[pl.BlockSpec((1, A1_TL, MLA_HEADS * HEAD_PAD), t3),
                  pl.BlockSpec((1, A1_TL, S5_WIDTH), t3),
                  pl.BlockSpec((1, A1_TL, S5_WIDTH), t3),
                  pl.BlockSpec((1, S5_WIDTH), full2),
                  pl.BlockSpec((S5_WIDTH, S5_WIDTH), full2),
                  pl.BlockSpec((1, S5_WIDTH), full2),
                  pl.BlockSpec((MLA_HEADS * HEAD_PAD, D_MODEL), full2),
                  pl.BlockSpec((S5_WIDTH, D_MODEL), full2),
                  pl.BlockSpec((1, A1_TL, D_MODEL), t3),
                  pl.BlockSpec((1, 1, D_MODEL), lambda b, j: (b, 0, 0))],
        out_specs=pl.BlockSpec((1, A1_TL, D_MODEL), t3),
        out_shape=jax.ShapeDtypeStruct((BATCH, SEQ, D_MODEL), F32),
        compiler_params=_params(("parallel", "arbitrary")), name="even_out_proj",
    )(att, ys, u, d, gw, gb, woa, wos, x, g1)


MOE_TL = 512
SLOT_PAD = 8


def _slot_columns(cols, lane):
    out = jnp.zeros(lane.shape, cols[0].dtype)
    for k, col in enumerate(cols):
        out = jnp.where(lane == k, col, out)
    return out[:, :SLOT_PAD]


def _pack_rows(v):
    halves = []
    for p in range(2):
        base = 2 * p * ROW_WORDS
        a = pltpu.bitcast(v[:, base:base + ROW_WORDS].astype(BF16).astype(F32), jnp.uint32)
        b = pltpu.bitcast(v[:, base + ROW_WORDS:base + 2 * ROW_WORDS].astype(BF16).astype(F32), jnp.uint32)
        halves.append((a >> 16) | b)
    return halves


def _unpack_rows(lo, hi):
    out = []
    for w in (lo, hi):
        out.append(pltpu.bitcast(w << 16, F32))
        out.append(pltpu.bitcast(w & jnp.uint32(0xFFFF0000), F32))
    return out


def _moe_in_kernel(x_ref, sh_ref, sc_ref, rw_ref, rb_ref, tri_ref,
                   hlo_ref, hhi_ref, idx_ref, wt_ref, rank_ref, cnt_ref, run_sc):
    @pl.when((pl.program_id(0) == 0) & (pl.program_id(1) == 0))
    def _():
        run_sc[...] = jnp.zeros_like(run_sc)

    h = _norm_mod(x_ref[0], sh_ref[0], sc_ref[0])
    hb = h.astype(BF16)
    h_lo = (h - hb.astype(F32)).astype(BF16)
    rw = rw_ref[...]
    rw_hi = rw.astype(BF16)
    rw_lo = (rw - rw_hi.astype(F32)).astype(BF16)
    logits = jnp.dot(jnp.concatenate([hb, hb, h_lo], axis=1), jnp.concatenate([rw_hi, rw_lo, rw_hi], axis=0),
                     preferred_element_type=F32)
    scores = jax.nn.sigmoid(logits)
    hlo_ref[0], hhi_ref[0] = _pack_rows(h)

    work = scores + rb_ref[...]
    lane = lax.broadcasted_iota(jnp.int32, work.shape, 1)
    hits, ids = [], []
    for _ in range(TOP_K):
        m = jnp.max(work, axis=-1, keepdims=True)
        ik = jnp.min(jnp.where(work == m, lane, N_EXPERTS), axis=-1, keepdims=True)
        hit = lane == ik
        hits.append(hit)
        ids.append(ik)
        work = jnp.where(hit, -jnp.inf, work)
    mask = hits[0]
    for hit in hits[1:]:
        mask = jnp.logical_or(mask, hit)
    maskf = mask.astype(F32)
    before = jnp.dot(tri_ref[...], maskf.astype(BF16), preferred_element_type=F32) + run_sc[...]
    sel = [jnp.sum(jnp.where(hit, scores, 0.0), axis=-1, keepdims=True) for hit in hits]
    denom = sel[0]
    for s in sel[1:]:
        denom = denom + s
    ranks = [jnp.sum(jnp.where(hit, before, 0.0), axis=-1, keepdims=True) for hit in hits]
    lane128 = lax.broadcasted_iota(jnp.int32, (MOE_TL, 128), 1)
    idx_ref[0] = _slot_columns(ids, lane128)
    wt_ref[0] = _slot_columns([s / denom * ROUTE_SCALE for s in sel], lane128)
    rank_ref[0] = _slot_columns([r.astype(jnp.int32) for r in ranks], lane128)
    run_sc[...] += jnp.sum(maskf, axis=0, keepdims=True)
    cnt_ref[...] = run_sc[...]


def _moe_in_call(x, sh, sc, rw, rb, part):
    b0 = part * MOE_GROUP_B
    t3 = lambda b, j: (b, j, 0)
    full2 = lambda b, j: (0, 0)
    per_b = lambda b, j: (b + b0, 0, 0)
    tri = jnp.asarray(np.tril(np.ones((MOE_TL, MOE_TL), np.float32), -1), BF16)
    slot = jax.ShapeDtypeStruct((MOE_GROUP_B, SEQ, SLOT_PAD), jnp.int32)
    return pl.pallas_call(
        _moe_in_kernel, grid=(MOE_GROUP_B, SEQ // MOE_TL),
        in_specs=[pl.BlockSpec((1, MOE_TL, D_MODEL), lambda b, j: (b + b0, j, 0)),
                  pl.BlockSpec((1, 1, D_MODEL), per_b),
                  pl.BlockSpec((1, 1, D_MODEL), per_b),
                  pl.BlockSpec((D_MODEL, N_EXPERTS), full2),
                  pl.BlockSpec((1, N_EXPERTS), full2),
                  pl.BlockSpec((MOE_TL, MOE_TL), full2)],
        out_specs=[pl.BlockSpec((1, MOE_TL, ROW_WORDS), t3),
                   pl.BlockSpec((1, MOE_TL, ROW_WORDS), t3),
                   pl.BlockSpec((1, MOE_TL, SLOT_PAD), t3),
                   pl.BlockSpec((1, MOE_TL, SLOT_PAD), t3),
                   pl.BlockSpec((1, MOE_TL, SLOT_PAD), t3),
                   pl.BlockSpec((1, N_EXPERTS), full2)],
        out_shape=[jax.ShapeDtypeStruct((MOE_GROUP_B, SEQ, ROW_WORDS), jnp.uint32),
                   jax.ShapeDtypeStruct((MOE_GROUP_B, SEQ, ROW_WORDS), jnp.uint32),
                   slot,
                   jax.ShapeDtypeStruct((MOE_GROUP_B, SEQ, SLOT_PAD), F32),
                   slot,
                   jax.ShapeDtypeStruct((1, N_EXPERTS), F32)],
        scratch_shapes=[pltpu.VMEM((1, N_EXPERTS), F32)],
        compiler_params=_params(("arbitrary", "arbitrary")), name="moe_router",
    )(x, sh, sc, rw, rb, tri)


def _sc_mesh():
    return plsc.VectorSubcoreMesh(core_axis_name="c", subcore_axis_name="s")


def _sc_dispatch(h_words, dest, n_rows):
    n_tok = h_words.shape[0]

    @pl.kernel(out_type=jax.ShapeDtypeStruct((n_rows, ROW_WORDS), jnp.uint32), mesh=_sc_mesh(), scratch_types=[])
    def scatter_rows(h_hbm, i_hbm, o_hbm):
        def body(h_vmem, i_vmem):
            for k in range(TOP_K):
                pltpu.sync_copy(h_vmem, o_hbm.at[i_vmem.at[k]])

        pltpu.emit_pipeline(
            body, grid=(n_tok // SC_WINDOW,),
            in_specs=[pl.BlockSpec((SC_WINDOW, ROW_WORDS), index_map=lambda i: (i, 0)),
                      pl.BlockSpec((SLOT_PAD, SC_WINDOW), index_map=lambda i: (0, i))],
            out_specs=[],
            core_axis_name=("c", "s"), dimension_semantics=(pltpu.PARALLEL,),
        )(h_hbm, i_hbm)

    return scatter_rows(h_words, dest)


def _sc_collect(y_words, dest):
    n_tok = dest.shape[1]

    @pl.kernel(out_type=jax.ShapeDtypeStruct((TOP_K, n_tok, ROW_WORDS), jnp.uint32), mesh=_sc_mesh(),
               scratch_types=[])
    def gather_rows(y_hbm, i_hbm, o_hbm):
        def body(i_vmem, o_vmem):
            pltpu.sync_copy(y_hbm.at[i_vmem.at[0]], o_vmem.at[0])

        pltpu.emit_pipeline(
            body, grid=(TOP_K, n_tok // SC_WINDOW),
            in_specs=[pl.BlockSpec((1, SC_WINDOW), index_map=lambda k, i: (k, i))],
            out_specs=[pl.BlockSpec((1, SC_WINDOW, ROW_WORDS), index_map=lambda k, i: (k, i, 0))],
            core_axis_name=("c", "s"), dimension_semantics=(pltpu.PARALLEL, pltpu.PARALLEL),
        )(i_hbm, o_hbm)

    return gather_rows(y_words, dest)


def _expert_kernel(be_ref, nv_ref, xlo_ref, xhi_ref, wg_ref, wu_ref, wd_ref, ylo_ref, yhi_ref,
                   wg_sc, wu_sc, wd_sc):
    i = pl.program_id(0)
    nv = nv_ref[i]

    @pl.when(jnp.logical_or(i == 0, be_ref[i] != be_ref[jnp.maximum(i - 1, 0)]))
    def _():
        wg_sc[...] = wg_ref[0, 0].astype(BF16)
        wu_sc[...] = wu_ref[0, 0].astype(BF16)
        wd_sc[...] = wd_ref[0, 0].astype(BF16)

    @pl.when(nv > 0)
    def _():
        parts = _unpack_rows(xlo_ref[...], xhi_ref[...])
        xb = jnp.concatenate([p.astype(BF16) for p in parts], axis=1)
        live = lax.broadcasted_iota(jnp.int32, xb.shape, 0) < nv
        xb = jnp.where(live, xb, jnp.zeros_like(xb))
        hid = jax.nn.silu(jnp.dot(xb, wg_sc[...], preferred_element_type=F32))
        hid = hid * jnp.dot(xb, wu_sc[...], preferred_element_type=F32)
        y = jnp.dot(hid.astype(BF16), wd_sc[...], preferred_element_type=F32)
        ylo_ref[...], yhi_ref[...] = _pack_rows(y)

    @pl.when(nv == 0)
    def _():
        ylo_ref[...] = jnp.zeros_like(ylo_ref)
        yhi_ref[...] = jnp.zeros_like(yhi_ref)


def _expert_call(block_e, n_valid, xlo, xhi, wg, wu, wd, li):
    n_rows = xlo.shape[0]
    n_blocks = n_rows // MOE_BLOCK
    rows = pl.BlockSpec((MOE_BLOCK, ROW_WORDS), lambda i, be, nv: (i, 0))
    grid_spec = pltpu.PrefetchScalarGridSpec(
        num_scalar_prefetch=2, grid=(n_blocks,),
        in_specs=[rows, rows,
                  pl.BlockSpec((1, 1, D_MODEL, EXPERT_FF), lambda i, be, nv: (li, be[i], 0, 0)),
                  pl.BlockSpec((1, 1, D_MODEL, EXPERT_FF), lambda i, be, nv: (li, be[i], 0, 0)),
                  pl.BlockSpec((1, 1, EXPERT_FF, D_MODEL), lambda i, be, nv: (li, be[i], 0, 0))],
        out_specs=[rows, rows],
        scratch_shapes=[pltpu.VMEM((D_MODEL, EXPERT_FF), BF16), pltpu.VMEM((D_MODEL, EXPERT_FF), BF16),
                        pltpu.VMEM((EXPERT_FF, D_MODEL), BF16)])
    out = jax.ShapeDtypeStruct((n_rows, ROW_WORDS), jnp.uint32)
    return pl.pallas_call(
        _expert_kernel, grid_spec=grid_spec, out_shape=[out, out],
        compiler_params=_params(("arbitrary",)), name="moe_experts",
    )(block_e, n_valid, xlo, xhi, wg, wu, wd)


def _combine_kernel(ylo_ref, yhi_ref, w_ref, x_ref, sh_ref, sc_ref, g_ref, sg_ref, su_ref, sd_ref, *rest):
    o_ref = rest[-1]
    hb = _norm_mod(x_ref[0], sh_ref[0], sc_ref[0]).astype(BF16)
    hid = jax.nn.silu(jnp.dot(hb, sg_ref[...].astype(BF16), preferred_element_type=F32))
    hid = hid * jnp.dot(hb, su_ref[...].astype(BF16), preferred_element_type=F32)
    shared = jnp.dot(hid.astype(BF16), sd_ref[...].astype(BF16), preferred_element_type=F32)
    w = w_ref[0]
    acc = [None] * 4
    for k in range(TOP_K):
        wk = w[:, k:k + 1]
        for c, part in enumerate(_unpack_rows(ylo_ref[k], yhi_ref[k])):
            acc[c] = wk * part if acc[c] is None else acc[c] + wk * part
    for c in range(4):
        sl = slice(c * ROW_WORDS, (c + 1) * ROW_WORDS)
        o_ref[0, :, sl] = x_ref[0, :, sl] + g_ref[0, :, sl] * (acc[c] + shared[:, sl])


def _combine_call(ylo, yhi, wts, x, sh, sc, g2, sg, su, sd, part, prev):
    b0 = part * MOE_GROUP_B
    t3 = lambda b, j: (b, j, 0)
    g3 = lambda b, j: (b + b0, j, 0)
    per_b = lambda b, j: (b + b0, 0, 0)
    full2 = lambda b, j: (0, 0)
    ff = sg.shape[1]
    nt = SEQ // MOE_TL
    rows = pl.BlockSpec((TOP_K, MOE_TL, ROW_WORDS), lambda b, j: (0, b * nt + j, 0))
    extra_specs = [] if prev is None else [pl.BlockSpec(memory_space=pl.ANY)]
    extra_args = [] if prev is None else [prev]
    aliases = {} if prev is None else {10: 0}
    return pl.pallas_call(
        _combine_kernel, grid=(MOE_GROUP_B, nt),
        in_specs=[rows, rows,
                  pl.BlockSpec((1, MOE_TL, SLOT_PAD), t3),
                  pl.BlockSpec((1, MOE_TL, D_MODEL), g3),
                  pl.BlockSpec((1, 1, D_MODEL), per_b),
                  pl.BlockSpec((1, 1, D_MODEL), per_b),
                  pl.BlockSpec((1, 1, D_MODEL), per_b),
                  pl.BlockSpec((D_MODEL, ff), full2),
                  pl.BlockSpec((D_MODEL, ff), full2),
                  pl.BlockSpec((ff, D_MODEL), full2)] + extra_specs,
        out_specs=pl.BlockSpec((1, MOE_TL, D_MODEL), g3),
        out_shape=jax.ShapeDtypeStruct((BATCH, SEQ, D_MODEL), F32),
        input_output_aliases=aliases,
        compiler_params=_params(("parallel", "arbitrary")), name="moe_combine_shared",
    )(ylo, yhi, wts, x, sh, sc, g2, sg, su, sd, *extra_args)


def _moe_group(x, sh, sc, router_w, router_b, w_gate, w_up, w_down, li, part):
    T = MOE_GROUP_B * SEQ
    TK = T * TOP_K
    hlo, hhi, idx, wts, rank, counts = _moe_in_call(x, sh, sc, router_w, router_b[None, :], part)
    counts = counts[0].astype(jnp.int32)
    padded = (counts + MOE_BLOCK - 1) // MOE_BLOCK * MOE_BLOCK
    pad_end = jnp.cumsum(padded)
    pad_start = pad_end - padded
    n_blocks = -(-TK // MOE_BLOCK) + N_EXPERTS
    n_rows = n_blocks * MOE_BLOCK
    block_start = jnp.arange(n_blocks, dtype=jnp.int32) * MOE_BLOCK
    owns = jnp.logical_and(block_start[:, None] >= pad_start[None, :], block_start[:, None] < pad_end[None, :])
    owns = owns.astype(jnp.int32)
    experts = jnp.arange(N_EXPERTS, dtype=jnp.int32)[None, :]
    block_e = jnp.sum(owns * experts, axis=1) + (N_EXPERTS - 1) * (1 - jnp.sum(owns, axis=1))
    n_valid = jnp.sum(owns * (counts[None, :] - (block_start[:, None] - pad_start[None, :])), axis=1)
    n_valid = jnp.clip(n_valid, 0, MOE_BLOCK).astype(jnp.int32)
    dest = (pad_start[idx.reshape(T, SLOT_PAD)] + rank.reshape(T, SLOT_PAD)).T.astype(jnp.int32)
    xlo = _sc_dispatch(hlo.reshape(T, ROW_WORDS), dest, n_rows)
    xhi = _sc_dispatch(hhi.reshape(T, ROW_WORDS), dest, n_rows)
    ylo, yhi = _expert_call(block_e, n_valid, xlo, xhi, w_gate, w_up, w_down, li)
    return _sc_collect(ylo, dest), _sc_collect(yhi, dest), wts


def _moe(x, sh, sc, g2, router_w, router_b, w_gate, w_up, w_down, sh_gate, sh_up, sh_down, li):
    groups = [_moe_group(x, sh, sc, router_w, router_b, w_gate, w_up, w_down, li, part)
              for part in range(BATCH // MOE_GROUP_B)]
    out = None
    for part, (glo, ghi, wts) in enumerate(groups):
        out = _combine_call(glo, ghi, wts, x, sh, sc, g2, sh_gate, sh_up, sh_down, part, out)
    return out


HY_TL = 512
HALO = 8


def _hy_in_kernel(x_ref, xp_ref, xn_ref, sh_ref, sc_ref, w_ref, cw_ref, cb_ref, z_ref, x0_ref, h_sc):
    j = pl.program_id(1)
    shift, scale = sh_ref[0], sc_ref[0]
    keep_prev = (j > 0).astype(F32)
    keep_next = (j < SEQ // HY_TL - 1).astype(F32)
    h_sc[0:HALO, :] = _norm_mod(xp_ref[0], shift, scale) * keep_prev
    h_sc[HALO:HALO + HY_TL, :] = _norm_mod(x_ref[0], shift, scale)
    h_sc[HALO + HY_TL:, :] = _norm_mod(xn_ref[0], shift, scale) * keep_next
    hcat = h_sc[...].astype(BF16)
    outs = []
    for part in range(3):
        sl = slice(part * HY_WIDTH, (part + 1) * HY_WIDTH)
        p = jnp.dot(hcat, w_ref[:, sl], preferred_element_type=F32)
        o = (p[HALO - 1:HALO - 1 + HY_TL] * cw_ref[0:1, sl] + p[HALO:HALO + HY_TL] * cw_ref[1:2, sl]
             + p[HALO + 1:HALO + 1 + HY_TL] * cw_ref[2:3, sl] + cb_ref[:, sl])
        outs.append(o)
    x0_ref[0] = outs[0].astype(BF16)
    z_ref[0] = (outs[2] * outs[1]).astype(BF16)


def _hy_in_call(x, sh, sc, w, cw, cb):
    nb8 = HY_TL // HALO
    t3 = lambda b, j: (b, j, 0)
    full2 = lambda b, j: (0, 0)
    per_b = lambda b, j: (b, 0, 0)
    return pl.pallas_call(
        _hy_in_kernel, grid=(BATCH, SEQ // HY_TL),
        in_specs=[pl.BlockSpec((1, HY_TL, D_MODEL), t3),
                  pl.BlockSpec((1, HALO, D_MODEL), lambda b, j: (b, jnp.maximum(j * nb8 - 1, 0), 0)),
                  pl.BlockSpec((1, HALO, D_MODEL), lambda b, j: (b, jnp.minimum((j + 1) * nb8, SEQ // HALO - 1), 0)),
                  pl.BlockSpec((1, 1, D_MODEL), per_b),
                  pl.BlockSpec((1, 1, D_MODEL), per_b),
                  pl.BlockSpec((D_MODEL, 3 * HY_WIDTH), full2),
                  pl.BlockSpec((SHORT_CONV, 3 * HY_WIDTH), full2),
                  pl.BlockSpec((1, 3 * HY_WIDTH), full2)],
        out_specs=[pl.BlockSpec((1, HY_TL, HY_WIDTH), t3), pl.BlockSpec((1, HY_TL, HY_WIDTH), t3)],
        out_shape=[jax.ShapeDtypeStruct((BATCH, SEQ, HY_WIDTH), BF16),
                   jax.ShapeDtypeStruct((BATCH, SEQ, HY_WIDTH), BF16)],
        scratch_shapes=[pltpu.VMEM((HY_TL + 2 * HALO, D_MODEL), F32)],
        compiler_params=_params(("parallel", "arbitrary")), name="hyena_in_proj",
    )(x, x, x, sh, sc, w, cw, cb)


def _fft_tables():
    c = np.arange(FFT_N2, dtype=np.int64)
    ang = 2.0 * np.pi * ((c[:, None] * c[None, :]) % FFT_N2) / FFT_N2
    sr, si = np.cos(ang), -np.sin(ang)
    m = np.block([[sr, -si], [si, sr]])
    k1 = np.arange(FFT_NK, dtype=np.int64)
    ang_t = 2.0 * np.pi * (k1[:, None] * c[None, :]) / DFT_N
    lanes = np.ones((1, 1, 128))
    tr = np.cos(ang_t)[:, :, None] * lanes
    ti = -np.sin(ang_t)[:, :, None] * lanes
    return jnp.asarray(m, BF16), jnp.asarray(tr, F32), jnp.asarray(ti, F32)


def _lin(acc, coef, val):
    if abs(coef) < 1e-12:
        return acc
    term = val if coef == 1.0 else (-val if coef == -1.0 else coef * val)
    return term if acc is None else acc + term


def _twiddle(tr_ref, ti_ref, k1, width):
    reps = width // 128
    tr, ti = tr_ref[k1], ti_ref[k1]
    return jnp.concatenate([tr] * reps, axis=1), jnp.concatenate([ti] * reps, axis=1)


def _class_spectrum(block, k1, m_ref, tr_ref, ti_ref, width):
    yr = yi = None
    for a in range(FFT_NA):
        th = 2.0 * math.pi * ((a * k1) % FFT_N1) / FFT_N1
        za = block(a)
        yr = _lin(yr, round(math.cos(th), 15), za)
        yi = _lin(yi, round(-math.sin(th), 15), za)
    if k1 > 0:
        tr, ti = _twiddle(tr_ref, ti_ref, k1, width)
        yr, yi = (yr * tr, yr * ti) if yi is None else (yr * tr - yi * ti, yr * ti + yi * tr)
    if yi is None:
        x = jnp.dot(m_ref[:, :FFT_N2], yr.astype(BF16), preferred_element_type=F32)
    else:
        x = jnp.dot(m_ref[...], jnp.concatenate([yr, yi], axis=0).astype(BF16), preferred_element_type=F32)
    return x[:FFT_N2], x[FFT_N2:]


def _class_inverse(yr, yi, k1, m_ref, tr_ref, ti_ref, acc_ref, width):
    v = jnp.dot(m_ref[...], jnp.concatenate([yr, -yi], axis=0).astype(BF16), preferred_element_type=F32)
    ur, ui = v[:FFT_N2], -v[FFT_N2:]
    if k1 > 0:
        tr, ti = _twiddle(tr_ref, ti_ref, k1, width)
        ur, ui = ur * tr + ui * ti, ui * tr - ur * ti
    scale = (1.0 if k1 in (0, FFT_N1 // 2) else 2.0) / DFT_N
    for a in range(FFT_NA):
        th = 2.0 * math.pi * ((a * k1) % FFT_N1) / FFT_N1
        term = _lin(None, round(math.cos(th), 15) * scale, ur)
        term = _lin(term, round(-math.sin(th), 15) * scale, ui)
        rows = slice(a * FFT_N2, (a + 1) * FFT_N2)
        if k1 == 0:
            acc_ref[rows, :] = term
        else:
            acc_ref[rows, :] += term


def _spec_kernel(hf_ref, hb_ref, m_ref, tr_ref, ti_ref, c_ref):
    for k1 in range(FFT_NK):
        fr, fi = _class_spectrum(lambda a: hf_ref[a * FFT_N2:(a + 1) * FFT_N2, :], k1, m_ref, tr_ref, ti_ref, HY_CT)
        br, bi = _class_spectrum(lambda a: hb_ref[a * FFT_N2:(a + 1) * FFT_N2, :], k1, m_ref, tr_ref, ti_ref, HY_CT)
        c_ref[k1, :FFT_N2, :] = (fr + br).astype(BF16)
        c_ref[k1, FFT_N2:, :] = (fi - bi).astype(BF16)


def _fft_table_specs(ngrid):
    z = (0,) * 2
    z3 = (0,) * 3
    if ngrid == 1:
        return [pl.BlockSpec((2 * FFT_N2, 2 * FFT_N2), lambda c: z),
                pl.BlockSpec((FFT_NK, FFT_N2, 128), lambda c: z3),
                pl.BlockSpec((FFT_NK, FFT_N2, 128), lambda c: z3)]
    return [pl.BlockSpec((2 * FFT_N2, 2 * FFT_N2), lambda b, c: z),
            pl.BlockSpec((FFT_NK, FFT_N2, 128), lambda b, c: z3),
            pl.BlockSpec((FFT_NK, FFT_N2, 128), lambda b, c: z3)]


def _spec_call(hfb, m, tr, ti):
    nct = HY_WIDTH // HY_CT
    return pl.pallas_call(
        _spec_kernel, grid=(nct,),
        in_specs=[pl.BlockSpec((SEQ, HY_CT), lambda c: (0, c)),
                  pl.BlockSpec((SEQ, HY_CT), lambda c: (0, c + nct))] + _fft_table_specs(1),
        out_specs=pl.BlockSpec((FFT_NK, 2 * FFT_N2, HY_CT), lambda c: (0, 0, c)),
        out_shape=jax.ShapeDtypeStruct((FFT_NK, 2 * FFT_N2, HY_WIDTH), BF16),
        compiler_params=_params(("arbitrary",)), name="hyena_filter_spectrum",
    )(hfb, hfb, m, tr, ti)


def _conv_kernel(z_ref, c_ref, m_ref, tr_ref, ti_ref, y_ref, acc):
    for k1 in range(FFT_NK):
        xr, xi = _class_spectrum(lambda a: z_ref[0, a * FFT_N2:(a + 1) * FFT_N2, :].astype(F32), k1,
                                 m_ref, tr_ref, ti_ref, HY_CT)
        cr = c_ref[k1, :FFT_N2, :].astype(F32)
        ci = c_ref[k1, FFT_N2:, :].astype(F32)
        _class_inverse(xr * cr - xi * ci, xr * ci + xi * cr, k1, m_ref, tr_ref, ti_ref, acc, HY_CT)
    y_ref[0] = acc[...].astype(BF16)


def _conv_call(z, spec, m, tr, ti):
    return pl.pallas_call(
        _conv_kernel, grid=(HY_WIDTH // HY_CT, BATCH),
        in_specs=[pl.BlockSpec((1, SEQ, HY_CT), lambda c, b: (b, 0, c)),
                  pl.BlockSpec((FFT_NK, 2 * FFT_N2, HY_CT), lambda c, b: (0, 0, c))] + _fft_table_specs(2),
        out_specs=pl.BlockSpec((1, SEQ, HY_CT), lambda c, b: (b, 0, c)),
        out_shape=jax.ShapeDtypeStruct((BATCH, SEQ, HY_WIDTH), BF16),
        scratch_shapes=[pltpu.VMEM((SEQ, HY_CT), F32)],
        compiler_params=_params(("parallel", "arbitrary")), name="hyena_long_conv",
    )(z, spec, m, tr, ti)


def _hy_out_kernel(y_ref, z_ref, x0_ref, b_ref, w_ref, x_ref, g_ref, o_ref):
    z = z_ref[0].astype(F32)
    gated = x0_ref[0].astype(F32) * (y_ref[0].astype(F32) + b_ref[...] * z)
    mix = jnp.dot(gated.astype(BF16), w_ref[...], preferred_element_type=F32)
    o_ref[0] = x_ref[0] + g_ref[0] * mix


def _hy_out_call(y, z, x0, bias, w, x, g1):
    t3 = lambda b, j: (b, j, 0)
    full2 = lambda b, j: (0, 0)
    return pl.pallas_call(
        _hy_out_kernel, grid=(BATCH, SEQ // HY_TL),
        in_specs=[pl.BlockSpec((1, HY_TL, HY_WIDTH), t3),
                  pl.BlockSpec((1, HY_TL, HY_WIDTH), t3),
                  pl.BlockSpec((1, HY_TL, HY_WIDTH), t3),
                  pl.BlockSpec((1, HY_WIDTH), full2),
                  pl.BlockSpec((HY_WIDTH, D_MODEL), full2),
                  pl.BlockSpec((1, HY_TL, D_MODEL), t3),
                  pl.BlockSpec((1, 1, D_MODEL), lambda b, j: (b, 0, 0))],
        out_specs=pl.BlockSpec((1, HY_TL, D_MODEL), t3),
        out_shape=jax.ShapeDtypeStruct((BATCH, SEQ, D_MODEL), F32),
        compiler_params=_params(("parallel", "arbitrary")), name="hyena_out_proj",
    )(y, z, x0, bias, w, x, g1)


def _hyena_filter(w1, b1, w2, b2, w3, freq):
    hi = lax.Precision.HIGHEST
    Lq = SEQ
    t = jnp.linspace(0.0, 1.0, Lq, dtype=F32)[:, None]
    ang = 2.0 * math.pi * jnp.arange(Lq, dtype=F32)[:, None] / Lq
    bands = jnp.linspace(1e-4, FILT_BANDS - 1, FILT_BANDS, dtype=F32)
    z = jnp.concatenate([t, jnp.cos(bands * ang), -jnp.sin(bands * ang)], axis=-1)
    hid = jnp.sin(freq * (jnp.dot(z, w1, precision=hi) + b1))
    hid = jnp.sin(freq * (jnp.dot(hid, w2, precision=hi) + b2))
    deltas = jnp.linspace(HY_MIN_DECAY, HY_MAX_DECAY, HY_WIDTH, dtype=F32)
    hf = jnp.dot(hid, w3, precision=hi) * jnp.exp(-t * jnp.tile(deltas, 2))
    ssq = jnp.sum(hf * hf, axis=0)
    ssq = ssq[:HY_WIDTH] + ssq[HY_WIDTH:]
    return hf * jnp.tile(lax.rsqrt(ssq + EPS), 2)


def kernel(x, c, ctx, c_ctx, ada_w, ada_b, ev_w_in, mla_q_norm, mla_w_uq, mla_kv_norm, mla_w_ukv, mla_q_qknorm, mla_k_qknorm, s5_lam_re, s5_lam_im, s5_log_step, s5_b_re, s5_b_im, s5_c_re, s5_c_im, s5_d, s5_glu_w, s5_glu_b, ev_w_out, hy_w_in, hy_conv_w, hy_conv_b, hy_f_w1, hy_f_b1, hy_f_w2, hy_f_b2, hy_f_w3, hy_f_freq, hy_bias, hy_w_out, moe_router_w, moe_router_b, moe_w_gate, moe_w_up, moe_w_down, moe_sh_gate, moe_sh_up, moe_sh_down):
    hi = lax.Precision.HIGHEST
    D = D_MODEL
    sc = jax.nn.silu(c)
    sc_ctx = jax.nn.silu(c_ctx)

    def mods(li):
        mod = jnp.dot(sc, ada_w[li], precision=hi) + ada_b[li]
        return [m[:, None, :] for m in jnp.split(mod, 6, axis=-1)]

    sh1, sc1, g1, sh2, sc2, g2 = mods(0)
    mod_ctx = jnp.dot(sc_ctx, ada_w[0][:, :2 * D], precision=hi) + ada_b[0][:2 * D]
    w0 = _a0_weights(ev_w_in[0], mla_q_norm[0], mla_w_uq[0], mla_kv_norm[0], mla_w_ukv[0],
                     mla_q_qknorm[0], mla_k_qknorm[0])
    q, k, v, u = _a0_call(x, ctx, sh1, sc1, mod_ctx[None, :D], mod_ctx[None, D:], w0)
    att = _attn_call(q, k, v)
    ug = u.reshape(BATCH, S5_NCHUNK, S5_CHUNK, S5_GROUPS, S5_GROUP)
    ug = ug.transpose(3, 1, 0, 2, 4).reshape(S5_GROUPS, S5_ROWS, S5_COLS)
    ys = _s5_call(ug, *_s5_weights(s5_lam_re[0], s5_lam_im[0], s5_log_step[0], s5_b_re[0], s5_b_im[0],
                                   s5_c_re[0], s5_c_im[0]))
    ys = ys.reshape(S5_GROUPS, S5_NCHUNK_LAT, BATCH, S5_CHUNK, S5_GROUP)
    ys = ys.transpose(2, 1, 3, 0, 4).reshape(BATCH, SEQ, S5_WIDTH)
    wo = ev_w_out[0].astype(BF16)
    wo_att = jnp.concatenate([wo[:MLA_WIDTH].reshape(MLA_HEADS, V_HEAD, D_MODEL),
                              jnp.zeros((MLA_HEADS, HEAD_PAD - V_HEAD, D_MODEL), BF16)], axis=1)
    x = _a1_call(att, ys, u, s5_d[0][None, :], s5_glu_w[0].astype(BF16), s5_glu_b[0][None, :],
                 wo_att.reshape(MLA_HEADS * HEAD_PAD, D_MODEL), wo[MLA_WIDTH:], x, g1)
    x = _moe(x, sh2, sc2, g2, moe_router_w[0], moe_router_b[0], moe_w_gate, moe_w_up, moe_w_down,
             moe_sh_gate[0], moe_sh_up[0], moe_sh_down[0], 0)

    sh1, sc1, g1, sh2, sc2, g2 = mods(1)
    z, x0 = _hy_in_call(x, sh1, sc1, hy_w_in[0].astype(BF16), hy_conv_w[0], hy_conv_b[0][None, :])
    fft_tabs = _fft_tables()
    hfb = _hyena_filter(hy_f_w1[0], hy_f_b1[0], hy_f_w2[0], hy_f_b2[0], hy_f_w3[0], hy_f_freq[0])
    y = _conv_call(z, _spec_call(hfb, *fft_tabs), *fft_tabs)
    x = _hy_out_call(y, z, x0, hy_bias[0][None, :], hy_w_out[0].astype(BF16), x, g1)
    x = _moe(x, sh2, sc2, g2, moe_router_w[1], moe_router_b[1], moe_w_gate, moe_w_up, moe_w_down,
             moe_sh_gate[1], moe_sh_up[1], moe_sh_down[1], 1)
    return x
```

```python
import functools
import math

import numpy as np
import jax
import jax.numpy as jnp
from jax import lax
from jax.experimental import pallas as pl
from jax.experimental.pallas import tpu as pltpu
from jax.experimental.pallas import tpu_sc as plsc

F32 = jnp.float32
BF16 = jnp.bfloat16

D_MODEL = 1024
BATCH = 8
SEQ = 4096
CTX_LEN = 256
KV_LEN = SEQ + CTX_LEN
GRID_W = 64
EPS = 1e-6

MLA_HEADS = 8
QK_NOPE = 64
QK_ROPE = 32
QK_HEAD = QK_NOPE + QK_ROPE
V_HEAD = 64
Q_LORA = 256
KV_LORA = 128
MLA_WIDTH = MLA_HEADS * V_HEAD
ROPE_BASE = 10000.0
HEAD_PAD = 128

S5_WIDTH = 512
S5_GROUP = 16
S5_GROUPS = S5_WIDTH // S5_GROUP
S5_STATE = 64
S5_CHUNK = 32
S5_NCHUNK = KV_LEN // S5_CHUNK
S5_NCHUNK_LAT = SEQ // S5_CHUNK
S5_NCHUNK_CTX = CTX_LEN // S5_CHUNK

HY_WIDTH = D_MODEL
FILT_EMB = 33
FILT_BANDS = (FILT_EMB - 1) // 2
SHORT_CONV = 3
HY_MIN_DECAY = -math.log(1e-2) / 1.5
HY_MAX_DECAY = -math.log(1e-2) / 0.3
DFT_N = 2 * SEQ
FFT_N1 = 16
FFT_N2 = DFT_N // FFT_N1
FFT_NA = FFT_N1 // 2
FFT_NK = FFT_N1 // 2 + 1
HY_CT = 256

N_EXPERTS = 64
TOP_K = 6
EXPERT_FF = 256
ROUTE_SCALE = 2.5
MOE_BLOCK = 1024
ROW_WORDS = D_MODEL // 4
SC_WINDOW = 128

V7X_VMEM_BYTES = 64 * 1024 * 1024
VMEM_LIMIT = V7X_VMEM_BYTES - 8 * 1024 * 1024


def _params(semantics):
    return pltpu.CompilerParams(dimension_semantics=semantics, vmem_limit_bytes=VMEM_LIMIT)


def _norm_mod(x, shift, scale):
    ms = jnp.mean(x * x, axis=-1, keepdims=True)
    return x * lax.rsqrt(ms + EPS) * (1.0 + scale) + shift


def _rms(x, gain, n):
    ms = jnp.sum(x * x, axis=-1, keepdims=True) * (1.0 / n)
    return x * lax.rsqrt(ms + EPS) * gain


A0_TL = 256
A0_NT = SEQ // A0_TL


def _rope_perm():
    return np.concatenate([np.arange(0, QK_ROPE, 2), np.arange(1, QK_ROPE, 2)])


def _rope_tables():
    t = np.arange(SEQ)
    row = (t // GRID_W).astype(np.float64)
    col = (t % GRID_W).astype(np.float64)
    n_freq = QK_ROPE // 4
    inv = ROPE_BASE ** (-np.arange(n_freq, dtype=np.float64) / n_freq)
    ang = np.concatenate([row[:, None] * inv, col[:, None] * inv], axis=-1)
    cos, sin = np.cos(ang), np.sin(ang)
    half = QK_ROPE // 2
    a = np.zeros((KV_LEN, HEAD_PAD))
    b = np.zeros((KV_LEN, HEAD_PAD))
    a[:, :QK_HEAD] = 1.0
    a[:SEQ, QK_NOPE:QK_NOPE + half] = cos
    a[:SEQ, QK_NOPE + half:QK_HEAD] = cos
    b[:SEQ, QK_NOPE:QK_NOPE + half] = -sin
    b[:SEQ, QK_NOPE + half:QK_HEAD] = sin
    return a, b


def _norm_rope_heads(f, gain_ref, a, b, out_ref):
    width = MLA_HEADS * HEAD_PAD
    for hd in range(MLA_HEADS):
        sl = slice(hd * HEAD_PAD, (hd + 1) * HEAD_PAD)
        x = f[:, sl]
        r = lax.rsqrt(jnp.sum(x * x, axis=-1, keepdims=True) * (1.0 / QK_HEAD) + EPS)
        rot = x * (a * gain_ref[:, sl]) + f[:, width + hd * HEAD_PAD:width + (hd + 1) * HEAD_PAD] * b
        out_ref[0, :, sl] = (rot * r).astype(BF16)


def _a0_kernel(x_ref, ctx_ref, sh_ref, sc_ref, shc_ref, scc_ref, win_ref, qn_ref, wuq_ref, kvn_ref,
               wk_ref, wuv_ref, qg_ref, kg_ref, ka_ref, kb_ref, qa_ref, qb_ref,
               q_ref, k_ref, v_ref, u_ref):
    j = pl.program_id(1)
    is_ctx = j == A0_NT
    xin = jnp.where(is_ctx, ctx_ref[0], x_ref[0])
    shift = jnp.where(is_ctx, shc_ref[...], sh_ref[0])
    scale = jnp.where(is_ctx, scc_ref[...], sc_ref[0])
    h = _norm_mod(xin, shift, scale).astype(BF16)
    proj = jnp.dot(h, win_ref[...], preferred_element_type=F32)
    u_ref[0] = proj[:, 512:].astype(BF16)

    c_kv = _rms(proj[:, Q_LORA:Q_LORA + KV_LORA], kvn_ref[...], KV_LORA).astype(BF16)
    lane = lax.broadcasted_iota(jnp.int32, (1, MLA_HEADS * HEAD_PAD), 1)
    ones_lane = (lane % HEAD_PAD == V_HEAD).astype(F32)
    v_ref[0] = (jnp.dot(c_kv, wuv_ref[...], preferred_element_type=F32) + ones_lane).astype(BF16)
    kin = jnp.concatenate([c_kv, proj[:, 384:512].astype(BF16)], axis=1)
    kf = jnp.dot(kin, wk_ref[...], preferred_element_type=F32)
    _norm_rope_heads(kf, kg_ref, ka_ref[...], kb_ref[...], k_ref)

    @pl.when(j < A0_NT)
    def _():
        ql = _rms(proj[:, :Q_LORA], qn_ref[...], Q_LORA).astype(BF16)
        qf = jnp.dot(ql, wuq_ref[...], preferred_element_type=F32)
        _norm_rope_heads(qf, qg_ref, qa_ref[...], qb_ref[...], q_ref)


def _a0_call(x, ctx, sh, sc, shc, scc, w):
    nt = A0_NT
    lat = lambda b, j: (b, jnp.minimum(j, nt - 1), 0)
    full2 = lambda b, j: (0, 0)
    per_b = lambda b, j: (b, 0, 0)
    tab = pl.BlockSpec((A0_TL, HEAD_PAD), lambda b, j: (j, 0))
    in_specs = [
        pl.BlockSpec((1, A0_TL, D_MODEL), lat),
        pl.BlockSpec((1, CTX_LEN, D_MODEL), per_b),
        pl.BlockSpec((1, 1, D_MODEL), per_b),
        pl.BlockSpec((1, 1, D_MODEL), per_b),
        pl.BlockSpec((1, D_MODEL), full2),
        pl.BlockSpec((1, D_MODEL), full2),
        pl.BlockSpec((D_MODEL, 1024), full2),
        pl.BlockSpec((1, Q_LORA), full2),
        pl.BlockSpec((Q_LORA, 2 * MLA_HEADS * HEAD_PAD), full2),
        pl.BlockSpec((1, KV_LORA), full2),
        pl.BlockSpec((2 * KV_LORA, 2 * MLA_HEADS * HEAD_PAD), full2),
        pl.BlockSpec((KV_LORA, MLA_HEADS * HEAD_PAD), full2),
        pl.BlockSpec((1, MLA_HEADS * HEAD_PAD), full2),
        pl.BlockSpec((1, MLA_HEADS * HEAD_PAD), full2),
        tab, tab, tab, tab,
    ]
    out_specs = [
        pl.BlockSpec((1, A0_TL, MLA_HEADS * HEAD_PAD), lat),
        pl.BlockSpec((1, A0_TL, MLA_HEADS * HEAD_PAD), lambda b, j: (b, j, 0)),
        pl.BlockSpec((1, A0_TL, MLA_HEADS * HEAD_PAD), lambda b, j: (b, j, 0)),
        pl.BlockSpec((1, A0_TL, S5_WIDTH), lambda b, j: (b, j, 0)),
    ]
    out_shape = [
        jax.ShapeDtypeStruct((BATCH, SEQ, MLA_HEADS * HEAD_PAD), BF16),
        jax.ShapeDtypeStruct((BATCH, KV_LEN, MLA_HEADS * HEAD_PAD), BF16),
        jax.ShapeDtypeStruct((BATCH, KV_LEN, MLA_HEADS * HEAD_PAD), BF16),
        jax.ShapeDtypeStruct((BATCH, KV_LEN, S5_WIDTH), BF16),
    ]
    return pl.pallas_call(
        _a0_kernel, grid=(BATCH, nt + 1), in_specs=in_specs, out_specs=out_specs, out_shape=out_shape,
        compiler_params=_params(("parallel", "arbitrary")), name="even_in_proj",
    )(x, ctx, sh, sc, shc, scc, *w)


def _gain_swap(w, gain):
    half = QK_ROPE // 2
    wg = (w * gain).reshape(w.shape[0], MLA_HEADS, HEAD_PAD)
    re, im = wg[..., QK_NOPE:QK_NOPE + half], wg[..., QK_NOPE + half:QK_HEAD]
    out = jnp.concatenate([jnp.zeros_like(wg[..., :QK_NOPE]), im, re, jnp.zeros_like(wg[..., QK_HEAD:])], axis=-1)
    return out.reshape(w.shape)


def _a0_weights(w_in, q_norm, w_uq, kv_norm, w_ukv, q_qk, k_qk):
    perm = _rope_perm()
    kr0 = Q_LORA + KV_LORA
    w_cat = jnp.concatenate([
        w_in[:, :kr0], w_in[:, kr0:kr0 + QK_ROPE][:, perm],
        jnp.zeros((D_MODEL, HEAD_PAD - QK_ROPE), F32), w_in[:, kr0 + QK_ROPE:]], axis=1).astype(BF16)
    pad = HEAD_PAD - QK_HEAD

    def head_gain(g):
        gh = jnp.concatenate([g[:QK_NOPE], g[QK_NOPE:][perm], jnp.zeros((pad,), F32)])
        return jnp.tile(gh, MLA_HEADS)[None, :]

    uq = w_uq.reshape(Q_LORA, MLA_HEADS, QK_HEAD)
    uq = jnp.concatenate([uq[..., :QK_NOPE], uq[..., QK_NOPE:][..., perm],
                          jnp.zeros((Q_LORA, MLA_HEADS, pad), F32)], axis=-1)
    uq = uq.reshape(Q_LORA, MLA_HEADS * HEAD_PAD)
    uq = jnp.concatenate([uq, _gain_swap(uq, head_gain(q_qk))], axis=1).astype(BF16)
    ukv = w_ukv.reshape(KV_LORA, MLA_HEADS, QK_NOPE + V_HEAD)
    uk = jnp.concatenate([ukv[..., :QK_NOPE], jnp.zeros((KV_LORA, MLA_HEADS, HEAD_PAD - QK_NOPE), F32)], axis=-1)
    uk = uk.reshape(KV_LORA, MLA_HEADS * HEAD_PAD)
    place = np.zeros((KV_LORA, MLA_HEADS, HEAD_PAD), np.float32)
    for i in range(QK_ROPE):
        place[i, :, QK_NOPE + i] = 1.0
    wk = jnp.concatenate([uk, jnp.asarray(place.reshape(KV_LORA, MLA_HEADS * HEAD_PAD))], axis=0)
    wk = jnp.concatenate([wk, _gain_swap(wk, head_gain(k_qk))], axis=1).astype(BF16)
    wuv = jnp.concatenate([ukv[..., QK_NOPE:], jnp.zeros((KV_LORA, MLA_HEADS, HEAD_PAD - V_HEAD), F32)], axis=-1)
    wuv = wuv.reshape(KV_LORA, MLA_HEADS * HEAD_PAD).astype(BF16)
    a, b = _rope_tables()
    qs = QK_HEAD ** -0.5 * math.log2(math.e)
    tabs = [jnp.asarray(t, F32) for t in (a, b, a * qs, b * qs)]
    return [w_cat, q_norm[None, :], uq, kv_norm[None, :], wk, wuv, head_gain(q_qk), head_gain(k_qk)] + tabs


ATT_TQ = 256
HEADS_PER_STEP = 4


def _attn_kernel(q_ref, k_ref, v_ref, o_ref):
    for hh in range(HEADS_PER_STEP):
        sl = slice(hh * HEAD_PAD, (hh + 1) * HEAD_PAD)
        s = lax.dot_general(q_ref[0, :, sl], k_ref[0, :, sl], (((1,), (1,)), ((), ())),
                            preferred_element_type=F32)
        m = jnp.max(s, axis=-1, keepdims=True)
        p = jnp.exp2(s - m).astype(BF16)
        acc = jnp.dot(p, v_ref[0, :, sl], preferred_element_type=F32)
        o_ref[0, :, sl] = (acc * (1.0 / acc[:, V_HEAD:V_HEAD + 1])).astype(BF16)


def _attn_call(q, k, v):
    wq = HEADS_PER_STEP * HEAD_PAD
    return pl.pallas_call(
        _attn_kernel, grid=(BATCH, MLA_HEADS // HEADS_PER_STEP, SEQ // ATT_TQ),
        in_specs=[pl.BlockSpec((1, ATT_TQ, wq), lambda b, h, i: (b, i, h)),
                  pl.BlockSpec((1, KV_LEN, wq), lambda b, h, i: (b, 0, h)),
                  pl.BlockSpec((1, KV_LEN, wq), lambda b, h, i: (b, 0, h))],
        out_specs=pl.BlockSpec((1, ATT_TQ, wq), lambda b, h, i: (b, i, h)),
        out_shape=jax.ShapeDtypeStruct((BATCH, SEQ, MLA_HEADS * HEAD_PAD), BF16),
        compiler_params=_params(("parallel", "parallel", "arbitrary")), name="mla_attention",
    )(q, k, v)


S5_ROWS = S5_NCHUNK * BATCH
S5_ROWS_LAT = S5_NCHUNK_LAT * BATCH
S5_COLS = S5_CHUNK * S5_GROUP
S5_SW = 2 * S5_STATE


def _s5_kernel(u_ref, r_ref, mb_ref, mc_ref, coef_ref, y_ref, x_sc, sp_sc, t_sc):
    u = u_ref[0]
    lags = r_ref[0]
    for sg in range(S5_CHUNK):
        off = (S5_CHUNK - 1 - sg) * S5_GROUP
        t_sc[sg * S5_GROUP:(sg + 1) * S5_GROUP, :] = lags[:, off:off + S5_COLS].astype(BF16)
    x_sc[...] = jnp.dot(u, mb_ref[0], preferred_element_type=F32)
    cf = coef_ref[0]
    af, bfm, bfp, ab, bbm, bbp = [cf[i * 8:(i + 1) * 8] for i in range(6)]

    def body(i, carry):
        sf, sfw, sb, sbw = carry
        cfw = jnp.where(i < S5_NCHUNK_CTX, i + S5_NCHUNK_LAT, i - S5_NCHUNK_CTX)
        rf = pl.multiple_of(cfw * BATCH, BATCH)
        rb = pl.multiple_of((S5_NCHUNK - 1 - i) * BATCH, BATCH)
        sp_sc[pl.ds(rf, BATCH), 0:S5_SW] = sf
        sp_sc[pl.ds(rb, BATCH), S5_SW:2 * S5_SW] = sb
        xf = x_sc[pl.ds(rf, BATCH), 0:S5_SW]
        xfw = x_sc[pl.ds(rf, BATCH), S5_SW:2 * S5_SW]
        xb = x_sc[pl.ds(rb, BATCH), 2 * S5_SW:3 * S5_SW]
        xbw = x_sc[pl.ds(rb, BATCH), 3 * S5_SW:4 * S5_SW]
        return (sf * af + sfw * bfm + xf, sfw * af + sf * bfp + xfw,
                sb * ab + sbw * bbm + xb, sbw * ab + sb * bbp + xbw)

    z = jnp.zeros((BATCH, S5_SW), F32)
    lax.fori_loop(0, S5_NCHUNK, body, (z, z, z, z))
    y = jnp.dot(u[:S5_ROWS_LAT], t_sc[...], preferred_element_type=F32)
    y = y + jnp.dot(sp_sc[0:S5_ROWS_LAT, :].astype(BF16), mc_ref[0], preferred_element_type=F32)
    y_ref[0] = y.astype(BF16)


def _s5_call(ug, t, mb, mc, coef):
    g3 = lambda g: (g, 0, 0)
    return pl.pallas_call(
        _s5_kernel, grid=(S5_GROUPS,),
        in_specs=[pl.BlockSpec((1, S5_ROWS, S5_COLS), g3),
                  pl.BlockSpec((1, S5_GROUP, 2 * S5_COLS), g3),
                  pl.BlockSpec((1, S5_COLS, 4 * S5_SW), g3),
                  pl.BlockSpec((1, 2 * S5_SW, S5_COLS), g3),
                  pl.BlockSpec((1, 6 * 8, S5_SW), g3)],
        out_specs=pl.BlockSpec((1, S5_ROWS_LAT, S5_COLS), g3),
        out_shape=jax.ShapeDtypeStruct((S5_GROUPS, S5_ROWS_LAT, S5_COLS), BF16),
        scratch_shapes=[pltpu.VMEM((S5_ROWS, 4 * S5_SW), F32), pltpu.VMEM((S5_ROWS, 2 * S5_SW), F32),
                        pltpu.VMEM((S5_COLS, S5_COLS), BF16)],
        compiler_params=_params(("parallel",)), name="s5_chunked_scan",
    )(ug, t, mb, mc, coef)


def _s5_weights(lam_re, lam_im, log_step, b_re, b_im, c_re, c_im):
    q = S5_CHUNK
    hi = lax.Precision.HIGHEST
    t_blocks, mbs, mcs, coefs = [], [], [], []
    sig = jnp.arange(q)
    for d in range(2):
        lr = jnp.minimum(lam_re[d], -1e-4)
        li = lam_im[d]
        step = jnp.exp(log_step[d])[:, None]
        jj = jnp.arange(q + 1, dtype=F32)[:, None, None]
        mag = jnp.exp(lr * step * jj)
        ph = li * step * jj
        pr, pi = mag * jnp.cos(ph), mag * jnp.sin(ph)
        nr, ni = pr[1] - 1.0, pi[1]
        den = lr * lr + li * li
        fr, fi = (nr * lr + ni * li) / den, (ni * lr - nr * li) / den
        br = fr[..., None] * b_re[d] - fi[..., None] * b_im[d]
        bi = fr[..., None] * b_im[d] + fi[..., None] * b_re[d]
        cr, ci = c_re[d], c_im[d]
        cpr = cr[None] * pr[:, :, None, :] - ci[None] * pi[:, :, None, :]
        cpi = cr[None] * pi[:, :, None, :] + ci[None] * pr[:, :, None, :]
        kern = (jnp.einsum('jghp,gpk->jghk', cpr[:q], br, precision=hi)
                - jnp.einsum('jghp,gpk->jghk', cpi[:q], bi, precision=hi))
        kt = kern.transpose(1, 3, 0, 2)
        zero_slots = jnp.zeros((S5_GROUPS, S5_GROUP, q, S5_GROUP), F32)
        if d == 0:
            t_blocks.append(jnp.concatenate([zero_slots[:, :, :q - 1], kt, zero_slots[:, :, :1]], axis=2))
        else:
            t_blocks.append(jnp.concatenate([kt[:, :, ::-1], zero_slots], axis=2))
        pw = (q - 1 - sig) if d == 0 else sig
        xr = pr[pw][..., None] * br[None] - pi[pw][..., None] * bi[None]
        xi = pr[pw][..., None] * bi[None] + pi[pw][..., None] * br[None]
        xr = xr.transpose(1, 0, 3, 2).reshape(S5_GROUPS, S5_COLS, S5_STATE)
        xi = xi.transpose(1, 0, 3, 2).reshape(S5_GROUPS, S5_COLS, S5_STATE)
        mbs += [xr, xi, xi, xr]
        po = (sig + 1) if d == 0 else (q - sig)
        mr = cpr[po].transpose(1, 3, 0, 2).reshape(S5_GROUPS, S5_STATE, S5_COLS)
        mi = -cpi[po].transpose(1, 3, 0, 2).reshape(S5_GROUPS, S5_STATE, S5_COLS)
        mcs += [mr, mi]
        are, aim = pr[q], pi[q]
        rows = [jnp.concatenate([are, are], -1), jnp.concatenate([-aim, aim], -1), jnp.concatenate([aim, -aim], -1)]
        coefs += [jnp.broadcast_to(r[:, None, :], (S5_GROUPS, 8, S5_SW)) for r in rows]
    t = (t_blocks[0] + t_blocks[1]).reshape(S5_GROUPS, S5_GROUP, 2 * S5_COLS)
    mb = jnp.concatenate(mbs, axis=-1).astype(BF16)
    mc = jnp.concatenate(mcs, axis=1).astype(BF16)
    coef = jnp.concatenate(coefs, axis=1)
    return t, mb, mc, coef


A1_TL = 512


def _a1_kernel(att_ref, ys_ref, u_ref, d_ref, gw_ref, gb_ref, woa_ref, wos_ref, x_ref, g_ref, o_ref):
    y = u_ref[0].astype(F32) * d_ref[...] + ys_ref[0].astype(F32)
    z = jax.nn.gelu(y)
    gate = jax.nn.sigmoid(jnp.dot(z.astype(BF16), gw_ref[...], preferred_element_type=F32) + gb_ref[...])
    s5 = (z * gate).astype(BF16)
    mix = jnp.dot(att_ref[0], woa_ref[...], preferred_element_type=F32)
    mix = mix + jnp.dot(s5, wos_ref[...], preferred_element_type=F32)
    o_ref[0] = x_ref[0] + g_ref[0] * mix


def _a1_call(att, ys, u, d, gw, gb, woa, wos, x, g1):
    t3 = lambda b, j: (b, j, 0)
    full2 = lambda b, j: (0, 0)
    return pl.pallas_call(
        _a1_kernel, grid=(BATCH, SEQ // A1_TL),
        in_specs=[pl.BlockSpec((1, A1_TL, MLA_HEADS * HEAD_PAD), t3),
                  pl.BlockSpec((1, A1_TL, S5_WIDTH), t3),
                  pl.BlockSpec((1, A1_TL, S5_WIDTH), t3),
                  pl.BlockSpec((1, S5_WIDTH), full2),
                  pl.BlockSpec((S5_WIDTH, S5_WIDTH), full2),
                  pl.BlockSpec((1, S5_WIDTH), full2),
                  pl.BlockSpec((MLA_HEADS * HEAD_PAD, D_MODEL), full2),
                  pl.BlockSpec((S5_WIDTH, D_MODEL), full2),
                  pl.BlockSpec((1, A1_TL, D_MODEL), t3),
                  pl.BlockSpec((1, 1, D_MODEL), lambda b, j: (b, 0, 0))],
        out_specs=pl.BlockSpec((1, A1_TL, D_MODEL), t3),
        out_shape=jax.ShapeDtypeStruct((BATCH, SEQ, D_MODEL), F32),
        compiler_params=_params(("parallel", "arbitrary")), name="even_out_proj",
    )(att, ys, u, d, gw, gb, woa, wos, x, g1)


MOE_TL = 512
SLOT_PAD = 8


def _slot_columns(cols, lane):
    out = jnp.zeros(lane.shape, cols[0].dtype)
    for k, col in enumerate(cols):
        out = jnp.where(lane == k, col, out)
    return out[:, :SLOT_PAD]


def _pack_rows(v):
    halves = []
    for p in range(2):
        base = 2 * p * ROW_WORDS
        a = pltpu.bitcast(v[:, base:base + ROW_WORDS].astype(BF16).astype(F32), jnp.uint32)
        b = pltpu.bitcast(v[:, base + ROW_WORDS:base + 2 * ROW_WORDS].astype(BF16).astype(F32), jnp.uint32)
        halves.append((a >> 16) | b)
    return halves


def _unpack_rows(lo, hi):
    out = []
    for w in (lo, hi):
        out.append(pltpu.bitcast(w << 16, F32))
        out.append(pltpu.bitcast(w & jnp.uint32(0xFFFF0000), F32))
    return out


def _moe_in_kernel(x_ref, sh_ref, sc_ref, rw_ref, rb_ref, tri_ref,
                   hlo_ref, hhi_ref, idx_ref, wt_ref, rank_ref, cnt_ref, run_sc):
    @pl.when((pl.program_id(0) == 0) & (pl.program_id(1) == 0))
    def _():
        run_sc[...] = jnp.zeros_like(run_sc)

    h = _norm_mod(x_ref[0], sh_ref[0], sc_ref[0])
    hb = h.astype(BF16)
    h_lo = (h - hb.astype(F32)).astype(BF16)
    rw = rw_ref[...]
    rw_hi = rw.astype(BF16)
    rw_lo = (rw - rw_hi.astype(F32)).astype(BF16)
    logits = jnp.dot(jnp.concatenate([hb, hb, h_lo], axis=1), jnp.concatenate([rw_hi, rw_lo, rw_hi], axis=0),
                     preferred_element_type=F32)
    scores = jax.nn.sigmoid(logits)
    hlo_ref[0], hhi_ref[0] = _pack_rows(h)

    work = scores + rb_ref[...]
    lane = lax.broadcasted_iota(jnp.int32, work.shape, 1).astype(F32)
    hits, ids = [], []
    for _ in range(TOP_K):
        m = jnp.max(work, axis=-1, keepdims=True)
        ik = jnp.min(jnp.where(work == m, lane, float(N_EXPERTS)), axis=-1, keepdims=True)
        hit = lane == ik
        hits.append(hit)
        ids.append(ik)
        work = jnp.where(hit, -jnp.inf, work)
    mask = hits[0]
    for hit in hits[1:]:
        mask = jnp.logical_or(mask, hit)
    maskf = mask.astype(F32)
    before = jnp.dot(tri_ref[...], maskf.astype(BF16), preferred_element_type=F32) + run_sc[...]
    sel = [jnp.sum(jnp.where(hit, scores, 0.0), axis=-1, keepdims=True) for hit in hits]
    denom = sel[0]
    for s in sel[1:]:
        denom = denom + s
    ranks = [jnp.sum(jnp.where(hit, before, 0.0), axis=-1, keepdims=True) for hit in hits]
    lane128 = lax.broadcasted_iota(jnp.int32, (MOE_TL, 128), 1)
    idx_ref[0] = _slot_columns(ids, lane128).astype(jnp.int32)
    wt_ref[0] = _slot_columns([s / denom * ROUTE_SCALE for s in sel], lane128)
    rank_ref[0] = _slot_columns(ranks, lane128).astype(jnp.int32)
    run_sc[...] += jnp.sum(maskf, axis=0, keepdims=True)
    cnt_ref[...] = run_sc[...]


def _moe_in_call(x, sh, sc, rw, rb):
    t3 = lambda b, j: (b, j, 0)
    full2 = lambda b, j: (0, 0)
    per_b = lambda b, j: (b, 0, 0)
    tri = jnp.asarray(np.tril(np.ones((MOE_TL, MOE_TL), np.float32), -1), BF16)
    slot = jax.ShapeDtypeStruct((BATCH, SEQ, SLOT_PAD), jnp.int32)
    return pl.pallas_call(
        _moe_in_kernel, grid=(BATCH, SEQ // MOE_TL),
        in_specs=[pl.BlockSpec((1, MOE_TL, D_MODEL), t3),
                  pl.BlockSpec((1, 1, D_MODEL), per_b),
                  pl.BlockSpec((1, 1, D_MODEL), per_b),
                  pl.BlockSpec((D_MODEL, N_EXPERTS), full2),
                  pl.BlockSpec((1, N_EXPERTS), full2),
                  pl.BlockSpec((MOE_TL, MOE_TL), full2)],
        out_specs=[pl.BlockSpec((1, MOE_TL, ROW_WORDS), t3),
                   pl.BlockSpec((1, MOE_TL, ROW_WORDS), t3),
                   pl.BlockSpec((1, MOE_TL, SLOT_PAD), t3),
                   pl.BlockSpec((1, MOE_TL, SLOT_PAD), t3),
                   pl.BlockSpec((1, MOE_TL, SLOT_PAD), t3),
                   pl.BlockSpec((1, N_EXPERTS), full2)],
        out_shape=[jax.ShapeDtypeStruct((BATCH, SEQ, ROW_WORDS), jnp.uint32),
                   jax.ShapeDtypeStruct((BATCH, SEQ, ROW_WORDS), jnp.uint32),
                   slot,
                   jax.ShapeDtypeStruct((BATCH, SEQ, SLOT_PAD), F32),
                   slot,
                   jax.ShapeDtypeStruct((1, N_EXPERTS), F32)],
        scratch_shapes=[pltpu.VMEM((1, N_EXPERTS), F32)],
        compiler_params=_params(("arbitrary", "arbitrary")), name="moe_router",
    )(x, sh, sc, rw, rb, tri)


def _sc_mesh():
    return plsc.VectorSubcoreMesh(core_axis_name="c", subcore_axis_name="s")


def _sc_dispatch(h_words, dest, n_rows):
    n_tok = h_words.shape[0]

    @pl.kernel(out_type=jax.ShapeDtypeStruct((n_rows, ROW_WORDS), jnp.uint32), mesh=_sc_mesh(), scratch_types=[])
    def scatter_rows(h_hbm, i_hbm, o_hbm):
        def body(h_vmem, i_vmem):
            for k in range(TOP_K):
                pltpu.sync_copy(h_vmem, o_hbm.at[i_vmem.at[k]])

        pltpu.emit_pipeline(
            body, grid=(n_tok // SC_WINDOW,),
            in_specs=[pl.BlockSpec((SC_WINDOW, ROW_WORDS), index_map=lambda i: (i, 0)),
                      pl.BlockSpec((SLOT_PAD, SC_WINDOW), index_map=lambda i: (0, i))],
            out_specs=[],
            core_axis_name=("c", "s"), dimension_semantics=(pltpu.PARALLEL,),
        )(h_hbm, i_hbm)

    return scatter_rows(h_words, dest)


def _sc_collect(y_words, dest):
    n_tok = dest.shape[1]

    @pl.kernel(out_type=jax.ShapeDtypeStruct((TOP_K, n_tok, ROW_WORDS), jnp.uint32), mesh=_sc_mesh(),
               scratch_types=[])
    def gather_rows(y_hbm, i_hbm, o_hbm):
        def body(i_vmem, o_vmem):
            pltpu.sync_copy(y_hbm.at[i_vmem.at[0]], o_vmem.at[0])

        pltpu.emit_pipeline(
            body, grid=(TOP_K, n_tok // SC_WINDOW),
            in_specs=[pl.BlockSpec((1, SC_WINDOW), index_map=lambda k, i: (k, i))],
            out_specs=[pl.BlockSpec((1, SC_WINDOW, ROW_WORDS), index_map=lambda k, i: (k, i, 0))],
            core_axis_name=("c", "s"), dimension_semantics=(pltpu.PARALLEL, pltpu.PARALLEL),
        )(i_hbm, o_hbm)

    return gather_rows(y_words, dest)


def _expert_kernel(be_ref, nv_ref, xlo_ref, xhi_ref, wg_ref, wu_ref, wd_ref, ylo_ref, yhi_ref,
                   wg_sc, wu_sc, wd_sc):
    i = pl.program_id(0)
    nv = nv_ref[i]

    @pl.when(jnp.logical_or(i == 0, be_ref[i] != be_ref[jnp.maximum(i - 1, 0)]))
    def _():
        wg_sc[...] = wg_ref[0, 0].astype(BF16)
        wu_sc[...] = wu_ref[0, 0].astype(BF16)
        wd_sc[...] = wd_ref[0, 0].astype(BF16)

    @pl.when(nv > 0)
    def _():
        parts = _unpack_rows(xlo_ref[...], xhi_ref[...])
        xb = jnp.concatenate([p.astype(BF16) for p in parts], axis=1)
        live = lax.broadcasted_iota(jnp.int32, xb.shape, 0) < nv
        xb = jnp.where(live, xb, jnp.zeros_like(xb))
        hid = jax.nn.silu(jnp.dot(xb, wg_sc[...], preferred_element_type=F32))
        hid = hid * jnp.dot(xb, wu_sc[...], preferred_element_type=F32)
        y = jnp.dot(hid.astype(BF16), wd_sc[...], preferred_element_type=F32)
        ylo_ref[...], yhi_ref[...] = _pack_rows(y)

    @pl.when(nv == 0)
    def _():
        ylo_ref[...] = jnp.zeros_like(ylo_ref)
        yhi_ref[...] = jnp.zeros_like(yhi_ref)


def _expert_call(block_e, n_valid, xlo, xhi, wg, wu, wd, li):
    n_rows = xlo.shape[0]
    n_blocks = n_rows // MOE_BLOCK
    rows = pl.BlockSpec((MOE_BLOCK, ROW_WORDS), lambda i, be, nv: (i, 0))
    grid_spec = pltpu.PrefetchScalarGridSpec(
        num_scalar_prefetch=2, grid=(n_blocks,),
        in_specs=[rows, rows,
                  pl.BlockSpec((1, 1, D_MODEL, EXPERT_FF), lambda i, be, nv: (li, be[i], 0, 0)),
                  pl.BlockSpec((1, 1, D_MODEL, EXPERT_FF), lambda i, be, nv: (li, be[i], 0, 0)),
                  pl.BlockSpec((1, 1, EXPERT_FF, D_MODEL), lambda i, be, nv: (li, be[i], 0, 0))],
        out_specs=[rows, rows],
        scratch_shapes=[pltpu.VMEM((D_MODEL, EXPERT_FF), BF16), pltpu.VMEM((D_MODEL, EXPERT_FF), BF16),
                        pltpu.VMEM((EXPERT_FF, D_MODEL), BF16)])
    out = jax.ShapeDtypeStruct((n_rows, ROW_WORDS), jnp.uint32)
    return pl.pallas_call(
        _expert_kernel, grid_spec=grid_spec, out_shape=[out, out],
        compiler_params=_params(("arbitrary",)), name="moe_experts",
    )(block_e, n_valid, xlo, xhi, wg, wu, wd)


def _combine_kernel(ylo_ref, yhi_ref, w_ref, x_ref, sh_ref, sc_ref, g_ref, sg_ref, su_ref, sd_ref, o_ref):
    hb = _norm_mod(x_ref[0], sh_ref[0], sc_ref[0]).astype(BF16)
    hid = jax.nn.silu(jnp.dot(hb, sg_ref[...].astype(BF16), preferred_element_type=F32))
    hid = hid * jnp.dot(hb, su_ref[...].astype(BF16), preferred_element_type=F32)
    shared = jnp.dot(hid.astype(BF16), sd_ref[...].astype(BF16), preferred_element_type=F32)
    w = w_ref[0]
    acc = [None] * 4
    for k in range(TOP_K):
        wk = w[:, k:k + 1]
        for c, part in enumerate(_unpack_rows(ylo_ref[k], yhi_ref[k])):
            acc[c] = wk * part if acc[c] is None else acc[c] + wk * part
    for c in range(4):
        sl = slice(c * ROW_WORDS, (c + 1) * ROW_WORDS)
        o_ref[0, :, sl] = x_ref[0, :, sl] + g_ref[0, :, sl] * (acc[c] + shared[:, sl])


def _combine_call(ylo, yhi, wts, x, sh, sc, g2, sg, su, sd):
    t3 = lambda b, j: (b, j, 0)
    per_b = lambda b, j: (b, 0, 0)
    full2 = lambda b, j: (0, 0)
    ff = sg.shape[1]
    nt = SEQ // MOE_TL
    rows = pl.BlockSpec((TOP_K, MOE_TL, ROW_WORDS), lambda b, j: (0, b * nt + j, 0))
    return pl.pallas_call(
        _combine_kernel, grid=(BATCH, nt),
        in_specs=[rows, rows,
                  pl.BlockSpec((1, MOE_TL, SLOT_PAD), t3),
                  pl.BlockSpec((1, MOE_TL, D_MODEL), t3),
                  pl.BlockSpec((1, 1, D_MODEL), per_b),
                  pl.BlockSpec((1, 1, D_MODEL), per_b),
                  pl.BlockSpec((1, 1, D_MODEL), per_b),
                  pl.BlockSpec((D_MODEL, ff), full2),
                  pl.BlockSpec((D_MODEL, ff), full2),
                  pl.BlockSpec((ff, D_MODEL), full2)],
        out_specs=pl.BlockSpec((1, MOE_TL, D_MODEL), t3),
        out_shape=jax.ShapeDtypeStruct((BATCH, SEQ, D_MODEL), F32),
        compiler_params=_params(("parallel", "arbitrary")), name="moe_combine_shared",
    )(ylo, yhi, wts, x, sh, sc, g2, sg, su, sd)


def _moe(x, sh, sc, g2, router_w, router_b, w_gate, w_up, w_down, sh_gate, sh_up, sh_down, li):
    T = BATCH * SEQ
    TK = T * TOP_K
    hlo, hhi, idx, wts, rank, counts = _moe_in_call(x, sh, sc, router_w, router_b[None, :])
    counts = counts[0].astype(jnp.int32)
    padded = (counts + MOE_BLOCK - 1) // MOE_BLOCK * MOE_BLOCK
    pad_end = jnp.cumsum(padded)
    pad_start = pad_end - padded
    n_blocks = -(-TK // MOE_BLOCK) + N_EXPERTS
    n_rows = n_blocks * MOE_BLOCK
    block_start = jnp.arange(n_blocks, dtype=jnp.int32) * MOE_BLOCK
    owns = jnp.logical_and(block_start[:, None] >= pad_start[None, :], block_start[:, None] < pad_end[None, :])
    owns = owns.astype(jnp.int32)
    experts = jnp.arange(N_EXPERTS, dtype=jnp.int32)[None, :]
    block_e = jnp.sum(owns * experts, axis=1) + (N_EXPERTS - 1) * (1 - jnp.sum(owns, axis=1))
    n_valid = jnp.sum(owns * (counts[None, :] - (block_start[:, None] - pad_start[None, :])), axis=1)
    n_valid = jnp.clip(n_valid, 0, MOE_BLOCK).astype(jnp.int32)
    dest = (pad_start[idx.reshape(T, SLOT_PAD)] + rank.reshape(T, SLOT_PAD)).T.astype(jnp.int32)
    xlo = _sc_dispatch(hlo.reshape(T, ROW_WORDS), dest, n_rows)
    xhi = _sc_dispatch(hhi.reshape(T, ROW_WORDS), dest, n_rows)
    ylo, yhi = _expert_call(block_e, n_valid, xlo, xhi, w_gate, w_up, w_down, li)
    return _combine_call(_sc_collect(ylo, dest), _sc_collect(yhi, dest), wts, x, sh, sc, g2,
                         sh_gate, sh_up, sh_down)


HY_TL = 512
HALO = 8


def _hy_in_kernel(x_ref, xp_ref, xn_ref, sh_ref, sc_ref, w_ref, cw_ref, cb_ref, z_ref, x0_ref, h_sc):
    j = pl.program_id(1)
    shift, scale = sh_ref[0], sc_ref[0]
    keep_prev = (j > 0).astype(F32)
    keep_next = (j < SEQ // HY_TL - 1).astype(F32)
    h_sc[0:HALO, :] = _norm_mod(xp_ref[0], shift, scale) * keep_prev
    h_sc[HALO:HALO + HY_TL, :] = _norm_mod(x_ref[0], shift, scale)
    h_sc[HALO + HY_TL:, :] = _norm_mod(xn_ref[0], shift, scale) * keep_next
    hcat = h_sc[...].astype(BF16)
    outs = []
    for part in range(3):
        sl = slice(part * HY_WIDTH, (part + 1) * HY_WIDTH)
        p = jnp.dot(hcat, w_ref[:, sl], preferred_element_type=F32)
        o = (p[HALO - 1:HALO - 1 + HY_TL] * cw_ref[0:1, sl] + p[HALO:HALO + HY_TL] * cw_ref[1:2, sl]
             + p[HALO + 1:HALO + 1 + HY_TL] * cw_ref[2:3, sl] + cb_ref[:, sl])
        outs.append(o)
    x0_ref[0] = outs[0].astype(BF16)
    z_ref[0] = (outs[2] * outs[1]).astype(BF16)


def _hy_in_call(x, sh, sc, w, cw, cb):
    nb8 = HY_TL // HALO
    t3 = lambda b, j: (b, j, 0)
    full2 = lambda b, j: (0, 0)
    per_b = lambda b, j: (b, 0, 0)
    return pl.pallas_call(
        _hy_in_kernel, grid=(BATCH, SEQ // HY_TL),
        in_specs=[pl.BlockSpec((1, HY_TL, D_MODEL), t3),
                  pl.BlockSpec((1, HALO, D_MODEL), lambda b, j: (b, jnp.maximum(j * nb8 - 1, 0), 0)),
                  pl.BlockSpec((1, HALO, D_MODEL), lambda b, j: (b, jnp.minimum((j + 1) * nb8, SEQ // HALO - 1), 0)),
                  pl.BlockSpec((1, 1, D_MODEL), per_b),
                  pl.BlockSpec((1, 1, D_MODEL), per_b),
                  pl.BlockSpec((D_MODEL, 3 * HY_WIDTH), full2),
                  pl.BlockSpec((SHORT_CONV, 3 * HY_WIDTH), full2),
                  pl.BlockSpec((1, 3 * HY_WIDTH), full2)],
        out_specs=[pl.BlockSpec((1, HY_TL, HY_WIDTH), t3), pl.BlockSpec((1, HY_TL, HY_WIDTH), t3)],
        out_shape=[jax.ShapeDtypeStruct((BATCH, SEQ, HY_WIDTH), BF16),
                   jax.ShapeDtypeStruct((BATCH, SEQ, HY_WIDTH), BF16)],
        scratch_shapes=[pltpu.VMEM((HY_TL + 2 * HALO, D_MODEL), F32)],
        compiler_params=_params(("parallel", "arbitrary")), name="hyena_in_proj",
    )(x, x, x, sh, sc, w, cw, cb)


def _fft_tables():
    c = np.arange(FFT_N2, dtype=np.int64)
    ang = 2.0 * np.pi * ((c[:, None] * c[None, :]) % FFT_N2) / FFT_N2
    sr, si = np.cos(ang), -np.sin(ang)
    m = np.block([[sr, -si], [si, sr]])
    k1 = np.arange(FFT_NK, dtype=np.int64)
    ang_t = 2.0 * np.pi * (k1[:, None] * c[None, :]) / DFT_N
    lanes = np.ones((1, 1, 128))
    tr = np.cos(ang_t)[:, :, None] * lanes
    ti = -np.sin(ang_t)[:, :, None] * lanes
    return jnp.asarray(m, BF16), jnp.asarray(tr, F32), jnp.asarray(ti, F32)


def _lin(acc, coef, val):
    if abs(coef) < 1e-12:
        return acc
    term = val if coef == 1.0 else (-val if coef == -1.0 else coef * val)
    return term if acc is None else acc + term


def _twiddle(tr_ref, ti_ref, k1, width):
    reps = width // 128
    tr, ti = tr_ref[k1], ti_ref[k1]
    return jnp.concatenate([tr] * reps, axis=1), jnp.concatenate([ti] * reps, axis=1)


def _class_spectrum(block, k1, m_ref, tr_ref, ti_ref, width):
    yr = yi = None
    for a in range(FFT_NA):
        th = 2.0 * math.pi * ((a * k1) % FFT_N1) / FFT_N1
        za = block(a)
        yr = _lin(yr, round(math.cos(th), 15), za)
        yi = _lin(yi, round(-math.sin(th), 15), za)
    if k1 > 0:
        tr, ti = _twiddle(tr_ref, ti_ref, k1, width)
        yr, yi = (yr * tr, yr * ti) if yi is None else (yr * tr - yi * ti, yr * ti + yi * tr)
    if yi is None:
        x = jnp.dot(m_ref[:, :FFT_N2], yr.astype(BF16), preferred_element_type=F32)
    else:
        x = jnp.dot(m_ref[...], jnp.concatenate([yr, yi], axis=0).astype(BF16), preferred_element_type=F32)
    return x[:FFT_N2], x[FFT_N2:]


def _class_inverse(yr, yi, k1, m_ref, tr_ref, ti_ref, acc_ref, width):
    v = jnp.dot(m_ref[...], jnp.concatenate([yr, -yi], axis=0).astype(BF16), preferred_element_type=F32)
    ur, ui = v[:FFT_N2], -v[FFT_N2:]
    if k1 > 0:
        tr, ti = _twiddle(tr_ref, ti_ref, k1, width)
        ur, ui = ur * tr + ui * ti, ui * tr - ur * ti
    scale = (1.0 if k1 in (0, FFT_N1 // 2) else 2.0) / DFT_N
    for a in range(FFT_NA):
        th = 2.0 * math.pi * ((a * k1) % FFT_N1) / FFT_N1
        term = _lin(None, round(math.cos(th), 15) * scale, ur)
        term = _lin(term, round(-math.sin(th), 15) * scale, ui)
        rows = slice(a * FFT_N2, (a + 1) * FFT_N2)
        if k1 == 0:
            acc_ref[rows, :] = term
        else:
            acc_ref[rows, :] += term


def _spec_kernel(hf_ref, hb_ref, m_ref, tr_ref, ti_ref, c_ref):
    for k1 in range(FFT_NK):
        fr, fi = _class_spectrum(lambda a: hf_ref[a * FFT_N2:(a + 1) * FFT_N2, :], k1, m_ref, tr_ref, ti_ref, HY_CT)
        br, bi = _class_spectrum(lambda a: hb_ref[a * FFT_N2:(a + 1) * FFT_N2, :], k1, m_ref, tr_ref, ti_ref, HY_CT)
        c_ref[k1, :FFT_N2, :] = (fr + br).astype(BF16)
        c_ref[k1, FFT_N2:, :] = (fi - bi).astype(BF16)


def _fft_table_specs(ngrid):
    z = (0,) * 2
    z3 = (0,) * 3
    if ngrid == 1:
        return [pl.BlockSpec((2 * FFT_N2, 2 * FFT_N2), lambda c: z),
                pl.BlockSpec((FFT_NK, FFT_N2, 128), lambda c: z3),
                pl.BlockSpec((FFT_NK, FFT_N2, 128), lambda c: z3)]
    return [pl.BlockSpec((2 * FFT_N2, 2 * FFT_N2), lambda b, c: z),
            pl.BlockSpec((FFT_NK, FFT_N2, 128), lambda b, c: z3),
            pl.BlockSpec((FFT_NK, FFT_N2, 128), lambda b, c: z3)]


def _spec_call(hfb, m, tr, ti):
    nct = HY_WIDTH // HY_CT
    return pl.pallas_call(
        _spec_kernel, grid=(nct,),
        in_specs=[pl.BlockSpec((SEQ, HY_CT), lambda c: (0, c)),
                  pl.BlockSpec((SEQ, HY_CT), lambda c: (0, c + nct))] + _fft_table_specs(1),
        out_specs=pl.BlockSpec((FFT_NK, 2 * FFT_N2, HY_CT), lambda c: (0, 0, c)),
        out_shape=jax.ShapeDtypeStruct((FFT_NK, 2 * FFT_N2, HY_WIDTH), BF16),
        compiler_params=_params(("arbitrary",)), name="hyena_filter_spectrum",
    )(hfb, hfb, m, tr, ti)


def _conv_kernel(z_ref, c_ref, m_ref, tr_ref, ti_ref, y_ref, acc):
    for k1 in range(FFT_NK):
        xr, xi = _class_spectrum(lambda a: z_ref[0, a * FFT_N2:(a + 1) * FFT_N2, :].astype(F32), k1,
                                 m_ref, tr_ref, ti_ref, HY_CT)
        cr = c_ref[k1, :FFT_N2, :].astype(F32)
        ci = c_ref[k1, FFT_N2:, :].astype(F32)
        _class_inverse(xr * cr - xi * ci, xr * ci + xi * cr, k1, m_ref, tr_ref, ti_ref, acc, HY_CT)
    y_ref[0] = acc[...].astype(BF16)


def _conv_call(z, spec, m, tr, ti):
    return pl.pallas_call(
        _conv_kernel, grid=(HY_WIDTH // HY_CT, BATCH),
        in_specs=[pl.BlockSpec((1, SEQ, HY_CT), lambda c, b: (b, 0, c)),
                  pl.BlockSpec((FFT_NK, 2 * FFT_N2, HY_CT), lambda c, b: (0, 0, c))] + _fft_table_specs(2),
        out_specs=pl.BlockSpec((1, SEQ, HY_CT), lambda c, b: (b, 0, c)),
        out_shape=jax.ShapeDtypeStruct((BATCH, SEQ, HY_WIDTH), BF16),
        scratch_shapes=[pltpu.VMEM((SEQ, HY_CT), F32)],
        compiler_params=_params(("parallel", "arbitrary")), name="hyena_long_conv",
    )(z, spec, m, tr, ti)


def _hy_out_kernel(y_ref, z_ref, x0_ref, b_ref, w_ref, x_ref, g_ref, o_ref):
    z = z_ref[0].astype(F32)
    gated = x0_ref[0].astype(F32) * (y_ref[0].astype(F32) + b_ref[...] * z)
    mix = jnp.dot(gated.astype(BF16), w_ref[...], preferred_element_type=F32)
    o_ref[0] = x_ref[0] + g_ref[0] * mix


def _hy_out_call(y, z, x0, bias, w, x, g1):
    t3 = lambda b, j: (b, j, 0)
    full2 = lambda b, j: (0, 0)
    return pl.pallas_call(
        _hy_out_kernel, grid=(BATCH, SEQ // HY_TL),
        in_specs=[pl.BlockSpec((1, HY_TL, HY_WIDTH), t3),
                  pl.BlockSpec((1, HY_TL, HY_WIDTH), t3),
                  pl.BlockSpec((1, HY_TL, HY_WIDTH), t3),
                  pl.BlockSpec((1, HY_WIDTH), full2),
                  pl.BlockSpec((HY_WIDTH, D_MODEL), full2),
                  pl.BlockSpec((1, HY_TL, D_MODEL), t3),
                  pl.BlockSpec((1, 1, D_MODEL), lambda b, j: (b, 0, 0))],
        out_specs=pl.BlockSpec((1, HY_TL, D_MODEL), t3),
        out_shape=jax.ShapeDtypeStruct((BATCH, SEQ, D_MODEL), F32),
        compiler_params=_params(("parallel", "arbitrary")), name="hyena_out_proj",
    )(y, z, x0, bias, w, x, g1)


def _hyena_filter(w1, b1, w2, b2, w3, freq):
    hi = lax.Precision.HIGHEST
    Lq = SEQ
    t = jnp.linspace(0.0, 1.0, Lq, dtype=F32)[:, None]
    ang = 2.0 * math.pi * jnp.arange(Lq, dtype=F32)[:, None] / Lq
    bands = jnp.linspace(1e-4, FILT_BANDS - 1, FILT_BANDS, dtype=F32)
    z = jnp.concatenate([t, jnp.cos(bands * ang), -jnp.sin(bands * ang)], axis=-1)
    hid = jnp.sin(freq * (jnp.dot(z, w1, precision=hi) + b1))
    hid = jnp.sin(freq * (jnp.dot(hid, w2, precision=hi) + b2))
    deltas = jnp.linspace(HY_MIN_DECAY, HY_MAX_DECAY, HY_WIDTH, dtype=F32)
    hf = jnp.dot(hid, w3, precision=hi) * jnp.exp(-t * jnp.tile(deltas, 2))
    ssq = jnp.sum(hf * hf, axis=0)
    ssq = ssq[:HY_WIDTH] + ssq[HY_WIDTH:]
    return hf * jnp.tile(lax.rsqrt(ssq + EPS), 2)


def kernel(x, c, ctx, c_ctx, ada_w, ada_b, ev_w_in, mla_q_norm, mla_w_uq, mla_kv_norm, mla_w_ukv, mla_q_qknorm, mla_k_qknorm, s5_lam_re, s5_lam_im, s5_log_step, s5_b_re, s5_b_im, s5_c_re, s5_c_im, s5_d, s5_glu_w, s5_glu_b, ev_w_out, hy_w_in, hy_conv_w, hy_conv_b, hy_f_w1, hy_f_b1, hy_f_w2, hy_f_b2, hy_f_w3, hy_f_freq, hy_bias, hy_w_out, moe_router_w, moe_router_b, moe_w_gate, moe_w_up, moe_w_down, moe_sh_gate, moe_sh_up, moe_sh_down):
    hi = lax.Precision.HIGHEST
    D = D_MODEL
    sc = jax.nn.silu(c)
    sc_ctx = jax.nn.silu(c_ctx)

    def mods(li):
        mod = jnp.dot(sc, ada_w[li], precision=hi) + ada_b[li]
        return [m[:, None, :] for m in jnp.split(mod, 6, axis=-1)]

    sh1, sc1, g1, sh2, sc2, g2 = mods(0)
    mod_ctx = jnp.dot(sc_ctx, ada_w[0][:, :2 * D], precision=hi) + ada_b[0][:2 * D]
    w0 = _a0_weights(ev_w_in[0], mla_q_norm[0], mla_w_uq[0], mla_kv_norm[0], mla_w_ukv[0],
                     mla_q_qknorm[0], mla_k_qknorm[0])
    q, k, v, u = _a0_call(x, ctx, sh1, sc1, mod_ctx[None, :D], mod_ctx[None, D:], w0)
    att = _attn_call(q, k, v)
    ug = u.reshape(BATCH, S5_NCHUNK, S5_CHUNK, S5_GROUPS, S5_GROUP)
    ug = ug.transpose(3, 1, 0, 2, 4).reshape(S5_GROUPS, S5_ROWS, S5_COLS)
    ys = _s5_call(ug, *_s5_weights(s5_lam_re[0], s5_lam_im[0], s5_log_step[0], s5_b_re[0], s5_b_im[0],
                                   s5_c_re[0], s5_c_im[0]))
    ys = ys.reshape(S5_GROUPS, S5_NCHUNK_LAT, BATCH, S5_CHUNK, S5_GROUP)
    ys = ys.transpose(2, 1, 3, 0, 4).reshape(BATCH, SEQ, S5_WIDTH)
    wo = ev_w_out[0].astype(BF16)
    wo_att = jnp.concatenate([wo[:MLA_WIDTH].reshape(MLA_HEADS, V_HEAD, D_MODEL),
                              jnp.zeros((MLA_HEADS, HEAD_PAD - V_HEAD, D_MODEL), BF16)], axis=1)
    x = _a1_call(att, ys, u, s5_d[0][None, :], s5_glu_w[0].astype(BF16), s5_glu_b[0][None, :],
                 wo_att.reshape(MLA_HEADS * HEAD_PAD, D_MODEL), wo[MLA_WIDTH:], x, g1)
    x = _moe(x, sh2, sc2, g2, moe_router_w[0], moe_router_b[0], moe_w_gate, moe_w_up, moe_w_down,
             moe_sh_gate[0], moe_sh_up[0], moe_sh_down[0], 0)

    sh1, sc1, g1, sh2, sc2, g2 = mods(1)
    z, x0 = _hy_in_call(x, sh1, sc1, hy_w_in[0].astype(BF16), hy_conv_w[0], hy_conv_b[0][None, :])
    fft_tabs = _fft_tables()
    hfb = _hyena_filter(hy_f_w1[0], hy_f_b1[0], hy_f_w2[0], hy_f_b2[0], hy_f_w3[0], hy_f_freq[0])
    y = _conv_call(z, _spec_call(hfb, *fft_tabs), *fft_tabs)
    x = _hy_out_call(y, z, x0, hy_bias[0][None, :], hy_w_out[0].astype(BF16), x, g1)
    x = _moe(x, sh2, sc2, g2, moe_router_w[1], moe_router_b[1], moe_w_gate, moe_w_up, moe_w_down,
             moe_sh_gate[1], moe_sh_up[1], moe_sh_down[1], 1)
    return x
```

```python
import functools
import math

import numpy as np
import jax
import jax.numpy as jnp
from jax import lax
from jax.experimental import pallas as pl
from jax.experimental.pallas import tpu as pltpu
from jax.experimental.pallas import tpu_sc as plsc

F32 = jnp.float32
BF16 = jnp.bfloat16

D_MODEL = 1024
BATCH = 8
SEQ = 4096
CTX_LEN = 256
KV_LEN = SEQ + CTX_LEN
GRID_W = 64
EPS = 1e-6

MLA_HEADS = 8
QK_NOPE = 64
QK_ROPE = 32
QK_HEAD = QK_NOPE + QK_ROPE
V_HEAD = 64
Q_LORA = 256
KV_LORA = 128
MLA_WIDTH = MLA_HEADS * V_HEAD
ROPE_BASE = 10000.0
HEAD_PAD = 128

S5_WIDTH = 512
S5_GROUP = 16
S5_GROUPS = S5_WIDTH // S5_GROUP
S5_STATE = 64
S5_CHUNK = 32
S5_NCHUNK = KV_LEN // S5_CHUNK
S5_NCHUNK_LAT = SEQ // S5_CHUNK
S5_NCHUNK_CTX = CTX_LEN // S5_CHUNK

HY_WIDTH = D_MODEL
FILT_EMB = 33
FILT_BANDS = (FILT_EMB - 1) // 2
SHORT_CONV = 3
HY_MIN_DECAY = -math.log(1e-2) / 1.5
HY_MAX_DECAY = -math.log(1e-2) / 0.3
DFT_N = 2 * SEQ
FFT_N1 = 16
FFT_N2 = DFT_N // FFT_N1
FFT_NA = FFT_N1 // 2
FFT_NK = FFT_N1 // 2 + 1
HY_CT = 256

N_EXPERTS = 64
TOP_K = 6
EXPERT_FF = 256
ROUTE_SCALE = 2.5
MOE_BLOCK = 1024
ROW_WORDS = D_MODEL // 4
SC_WINDOW = 128

V7X_VMEM_BYTES = 64 * 1024 * 1024
VMEM_LIMIT = V7X_VMEM_BYTES - 8 * 1024 * 1024


def _params(semantics):
    return pltpu.CompilerParams(dimension_semantics=semantics, vmem_limit_bytes=VMEM_LIMIT)


def _norm_mod(x, shift, scale):
    ms = jnp.mean(x * x, axis=-1, keepdims=True)
    return x * lax.rsqrt(ms + EPS) * (1.0 + scale) + shift


def _rms(x, gain, n):
    ms = jnp.sum(x * x, axis=-1, keepdims=True) * (1.0 / n)
    return x * lax.rsqrt(ms + EPS) * gain


A0_TL = 256
A0_NT = SEQ // A0_TL


def _rope_perm():
    return np.concatenate([np.arange(0, QK_ROPE, 2), np.arange(1, QK_ROPE, 2)])


def _rope_tables():
    t = np.arange(SEQ)
    row = (t // GRID_W).astype(np.float64)
    col = (t % GRID_W).astype(np.float64)
    n_freq = QK_ROPE // 4
    inv = ROPE_BASE ** (-np.arange(n_freq, dtype=np.float64) / n_freq)
    ang = np.concatenate([row[:, None] * inv, col[:, None] * inv], axis=-1)
    cos, sin = np.cos(ang), np.sin(ang)
    half = QK_ROPE // 2
    a = np.zeros((KV_LEN, HEAD_PAD))
    b = np.zeros((KV_LEN, HEAD_PAD))
    a[:, :QK_HEAD] = 1.0
    a[:SEQ, QK_NOPE:QK_NOPE + half] = cos
    a[:SEQ, QK_NOPE + half:QK_HEAD] = cos
    b[:SEQ, QK_NOPE:QK_NOPE + half] = -sin
    b[:SEQ, QK_NOPE + half:QK_HEAD] = sin
    return a, b


def _norm_rope_heads(f, gain_ref, a, b, out_ref):
    width = MLA_HEADS * HEAD_PAD
    for hd in range(MLA_HEADS):
        sl = slice(hd * HEAD_PAD, (hd + 1) * HEAD_PAD)
        x = f[:, sl]
        r = lax.rsqrt(jnp.sum(x * x, axis=-1, keepdims=True) * (1.0 / QK_HEAD) + EPS)
        rot = x * (a * gain_ref[:, sl]) + f[:, width + hd * HEAD_PAD:width + (hd + 1) * HEAD_PAD] * b
        out_ref[0, :, sl] = (rot * r).astype(BF16)


def _a0_kernel(x_ref, ctx_ref, sh_ref, sc_ref, shc_ref, scc_ref, win_ref, qn_ref, wuq_ref, kvn_ref,
               wk_ref, wuv_ref, qg_ref, kg_ref, ka_ref, kb_ref, qa_ref, qb_ref,
               q_ref, k_ref, v_ref, u_ref):
    j = pl.program_id(1)
    is_ctx = j == A0_NT
    xin = jnp.where(is_ctx, ctx_ref[0], x_ref[0])
    shift = jnp.where(is_ctx, shc_ref[...], sh_ref[0])
    scale = jnp.where(is_ctx, scc_ref[...], sc_ref[0])
    h = _norm_mod(xin, shift, scale).astype(BF16)
    proj = jnp.dot(h, win_ref[...], preferred_element_type=F32)
    u_ref[0] = proj[:, 512:].astype(BF16)

    c_kv = _rms(proj[:, Q_LORA:Q_LORA + KV_LORA], kvn_ref[...], KV_LORA).astype(BF16)
    lane = lax.broadcasted_iota(jnp.int32, (1, MLA_HEADS * HEAD_PAD), 1)
    ones_lane = (lane % HEAD_PAD == V_HEAD).astype(F32)
    v_ref[0] = (jnp.dot(c_kv, wuv_ref[...], preferred_element_type=F32) + ones_lane).astype(BF16)
    kin = jnp.concatenate([c_kv, proj[:, 384:512].astype(BF16)], axis=1)
    kf = jnp.dot(kin, wk_ref[...], preferred_element_type=F32)
    _norm_rope_heads(kf, kg_ref, ka_ref[...], kb_ref[...], k_ref)

    @pl.when(j < A0_NT)
    def _():
        ql = _rms(proj[:, :Q_LORA], qn_ref[...], Q_LORA).astype(BF16)
        qf = jnp.dot(ql, wuq_ref[...], preferred_element_type=F32)
        _norm_rope_heads(qf, qg_ref, qa_ref[...], qb_ref[...], q_ref)


def _a0_call(x, ctx, sh, sc, shc, scc, w):
    nt = A0_NT
    lat = lambda b, j: (b, jnp.minimum(j, nt - 1), 0)
    full2 = lambda b, j: (0, 0)
    per_b = lambda b, j: (b, 0, 0)
    tab = pl.BlockSpec((A0_TL, HEAD_PAD), lambda b, j: (j, 0))
    in_specs = [
        pl.BlockSpec((1, A0_TL, D_MODEL), lat),
        pl.BlockSpec((1, CTX_LEN, D_MODEL), per_b),
        pl.BlockSpec((1, 1, D_MODEL), per_b),
        pl.BlockSpec((1, 1, D_MODEL), per_b),
        pl.BlockSpec((1, D_MODEL), full2),
        pl.BlockSpec((1, D_MODEL), full2),
        pl.BlockSpec((D_MODEL, 1024), full2),
        pl.BlockSpec((1, Q_LORA), full2),
        pl.BlockSpec((Q_LORA, 2 * MLA_HEADS * HEAD_PAD), full2),
        pl.BlockSpec((1, KV_LORA), full2),
        pl.BlockSpec((2 * KV_LORA, 2 * MLA_HEADS * HEAD_PAD), full2),
        pl.BlockSpec((KV_LORA, MLA_HEADS * HEAD_PAD), full2),
        pl.BlockSpec((1, MLA_HEADS * HEAD_PAD), full2),
        pl.BlockSpec((1, MLA_HEADS * HEAD_PAD), full2),
        tab, tab, tab, tab,
    ]
    out_specs = [
        pl.BlockSpec((1, A0_TL, MLA_HEADS * HEAD_PAD), lat),
        pl.BlockSpec((1, A0_TL, MLA_HEADS * HEAD_PAD), lambda b, j: (b, j, 0)),
        pl.BlockSpec((1, A0_TL, MLA_HEADS * HEAD_PAD), lambda b, j: (b, j, 0)),
        pl.BlockSpec((1, A0_TL, S5_WIDTH), lambda b, j: (b, j, 0)),
    ]
    out_shape = [
        jax.ShapeDtypeStruct((BATCH, SEQ, MLA_HEADS * HEAD_PAD), BF16),
        jax.ShapeDtypeStruct((BATCH, KV_LEN, MLA_HEADS * HEAD_PAD), BF16),
        jax.ShapeDtypeStruct((BATCH, KV_LEN, MLA_HEADS * HEAD_PAD), BF16),
        jax.ShapeDtypeStruct((BATCH, KV_LEN, S5_WIDTH), BF16),
    ]
    return pl.pallas_call(
        _a0_kernel, grid=(BATCH, nt + 1), in_specs=in_specs, out_specs=out_specs, out_shape=out_shape,
        compiler_params=_params(("parallel", "arbitrary")), name="even_in_proj",
    )(x, ctx, sh, sc, shc, scc, *w)


def _gain_swap(w, gain):
    half = QK_ROPE // 2
    wg = (w * gain).reshape(w.shape[0], MLA_HEADS, HEAD_PAD)
    re, im = wg[..., QK_NOPE:QK_NOPE + half], wg[..., QK_NOPE + half:QK_HEAD]
    out = jnp.concatenate([jnp.zeros_like(wg[..., :QK_NOPE]), im, re, jnp.zeros_like(wg[..., QK_HEAD:])], axis=-1)
    return out.reshape(w.shape)


def _a0_weights(w_in, q_norm, w_uq, kv_norm, w_ukv, q_qk, k_qk):
    perm = _rope_perm()
    kr0 = Q_LORA + KV_LORA
    w_cat = jnp.concatenate([
        w_in[:, :kr0], w_in[:, kr0:kr0 + QK_ROPE][:, perm],
        jnp.zeros((D_MODEL, HEAD_PAD - QK_ROPE), F32), w_in[:, kr0 + QK_ROPE:]], axis=1).astype(BF16)
    pad = HEAD_PAD - QK_HEAD

    def head_gain(g):
        gh = jnp.concatenate([g[:QK_NOPE], g[QK_NOPE:][perm], jnp.zeros((pad,), F32)])
        return jnp.tile(gh, MLA_HEADS)[None, :]

    uq = w_uq.reshape(Q_LORA, MLA_HEADS, QK_HEAD)
    uq = jnp.concatenate([uq[..., :QK_NOPE], uq[..., QK_NOPE:][..., perm],
                          jnp.zeros((Q_LORA, MLA_HEADS, pad), F32)], axis=-1)
    uq = uq.reshape(Q_LORA, MLA_HEADS * HEAD_PAD)
    uq = jnp.concatenate([uq, _gain_swap(uq, head_gain(q_qk))], axis=1).astype(BF16)
    ukv = w_ukv.reshape(KV_LORA, MLA_HEADS, QK_NOPE + V_HEAD)
    uk = jnp.concatenate([ukv[..., :QK_NOPE], jnp.zeros((KV_LORA, MLA_HEADS, HEAD_PAD - QK_NOPE), F32)], axis=-1)
    uk = uk.reshape(KV_LORA, MLA_HEADS * HEAD_PAD)
    place = np.zeros((KV_LORA, MLA_HEADS, HEAD_PAD), np.float32)
    for i in range(QK_ROPE):
        place[i, :, QK_NOPE + i] = 1.0
    wk = jnp.concatenate([uk, jnp.asarray(place.reshape(KV_LORA, MLA_HEADS * HEAD_PAD))], axis=0)
    wk = jnp.concatenate([wk, _gain_swap(wk, head_gain(k_qk))], axis=1).astype(BF16)
    wuv = jnp.concatenate([ukv[..., QK_NOPE:], jnp.zeros((KV_LORA, MLA_HEADS, HEAD_PAD - V_HEAD), F32)], axis=-1)
    wuv = wuv.reshape(KV_LORA, MLA_HEADS * HEAD_PAD).astype(BF16)
    a, b = _rope_tables()
    qs = QK_HEAD ** -0.5 * math.log2(math.e)
    tabs = [jnp.asarray(t, F32) for t in (a, b, a * qs, b * qs)]
    return [w_cat, q_norm[None, :], uq, kv_norm[None, :], wk, wuv, head_gain(q_qk), head_gain(k_qk)] + tabs


ATT_TQ = 256
HEADS_PER_STEP = 4


def _attn_kernel(q_ref, k_ref, v_ref, o_ref):
    for hh in range(HEADS_PER_STEP):
        sl = slice(hh * HEAD_PAD, (hh + 1) * HEAD_PAD)
        s = lax.dot_general(q_ref[0, :, sl], k_ref[0, :, sl], (((1,), (1,)), ((), ())),
                            preferred_element_type=F32)
        m = jnp.max(s, axis=-1, keepdims=True)
        p = jnp.exp2(s - m).astype(BF16)
        acc = jnp.dot(p, v_ref[0, :, sl], preferred_element_type=F32)
        o_ref[0, :, sl] = (acc * (1.0 / acc[:, V_HEAD:V_HEAD + 1])).astype(BF16)


def _attn_call(q, k, v):
    wq = HEADS_PER_STEP * HEAD_PAD
    return pl.pallas_call(
        _attn_kernel, grid=(BATCH, MLA_HEADS // HEADS_PER_STEP, SEQ // ATT_TQ),
        in_specs=[pl.BlockSpec((1, ATT_TQ, wq), lambda b, h, i: (b, i, h)),
                  pl.BlockSpec((1, KV_LEN, wq), lambda b, h, i: (b, 0, h)),
                  pl.BlockSpec((1, KV_LEN, wq), lambda b, h, i: (b, 0, h))],
        out_specs=pl.BlockSpec((1, ATT_TQ, wq), lambda b, h, i: (b, i, h)),
        out_shape=jax.ShapeDtypeStruct((BATCH, SEQ, MLA_HEADS * HEAD_PAD), BF16),
        compiler_params=_params(("parallel", "parallel", "arbitrary")), name="mla_attention",
    )(q, k, v)


S5_ROWS = S5_NCHUNK * BATCH
S5_ROWS_LAT = S5_NCHUNK_LAT * BATCH
S5_COLS = S5_CHUNK * S5_GROUP
S5_SW = 2 * S5_STATE


def _s5_kernel(u_ref, r_ref, mb_ref, mc_ref, coef_ref, y_ref, x_sc, sp_sc, t_sc):
    u = u_ref[0]
    lags = r_ref[0]
    for sg in range(S5_CHUNK):
        off = (S5_CHUNK - 1 - sg) * S5_GROUP
        t_sc[sg * S5_GROUP:(sg + 1) * S5_GROUP, :] = lags[:, off:off + S5_COLS].astype(BF16)
    x_sc[...] = jnp.dot(u, mb_ref[0], preferred_element_type=F32)
    cf = coef_ref[0]
    af, bfm, bfp, ab, bbm, bbp = [cf[i * 8:(i + 1) * 8] for i in range(6)]

    def body(i, carry):
        sf, sfw, sb, sbw = carry
        cfw = jnp.where(i < S5_NCHUNK_CTX, i + S5_NCHUNK_LAT, i - S5_NCHUNK_CTX)
        rf = pl.multiple_of(cfw * BATCH, BATCH)
        rb = pl.multiple_of((S5_NCHUNK - 1 - i) * BATCH, BATCH)
        sp_sc[pl.ds(rf, BATCH), 0:S5_SW] = sf
        sp_sc[pl.ds(rb, BATCH), S5_SW:2 * S5_SW] = sb
        xf = x_sc[pl.ds(rf, BATCH), 0:S5_SW]
        xfw = x_sc[pl.ds(rf, BATCH), S5_SW:2 * S5_SW]
        xb = x_sc[pl.ds(rb, BATCH), 2 * S5_SW:3 * S5_SW]
        xbw = x_sc[pl.ds(rb, BATCH), 3 * S5_SW:4 * S5_SW]
        return (sf * af + sfw * bfm + xf, sfw * af + sf * bfp + xfw,
                sb * ab + sbw * bbm + xb, sbw * ab + sb * bbp + xbw)

    z = jnp.zeros((BATCH, S5_SW), F32)
    lax.fori_loop(0, S5_NCHUNK, body, (z, z, z, z))
    y = jnp.dot(u[:S5_ROWS_LAT], t_sc[...], preferred_element_type=F32)
    y = y + jnp.dot(sp_sc[0:S5_ROWS_LAT, :].astype(BF16), mc_ref[0], preferred_element_type=F32)
    y_ref[0] = y.astype(BF16)


def _s5_call(ug, t, mb, mc, coef):
    g3 = lambda g: (g, 0, 0)
    return pl.pallas_call(
        _s5_kernel, grid=(S5_GROUPS,),
        in_specs=[pl.BlockSpec((1, S5_ROWS, S5_COLS), g3),
                  pl.BlockSpec((1, S5_GROUP, 2 * S5_COLS), g3),
                  pl.BlockSpec((1, S5_COLS, 4 * S5_SW), g3),
                  pl.BlockSpec((1, 2 * S5_SW, S5_COLS), g3),
                  pl.BlockSpec((1, 6 * 8, S5_SW), g3)],
        out_specs=pl.BlockSpec((1, S5_ROWS_LAT, S5_COLS), g3),
        out_shape=jax.ShapeDtypeStruct((S5_GROUPS, S5_ROWS_LAT, S5_COLS), BF16),
        scratch_shapes=[pltpu.VMEM((S5_ROWS, 4 * S5_SW), F32), pltpu.VMEM((S5_ROWS, 2 * S5_SW), F32),
                        pltpu.VMEM((S5_COLS, S5_COLS), BF16)],
        compiler_params=_params(("parallel",)), name="s5_chunked_scan",
    )(ug, t, mb, mc, coef)


def _s5_weights(lam_re, lam_im, log_step, b_re, b_im, c_re, c_im):
    q = S5_CHUNK
    hi = lax.Precision.HIGHEST
    t_blocks, mbs, mcs, coefs = [], [], [], []
    sig = jnp.arange(q)
    for d in range(2):
        lr = jnp.minimum(lam_re[d], -1e-4)
        li = lam_im[d]
        step = jnp.exp(log_step[d])[:, None]
        jj = jnp.arange(q + 1, dtype=F32)[:, None, None]
        mag = jnp.exp(lr * step * jj)
        ph = li * step * jj
        pr, pi = mag * jnp.cos(ph), mag * jnp.sin(ph)
        nr, ni = pr[1] - 1.0, pi[1]
        den = lr * lr + li * li
        fr, fi = (nr * lr + ni * li) / den, (ni * lr - nr * li) / den
        br = fr[..., None] * b_re[d] - fi[..., None] * b_im[d]
        bi = fr[..., None] * b_im[d] + fi[..., None] * b_re[d]
        cr, ci = c_re[d], c_im[d]
        cpr = cr[None] * pr[:, :, None, :] - ci[None] * pi[:, :, None, :]
        cpi = cr[None] * pi[:, :, None, :] + ci[None] * pr[:, :, None, :]
        kern = jnp.einsum('jghp,gpk->jghk', jnp.concatenate([cpr[:q], -cpi[:q]], axis=-1),
                          jnp.concatenate([br, bi], axis=1), precision=hi)
        kt = kern.transpose(1, 3, 0, 2)
        zero_slots = jnp.zeros((S5_GROUPS, S5_GROUP, q, S5_GROUP), F32)
        if d == 0:
            t_blocks.append(jnp.concatenate([zero_slots[:, :, :q - 1], kt, zero_slots[:, :, :1]], axis=2))
        else:
            t_blocks.append(jnp.concatenate([kt[:, :, ::-1], zero_slots], axis=2))
        pw = (q - 1 - sig) if d == 0 else sig
        xr = pr[pw][..., None] * br[None] - pi[pw][..., None] * bi[None]
        xi = pr[pw][..., None] * bi[None] + pi[pw][..., None] * br[None]
        xr = xr.transpose(1, 0, 3, 2).reshape(S5_GROUPS, S5_COLS, S5_STATE)
        xi = xi.transpose(1, 0, 3, 2).reshape(S5_GROUPS, S5_COLS, S5_STATE)
        mbs += [xr, xi, xi, xr]
        po = (sig + 1) if d == 0 else (q - sig)
        mr = cpr[po].transpose(1, 3, 0, 2).reshape(S5_GROUPS, S5_STATE, S5_COLS)
        mi = -cpi[po].transpose(1, 3, 0, 2).reshape(S5_GROUPS, S5_STATE, S5_COLS)
        mcs += [mr, mi]
        are, aim = pr[q], pi[q]
        rows = [jnp.concatenate([are, are], -1), jnp.concatenate([-aim, aim], -1), jnp.concatenate([aim, -aim], -1)]
        coefs += [jnp.broadcast_to(r[:, None, :], (S5_GROUPS, 8, S5_SW)) for r in rows]
    t = (t_blocks[0] + t_blocks[1]).reshape(S5_GROUPS, S5_GROUP, 2 * S5_COLS)
    mb = jnp.concatenate(mbs, axis=-1).astype(BF16)
    mc = jnp.concatenate(mcs, axis=1).astype(BF16)
    coef = jnp.concatenate(coefs, axis=1)
    return t, mb, mc, coef


A1_TL = 512


def _a1_kernel(att_ref, ys_ref, u_ref, d_ref, gw_ref, gb_ref, woa_ref, wos_ref, x_ref, g_ref, o_ref):
    y = u_ref[0].astype(F32) * d_ref[...] + ys_ref[0].astype(F32)
    z = jax.nn.gelu(y)
    gate = jax.nn.sigmoid(jnp.dot(z.astype(BF16), gw_ref[...], preferred_element_type=F32) + gb_ref[...])
    s5 = (z * gate).astype(BF16)
    mix = jnp.dot(att_ref[0], woa_ref[...], preferred_element_type=F32)
    mix = mix + jnp.dot(s5, wos_ref[...], preferred_element_type=F32)
    o_ref[0] = x_ref[0] + g_ref[0] * mix


def _a1_call(att, ys, u, d, gw, gb, woa, wos, x, g1):
    t3 = lambda b, j: (b, j, 0)
    full2 = lambda b, j: (0, 0)
    return pl.pallas_call(
        _a1_kernel, grid=(BATCH, SEQ // A1_TL),
        in_specs=[pl.BlockSpec((1, A1_TL, MLA_HEADS * HEAD_PAD), t3),
                  pl.BlockSpec((1, A1_TL, S5_WIDTH), t3),
                  pl.BlockSpec((1, A1_TL, S5_WIDTH), t3),
                  pl.BlockSpec((1, S5_WIDTH), full2),
                  pl.BlockSpec((S5_WIDTH, S5_WIDTH), full2),
                  pl.BlockSpec((1, S5_WIDTH), full2),
                  pl.BlockSpec((MLA_HEADS * HEAD_PAD, D_MODEL), full2),
                  pl.BlockSpec((S5_WIDTH, D_MODEL), full2),
                  pl.BlockSpec((1, A1_TL, D_MODEL), t3),
                  pl.BlockSpec((1, 1, D_MODEL), lambda b, j: (b, 0, 0))],
        out_specs=pl.BlockSpec((1, A1_TL, D_MODEL), t3),
        out_shape=jax.ShapeDtypeStruct((BATCH, SEQ, D_MODEL), F32),
        compiler_params=_params(("parallel", "arbitrary")), name="even_out_proj",
    )(att, ys, u, d, gw, gb, woa, wos, x, g1)


MOE_TL = 512
SLOT_PAD = 8


def _pack_rows(v):
    halves = []
    for p in range(2):
        base = 2 * p * ROW_WORDS
        a = pltpu.bitcast(v[:, base:base + ROW_WORDS].astype(BF16).astype(F32), jnp.uint32)
        b = pltpu.bitcast(v[:, base + ROW_WORDS:base + 2 * ROW_WORDS].astype(BF16).astype(F32), jnp.uint32)
        halves.append((a >> 16) | b)
    return halves


def _unpack_rows(lo, hi):
    out = []
    for w in (lo, hi):
        out.append(pltpu.bitcast(w << 16, F32))
        out.append(pltpu.bitcast(w & jnp.uint32(0xFFFF0000), F32))
    return out


def _moe_in_kernel(x_ref, sh_ref, sc_ref, rwt_ref, rb_ref, tri_ref,
                   hlo_ref, hhi_ref, idx_ref, wt_ref, rank_ref, cnt_ref, run_sc):
    @pl.when((pl.program_id(0) == 0) & (pl.program_id(1) == 0))
    def _():
        run_sc[...] = jnp.zeros_like(run_sc)

    h = _norm_mod(x_ref[0], sh_ref[0], sc_ref[0])
    hb = h.astype(BF16)
    hlo_ref[0], hhi_ref[0] = _pack_rows(h)
    h_lo = (h - hb.astype(F32)).astype(BF16)
    rwt = rwt_ref[...]
    rw_hi = rwt.astype(BF16)
    rw_lo = (rwt - rw_hi.astype(F32)).astype(BF16)
    logits = lax.dot_general(jnp.concatenate([rw_hi, rw_lo, rw_hi], axis=1), jnp.concatenate([hb, hb, h_lo], axis=1),
                             (((1,), (1,)), ((), ())), preferred_element_type=F32)
    scores = jax.nn.sigmoid(logits)

    work = scores + rb_ref[...]
    expert = lax.broadcasted_iota(jnp.int32, work.shape, 0).astype(F32)
    hits, ids = [], []
    for _ in range(TOP_K):
        m = jnp.max(work, axis=0, keepdims=True)
        ik = jnp.min(jnp.where(work == m, expert, float(N_EXPERTS)), axis=0, keepdims=True)
        hit = expert == ik
        hits.append(hit)
        ids.append(ik)
        work = jnp.where(hit, -jnp.inf, work)
    mask = hits[0]
    for hit in hits[1:]:
        mask = jnp.logical_or(mask, hit)
    maskf = mask.astype(F32)
    before = jnp.dot(maskf.astype(BF16), tri_ref[...], preferred_element_type=F32) + run_sc[:, 0:1]
    sel = [jnp.sum(jnp.where(hit, scores, 0.0), axis=0, keepdims=True) for hit in hits]
    denom = sel[0]
    for s in sel[1:]:
        denom = denom + s
    ranks = [jnp.sum(jnp.where(hit, before, 0.0), axis=0, keepdims=True) for hit in hits]
    pad = [jnp.zeros_like(denom)] * (SLOT_PAD - TOP_K)
    idx_ref[...] = jnp.concatenate(ids + pad, axis=0).astype(jnp.int32)
    wt_ref[...] = jnp.concatenate([s / denom * ROUTE_SCALE for s in sel] + pad, axis=0)
    rank_ref[...] = jnp.concatenate(ranks + pad, axis=0).astype(jnp.int32)
    run_sc[...] += jnp.sum(maskf, axis=1, keepdims=True)
    cnt_ref[...] = run_sc[...]


def _moe_in_call(x, sh, sc, rw, rb):
    t3 = lambda b, j: (b, j, 0)
    full2 = lambda b, j: (0, 0)
    per_b = lambda b, j: (b, 0, 0)
    nt = SEQ // MOE_TL
    slots = lambda b, j: (0, b * nt + j)
    tri = jnp.asarray(np.triu(np.ones((MOE_TL, MOE_TL), np.float32), 1), BF16)
    slot_i = jax.ShapeDtypeStruct((SLOT_PAD, BATCH * SEQ), jnp.int32)
    return pl.pallas_call(
        _moe_in_kernel, grid=(BATCH, nt),
        in_specs=[pl.BlockSpec((1, MOE_TL, D_MODEL), t3),
                  pl.BlockSpec((1, 1, D_MODEL), per_b),
                  pl.BlockSpec((1, 1, D_MODEL), per_b),
                  pl.BlockSpec((N_EXPERTS, D_MODEL), full2),
                  pl.BlockSpec((N_EXPERTS, 1), full2),
                  pl.BlockSpec((MOE_TL, MOE_TL), full2)],
        out_specs=[pl.BlockSpec((1, MOE_TL, ROW_WORDS), t3),
                   pl.BlockSpec((1, MOE_TL, ROW_WORDS), t3),
                   pl.BlockSpec((SLOT_PAD, MOE_TL), slots),
                   pl.BlockSpec((SLOT_PAD, MOE_TL), slots),
                   pl.BlockSpec((SLOT_PAD, MOE_TL), slots),
                   pl.BlockSpec((N_EXPERTS, 128), full2)],
        out_shape=[jax.ShapeDtypeStruct((BATCH, SEQ, ROW_WORDS), jnp.uint32),
                   jax.ShapeDtypeStruct((BATCH, SEQ, ROW_WORDS), jnp.uint32),
                   slot_i,
                   jax.ShapeDtypeStruct((SLOT_PAD, BATCH * SEQ), F32),
                   slot_i,
                   jax.ShapeDtypeStruct((N_EXPERTS, 128), F32)],
        scratch_shapes=[pltpu.VMEM((N_EXPERTS, 128), F32)],
        compiler_params=_params(("arbitrary", "arbitrary")), name="moe_router",
    )(x, sh, sc, rw.T, rb[:, None], tri)


def _sc_mesh():
    return plsc.VectorSubcoreMesh(core_axis_name="c", subcore_axis_name="s")


def _sc_dispatch(h_words, dest, n_rows):
    n_tok = h_words.shape[0]

    @pl.kernel(out_type=jax.ShapeDtypeStruct((n_rows, ROW_WORDS), jnp.uint32), mesh=_sc_mesh(), scratch_types=[])
    def scatter_rows(h_hbm, i_hbm, o_hbm):
        def body(h_vmem, i_vmem):
            for k in range(TOP_K):
                pltpu.sync_copy(h_vmem, o_hbm.at[i_vmem.at[k]])

        pltpu.emit_pipeline(
            body, grid=(n_tok // SC_WINDOW,),
            in_specs=[pl.BlockSpec((SC_WINDOW, ROW_WORDS), index_map=lambda i: (i, 0)),
                      pl.BlockSpec((SLOT_PAD, SC_WINDOW), index_map=lambda i: (0, i))],
            out_specs=[],
            core_axis_name=("c", "s"), dimension_semantics=(pltpu.PARALLEL,),
        )(h_hbm, i_hbm)

    return scatter_rows(h_words, dest)


def _sc_collect(y_words, dest):
    n_tok = dest.shape[1]

    @pl.kernel(out_type=jax.ShapeDtypeStruct((TOP_K, n_tok, ROW_WORDS), jnp.uint32), mesh=_sc_mesh(),
               scratch_types=[])
    def gather_rows(y_hbm, i_hbm, o_hbm):
        def body(i_vmem, o_vmem):
            pltpu.sync_copy(y_hbm.at[i_vmem.at[0]], o_vmem.at[0])

        pltpu.emit_pipeline(
            body, grid=(TOP_K, n_tok // SC_WINDOW),
            in_specs=[pl.BlockSpec((1, SC_WINDOW), index_map=lambda k, i: (k, i))],
            out_specs=[pl.BlockSpec((1, SC_WINDOW, ROW_WORDS), index_map=lambda k, i: (k, i, 0))],
            core_axis_name=("c", "s"), dimension_semantics=(pltpu.PARALLEL, pltpu.PARALLEL),
        )(i_hbm, o_hbm)

    return gather_rows(y_words, dest)


def _expert_kernel(be_ref, nv_ref, xlo_ref, xhi_ref, wg_ref, wu_ref, wd_ref, ylo_ref, yhi_ref,
                   wg_sc, wu_sc, wd_sc):
    i = pl.program_id(0)
    nv = nv_ref[i]

    @pl.when(jnp.logical_or(i == 0, be_ref[i] != be_ref[jnp.maximum(i - 1, 0)]))
    def _():
        wg_sc[...] = wg_ref[0, 0].astype(BF16)
        wu_sc[...] = wu_ref[0, 0].astype(BF16)
        wd_sc[...] = wd_ref[0, 0].astype(BF16)

    @pl.when(nv > 0)
    def _():
        parts = _unpack_rows(xlo_ref[...], xhi_ref[...])
        xb = jnp.concatenate([p.astype(BF16) for p in parts], axis=1)
        live = lax.broadcasted_iota(jnp.int32, xb.shape, 0) < nv
        xb = jnp.where(live, xb, jnp.zeros_like(xb))
        hid = jax.nn.silu(jnp.dot(xb, wg_sc[...], preferred_element_type=F32))
        hid = hid * jnp.dot(xb, wu_sc[...], preferred_element_type=F32)
        y = jnp.dot(hid.astype(BF16), wd_sc[...], preferred_element_type=F32)
        ylo_ref[...], yhi_ref[...] = _pack_rows(y)

    @pl.when(nv == 0)
    def _():
        ylo_ref[...] = jnp.zeros_like(ylo_ref)
        yhi_ref[...] = jnp.zeros_like(yhi_ref)


def _expert_call(block_e, n_valid, xlo, xhi, wg, wu, wd, li):
    n_rows = xlo.shape[0]
    n_blocks = n_rows // MOE_BLOCK
    rows = pl.BlockSpec((MOE_BLOCK, ROW_WORDS), lambda i, be, nv: (i, 0))
    grid_spec = pltpu.PrefetchScalarGridSpec(
        num_scalar_prefetch=2, grid=(n_blocks,),
        in_specs=[rows, rows,
                  pl.BlockSpec((1, 1, D_MODEL, EXPERT_FF), lambda i, be, nv: (li, be[i], 0, 0)),
                  pl.BlockSpec((1, 1, D_MODEL, EXPERT_FF), lambda i, be, nv: (li, be[i], 0, 0)),
                  pl.BlockSpec((1, 1, EXPERT_FF, D_MODEL), lambda i, be, nv: (li, be[i], 0, 0))],
        out_specs=[rows, rows],
        scratch_shapes=[pltpu.VMEM((D_MODEL, EXPERT_FF), BF16), pltpu.VMEM((D_MODEL, EXPERT_FF), BF16),
                        pltpu.VMEM((EXPERT_FF, D_MODEL), BF16)])
    out = jax.ShapeDtypeStruct((n_rows, ROW_WORDS), jnp.uint32)
    return pl.pallas_call(
        _expert_kernel, grid_spec=grid_spec, out_shape=[out, out],
        compiler_params=_params(("arbitrary",)), name="moe_experts",
    )(block_e, n_valid, xlo, xhi, wg, wu, wd)


def _combine_kernel(ylo_ref, yhi_ref, w_ref, x_ref, sh_ref, sc_ref, g_ref, sg_ref, su_ref, sd_ref, o_ref):
    hb = _norm_mod(x_ref[0], sh_ref[0], sc_ref[0]).astype(BF16)
    hid = jax.nn.silu(jnp.dot(hb, sg_ref[...].astype(BF16), preferred_element_type=F32))
    hid = hid * jnp.dot(hb, su_ref[...].astype(BF16), preferred_element_type=F32)
    shared = jnp.dot(hid.astype(BF16), sd_ref[...].astype(BF16), preferred_element_type=F32)
    w = w_ref[0]
    acc = [None] * 4
    for k in range(TOP_K):
        wk = w[:, k:k + 1]
        for c, part in enumerate(_unpack_rows(ylo_ref[k], yhi_ref[k])):
            acc[c] = wk * part if acc[c] is None else acc[c] + wk * part
    for c in range(4):
        sl = slice(c * ROW_WORDS, (c + 1) * ROW_WORDS)
        o_ref[0, :, sl] = x_ref[0, :, sl] + g_ref[0, :, sl] * (acc[c] + shared[:, sl])


def _combine_call(ylo, yhi, wts, x, sh, sc, g2, sg, su, sd):
    t3 = lambda b, j: (b, j, 0)
    per_b = lambda b, j: (b, 0, 0)
    full2 = lambda b, j: (0, 0)
    ff = sg.shape[1]
    nt = SEQ // MOE_TL
    rows = pl.BlockSpec((TOP_K, MOE_TL, ROW_WORDS), lambda b, j: (0, b * nt + j, 0))
    return pl.pallas_call(
        _combine_kernel, grid=(BATCH, nt),
        in_specs=[rows, rows,
                  pl.BlockSpec((1, MOE_TL, SLOT_PAD), t3),
                  pl.BlockSpec((1, MOE_TL, D_MODEL), t3),
                  pl.BlockSpec((1, 1, D_MODEL), per_b),
                  pl.BlockSpec((1, 1, D_MODEL), per_b),
                  pl.BlockSpec((1, 1, D_MODEL), per_b),
                  pl.BlockSpec((D_MODEL, ff), full2),
                  pl.BlockSpec((D_MODEL, ff), full2),
                  pl.BlockSpec((ff, D_MODEL), full2)],
        out_specs=pl.BlockSpec((1, MOE_TL, D_MODEL), t3),
        out_shape=jax.ShapeDtypeStruct((BATCH, SEQ, D_MODEL), F32),
        compiler_params=_params(("parallel", "arbitrary")), name="moe_combine_shared",
    )(ylo, yhi, wts, x, sh, sc, g2, sg, su, sd)


def _moe(x, sh, sc, g2, router_w, router_b, w_gate, w_up, w_down, sh_gate, sh_up, sh_down, li):
    T = BATCH * SEQ
    TK = T * TOP_K
    hlo, hhi, idx, wts, rank, counts = _moe_in_call(x, sh, sc, router_w, router_b)
    wts = wts.T.reshape(BATCH, SEQ, SLOT_PAD)
    counts = counts[:, 0].astype(jnp.int32)
    padded = (counts + MOE_BLOCK - 1) // MOE_BLOCK * MOE_BLOCK
    pad_end = jnp.cumsum(padded)
    pad_start = pad_end - padded
    n_blocks = -(-TK // MOE_BLOCK) + N_EXPERTS
    n_rows = n_blocks * MOE_BLOCK
    block_start = jnp.arange(n_blocks, dtype=jnp.int32) * MOE_BLOCK
    owns = jnp.logical_and(block_start[:, None] >= pad_start[None, :], block_start[:, None] < pad_end[None, :])
    owns = owns.astype(jnp.int32)
    experts = jnp.arange(N_EXPERTS, dtype=jnp.int32)[None, :]
    block_e = jnp.sum(owns * experts, axis=1) + (N_EXPERTS - 1) * (1 - jnp.sum(owns, axis=1))
    n_valid = jnp.sum(owns * (counts[None, :] - (block_start[:, None] - pad_start[None, :])), axis=1)
    n_valid = jnp.clip(n_valid, 0, MOE_BLOCK).astype(jnp.int32)
    dest = pad_start[idx] + rank
    xlo = _sc_dispatch(hlo.reshape(T, ROW_WORDS), dest, n_rows)
    xhi = _sc_dispatch(hhi.reshape(T, ROW_WORDS), dest, n_rows)
    ylo, yhi = _expert_call(block_e, n_valid, xlo, xhi, w_gate, w_up, w_down, li)
    return _combine_call(_sc_collect(ylo, dest), _sc_collect(yhi, dest), wts, x, sh, sc, g2,
                         sh_gate, sh_up, sh_down)


HY_TL = 512
HALO = 8


def _hy_in_kernel(x_ref, xp_ref, xn_ref, sh_ref, sc_ref, w_ref, cw_ref, cb_ref, z_ref, x0_ref, h_sc):
    j = pl.program_id(1)
    shift, scale = sh_ref[0], sc_ref[0]
    keep_prev = (j > 0).astype(F32)
    keep_next = (j < SEQ // HY_TL - 1).astype(F32)
    h_sc[0:HALO, :] = _norm_mod(xp_ref[0], shift, scale) * keep_prev
    h_sc[HALO:HALO + HY_TL, :] = _norm_mod(x_ref[0], shift, scale)
    h_sc[HALO + HY_TL:, :] = _norm_mod(xn_ref[0], shift, scale) * keep_next
    hcat = h_sc[...].astype(BF16)
    outs = []
    for part in range(3):
        sl = slice(part * HY_WIDTH, (part + 1) * HY_WIDTH)
        p = jnp.dot(hcat, w_ref[:, sl], preferred_element_type=F32)
        o = (p[HALO - 1:HALO - 1 + HY_TL] * cw_ref[0:1, sl] + p[HALO:HALO + HY_TL] * cw_ref[1:2, sl]
             + p[HALO + 1:HALO + 1 + HY_TL] * cw_ref[2:3, sl] + cb_ref[:, sl])
        outs.append(o)
    x0_ref[0] = outs[0].astype(BF16)
    z_ref[0] = (outs[2] * outs[1]).astype(BF16)


def _hy_in_call(x, sh, sc, w, cw, cb):
    nb8 = HY_TL // HALO
    t3 = lambda b, j: (b, j, 0)
    full2 = lambda b, j: (0, 0)
    per_b = lambda b, j: (b, 0, 0)
    return pl.pallas_call(
        _hy_in_kernel, grid=(BATCH, SEQ // HY_TL),
        in_specs=[pl.BlockSpec((1, HY_TL, D_MODEL), t3),
                  pl.BlockSpec((1, HALO, D_MODEL), lambda b, j: (b, jnp.maximum(j * nb8 - 1, 0), 0)),
                  pl.BlockSpec((1, HALO, D_MODEL), lambda b, j: (b, jnp.minimum((j + 1) * nb8, SEQ // HALO - 1), 0)),
                  pl.BlockSpec((1, 1, D_MODEL), per_b),
                  pl.BlockSpec((1, 1, D_MODEL), per_b),
                  pl.BlockSpec((D_MODEL, 3 * HY_WIDTH), full2),
                  pl.BlockSpec((SHORT_CONV, 3 * HY_WIDTH), full2),
                  pl.BlockSpec((1, 3 * HY_WIDTH), full2)],
        out_specs=[pl.BlockSpec((1, HY_TL, HY_WIDTH), t3), pl.BlockSpec((1, HY_TL, HY_WIDTH), t3)],
        out_shape=[jax.ShapeDtypeStruct((BATCH, SEQ, HY_WIDTH), BF16),
                   jax.ShapeDtypeStruct((BATCH, SEQ, HY_WIDTH), BF16)],
        scratch_shapes=[pltpu.VMEM((HY_TL + 2 * HALO, D_MODEL), F32)],
        compiler_params=_params(("parallel", "arbitrary")), name="hyena_in_proj",
    )(x, x, x, sh, sc, w, cw, cb)


def _fft_tables():
    c = np.arange(FFT_N2, dtype=np.int64)
    ang = 2.0 * np.pi * ((c[:, None] * c[None, :]) % FFT_N2) / FFT_N2
    sr, si = np.cos(ang), -np.sin(ang)
    m = np.block([[sr, -si], [si, sr]])
    k1 = np.arange(FFT_NK, dtype=np.int64)
    ang_t = 2.0 * np.pi * (k1[:, None] * c[None, :]) / DFT_N
    lanes = np.ones((1, 1, 128))
    tr = np.cos(ang_t)[:, :, None] * lanes
    ti = -np.sin(ang_t)[:, :, None] * lanes
    return jnp.asarray(m, BF16), jnp.asarray(tr, F32), jnp.asarray(ti, F32)


def _lin(acc, coef, val):
    if abs(coef) < 1e-12:
        return acc
    term = val if coef == 1.0 else (-val if coef == -1.0 else coef * val)
    return term if acc is None else acc + term


def _twiddle(tr_ref, ti_ref, k1, width):
    reps = width // 128
    tr, ti = tr_ref[k1], ti_ref[k1]
    return jnp.concatenate([tr] * reps, axis=1), jnp.concatenate([ti] * reps, axis=1)


def _class_spectrum(block, k1, m_ref, tr_ref, ti_ref, width):
    yr = yi = None
    for a in range(FFT_NA):
        th = 2.0 * math.pi * ((a * k1) % FFT_N1) / FFT_N1
        za = block(a)
        yr = _lin(yr, round(math.cos(th), 15), za)
        yi = _lin(yi, round(-math.sin(th), 15), za)
    if k1 > 0:
        tr, ti = _twiddle(tr_ref, ti_ref, k1, width)
        yr, yi = (yr * tr, yr * ti) if yi is None else (yr * tr - yi * ti, yr * ti + yi * tr)
    if yi is None:
        x = jnp.dot(m_ref[:, :FFT_N2], yr.astype(BF16), preferred_element_type=F32)
    else:
        x = jnp.dot(m_ref[...], jnp.concatenate([yr, yi], axis=0).astype(BF16), preferred_element_type=F32)
    return x[:FFT_N2], x[FFT_N2:]


def _class_inverse(yr, yi, k1, m_ref, tr_ref, ti_ref, acc_ref, width):
    v = jnp.dot(m_ref[...], jnp.concatenate([yr, -yi], axis=0).astype(BF16), preferred_element_type=F32)
    ur, ui = v[:FFT_N2], -v[FFT_N2:]
    if k1 > 0:
        tr, ti = _twiddle(tr_ref, ti_ref, k1, width)
        ur, ui = ur * tr + ui * ti, ui * tr - ur * ti
    scale = (1.0 if k1 in (0, FFT_N1 // 2) else 2.0) / DFT_N
    for a in range(FFT_NA):
        th = 2.0 * math.pi * ((a * k1) % FFT_N1) / FFT_N1
        term = _lin(None, round(math.cos(th), 15) * scale, ur)
        term = _lin(term, round(-math.sin(th), 15) * scale, ui)
        rows = slice(a * FFT_N2, (a + 1) * FFT_N2)
        if k1 == 0:
            acc_ref[rows, :] = term
        else:
            acc_ref[rows, :] += term


def _spec_kernel(hf_ref, hb_ref, m_ref, tr_ref, ti_ref, c_ref):
    for k1 in range(FFT_NK):
        fr, fi = _class_spectrum(lambda a: hf_ref[a * FFT_N2:(a + 1) * FFT_N2, :], k1, m_ref, tr_ref, ti_ref, HY_CT)
        br, bi = _class_spectrum(lambda a: hb_ref[a * FFT_N2:(a + 1) * FFT_N2, :], k1, m_ref, tr_ref, ti_ref, HY_CT)
        c_ref[k1, :FFT_N2, :] = (fr + br).astype(BF16)
        c_ref[k1, FFT_N2:, :] = (fi - bi).astype(BF16)


def _fft_table_specs(ngrid):
    z = (0,) * 2
    z3 = (0,) * 3
    if ngrid == 1:
        return [pl.BlockSpec((2 * FFT_N2, 2 * FFT_N2), lambda c: z),
                pl.BlockSpec((FFT_NK, FFT_N2, 128), lambda c: z3),
                pl.BlockSpec((FFT_NK, FFT_N2, 128), lambda c: z3)]
    return [pl.BlockSpec((2 * FFT_N2, 2 * FFT_N2), lambda b, c: z),
            pl.BlockSpec((FFT_NK, FFT_N2, 128), lambda b, c: z3),
            pl.BlockSpec((FFT_NK, FFT_N2, 128), lambda b, c: z3)]


def _spec_call(hfb, m, tr, ti):
    nct = HY_WIDTH // HY_CT
    return pl.pallas_call(
        _spec_kernel, grid=(nct,),
        in_specs=[pl.BlockSpec((SEQ, HY_CT), lambda c: (0, c)),
                  pl.BlockSpec((SEQ, HY_CT), lambda c: (0, c + nct))] + _fft_table_specs(1),
        out_specs=pl.BlockSpec((FFT_NK, 2 * FFT_N2, HY_CT), lambda c: (0, 0, c)),
        out_shape=jax.ShapeDtypeStruct((FFT_NK, 2 * FFT_N2, HY_WIDTH), BF16),
        compiler_params=_params(("arbitrary",)), name="hyena_filter_spectrum",
    )(hfb, hfb, m, tr, ti)


def _conv_kernel(z_ref, c_ref, m_ref, tr_ref, ti_ref, y_ref, acc):
    for k1 in range(FFT_NK):
        xr, xi = _class_spectrum(lambda a: z_ref[0, a * FFT_N2:(a + 1) * FFT_N2, :].astype(F32), k1,
                                 m_ref, tr_ref, ti_ref, HY_CT)
        cr = c_ref[k1, :FFT_N2, :].astype(F32)
        ci = c_ref[k1, FFT_N2:, :].astype(F32)
        _class_inverse(xr * cr - xi * ci, xr * ci + xi * cr, k1, m_ref, tr_ref, ti_ref, acc, HY_CT)
    y_ref[0] = acc[...].astype(BF16)


def _conv_call(z, spec, m, tr, ti):
    return pl.pallas_call(
        _conv_kernel, grid=(HY_WIDTH // HY_CT, BATCH),
        in_specs=[pl.BlockSpec((1, SEQ, HY_CT), lambda c, b: (b, 0, c)),
                  pl.BlockSpec((FFT_NK, 2 * FFT_N2, HY_CT), lambda c, b: (0, 0, c))] + _fft_table_specs(2),
        out_specs=pl.BlockSpec((1, SEQ, HY_CT), lambda c, b: (b, 0, c)),
        out_shape=jax.ShapeDtypeStruct((BATCH, SEQ, HY_WIDTH), BF16),
        scratch_shapes=[pltpu.VMEM((SEQ, HY_CT), F32)],
        compiler_params=_params(("parallel", "arbitrary")), name="hyena_long_conv",
    )(z, spec, m, tr, ti)


def _hy_out_kernel(y_ref, z_ref, x0_ref, b_ref, w_ref, x_ref, g_ref, o_ref):
    z = z_ref[0].astype(F32)
    gated = x0_ref[0].astype(F32) * (y_ref[0].astype(F32) + b_ref[...] * z)
    mix = jnp.dot(gated.astype(BF16), w_ref[...], preferred_element_type=F32)
    o_ref[0] = x_ref[0] + g_ref[0] * mix


def _hy_out_call(y, z, x0, bias, w, x, g1):
    t3 = lambda b, j: (b, j, 0)
    full2 = lambda b, j: (0, 0)
    return pl.pallas_call(
        _hy_out_kernel, grid=(BATCH, SEQ // HY_TL),
        in_specs=[pl.BlockSpec((1, HY_TL, HY_WIDTH), t3),
                  pl.BlockSpec((1, HY_TL, HY_WIDTH), t3),
                  pl.BlockSpec((1, HY_TL, HY_WIDTH), t3),
                  pl.BlockSpec((1, HY_WIDTH), full2),
                  pl.BlockSpec((HY_WIDTH, D_MODEL), full2),
                  pl.BlockSpec((1, HY_TL, D_MODEL), t3),
                  pl.BlockSpec((1, 1, D_MODEL), lambda b, j: (b, 0, 0))],
        out_specs=pl.BlockSpec((1, HY_TL, D_MODEL), t3),
        out_shape=jax.ShapeDtypeStruct((BATCH, SEQ, D_MODEL), F32),
        compiler_params=_params(("parallel", "arbitrary")), name="hyena_out_proj",
    )(y, z, x0, bias, w, x, g1)


def _hyena_filter(w1, b1, w2, b2, w3, freq):
    hi = lax.Precision.HIGHEST
    Lq = SEQ
    t = jnp.linspace(0.0, 1.0, Lq, dtype=F32)[:, None]
    ang = 2.0 * math.pi * jnp.arange(Lq, dtype=F32)[:, None] / Lq
    bands = jnp.linspace(1e-4, FILT_BANDS - 1, FILT_BANDS, dtype=F32)
    z = jnp.concatenate([t, jnp.cos(bands * ang), -jnp.sin(bands * ang)], axis=-1)
    hid = jnp.sin(freq * (jnp.dot(z, w1, precision=hi) + b1))
    hid = jnp.sin(freq * (jnp.dot(hid, w2, precision=hi) + b2))
    deltas = jnp.linspace(HY_MIN_DECAY, HY_MAX_DECAY, HY_WIDTH, dtype=F32)
    hf = jnp.dot(hid, w3, precision=hi) * jnp.exp(-t * jnp.tile(deltas, 2))
    ssq = jnp.sum(hf * hf, axis=0)
    ssq = ssq[:HY_WIDTH] + ssq[HY_WIDTH:]
    return hf * jnp.tile(lax.rsqrt(ssq + EPS), 2)


def kernel(x, c, ctx, c_ctx, ada_w, ada_b, ev_w_in, mla_q_norm, mla_w_uq, mla_kv_norm, mla_w_ukv, mla_q_qknorm, mla_k_qknorm, s5_lam_re, s5_lam_im, s5_log_step, s5_b_re, s5_b_im, s5_c_re, s5_c_im, s5_d, s5_glu_w, s5_glu_b, ev_w_out, hy_w_in, hy_conv_w, hy_conv_b, hy_f_w1, hy_f_b1, hy_f_w2, hy_f_b2, hy_f_w3, hy_f_freq, hy_bias, hy_w_out, moe_router_w, moe_router_b, moe_w_gate, moe_w_up, moe_w_down, moe_sh_gate, moe_sh_up, moe_sh_down):
    hi = lax.Precision.HIGHEST
    D = D_MODEL
    sc = jax.nn.silu(c)
    sc_ctx = jax.nn.silu(c_ctx)

    def mods(li):
        mod = jnp.dot(sc, ada_w[li], precision=hi) + ada_b[li]
        return [m[:, None, :] for m in jnp.split(mod, 6, axis=-1)]

    sh1, sc1, g1, sh2, sc2, g2 = mods(0)
    mod_ctx = jnp.dot(sc_ctx, ada_w[0][:, :2 * D], precision=hi) + ada_b[0][:2 * D]
    w0 = _a0_weights(ev_w_in[0], mla_q_norm[0], mla_w_uq[0], mla_kv_norm[0], mla_w_ukv[0],
                     mla_q_qknorm[0], mla_k_qknorm[0])
    q, k, v, u = _a0_call(x, ctx, sh1, sc1, mod_ctx[None, :D], mod_ctx[None, D:], w0)
    att = _attn_call(q, k, v)
    ug = u.reshape(BATCH, S5_NCHUNK, S5_CHUNK, S5_GROUPS, S5_GROUP)
    ug = ug.transpose(3, 1, 0, 2, 4).reshape(S5_GROUPS, S5_ROWS, S5_COLS)
    ys = _s5_call(ug, *_s5_weights(s5_lam_re[0], s5_lam_im[0], s5_log_step[0], s5_b_re[0], s5_b_im[0],
                                   s5_c_re[0], s5_c_im[0]))
    ys = ys.reshape(S5_GROUPS, S5_NCHUNK_LAT, BATCH, S5_CHUNK, S5_GROUP)
    ys = ys.transpose(2, 1, 3, 0, 4).reshape(BATCH, SEQ, S5_WIDTH)
    wo = ev_w_out[0].astype(BF16)
    wo_att = jnp.concatenate([wo[:MLA_WIDTH].reshape(MLA_HEADS, V_HEAD, D_MODEL),
                              jnp.zeros((MLA_HEADS, HEAD_PAD - V_HEAD, D_MODEL), BF16)], axis=1)
    x = _a1_call(att, ys, u, s5_d[0][None, :], s5_glu_w[0].astype(BF16), s5_glu_b[0][None, :],
                 wo_att.reshape(MLA_HEADS * HEAD_PAD, D_MODEL), wo[MLA_WIDTH:], x, g1)
    x = _moe(x, sh2, sc2, g2, moe_router_w[0], moe_router_b[0], moe_w_gate, moe_w_up, moe_w_down,
             moe_sh_gate[0], moe_sh_up[0], moe_sh_down[0], 0)

    sh1, sc1, g1, sh2, sc2, g2 = mods(1)
    z, x0 = _hy_in_call(x, sh1, sc1, hy_w_in[0].astype(BF16), hy_conv_w[0], hy_conv_b[0][None, :])
    fft_tabs = _fft_tables()
    hfb = _hyena_filter(hy_f_w1[0], hy_f_b1[0], hy_f_w2[0], hy_f_b2[0], hy_f_w3[0], hy_f_freq[0])
    y = _conv_call(z, _spec_call(hfb, *fft_tabs), *fft_tabs)
    x = _hy_out_call(y, z, x0, hy_bias[0][None, :], hy_w_out[0].astype(BF16), x, g1)
    x = _moe(x, sh2, sc2, g2, moe_router_w[1], moe_router_b[1], moe_w_gate, moe_w_up, moe_w_down,
             moe_sh_gate[1], moe_sh_up[1], moe_sh_down[1], 1)
    return x
```

```python
import functools
import math

import numpy as np
import jax
import jax.numpy as jnp
from jax import lax
from jax.experimental import pallas as pl
from jax.experimental.pallas import tpu as pltpu
from jax.experimental.pallas import tpu_sc as plsc

F32 = jnp.float32
BF16 = jnp.bfloat16

D_MODEL = 1024
BATCH = 8
SEQ = 4096
CTX_LEN = 256
KV_LEN = SEQ + CTX_LEN
GRID_W = 64
EPS = 1e-6

MLA_HEADS = 8
QK_NOPE = 64
QK_ROPE = 32
QK_HEAD = QK_NOPE + QK_ROPE
V_HEAD = 64
Q_LORA = 256
KV_LORA = 128
MLA_WIDTH = MLA_HEADS * V_HEAD
ROPE_BASE = 10000.0
HEAD_PAD = 128

S5_WIDTH = 512
S5_GROUP = 16
S5_GROUPS = S5_WIDTH // S5_GROUP
S5_STATE = 64
S5_CHUNK = 32
S5_NCHUNK = KV_LEN // S5_CHUNK
S5_NCHUNK_LAT = SEQ // S5_CHUNK
S5_NCHUNK_CTX = CTX_LEN // S5_CHUNK

HY_WIDTH = D_MODEL
FILT_EMB = 33
FILT_BANDS = (FILT_EMB - 1) // 2
SHORT_CONV = 3
HY_MIN_DECAY = -math.log(1e-2) / 1.5
HY_MAX_DECAY = -math.log(1e-2) / 0.3
DFT_N = 2 * SEQ
FFT_N1 = 16
FFT_N2 = DFT_N // FFT_N1
FFT_NA = FFT_N1 // 2
FFT_NK = FFT_N1 // 2 + 1
HY_CT = 256

N_EXPERTS = 64
TOP_K = 6
EXPERT_FF = 256
ROUTE_SCALE = 2.5
MOE_BLOCK = 1024
ROW_WORDS = D_MODEL // 4
SC_WINDOW = 128

V7X_VMEM_BYTES = 64 * 1024 * 1024
VMEM_LIMIT = V7X_VMEM_BYTES - 8 * 1024 * 1024


def _params(semantics):
    return pltpu.CompilerParams(dimension_semantics=semantics, vmem_limit_bytes=VMEM_LIMIT)


def _norm_mod(x, shift, scale):
    ms = jnp.mean(x * x, axis=-1, keepdims=True)
    return x * lax.rsqrt(ms + EPS) * (1.0 + scale) + shift


def _rms(x, gain, n):
    ms = jnp.sum(x * x, axis=-1, keepdims=True) * (1.0 / n)
    return x * lax.rsqrt(ms + EPS) * gain


A0_TL = 256
A0_NT = SEQ // A0_TL


def _rope_perm():
    return np.concatenate([np.arange(0, QK_ROPE, 2), np.arange(1, QK_ROPE, 2)])


def _rope_tables():
    t = np.arange(SEQ)
    row = (t // GRID_W).astype(np.float64)
    col = (t % GRID_W).astype(np.float64)
    n_freq = QK_ROPE // 4
    inv = ROPE_BASE ** (-np.arange(n_freq, dtype=np.float64) / n_freq)
    ang = np.concatenate([row[:, None] * inv, col[:, None] * inv], axis=-1)
    cos, sin = np.cos(ang), np.sin(ang)
    half = QK_ROPE // 2
    a = np.zeros((KV_LEN, HEAD_PAD))
    b = np.zeros((KV_LEN, HEAD_PAD))
    a[:, :QK_HEAD] = 1.0
    a[:SEQ, QK_NOPE:QK_NOPE + half] = cos
    a[:SEQ, QK_NOPE + half:QK_HEAD] = cos
    b[:SEQ, QK_NOPE:QK_NOPE + half] = -sin
    b[:SEQ, QK_NOPE + half:QK_HEAD] = sin
    return a, b


def _norm_rope_heads(f, gain_ref, a, b, out_ref):
    width = MLA_HEADS * HEAD_PAD
    for hd in range(MLA_HEADS):
        sl = slice(hd * HEAD_PAD, (hd + 1) * HEAD_PAD)
        x = f[:, sl]
        r = lax.rsqrt(jnp.sum(x * x, axis=-1, keepdims=True) * (1.0 / QK_HEAD) + EPS)
        rot = x * (a * gain_ref[:, sl]) + f[:, width + hd * HEAD_PAD:width + (hd + 1) * HEAD_PAD] * b
        out_ref[0, :, sl] = (rot * r).astype(BF16)


def _a0_kernel(x_ref, ctx_ref, sh_ref, sc_ref, shc_ref, scc_ref, win_ref, qn_ref, wuq_ref, kvn_ref,
               wk_ref, wuv_ref, qg_ref, kg_ref, ka_ref, kb_ref, qa_ref, qb_ref,
               q_ref, k_ref, v_ref, u_ref):
    j = pl.program_id(1)
    is_ctx = j == A0_NT
    xin = jnp.where(is_ctx, ctx_ref[0], x_ref[0])
    shift = jnp.where(is_ctx, shc_ref[...], sh_ref[0])
    scale = jnp.where(is_ctx, scc_ref[...], sc_ref[0])
    h = _norm_mod(xin, shift, scale).astype(BF16)
    proj = jnp.dot(h, win_ref[...], preferred_element_type=F32)
    u_ref[0] = proj[:, 512:].astype(BF16)

    c_kv = _rms(proj[:, Q_LORA:Q_LORA + KV_LORA], kvn_ref[...], KV_LORA).astype(BF16)
    lane = lax.broadcasted_iota(jnp.int32, (1, MLA_HEADS * HEAD_PAD), 1)
    ones_lane = (lane % HEAD_PAD == V_HEAD).astype(F32)
    v_ref[0] = (jnp.dot(c_kv, wuv_ref[...], preferred_element_type=F32) + ones_lane).astype(BF16)
    kin = jnp.concatenate([c_kv, proj[:, 384:512].astype(BF16)], axis=1)
    kf = jnp.dot(kin, wk_ref[...], preferred_element_type=F32)
    _norm_rope_heads(kf, kg_ref, ka_ref[...], kb_ref[...], k_ref)

    @pl.when(j < A0_NT)
    def _():
        ql = _rms(proj[:, :Q_LORA], qn_ref[...], Q_LORA).astype(BF16)
        qf = jnp.dot(ql, wuq_ref[...], preferred_element_type=F32)
        _norm_rope_heads(qf, qg_ref, qa_ref[...], qb_ref[...], q_ref)


def _a0_call(x, ctx, sh, sc, shc, scc, w):
    nt = A0_NT
    lat = lambda b, j: (b, jnp.minimum(j, nt - 1), 0)
    full2 = lambda b, j: (0, 0)
    per_b = lambda b, j: (b, 0, 0)
    tab = pl.BlockSpec((A0_TL, HEAD_PAD), lambda b, j: (j, 0))
    in_specs = [
        pl.BlockSpec((1, A0_TL, D_MODEL), lat),
        pl.BlockSpec((1, CTX_LEN, D_MODEL), per_b),
        pl.BlockSpec((1, 1, D_MODEL), per_b),
        pl.BlockSpec((1, 1, D_MODEL), per_b),
        pl.BlockSpec((1, D_MODEL), full2),
        pl.BlockSpec((1, D_MODEL), full2),
        pl.BlockSpec((D_MODEL, 1024), full2),
        pl.BlockSpec((1, Q_LORA), full2),
        pl.BlockSpec((Q_LORA, 2 * MLA_HEADS * HEAD_PAD), full2),
        pl.BlockSpec((1, KV_LORA), full2),
        pl.BlockSpec((2 * KV_LORA, 2 * MLA_HEADS * HEAD_PAD), full2),
        pl.BlockSpec((KV_LORA, MLA_HEADS * HEAD_PAD), full2),
        pl.BlockSpec((1, MLA_HEADS * HEAD_PAD), full2),
        pl.BlockSpec((1, MLA_HEADS * HEAD_PAD), full2),
        tab, tab, tab, tab,
    ]
    out_specs = [
        pl.BlockSpec((1, A0_TL, MLA_HEADS * HEAD_PAD), lat),
        pl.BlockSpec((1, A0_TL, MLA_HEADS * HEAD_PAD), lambda b, j: (b, j, 0)),
        pl.BlockSpec((1, A0_TL, MLA_HEADS * HEAD_PAD), lambda b, j: (b, j, 0)),
        pl.BlockSpec((1, A0_TL, S5_WIDTH), lambda b, j: (b, j, 0)),
    ]
    out_shape = [
        jax.ShapeDtypeStruct((BATCH, SEQ, MLA_HEADS * HEAD_PAD), BF16),
        jax.ShapeDtypeStruct((BATCH, KV_LEN, MLA_HEADS * HEAD_PAD), BF16),
        jax.ShapeDtypeStruct((BATCH, KV_LEN, MLA_HEADS * HEAD_PAD), BF16),
        jax.ShapeDtypeStruct((BATCH, KV_LEN, S5_WIDTH), BF16),
    ]
    return pl.pallas_call(
        _a0_kernel, grid=(BATCH, nt + 1), in_specs=in_specs, out_specs=out_specs, out_shape=out_shape,
        compiler_params=_params(("parallel", "arbitrary")), name="even_in_proj",
    )(x, ctx, sh, sc, shc, scc, *w)


def _gain_swap(w, gain):
    half = QK_ROPE // 2
    wg = (w * gain).reshape(w.shape[0], MLA_HEADS, HEAD_PAD)
    re, im = wg[..., QK_NOPE:QK_NOPE + half], wg[..., QK_NOPE + half:QK_HEAD]
    out = jnp.concatenate([jnp.zeros_like(wg[..., :QK_NOPE]), im, re, jnp.zeros_like(wg[..., QK_HEAD:])], axis=-1)
    return out.reshape(w.shape)


def _a0_weights(w_in, q_norm, w_uq, kv_norm, w_ukv, q_qk, k_qk):
    perm = _rope_perm()
    kr0 = Q_LORA + KV_LORA
    w_cat = jnp.concatenate([
        w_in[:, :kr0], w_in[:, kr0:kr0 + QK_ROPE][:, perm],
        jnp.zeros((D_MODEL, HEAD_PAD - QK_ROPE), F32), w_in[:, kr0 + QK_ROPE:]], axis=1).astype(BF16)
    pad = HEAD_PAD - QK_HEAD

    def head_gain(g):
        gh = jnp.concatenate([g[:QK_NOPE], g[QK_NOPE:][perm], jnp.zeros((pad,), F32)])
        return jnp.tile(gh, MLA_HEADS)[None, :]

    uq = w_uq.reshape(Q_LORA, MLA_HEADS, QK_HEAD)
    uq = jnp.concatenate([uq[..., :QK_NOPE], uq[..., QK_NOPE:][..., perm],
                          jnp.zeros((Q_LORA, MLA_HEADS, pad), F32)], axis=-1)
    uq = uq.reshape(Q_LORA, MLA_HEADS * HEAD_PAD)
    uq = jnp.concatenate([uq, _gain_swap(uq, head_gain(q_qk))], axis=1).astype(BF16)
    ukv = w_ukv.reshape(KV_LORA, MLA_HEADS, QK_NOPE + V_HEAD)
    uk = jnp.concatenate([ukv[..., :QK_NOPE], jnp.zeros((KV_LORA, MLA_HEADS, HEAD_PAD - QK_NOPE), F32)], axis=-1)
    uk = uk.reshape(KV_LORA, MLA_HEADS * HEAD_PAD)
    place = np.zeros((KV_LORA, MLA_HEADS, HEAD_PAD), np.float32)
    for i in range(QK_ROPE):
        place[i, :, QK_NOPE + i] = 1.0
    wk = jnp.concatenate([uk, jnp.asarray(place.reshape(KV_LORA, MLA_HEADS * HEAD_PAD))], axis=0)
    wk = jnp.concatenate([wk, _gain_swap(wk, head_gain(k_qk))], axis=1).astype(BF16)
    wuv = jnp.concatenate([ukv[..., QK_NOPE:], jnp.zeros((KV_LORA, MLA_HEADS, HEAD_PAD - V_HEAD), F32)], axis=-1)
    wuv = wuv.reshape(KV_LORA, MLA_HEADS * HEAD_PAD).astype(BF16)
    a, b = _rope_tables()
    qs = QK_HEAD ** -0.5 * math.log2(math.e)
    tabs = [jnp.asarray(t, F32) for t in (a, b, a * qs, b * qs)]
    return [w_cat, q_norm[None, :], uq, kv_norm[None, :], wk, wuv, head_gain(q_qk), head_gain(k_qk)] + tabs


ATT_TQ = 256
HEADS_PER_STEP = 4


def _attn_kernel(q_ref, k_ref, v_ref, o_ref):
    for hh in range(HEADS_PER_STEP):
        sl = slice(hh * HEAD_PAD, (hh + 1) * HEAD_PAD)
        s = lax.dot_general(q_ref[0, :, sl], k_ref[0, :, sl], (((1,), (1,)), ((), ())),
                            preferred_element_type=F32)
        m = jnp.max(s, axis=-1, keepdims=True)
        p = jnp.exp2(s - m).astype(BF16)
        acc = jnp.dot(p, v_ref[0, :, sl], preferred_element_type=F32)
        o_ref[0, :, sl] = (acc * (1.0 / acc[:, V_HEAD:V_HEAD + 1])).astype(BF16)


def _attn_call(q, k, v):
    wq = HEADS_PER_STEP * HEAD_PAD
    return pl.pallas_call(
        _attn_kernel, grid=(BATCH, MLA_HEADS // HEADS_PER_STEP, SEQ // ATT_TQ),
        in_specs=[pl.BlockSpec((1, ATT_TQ, wq), lambda b, h, i: (b, i, h)),
                  pl.BlockSpec((1, KV_LEN, wq), lambda b, h, i: (b, 0, h)),
                  pl.BlockSpec((1, KV_LEN, wq), lambda b, h, i: (b, 0, h))],
        out_specs=pl.BlockSpec((1, ATT_TQ, wq), lambda b, h, i: (b, i, h)),
        out_shape=jax.ShapeDtypeStruct((BATCH, SEQ, MLA_HEADS * HEAD_PAD), BF16),
        compiler_params=_params(("parallel", "parallel", "arbitrary")), name="mla_attention",
    )(q, k, v)


S5_ROWS = S5_NCHUNK * BATCH
S5_ROWS_LAT = S5_NCHUNK_LAT * BATCH
S5_COLS = S5_CHUNK * S5_GROUP
S5_SW = 2 * S5_STATE


def _s5_kernel(u_ref, r_ref, mb_ref, mc_ref, coef_ref, y_ref, x_sc, sp_sc, t_sc):
    u = u_ref[0]
    lags = r_ref[0]
    for sg in range(S5_CHUNK):
        off = (S5_CHUNK - 1 - sg) * S5_GROUP
        t_sc[sg * S5_GROUP:(sg + 1) * S5_GROUP, :] = lags[:, off:off + S5_COLS].astype(BF16)
    x_sc[...] = jnp.dot(u, mb_ref[0], preferred_element_type=F32)
    cf = coef_ref[0]
    af, bfm, bfp, ab, bbm, bbp = [cf[i * 8:(i + 1) * 8] for i in range(6)]

    def body(i, carry):
        sf, sfw, sb, sbw = carry
        cfw = jnp.where(i < S5_NCHUNK_CTX, i + S5_NCHUNK_LAT, i - S5_NCHUNK_CTX)
        rf = pl.multiple_of(cfw * BATCH, BATCH)
        rb = pl.multiple_of((S5_NCHUNK - 1 - i) * BATCH, BATCH)
        sp_sc[pl.ds(rf, BATCH), 0:S5_SW] = sf
        sp_sc[pl.ds(rb, BATCH), S5_SW:2 * S5_SW] = sb
        xf = x_sc[pl.ds(rf, BATCH), 0:S5_SW]
        xfw = x_sc[pl.ds(rf, BATCH), S5_SW:2 * S5_SW]
        xb = x_sc[pl.ds(rb, BATCH), 2 * S5_SW:3 * S5_SW]
        xbw = x_sc[pl.ds(rb, BATCH), 3 * S5_SW:4 * S5_SW]
        return (sf * af + sfw * bfm + xf, sfw * af + sf * bfp + xfw,
                sb * ab + sbw * bbm + xb, sbw * ab + sb * bbp + xbw)

    z = jnp.zeros((BATCH, S5_SW), F32)
    lax.fori_loop(0, S5_NCHUNK, body, (z, z, z, z))
    y = jnp.dot(u[:S5_ROWS_LAT], t_sc[...], preferred_element_type=F32)
    y = y + jnp.dot(sp_sc[0:S5_ROWS_LAT, :].astype(BF16), mc_ref[0], preferred_element_type=F32)
    y_ref[0] = y.astype(BF16)


def _s5_call(ug, t, mb, mc, coef):
    g3 = lambda g: (g, 0, 0)
    return pl.pallas_call(
        _s5_kernel, grid=(S5_GROUPS,),
        in_specs=[pl.BlockSpec((1, S5_ROWS, S5_COLS), g3),
                  pl.BlockSpec((1, S5_GROUP, 2 * S5_COLS), g3),
                  pl.BlockSpec((1, S5_COLS, 4 * S5_SW), g3),
                  pl.BlockSpec((1, 2 * S5_SW, S5_COLS), g3),
                  pl.BlockSpec((1, 6 * 8, S5_SW), g3)],
        out_specs=pl.BlockSpec((1, S5_ROWS_LAT, S5_COLS), g3),
        out_shape=jax.ShapeDtypeStruct((S5_GROUPS, S5_ROWS_LAT, S5_COLS), BF16),
        scratch_shapes=[pltpu.VMEM((S5_ROWS, 4 * S5_SW), F32), pltpu.VMEM((S5_ROWS, 2 * S5_SW), F32),
                        pltpu.VMEM((S5_COLS, S5_COLS), BF16)],
        compiler_params=_params(("parallel",)), name="s5_chunked_scan",
    )(ug, t, mb, mc, coef)


def _s5_weights(lam_re, lam_im, log_step, b_re, b_im, c_re, c_im):
    q = S5_CHUNK
    hi = lax.Precision.HIGHEST
    t_blocks, mbs, mcs, coefs = [], [], [], []
    sig = jnp.arange(q)
    for d in range(2):
        lr = jnp.minimum(lam_re[d], -1e-4)
        li = lam_im[d]
        step = jnp.exp(log_step[d])[:, None]
        jj = jnp.arange(q + 1, dtype=F32)[:, None, None]
        mag = jnp.exp(lr * step * jj)
        ph = li * step * jj
        pr, pi = mag * jnp.cos(ph), mag * jnp.sin(ph)
        nr, ni = pr[1] - 1.0, pi[1]
        den = lr * lr + li * li
        fr, fi = (nr * lr + ni * li) / den, (ni * lr - nr * li) / den
        br = fr[..., None] * b_re[d] - fi[..., None] * b_im[d]
        bi = fr[..., None] * b_im[d] + fi[..., None] * b_re[d]
        cr, ci = c_re[d], c_im[d]
        cpr = cr[None] * pr[:, :, None, :] - ci[None] * pi[:, :, None, :]
        cpi = cr[None] * pi[:, :, None, :] + ci[None] * pr[:, :, None, :]
        kern = jnp.einsum('jghp,gpk->jghk', jnp.concatenate([cpr[:q], -cpi[:q]], axis=-1),
                          jnp.concatenate([br, bi], axis=1), precision=hi)
        kt = kern.transpose(1, 3, 0, 2)
        zero_slots = jnp.zeros((S5_GROUPS, S5_GROUP, q, S5_GROUP), F32)
        if d == 0:
            t_blocks.append(jnp.concatenate([zero_slots[:, :, :q - 1], kt, zero_slots[:, :, :1]], axis=2))
        else:
            t_blocks.append(jnp.concatenate([kt[:, :, ::-1], zero_slots], axis=2))
        pw = (q - 1 - sig) if d == 0 else sig
        xr = pr[pw][..., None] * br[None] - pi[pw][..., None] * bi[None]
        xi = pr[pw][..., None] * bi[None] + pi[pw][..., None] * br[None]
        xr = xr.transpose(1, 0, 3, 2).reshape(S5_GROUPS, S5_COLS, S5_STATE)
        xi = xi.transpose(1, 0, 3, 2).reshape(S5_GROUPS, S5_COLS, S5_STATE)
        mbs += [xr, xi, xi, xr]
        po = (sig + 1) if d == 0 else (q - sig)
        mr = cpr[po].transpose(1, 3, 0, 2).reshape(S5_GROUPS, S5_STATE, S5_COLS)
        mi = -cpi[po].transpose(1, 3, 0, 2).reshape(S5_GROUPS, S5_STATE, S5_COLS)
        mcs += [mr, mi]
        are, aim = pr[q], pi[q]
        rows = [jnp.concatenate([are, are], -1), jnp.concatenate([-aim, aim], -1), jnp.concatenate([aim, -aim], -1)]
        coefs += [jnp.broadcast_to(r[:, None, :], (S5_GROUPS, 8, S5_SW)) for r in rows]
    t = (t_blocks[0] + t_blocks[1]).reshape(S5_GROUPS, S5_GROUP, 2 * S5_COLS)
    mb = jnp.concatenate(mbs, axis=-1).astype(BF16)
    mc = jnp.concatenate(mcs, axis=1).astype(BF16)
    coef = jnp.concatenate(coefs, axis=1)
    return t, mb, mc, coef


A1_TL = 512


def _a1_kernel(att_ref, ys_ref, u_ref, d_ref, gw_ref, gb_ref, woa_ref, wos_ref, x_ref, g_ref, o_ref):
    y = u_ref[0].astype(F32) * d_ref[...] + ys_ref[0].astype(F32)
    z = jax.nn.gelu(y)
    gate = jax.nn.sigmoid(jnp.dot(z.astype(BF16), gw_ref[...], preferred_element_type=F32) + gb_ref[...])
    s5 = (z * gate).astype(BF16)
    mix = jnp.dot(att_ref[0], woa_ref[...], preferred_element_type=F32)
    mix = mix + jnp.dot(s5, wos_ref[...], preferred_element_type=F32)
    o_ref[0] = x_ref[0] + g_ref[0] * mix


def _a1_call(att, ys, u, d, gw, gb, woa, wos, x, g1):
    t3 = lambda b, j: (b, j, 0)
    full2 = lambda b, j: (0, 0)
    return pl.pallas_call(
        _a1_kernel, grid=(BATCH, SEQ // A1_TL),
        in_specs=[pl.BlockSpec((1, A1_TL, MLA_HEADS * HEAD_PAD), t3),
                  pl.BlockSpec((1, A1_TL, S5_WIDTH), t3),
                  pl.BlockSpec((1, A1_TL, S5_WIDTH), t3),
                  pl.BlockSpec((1, S5_WIDTH), full2),
                  pl.BlockSpec((S5_WIDTH, S5_WIDTH), full2),
                  pl.BlockSpec((1, S5_WIDTH), full2),
                  pl.BlockSpec((MLA_HEADS * HEAD_PAD, D_MODEL), full2),
                  pl.BlockSpec((S5_WIDTH, D_MODEL), full2),
                  pl.BlockSpec((1, A1_TL, D_MODEL), t3),
                  pl.BlockSpec((1, 1, D_MODEL), lambda b, j: (b, 0, 0))],
        out_specs=pl.BlockSpec((1, A1_TL, D_MODEL), t3),
        out_shape=jax.ShapeDtypeStruct((BATCH, SEQ, D_MODEL), F32),
        compiler_params=_params(("parallel", "arbitrary")), name="even_out_proj",
    )(att, ys, u, d, gw, gb, woa, wos, x, g1)


MOE_TL = 512
SLOT_PAD = 8


def _pack_rows(v):
    halves = []
    for p in range(2):
        base = 2 * p * ROW_WORDS
        a = pltpu.bitcast(v[:, base:base + ROW_WORDS].astype(BF16).astype(F32), jnp.uint32)
        b = pltpu.bitcast(v[:, base + ROW_WORDS:base + 2 * ROW_WORDS].astype(BF16).astype(F32), jnp.uint32)
        halves.append((a >> 16) | b)
    return halves


def _unpack_rows(lo, hi):
    out = []
    for w in (lo, hi):
        out.append(pltpu.bitcast(w << 16, F32))
        out.append(pltpu.bitcast(w & jnp.uint32(0xFFFF0000), F32))
    return out


def _moe_in_kernel(x_ref, sh_ref, sc_ref, rwt_ref, rb_ref, tri_ref,
                   hlo_ref, hhi_ref, idx_ref, wt_ref, rank_ref, cnt_ref, run_sc):
    @pl.when((pl.program_id(0) == 0) & (pl.program_id(1) == 0))
    def _():
        run_sc[...] = jnp.zeros_like(run_sc)

    h = _norm_mod(x_ref[0], sh_ref[0], sc_ref[0])
    hb = h.astype(BF16)
    hlo_ref[0], hhi_ref[0] = _pack_rows(h)
    h_lo = (h - hb.astype(F32)).astype(BF16)
    rwt = rwt_ref[...]
    rw_hi = rwt.astype(BF16)
    rw_lo = (rwt - rw_hi.astype(F32)).astype(BF16)
    logits = lax.dot_general(jnp.concatenate([rw_hi, rw_lo, rw_hi], axis=1), jnp.concatenate([hb, hb, h_lo], axis=1),
                             (((1,), (1,)), ((), ())), preferred_element_type=F32)
    scores = jax.nn.sigmoid(logits)

    work = scores + rb_ref[...]
    expert = lax.broadcasted_iota(jnp.int32, work.shape, 0).astype(F32)
    hits, ids = [], []
    for _ in range(TOP_K):
        m = jnp.max(work, axis=0, keepdims=True)
        ik = jnp.min(jnp.where(work == m, expert, float(N_EXPERTS)), axis=0, keepdims=True)
        hit = expert == ik
        hits.append(hit)
        ids.append(ik)
        work = jnp.where(hit, -jnp.inf, work)
    mask = hits[0]
    for hit in hits[1:]:
        mask = jnp.logical_or(mask, hit)
    maskf = mask.astype(F32)
    before = jnp.dot(maskf.astype(BF16), tri_ref[...], preferred_element_type=F32) + run_sc[:, 0:1]
    sel = [jnp.sum(jnp.where(hit, scores, 0.0), axis=0, keepdims=True) for hit in hits]
    denom = sel[0]
    for s in sel[1:]:
        denom = denom + s
    ranks = [jnp.sum(jnp.where(hit, before, 0.0), axis=0, keepdims=True) for hit in hits]
    pad = [jnp.zeros_like(denom)] * (SLOT_PAD - TOP_K)
    idx_ref[...] = jnp.concatenate(ids + pad, axis=0).astype(jnp.int32)
    wt_ref[...] = jnp.concatenate([s / denom * ROUTE_SCALE for s in sel] + pad, axis=0)
    rank_ref[...] = jnp.concatenate(ranks + pad, axis=0).astype(jnp.int32)
    run_sc[...] += jnp.sum(maskf, axis=1, keepdims=True)
    cnt_ref[...] = run_sc[...]


def _moe_in_call(x, sh, sc, rw, rb):
    t3 = lambda b, j: (b, j, 0)
    full2 = lambda b, j: (0, 0)
    per_b = lambda b, j: (b, 0, 0)
    nt = SEQ // MOE_TL
    slots = lambda b, j: (0, b * nt + j)
    tri = jnp.asarray(np.triu(np.ones((MOE_TL, MOE_TL), np.float32), 1), BF16)
    slot_i = jax.ShapeDtypeStruct((SLOT_PAD, BATCH * SEQ), jnp.int32)
    return pl.pallas_call(
        _moe_in_kernel, grid=(BATCH, nt),
        in_specs=[pl.BlockSpec((1, MOE_TL, D_MODEL), t3),
                  pl.BlockSpec((1, 1, D_MODEL), per_b),
                  pl.BlockSpec((1, 1, D_MODEL), per_b),
                  pl.BlockSpec((N_EXPERTS, D_MODEL), full2),
                  pl.BlockSpec((N_EXPERTS, 1), full2),
                  pl.BlockSpec((MOE_TL, MOE_TL), full2)],
        out_specs=[pl.BlockSpec((1, MOE_TL, ROW_WORDS), t3),
                   pl.BlockSpec((1, MOE_TL, ROW_WORDS), t3),
                   pl.BlockSpec((SLOT_PAD, MOE_TL), slots),
                   pl.BlockSpec((SLOT_PAD, MOE_TL), slots),
                   pl.BlockSpec((SLOT_PAD, MOE_TL), slots),
                   pl.BlockSpec((N_EXPERTS, 128), full2)],
        out_shape=[jax.ShapeDtypeStruct((BATCH, SEQ, ROW_WORDS), jnp.uint32),
                   jax.ShapeDtypeStruct((BATCH, SEQ, ROW_WORDS), jnp.uint32),
                   slot_i,
                   jax.ShapeDtypeStruct((SLOT_PAD, BATCH * SEQ), F32),
                   slot_i,
                   jax.ShapeDtypeStruct((N_EXPERTS, 128), F32)],
        scratch_shapes=[pltpu.VMEM((N_EXPERTS, 128), F32)],
        compiler_params=_params(("arbitrary", "arbitrary")), name="moe_router",
    )(x, sh, sc, rw.T, rb[:, None], tri)


def _sc_mesh():
    return plsc.VectorSubcoreMesh(core_axis_name="c", subcore_axis_name="s")


def _sc_dispatch(h_words, dest, n_rows):
    n_tok = h_words.shape[0]

    @pl.kernel(out_type=jax.ShapeDtypeStruct((n_rows, ROW_WORDS), jnp.uint32), mesh=_sc_mesh(), scratch_types=[])
    def scatter_rows(h_hbm, i_hbm, o_hbm):
        def body(h_vmem, i_vmem):
            for k in range(TOP_K):
                pltpu.sync_copy(h_vmem, o_hbm.at[i_vmem.at[k]])

        pltpu.emit_pipeline(
            body, grid=(n_tok // SC_WINDOW,),
            in_specs=[pl.BlockSpec((SC_WINDOW, ROW_WORDS), index_map=lambda i: (i, 0)),
                      pl.BlockSpec((SLOT_PAD, SC_WINDOW), index_map=lambda i: (0, i))],
            out_specs=[],
            core_axis_name=("c", "s"), dimension_semantics=(pltpu.PARALLEL,),
        )(h_hbm, i_hbm)

    return scatter_rows(h_words, dest)


def _sc_collect(y_words, dest):
    n_tok = dest.shape[1]

    @pl.kernel(out_type=jax.ShapeDtypeStruct((TOP_K, n_tok, ROW_WORDS), jnp.uint32), mesh=_sc_mesh(),
               scratch_types=[])
    def gather_rows(y_hbm, i_hbm, o_hbm):
        def body(i_vmem, o_vmem):
            pltpu.sync_copy(y_hbm.at[i_vmem.at[0]], o_vmem.at[0])

        pltpu.emit_pipeline(
            body, grid=(TOP_K, n_tok // SC_WINDOW),
            in_specs=[pl.BlockSpec((1, SC_WINDOW), index_map=lambda k, i: (k, i))],
            out_specs=[pl.BlockSpec((1, SC_WINDOW, ROW_WORDS), index_map=lambda k, i: (k, i, 0))],
            core_axis_name=("c", "s"), dimension_semantics=(pltpu.PARALLEL, pltpu.PARALLEL),
        )(i_hbm, o_hbm)

    return gather_rows(y_words, dest)


def _expert_kernel(be_ref, nv_ref, xlo_ref, xhi_ref, wg_ref, wu_ref, wd_ref, ylo_ref, yhi_ref,
                   wg_sc, wu_sc, wd_sc):
    i = pl.program_id(0)
    nv = nv_ref[i]

    @pl.when(jnp.logical_or(i == 0, be_ref[i] != be_ref[jnp.maximum(i - 1, 0)]))
    def _():
        wg_sc[...] = wg_ref[0, 0].astype(BF16)
        wu_sc[...] = wu_ref[0, 0].astype(BF16)
        wd_sc[...] = wd_ref[0, 0].astype(BF16)

    @pl.when(nv > 0)
    def _():
        parts = _unpack_rows(xlo_ref[...], xhi_ref[...])
        xb = jnp.concatenate([p.astype(BF16) for p in parts], axis=1)
        live = lax.broadcasted_iota(jnp.int32, xb.shape, 0) < nv
        xb = jnp.where(live, xb, jnp.zeros_like(xb))
        hid = jax.nn.silu(jnp.dot(xb, wg_sc[...], preferred_element_type=F32))
        hid = hid * jnp.dot(xb, wu_sc[...], preferred_element_type=F32)
        y = jnp.dot(hid.astype(BF16), wd_sc[...], preferred_element_type=F32)
        ylo_ref[...], yhi_ref[...] = _pack_rows(y)

    @pl.when(nv == 0)
    def _():
        ylo_ref[...] = jnp.zeros_like(ylo_ref)
        yhi_ref[...] = jnp.zeros_like(yhi_ref)


def _expert_call(block_e, n_valid, xlo, xhi, wg, wu, wd, li):
    n_rows = xlo.shape[0]
    n_blocks = n_rows // MOE_BLOCK
    rows = pl.BlockSpec((MOE_BLOCK, ROW_WORDS), lambda i, be, nv: (i, 0))
    grid_spec = pltpu.PrefetchScalarGridSpec(
        num_scalar_prefetch=2, grid=(n_blocks,),
        in_specs=[rows, rows,
                  pl.BlockSpec((1, 1, D_MODEL, EXPERT_FF), lambda i, be, nv: (li, be[i], 0, 0)),
                  pl.BlockSpec((1, 1, D_MODEL, EXPERT_FF), lambda i, be, nv: (li, be[i], 0, 0)),
                  pl.BlockSpec((1, 1, EXPERT_FF, D_MODEL), lambda i, be, nv: (li, be[i], 0, 0))],
        out_specs=[rows, rows],
        scratch_shapes=[pltpu.VMEM((D_MODEL, EXPERT_FF), BF16), pltpu.VMEM((D_MODEL, EXPERT_FF), BF16),
                        pltpu.VMEM((EXPERT_FF, D_MODEL), BF16)])
    out = jax.ShapeDtypeStruct((n_rows, ROW_WORDS), jnp.uint32)
    return pl.pallas_call(
        _expert_kernel, grid_spec=grid_spec, out_shape=[out, out],
        compiler_params=_params(("arbitrary",)), name="moe_experts",
    )(block_e, n_valid, xlo, xhi, wg, wu, wd)


def _combine_kernel(ylo_ref, yhi_ref, w_ref, x_ref, sh_ref, sc_ref, g_ref, sg_ref, su_ref, sd_ref, o_ref):
    hb = _norm_mod(x_ref[0], sh_ref[0], sc_ref[0]).astype(BF16)
    hid = jax.nn.silu(jnp.dot(hb, sg_ref[...].astype(BF16), preferred_element_type=F32))
    hid = hid * jnp.dot(hb, su_ref[...].astype(BF16), preferred_element_type=F32)
    shared = jnp.dot(hid.astype(BF16), sd_ref[...].astype(BF16), preferred_element_type=F32)
    w = w_ref[0]
    acc = [None] * 4
    for k in range(TOP_K):
        wk = w[:, k:k + 1]
        for c, part in enumerate(_unpack_rows(ylo_ref[k], yhi_ref[k])):
            acc[c] = wk * part if acc[c] is None else acc[c] + wk * part
    for c in range(4):
        sl = slice(c * ROW_WORDS, (c + 1) * ROW_WORDS)
        o_ref[0, :, sl] = x_ref[0, :, sl] + g_ref[0, :, sl] * (acc[c] + shared[:, sl])


def _combine_call(ylo, yhi, wts, x, sh, sc, g2, sg, su, sd):
    t3 = lambda b, j: (b, j, 0)
    per_b = lambda b, j: (b, 0, 0)
    full2 = lambda b, j: (0, 0)
    ff = sg.shape[1]
    nt = SEQ // MOE_TL
    rows = pl.BlockSpec((TOP_K, MOE_TL, ROW_WORDS), lambda b, j: (0, b * nt + j, 0))
    return pl.pallas_call(
        _combine_kernel, grid=(BATCH, nt),
        in_specs=[rows, rows,
                  pl.BlockSpec((1, MOE_TL, SLOT_PAD), t3),
                  pl.BlockSpec((1, MOE_TL, D_MODEL), t3),
                  pl.BlockSpec((1, 1, D_MODEL), per_b),
                  pl.BlockSpec((1, 1, D_MODEL), per_b),
                  pl.BlockSpec((1, 1, D_MODEL), per_b),
                  pl.BlockSpec((D_MODEL, ff), full2),
                  pl.BlockSpec((D_MODEL, ff), full2),
                  pl.BlockSpec((ff, D_MODEL), full2)],
        out_specs=pl.BlockSpec((1, MOE_TL, D_MODEL), t3),
        out_shape=jax.ShapeDtypeStruct((BATCH, SEQ, D_MODEL), F32),
        compiler_params=_params(("parallel", "arbitrary")), name="moe_combine_shared",
    )(ylo, yhi, wts, x, sh, sc, g2, sg, su, sd)


def _moe(x, sh, sc, g2, router_w, router_b, w_gate, w_up, w_down, sh_gate, sh_up, sh_down, li):
    T = BATCH * SEQ
    TK = T * TOP_K
    hlo, hhi, idx, wts, rank, counts = _moe_in_call(x, sh, sc, router_w, router_b)
    wts = wts.T.reshape(BATCH, SEQ, SLOT_PAD)
    counts = counts[:, 0].astype(jnp.int32)
    padded = (counts + MOE_BLOCK - 1) // MOE_BLOCK * MOE_BLOCK
    pad_end = jnp.cumsum(padded)
    pad_start = pad_end - padded
    n_blocks = -(-TK // MOE_BLOCK) + N_EXPERTS
    n_rows = n_blocks * MOE_BLOCK
    block_start = jnp.arange(n_blocks, dtype=jnp.int32) * MOE_BLOCK
    owns = jnp.logical_and(block_start[:, None] >= pad_start[None, :], block_start[:, None] < pad_end[None, :])
    owns = owns.astype(jnp.int32)
    experts = jnp.arange(N_EXPERTS, dtype=jnp.int32)[None, :]
    block_e = jnp.sum(owns * experts, axis=1) + (N_EXPERTS - 1) * (1 - jnp.sum(owns, axis=1))
    n_valid = jnp.sum(owns * (counts[None, :] - (block_start[:, None] - pad_start[None, :])), axis=1)
    n_valid = jnp.clip(n_valid, 0, MOE_BLOCK).astype(jnp.int32)
    first_row = jnp.sum(jnp.where(idx[None] == experts.T[:, :, None], pad_start[:, None, None], 0), axis=0)
    dest = first_row + rank
    xlo = _sc_dispatch(hlo.reshape(T, ROW_WORDS), dest, n_rows)
    xhi = _sc_dispatch(hhi.reshape(T, ROW_WORDS), dest, n_rows)
    ylo, yhi = _expert_call(block_e, n_valid, xlo, xhi, w_gate, w_up, w_down, li)
    return _combine_call(_sc_collect(ylo, dest), _sc_collect(yhi, dest), wts, x, sh, sc, g2,
                         sh_gate, sh_up, sh_down)


HY_TL = 512
HALO = 8


def _hy_in_kernel(x_ref, xp_ref, xn_ref, sh_ref, sc_ref, w_ref, cw_ref, cb_ref, z_ref, x0_ref, h_sc):
    j = pl.program_id(1)
    shift, scale = sh_ref[0], sc_ref[0]
    keep_prev = (j > 0).astype(F32)
    keep_next = (j < SEQ // HY_TL - 1).astype(F32)
    h_sc[0:HALO, :] = _norm_mod(xp_ref[0], shift, scale) * keep_prev
    h_sc[HALO:HALO + HY_TL, :] = _norm_mod(x_ref[0], shift, scale)
    h_sc[HALO + HY_TL:, :] = _norm_mod(xn_ref[0], shift, scale) * keep_next
    hcat = h_sc[...].astype(BF16)
    outs = []
    for part in range(3):
        sl = slice(part * HY_WIDTH, (part + 1) * HY_WIDTH)
        p = jnp.dot(hcat, w_ref[:, sl], preferred_element_type=F32)
        o = (p[HALO - 1:HALO - 1 + HY_TL] * cw_ref[0:1, sl] + p[HALO:HALO + HY_TL] * cw_ref[1:2, sl]
             + p[HALO + 1:HALO + 1 + HY_TL] * cw_ref[2:3, sl] + cb_ref[:, sl])
        outs.append(o)
    x0_ref[0] = outs[0].astype(BF16)
    z_ref[0] = (outs[2] * outs[1]).astype(BF16)


def _hy_in_call(x, sh, sc, w, cw, cb):
    nb8 = HY_TL // HALO
    t3 = lambda b, j: (b, j, 0)
    full2 = lambda b, j: (0, 0)
    per_b = lambda b, j: (b, 0, 0)
    return pl.pallas_call(
        _hy_in_kernel, grid=(BATCH, SEQ // HY_TL),
        in_specs=[pl.BlockSpec((1, HY_TL, D_MODEL), t3),
                  pl.BlockSpec((1, HALO, D_MODEL), lambda b, j: (b, jnp.maximum(j * nb8 - 1, 0), 0)),
                  pl.BlockSpec((1, HALO, D_MODEL), lambda b, j: (b, jnp.minimum((j + 1) * nb8, SEQ // HALO - 1), 0)),
                  pl.BlockSpec((1, 1, D_MODEL), per_b),
                  pl.BlockSpec((1, 1, D_MODEL), per_b),
                  pl.BlockSpec((D_MODEL, 3 * HY_WIDTH), full2),
                  pl.BlockSpec((SHORT_CONV, 3 * HY_WIDTH), full2),
                  pl.BlockSpec((1, 3 * HY_WIDTH), full2)],
        out_specs=[pl.BlockSpec((1, HY_TL, HY_WIDTH), t3), pl.BlockSpec((1, HY_TL, HY_WIDTH), t3)],
        out_shape=[jax.ShapeDtypeStruct((BATCH, SEQ, HY_WIDTH), BF16),
                   jax.ShapeDtypeStruct((BATCH, SEQ, HY_WIDTH), BF16)],
        scratch_shapes=[pltpu.VMEM((HY_TL + 2 * HALO, D_MODEL), F32)],
        compiler_params=_params(("parallel", "arbitrary")), name="hyena_in_proj",
    )(x, x, x, sh, sc, w, cw, cb)


def _fft_tables():
    c = np.arange(FFT_N2, dtype=np.int64)
    ang = 2.0 * np.pi * ((c[:, None] * c[None, :]) % FFT_N2) / FFT_N2
    sr, si = np.cos(ang), -np.sin(ang)
    m = np.block([[sr, -si], [si, sr]])
    k1 = np.arange(FFT_NK, dtype=np.int64)
    ang_t = 2.0 * np.pi * (k1[:, None] * c[None, :]) / DFT_N
    lanes = np.ones((1, 1, 128))
    tr = np.cos(ang_t)[:, :, None] * lanes
    ti = -np.sin(ang_t)[:, :, None] * lanes
    return jnp.asarray(m, BF16), jnp.asarray(tr, F32), jnp.asarray(ti, F32)


def _lin(acc, coef, val):
    if abs(coef) < 1e-12:
        return acc
    term = val if coef == 1.0 else (-val if coef == -1.0 else coef * val)
    return term if acc is None else acc + term


def _twiddle(tr_ref, ti_ref, k1, width):
    reps = width // 128
    tr, ti = tr_ref[k1], ti_ref[k1]
    return jnp.concatenate([tr] * reps, axis=1), jnp.concatenate([ti] * reps, axis=1)


def _class_spectrum(block, k1, m_ref, tr_ref, ti_ref, width):
    yr = yi = None
    for a in range(FFT_NA):
        th = 2.0 * math.pi * ((a * k1) % FFT_N1) / FFT_N1
        za = block(a)
        yr = _lin(yr, round(math.cos(th), 15), za)
        yi = _lin(yi, round(-math.sin(th), 15), za)
    if k1 > 0:
        tr, ti = _twiddle(tr_ref, ti_ref, k1, width)
        yr, yi = (yr * tr, yr * ti) if yi is None else (yr * tr - yi * ti, yr * ti + yi * tr)
    if yi is None:
        x = jnp.dot(m_ref[:, :FFT_N2], yr.astype(BF16), preferred_element_type=F32)
    else:
        x = jnp.dot(m_ref[...], jnp.concatenate([yr, yi], axis=0).astype(BF16), preferred_element_type=F32)
    return x[:FFT_N2], x[FFT_N2:]


def _class_inverse(yr, yi, k1, m_ref, tr_ref, ti_ref, acc_ref, width):
    v = jnp.dot(m_ref[...], jnp.concatenate([yr, -yi], axis=0).astype(BF16), preferred_element_type=F32)
    ur, ui = v[:FFT_N2], -v[FFT_N2:]
    if k1 > 0:
        tr, ti = _twiddle(tr_ref, ti_ref, k1, width)
        ur, ui = ur * tr + ui * ti, ui * tr - ur * ti
    scale = (1.0 if k1 in (0, FFT_N1 // 2) else 2.0) / DFT_N
    for a in range(FFT_NA):
        th = 2.0 * math.pi * ((a * k1) % FFT_N1) / FFT_N1
        term = _lin(None, round(math.cos(th), 15) * scale, ur)
        term = _lin(term, round(-math.sin(th), 15) * scale, ui)
        rows = slice(a * FFT_N2, (a + 1) * FFT_N2)
        if k1 == 0:
            acc_ref[rows, :] = term
        else:
            acc_ref[rows, :] += term


def _spec_kernel(hf_ref, hb_ref, m_ref, tr_ref, ti_ref, c_ref):
    for k1 in range(FFT_NK):
        fr, fi = _class_spectrum(lambda a: hf_ref[a * FFT_N2:(a + 1) * FFT_N2, :], k1, m_ref, tr_ref, ti_ref, HY_CT)
        br, bi = _class_spectrum(lambda a: hb_ref[a * FFT_N2:(a + 1) * FFT_N2, :], k1, m_ref, tr_ref, ti_ref, HY_CT)
        c_ref[k1, :FFT_N2, :] = (fr + br).astype(BF16)
        c_ref[k1, FFT_N2:, :] = (fi - bi).astype(BF16)


def _fft_table_specs(ngrid):
    z = (0,) * 2
    z3 = (0,) * 3
    if ngrid == 1:
        return [pl.BlockSpec((2 * FFT_N2, 2 * FFT_N2), lambda c: z),
                pl.BlockSpec((FFT_NK, FFT_N2, 128), lambda c: z3),
                pl.BlockSpec((FFT_NK, FFT_N2, 128), lambda c: z3)]
    return [pl.BlockSpec((2 * FFT_N2, 2 * FFT_N2), lambda b, c: z),
            pl.BlockSpec((FFT_NK, FFT_N2, 128), lambda b, c: z3),
            pl.BlockSpec((FFT_NK, FFT_N2, 128), lambda b, c: z3)]


def _spec_call(hfb, m, tr, ti):
    nct = HY_WIDTH // HY_CT
    return pl.pallas_call(
        _spec_kernel, grid=(nct,),
        in_specs=[pl.BlockSpec((SEQ, HY_CT), lambda c: (0, c)),
                  pl.BlockSpec((SEQ, HY_CT), lambda c: (0, c + nct))] + _fft_table_specs(1),
        out_specs=pl.BlockSpec((FFT_NK, 2 * FFT_N2, HY_CT), lambda c: (0, 0, c)),
        out_shape=jax.ShapeDtypeStruct((FFT_NK, 2 * FFT_N2, HY_WIDTH), BF16),
        compiler_params=_params(("arbitrary",)), name="hyena_filter_spectrum",
    )(hfb, hfb, m, tr, ti)


def _conv_kernel(z_ref, c_ref, m_ref, tr_ref, ti_ref, y_ref, acc):
    for k1 in range(FFT_NK):
        xr, xi = _class_spectrum(lambda a: z_ref[0, a * FFT_N2:(a + 1) * FFT_N2, :].astype(F32), k1,
                                 m_ref, tr_ref, ti_ref, HY_CT)
        cr = c_ref[k1, :FFT_N2, :].astype(F32)
        ci = c_ref[k1, FFT_N2:, :].astype(F32)
        _class_inverse(xr * cr - xi * ci, xr * ci + xi * cr, k1, m_ref, tr_ref, ti_ref, acc, HY_CT)
    y_ref[0] = acc[...].astype(BF16)


def _conv_call(z, spec, m, tr, ti):
    return pl.pallas_call(
        _conv_kernel, grid=(HY_WIDTH // HY_CT, BATCH),
        in_specs=[pl.BlockSpec((1, SEQ, HY_CT), lambda c, b: (b, 0, c)),
                  pl.BlockSpec((FFT_NK, 2 * FFT_N2, HY_CT), lambda c, b: (0, 0, c))] + _fft_table_specs(2),
        out_specs=pl.BlockSpec((1, SEQ, HY_CT), lambda c, b: (b, 0, c)),
        out_shape=jax.ShapeDtypeStruct((BATCH, SEQ, HY_WIDTH), BF16),
        scratch_shapes=[pltpu.VMEM((SEQ, HY_CT), F32)],
        compiler_params=_params(("parallel", "arbitrary")), name="hyena_long_conv",
    )(z, spec, m, tr, ti)


def _hy_out_kernel(y_ref, z_ref, x0_ref, b_ref, w_ref, x_ref, g_ref, o_ref):
    z = z_ref[0].astype(F32)
    gated = x0_ref[0].astype(F32) * (y_ref[0].astype(F32) + b_ref[...] * z)
    mix = jnp.dot(gated.astype(BF16), w_ref[...], preferred_element_type=F32)
    o_ref[0] = x_ref[0] + g_ref[0] * mix


def _hy_out_call(y, z, x0, bias, w, x, g1):
    t3 = lambda b, j: (b, j, 0)
    full2 = lambda b, j: (0, 0)
    return pl.pallas_call(
        _hy_out_kernel, grid=(BATCH, SEQ // HY_TL),
        in_specs=[pl.BlockSpec((1, HY_TL, HY_WIDTH), t3),
                  pl.BlockSpec((1, HY_TL, HY_WIDTH), t3),
                  pl.BlockSpec((1, HY_TL, HY_WIDTH), t3),
                  pl.BlockSpec((1, HY_WIDTH), full2),
                  pl.BlockSpec((HY_WIDTH, D_MODEL), full2),
                  pl.BlockSpec((1, HY_TL, D_MODEL), t3),
                  pl.BlockSpec((1, 1, D_MODEL), lambda b, j: (b, 0, 0))],
        out_specs=pl.BlockSpec((1, HY_TL, D_MODEL), t3),
        out_shape=jax.ShapeDtypeStruct((BATCH, SEQ, D_MODEL), F32),
        compiler_params=_params(("parallel", "arbitrary")), name="hyena_out_proj",
    )(y, z, x0, bias, w, x, g1)


def _hyena_filter(w1, b1, w2, b2, w3, freq):
    hi = lax.Precision.HIGHEST
    Lq = SEQ
    t = jnp.linspace(0.0, 1.0, Lq, dtype=F32)[:, None]
    ang = 2.0 * math.pi * jnp.arange(Lq, dtype=F32)[:, None] / Lq
    bands = jnp.linspace(1e-4, FILT_BANDS - 1, FILT_BANDS, dtype=F32)
    z = jnp.concatenate([t, jnp.cos(bands * ang), -jnp.sin(bands * ang)], axis=-1)
    hid = jnp.sin(freq * (jnp.dot(z, w1, precision=hi) + b1))
    hid = jnp.sin(freq * (jnp.dot(hid, w2, precision=hi) + b2))
    deltas = jnp.linspace(HY_MIN_DECAY, HY_MAX_DECAY, HY_WIDTH, dtype=F32)
    hf = jnp.dot(hid, w3, precision=hi) * jnp.exp(-t * jnp.tile(deltas, 2))
    ssq = jnp.sum(hf * hf, axis=0)
    ssq = ssq[:HY_WIDTH] + ssq[HY_WIDTH:]
    return hf * jnp.tile(lax.rsqrt(ssq + EPS), 2)


def kernel(x, c, ctx, c_ctx, ada_w, ada_b, ev_w_in, mla_q_norm, mla_w_uq, mla_kv_norm, mla_w_ukv, mla_q_qknorm, mla_k_qknorm, s5_lam_re, s5_lam_im, s5_log_step, s5_b_re, s5_b_im, s5_c_re, s5_c_im, s5_d, s5_glu_w, s5_glu_b, ev_w_out, hy_w_in, hy_conv_w, hy_conv_b, hy_f_w1, hy_f_b1, hy_f_w2, hy_f_b2, hy_f_w3, hy_f_freq, hy_bias, hy_w_out, moe_router_w, moe_router_b, moe_w_gate, moe_w_up, moe_w_down, moe_sh_gate, moe_sh_up, moe_sh_down):
    hi = lax.Precision.HIGHEST
    D = D_MODEL
    sc = jax.nn.silu(c)
    sc_ctx = jax.nn.silu(c_ctx)

    def mods(li):
        mod = jnp.dot(sc, ada_w[li], precision=hi) + ada_b[li]
        return [m[:, None, :] for m in jnp.split(mod, 6, axis=-1)]

    sh1, sc1, g1, sh2, sc2, g2 = mods(0)
    mod_ctx = jnp.dot(sc_ctx, ada_w[0][:, :2 * D], precision=hi) + ada_b[0][:2 * D]
    w0 = _a0_weights(ev_w_in[0], mla_q_norm[0], mla_w_uq[0], mla_kv_norm[0], mla_w_ukv[0],
                     mla_q_qknorm[0], mla_k_qknorm[0])
    q, k, v, u = _a0_call(x, ctx, sh1, sc1, mod_ctx[None, :D], mod_ctx[None, D:], w0)
    att = _attn_call(q, k, v)
    ug = u.reshape(BATCH, S5_NCHUNK, S5_CHUNK, S5_GROUPS, S5_GROUP)
    ug = ug.transpose(3, 1, 0, 2, 4).reshape(S5_GROUPS, S5_ROWS, S5_COLS)
    ys = _s5_call(ug, *_s5_weights(s5_lam_re[0], s5_lam_im[0], s5_log_step[0], s5_b_re[0], s5_b_im[0],
                                   s5_c_re[0], s5_c_im[0]))
    ys = ys.reshape(S5_GROUPS, S5_NCHUNK_LAT, BATCH, S5_CHUNK, S5_GROUP)
    ys = ys.transpose(2, 1, 3, 0, 4).reshape(BATCH, SEQ, S5_WIDTH)
    wo = ev_w_out[0].astype(BF16)
    wo_att = jnp.concatenate([wo[:MLA_WIDTH].reshape(MLA_HEADS, V_HEAD, D_MODEL),
                              jnp.zeros((MLA_HEADS, HEAD_PAD - V_HEAD, D_MODEL), BF16)], axis=1)
    x = _a1_call(att, ys, u, s5_d[0][None, :], s5_glu_w[0].astype(BF16), s5_glu_b[0][None, :],
                 wo_att.reshape(MLA_HEADS * HEAD_PAD, D_MODEL), wo[MLA_WIDTH:], x, g1)
    x = _moe(x, sh2, sc2, g2, moe_router_w[0], moe_router_b[0], moe_w_gate, moe_w_up, moe_w_down,
             moe_sh_gate[0], moe_sh_up[0], moe_sh_down[0], 0)

    sh1, sc1, g1, sh2, sc2, g2 = mods(1)
    z, x0 = _hy_in_call(x, sh1, sc1, hy_w_in[0].astype(BF16), hy_conv_w[0], hy_conv_b[0][None, :])
    fft_tabs = _fft_tables()
    hfb = _hyena_filter(hy_f_w1[0], hy_f_b1[0], hy_f_w2[0], hy_f_b2[0], hy_f_w3[0], hy_f_freq[0])
    y = _conv_call(z, _spec_call(hfb, *fft_tabs), *fft_tabs)
    x = _hy_out_call(y, z, x0, hy_bias[0][None, :], hy_w_out[0].astype(BF16), x, g1)
    x = _moe(x, sh2, sc2, g2, moe_router_w[1], moe_router_b[1], moe_w_gate, moe_w_up, moe_w_down,
             moe_sh_gate[1], moe_sh_up[1], moe_sh_down[1], 1)
    return x
```

```python
import functools
import math

import numpy as np
import jax
import jax.numpy as jnp
from jax import lax
from jax.experimental import pallas as pl
from jax.experimental.pallas import tpu as pltpu
from jax.experimental.pallas import tpu_sc as plsc

F32 = jnp.float32
BF16 = jnp.bfloat16

D_MODEL = 1024
BATCH = 8
SEQ = 4096
CTX_LEN = 256
KV_LEN = SEQ + CTX_LEN
GRID_W = 64
EPS = 1e-6

MLA_HEADS = 8
QK_NOPE = 64
QK_ROPE = 32
QK_HEAD = QK_NOPE + QK_ROPE
V_HEAD = 64
Q_LORA = 256
KV_LORA = 128
MLA_WIDTH = MLA_HEADS * V_HEAD
ROPE_BASE = 10000.0
HEAD_PAD = 128

S5_WIDTH = 512
S5_GROUP = 16
S5_GROUPS = S5_WIDTH // S5_GROUP
S5_STATE = 64
S5_CHUNK = 32
S5_NCHUNK = KV_LEN // S5_CHUNK
S5_NCHUNK_LAT = SEQ // S5_CHUNK
S5_NCHUNK_CTX = CTX_LEN // S5_CHUNK

HY_WIDTH = D_MODEL
FILT_EMB = 33
FILT_BANDS = (FILT_EMB - 1) // 2
SHORT_CONV = 3
HY_MIN_DECAY = -math.log(1e-2) / 1.5
HY_MAX_DECAY = -math.log(1e-2) / 0.3
DFT_N = 2 * SEQ
FFT_N1 = 16
FFT_N2 = DFT_N // FFT_N1
FFT_NA = FFT_N1 // 2
FFT_NK = FFT_N1 // 2 + 1
HY_CT = 256

N_EXPERTS = 64
TOP_K = 6
EXPERT_FF = 256
ROUTE_SCALE = 2.5
MOE_BLOCK = 1024
ROW_WORDS = D_MODEL // 4
SC_WINDOW = 128

V7X_VMEM_BYTES = 64 * 1024 * 1024
VMEM_LIMIT = V7X_VMEM_BYTES - 8 * 1024 * 1024


def _params(semantics):
    return pltpu.CompilerParams(dimension_semantics=semantics, vmem_limit_bytes=VMEM_LIMIT)


def _norm_mod(x, shift, scale):
    ms = jnp.mean(x * x, axis=-1, keepdims=True)
    return x * lax.rsqrt(ms + EPS) * (1.0 + scale) + shift


def _rms(x, gain, n):
    ms = jnp.sum(x * x, axis=-1, keepdims=True) * (1.0 / n)
    return x * lax.rsqrt(ms + EPS) * gain


A0_TL = 256
A0_NT = SEQ // A0_TL


def _rope_perm():
    return np.concatenate([np.arange(0, QK_ROPE, 2), np.arange(1, QK_ROPE, 2)])


def _rope_tables():
    t = np.arange(SEQ)
    row = (t // GRID_W).astype(np.float64)
    col = (t % GRID_W).astype(np.float64)
    n_freq = QK_ROPE // 4
    inv = ROPE_BASE ** (-np.arange(n_freq, dtype=np.float64) / n_freq)
    ang = np.concatenate([row[:, None] * inv, col[:, None] * inv], axis=-1)
    cos, sin = np.cos(ang), np.sin(ang)
    half = QK_ROPE // 2
    a = np.zeros((KV_LEN, HEAD_PAD))
    b = np.zeros((KV_LEN, HEAD_PAD))
    a[:, :QK_HEAD] = 1.0
    a[:SEQ, QK_NOPE:QK_NOPE + half] = cos
    a[:SEQ, QK_NOPE + half:QK_HEAD] = cos
    b[:SEQ, QK_NOPE:QK_NOPE + half] = -sin
    b[:SEQ, QK_NOPE + half:QK_HEAD] = sin
    return a, b


def _norm_rope_heads(f, gain_ref, a, b, out_ref):
    width = MLA_HEADS * HEAD_PAD
    for hd in range(MLA_HEADS):
        sl = slice(hd * HEAD_PAD, (hd + 1) * HEAD_PAD)
        x = f[:, sl]
        r = lax.rsqrt(jnp.sum(x * x, axis=-1, keepdims=True) * (1.0 / QK_HEAD) + EPS)
        rot = x * (a * gain_ref[:, sl]) + f[:, width + hd * HEAD_PAD:width + (hd + 1) * HEAD_PAD] * b
        out_ref[0, :, sl] = (rot * r).astype(BF16)


def _a0_kernel(x_ref, ctx_ref, sh_ref, sc_ref, shc_ref, scc_ref, win_ref, qn_ref, wuq_ref, kvn_ref,
               wk_ref, wuv_ref, qg_ref, kg_ref, ka_ref, kb_ref, qa_ref, qb_ref,
               q_ref, k_ref, v_ref, u_ref):
    j = pl.program_id(1)
    is_ctx = j == A0_NT
    xin = jnp.where(is_ctx, ctx_ref[0], x_ref[0])
    shift = jnp.where(is_ctx, shc_ref[...], sh_ref[0])
    scale = jnp.where(is_ctx, scc_ref[...], sc_ref[0])
    h = _norm_mod(xin, shift, scale).astype(BF16)
    proj = jnp.dot(h, win_ref[...], preferred_element_type=F32)
    u_ref[0] = proj[:, 512:].astype(BF16)

    c_kv = _rms(proj[:, Q_LORA:Q_LORA + KV_LORA], kvn_ref[...], KV_LORA).astype(BF16)
    lane = lax.broadcasted_iota(jnp.int32, (1, MLA_HEADS * HEAD_PAD), 1)
    ones_lane = (lane % HEAD_PAD == V_HEAD).astype(F32)
    v_ref[0] = (jnp.dot(c_kv, wuv_ref[...], preferred_element_type=F32) + ones_lane).astype(BF16)
    kin = jnp.concatenate([c_kv, proj[:, 384:512].astype(BF16)], axis=1)
    kf = jnp.dot(kin, wk_ref[...], preferred_element_type=F32)
    _norm_rope_heads(kf, kg_ref, ka_ref[...], kb_ref[...], k_ref)

    @pl.when(j < A0_NT)
    def _():
        ql = _rms(proj[:, :Q_LORA], qn_ref[...], Q_LORA).astype(BF16)
        qf = jnp.dot(ql, wuq_ref[...], preferred_element_type=F32)
        _norm_rope_heads(qf, qg_ref, qa_ref[...], qb_ref[...], q_ref)


def _a0_call(x, ctx, sh, sc, shc, scc, w):
    nt = A0_NT
    lat = lambda b, j: (b, jnp.minimum(j, nt - 1), 0)
    full2 = lambda b, j: (0, 0)
    per_b = lambda b, j: (b, 0, 0)
    tab = pl.BlockSpec((A0_TL, HEAD_PAD), lambda b, j: (j, 0))
    in_specs = [
        pl.BlockSpec((1, A0_TL, D_MODEL), lat),
        pl.BlockSpec((1, CTX_LEN, D_MODEL), per_b),
        pl.BlockSpec((1, 1, D_MODEL), per_b),
        pl.BlockSpec((1, 1, D_MODEL), per_b),
        pl.BlockSpec((1, D_MODEL), full2),
        pl.BlockSpec((1, D_MODEL), full2),
        pl.BlockSpec((D_MODEL, 1024), full2),
        pl.BlockSpec((1, Q_LORA), full2),
        pl.BlockSpec((Q_LORA, 2 * MLA_HEADS * HEAD_PAD), full2),
        pl.BlockSpec((1, KV_LORA), full2),
        pl.BlockSpec((2 * KV_LORA, 2 * MLA_HEADS * HEAD_PAD), full2),
        pl.BlockSpec((KV_LORA, MLA_HEADS * HEAD_PAD), full2),
        pl.BlockSpec((1, MLA_HEADS * HEAD_PAD), full2),
        pl.BlockSpec((1, MLA_HEADS * HEAD_PAD), full2),
        tab, tab, tab, tab,
    ]
    out_specs = [
        pl.BlockSpec((1, A0_TL, MLA_HEADS * HEAD_PAD), lat),
        pl.BlockSpec((1, A0_TL, MLA_HEADS * HEAD_PAD), lambda b, j: (b, j, 0)),
        pl.BlockSpec((1, A0_TL, MLA_HEADS * HEAD_PAD), lambda b, j: (b, j, 0)),
        pl.BlockSpec((1, A0_TL, S5_WIDTH), lambda b, j: (b, j, 0)),
    ]
    out_shape = [
        jax.ShapeDtypeStruct((BATCH, SEQ, MLA_HEADS * HEAD_PAD), BF16),
        jax.ShapeDtypeStruct((BATCH, KV_LEN, MLA_HEADS * HEAD_PAD), BF16),
        jax.ShapeDtypeStruct((BATCH, KV_LEN, MLA_HEADS * HEAD_PAD), BF16),
        jax.ShapeDtypeStruct((BATCH, KV_LEN, S5_WIDTH), BF16),
    ]
    return pl.pallas_call(
        _a0_kernel, grid=(BATCH, nt + 1), in_specs=in_specs, out_specs=out_specs, out_shape=out_shape,
        compiler_params=_params(("parallel", "arbitrary")), name="even_in_proj",
    )(x, ctx, sh, sc, shc, scc, *w)


def _gain_swap(w, gain):
    half = QK_ROPE // 2
    wg = (w * gain).reshape(w.shape[0], MLA_HEADS, HEAD_PAD)
    re, im = wg[..., QK_NOPE:QK_NOPE + half], wg[..., QK_NOPE + half:QK_HEAD]
    out = jnp.concatenate([jnp.zeros_like(wg[..., :QK_NOPE]), im, re, jnp.zeros_like(wg[..., QK_HEAD:])], axis=-1)
    return out.reshape(w.shape)


def _a0_weights(w_in, q_norm, w_uq, kv_norm, w_ukv, q_qk, k_qk):
    perm = _rope_perm()
    kr0 = Q_LORA + KV_LORA
    w_cat = jnp.concatenate([
        w_in[:, :kr0], w_in[:, kr0:kr0 + QK_ROPE][:, perm],
        jnp.zeros((D_MODEL, HEAD_PAD - QK_ROPE), F32), w_in[:, kr0 + QK_ROPE:]], axis=1).astype(BF16)
    pad = HEAD_PAD - QK_HEAD

    def head_gain(g):
        gh = jnp.concatenate([g[:QK_NOPE], g[QK_NOPE:][perm], jnp.zeros((pad,), F32)])
        return jnp.tile(gh, MLA_HEADS)[None, :]

    uq = w_uq.reshape(Q_LORA, MLA_HEADS, QK_HEAD)
    uq = jnp.concatenate([uq[..., :QK_NOPE], uq[..., QK_NOPE:][..., perm],
                          jnp.zeros((Q_LORA, MLA_HEADS, pad), F32)], axis=-1)
    uq = uq.reshape(Q_LORA, MLA_HEADS * HEAD_PAD)
    uq = jnp.concatenate([uq, _gain_swap(uq, head_gain(q_qk))], axis=1).astype(BF16)
    ukv = w_ukv.reshape(KV_LORA, MLA_HEADS, QK_NOPE + V_HEAD)
    uk = jnp.concatenate([ukv[..., :QK_NOPE], jnp.zeros((KV_LORA, MLA_HEADS, HEAD_PAD - QK_NOPE), F32)], axis=-1)
    uk = uk.reshape(KV_LORA, MLA_HEADS * HEAD_PAD)
    place = np.zeros((KV_LORA, MLA_HEADS, HEAD_PAD), np.float32)
    for i in range(QK_ROPE):
        place[i, :, QK_NOPE + i] = 1.0
    wk = jnp.concatenate([uk, jnp.asarray(place.reshape(KV_LORA, MLA_HEADS * HEAD_PAD))], axis=0)
    wk = jnp.concatenate([wk, _gain_swap(wk, head_gain(k_qk))], axis=1).astype(BF16)
    wuv = jnp.concatenate([ukv[..., QK_NOPE:], jnp.zeros((KV_LORA, MLA_HEADS, HEAD_PAD - V_HEAD), F32)], axis=-1)
    wuv = wuv.reshape(KV_LORA, MLA_HEADS * HEAD_PAD).astype(BF16)
    a, b = _rope_tables()
    qs = QK_HEAD ** -0.5 * math.log2(math.e)
    tabs = [jnp.asarray(t, F32) for t in (a, b, a * qs, b * qs)]
    return [w_cat, q_norm[None, :], uq, kv_norm[None, :], wk, wuv, head_gain(q_qk), head_gain(k_qk)] + tabs


ATT_TQ = 256
HEADS_PER_STEP = 4


def _attn_kernel(q_ref, k_ref, v_ref, o_ref):
    for hh in range(HEADS_PER_STEP):
        sl = slice(hh * HEAD_PAD, (hh + 1) * HEAD_PAD)
        s = lax.dot_general(q_ref[0, :, sl], k_ref[0, :, sl], (((1,), (1,)), ((), ())),
                            preferred_element_type=F32)
        m = jnp.max(s, axis=-1, keepdims=True)
        p = jnp.exp2(s - m).astype(BF16)
        acc = jnp.dot(p, v_ref[0, :, sl], preferred_element_type=F32)
        o_ref[0, :, sl] = (acc * (1.0 / acc[:, V_HEAD:V_HEAD + 1])).astype(BF16)


def _attn_call(q, k, v):
    wq = HEADS_PER_STEP * HEAD_PAD
    return pl.pallas_call(
        _attn_kernel, grid=(BATCH, MLA_HEADS // HEADS_PER_STEP, SEQ // ATT_TQ),
        in_specs=[pl.BlockSpec((1, ATT_TQ, wq), lambda b, h, i: (b, i, h)),
                  pl.BlockSpec((1, KV_LEN, wq), lambda b, h, i: (b, 0, h)),
                  pl.BlockSpec((1, KV_LEN, wq), lambda b, h, i: (b, 0, h))],
        out_specs=pl.BlockSpec((1, ATT_TQ, wq), lambda b, h, i: (b, i, h)),
        out_shape=jax.ShapeDtypeStruct((BATCH, SEQ, MLA_HEADS * HEAD_PAD), BF16),
        compiler_params=_params(("parallel", "parallel", "arbitrary")), name="mla_attention",
    )(q, k, v)


S5_ROWS = S5_NCHUNK * BATCH
S5_ROWS_LAT = S5_NCHUNK_LAT * BATCH
S5_COLS = S5_CHUNK * S5_GROUP
S5_SW = 2 * S5_STATE


def _s5_kernel(u_ref, r_ref, mb_ref, mc_ref, coef_ref, y_ref, x_sc, sp_sc, t_sc):
    u = u_ref[0]
    lags = r_ref[0]
    for sg in range(S5_CHUNK):
        off = (S5_CHUNK - 1 - sg) * S5_GROUP
        t_sc[sg * S5_GROUP:(sg + 1) * S5_GROUP, :] = lags[:, off:off + S5_COLS].astype(BF16)
    x_sc[...] = jnp.dot(u, mb_ref[0], preferred_element_type=F32)
    cf = coef_ref[0]
    af, bfm, bfp, ab, bbm, bbp = [cf[i * 8:(i + 1) * 8] for i in range(6)]

    def body(i, carry):
        sf, sfw, sb, sbw = carry
        cfw = jnp.where(i < S5_NCHUNK_CTX, i + S5_NCHUNK_LAT, i - S5_NCHUNK_CTX)
        rf = pl.multiple_of(cfw * BATCH, BATCH)
        rb = pl.multiple_of((S5_NCHUNK - 1 - i) * BATCH, BATCH)
        sp_sc[pl.ds(rf, BATCH), 0:S5_SW] = sf
        sp_sc[pl.ds(rb, BATCH), S5_SW:2 * S5_SW] = sb
        xf = x_sc[pl.ds(rf, BATCH), 0:S5_SW]
        xfw = x_sc[pl.ds(rf, BATCH), S5_SW:2 * S5_SW]
        xb = x_sc[pl.ds(rb, BATCH), 2 * S5_SW:3 * S5_SW]
        xbw = x_sc[pl.ds(rb, BATCH), 3 * S5_SW:4 * S5_SW]
        return (sf * af + sfw * bfm + xf, sfw * af + sf * bfp + xfw,
                sb * ab + sbw * bbm + xb, sbw * ab + sb * bbp + xbw)

    z = jnp.zeros((BATCH, S5_SW), F32)
    lax.fori_loop(0, S5_NCHUNK, body, (z, z, z, z))
    y = jnp.dot(u[:S5_ROWS_LAT], t_sc[...], preferred_element_type=F32)
    y = y + jnp.dot(sp_sc[0:S5_ROWS_LAT, :].astype(BF16), mc_ref[0], preferred_element_type=F32)
    y_ref[0] = y.astype(BF16)


def _s5_call(ug, t, mb, mc, coef):
    g3 = lambda g: (g, 0, 0)
    return pl.pallas_call(
        _s5_kernel, grid=(S5_GROUPS,),
        in_specs=[pl.BlockSpec((1, S5_ROWS, S5_COLS), g3),
                  pl.BlockSpec((1, S5_GROUP, 2 * S5_COLS), g3),
                  pl.BlockSpec((1, S5_COLS, 4 * S5_SW), g3),
                  pl.BlockSpec((1, 2 * S5_SW, S5_COLS), g3),
                  pl.BlockSpec((1, 6 * 8, S5_SW), g3)],
        out_specs=pl.BlockSpec((1, S5_ROWS_LAT, S5_COLS), g3),
        out_shape=jax.ShapeDtypeStruct((S5_GROUPS, S5_ROWS_LAT, S5_COLS), BF16),
        scratch_shapes=[pltpu.VMEM((S5_ROWS, 4 * S5_SW), F32), pltpu.VMEM((S5_ROWS, 2 * S5_SW), F32),
                        pltpu.VMEM((S5_COLS, S5_COLS), BF16)],
        compiler_params=_params(("parallel",)), name="s5_chunked_scan",
    )(ug, t, mb, mc, coef)


def _s5_weights(lam_re, lam_im, log_step, b_re, b_im, c_re, c_im):
    q = S5_CHUNK
    hi = lax.Precision.HIGHEST
    t_blocks, mbs, mcs, coefs = [], [], [], []
    sig = jnp.arange(q)
    for d in range(2):
        lr = jnp.minimum(lam_re[d], -1e-4)
        li = lam_im[d]
        step = jnp.exp(log_step[d])[:, None]
        jj = jnp.arange(q + 1, dtype=F32)[:, None, None]
        mag = jnp.exp(lr * step * jj)
        ph = li * step * jj
        pr, pi = mag * jnp.cos(ph), mag * jnp.sin(ph)
        nr, ni = pr[1] - 1.0, pi[1]
        den = lr * lr + li * li
        fr, fi = (nr * lr + ni * li) / den, (ni * lr - nr * li) / den
        br = fr[..., None] * b_re[d] - fi[..., None] * b_im[d]
        bi = fr[..., None] * b_im[d] + fi[..., None] * b_re[d]
        cr, ci = c_re[d], c_im[d]
        cpr = cr[None] * pr[:, :, None, :] - ci[None] * pi[:, :, None, :]
        cpi = cr[None] * pi[:, :, None, :] + ci[None] * pr[:, :, None, :]
        kern = jnp.einsum('jghp,gpk->jghk', jnp.concatenate([cpr[:q], -cpi[:q]], axis=-1),
                          jnp.concatenate([br, bi], axis=1), precision=hi)
        kt = kern.transpose(1, 3, 0, 2)
        zero_slots = jnp.zeros((S5_GROUPS, S5_GROUP, q, S5_GROUP), F32)
        if d == 0:
            t_blocks.append(jnp.concatenate([zero_slots[:, :, :q - 1], kt, zero_slots[:, :, :1]], axis=2))
        else:
            t_blocks.append(jnp.concatenate([kt[:, :, ::-1], zero_slots], axis=2))
        pw = (q - 1 - sig) if d == 0 else sig
        xr = pr[pw][..., None] * br[None] - pi[pw][..., None] * bi[None]
        xi = pr[pw][..., None] * bi[None] + pi[pw][..., None] * br[None]
        xr = xr.transpose(1, 0, 3, 2).reshape(S5_GROUPS, S5_COLS, S5_STATE)
        xi = xi.transpose(1, 0, 3, 2).reshape(S5_GROUPS, S5_COLS, S5_STATE)
        mbs += [xr, xi, xi, xr]
        po = (sig + 1) if d == 0 else (q - sig)
        mr = cpr[po].transpose(1, 3, 0, 2).reshape(S5_GROUPS, S5_STATE, S5_COLS)
        mi = -cpi[po].transpose(1, 3, 0, 2).reshape(S5_GROUPS, S5_STATE, S5_COLS)
        mcs += [mr, mi]
        are, aim = pr[q], pi[q]
        rows = [jnp.concatenate([are, are], -1), jnp.concatenate([-aim, aim], -1), jnp.concatenate([aim, -aim], -1)]
        coefs += [jnp.broadcast_to(r[:, None, :], (S5_GROUPS, 8, S5_SW)) for r in rows]
    t = (t_blocks[0] + t_blocks[1]).reshape(S5_GROUPS, S5_GROUP, 2 * S5_COLS)
    mb = jnp.concatenate(mbs, axis=-1).astype(BF16)
    mc = jnp.concatenate(mcs, axis=1).astype(BF16)
    coef = jnp.concatenate(coefs, axis=1)
    return t, mb, mc, coef


A1_TL = 512


def _a1_kernel(att_ref, ys_ref, u_ref, d_ref, gw_ref, gb_ref, woa_ref, wos_ref, x_ref, g_ref, *route):
    o_ref = route[5]
    y = u_ref[0].astype(F32) * d_ref[...] + ys_ref[0].astype(F32)
    z = jax.nn.gelu(y)
    gate = jax.nn.sigmoid(jnp.dot(z.astype(BF16), gw_ref[...], preferred_element_type=F32) + gb_ref[...])
    s5 = (z * gate).astype(BF16)
    mix = jnp.dot(att_ref[0], woa_ref[...], preferred_element_type=F32)
    mix = mix + jnp.dot(s5, wos_ref[...], preferred_element_type=F32)
    x_new = x_ref[0] + g_ref[0] * mix
    o_ref[0] = x_new
    _route_tile(x_new, *route[:5], *route[6:])


def _a1_call(att, ys, u, d, gw, gb, woa, wos, x, g1, route):
    r_args, r_in, r_out, r_shape, r_scratch = route
    t3 = lambda b, j: (b, j, 0)
    full2 = lambda b, j: (0, 0)
    return pl.pallas_call(
        _a1_kernel, grid=(BATCH, SEQ // A1_TL),
        in_specs=[pl.BlockSpec((1, A1_TL, MLA_HEADS * HEAD_PAD), t3),
                  pl.BlockSpec((1, A1_TL, S5_WIDTH), t3),
                  pl.BlockSpec((1, A1_TL, S5_WIDTH), t3),
                  pl.BlockSpec((1, S5_WIDTH), full2),
                  pl.BlockSpec((S5_WIDTH, S5_WIDTH), full2),
                  pl.BlockSpec((1, S5_WIDTH), full2),
                  pl.BlockSpec((MLA_HEADS * HEAD_PAD, D_MODEL), full2),
                  pl.BlockSpec((S5_WIDTH, D_MODEL), full2),
                  pl.BlockSpec((1, A1_TL, D_MODEL), t3),
                  pl.BlockSpec((1, 1, D_MODEL), lambda b, j: (b, 0, 0))] + r_in,
        out_specs=[pl.BlockSpec((1, A1_TL, D_MODEL), t3)] + r_out,
        out_shape=[jax.ShapeDtypeStruct((BATCH, SEQ, D_MODEL), F32)] + r_shape,
        scratch_shapes=r_scratch,
        compiler_params=_params(("arbitrary", "arbitrary")), name="even_out_proj_route",
    )(att, ys, u, d, gw, gb, woa, wos, x, g1, *r_args)


MOE_TL = 512
SLOT_PAD = 8


def _pack_rows(v):
    halves = []
    for p in range(2):
        base = 2 * p * ROW_WORDS
        a = pltpu.bitcast(v[:, base:base + ROW_WORDS].astype(BF16).astype(F32), jnp.uint32)
        b = pltpu.bitcast(v[:, base + ROW_WORDS:base + 2 * ROW_WORDS].astype(BF16).astype(F32), jnp.uint32)
        halves.append((a >> 16) | b)
    return halves


def _unpack_rows(lo, hi):
    out = []
    for w in (lo, hi):
        out.append(pltpu.bitcast(w << 16, F32))
        out.append(pltpu.bitcast(w & jnp.uint32(0xFFFF0000), F32))
    return out


def _route_tile(x, sh_ref, sc_ref, rwt_ref, rb_ref, tri_ref,
                hlo_ref, hhi_ref, idx_ref, wt_ref, rank_ref, cnt_ref, run_sc):
    @pl.when((pl.program_id(0) == 0) & (pl.program_id(1) == 0))
    def _():
        run_sc[...] = jnp.zeros_like(run_sc)

    h = _norm_mod(x, sh_ref[0], sc_ref[0])
    hb = h.astype(BF16)
    hlo_ref[0], hhi_ref[0] = _pack_rows(h)
    h_lo = (h - hb.astype(F32)).astype(BF16)
    rwt = rwt_ref[...]
    rw_hi = rwt.astype(BF16)
    rw_lo = (rwt - rw_hi.astype(F32)).astype(BF16)
    logits = lax.dot_general(jnp.concatenate([rw_hi, rw_lo, rw_hi], axis=1), jnp.concatenate([hb, hb, h_lo], axis=1),
                             (((1,), (1,)), ((), ())), preferred_element_type=F32)
    scores = jax.nn.sigmoid(logits)

    work = scores + rb_ref[...]
    expert = lax.broadcasted_iota(jnp.int32, work.shape, 0).astype(F32)
    hits, ids = [], []
    for _ in range(TOP_K):
        m = jnp.max(work, axis=0, keepdims=True)
        ik = jnp.min(jnp.where(work == m, expert, float(N_EXPERTS)), axis=0, keepdims=True)
        hit = expert == ik
        hits.append(hit)
        ids.append(ik)
        work = jnp.where(hit, -jnp.inf, work)
    mask = hits[0]
    for hit in hits[1:]:
        mask = jnp.logical_or(mask, hit)
    maskf = mask.astype(F32)
    before = jnp.dot(maskf.astype(BF16), tri_ref[...], preferred_element_type=F32) + run_sc[:, 0:1]
    sel = [jnp.sum(jnp.where(hit, scores, 0.0), axis=0, keepdims=True) for hit in hits]
    denom = sel[0]
    for s in sel[1:]:
        denom = denom + s
    ranks = [jnp.sum(jnp.where(hit, before, 0.0), axis=0, keepdims=True) for hit in hits]
    pad = [jnp.zeros_like(denom)] * (SLOT_PAD - TOP_K)
    idx_ref[...] = jnp.concatenate(ids + pad, axis=0).astype(jnp.int32)
    wt_ref[...] = jnp.concatenate([s / denom * ROUTE_SCALE for s in sel] + pad, axis=0)
    rank_ref[...] = jnp.concatenate(ranks + pad, axis=0).astype(jnp.int32)
    run_sc[...] += jnp.sum(maskf, axis=1, keepdims=True)
    cnt_ref[...] = run_sc[...]


def _route_plumbing(sh, sc, rw, rb):
    t3 = lambda b, j: (b, j, 0)
    full2 = lambda b, j: (0, 0)
    per_b = lambda b, j: (b, 0, 0)
    nt = SEQ // MOE_TL
    slots = lambda b, j: (0, b * nt + j)
    tri = jnp.asarray(np.triu(np.ones((MOE_TL, MOE_TL), np.float32), 1), BF16)
    args = [sh, sc, rw.T, rb[:, None], tri]
    in_specs = [pl.BlockSpec((1, 1, D_MODEL), per_b),
                pl.BlockSpec((1, 1, D_MODEL), per_b),
                pl.BlockSpec((N_EXPERTS, D_MODEL), full2),
                pl.BlockSpec((N_EXPERTS, 1), full2),
                pl.BlockSpec((MOE_TL, MOE_TL), full2)]
    out_specs = [pl.BlockSpec((1, MOE_TL, ROW_WORDS), t3),
                 pl.BlockSpec((1, MOE_TL, ROW_WORDS), t3),
                 pl.BlockSpec((SLOT_PAD, MOE_TL), slots),
                 pl.BlockSpec((SLOT_PAD, MOE_TL), slots),
                 pl.BlockSpec((SLOT_PAD, MOE_TL), slots),
                 pl.BlockSpec((N_EXPERTS, 128), full2)]
    slot_i = jax.ShapeDtypeStruct((SLOT_PAD, BATCH * SEQ), jnp.int32)
    out_shape = [jax.ShapeDtypeStruct((BATCH, SEQ, ROW_WORDS), jnp.uint32),
                 jax.ShapeDtypeStruct((BATCH, SEQ, ROW_WORDS), jnp.uint32),
                 slot_i,
                 jax.ShapeDtypeStruct((SLOT_PAD, BATCH * SEQ), F32),
                 slot_i,
                 jax.ShapeDtypeStruct((N_EXPERTS, 128), F32)]
    scratch = [pltpu.VMEM((N_EXPERTS, 128), F32)]
    return args, in_specs, out_specs, out_shape, scratch


def _sc_mesh():
    return plsc.VectorSubcoreMesh(core_axis_name="c", subcore_axis_name="s")


def _sc_dispatch(h_words, dest, n_rows):
    n_tok = h_words.shape[0]

    @pl.kernel(out_type=jax.ShapeDtypeStruct((n_rows, ROW_WORDS), jnp.uint32), mesh=_sc_mesh(), scratch_types=[])
    def scatter_rows(h_hbm, i_hbm, o_hbm):
        def body(h_vmem, i_vmem):
            for k in range(TOP_K):
                pltpu.sync_copy(h_vmem, o_hbm.at[i_vmem.at[k]])

        pltpu.emit_pipeline(
            body, grid=(n_tok // SC_WINDOW,),
            in_specs=[pl.BlockSpec((SC_WINDOW, ROW_WORDS), index_map=lambda i: (i, 0)),
                      pl.BlockSpec((SLOT_PAD, SC_WINDOW), index_map=lambda i: (0, i))],
            out_specs=[],
            core_axis_name=("c", "s"), dimension_semantics=(pltpu.PARALLEL,),
        )(h_hbm, i_hbm)

    return scatter_rows(h_words, dest)


def _sc_collect(y_words, dest):
    n_tok = dest.shape[1]

    @pl.kernel(out_type=jax.ShapeDtypeStruct((TOP_K, n_tok, ROW_WORDS), jnp.uint32), mesh=_sc_mesh(),
               scratch_types=[])
    def gather_rows(y_hbm, i_hbm, o_hbm):
        def body(i_vmem, o_vmem):
            pltpu.sync_copy(y_hbm.at[i_vmem.at[0]], o_vmem.at[0])

        pltpu.emit_pipeline(
            body, grid=(TOP_K, n_tok // SC_WINDOW),
            in_specs=[pl.BlockSpec((1, SC_WINDOW), index_map=lambda k, i: (k, i))],
            out_specs=[pl.BlockSpec((1, SC_WINDOW, ROW_WORDS), index_map=lambda k, i: (k, i, 0))],
            core_axis_name=("c", "s"), dimension_semantics=(pltpu.PARALLEL, pltpu.PARALLEL),
        )(i_hbm, o_hbm)

    return gather_rows(y_words, dest)


def _expert_kernel(be_ref, nv_ref, xlo_ref, xhi_ref, wg_ref, wu_ref, wd_ref, ylo_ref, yhi_ref,
                   wg_sc, wu_sc, wd_sc):
    i = pl.program_id(0)
    nv = nv_ref[i]

    @pl.when(jnp.logical_or(i == 0, be_ref[i] != be_ref[jnp.maximum(i - 1, 0)]))
    def _():
        wg_sc[...] = wg_ref[0, 0].astype(BF16)
        wu_sc[...] = wu_ref[0, 0].astype(BF16)
        wd_sc[...] = wd_ref[0, 0].astype(BF16)

    @pl.when(nv > 0)
    def _():
        parts = _unpack_rows(xlo_ref[...], xhi_ref[...])
        xb = jnp.concatenate([p.astype(BF16) for p in parts], axis=1)
        live = lax.broadcasted_iota(jnp.int32, xb.shape, 0) < nv
        xb = jnp.where(live, xb, jnp.zeros_like(xb))
        hid = jax.nn.silu(jnp.dot(xb, wg_sc[...], preferred_element_type=F32))
        hid = hid * jnp.dot(xb, wu_sc[...], preferred_element_type=F32)
        y = jnp.dot(hid.astype(BF16), wd_sc[...], preferred_element_type=F32)
        ylo_ref[...], yhi_ref[...] = _pack_rows(y)

    @pl.when(nv == 0)
    def _():
        ylo_ref[...] = jnp.zeros_like(ylo_ref)
        yhi_ref[...] = jnp.zeros_like(yhi_ref)


def _expert_call(block_e, n_valid, xlo, xhi, wg, wu, wd, li):
    n_rows = xlo.shape[0]
    n_blocks = n_rows // MOE_BLOCK
    rows = pl.BlockSpec((MOE_BLOCK, ROW_WORDS), lambda i, be, nv: (i, 0))
    grid_spec = pltpu.PrefetchScalarGridSpec(
        num_scalar_prefetch=2, grid=(n_blocks,),
        in_specs=[rows, rows,
                  pl.BlockSpec((1, 1, D_MODEL, EXPERT_FF), lambda i, be, nv: (li, be[i], 0, 0)),
                  pl.BlockSpec((1, 1, D_MODEL, EXPERT_FF), lambda i, be, nv: (li, be[i], 0, 0)),
                  pl.BlockSpec((1, 1, EXPERT_FF, D_MODEL), lambda i, be, nv: (li, be[i], 0, 0))],
        out_specs=[rows, rows],
        scratch_shapes=[pltpu.VMEM((D_MODEL, EXPERT_FF), BF16), pltpu.VMEM((D_MODEL, EXPERT_FF), BF16),
                        pltpu.VMEM((EXPERT_FF, D_MODEL), BF16)])
    out = jax.ShapeDtypeStruct((n_rows, ROW_WORDS), jnp.uint32)
    return pl.pallas_call(
        _expert_kernel, grid_spec=grid_spec, out_shape=[out, out],
        compiler_params=_params(("arbitrary",)), name="moe_experts",
    )(block_e, n_valid, xlo, xhi, wg, wu, wd)


def _combine_kernel(ylo_ref, yhi_ref, w_ref, x_ref, sh_ref, sc_ref, g_ref, sg_ref, su_ref, sd_ref, o_ref):
    hb = _norm_mod(x_ref[0], sh_ref[0], sc_ref[0]).astype(BF16)
    hid = jax.nn.silu(jnp.dot(hb, sg_ref[...].astype(BF16), preferred_element_type=F32))
    hid = hid * jnp.dot(hb, su_ref[...].astype(BF16), preferred_element_type=F32)
    shared = jnp.dot(hid.astype(BF16), sd_ref[...].astype(BF16), preferred_element_type=F32)
    w = w_ref[0]
    acc = [None] * 4
    for k in range(TOP_K):
        wk = w[:, k:k + 1]
        for c, part in enumerate(_unpack_rows(ylo_ref[k], yhi_ref[k])):
            acc[c] = wk * part if acc[c] is None else acc[c] + wk * part
    for c in range(4):
        sl = slice(c * ROW_WORDS, (c + 1) * ROW_WORDS)
        o_ref[0, :, sl] = x_ref[0, :, sl] + g_ref[0, :, sl] * (acc[c] + shared[:, sl])


def _combine_call(ylo, yhi, wts, x, sh, sc, g2, sg, su, sd):
    t3 = lambda b, j: (b, j, 0)
    per_b = lambda b, j: (b, 0, 0)
    full2 = lambda b, j: (0, 0)
    ff = sg.shape[1]
    nt = SEQ // MOE_TL
    rows = pl.BlockSpec((TOP_K, MOE_TL, ROW_WORDS), lambda b, j: (0, b * nt + j, 0))
    return pl.pallas_call(
        _combine_kernel, grid=(BATCH, nt),
        in_specs=[rows, rows,
                  pl.BlockSpec((1, MOE_TL, SLOT_PAD), t3),
                  pl.BlockSpec((1, MOE_TL, D_MODEL), t3),
                  pl.BlockSpec((1, 1, D_MODEL), per_b),
                  pl.BlockSpec((1, 1, D_MODEL), per_b),
                  pl.BlockSpec((1, 1, D_MODEL), per_b),
                  pl.BlockSpec((D_MODEL, ff), full2),
                  pl.BlockSpec((D_MODEL, ff), full2),
                  pl.BlockSpec((ff, D_MODEL), full2)],
        out_specs=pl.BlockSpec((1, MOE_TL, D_MODEL), t3),
        out_shape=jax.ShapeDtypeStruct((BATCH, SEQ, D_MODEL), F32),
        compiler_params=_params(("parallel", "arbitrary")), name="moe_combine_shared",
    )(ylo, yhi, wts, x, sh, sc, g2, sg, su, sd)


def _moe(x, routed, sh, sc, g2, w_gate, w_up, w_down, sh_gate, sh_up, sh_down, li):
    T = BATCH * SEQ
    TK = T * TOP_K
    hlo, hhi, idx, wts, rank, counts = routed
    wts = wts.T.reshape(BATCH, SEQ, SLOT_PAD)
    counts = counts[:, 0].astype(jnp.int32)
    padded = (counts + MOE_BLOCK - 1) // MOE_BLOCK * MOE_BLOCK
    pad_end = jnp.cumsum(padded)
    pad_start = pad_end - padded
    n_blocks = -(-TK // MOE_BLOCK) + N_EXPERTS
    n_rows = n_blocks * MOE_BLOCK
    block_start = jnp.arange(n_blocks, dtype=jnp.int32) * MOE_BLOCK
    owns = jnp.logical_and(block_start[:, None] >= pad_start[None, :], block_start[:, None] < pad_end[None, :])
    owns = owns.astype(jnp.int32)
    experts = jnp.arange(N_EXPERTS, dtype=jnp.int32)[None, :]
    block_e = jnp.sum(owns * experts, axis=1) + (N_EXPERTS - 1) * (1 - jnp.sum(owns, axis=1))
    n_valid = jnp.sum(owns * (counts[None, :] - (block_start[:, None] - pad_start[None, :])), axis=1)
    n_valid = jnp.clip(n_valid, 0, MOE_BLOCK).astype(jnp.int32)
    first_row = jnp.sum(jnp.where(idx[None] == experts.T[:, :, None], pad_start[:, None, None], 0), axis=0)
    dest = first_row + rank
    xlo = _sc_dispatch(hlo.reshape(T, ROW_WORDS), dest, n_rows)
    xhi = _sc_dispatch(hhi.reshape(T, ROW_WORDS), dest, n_rows)
    ylo, yhi = _expert_call(block_e, n_valid, xlo, xhi, w_gate, w_up, w_down, li)
    return _combine_call(_sc_collect(ylo, dest), _sc_collect(yhi, dest), wts, x, sh, sc, g2,
                         sh_gate, sh_up, sh_down)


HY_TL = 512
HALO = 8


def _hy_in_kernel(x_ref, xp_ref, xn_ref, sh_ref, sc_ref, w_ref, cw_ref, cb_ref, z_ref, x0_ref, h_sc):
    j = pl.program_id(1)
    shift, scale = sh_ref[0], sc_ref[0]
    keep_prev = (j > 0).astype(F32)
    keep_next = (j < SEQ // HY_TL - 1).astype(F32)
    h_sc[0:HALO, :] = _norm_mod(xp_ref[0], shift, scale) * keep_prev
    h_sc[HALO:HALO + HY_TL, :] = _norm_mod(x_ref[0], shift, scale)
    h_sc[HALO + HY_TL:, :] = _norm_mod(xn_ref[0], shift, scale) * keep_next
    hcat = h_sc[...].astype(BF16)
    outs = []
    for part in range(3):
        sl = slice(part * HY_WIDTH, (part + 1) * HY_WIDTH)
        p = jnp.dot(hcat, w_ref[:, sl], preferred_element_type=F32)
        o = (p[HALO - 1:HALO - 1 + HY_TL] * cw_ref[0:1, sl] + p[HALO:HALO + HY_TL] * cw_ref[1:2, sl]
             + p[HALO + 1:HALO + 1 + HY_TL] * cw_ref[2:3, sl] + cb_ref[:, sl])
        outs.append(o)
    x0_ref[0] = outs[0].astype(BF16)
    z_ref[0] = (outs[2] * outs[1]).astype(BF16)


def _hy_in_call(x, sh, sc, w, cw, cb):
    nb8 = HY_TL // HALO
    t3 = lambda b, j: (b, j, 0)
    full2 = lambda b, j: (0, 0)
    per_b = lambda b, j: (b, 0, 0)
    return pl.pallas_call(
        _hy_in_kernel, grid=(BATCH, SEQ // HY_TL),
        in_specs=[pl.BlockSpec((1, HY_TL, D_MODEL), t3),
                  pl.BlockSpec((1, HALO, D_MODEL), lambda b, j: (b, jnp.maximum(j * nb8 - 1, 0), 0)),
                  pl.BlockSpec((1, HALO, D_MODEL), lambda b, j: (b, jnp.minimum((j + 1) * nb8, SEQ // HALO - 1), 0)),
                  pl.BlockSpec((1, 1, D_MODEL), per_b),
                  pl.BlockSpec((1, 1, D_MODEL), per_b),
                  pl.BlockSpec((D_MODEL, 3 * HY_WIDTH), full2),
                  pl.BlockSpec((SHORT_CONV, 3 * HY_WIDTH), full2),
                  pl.BlockSpec((1, 3 * HY_WIDTH), full2)],
        out_specs=[pl.BlockSpec((1, HY_TL, HY_WIDTH), t3), pl.BlockSpec((1, HY_TL, HY_WIDTH), t3)],
        out_shape=[jax.ShapeDtypeStruct((BATCH, SEQ, HY_WIDTH), BF16),
                   jax.ShapeDtypeStruct((BATCH, SEQ, HY_WIDTH), BF16)],
        scratch_shapes=[pltpu.VMEM((HY_TL + 2 * HALO, D_MODEL), F32)],
        compiler_params=_params(("parallel", "arbitrary")), name="hyena_in_proj",
    )(x, x, x, sh, sc, w, cw, cb)


def _fft_tables():
    c = np.arange(FFT_N2, dtype=np.int64)
    ang = 2.0 * np.pi * ((c[:, None] * c[None, :]) % FFT_N2) / FFT_N2
    sr, si = np.cos(ang), -np.sin(ang)
    m = np.block([[sr, -si], [si, sr]])
    k1 = np.arange(FFT_NK, dtype=np.int64)
    ang_t = 2.0 * np.pi * (k1[:, None] * c[None, :]) / DFT_N
    lanes = np.ones((1, 1, 128))
    tr = np.cos(ang_t)[:, :, None] * lanes
    ti = -np.sin(ang_t)[:, :, None] * lanes
    return jnp.asarray(m, BF16), jnp.asarray(tr, F32), jnp.asarray(ti, F32)


def _lin(acc, coef, val):
    if abs(coef) < 1e-12:
        return acc
    term = val if coef == 1.0 else (-val if coef == -1.0 else coef * val)
    return term if acc is None else acc + term


def _twiddle(tr_ref, ti_ref, k1, width):
    reps = width // 128
    tr, ti = tr_ref[k1], ti_ref[k1]
    return jnp.concatenate([tr] * reps, axis=1), jnp.concatenate([ti] * reps, axis=1)


def _class_spectrum(block, k1, m_ref, tr_ref, ti_ref, width):
    yr = yi = None
    for a in range(FFT_NA):
        th = 2.0 * math.pi * ((a * k1) % FFT_N1) / FFT_N1
        za = block(a)
        yr = _lin(yr, round(math.cos(th), 15), za)
        yi = _lin(yi, round(-math.sin(th), 15), za)
    if k1 > 0:
        tr, ti = _twiddle(tr_ref, ti_ref, k1, width)
        yr, yi = (yr * tr, yr * ti) if yi is None else (yr * tr - yi * ti, yr * ti + yi * tr)
    if yi is None:
        x = jnp.dot(m_ref[:, :FFT_N2], yr.astype(BF16), preferred_element_type=F32)
    else:
        x = jnp.dot(m_ref[...], jnp.concatenate([yr, yi], axis=0).astype(BF16), preferred_element_type=F32)
    return x[:FFT_N2], x[FFT_N2:]


def _class_inverse(yr, yi, k1, m_ref, tr_ref, ti_ref, acc_ref, width):
    v = jnp.dot(m_ref[...], jnp.concatenate([yr, -yi], axis=0).astype(BF16), preferred_element_type=F32)
    ur, ui = v[:FFT_N2], -v[FFT_N2:]
    if k1 > 0:
        tr, ti = _twiddle(tr_ref, ti_ref, k1, width)
        ur, ui = ur * tr + ui * ti, ui * tr - ur * ti
    scale = (1.0 if k1 in (0, FFT_N1 // 2) else 2.0) / DFT_N
    for a in range(FFT_NA):
        th = 2.0 * math.pi * ((a * k1) % FFT_N1) / FFT_N1
        term = _lin(None, round(math.cos(th), 15) * scale, ur)
        term = _lin(term, round(-math.sin(th), 15) * scale, ui)
        rows = slice(a * FFT_N2, (a + 1) * FFT_N2)
        if k1 == 0:
            acc_ref[rows, :] = term
        else:
            acc_ref[rows, :] += term


def _spec_kernel(hf_ref, hb_ref, m_ref, tr_ref, ti_ref, c_ref):
    for k1 in range(FFT_NK):
        fr, fi = _class_spectrum(lambda a: hf_ref[a * FFT_N2:(a + 1) * FFT_N2, :], k1, m_ref, tr_ref, ti_ref, HY_CT)
        br, bi = _class_spectrum(lambda a: hb_ref[a * FFT_N2:(a + 1) * FFT_N2, :], k1, m_ref, tr_ref, ti_ref, HY_CT)
        c_ref[k1, :FFT_N2, :] = (fr + br).astype(BF16)
        c_ref[k1, FFT_N2:, :] = (fi - bi).astype(BF16)


def _fft_table_specs(ngrid):
    z = (0,) * 2
    z3 = (0,) * 3
    if ngrid == 1:
        return [pl.BlockSpec((2 * FFT_N2, 2 * FFT_N2), lambda c: z),
                pl.BlockSpec((FFT_NK, FFT_N2, 128), lambda c: z3),
                pl.BlockSpec((FFT_NK, FFT_N2, 128), lambda c: z3)]
    return [pl.BlockSpec((2 * FFT_N2, 2 * FFT_N2), lambda b, c: z),
            pl.BlockSpec((FFT_NK, FFT_N2, 128), lambda b, c: z3),
            pl.BlockSpec((FFT_NK, FFT_N2, 128), lambda b, c: z3)]


def _spec_call(hfb, m, tr, ti):
    nct = HY_WIDTH // HY_CT
    return pl.pallas_call(
        _spec_kernel, grid=(nct,),
        in_specs=[pl.BlockSpec((SEQ, HY_CT), lambda c: (0, c)),
                  pl.BlockSpec((SEQ, HY_CT), lambda c: (0, c + nct))] + _fft_table_specs(1),
        out_specs=pl.BlockSpec((FFT_NK, 2 * FFT_N2, HY_CT), lambda c: (0, 0, c)),
        out_shape=jax.ShapeDtypeStruct((FFT_NK, 2 * FFT_N2, HY_WIDTH), BF16),
        compiler_params=_params(("arbitrary",)), name="hyena_filter_spectrum",
    )(hfb, hfb, m, tr, ti)


def _conv_kernel(z_ref, c_ref, m_ref, tr_ref, ti_ref, y_ref, acc):
    for k1 in range(FFT_NK):
        xr, xi = _class_spectrum(lambda a: z_ref[0, a * FFT_N2:(a + 1) * FFT_N2, :].astype(F32), k1,
                                 m_ref, tr_ref, ti_ref, HY_CT)
        cr = c_ref[k1, :FFT_N2, :].astype(F32)
        ci = c_ref[k1, FFT_N2:, :].astype(F32)
        _class_inverse(xr * cr - xi * ci, xr * ci + xi * cr, k1, m_ref, tr_ref, ti_ref, acc, HY_CT)
    y_ref[0] = acc[...].astype(BF16)


def _conv_call(z, spec, m, tr, ti):
    return pl.pallas_call(
        _conv_kernel, grid=(HY_WIDTH // HY_CT, BATCH),
        in_specs=[pl.BlockSpec((1, SEQ, HY_CT), lambda c, b: (b, 0, c)),
                  pl.BlockSpec((FFT_NK, 2 * FFT_N2, HY_CT), lambda c, b: (0, 0, c))] + _fft_table_specs(2),
        out_specs=pl.BlockSpec((1, SEQ, HY_CT), lambda c, b: (b, 0, c)),
        out_shape=jax.ShapeDtypeStruct((BATCH, SEQ, HY_WIDTH), BF16),
        scratch_shapes=[pltpu.VMEM((SEQ, HY_CT), F32)],
        compiler_params=_params(("parallel", "arbitrary")), name="hyena_long_conv",
    )(z, spec, m, tr, ti)


def _hy_out_kernel(y_ref, z_ref, x0_ref, b_ref, w_ref, x_ref, g_ref, *route):
    o_ref = route[5]
    z = z_ref[0].astype(F32)
    gated = x0_ref[0].astype(F32) * (y_ref[0].astype(F32) + b_ref[...] * z)
    mix = jnp.dot(gated.astype(BF16), w_ref[...], preferred_element_type=F32)
    x_new = x_ref[0] + g_ref[0] * mix
    o_ref[0] = x_new
    _route_tile(x_new, *route[:5], *route[6:])


def _hy_out_call(y, z, x0, bias, w, x, g1, route):
    r_args, r_in, r_out, r_shape, r_scratch = route
    t3 = lambda b, j: (b, j, 0)
    full2 = lambda b, j: (0, 0)
    return pl.pallas_call(
        _hy_out_kernel, grid=(BATCH, SEQ // HY_TL),
        in_specs=[pl.BlockSpec((1, HY_TL, HY_WIDTH), t3),
                  pl.BlockSpec((1, HY_TL, HY_WIDTH), t3),
                  pl.BlockSpec((1, HY_TL, HY_WIDTH), t3),
                  pl.BlockSpec((1, HY_WIDTH), full2),
                  pl.BlockSpec((HY_WIDTH, D_MODEL), full2),
                  pl.BlockSpec((1, HY_TL, D_MODEL), t3),
                  pl.BlockSpec((1, 1, D_MODEL), lambda b, j: (b, 0, 0))] + r_in,
        out_specs=[pl.BlockSpec((1, HY_TL, D_MODEL), t3)] + r_out,
        out_shape=[jax.ShapeDtypeStruct((BATCH, SEQ, D_MODEL), F32)] + r_shape,
        scratch_shapes=r_scratch,
        compiler_params=_params(("arbitrary", "arbitrary")), name="hyena_out_proj_route",
    )(y, z, x0, bias, w, x, g1, *r_args)


def _hyena_filter(w1, b1, w2, b2, w3, freq):
    hi = lax.Precision.HIGHEST
    Lq = SEQ
    t = jnp.linspace(0.0, 1.0, Lq, dtype=F32)[:, None]
    ang = 2.0 * math.pi * jnp.arange(Lq, dtype=F32)[:, None] / Lq
    bands = jnp.linspace(1e-4, FILT_BANDS - 1, FILT_BANDS, dtype=F32)
    z = jnp.concatenate([t, jnp.cos(bands * ang), -jnp.sin(bands * ang)], axis=-1)
    hid = jnp.sin(freq * (jnp.dot(z, w1, precision=hi) + b1))
    hid = jnp.sin(freq * (jnp.dot(hid, w2, precision=hi) + b2))
    deltas = jnp.linspace(HY_MIN_DECAY, HY_MAX_DECAY, HY_WIDTH, dtype=F32)
    hf = jnp.dot(hid, w3, precision=hi) * jnp.exp(-t * jnp.tile(deltas, 2))
    ssq = jnp.sum(hf * hf, axis=0)
    ssq = ssq[:HY_WIDTH] + ssq[HY_WIDTH:]
    return hf * jnp.tile(lax.rsqrt(ssq + EPS), 2)


def kernel(x, c, ctx, c_ctx, ada_w, ada_b, ev_w_in, mla_q_norm, mla_w_uq, mla_kv_norm, mla_w_ukv, mla_q_qknorm, mla_k_qknorm, s5_lam_re, s5_lam_im, s5_log_step, s5_b_re, s5_b_im, s5_c_re, s5_c_im, s5_d, s5_glu_w, s5_glu_b, ev_w_out, hy_w_in, hy_conv_w, hy_conv_b, hy_f_w1, hy_f_b1, hy_f_w2, hy_f_b2, hy_f_w3, hy_f_freq, hy_bias, hy_w_out, moe_router_w, moe_router_b, moe_w_gate, moe_w_up, moe_w_down, moe_sh_gate, moe_sh_up, moe_sh_down):
    hi = lax.Precision.HIGHEST
    D = D_MODEL
    sc = jax.nn.silu(c)
    sc_ctx = jax.nn.silu(c_ctx)

    def mods(li):
        mod = jnp.dot(sc, ada_w[li], precision=hi) + ada_b[li]
        return [m[:, None, :] for m in jnp.split(mod, 6, axis=-1)]

    sh1, sc1, g1, sh2, sc2, g2 = mods(0)
    mod_ctx = jnp.dot(sc_ctx, ada_w[0][:, :2 * D], precision=hi) + ada_b[0][:2 * D]
    w0 = _a0_weights(ev_w_in[0], mla_q_norm[0], mla_w_uq[0], mla_kv_norm[0], mla_w_ukv[0],
                     mla_q_qknorm[0], mla_k_qknorm[0])
    q, k, v, u = _a0_call(x, ctx, sh1, sc1, mod_ctx[None, :D], mod_ctx[None, D:], w0)
    att = _attn_call(q, k, v)
    ug = u.reshape(BATCH, S5_NCHUNK, S5_CHUNK, S5_GROUPS, S5_GROUP)
    ug = ug.transpose(3, 1, 0, 2, 4).reshape(S5_GROUPS, S5_ROWS, S5_COLS)
    ys = _s5_call(ug, *_s5_weights(s5_lam_re[0], s5_lam_im[0], s5_log_step[0], s5_b_re[0], s5_b_im[0],
                                   s5_c_re[0], s5_c_im[0]))
    ys = ys.reshape(S5_GROUPS, S5_NCHUNK_LAT, BATCH, S5_CHUNK, S5_GROUP)
    ys = ys.transpose(2, 1, 3, 0, 4).reshape(BATCH, SEQ, S5_WIDTH)
    wo = ev_w_out[0].astype(BF16)
    wo_att = jnp.concatenate([wo[:MLA_WIDTH].reshape(MLA_HEADS, V_HEAD, D_MODEL),
                              jnp.zeros((MLA_HEADS, HEAD_PAD - V_HEAD, D_MODEL), BF16)], axis=1)
    x, *routed = _a1_call(att, ys, u, s5_d[0][None, :], s5_glu_w[0].astype(BF16), s5_glu_b[0][None, :],
                          wo_att.reshape(MLA_HEADS * HEAD_PAD, D_MODEL), wo[MLA_WIDTH:], x, g1,
                          _route_plumbing(sh2, sc2, moe_router_w[0], moe_router_b[0]))
    x = _moe(x, routed, sh2, sc2, g2, moe_w_gate, moe_w_up, moe_w_down,
             moe_sh_gate[0], moe_sh_up[0], moe_sh_down[0], 0)

    sh1, sc1, g1, sh2, sc2, g2 = mods(1)
    z, x0 = _hy_in_call(x, sh1, sc1, hy_w_in[0].astype(BF16), hy_conv_w[0], hy_conv_b[0][None, :])
    fft_tabs = _fft_tables()
    hfb = _hyena_filter(hy_f_w1[0], hy_f_b1[0], hy_f_w2[0], hy_f_b2[0], hy_f_w3[0], hy_f_freq[0])
    y = _conv_call(z, _spec_call(hfb, *fft_tabs), *fft_tabs)
    x, *routed = _hy_out_call(y, z, x0, hy_bias[0][None, :], hy_w_out[0].astype(BF16), x, g1,
                              _route_plumbing(sh2, sc2, moe_router_w[1], moe_router_b[1]))
    x = _moe(x, routed, sh2, sc2, g2, moe_w_gate, moe_w_up, moe_w_down,
             moe_sh_gate[1], moe_sh_up[1], moe_sh_down[1], 1)
    return x
```

```python
import functools
import math

import numpy as np
import jax
import jax.numpy as jnp
from jax import lax
from jax.experimental import pallas as pl
from jax.experimental.pallas import tpu as pltpu
from jax.experimental.pallas import tpu_sc as plsc

F32 = jnp.float32
BF16 = jnp.bfloat16

D_MODEL = 1024
BATCH = 8
SEQ = 4096
CTX_LEN = 256
KV_LEN = SEQ + CTX_LEN
GRID_W = 64
EPS = 1e-6

MLA_HEADS = 8
QK_NOPE = 64
QK_ROPE = 32
QK_HEAD = QK_NOPE + QK_ROPE
V_HEAD = 64
Q_LORA = 256
KV_LORA = 128
MLA_WIDTH = MLA_HEADS * V_HEAD
ROPE_BASE = 10000.0
HEAD_PAD = 128

S5_WIDTH = 512
S5_GROUP = 16
S5_GROUPS = S5_WIDTH // S5_GROUP
S5_STATE = 64
S5_CHUNK = 32
S5_NCHUNK = KV_LEN // S5_CHUNK
S5_NCHUNK_LAT = SEQ // S5_CHUNK
S5_NCHUNK_CTX = CTX_LEN // S5_CHUNK

HY_WIDTH = D_MODEL
FILT_EMB = 33
FILT_BANDS = (FILT_EMB - 1) // 2
SHORT_CONV = 3
HY_MIN_DECAY = -math.log(1e-2) / 1.5
HY_MAX_DECAY = -math.log(1e-2) / 0.3
DFT_N = 2 * SEQ
FFT_N1 = 16
FFT_N2 = DFT_N // FFT_N1
FFT_NA = FFT_N1 // 2
FFT_NK = FFT_N1 // 2 + 1
HY_CT = 256

N_EXPERTS = 64
TOP_K = 6
EXPERT_FF = 256
ROUTE_SCALE = 2.5
MOE_BLOCK = 1024
ROW_WORDS = D_MODEL // 4
SC_WINDOW = 128

V7X_VMEM_BYTES = 64 * 1024 * 1024
VMEM_LIMIT = V7X_VMEM_BYTES - 8 * 1024 * 1024


def _params(semantics):
    return pltpu.CompilerParams(dimension_semantics=semantics, vmem_limit_bytes=VMEM_LIMIT)


def _norm_mod(x, shift, scale):
    ms = jnp.mean(x * x, axis=-1, keepdims=True)
    return x * lax.rsqrt(ms + EPS) * (1.0 + scale) + shift


def _rms(x, gain, n):
    ms = jnp.sum(x * x, axis=-1, keepdims=True) * (1.0 / n)
    return x * lax.rsqrt(ms + EPS) * gain


A0_TL = 256
A0_NT = SEQ // A0_TL


def _rope_perm():
    return np.concatenate([np.arange(0, QK_ROPE, 2), np.arange(1, QK_ROPE, 2)])


def _rope_tables():
    t = np.arange(SEQ)
    row = (t // GRID_W).astype(np.float64)
    col = (t % GRID_W).astype(np.float64)
    n_freq = QK_ROPE // 4
    inv = ROPE_BASE ** (-np.arange(n_freq, dtype=np.float64) / n_freq)
    ang = np.concatenate([row[:, None] * inv, col[:, None] * inv], axis=-1)
    cos, sin = np.cos(ang), np.sin(ang)
    half = QK_ROPE // 2
    a = np.zeros((KV_LEN, HEAD_PAD))
    b = np.zeros((KV_LEN, HEAD_PAD))
    a[:, :QK_HEAD] = 1.0
    a[:SEQ, QK_NOPE:QK_NOPE + half] = cos
    a[:SEQ, QK_NOPE + half:QK_HEAD] = cos
    b[:SEQ, QK_NOPE:QK_NOPE + half] = -sin
    b[:SEQ, QK_NOPE + half:QK_HEAD] = sin
    return a, b


def _norm_rope_heads(f, gain_ref, a, b, out_ref):
    width = MLA_HEADS * HEAD_PAD
    for hd in range(MLA_HEADS):
        sl = slice(hd * HEAD_PAD, (hd + 1) * HEAD_PAD)
        x = f[:, sl]
        r = lax.rsqrt(jnp.sum(x * x, axis=-1, keepdims=True) * (1.0 / QK_HEAD) + EPS)
        rot = x * (a * gain_ref[:, sl]) + f[:, width + hd * HEAD_PAD:width + (hd + 1) * HEAD_PAD] * b
        out_ref[0, :, sl] = (rot * r).astype(BF16)


def _a0_kernel(x_ref, ctx_ref, sh_ref, sc_ref, shc_ref, scc_ref, win_ref, qn_ref, wuq_ref, kvn_ref,
               wk_ref, wuv_ref, qg_ref, kg_ref, ka_ref, kb_ref, qa_ref, qb_ref,
               q_ref, k_ref, v_ref, u_ref):
    j = pl.program_id(1)
    is_ctx = j == A0_NT
    xin = jnp.where(is_ctx, ctx_ref[0], x_ref[0])
    shift = jnp.where(is_ctx, shc_ref[...], sh_ref[0])
    scale = jnp.where(is_ctx, scc_ref[...], sc_ref[0])
    h = _norm_mod(xin, shift, scale).astype(BF16)
    proj = jnp.dot(h, win_ref[...], preferred_element_type=F32)
    u_ref[0] = proj[:, 512:].astype(BF16)

    c_kv = _rms(proj[:, Q_LORA:Q_LORA + KV_LORA], kvn_ref[...], KV_LORA).astype(BF16)
    lane = lax.broadcasted_iota(jnp.int32, (1, MLA_HEADS * HEAD_PAD), 1)
    ones_lane = (lane % HEAD_PAD == V_HEAD).astype(F32)
    v_ref[0] = (jnp.dot(c_kv, wuv_ref[...], preferred_element_type=F32) + ones_lane).astype(BF16)
    kin = jnp.concatenate([c_kv, proj[:, 384:512].astype(BF16)], axis=1)
    kf = jnp.dot(kin, wk_ref[...], preferred_element_type=F32)
    _norm_rope_heads(kf, kg_ref, ka_ref[...], kb_ref[...], k_ref)

    @pl.when(j < A0_NT)
    def _():
        ql = _rms(proj[:, :Q_LORA], qn_ref[...], Q_LORA).astype(BF16)
        qf = jnp.dot(ql, wuq_ref[...], preferred_element_type=F32)
        _norm_rope_heads(qf, qg_ref, qa_ref[...], qb_ref[...], q_ref)


def _a0_call(x, ctx, sh, sc, shc, scc, w):
    nt = A0_NT
    lat = lambda b, j: (b, jnp.minimum(j, nt - 1), 0)
    full2 = lambda b, j: (0, 0)
    per_b = lambda b, j: (b, 0, 0)
    tab = pl.BlockSpec((A0_TL, HEAD_PAD), lambda b, j: (j, 0))
    in_specs = [
        pl.BlockSpec((1, A0_TL, D_MODEL), lat),
        pl.BlockSpec((1, CTX_LEN, D_MODEL), per_b),
        pl.BlockSpec((1, 1, D_MODEL), per_b),
        pl.BlockSpec((1, 1, D_MODEL), per_b),
        pl.BlockSpec((1, D_MODEL), full2),
        pl.BlockSpec((1, D_MODEL), full2),
        pl.BlockSpec((D_MODEL, 1024), full2),
        pl.BlockSpec((1, Q_LORA), full2),
        pl.BlockSpec((Q_LORA, 2 * MLA_HEADS * HEAD_PAD), full2),
        pl.BlockSpec((1, KV_LORA), full2),
        pl.BlockSpec((2 * KV_LORA, 2 * MLA_HEADS * HEAD_PAD), full2),
        pl.BlockSpec((KV_LORA, MLA_HEADS * HEAD_PAD), full2),
        pl.BlockSpec((1, MLA_HEADS * HEAD_PAD), full2),
        pl.BlockSpec((1, MLA_HEADS * HEAD_PAD), full2),
        tab, tab, tab, tab,
    ]
    out_specs = [
        pl.BlockSpec((1, A0_TL, MLA_HEADS * HEAD_PAD), lat),
        pl.BlockSpec((1, A0_TL, MLA_HEADS * HEAD_PAD), lambda b, j: (b, j, 0)),
        pl.BlockSpec((1, A0_TL, MLA_HEADS * HEAD_PAD), lambda b, j: (b, j, 0)),
        pl.BlockSpec((1, A0_TL, S5_WIDTH), lambda b, j: (b, j, 0)),
    ]
    out_shape = [
        jax.ShapeDtypeStruct((BATCH, SEQ, MLA_HEADS * HEAD_PAD), BF16),
        jax.ShapeDtypeStruct((BATCH, KV_LEN, MLA_HEADS * HEAD_PAD), BF16),
        jax.ShapeDtypeStruct((BATCH, KV_LEN, MLA_HEADS * HEAD_PAD), BF16),
        jax.ShapeDtypeStruct((BATCH, KV_LEN, S5_WIDTH), BF16),
    ]
    return pl.pallas_call(
        _a0_kernel, grid=(BATCH, nt + 1), in_specs=in_specs, out_specs=out_specs, out_shape=out_shape,
        compiler_params=_params(("parallel", "arbitrary")), name="even_in_proj",
    )(x, ctx, sh, sc, shc, scc, *w)


def _gain_swap(w, gain):
    half = QK_ROPE // 2
    wg = (w * gain).reshape(w.shape[0], MLA_HEADS, HEAD_PAD)
    re, im = wg[..., QK_NOPE:QK_NOPE + half], wg[..., QK_NOPE + half:QK_HEAD]
    out = jnp.concatenate([jnp.zeros_like(wg[..., :QK_NOPE]), im, re, jnp.zeros_like(wg[..., QK_HEAD:])], axis=-1)
    return out.reshape(w.shape)


def _a0_weights(w_in, q_norm, w_uq, kv_norm, w_ukv, q_qk, k_qk):
    perm = _rope_perm()
    kr0 = Q_LORA + KV_LORA
    w_cat = jnp.concatenate([
        w_in[:, :kr0], w_in[:, kr0:kr0 + QK_ROPE][:, perm],
        jnp.zeros((D_MODEL, HEAD_PAD - QK_ROPE), F32), w_in[:, kr0 + QK_ROPE:]], axis=1).astype(BF16)
    pad = HEAD_PAD - QK_HEAD

    def head_gain(g):
        gh = jnp.concatenate([g[:QK_NOPE], g[QK_NOPE:][perm], jnp.zeros((pad,), F32)])
        return jnp.tile(gh, MLA_HEADS)[None, :]

    uq = w_uq.reshape(Q_LORA, MLA_HEADS, QK_HEAD)
    uq = jnp.concatenate([uq[..., :QK_NOPE], uq[..., QK_NOPE:][..., perm],
                          jnp.zeros((Q_LORA, MLA_HEADS, pad), F32)], axis=-1)
    uq = uq.reshape(Q_LORA, MLA_HEADS * HEAD_PAD)
    uq = jnp.concatenate([uq, _gain_swap(uq, head_gain(q_qk))], axis=1).astype(BF16)
    ukv = w_ukv.reshape(KV_LORA, MLA_HEADS, QK_NOPE + V_HEAD)
    uk = jnp.concatenate([ukv[..., :QK_NOPE], jnp.zeros((KV_LORA, MLA_HEADS, HEAD_PAD - QK_NOPE), F32)], axis=-1)
    uk = uk.reshape(KV_LORA, MLA_HEADS * HEAD_PAD)
    place = np.zeros((KV_LORA, MLA_HEADS, HEAD_PAD), np.float32)
    for i in range(QK_ROPE):
        place[i, :, QK_NOPE + i] = 1.0
    wk = jnp.concatenate([uk, jnp.asarray(place.reshape(KV_LORA, MLA_HEADS * HEAD_PAD))], axis=0)
    wk = jnp.concatenate([wk, _gain_swap(wk, head_gain(k_qk))], axis=1).astype(BF16)
    wuv = jnp.concatenate([ukv[..., QK_NOPE:], jnp.zeros((KV_LORA, MLA_HEADS, HEAD_PAD - V_HEAD), F32)], axis=-1)
    wuv = wuv.reshape(KV_LORA, MLA_HEADS * HEAD_PAD).astype(BF16)
    a, b = _rope_tables()
    qs = QK_HEAD ** -0.5 * math.log2(math.e)
    tabs = [jnp.asarray(t, F32) for t in (a, b, a * qs, b * qs)]
    return [w_cat, q_norm[None, :], uq, kv_norm[None, :], wk, wuv, head_gain(q_qk), head_gain(k_qk)] + tabs


ATT_TQ = 256
HEADS_PER_STEP = 4


def _attn_kernel(q_ref, k_ref, v_ref, o_ref):
    for hh in range(HEADS_PER_STEP):
        sl = slice(hh * HEAD_PAD, (hh + 1) * HEAD_PAD)
        s = lax.dot_general(q_ref[0, :, sl], k_ref[0, :, sl], (((1,), (1,)), ((), ())),
                            preferred_element_type=F32)
        m = jnp.max(s, axis=-1, keepdims=True)
        p = jnp.exp2(s - m).astype(BF16)
        acc = jnp.dot(p, v_ref[0, :, sl], preferred_element_type=F32)
        o_ref[0, :, sl] = (acc * (1.0 / acc[:, V_HEAD:V_HEAD + 1])).astype(BF16)


def _attn_call(q, k, v):
    wq = HEADS_PER_STEP * HEAD_PAD
    return pl.pallas_call(
        _attn_kernel, grid=(BATCH, MLA_HEADS // HEADS_PER_STEP, SEQ // ATT_TQ),
        in_specs=[pl.BlockSpec((1, ATT_TQ, wq), lambda b, h, i: (b, i, h)),
                  pl.BlockSpec((1, KV_LEN, wq), lambda b, h, i: (b, 0, h)),
                  pl.BlockSpec((1, KV_LEN, wq), lambda b, h, i: (b, 0, h))],
        out_specs=pl.BlockSpec((1, ATT_TQ, wq), lambda b, h, i: (b, i, h)),
        out_shape=jax.ShapeDtypeStruct((BATCH, SEQ, MLA_HEADS * HEAD_PAD), BF16),
        compiler_params=_params(("parallel", "parallel", "arbitrary")), name="mla_attention",
    )(q, k, v)


S5_ROWS = S5_NCHUNK * BATCH
S5_ROWS_LAT = S5_NCHUNK_LAT * BATCH
S5_COLS = S5_CHUNK * S5_GROUP
S5_SW = 2 * S5_STATE


def _s5_kernel(u_ref, r_ref, mb_ref, mc_ref, coef_ref, y_ref, x_sc, sp_sc, t_sc):
    u = u_ref[0]
    lags = r_ref[0]
    for sg in range(S5_CHUNK):
        off = (S5_CHUNK - 1 - sg) * S5_GROUP
        t_sc[sg * S5_GROUP:(sg + 1) * S5_GROUP, :] = lags[:, off:off + S5_COLS].astype(BF16)
    x_sc[...] = jnp.dot(u, mb_ref[0], preferred_element_type=F32)
    cf = coef_ref[0]
    af, bfm, bfp, ab, bbm, bbp = [cf[i * 8:(i + 1) * 8] for i in range(6)]

    def body(i, carry):
        sf, sfw, sb, sbw = carry
        cfw = jnp.where(i < S5_NCHUNK_CTX, i + S5_NCHUNK_LAT, i - S5_NCHUNK_CTX)
        rf = pl.multiple_of(cfw * BATCH, BATCH)
        rb = pl.multiple_of((S5_NCHUNK - 1 - i) * BATCH, BATCH)
        sp_sc[pl.ds(rf, BATCH), 0:S5_SW] = sf
        sp_sc[pl.ds(rb, BATCH), S5_SW:2 * S5_SW] = sb
        xf = x_sc[pl.ds(rf, BATCH), 0:S5_SW]
        xfw = x_sc[pl.ds(rf, BATCH), S5_SW:2 * S5_SW]
        xb = x_sc[pl.ds(rb, BATCH), 2 * S5_SW:3 * S5_SW]
        xbw = x_sc[pl.ds(rb, BATCH), 3 * S5_SW:4 * S5_SW]
        return (sf * af + sfw * bfm + xf, sfw * af + sf * bfp + xfw,
                sb * ab + sbw * bbm + xb, sbw * ab + sb * bbp + xbw)

    z = jnp.zeros((BATCH, S5_SW), F32)
    lax.fori_loop(0, S5_NCHUNK, body, (z, z, z, z))
    y = jnp.dot(u[:S5_ROWS_LAT], t_sc[...], preferred_element_type=F32)
    y = y + jnp.dot(sp_sc[0:S5_ROWS_LAT, :].astype(BF16), mc_ref[0], preferred_element_type=F32)
    y_ref[0] = y.astype(BF16)


def _s5_call(ug, t, mb, mc, coef):
    g3 = lambda g: (g, 0, 0)
    return pl.pallas_call(
        _s5_kernel, grid=(S5_GROUPS,),
        in_specs=[pl.BlockSpec((1, S5_ROWS, S5_COLS), g3),
                  pl.BlockSpec((1, S5_GROUP, 2 * S5_COLS), g3),
                  pl.BlockSpec((1, S5_COLS, 4 * S5_SW), g3),
                  pl.BlockSpec((1, 2 * S5_SW, S5_COLS), g3),
                  pl.BlockSpec((1, 6 * 8, S5_SW), g3)],
        out_specs=pl.BlockSpec((1, S5_ROWS_LAT, S5_COLS), g3),
        out_shape=jax.ShapeDtypeStruct((S5_GROUPS, S5_ROWS_LAT, S5_COLS), BF16),
        scratch_shapes=[pltpu.VMEM((S5_ROWS, 4 * S5_SW), F32), pltpu.VMEM((S5_ROWS, 2 * S5_SW), F32),
                        pltpu.VMEM((S5_COLS, S5_COLS), BF16)],
        compiler_params=_params(("parallel",)), name="s5_chunked_scan",
    )(ug, t, mb, mc, coef)


def _s5_weights(lam_re, lam_im, log_step, b_re, b_im, c_re, c_im):
    q = S5_CHUNK
    hi = lax.Precision.HIGHEST
    t_blocks, mbs, mcs, coefs = [], [], [], []
    sig = jnp.arange(q)
    for d in range(2):
        lr = jnp.minimum(lam_re[d], -1e-4)
        li = lam_im[d]
        step = jnp.exp(log_step[d])[:, None]
        jj = jnp.arange(q + 1, dtype=F32)[:, None, None]
        mag = jnp.exp(lr * step * jj)
        ph = li * step * jj
        pr, pi = mag * jnp.cos(ph), mag * jnp.sin(ph)
        nr, ni = pr[1] - 1.0, pi[1]
        den = lr * lr + li * li
        fr, fi = (nr * lr + ni * li) / den, (ni * lr - nr * li) / den
        br = fr[..., None] * b_re[d] - fi[..., None] * b_im[d]
        bi = fr[..., None] * b_im[d] + fi[..., None] * b_re[d]
        cr, ci = c_re[d], c_im[d]
        cpr = cr[None] * pr[:, :, None, :] - ci[None] * pi[:, :, None, :]
        cpi = cr[None] * pi[:, :, None, :] + ci[None] * pr[:, :, None, :]
        kern = jnp.einsum('jghp,gpk->jghk', jnp.concatenate([cpr[:q], -cpi[:q]], axis=-1),
                          jnp.concatenate([br, bi], axis=1), precision=hi)
        kt = kern.transpose(1, 3, 0, 2)
        zero_slots = jnp.zeros((S5_GROUPS, S5_GROUP, q, S5_GROUP), F32)
        if d == 0:
            t_blocks.append(jnp.concatenate([zero_slots[:, :, :q - 1], kt, zero_slots[:, :, :1]], axis=2))
        else:
            t_blocks.append(jnp.concatenate([kt[:, :, ::-1], zero_slots], axis=2))
        pw = (q - 1 - sig) if d == 0 else sig
        xr = pr[pw][..., None] * br[None] - pi[pw][..., None] * bi[None]
        xi = pr[pw][..., None] * bi[None] + pi[pw][..., None] * br[None]
        xr = xr.transpose(1, 0, 3, 2).reshape(S5_GROUPS, S5_COLS, S5_STATE)
        xi = xi.transpose(1, 0, 3, 2).reshape(S5_GROUPS, S5_COLS, S5_STATE)
        mbs += [xr, xi, xi, xr]
        po = (sig + 1) if d == 0 else (q - sig)
        mr = cpr[po].transpose(1, 3, 0, 2).reshape(S5_GROUPS, S5_STATE, S5_COLS)
        mi = -cpi[po].transpose(1, 3, 0, 2).reshape(S5_GROUPS, S5_STATE, S5_COLS)
        mcs += [mr, mi]
        are, aim = pr[q], pi[q]
        rows = [jnp.concatenate([are, are], -1), jnp.concatenate([-aim, aim], -1), jnp.concatenate([aim, -aim], -1)]
        coefs += [jnp.broadcast_to(r[:, None, :], (S5_GROUPS, 8, S5_SW)) for r in rows]
    t = (t_blocks[0] + t_blocks[1]).reshape(S5_GROUPS, S5_GROUP, 2 * S5_COLS)
    mb = jnp.concatenate(mbs, axis=-1).astype(BF16)
    mc = jnp.concatenate(mcs, axis=1).astype(BF16)
    coef = jnp.concatenate(coefs, axis=1)
    return t, mb, mc, coef


A1_TL = 512


def _a1_kernel(att_ref, ys_ref, u_ref, d_ref, gw_ref, gb_ref, woa_ref, wos_ref, x_ref, g_ref, *route):
    o_ref = route[5]
    y = u_ref[0].astype(F32) * d_ref[...] + ys_ref[0].astype(F32)
    z = jax.nn.gelu(y)
    gate = jax.nn.sigmoid(jnp.dot(z.astype(BF16), gw_ref[...], preferred_element_type=F32) + gb_ref[...])
    s5 = (z * gate).astype(BF16)
    mix = jnp.dot(att_ref[0], woa_ref[...], preferred_element_type=F32)
    mix = mix + jnp.dot(s5, wos_ref[...], preferred_element_type=F32)
    x_new = x_ref[0] + g_ref[0] * mix
    o_ref[0] = x_new
    _route_tile(x_new, *route[:5], *route[6:])


def _a1_call(att, ys, u, d, gw, gb, woa, wos, x, g1, route):
    r_args, r_in, r_out, r_shape, r_scratch = route
    t3 = lambda b, j: (b, j, 0)
    full2 = lambda b, j: (0, 0)
    return pl.pallas_call(
        _a1_kernel, grid=(BATCH, SEQ // A1_TL),
        in_specs=[pl.BlockSpec((1, A1_TL, MLA_HEADS * HEAD_PAD), t3),
                  pl.BlockSpec((1, A1_TL, S5_WIDTH), t3),
                  pl.BlockSpec((1, A1_TL, S5_WIDTH), t3),
                  pl.BlockSpec((1, S5_WIDTH), full2),
                  pl.BlockSpec((S5_WIDTH, S5_WIDTH), full2),
                  pl.BlockSpec((1, S5_WIDTH), full2),
                  pl.BlockSpec((MLA_HEADS * HEAD_PAD, D_MODEL), full2),
                  pl.BlockSpec((S5_WIDTH, D_MODEL), full2),
                  pl.BlockSpec((1, A1_TL, D_MODEL), t3),
                  pl.BlockSpec((1, 1, D_MODEL), lambda b, j: (b, 0, 0))] + r_in,
        out_specs=[pl.BlockSpec((1, A1_TL, D_MODEL), t3)] + r_out,
        out_shape=[jax.ShapeDtypeStruct((BATCH, SEQ, D_MODEL), F32)] + r_shape,
        scratch_shapes=r_scratch,
        compiler_params=_params(("arbitrary", "arbitrary")), name="even_out_proj_route",
    )(att, ys, u, d, gw, gb, woa, wos, x, g1, *r_args)


MOE_TL = 512
SLOT_PAD = 8


def _pack_rows(v):
    halves = []
    for p in range(2):
        base = 2 * p * ROW_WORDS
        a = pltpu.bitcast(v[:, base:base + ROW_WORDS].astype(BF16).astype(F32), jnp.uint32)
        b = pltpu.bitcast(v[:, base + ROW_WORDS:base + 2 * ROW_WORDS].astype(BF16).astype(F32), jnp.uint32)
        halves.append((a >> 16) | b)
    return halves


def _unpack_rows(lo, hi):
    out = []
    for w in (lo, hi):
        out.append(pltpu.bitcast(w << 16, F32))
        out.append(pltpu.bitcast(w & jnp.uint32(0xFFFF0000), F32))
    return out


def _route_tile(x, sh_ref, sc_ref, rwt_ref, rb_ref, tri_ref,
                hlo_ref, hhi_ref, idx_ref, wt_ref, rank_ref, cnt_ref, run_sc):
    @pl.when((pl.program_id(0) == 0) & (pl.program_id(1) == 0))
    def _():
        run_sc[...] = jnp.zeros_like(run_sc)

    h = _norm_mod(x, sh_ref[0], sc_ref[0])
    hb = h.astype(BF16)
    hlo_ref[0], hhi_ref[0] = _pack_rows(h)
    h_lo = (h - hb.astype(F32)).astype(BF16)
    rwt = rwt_ref[...]
    rw_hi = rwt.astype(BF16)
    rw_lo = (rwt - rw_hi.astype(F32)).astype(BF16)
    logits = lax.dot_general(jnp.concatenate([rw_hi, rw_lo, rw_hi], axis=1), jnp.concatenate([hb, hb, h_lo], axis=1),
                             (((1,), (1,)), ((), ())), preferred_element_type=F32)
    scores = jax.nn.sigmoid(logits)

    work = scores + rb_ref[...]
    expert = lax.broadcasted_iota(jnp.int32, work.shape, 0).astype(F32)
    hits, ids = [], []
    for _ in range(TOP_K):
        m = jnp.max(work, axis=0, keepdims=True)
        ik = jnp.min(jnp.where(work == m, expert, float(N_EXPERTS)), axis=0, keepdims=True)
        hit = expert == ik
        hits.append(hit)
        ids.append(ik)
        work = jnp.where(hit, -jnp.inf, work)
    mask = hits[0]
    for hit in hits[1:]:
        mask = jnp.logical_or(mask, hit)
    maskf = mask.astype(F32)
    before = jnp.dot(maskf.astype(BF16), tri_ref[...], preferred_element_type=F32) + run_sc[:, 0:1]
    sel = [jnp.sum(jnp.where(hit, scores, 0.0), axis=0, keepdims=True) for hit in hits]
    denom = sel[0]
    for s in sel[1:]:
        denom = denom + s
    ranks = [jnp.sum(jnp.where(hit, before, 0.0), axis=0, keepdims=True) for hit in hits]
    pad = [jnp.zeros_like(denom)] * (SLOT_PAD - TOP_K)
    idx_ref[...] = jnp.concatenate(ids + pad, axis=0).astype(jnp.int32)
    wt_ref[...] = jnp.concatenate([s / denom * ROUTE_SCALE for s in sel] + pad, axis=0)
    rank_ref[...] = jnp.concatenate(ranks + pad, axis=0).astype(jnp.int32)
    run_sc[...] += jnp.sum(maskf, axis=1, keepdims=True)
    cnt_ref[...] = run_sc[...]


def _route_plumbing(sh, sc, rw, rb):
    t3 = lambda b, j: (b, j, 0)
    full2 = lambda b, j: (0, 0)
    per_b = lambda b, j: (b, 0, 0)
    nt = SEQ // MOE_TL
    slots = lambda b, j: (0, b * nt + j)
    tri = jnp.asarray(np.triu(np.ones((MOE_TL, MOE_TL), np.float32), 1), BF16)
    args = [sh, sc, rw.T, rb[:, None], tri]
    in_specs = [pl.BlockSpec((1, 1, D_MODEL), per_b),
                pl.BlockSpec((1, 1, D_MODEL), per_b),
                pl.BlockSpec((N_EXPERTS, D_MODEL), full2),
                pl.BlockSpec((N_EXPERTS, 1), full2),
                pl.BlockSpec((MOE_TL, MOE_TL), full2)]
    out_specs = [pl.BlockSpec((1, MOE_TL, ROW_WORDS), t3),
                 pl.BlockSpec((1, MOE_TL, ROW_WORDS), t3),
                 pl.BlockSpec((SLOT_PAD, MOE_TL), slots),
                 pl.BlockSpec((SLOT_PAD, MOE_TL), slots),
                 pl.BlockSpec((SLOT_PAD, MOE_TL), slots),
                 pl.BlockSpec((N_EXPERTS, 128), full2)]
    slot_i = jax.ShapeDtypeStruct((SLOT_PAD, BATCH * SEQ), jnp.int32)
    out_shape = [jax.ShapeDtypeStruct((BATCH, SEQ, ROW_WORDS), jnp.uint32),
                 jax.ShapeDtypeStruct((BATCH, SEQ, ROW_WORDS), jnp.uint32),
                 slot_i,
                 jax.ShapeDtypeStruct((SLOT_PAD, BATCH * SEQ), F32),
                 slot_i,
                 jax.ShapeDtypeStruct((N_EXPERTS, 128), F32)]
    scratch = [pltpu.VMEM((N_EXPERTS, 128), F32)]
    return args, in_specs, out_specs, out_shape, scratch


def _sc_mesh():
    return plsc.VectorSubcoreMesh(core_axis_name="c", subcore_axis_name="s")


def _sc_dispatch(h_words, dest, n_rows):
    n_tok = h_words.shape[0]

    @pl.kernel(out_type=jax.ShapeDtypeStruct((n_rows, ROW_WORDS), jnp.uint32), mesh=_sc_mesh(), scratch_types=[])
    def scatter_rows(h_hbm, i_hbm, o_hbm):
        def body(h_vmem, i_vmem):
            for k in range(TOP_K):
                pltpu.sync_copy(h_vmem, o_hbm.at[i_vmem.at[k]])

        pltpu.emit_pipeline(
            body, grid=(n_tok // SC_WINDOW,),
            in_specs=[pl.BlockSpec((SC_WINDOW, ROW_WORDS), index_map=lambda i: (i, 0)),
                      pl.BlockSpec((SLOT_PAD, SC_WINDOW), index_map=lambda i: (0, i))],
            out_specs=[],
            core_axis_name=("c", "s"), dimension_semantics=(pltpu.PARALLEL,),
        )(h_hbm, i_hbm)

    return scatter_rows(h_words, dest)


def _sc_collect(y_words, dest):
    n_tok = dest.shape[1]

    @pl.kernel(out_type=jax.ShapeDtypeStruct((TOP_K, n_tok, ROW_WORDS), jnp.uint32), mesh=_sc_mesh(),
               scratch_types=[])
    def gather_rows(y_hbm, i_hbm, o_hbm):
        def body(i_vmem, o_vmem):
            pltpu.sync_copy(y_hbm.at[i_vmem.at[0]], o_vmem.at[0])

        pltpu.emit_pipeline(
            body, grid=(TOP_K, n_tok // SC_WINDOW),
            in_specs=[pl.BlockSpec((1, SC_WINDOW), index_map=lambda k, i: (k, i))],
            out_specs=[pl.BlockSpec((1, SC_WINDOW, ROW_WORDS), index_map=lambda k, i: (k, i, 0))],
            core_axis_name=("c", "s"), dimension_semantics=(pltpu.PARALLEL, pltpu.PARALLEL),
        )(i_hbm, o_hbm)

    return gather_rows(y_words, dest)


def _expert_kernel(be_ref, nv_ref, xlo_ref, xhi_ref, wg_ref, wu_ref, wd_ref, ylo_ref, yhi_ref,
                   wg_sc, wu_sc, wd_sc):
    i = pl.program_id(0)
    nv = nv_ref[i]

    @pl.when(jnp.logical_or(i == 0, be_ref[i] != be_ref[jnp.maximum(i - 1, 0)]))
    def _():
        wg_sc[...] = wg_ref[0, 0].astype(BF16)
        wu_sc[...] = wu_ref[0, 0].astype(BF16)
        wd_sc[...] = wd_ref[0, 0].astype(BF16)

    @pl.when(nv > 0)
    def _():
        parts = _unpack_rows(xlo_ref[...], xhi_ref[...])
        xb = jnp.concatenate([p.astype(BF16) for p in parts], axis=1)
        live = lax.broadcasted_iota(jnp.int32, xb.shape, 0) < nv
        xb = jnp.where(live, xb, jnp.zeros_like(xb))
        hid = jax.nn.silu(jnp.dot(xb, wg_sc[...], preferred_element_type=F32))
        hid = hid * jnp.dot(xb, wu_sc[...], preferred_element_type=F32)
        y = jnp.dot(hid.astype(BF16), wd_sc[...], preferred_element_type=F32)
        ylo_ref[...], yhi_ref[...] = _pack_rows(y)

    @pl.when(nv == 0)
    def _():
        ylo_ref[...] = jnp.zeros_like(ylo_ref)
        yhi_ref[...] = jnp.zeros_like(yhi_ref)


def _expert_call(block_e, n_valid, xlo, xhi, wg, wu, wd, li):
    n_rows = xlo.shape[0]
    n_blocks = n_rows // MOE_BLOCK
    rows = pl.BlockSpec((MOE_BLOCK, ROW_WORDS), lambda i, be, nv: (i, 0))
    grid_spec = pltpu.PrefetchScalarGridSpec(
        num_scalar_prefetch=2, grid=(n_blocks,),
        in_specs=[rows, rows,
                  pl.BlockSpec((1, 1, D_MODEL, EXPERT_FF), lambda i, be, nv: (li, be[i], 0, 0)),
                  pl.BlockSpec((1, 1, D_MODEL, EXPERT_FF), lambda i, be, nv: (li, be[i], 0, 0)),
                  pl.BlockSpec((1, 1, EXPERT_FF, D_MODEL), lambda i, be, nv: (li, be[i], 0, 0))],
        out_specs=[rows, rows],
        scratch_shapes=[pltpu.VMEM((D_MODEL, EXPERT_FF), BF16), pltpu.VMEM((D_MODEL, EXPERT_FF), BF16),
                        pltpu.VMEM((EXPERT_FF, D_MODEL), BF16)])
    out = jax.ShapeDtypeStruct((n_rows, ROW_WORDS), jnp.uint32)
    return pl.pallas_call(
        _expert_kernel, grid_spec=grid_spec, out_shape=[out, out],
        compiler_params=_params(("arbitrary",)), name="moe_experts",
    )(block_e, n_valid, xlo, xhi, wg, wu, wd)


def _combine_kernel(ylo_ref, yhi_ref, w_ref, x_ref, sh_ref, sc_ref, g_ref, sg_ref, su_ref, sd_ref, o_ref):
    hb = _norm_mod(x_ref[0], sh_ref[0], sc_ref[0]).astype(BF16)
    hid = jax.nn.silu(jnp.dot(hb, sg_ref[...].astype(BF16), preferred_element_type=F32))
    hid = hid * jnp.dot(hb, su_ref[...].astype(BF16), preferred_element_type=F32)
    shared = jnp.dot(hid.astype(BF16), sd_ref[...].astype(BF16), preferred_element_type=F32)
    w = w_ref[0]
    acc = [None] * 4
    for k in range(TOP_K):
        wk = w[:, k:k + 1]
        for c, part in enumerate(_unpack_rows(ylo_ref[k], yhi_ref[k])):
            acc[c] = wk * part if acc[c] is None else acc[c] + wk * part
    for c in range(4):
        sl = slice(c * ROW_WORDS, (c + 1) * ROW_WORDS)
        o_ref[0, :, sl] = x_ref[0, :, sl] + g_ref[0, :, sl] * (acc[c] + shared[:, sl])


def _combine_call(ylo, yhi, wts, x, sh, sc, g2, sg, su, sd):
    t3 = lambda b, j: (b, j, 0)
    per_b = lambda b, j: (b, 0, 0)
    full2 = lambda b, j: (0, 0)
    ff = sg.shape[1]
    nt = SEQ // MOE_TL
    rows = pl.BlockSpec((TOP_K, MOE_TL, ROW_WORDS), lambda b, j: (0, b * nt + j, 0))
    return pl.pallas_call(
        _combine_kernel, grid=(BATCH, nt),
        in_specs=[rows, rows,
                  pl.BlockSpec((1, MOE_TL, SLOT_PAD), t3),
                  pl.BlockSpec((1, MOE_TL, D_MODEL), t3),
                  pl.BlockSpec((1, 1, D_MODEL), per_b),
                  pl.BlockSpec((1, 1, D_MODEL), per_b),
                  pl.BlockSpec((1, 1, D_MODEL), per_b),
                  pl.BlockSpec((D_MODEL, ff), full2),
                  pl.BlockSpec((D_MODEL, ff), full2),
                  pl.BlockSpec((ff, D_MODEL), full2)],
        out_specs=pl.BlockSpec((1, MOE_TL, D_MODEL), t3),
        out_shape=jax.ShapeDtypeStruct((BATCH, SEQ, D_MODEL), F32),
        compiler_params=_params(("parallel", "arbitrary")), name="moe_combine_shared",
    )(ylo, yhi, wts, x, sh, sc, g2, sg, su, sd)


def _moe(x, routed, sh, sc, g2, w_gate, w_up, w_down, sh_gate, sh_up, sh_down, li):
    T = BATCH * SEQ
    TK = T * TOP_K
    hlo, hhi, idx, wts, rank, counts = routed
    wts = wts.T.reshape(BATCH, SEQ, SLOT_PAD)
    counts = counts[:, 0].astype(jnp.int32)
    padded = (counts + MOE_BLOCK - 1) // MOE_BLOCK * MOE_BLOCK
    pad_end = jnp.cumsum(padded)
    pad_start = pad_end - padded
    n_blocks = -(-TK // MOE_BLOCK) + N_EXPERTS
    n_rows = n_blocks * MOE_BLOCK
    block_start = jnp.arange(n_blocks, dtype=jnp.int32) * MOE_BLOCK
    owns = jnp.logical_and(block_start[:, None] >= pad_start[None, :], block_start[:, None] < pad_end[None, :])
    owns = owns.astype(jnp.int32)
    experts = jnp.arange(N_EXPERTS, dtype=jnp.int32)[None, :]
    block_e = jnp.sum(owns * experts, axis=1) + (N_EXPERTS - 1) * (1 - jnp.sum(owns, axis=1))
    n_valid = jnp.sum(owns * (counts[None, :] - (block_start[:, None] - pad_start[None, :])), axis=1)
    n_valid = jnp.clip(n_valid, 0, MOE_BLOCK).astype(jnp.int32)
    first_row = jnp.sum(jnp.where(idx[None] == experts.T[:, :, None], pad_start[:, None, None], 0), axis=0)
    dest = first_row + rank
    xlo = _sc_dispatch(hlo.reshape(T, ROW_WORDS), dest, n_rows)
    xhi = _sc_dispatch(hhi.reshape(T, ROW_WORDS), dest, n_rows)
    ylo, yhi = _expert_call(block_e, n_valid, xlo, xhi, w_gate, w_up, w_down, li)
    return _combine_call(_sc_collect(ylo, dest), _sc_collect(yhi, dest), wts, x, sh, sc, g2,
                         sh_gate, sh_up, sh_down)


HY_TL = 512
HALO = 8


def _hy_in_kernel(x_ref, xp_ref, xn_ref, sh_ref, sc_ref, w_ref, cw_ref, cb_ref, z_ref, x0_ref, h_sc):
    j = pl.program_id(1)
    shift, scale = sh_ref[0], sc_ref[0]
    keep_prev = (j > 0).astype(F32)
    keep_next = (j < SEQ // HY_TL - 1).astype(F32)
    h_sc[0:HALO, :] = _norm_mod(xp_ref[0], shift, scale) * keep_prev
    h_sc[HALO:HALO + HY_TL, :] = _norm_mod(x_ref[0], shift, scale)
    h_sc[HALO + HY_TL:, :] = _norm_mod(xn_ref[0], shift, scale) * keep_next
    hcat = h_sc[...].astype(BF16)
    outs = []
    for part in range(3):
        sl = slice(part * HY_WIDTH, (part + 1) * HY_WIDTH)
        p = jnp.dot(hcat, w_ref[:, sl], preferred_element_type=F32)
        o = (p[HALO - 1:HALO - 1 + HY_TL] * cw_ref[0:1, sl] + p[HALO:HALO + HY_TL] * cw_ref[1:2, sl]
             + p[HALO + 1:HALO + 1 + HY_TL] * cw_ref[2:3, sl] + cb_ref[:, sl])
        outs.append(o)
    x0_ref[0] = outs[0].astype(BF16)
    z_ref[0] = (outs[2] * outs[1]).astype(BF16)


def _hy_in_call(x, sh, sc, w, cw, cb):
    nb8 = HY_TL // HALO
    t3 = lambda b, j: (b, j, 0)
    full2 = lambda b, j: (0, 0)
    per_b = lambda b, j: (b, 0, 0)
    return pl.pallas_call(
        _hy_in_kernel, grid=(BATCH, SEQ // HY_TL),
        in_specs=[pl.BlockSpec((1, HY_TL, D_MODEL), t3),
                  pl.BlockSpec((1, HALO, D_MODEL), lambda b, j: (b, jnp.maximum(j * nb8 - 1, 0), 0)),
                  pl.BlockSpec((1, HALO, D_MODEL), lambda b, j: (b, jnp.minimum((j + 1) * nb8, SEQ // HALO - 1), 0)),
                  pl.BlockSpec((1, 1, D_MODEL), per_b),
                  pl.BlockSpec((1, 1, D_MODEL), per_b),
                  pl.BlockSpec((D_MODEL, 3 * HY_WIDTH), full2),
                  pl.BlockSpec((SHORT_CONV, 3 * HY_WIDTH), full2),
                  pl.BlockSpec((1, 3 * HY_WIDTH), full2)],
        out_specs=[pl.BlockSpec((1, HY_TL, HY_WIDTH), t3), pl.BlockSpec((1, HY_TL, HY_WIDTH), t3)],
        out_shape=[jax.ShapeDtypeStruct((BATCH, SEQ, HY_WIDTH), BF16),
                   jax.ShapeDtypeStruct((BATCH, SEQ, HY_WIDTH), BF16)],
        scratch_shapes=[pltpu.VMEM((HY_TL + 2 * HALO, D_MODEL), F32)],
        compiler_params=_params(("parallel", "arbitrary")), name="hyena_in_proj",
    )(x, x, x, sh, sc, w, cw, cb)


def _fft_tables():
    c = np.arange(FFT_N2, dtype=np.int64)
    ang = 2.0 * np.pi * ((c[:, None] * c[None, :]) % FFT_N2) / FFT_N2
    sr, si = np.cos(ang), -np.sin(ang)
    m = np.block([[sr, -si], [si, sr]])
    k1 = np.arange(FFT_NK, dtype=np.int64)
    ang_t = 2.0 * np.pi * (k1[:, None] * c[None, :]) / DFT_N
    lanes = np.ones((1, 1, 128))
    tr = np.cos(ang_t)[:, :, None] * lanes
    ti = -np.sin(ang_t)[:, :, None] * lanes
    return jnp.asarray(m, BF16), jnp.asarray(tr, F32), jnp.asarray(ti, F32)


def _lin(acc, coef, val):
    if abs(coef) < 1e-12:
        return acc
    term = val if coef == 1.0 else (-val if coef == -1.0 else coef * val)
    return term if acc is None else acc + term


def _twiddle(tr_ref, ti_ref, k1, width):
    reps = width // 128
    tr, ti = tr_ref[k1], ti_ref[k1]
    return jnp.concatenate([tr] * reps, axis=1), jnp.concatenate([ti] * reps, axis=1)


_COS_PI_4 = math.sqrt(0.5)


def _blocks_to_classes(z):
    assert FFT_N1 == 16
    sp = [z[a] + z[a + 4] for a in range(4)]
    sm = [z[a] - z[a + 4] for a in range(4)]
    q0p, q0m, q1p, q1m = sp[0] + sp[2], sp[0] - sp[2], sp[1] + sp[3], sp[1] - sp[3]
    d, e = _COS_PI_4 * (sm[1] - sm[3]), _COS_PI_4 * (sm[1] + sm[3])
    out = [None] * FFT_NK
    out[0] = (q0p + q1p, None)
    out[8] = (q0p - q1p, None)
    out[4] = (q0m, -q1m)
    out[2] = (sm[0] + d, -(sm[2] + e))
    out[6] = (sm[0] - d, sm[2] - e)
    for k in (1, 3):
        halves = []
        for parity in (0, 1):
            hr = hi = None
            for a in range(parity, FFT_NA, 2):
                th = 2.0 * math.pi * ((a * k) % FFT_N1) / FFT_N1
                hr = _lin(hr, round(math.cos(th), 15), z[a])
                hi = _lin(hi, round(-math.sin(th), 15), z[a])
            halves.append((hr, hi))
        (er, ei), (odr, odi) = halves
        out[k] = (er + odr, ei + odi)
        out[8 - k] = (er - odr, odi - ei)
    return out


def _classes_to_blocks(v):
    assert FFT_N1 == 16
    base = (v[0][0] + v[8][0], v[0][0] - v[8][0])
    pr = {k: (v[k][0] + v[8 - k][0], v[k][0] - v[8 - k][0]) for k in (1, 2, 3)}
    pi = {k: (v[k][1] - v[8 - k][1], v[k][1] + v[8 - k][1]) for k in (1, 2, 3)}
    out = []
    for a in range(FFT_NA):
        odd = a % 2
        mid = (v[4][0], 1.0 - (a % 4)) if not odd else (v[4][1], (a % 4) - 2.0)
        acc = _lin(base[odd], mid[1], mid[0])
        for k in (1, 2, 3):
            th = 2.0 * math.pi * ((a * k) % FFT_N1) / FFT_N1
            acc = _lin(acc, round(math.cos(th), 15), pr[k][odd])
            acc = _lin(acc, round(-math.sin(th), 15), pi[k][odd])
        out.append(acc)
    return out


def _class_forward(yr, yi, k1, m_ref, tr_ref, ti_ref, width):
    if k1 == 0:
        x = jnp.dot(m_ref[:, :FFT_N2], yr.astype(BF16), preferred_element_type=F32)
    else:
        tr, ti = _twiddle(tr_ref, ti_ref, k1, width)
        yr, yi = (yr * tr, yr * ti) if yi is None else (yr * tr - yi * ti, yr * ti + yi * tr)
        x = jnp.dot(m_ref[...], jnp.concatenate([yr, yi], axis=0).astype(BF16), preferred_element_type=F32)
    return x[:FFT_N2], x[FFT_N2:]


def _class_backward(pr, pi, k1, m_ref, tr_ref, ti_ref, width):
    v = jnp.dot(m_ref[...], jnp.concatenate([pr, -pi], axis=0).astype(BF16), preferred_element_type=F32)
    ur, ui = v[:FFT_N2], -v[FFT_N2:]
    if k1 > 0:
        tr, ti = _twiddle(tr_ref, ti_ref, k1, width)
        ur, ui = ur * tr + ui * ti, ui * tr - ur * ti
    return ur, ui


def _spec_kernel(hf_ref, hb_ref, m_ref, tr_ref, ti_ref, c_ref):
    rows = lambda ref: [ref[a * FFT_N2:(a + 1) * FFT_N2, :] for a in range(FFT_NA)]
    fwd, bwd = _blocks_to_classes(rows(hf_ref)), _blocks_to_classes(rows(hb_ref))
    for k1 in range(FFT_NK):
        fr, fi = _class_forward(*fwd[k1], k1, m_ref, tr_ref, ti_ref, HY_CT)
        br, bi = _class_forward(*bwd[k1], k1, m_ref, tr_ref, ti_ref, HY_CT)
        scale = (1.0 if k1 in (0, FFT_N1 // 2) else 2.0) / DFT_N
        c_ref[k1, :FFT_N2, :] = ((fr + br) * scale).astype(BF16)
        c_ref[k1, FFT_N2:, :] = ((fi - bi) * scale).astype(BF16)


def _fft_table_specs(ngrid):
    z = (0,) * 2
    z3 = (0,) * 3
    if ngrid == 1:
        return [pl.BlockSpec((2 * FFT_N2, 2 * FFT_N2), lambda c: z),
                pl.BlockSpec((FFT_NK, FFT_N2, 128), lambda c: z3),
                pl.BlockSpec((FFT_NK, FFT_N2, 128), lambda c: z3)]
    return [pl.BlockSpec((2 * FFT_N2, 2 * FFT_N2), lambda b, c: z),
            pl.BlockSpec((FFT_NK, FFT_N2, 128), lambda b, c: z3),
            pl.BlockSpec((FFT_NK, FFT_N2, 128), lambda b, c: z3)]


def _spec_call(hfb, m, tr, ti):
    nct = HY_WIDTH // HY_CT
    return pl.pallas_call(
        _spec_kernel, grid=(nct,),
        in_specs=[pl.BlockSpec((SEQ, HY_CT), lambda c: (0, c)),
                  pl.BlockSpec((SEQ, HY_CT), lambda c: (0, c + nct))] + _fft_table_specs(1),
        out_specs=pl.BlockSpec((FFT_NK, 2 * FFT_N2, HY_CT), lambda c: (0, 0, c)),
        out_shape=jax.ShapeDtypeStruct((FFT_NK, 2 * FFT_N2, HY_WIDTH), BF16),
        compiler_params=_params(("arbitrary",)), name="hyena_filter_spectrum",
    )(hfb, hfb, m, tr, ti)


def _conv_kernel(z_ref, c_ref, m_ref, tr_ref, ti_ref, y_ref, cls):
    for k1, (yr, yi) in enumerate(_blocks_to_classes(
            [z_ref[0, a * FFT_N2:(a + 1) * FFT_N2, :].astype(F32) for a in range(FFT_NA)])):
        cls[k1, :FFT_N2, :] = yr
        if yi is not None:
            cls[k1, FFT_N2:, :] = yi
    for k1 in range(FFT_NK):
        yi = None if k1 in (0, FFT_N1 // 2) else cls[k1, FFT_N2:, :]
        xr, xi = _class_forward(cls[k1, :FFT_N2, :], yi, k1, m_ref, tr_ref, ti_ref, HY_CT)
        cr = c_ref[k1, :FFT_N2, :].astype(F32)
        ci = c_ref[k1, FFT_N2:, :].astype(F32)
        ur, ui = _class_backward(xr * cr - xi * ci, xr * ci + xi * cr, k1, m_ref, tr_ref, ti_ref, HY_CT)
        cls[k1, :FFT_N2, :] = ur
        cls[k1, FFT_N2:, :] = ui
    blocks = _classes_to_blocks([(cls[k1, :FFT_N2, :], cls[k1, FFT_N2:, :]) for k1 in range(FFT_NK)])
    for a, ya in enumerate(blocks):
        y_ref[0, a * FFT_N2:(a + 1) * FFT_N2, :] = ya.astype(BF16)


def _conv_call(z, spec, m, tr, ti):
    return pl.pallas_call(
        _conv_kernel, grid=(HY_WIDTH // HY_CT, BATCH),
        in_specs=[pl.BlockSpec((1, SEQ, HY_CT), lambda c, b: (b, 0, c)),
                  pl.BlockSpec((FFT_NK, 2 * FFT_N2, HY_CT), lambda c, b: (0, 0, c))] + _fft_table_specs(2),
        out_specs=pl.BlockSpec((1, SEQ, HY_CT), lambda c, b: (b, 0, c)),
        out_shape=jax.ShapeDtypeStruct((BATCH, SEQ, HY_WIDTH), BF16),
        scratch_shapes=[pltpu.VMEM((FFT_NK, 2 * FFT_N2, HY_CT), F32)],
        compiler_params=_params(("parallel", "arbitrary")), name="hyena_long_conv",
    )(z, spec, m, tr, ti)


def _hy_out_kernel(y_ref, z_ref, x0_ref, b_ref, w_ref, x_ref, g_ref, *route):
    o_ref = route[5]
    z = z_ref[0].astype(F32)
    gated = x0_ref[0].astype(F32) * (y_ref[0].astype(F32) + b_ref[...] * z)
    mix = jnp.dot(gated.astype(BF16), w_ref[...], preferred_element_type=F32)
    x_new = x_ref[0] + g_ref[0] * mix
    o_ref[0] = x_new
    _route_tile(x_new, *route[:5], *route[6:])


def _hy_out_call(y, z, x0, bias, w, x, g1, route):
    r_args, r_in, r_out, r_shape, r_scratch = route
    t3 = lambda b, j: (b, j, 0)
    full2 = lambda b, j: (0, 0)
    return pl.pallas_call(
        _hy_out_kernel, grid=(BATCH, SEQ // HY_TL),
        in_specs=[pl.BlockSpec((1, HY_TL, HY_WIDTH), t3),
                  pl.BlockSpec((1, HY_TL, HY_WIDTH), t3),
                  pl.BlockSpec((1, HY_TL, HY_WIDTH), t3),
                  pl.BlockSpec((1, HY_WIDTH), full2),
                  pl.BlockSpec((HY_WIDTH, D_MODEL), full2),
                  pl.BlockSpec((1, HY_TL, D_MODEL), t3),
                  pl.BlockSpec((1, 1, D_MODEL), lambda b, j: (b, 0, 0))] + r_in,
        out_specs=[pl.BlockSpec((1, HY_TL, D_MODEL), t3)] + r_out,
        out_shape=[jax.ShapeDtypeStruct((BATCH, SEQ, D_MODEL), F32)] + r_shape,
        scratch_shapes=r_scratch,
        compiler_params=_params(("arbitrary", "arbitrary")), name="hyena_out_proj_route",
    )(y, z, x0, bias, w, x, g1, *r_args)


def _hyena_filter(w1, b1, w2, b2, w3, freq):
    hi = lax.Precision.HIGHEST
    Lq = SEQ
    t = jnp.linspace(0.0, 1.0, Lq, dtype=F32)[:, None]
    ang = 2.0 * math.pi * jnp.arange(Lq, dtype=F32)[:, None] / Lq
    bands = jnp.linspace(1e-4, FILT_BANDS - 1, FILT_BANDS, dtype=F32)
    z = jnp.concatenate([t, jnp.cos(bands * ang), -jnp.sin(bands * ang)], axis=-1)
    hid = jnp.sin(freq * (jnp.dot(z, w1, precision=hi) + b1))
    hid = jnp.sin(freq * (jnp.dot(hid, w2, precision=hi) + b2))
    deltas = jnp.linspace(HY_MIN_DECAY, HY_MAX_DECAY, HY_WIDTH, dtype=F32)
    hf = jnp.dot(hid, w3, precision=hi) * jnp.exp(-t * jnp.tile(deltas, 2))
    ssq = jnp.sum(hf * hf, axis=0)
    ssq = ssq[:HY_WIDTH] + ssq[HY_WIDTH:]
    return hf * jnp.tile(lax.rsqrt(ssq + EPS), 2)


def kernel(x, c, ctx, c_ctx, ada_w, ada_b, ev_w_in, mla_q_norm, mla_w_uq, mla_kv_norm, mla_w_ukv, mla_q_qknorm, mla_k_qknorm, s5_lam_re, s5_lam_im, s5_log_step, s5_b_re, s5_b_im, s5_c_re, s5_c_im, s5_d, s5_glu_w, s5_glu_b, ev_w_out, hy_w_in, hy_conv_w, hy_conv_b, hy_f_w1, hy_f_b1, hy_f_w2, hy_f_b2, hy_f_w3, hy_f_freq, hy_bias, hy_w_out, moe_router_w, moe_router_b, moe_w_gate, moe_w_up, moe_w_down, moe_sh_gate, moe_sh_up, moe_sh_down):
    hi = lax.Precision.HIGHEST
    D = D_MODEL
    sc = jax.nn.silu(c)
    sc_ctx = jax.nn.silu(c_ctx)

    def mods(li):
        mod = jnp.dot(sc, ada_w[li], precision=hi) + ada_b[li]
        return [m[:, None, :] for m in jnp.split(mod, 6, axis=-1)]

    sh1, sc1, g1, sh2, sc2, g2 = mods(0)
    mod_ctx = jnp.dot(sc_ctx, ada_w[0][:, :2 * D], precision=hi) + ada_b[0][:2 * D]
    w0 = _a0_weights(ev_w_in[0], mla_q_norm[0], mla_w_uq[0], mla_kv_norm[0], mla_w_ukv[0],
                     mla_q_qknorm[0], mla_k_qknorm[0])
    q, k, v, u = _a0_call(x, ctx, sh1, sc1, mod_ctx[None, :D], mod_ctx[None, D:], w0)
    att = _attn_call(q, k, v)
    ug = u.reshape(BATCH, S5_NCHUNK, S5_CHUNK, S5_GROUPS, S5_GROUP)
    ug = ug.transpose(3, 1, 0, 2, 4).reshape(S5_GROUPS, S5_ROWS, S5_COLS)
    ys = _s5_call(ug, *_s5_weights(s5_lam_re[0], s5_lam_im[0], s5_log_step[0], s5_b_re[0], s5_b_im[0],
                                   s5_c_re[0], s5_c_im[0]))
    ys = ys.reshape(S5_GROUPS, S5_NCHUNK_LAT, BATCH, S5_CHUNK, S5_GROUP)
    ys = ys.transpose(2, 1, 3, 0, 4).reshape(BATCH, SEQ, S5_WIDTH)
    wo = ev_w_out[0].astype(BF16)
    wo_att = jnp.concatenate([wo[:MLA_WIDTH].reshape(MLA_HEADS, V_HEAD, D_MODEL),
                              jnp.zeros((MLA_HEADS, HEAD_PAD - V_HEAD, D_MODEL), BF16)], axis=1)
    x, *routed = _a1_call(att, ys, u, s5_d[0][None, :], s5_glu_w[0].astype(BF16), s5_glu_b[0][None, :],
                          wo_att.reshape(MLA_HEADS * HEAD_PAD, D_MODEL), wo[MLA_WIDTH:], x, g1,
                          _route_plumbing(sh2, sc2, moe_router_w[0], moe_router_b[0]))
    x = _moe(x, routed, sh2, sc2, g2, moe_w_gate, moe_w_up, moe_w_down,
             moe_sh_gate[0], moe_sh_up[0], moe_sh_down[0], 0)

    sh1, sc1, g1, sh2, sc2, g2 = mods(1)
    z, x0 = _hy_in_call(x, sh1, sc1, hy_w_in[0].astype(BF16), hy_conv_w[0], hy_conv_b[0][None, :])
    fft_tabs = _fft_tables()
    hfb = _hyena_filter(hy_f_w1[0], hy_f_b1[0], hy_f_w2[0], hy_f_b2[0], hy_f_w3[0], hy_f_freq[0])
    y = _conv_call(z, _spec_call(hfb, *fft_tabs), *fft_tabs)
    x, *routed = _hy_out_call(y, z, x0, hy_bias[0][None, :], hy_w_out[0].astype(BF16), x, g1,
                              _route_plumbing(sh2, sc2, moe_router_w[1], moe_router_b[1]))
    x = _moe(x, routed, sh2, sc2, g2, moe_w_gate, moe_w_up, moe_w_down,
             moe_sh_gate[1], moe_sh_up[1], moe_sh_down[1], 1)
    return x
```

```python
import functools
import math

import numpy as np
import jax
import jax.numpy as jnp
from jax import lax
from jax.experimental import pallas as pl
from jax.experimental.pallas import tpu as pltpu
from jax.experimental.pallas import tpu_sc as plsc

F32 = jnp.float32
BF16 = jnp.bfloat16

D_MODEL = 1024
BATCH = 8
SEQ = 4096
CTX_LEN = 256
KV_LEN = SEQ + CTX_LEN
GRID_W = 64
EPS = 1e-6

MLA_HEADS = 8
QK_NOPE = 64
QK_ROPE = 32
QK_HEAD = QK_NOPE + QK_ROPE
V_HEAD = 64
Q_LORA = 256
KV_LORA = 128
MLA_WIDTH = MLA_HEADS * V_HEAD
ROPE_BASE = 10000.0
HEAD_PAD = 128

S5_WIDTH = 512
S5_GROUP = 16
S5_GROUPS = S5_WIDTH // S5_GROUP
S5_STATE = 64
S5_CHUNK = 32
S5_NCHUNK = KV_LEN // S5_CHUNK
S5_NCHUNK_LAT = SEQ // S5_CHUNK
S5_NCHUNK_CTX = CTX_LEN // S5_CHUNK

HY_WIDTH = D_MODEL
FILT_EMB = 33
FILT_BANDS = (FILT_EMB - 1) // 2
SHORT_CONV = 3
HY_MIN_DECAY = -math.log(1e-2) / 1.5
HY_MAX_DECAY = -math.log(1e-2) / 0.3
DFT_N = 2 * SEQ
FFT_N1 = 16
FFT_N2 = DFT_N // FFT_N1
FFT_NA = FFT_N1 // 2
FFT_NK = FFT_N1 // 2 + 1
HY_CT = 256

N_EXPERTS = 64
TOP_K = 6
EXPERT_FF = 256
ROUTE_SCALE = 2.5
MOE_BLOCK = 1024
ROW_WORDS = D_MODEL // 4
SC_WINDOW = 128

V7X_LANES = 128
V7X_VMEM_BYTES = 64 * 1024 * 1024
VMEM_LIMIT = V7X_VMEM_BYTES - 8 * 1024 * 1024


def _params(semantics):
    return pltpu.CompilerParams(dimension_semantics=semantics, vmem_limit_bytes=VMEM_LIMIT)


def _norm_mod(x, shift, scale):
    ms = jnp.mean(x * x, axis=-1, keepdims=True)
    return x * lax.rsqrt(ms + EPS) * (1.0 + scale) + shift


def _rms(x, gain, n):
    ms = jnp.sum(x * x, axis=-1, keepdims=True) * (1.0 / n)
    return x * lax.rsqrt(ms + EPS) * gain


ADA_ROWS = 16
ADA_TN = 1024


def _ada_kernel(c_ref, w_ref, b_ref, o_ref):
    sc = jax.nn.silu(c_ref[...])
    o_ref[0] = jnp.dot(sc, w_ref[0], preferred_element_type=F32, precision=lax.Precision.HIGHEST) + b_ref[0]


def _ada_call(c, c_ctx, ada_w, ada_b):
    depth, _, width = ada_w.shape
    rows = jnp.concatenate([c, c_ctx[None, :], jnp.zeros((ADA_ROWS - BATCH - 1, D_MODEL), F32)], axis=0)
    return pl.pallas_call(
        _ada_kernel, grid=(depth, width // ADA_TN),
        in_specs=[pl.BlockSpec((ADA_ROWS, D_MODEL), lambda l, n: (0, 0)),
                  pl.BlockSpec((1, D_MODEL, ADA_TN), lambda l, n: (l, 0, n)),
                  pl.BlockSpec((1, 1, ADA_TN), lambda l, n: (l, 0, n))],
        out_specs=pl.BlockSpec((1, ADA_ROWS, ADA_TN), lambda l, n: (l, 0, n)),
        out_shape=jax.ShapeDtypeStruct((depth, ADA_ROWS, width), F32),
        compiler_params=_params(("parallel", "arbitrary")), name="adaln_modulation",
    )(rows, ada_w, ada_b[:, None, :])


A0_TL = 256
A0_NT = SEQ // A0_TL


def _rope_perm():
    return np.concatenate([np.arange(0, QK_ROPE, 2), np.arange(1, QK_ROPE, 2)])


def _rope_tables():
    t = np.arange(SEQ)
    row = (t // GRID_W).astype(np.float64)
    col = (t % GRID_W).astype(np.float64)
    n_freq = QK_ROPE // 4
    inv = ROPE_BASE ** (-np.arange(n_freq, dtype=np.float64) / n_freq)
    ang = np.concatenate([row[:, None] * inv, col[:, None] * inv], axis=-1)
    cos, sin = np.cos(ang), np.sin(ang)
    half = QK_ROPE // 2
    a = np.zeros((KV_LEN, HEAD_PAD))
    b = np.zeros((KV_LEN, HEAD_PAD))
    a[:, :QK_HEAD] = 1.0
    a[:SEQ, QK_NOPE:QK_NOPE + half] = cos
    a[:SEQ, QK_NOPE + half:QK_HEAD] = cos
    b[:SEQ, QK_NOPE:QK_NOPE + half] = -sin
    b[:SEQ, QK_NOPE + half:QK_HEAD] = sin
    return a, b


def _norm_rope_heads(f, gain_ref, a, b, out_ref):
    width = MLA_HEADS * HEAD_PAD
    for hd in range(MLA_HEADS):
        sl = slice(hd * HEAD_PAD, (hd + 1) * HEAD_PAD)
        x = f[:, sl]
        r = lax.rsqrt(jnp.sum(x * x, axis=-1, keepdims=True) * (1.0 / QK_HEAD) + EPS)
        rot = x * (a * gain_ref[:, sl]) + f[:, width + hd * HEAD_PAD:width + (hd + 1) * HEAD_PAD] * b
        out_ref[0, :, sl] = (rot * r).astype(BF16)


def _a0_kernel(x_ref, ctx_ref, sh_ref, sc_ref, shc_ref, scc_ref, win_ref, qn_ref, wuq_ref, kvn_ref,
               wk_ref, wuv_ref, qg_ref, kg_ref, ka_ref, kb_ref, qa_ref, qb_ref,
               q_ref, k_ref, v_ref, u_ref):
    j = pl.program_id(1)
    is_ctx = j == A0_NT
    xin = jnp.where(is_ctx, ctx_ref[0], x_ref[0])
    shift = jnp.where(is_ctx, shc_ref[...], sh_ref[0])
    scale = jnp.where(is_ctx, scc_ref[...], sc_ref[0])
    h = _norm_mod(xin, shift, scale).astype(BF16)
    proj = jnp.dot(h, win_ref[...], preferred_element_type=F32)
    u_ref[0] = proj[:, 512:].astype(BF16)

    c_kv = _rms(proj[:, Q_LORA:Q_LORA + KV_LORA], kvn_ref[...], KV_LORA).astype(BF16)
    lane = lax.broadcasted_iota(jnp.int32, (1, MLA_HEADS * HEAD_PAD), 1)
    ones_lane = (lane % HEAD_PAD == V_HEAD).astype(F32)
    v_ref[0] = (jnp.dot(c_kv, wuv_ref[...], preferred_element_type=F32) + ones_lane).astype(BF16)
    kin = jnp.concatenate([c_kv, proj[:, 384:512].astype(BF16)], axis=1)
    kf = jnp.dot(kin, wk_ref[...], preferred_element_type=F32)
    _norm_rope_heads(kf, kg_ref, ka_ref[...], kb_ref[...], k_ref)

    @pl.when(j < A0_NT)
    def _():
        ql = _rms(proj[:, :Q_LORA], qn_ref[...], Q_LORA).astype(BF16)
        qf = jnp.dot(ql, wuq_ref[...], preferred_element_type=F32)
        _norm_rope_heads(qf, qg_ref, qa_ref[...], qb_ref[...], q_ref)


def _a0_call(x, ctx, sh, sc, shc, scc, w):
    nt = A0_NT
    lat = lambda b, j: (b, jnp.minimum(j, nt - 1), 0)
    full2 = lambda b, j: (0, 0)
    per_b = lambda b, j: (b, 0, 0)
    tab = pl.BlockSpec((A0_TL, HEAD_PAD), lambda b, j: (j, 0))
    in_specs = [
        pl.BlockSpec((1, A0_TL, D_MODEL), lat),
        pl.BlockSpec((1, CTX_LEN, D_MODEL), per_b),
        pl.BlockSpec((1, 1, D_MODEL), per_b),
        pl.BlockSpec((1, 1, D_MODEL), per_b),
        pl.BlockSpec((1, D_MODEL), full2),
        pl.BlockSpec((1, D_MODEL), full2),
        pl.BlockSpec((D_MODEL, 1024), full2),
        pl.BlockSpec((1, Q_LORA), full2),
        pl.BlockSpec((Q_LORA, 2 * MLA_HEADS * HEAD_PAD), full2),
        pl.BlockSpec((1, KV_LORA), full2),
        pl.BlockSpec((2 * KV_LORA, 2 * MLA_HEADS * HEAD_PAD), full2),
        pl.BlockSpec((KV_LORA, MLA_HEADS * HEAD_PAD), full2),
        pl.BlockSpec((1, MLA_HEADS * HEAD_PAD), full2),
        pl.BlockSpec((1, MLA_HEADS * HEAD_PAD), full2),
        tab, tab, tab, tab,
    ]
    out_specs = [
        pl.BlockSpec((1, A0_TL, MLA_HEADS * HEAD_PAD), lat),
        pl.BlockSpec((1, A0_TL, MLA_HEADS * HEAD_PAD), lambda b, j: (b, j, 0)),
        pl.BlockSpec((1, A0_TL, MLA_HEADS * HEAD_PAD), lambda b, j: (b, j, 0)),
        pl.BlockSpec((1, A0_TL, S5_WIDTH), lambda b, j: (b, j, 0)),
    ]
    out_shape = [
        jax.ShapeDtypeStruct((BATCH, SEQ, MLA_HEADS * HEAD_PAD), BF16),
        jax.ShapeDtypeStruct((BATCH, KV_LEN, MLA_HEADS * HEAD_PAD), BF16),
        jax.ShapeDtypeStruct((BATCH, KV_LEN, MLA_HEADS * HEAD_PAD), BF16),
        jax.ShapeDtypeStruct((BATCH, KV_LEN, S5_WIDTH), BF16),
    ]
    return pl.pallas_call(
        _a0_kernel, grid=(BATCH, nt + 1), in_specs=in_specs, out_specs=out_specs, out_shape=out_shape,
        compiler_params=_params(("parallel", "arbitrary")), name="even_in_proj",
    )(x, ctx, sh, sc, shc, scc, *w)


def _gain_swap(w, gain):
    half = QK_ROPE // 2
    wg = (w * gain).reshape(w.shape[0], MLA_HEADS, HEAD_PAD)
    re, im = wg[..., QK_NOPE:QK_NOPE + half], wg[..., QK_NOPE + half:QK_HEAD]
    out = jnp.concatenate([jnp.zeros_like(wg[..., :QK_NOPE]), im, re, jnp.zeros_like(wg[..., QK_HEAD:])], axis=-1)
    return out.reshape(w.shape)


def _a0_weights(w_in, q_norm, w_uq, kv_norm, w_ukv, q_qk, k_qk):
    perm = _rope_perm()
    kr0 = Q_LORA + KV_LORA
    w_cat = jnp.concatenate([
        w_in[:, :kr0], w_in[:, kr0:kr0 + QK_ROPE][:, perm],
        jnp.zeros((D_MODEL, HEAD_PAD - QK_ROPE), F32), w_in[:, kr0 + QK_ROPE:]], axis=1).astype(BF16)
    pad = HEAD_PAD - QK_HEAD

    def head_gain(g):
        gh = jnp.concatenate([g[:QK_NOPE], g[QK_NOPE:][perm], jnp.zeros((pad,), F32)])
        return jnp.tile(gh, MLA_HEADS)[None, :]

    uq = w_uq.reshape(Q_LORA, MLA_HEADS, QK_HEAD)
    uq = jnp.concatenate([uq[..., :QK_NOPE], uq[..., QK_NOPE:][..., perm],
                          jnp.zeros((Q_LORA, MLA_HEADS, pad), F32)], axis=-1)
    uq = uq.reshape(Q_LORA, MLA_HEADS * HEAD_PAD)
    uq = jnp.concatenate([uq, _gain_swap(uq, head_gain(q_qk))], axis=1).astype(BF16)
    ukv = w_ukv.reshape(KV_LORA, MLA_HEADS, QK_NOPE + V_HEAD)
    uk = jnp.concatenate([ukv[..., :QK_NOPE], jnp.zeros((KV_LORA, MLA_HEADS, HEAD_PAD - QK_NOPE), F32)], axis=-1)
    uk = uk.reshape(KV_LORA, MLA_HEADS * HEAD_PAD)
    place = np.zeros((KV_LORA, MLA_HEADS, HEAD_PAD), np.float32)
    for i in range(QK_ROPE):
        place[i, :, QK_NOPE + i] = 1.0
    wk = jnp.concatenate([uk, jnp.asarray(place.reshape(KV_LORA, MLA_HEADS * HEAD_PAD))], axis=0)
    wk = jnp.concatenate([wk, _gain_swap(wk, head_gain(k_qk))], axis=1).astype(BF16)
    wuv = jnp.concatenate([ukv[..., QK_NOPE:], jnp.zeros((KV_LORA, MLA_HEADS, HEAD_PAD - V_HEAD), F32)], axis=-1)
    wuv = wuv.reshape(KV_LORA, MLA_HEADS * HEAD_PAD).astype(BF16)
    a, b = _rope_tables()
    qs = QK_HEAD ** -0.5 * math.log2(math.e)
    tabs = [jnp.asarray(t, F32) for t in (a, b, a * qs, b * qs)]
    return [w_cat, q_norm[None, :], uq, kv_norm[None, :], wk, wuv, head_gain(q_qk), head_gain(k_qk)] + tabs


ATT_TQ = 256
HEADS_PER_STEP = 4


def _attn_kernel(q_ref, k_ref, v_ref, o_ref):
    for hh in range(HEADS_PER_STEP):
        sl = slice(hh * HEAD_PAD, (hh + 1) * HEAD_PAD)
        s = lax.dot_general(q_ref[0, :, sl], k_ref[0, :, sl], (((1,), (1,)), ((), ())),
                            preferred_element_type=F32)
        m = jnp.max(s, axis=-1, keepdims=True)
        p = jnp.exp2(s - m).astype(BF16)
        acc = jnp.dot(p, v_ref[0, :, sl], preferred_element_type=F32)
        o_ref[0, :, sl] = (acc * (1.0 / acc[:, V_HEAD:V_HEAD + 1])).astype(BF16)


def _attn_call(q, k, v):
    wq = HEADS_PER_STEP * HEAD_PAD
    return pl.pallas_call(
        _attn_kernel, grid=(BATCH, MLA_HEADS // HEADS_PER_STEP, SEQ // ATT_TQ),
        in_specs=[pl.BlockSpec((1, ATT_TQ, wq), lambda b, h, i: (b, i, h)),
                  pl.BlockSpec((1, KV_LEN, wq), lambda b, h, i: (b, 0, h)),
                  pl.BlockSpec((1, KV_LEN, wq), lambda b, h, i: (b, 0, h))],
        out_specs=pl.BlockSpec((1, ATT_TQ, wq), lambda b, h, i: (b, i, h)),
        out_shape=jax.ShapeDtypeStruct((BATCH, SEQ, MLA_HEADS * HEAD_PAD), BF16),
        compiler_params=_params(("parallel", "parallel", "arbitrary")), name="mla_attention",
    )(q, k, v)


S5_ROWS = S5_NCHUNK * BATCH
S5_ROWS_LAT = S5_NCHUNK_LAT * BATCH
S5_COLS = S5_CHUNK * S5_GROUP
S5_SW = 2 * S5_STATE


def _s5_kernel(u_ref, r_ref, mb_ref, mc_ref, coef_ref, y_ref, x_sc, sp_sc, t_sc):
    u = u_ref[0]
    lags = r_ref[0]
    for sg in range(S5_CHUNK):
        off = (S5_CHUNK - 1 - sg) * S5_GROUP
        t_sc[sg * S5_GROUP:(sg + 1) * S5_GROUP, :] = lags[:, off:off + S5_COLS].astype(BF16)
    x_sc[...] = jnp.dot(u, mb_ref[0], preferred_element_type=F32)
    cf = coef_ref[0]
    af, bfm, bfp, ab, bbm, bbp = [cf[i * 8:(i + 1) * 8] for i in range(6)]

    def body(i, carry):
        sf, sfw, sb, sbw = carry
        cfw = jnp.where(i < S5_NCHUNK_CTX, i + S5_NCHUNK_LAT, i - S5_NCHUNK_CTX)
        rf = pl.multiple_of(cfw * BATCH, BATCH)
        rb = pl.multiple_of((S5_NCHUNK - 1 - i) * BATCH, BATCH)
        sp_sc[pl.ds(rf, BATCH), 0:S5_SW] = sf
        sp_sc[pl.ds(rb, BATCH), S5_SW:2 * S5_SW] = sb
        xf = x_sc[pl.ds(rf, BATCH), 0:S5_SW]
        xfw = x_sc[pl.ds(rf, BATCH), S5_SW:2 * S5_SW]
        xb = x_sc[pl.ds(rb, BATCH), 2 * S5_SW:3 * S5_SW]
        xbw = x_sc[pl.ds(rb, BATCH), 3 * S5_SW:4 * S5_SW]
        return (sf * af + sfw * bfm + xf, sfw * af + sf * bfp + xfw,
                sb * ab + sbw * bbm + xb, sbw * ab + sb * bbp + xbw)

    z = jnp.zeros((BATCH, S5_SW), F32)
    lax.fori_loop(0, S5_NCHUNK, body, (z, z, z, z))
    y = jnp.dot(u[:S5_ROWS_LAT], t_sc[...], preferred_element_type=F32)
    y = y + jnp.dot(sp_sc[0:S5_ROWS_LAT, :].astype(BF16), mc_ref[0], preferred_element_type=F32)
    y_ref[0] = y.astype(BF16)


def _s5_call(ug, t, mb, mc, coef):
    g3 = lambda g: (g, 0, 0)
    return pl.pallas_call(
        _s5_kernel, grid=(S5_GROUPS,),
        in_specs=[pl.BlockSpec((1, S5_ROWS, S5_COLS), g3),
                  pl.BlockSpec((1, S5_GROUP, 2 * S5_COLS), g3),
                  pl.BlockSpec((1, S5_COLS, 4 * S5_SW), g3),
                  pl.BlockSpec((1, 2 * S5_SW, S5_COLS), g3),
                  pl.BlockSpec((1, 6 * 8, S5_SW), g3)],
        out_specs=pl.BlockSpec((1, S5_ROWS_LAT, S5_COLS), g3),
        out_shape=jax.ShapeDtypeStruct((S5_GROUPS, S5_ROWS_LAT, S5_COLS), BF16),
        scratch_shapes=[pltpu.VMEM((S5_ROWS, 4 * S5_SW), F32), pltpu.VMEM((S5_ROWS, 2 * S5_SW), F32),
                        pltpu.VMEM((S5_COLS, S5_COLS), BF16)],
        compiler_params=_params(("parallel",)), name="s5_chunked_scan",
    )(ug, t, mb, mc, coef)


def _s5_weights(lam_re, lam_im, log_step, b_re, b_im, c_re, c_im):
    q = S5_CHUNK
    hi = lax.Precision.HIGHEST
    t_blocks, mbs, mcs, coefs = [], [], [], []
    sig = jnp.arange(q)
    for d in range(2):
        lr = jnp.minimum(lam_re[d], -1e-4)
        li = lam_im[d]
        step = jnp.exp(log_step[d])[:, None]
        jj = jnp.arange(q + 1, dtype=F32)[:, None, None]
        mag = jnp.exp(lr * step * jj)
        ph = li * step * jj
        pr, pi = mag * jnp.cos(ph), mag * jnp.sin(ph)
        nr, ni = pr[1] - 1.0, pi[1]
        den = lr * lr + li * li
        fr, fi = (nr * lr + ni * li) / den, (ni * lr - nr * li) / den
        br = fr[..., None] * b_re[d] - fi[..., None] * b_im[d]
        bi = fr[..., None] * b_im[d] + fi[..., None] * b_re[d]
        cr, ci = c_re[d], c_im[d]
        cpr = cr[None] * pr[:, :, None, :] - ci[None] * pi[:, :, None, :]
        cpi = cr[None] * pi[:, :, None, :] + ci[None] * pr[:, :, None, :]
        kern = jnp.einsum('jghp,gpk->jghk', jnp.concatenate([cpr[:q], -cpi[:q]], axis=-1),
                          jnp.concatenate([br, bi], axis=1), precision=hi)
        kt = kern.transpose(1, 3, 0, 2)
        zero_slots = jnp.zeros((S5_GROUPS, S5_GROUP, q, S5_GROUP), F32)
        if d == 0:
            t_blocks.append(jnp.concatenate([zero_slots[:, :, :q - 1], kt, zero_slots[:, :, :1]], axis=2))
        else:
            t_blocks.append(jnp.concatenate([kt[:, :, ::-1], zero_slots], axis=2))
        pw = (q - 1 - sig) if d == 0 else sig
        xr = pr[pw][..., None] * br[None] - pi[pw][..., None] * bi[None]
        xi = pr[pw][..., None] * bi[None] + pi[pw][..., None] * br[None]
        xr = xr.transpose(1, 0, 3, 2).reshape(S5_GROUPS, S5_COLS, S5_STATE)
        xi = xi.transpose(1, 0, 3, 2).reshape(S5_GROUPS, S5_COLS, S5_STATE)
        mbs += [xr, xi, xi, xr]
        po = (sig + 1) if d == 0 else (q - sig)
        mr = cpr[po].transpose(1, 3, 0, 2).reshape(S5_GROUPS, S5_STATE, S5_COLS)
        mi = -cpi[po].transpose(1, 3, 0, 2).reshape(S5_GROUPS, S5_STATE, S5_COLS)
        mcs += [mr, mi]
        are, aim = pr[q], pi[q]
        rows = [jnp.concatenate([are, are], -1), jnp.concatenate([-aim, aim], -1), jnp.concatenate([aim, -aim], -1)]
        coefs += [jnp.broadcast_to(r[:, None, :], (S5_GROUPS, 8, S5_SW)) for r in rows]
    t = (t_blocks[0] + t_blocks[1]).reshape(S5_GROUPS, S5_GROUP, 2 * S5_COLS)
    mb = jnp.concatenate(mbs, axis=-1).astype(BF16)
    mc = jnp.concatenate(mcs, axis=1).astype(BF16)
    coef = jnp.concatenate(coefs, axis=1)
    return t, mb, mc, coef


A1_TL = 512


def _a1_kernel(att_ref, ys_ref, u_ref, d_ref, gw_ref, gb_ref, woa_ref, wos_ref, x_ref, g_ref, *route):
    o_ref = route[5]
    y = u_ref[0].astype(F32) * d_ref[...] + ys_ref[0].astype(F32)
    z = jax.nn.gelu(y)
    gate = jax.nn.sigmoid(jnp.dot(z.astype(BF16), gw_ref[...], preferred_element_type=F32) + gb_ref[...])
    s5 = (z * gate).astype(BF16)
    mix = jnp.dot(att_ref[0], woa_ref[...], preferred_element_type=F32)
    mix = mix + jnp.dot(s5, wos_ref[...], preferred_element_type=F32)
    x_new = x_ref[0] + g_ref[0] * mix
    o_ref[0] = x_new
    _route_tile(x_new, *route[:5], *route[6:])


def _a1_call(att, ys, u, d, gw, gb, woa, wos, x, g1, route):
    r_args, r_in, r_out, r_shape, r_scratch = route
    t3 = lambda b, j: (b, j, 0)
    full2 = lambda b, j: (0, 0)
    return pl.pallas_call(
        _a1_kernel, grid=(BATCH, SEQ // A1_TL),
        in_specs=[pl.BlockSpec((1, A1_TL, MLA_HEADS * HEAD_PAD), t3),
                  pl.BlockSpec((1, A1_TL, S5_WIDTH), t3),
                  pl.BlockSpec((1, A1_TL, S5_WIDTH), t3),
                  pl.BlockSpec((1, S5_WIDTH), full2),
                  pl.BlockSpec((S5_WIDTH, S5_WIDTH), full2),
                  pl.BlockSpec((1, S5_WIDTH), full2),
                  pl.BlockSpec((MLA_HEADS * HEAD_PAD, D_MODEL), full2),
                  pl.BlockSpec((S5_WIDTH, D_MODEL), full2),
                  pl.BlockSpec((1, A1_TL, D_MODEL), t3),
                  pl.BlockSpec((1, 1, D_MODEL), lambda b, j: (b, 0, 0))] + r_in,
        out_specs=[pl.BlockSpec((1, A1_TL, D_MODEL), t3)] + r_out,
        out_shape=[jax.ShapeDtypeStruct((BATCH, SEQ, D_MODEL), F32)] + r_shape,
        scratch_shapes=r_scratch,
        compiler_params=_params(("arbitrary", "arbitrary")), name="even_out_proj_route",
    )(att, ys, u, d, gw, gb, woa, wos, x, g1, *r_args)


MOE_TL = 512
SLOT_PAD = 8


def _pack_rows(v):
    halves = []
    for p in range(2):
        base = 2 * p * ROW_WORDS
        a = pltpu.bitcast(v[:, base:base + ROW_WORDS].astype(BF16).astype(F32), jnp.uint32)
        b = pltpu.bitcast(v[:, base + ROW_WORDS:base + 2 * ROW_WORDS].astype(BF16).astype(F32), jnp.uint32)
        halves.append((a >> 16) | b)
    return halves


def _unpack_rows(lo, hi):
    out = []
    for w in (lo, hi):
        out.append(pltpu.bitcast(w << 16, F32))
        out.append(pltpu.bitcast(w & jnp.uint32(0xFFFF0000), F32))
    return out


def _route_tile(x, sh_ref, sc_ref, rwt_ref, rb_ref, tri_ref,
                hlo_ref, hhi_ref, idx_ref, wt_ref, rank_ref, cnt_ref, run_sc):
    @pl.when((pl.program_id(0) == 0) & (pl.program_id(1) == 0))
    def _():
        run_sc[...] = jnp.zeros_like(run_sc)

    h = _norm_mod(x, sh_ref[0], sc_ref[0])
    hb = h.astype(BF16)
    hlo_ref[0], hhi_ref[0] = _pack_rows(h)
    h_lo = (h - hb.astype(F32)).astype(BF16)
    rwt = rwt_ref[...]
    rw_hi = rwt.astype(BF16)
    rw_lo = (rwt - rw_hi.astype(F32)).astype(BF16)
    logits = lax.dot_general(jnp.concatenate([rw_hi, rw_lo, rw_hi], axis=1), jnp.concatenate([hb, hb, h_lo], axis=1),
                             (((1,), (1,)), ((), ())), preferred_element_type=F32)
    scores = jax.nn.sigmoid(logits)

    work = scores + rb_ref[...]
    expert = lax.broadcasted_iota(jnp.int32, work.shape, 0).astype(F32)
    hits, ids = [], []
    for _ in range(TOP_K):
        m = jnp.max(work, axis=0, keepdims=True)
        ik = jnp.min(jnp.where(work == m, expert, float(N_EXPERTS)), axis=0, keepdims=True)
        hit = expert == ik
        hits.append(hit)
        ids.append(ik)
        work = jnp.where(hit, -jnp.inf, work)
    mask = hits[0]
    for hit in hits[1:]:
        mask = jnp.logical_or(mask, hit)
    maskf = mask.astype(F32)
    before = jnp.dot(maskf.astype(BF16), tri_ref[...], preferred_element_type=F32) + run_sc[:, 0:1]
    sel = [jnp.sum(jnp.where(hit, scores, 0.0), axis=0, keepdims=True) for hit in hits]
    denom = sel[0]
    for s in sel[1:]:
        denom = denom + s
    ranks = [jnp.sum(jnp.where(hit, before, 0.0), axis=0, keepdims=True) for hit in hits]
    pad = [jnp.zeros_like(denom)] * (SLOT_PAD - TOP_K)
    idx_ref[...] = jnp.concatenate(ids + pad, axis=0).astype(jnp.int32)
    wt_ref[...] = jnp.concatenate([s / denom * ROUTE_SCALE for s in sel] + pad, axis=0)
    rank_ref[...] = jnp.concatenate(ranks + pad, axis=0).astype(jnp.int32)
    run_sc[...] += jnp.sum(maskf, axis=1, keepdims=True)
    cnt_ref[...] = run_sc[...]


def _route_plumbing(sh, sc, rw, rb):
    t3 = lambda b, j: (b, j, 0)
    full2 = lambda b, j: (0, 0)
    per_b = lambda b, j: (b, 0, 0)
    nt = SEQ // MOE_TL
    slots = lambda b, j: (0, b * nt + j)
    tri = jnp.asarray(np.triu(np.ones((MOE_TL, MOE_TL), np.float32), 1), BF16)
    args = [sh, sc, rw.T, rb[:, None], tri]
    in_specs = [pl.BlockSpec((1, 1, D_MODEL), per_b),
                pl.BlockSpec((1, 1, D_MODEL), per_b),
                pl.BlockSpec((N_EXPERTS, D_MODEL), full2),
                pl.BlockSpec((N_EXPERTS, 1), full2),
                pl.BlockSpec((MOE_TL, MOE_TL), full2)]
    out_specs = [pl.BlockSpec((1, MOE_TL, ROW_WORDS), t3),
                 pl.BlockSpec((1, MOE_TL, ROW_WORDS), t3),
                 pl.BlockSpec((SLOT_PAD, MOE_TL), slots),
                 pl.BlockSpec((SLOT_PAD, MOE_TL), slots),
                 pl.BlockSpec((SLOT_PAD, MOE_TL), slots),
                 pl.BlockSpec((N_EXPERTS, V7X_LANES), full2)]
    slot_i = jax.ShapeDtypeStruct((SLOT_PAD, BATCH * SEQ), jnp.int32)
    out_shape = [jax.ShapeDtypeStruct((BATCH, SEQ, ROW_WORDS), jnp.uint32),
                 jax.ShapeDtypeStruct((BATCH, SEQ, ROW_WORDS), jnp.uint32),
                 slot_i,
                 jax.ShapeDtypeStruct((SLOT_PAD, BATCH * SEQ), F32),
                 slot_i,
                 jax.ShapeDtypeStruct((N_EXPERTS, V7X_LANES), F32)]
    scratch = [pltpu.VMEM((N_EXPERTS, V7X_LANES), F32)]
    return args, in_specs, out_specs, out_shape, scratch


def _sc_mesh():
    return plsc.VectorSubcoreMesh(core_axis_name="c", subcore_axis_name="s")


def _sc_dispatch(h_words, dest, n_rows):
    n_tok = h_words.shape[0]

    @pl.kernel(out_type=jax.ShapeDtypeStruct((n_rows, ROW_WORDS), jnp.uint32), mesh=_sc_mesh(), scratch_types=[])
    def scatter_rows(h_hbm, i_hbm, o_hbm):
        def body(h_vmem, i_vmem):
            for k in range(TOP_K):
                pltpu.sync_copy(h_vmem, o_hbm.at[i_vmem.at[k]])

        pltpu.emit_pipeline(
            body, grid=(n_tok // SC_WINDOW,),
            in_specs=[pl.BlockSpec((SC_WINDOW, ROW_WORDS), index_map=lambda i: (i, 0)),
                      pl.BlockSpec((SLOT_PAD, SC_WINDOW), index_map=lambda i: (0, i))],
            out_specs=[],
            core_axis_name=("c", "s"), dimension_semantics=(pltpu.PARALLEL,),
        )(h_hbm, i_hbm)

    return scatter_rows(h_words, dest)


def _sc_collect(y_words, dest):
    n_tok = dest.shape[1]

    @pl.kernel(out_type=jax.ShapeDtypeStruct((TOP_K, n_tok, ROW_WORDS), jnp.uint32), mesh=_sc_mesh(),
               scratch_types=[])
    def gather_rows(y_hbm, i_hbm, o_hbm):
        def body(i_vmem, o_vmem):
            pltpu.sync_copy(y_hbm.at[i_vmem.at[0]], o_vmem.at[0])

        pltpu.emit_pipeline(
            body, grid=(TOP_K, n_tok // SC_WINDOW),
            in_specs=[pl.BlockSpec((1, SC_WINDOW), index_map=lambda k, i: (k, i))],
            out_specs=[pl.BlockSpec((1, SC_WINDOW, ROW_WORDS), index_map=lambda k, i: (k, i, 0))],
            core_axis_name=("c", "s"), dimension_semantics=(pltpu.PARALLEL, pltpu.PARALLEL),
        )(i_hbm, o_hbm)

    return gather_rows(y_words, dest)


def _expert_kernel(be_ref, nv_ref, xlo_ref, xhi_ref, wg_ref, wu_ref, wd_ref, ylo_ref, yhi_ref,
                   wg_sc, wu_sc, wd_sc):
    i = pl.program_id(0)
    nv = nv_ref[i]

    @pl.when(jnp.logical_or(i == 0, be_ref[i] != be_ref[jnp.maximum(i - 1, 0)]))
    def _():
        wg_sc[...] = wg_ref[0, 0].astype(BF16)
        wu_sc[...] = wu_ref[0, 0].astype(BF16)
        wd_sc[...] = wd_ref[0, 0].astype(BF16)

    @pl.when(nv > 0)
    def _():
        parts = _unpack_rows(xlo_ref[...], xhi_ref[...])
        xb = jnp.concatenate([p.astype(BF16) for p in parts], axis=1)
        live = lax.broadcasted_iota(jnp.int32, xb.shape, 0) < nv
        xb = jnp.where(live, xb, jnp.zeros_like(xb))
        hid = jax.nn.silu(jnp.dot(xb, wg_sc[...], preferred_element_type=F32))
        hid = hid * jnp.dot(xb, wu_sc[...], preferred_element_type=F32)
        y = jnp.dot(hid.astype(BF16), wd_sc[...], preferred_element_type=F32)
        ylo_ref[...], yhi_ref[...] = _pack_rows(y)

    @pl.when(nv == 0)
    def _():
        ylo_ref[...] = jnp.zeros_like(ylo_ref)
        yhi_ref[...] = jnp.zeros_like(yhi_ref)


def _expert_call(block_e, n_valid, xlo, xhi, wg, wu, wd, li):
    n_rows = xlo.shape[0]
    n_blocks = n_rows // MOE_BLOCK
    rows = pl.BlockSpec((MOE_BLOCK, ROW_WORDS), lambda i, be, nv: (i, 0))
    grid_spec = pltpu.PrefetchScalarGridSpec(
        num_scalar_prefetch=2, grid=(n_blocks,),
        in_specs=[rows, rows,
                  pl.BlockSpec((1, 1, D_MODEL, EXPERT_FF), lambda i, be, nv: (li, be[i], 0, 0)),
                  pl.BlockSpec((1, 1, D_MODEL, EXPERT_FF), lambda i, be, nv: (li, be[i], 0, 0)),
                  pl.BlockSpec((1, 1, EXPERT_FF, D_MODEL), lambda i, be, nv: (li, be[i], 0, 0))],
        out_specs=[rows, rows],
        scratch_shapes=[pltpu.VMEM((D_MODEL, EXPERT_FF), BF16), pltpu.VMEM((D_MODEL, EXPERT_FF), BF16),
                        pltpu.VMEM((EXPERT_FF, D_MODEL), BF16)])
    out = jax.ShapeDtypeStruct((n_rows, ROW_WORDS), jnp.uint32)
    return pl.pallas_call(
        _expert_kernel, grid_spec=grid_spec, out_shape=[out, out],
        compiler_params=_params(("arbitrary",)), name="moe_experts",
    )(block_e, n_valid, xlo, xhi, wg, wu, wd)


def _combine_kernel(ylo_ref, yhi_ref, w_ref, x_ref, sh_ref, sc_ref, g_ref, sg_ref, su_ref, sd_ref, o_ref):
    hb = _norm_mod(x_ref[0], sh_ref[0], sc_ref[0]).astype(BF16)
    hid = jax.nn.silu(jnp.dot(hb, sg_ref[...].astype(BF16), preferred_element_type=F32))
    hid = hid * jnp.dot(hb, su_ref[...].astype(BF16), preferred_element_type=F32)
    shared = jnp.dot(hid.astype(BF16), sd_ref[...].astype(BF16), preferred_element_type=F32)
    w = w_ref[0]
    acc = [None] * 4
    for k in range(TOP_K):
        wk = w[:, k:k + 1]
        for c, part in enumerate(_unpack_rows(ylo_ref[k], yhi_ref[k])):
            acc[c] = wk * part if acc[c] is None else acc[c] + wk * part
    for c in range(4):
        sl = slice(c * ROW_WORDS, (c + 1) * ROW_WORDS)
        o_ref[0, :, sl] = x_ref[0, :, sl] + g_ref[0, :, sl] * (acc[c] + shared[:, sl])


def _combine_call(ylo, yhi, wts, x, sh, sc, g2, sg, su, sd):
    t3 = lambda b, j: (b, j, 0)
    per_b = lambda b, j: (b, 0, 0)
    full2 = lambda b, j: (0, 0)
    ff = sg.shape[1]
    nt = SEQ // MOE_TL
    rows = pl.BlockSpec((TOP_K, MOE_TL, ROW_WORDS), lambda b, j: (0, b * nt + j, 0))
    return pl.pallas_call(
        _combine_kernel, grid=(BATCH, nt),
        in_specs=[rows, rows,
                  pl.BlockSpec((1, MOE_TL, SLOT_PAD), t3),
                  pl.BlockSpec((1, MOE_TL, D_MODEL), t3),
                  pl.BlockSpec((1, 1, D_MODEL), per_b),
                  pl.BlockSpec((1, 1, D_MODEL), per_b),
                  pl.BlockSpec((1, 1, D_MODEL), per_b),
                  pl.BlockSpec((D_MODEL, ff), full2),
                  pl.BlockSpec((D_MODEL, ff), full2),
                  pl.BlockSpec((ff, D_MODEL), full2)],
        out_specs=pl.BlockSpec((1, MOE_TL, D_MODEL), t3),
        out_shape=jax.ShapeDtypeStruct((BATCH, SEQ, D_MODEL), F32),
        compiler_params=_params(("parallel", "arbitrary")), name="moe_combine_shared",
    )(ylo, yhi, wts, x, sh, sc, g2, sg, su, sd)


def _moe(x, routed, sh, sc, g2, w_gate, w_up, w_down, sh_gate, sh_up, sh_down, li):
    T = BATCH * SEQ
    TK = T * TOP_K
    hlo, hhi, idx, wts, rank, counts = routed
    wts = wts.T.reshape(BATCH, SEQ, SLOT_PAD)
    counts = counts[:, 0].astype(jnp.int32)
    padded = (counts + MOE_BLOCK - 1) // MOE_BLOCK * MOE_BLOCK
    pad_end = jnp.cumsum(padded)
    pad_start = pad_end - padded
    n_blocks = -(-TK // MOE_BLOCK) + N_EXPERTS
    n_rows = n_blocks * MOE_BLOCK
    block_start = jnp.arange(n_blocks, dtype=jnp.int32) * MOE_BLOCK
    owns = jnp.logical_and(block_start[:, None] >= pad_start[None, :], block_start[:, None] < pad_end[None, :])
    owns = owns.astype(jnp.int32)
    experts = jnp.arange(N_EXPERTS, dtype=jnp.int32)[None, :]
    block_e = jnp.sum(owns * experts, axis=1) + (N_EXPERTS - 1) * (1 - jnp.sum(owns, axis=1))
    n_valid = jnp.sum(owns * (counts[None, :] - (block_start[:, None] - pad_start[None, :])), axis=1)
    n_valid = jnp.clip(n_valid, 0, MOE_BLOCK).astype(jnp.int32)
    first_row = jnp.sum(jnp.where(idx[None] == experts.T[:, :, None], pad_start[:, None, None], 0), axis=0)
    dest = first_row + rank
    xlo = _sc_dispatch(hlo.reshape(T, ROW_WORDS), dest, n_rows)
    xhi = _sc_dispatch(hhi.reshape(T, ROW_WORDS), dest, n_rows)
    ylo, yhi = _expert_call(block_e, n_valid, xlo, xhi, w_gate, w_up, w_down, li)
    return _combine_call(_sc_collect(ylo, dest), _sc_collect(yhi, dest), wts, x, sh, sc, g2,
                         sh_gate, sh_up, sh_down)


HY_TL = 512
HALO = 8


def _hy_in_kernel(x_ref, xp_ref, xn_ref, sh_ref, sc_ref, w_ref, cw_ref, cb_ref, z_ref, x0_ref, h_sc):
    j = pl.program_id(1)
    shift, scale = sh_ref[0], sc_ref[0]
    keep_prev = (j > 0).astype(F32)
    keep_next = (j < SEQ // HY_TL - 1).astype(F32)
    h_sc[0:HALO, :] = _norm_mod(xp_ref[0], shift, scale) * keep_prev
    h_sc[HALO:HALO + HY_TL, :] = _norm_mod(x_ref[0], shift, scale)
    h_sc[HALO + HY_TL:, :] = _norm_mod(xn_ref[0], shift, scale) * keep_next
    hcat = h_sc[...].astype(BF16)
    outs = []
    for part in range(3):
        sl = slice(part * HY_WIDTH, (part + 1) * HY_WIDTH)
        p = jnp.dot(hcat, w_ref[:, sl], preferred_element_type=F32)
        o = (p[HALO - 1:HALO - 1 + HY_TL] * cw_ref[0:1, sl] + p[HALO:HALO + HY_TL] * cw_ref[1:2, sl]
             + p[HALO + 1:HALO + 1 + HY_TL] * cw_ref[2:3, sl] + cb_ref[:, sl])
        outs.append(o)
    x0_ref[0] = outs[0].astype(BF16)
    z_ref[0] = (outs[2] * outs[1]).astype(BF16)


def _hy_in_call(x, sh, sc, w, cw, cb):
    nb8 = HY_TL // HALO
    t3 = lambda b, j: (b, j, 0)
    full2 = lambda b, j: (0, 0)
    per_b = lambda b, j: (b, 0, 0)
    return pl.pallas_call(
        _hy_in_kernel, grid=(BATCH, SEQ // HY_TL),
        in_specs=[pl.BlockSpec((1, HY_TL, D_MODEL), t3),
                  pl.BlockSpec((1, HALO, D_MODEL), lambda b, j: (b, jnp.maximum(j * nb8 - 1, 0), 0)),
                  pl.BlockSpec((1, HALO, D_MODEL), lambda b, j: (b, jnp.minimum((j + 1) * nb8, SEQ // HALO - 1), 0)),
                  pl.BlockSpec((1, 1, D_MODEL), per_b),
                  pl.BlockSpec((1, 1, D_MODEL), per_b),
                  pl.BlockSpec((D_MODEL, 3 * HY_WIDTH), full2),
                  pl.BlockSpec((SHORT_CONV, 3 * HY_WIDTH), full2),
                  pl.BlockSpec((1, 3 * HY_WIDTH), full2)],
        out_specs=[pl.BlockSpec((1, HY_TL, HY_WIDTH), t3), pl.BlockSpec((1, HY_TL, HY_WIDTH), t3)],
        out_shape=[jax.ShapeDtypeStruct((BATCH, SEQ, HY_WIDTH), BF16),
                   jax.ShapeDtypeStruct((BATCH, SEQ, HY_WIDTH), BF16)],
        scratch_shapes=[pltpu.VMEM((HY_TL + 2 * HALO, D_MODEL), F32)],
        compiler_params=_params(("parallel", "arbitrary")), name="hyena_in_proj",
    )(x, x, x, sh, sc, w, cw, cb)


def _fft_tables():
    c = np.arange(FFT_N2, dtype=np.int64)
    ang = 2.0 * np.pi * ((c[:, None] * c[None, :]) % FFT_N2) / FFT_N2
    sr, si = np.cos(ang), -np.sin(ang)
    m = np.block([[sr, -si], [si, sr]])
    k1 = np.arange(FFT_NK, dtype=np.int64)
    ang_t = 2.0 * np.pi * (k1[:, None] * c[None, :]) / DFT_N
    lanes = np.ones((1, 1, V7X_LANES))
    tr = np.cos(ang_t)[:, :, None] * lanes
    ti = -np.sin(ang_t)[:, :, None] * lanes
    return jnp.asarray(m, BF16), jnp.asarray(tr, F32), jnp.asarray(ti, F32)


def _lin(acc, coef, val):
    if abs(coef) < 1e-12:
        return acc
    term = val if coef == 1.0 else (-val if coef == -1.0 else coef * val)
    return term if acc is None else acc + term


def _twiddle(tr_ref, ti_ref, k1, width):
    reps = width // V7X_LANES
    tr, ti = tr_ref[k1], ti_ref[k1]
    return jnp.concatenate([tr] * reps, axis=1), jnp.concatenate([ti] * reps, axis=1)


_COS_PI_4 = math.sqrt(0.5)


def _blocks_to_classes(z):
    assert FFT_N1 == 16
    sp = [z[a] + z[a + 4] for a in range(4)]
    sm = [z[a] - z[a + 4] for a in range(4)]
    q0p, q0m, q1p, q1m = sp[0] + sp[2], sp[0] - sp[2], sp[1] + sp[3], sp[1] - sp[3]
    d, e = _COS_PI_4 * (sm[1] - sm[3]), _COS_PI_4 * (sm[1] + sm[3])
    out = [None] * FFT_NK
    out[0] = (q0p + q1p, None)
    out[8] = (q0p - q1p, None)
    out[4] = (q0m, -q1m)
    out[2] = (sm[0] + d, -(sm[2] + e))
    out[6] = (sm[0] - d, sm[2] - e)
    for k in (1, 3):
        halves = []
        for parity in (0, 1):
            hr = hi = None
            for a in range(parity, FFT_NA, 2):
                th = 2.0 * math.pi * ((a * k) % FFT_N1) / FFT_N1
                hr = _lin(hr, round(math.cos(th), 15), z[a])
                hi = _lin(hi, round(-math.sin(th), 15), z[a])
            halves.append((hr, hi))
        (er, ei), (odr, odi) = halves
        out[k] = (er + odr, ei + odi)
        out[8 - k] = (er - odr, odi - ei)
    return out


def _classes_to_blocks(v):
    assert FFT_N1 == 16
    base = (v[0][0] + v[8][0], v[0][0] - v[8][0])
    pr = {k: (v[k][0] + v[8 - k][0], v[k][0] - v[8 - k][0]) for k in (1, 2, 3)}
    pi = {k: (v[k][1] - v[8 - k][1], v[k][1] + v[8 - k][1]) for k in (1, 2, 3)}
    out = []
    for a in range(FFT_NA):
        odd = a % 2
        mid = (v[4][0], 1.0 - (a % 4)) if not odd else (v[4][1], (a % 4) - 2.0)
        acc = _lin(base[odd], mid[1], mid[0])
        for k in (1, 2, 3):
            th = 2.0 * math.pi * ((a * k) % FFT_N1) / FFT_N1
            acc = _lin(acc, round(math.cos(th), 15), pr[k][odd])
            acc = _lin(acc, round(-math.sin(th), 15), pi[k][odd])
        out.append(acc)
    return out


def _class_forward(yr, yi, k1, m_ref, tr_ref, ti_ref, width):
    if k1 == 0:
        x = jnp.dot(m_ref[:, :FFT_N2], yr.astype(BF16), preferred_element_type=F32)
    else:
        tr, ti = _twiddle(tr_ref, ti_ref, k1, width)
        yr, yi = (yr * tr, yr * ti) if yi is None else (yr * tr - yi * ti, yr * ti + yi * tr)
        x = jnp.dot(m_ref[...], jnp.concatenate([yr, yi], axis=0).astype(BF16), preferred_element_type=F32)
    return x[:FFT_N2], x[FFT_N2:]


def _class_backward(pr, pi, k1, m_ref, tr_ref, ti_ref, width):
    v = jnp.dot(m_ref[...], jnp.concatenate([pr, -pi], axis=0).astype(BF16), preferred_element_type=F32)
    ur, ui = v[:FFT_N2], -v[FFT_N2:]
    if k1 > 0:
        tr, ti = _twiddle(tr_ref, ti_ref, k1, width)
        ur, ui = ur * tr + ui * ti, ui * tr - ur * ti
    return ur, ui


def _spec_kernel(hf_ref, hb_ref, m_ref, tr_ref, ti_ref, c_ref):
    rows = lambda ref: [ref[a * FFT_N2:(a + 1) * FFT_N2, :] for a in range(FFT_NA)]
    fwd, bwd = _blocks_to_classes(rows(hf_ref)), _blocks_to_classes(rows(hb_ref))
    for k1 in range(FFT_NK):
        fr, fi = _class_forward(*fwd[k1], k1, m_ref, tr_ref, ti_ref, HY_CT)
        br, bi = _class_forward(*bwd[k1], k1, m_ref, tr_ref, ti_ref, HY_CT)
        scale = (1.0 if k1 in (0, FFT_N1 // 2) else 2.0) / DFT_N
        c_ref[k1, :FFT_N2, :] = ((fr + br) * scale).astype(BF16)
        c_ref[k1, FFT_N2:, :] = ((fi - bi) * scale).astype(BF16)


def _fft_table_specs(ngrid):
    z = (0,) * 2
    z3 = (0,) * 3
    if ngrid == 1:
        return [pl.BlockSpec((2 * FFT_N2, 2 * FFT_N2), lambda c: z),
                pl.BlockSpec((FFT_NK, FFT_N2, V7X_LANES), lambda c: z3),
                pl.BlockSpec((FFT_NK, FFT_N2, V7X_LANES), lambda c: z3)]
    return [pl.BlockSpec((2 * FFT_N2, 2 * FFT_N2), lambda b, c: z),
            pl.BlockSpec((FFT_NK, FFT_N2, V7X_LANES), lambda b, c: z3),
            pl.BlockSpec((FFT_NK, FFT_N2, V7X_LANES), lambda b, c: z3)]


def _spec_call(hfb, m, tr, ti):
    nct = HY_WIDTH // HY_CT
    return pl.pallas_call(
        _spec_kernel, grid=(nct,),
        in_specs=[pl.BlockSpec((SEQ, HY_CT), lambda c: (0, c)),
                  pl.BlockSpec((SEQ, HY_CT), lambda c: (0, c + nct))] + _fft_table_specs(1),
        out_specs=pl.BlockSpec((FFT_NK, 2 * FFT_N2, HY_CT), lambda c: (0, 0, c)),
        out_shape=jax.ShapeDtypeStruct((FFT_NK, 2 * FFT_N2, HY_WIDTH), BF16),
        compiler_params=_params(("arbitrary",)), name="hyena_filter_spectrum",
    )(hfb, hfb, m, tr, ti)


def _conv_kernel(z_ref, c_ref, m_ref, tr_ref, ti_ref, y_ref, cls):
    for k1, (yr, yi) in enumerate(_blocks_to_classes(
            [z_ref[0, a * FFT_N2:(a + 1) * FFT_N2, :].astype(F32) for a in range(FFT_NA)])):
        cls[k1, :FFT_N2, :] = yr
        if yi is not None:
            cls[k1, FFT_N2:, :] = yi
    for k1 in range(FFT_NK):
        yi = None if k1 in (0, FFT_N1 // 2) else cls[k1, FFT_N2:, :]
        xr, xi = _class_forward(cls[k1, :FFT_N2, :], yi, k1, m_ref, tr_ref, ti_ref, HY_CT)
        cr = c_ref[k1, :FFT_N2, :].astype(F32)
        ci = c_ref[k1, FFT_N2:, :].astype(F32)
        ur, ui = _class_backward(xr * cr - xi * ci, xr * ci + xi * cr, k1, m_ref, tr_ref, ti_ref, HY_CT)
        cls[k1, :FFT_N2, :] = ur
        cls[k1, FFT_N2:, :] = ui
    blocks = _classes_to_blocks([(cls[k1, :FFT_N2, :], cls[k1, FFT_N2:, :]) for k1 in range(FFT_NK)])
    for a, ya in enumerate(blocks):
        y_ref[0, a * FFT_N2:(a + 1) * FFT_N2, :] = ya.astype(BF16)


def _conv_call(z, spec, m, tr, ti):
    return pl.pallas_call(
        _conv_kernel, grid=(HY_WIDTH // HY_CT, BATCH),
        in_specs=[pl.BlockSpec((1, SEQ, HY_CT), lambda c, b: (b, 0, c)),
                  pl.BlockSpec((FFT_NK, 2 * FFT_N2, HY_CT), lambda c, b: (0, 0, c))] + _fft_table_specs(2),
        out_specs=pl.BlockSpec((1, SEQ, HY_CT), lambda c, b: (b, 0, c)),
        out_shape=jax.ShapeDtypeStruct((BATCH, SEQ, HY_WIDTH), BF16),
        scratch_shapes=[pltpu.VMEM((FFT_NK, 2 * FFT_N2, HY_CT), F32)],
        compiler_params=_params(("parallel", "arbitrary")), name="hyena_long_conv",
    )(z, spec, m, tr, ti)


def _hy_out_kernel(y_ref, z_ref, x0_ref, b_ref, w_ref, x_ref, g_ref, *route):
    o_ref = route[5]
    z = z_ref[0].astype(F32)
    gated = x0_ref[0].astype(F32) * (y_ref[0].astype(F32) + b_ref[...] * z)
    mix = jnp.dot(gated.astype(BF16), w_ref[...], preferred_element_type=F32)
    x_new = x_ref[0] + g_ref[0] * mix
    o_ref[0] = x_new
    _route_tile(x_new, *route[:5], *route[6:])


def _hy_out_call(y, z, x0, bias, w, x, g1, route):
    r_args, r_in, r_out, r_shape, r_scratch = route
    t3 = lambda b, j: (b, j, 0)
    full2 = lambda b, j: (0, 0)
    return pl.pallas_call(
        _hy_out_kernel, grid=(BATCH, SEQ // HY_TL),
        in_specs=[pl.BlockSpec((1, HY_TL, HY_WIDTH), t3),
                  pl.BlockSpec((1, HY_TL, HY_WIDTH), t3),
                  pl.BlockSpec((1, HY_TL, HY_WIDTH), t3),
                  pl.BlockSpec((1, HY_WIDTH), full2),
                  pl.BlockSpec((HY_WIDTH, D_MODEL), full2),
                  pl.BlockSpec((1, HY_TL, D_MODEL), t3),
                  pl.BlockSpec((1, 1, D_MODEL), lambda b, j: (b, 0, 0))] + r_in,
        out_specs=[pl.BlockSpec((1, HY_TL, D_MODEL), t3)] + r_out,
        out_shape=[jax.ShapeDtypeStruct((BATCH, SEQ, D_MODEL), F32)] + r_shape,
        scratch_shapes=r_scratch,
        compiler_params=_params(("arbitrary", "arbitrary")), name="hyena_out_proj_route",
    )(y, z, x0, bias, w, x, g1, *r_args)


def _hyena_filter(w1, b1, w2, b2, w3, freq):
    hi = lax.Precision.HIGHEST
    Lq = SEQ
    t = jnp.linspace(0.0, 1.0, Lq, dtype=F32)[:, None]
    ang = 2.0 * math.pi * jnp.arange(Lq, dtype=F32)[:, None] / Lq
    bands = jnp.linspace(1e-4, FILT_BANDS - 1, FILT_BANDS, dtype=F32)
    z = jnp.concatenate([t, jnp.cos(bands * ang), -jnp.sin(bands * ang)], axis=-1)
    hid = jnp.sin(freq * (jnp.dot(z, w1, precision=hi) + b1))
    hid = jnp.sin(freq * (jnp.dot(hid, w2, precision=hi) + b2))
    deltas = jnp.linspace(HY_MIN_DECAY, HY_MAX_DECAY, HY_WIDTH, dtype=F32)
    hf = jnp.dot(hid, w3, precision=hi) * jnp.exp(-t * jnp.tile(deltas, 2))
    ssq = jnp.sum(hf * hf, axis=0)
    ssq = ssq[:HY_WIDTH] + ssq[HY_WIDTH:]
    return hf * jnp.tile(lax.rsqrt(ssq + EPS), 2)


def kernel(x, c, ctx, c_ctx, ada_w, ada_b, ev_w_in, mla_q_norm, mla_w_uq, mla_kv_norm, mla_w_ukv, mla_q_qknorm, mla_k_qknorm, s5_lam_re, s5_lam_im, s5_log_step, s5_b_re, s5_b_im, s5_c_re, s5_c_im, s5_d, s5_glu_w, s5_glu_b, ev_w_out, hy_w_in, hy_conv_w, hy_conv_b, hy_f_w1, hy_f_b1, hy_f_w2, hy_f_b2, hy_f_w3, hy_f_freq, hy_bias, hy_w_out, moe_router_w, moe_router_b, moe_w_gate, moe_w_up, moe_w_down, moe_sh_gate, moe_sh_up, moe_sh_down):
    D = D_MODEL
    mod_all = _ada_call(c, c_ctx, ada_w, ada_b)

    def mods(li):
        return [m[:, None, :] for m in jnp.split(mod_all[li, :BATCH], 6, axis=-1)]

    sh1, sc1, g1, sh2, sc2, g2 = mods(0)
    mod_ctx = mod_all[0, BATCH, :2 * D]
    w0 = _a0_weights(ev_w_in[0], mla_q_norm[0], mla_w_uq[0], mla_kv_norm[0], mla_w_ukv[0],
                     mla_q_qknorm[0], mla_k_qknorm[0])
    q, k, v, u = _a0_call(x, ctx, sh1, sc1, mod_ctx[None, :D], mod_ctx[None, D:], w0)
    att = _attn_call(q, k, v)
    ug = u.reshape(BATCH, S5_NCHUNK, S5_CHUNK, S5_GROUPS, S5_GROUP)
    ug = ug.transpose(3, 1, 0, 2, 4).reshape(S5_GROUPS, S5_ROWS, S5_COLS)
    ys = _s5_call(ug, *_s5_weights(s5_lam_re[0], s5_lam_im[0], s5_log_step[0], s5_b_re[0], s5_b_im[0],
                                   s5_c_re[0], s5_c_im[0]))
    ys = ys.reshape(S5_GROUPS, S5_NCHUNK_LAT, BATCH, S5_CHUNK, S5_GROUP)
    ys = ys.transpose(2, 1, 3, 0, 4).reshape(BATCH, SEQ, S5_WIDTH)
    wo = ev_w_out[0].astype(BF16)
    wo_att = jnp.concatenate([wo[:MLA_WIDTH].reshape(MLA_HEADS, V_HEAD, D_MODEL),
                              jnp.zeros((MLA_HEADS, HEAD_PAD - V_HEAD, D_MODEL), BF16)], axis=1)
    x, *routed = _a1_call(att, ys, u, s5_d[0][None, :], s5_glu_w[0].astype(BF16), s5_glu_b[0][None, :],
                          wo_att.reshape(MLA_HEADS * HEAD_PAD, D_MODEL), wo[MLA_WIDTH:], x, g1,
                          _route_plumbing(sh2, sc2, moe_router_w[0], moe_router_b[0]))
    x = _moe(x, routed, sh2, sc2, g2, moe_w_gate, moe_w_up, moe_w_down,
             moe_sh_gate[0], moe_sh_up[0], moe_sh_down[0], 0)

    sh1, sc1, g1, sh2, sc2, g2 = mods(1)
    z, x0 = _hy_in_call(x, sh1, sc1, hy_w_in[0].astype(BF16), hy_conv_w[0], hy_conv_b[0][None, :])
    fft_tabs = _fft_tables()
    hfb = _hyena_filter(hy_f_w1[0], hy_f_b1[0], hy_f_w2[0], hy_f_b2[0], hy_f_w3[0], hy_f_freq[0])
    y = _conv_call(z, _spec_call(hfb, *fft_tabs), *fft_tabs)
    x, *routed = _hy_out_call(y, z, x0, hy_bias[0][None, :], hy_w_out[0].astype(BF16), x, g1,
                              _route_plumbing(sh2, sc2, moe_router_w[1], moe_router_b[1]))
    x = _moe(x, routed, sh2, sc2, g2, moe_w_gate, moe_w_up, moe_w_down,
             moe_sh_gate[1], moe_sh_up[1], moe_sh_down[1], 1)
    return x
```

```python
import functools
import math

import numpy as np
import jax
import jax.numpy as jnp
from jax import lax
from jax.experimental import pallas as pl
from jax.experimental.pallas import tpu as pltpu
from jax.experimental.pallas import tpu_sc as plsc

F32 = jnp.float32
BF16 = jnp.bfloat16

D_MODEL = 1024
BATCH = 8
SEQ = 4096
CTX_LEN = 256
KV_LEN = SEQ + CTX_LEN
GRID_W = 64
EPS = 1e-6

MLA_HEADS = 8
QK_NOPE = 64
QK_ROPE = 32
QK_HEAD = QK_NOPE + QK_ROPE
V_HEAD = 64
Q_LORA = 256
KV_LORA = 128
MLA_WIDTH = MLA_HEADS * V_HEAD
ROPE_BASE = 10000.0
HEAD_PAD = 128

S5_WIDTH = 512
S5_GROUP = 16
S5_GROUPS = S5_WIDTH // S5_GROUP
S5_STATE = 64
S5_CHUNK = 32
S5_NCHUNK = KV_LEN // S5_CHUNK
S5_NCHUNK_LAT = SEQ // S5_CHUNK
S5_NCHUNK_CTX = CTX_LEN // S5_CHUNK

HY_WIDTH = D_MODEL
FILT_EMB = 33
FILT_BANDS = (FILT_EMB - 1) // 2
FILT_PAD = 128
SHORT_CONV = 3
HY_MIN_DECAY = -math.log(1e-2) / 1.5
HY_MAX_DECAY = -math.log(1e-2) / 0.3
DFT_N = 2 * SEQ
FFT_N1 = 16
FFT_N2 = DFT_N // FFT_N1
FFT_NA = FFT_N1 // 2
FFT_NK = FFT_N1 // 2 + 1
HY_CT = 256

N_EXPERTS = 64
TOP_K = 6
EXPERT_FF = 256
ROUTE_SCALE = 2.5
MOE_BLOCK = 1024
ROW_WORDS = D_MODEL // 4
SC_WINDOW = 128

V7X_LANES = 128
V7X_VMEM_BYTES = 64 * 1024 * 1024
VMEM_LIMIT = V7X_VMEM_BYTES - 8 * 1024 * 1024


def _params(semantics):
    return pltpu.CompilerParams(dimension_semantics=semantics, vmem_limit_bytes=VMEM_LIMIT)


def _norm_mod(x, shift, scale):
    ms = jnp.mean(x * x, axis=-1, keepdims=True)
    return x * lax.rsqrt(ms + EPS) * (1.0 + scale) + shift


def _rms(x, gain, n):
    ms = jnp.sum(x * x, axis=-1, keepdims=True) * (1.0 / n)
    return x * lax.rsqrt(ms + EPS) * gain


ADA_ROWS = 16
ADA_TN = 1024


def _ada_kernel(c_ref, w_ref, b_ref, o_ref):
    sc = jax.nn.silu(c_ref[...])
    o_ref[0] = jnp.dot(sc, w_ref[0], preferred_element_type=F32, precision=lax.Precision.HIGHEST) + b_ref[0]


def _ada_call(c, c_ctx, ada_w, ada_b):
    depth, _, width = ada_w.shape
    rows = jnp.concatenate([c, c_ctx[None, :], jnp.zeros((ADA_ROWS - BATCH - 1, D_MODEL), F32)], axis=0)
    return pl.pallas_call(
        _ada_kernel, grid=(depth, width // ADA_TN),
        in_specs=[pl.BlockSpec((ADA_ROWS, D_MODEL), lambda l, n: (0, 0)),
                  pl.BlockSpec((1, D_MODEL, ADA_TN), lambda l, n: (l, 0, n)),
                  pl.BlockSpec((1, 1, ADA_TN), lambda l, n: (l, 0, n))],
        out_specs=pl.BlockSpec((1, ADA_ROWS, ADA_TN), lambda l, n: (l, 0, n)),
        out_shape=jax.ShapeDtypeStruct((depth, ADA_ROWS, width), F32),
        compiler_params=_params(("parallel", "arbitrary")), name="adaln_modulation",
    )(rows, ada_w, ada_b[:, None, :])


A0_TL = 256
A0_NT = SEQ // A0_TL


def _rope_perm():
    return np.concatenate([np.arange(0, QK_ROPE, 2), np.arange(1, QK_ROPE, 2)])


def _rope_tables():
    t = np.arange(SEQ)
    row = (t // GRID_W).astype(np.float64)
    col = (t % GRID_W).astype(np.float64)
    n_freq = QK_ROPE // 4
    inv = ROPE_BASE ** (-np.arange(n_freq, dtype=np.float64) / n_freq)
    ang = np.concatenate([row[:, None] * inv, col[:, None] * inv], axis=-1)
    cos, sin = np.cos(ang), np.sin(ang)
    half = QK_ROPE // 2
    a = np.zeros((KV_LEN, HEAD_PAD))
    b = np.zeros((KV_LEN, HEAD_PAD))
    a[:, :QK_HEAD] = 1.0
    a[:SEQ, QK_NOPE:QK_NOPE + half] = cos
    a[:SEQ, QK_NOPE + half:QK_HEAD] = cos
    b[:SEQ, QK_NOPE:QK_NOPE + half] = -sin
    b[:SEQ, QK_NOPE + half:QK_HEAD] = sin
    return a, b


def _norm_rope_heads(f, gain_ref, a, b, out_ref):
    width = MLA_HEADS * HEAD_PAD
    for hd in range(MLA_HEADS):
        sl = slice(hd * HEAD_PAD, (hd + 1) * HEAD_PAD)
        x = f[:, sl]
        r = lax.rsqrt(jnp.sum(x * x, axis=-1, keepdims=True) * (1.0 / QK_HEAD) + EPS)
        rot = x * (a * gain_ref[:, sl]) + f[:, width + hd * HEAD_PAD:width + (hd + 1) * HEAD_PAD] * b
        out_ref[0, :, sl] = (rot * r).astype(BF16)


def _a0_kernel(x_ref, ctx_ref, sh_ref, sc_ref, shc_ref, scc_ref, win_ref, qn_ref, wuq_ref, kvn_ref,
               wk_ref, wuv_ref, qg_ref, kg_ref, ka_ref, kb_ref, qa_ref, qb_ref,
               q_ref, k_ref, v_ref, u_ref):
    j = pl.program_id(1)
    is_ctx = j == A0_NT
    xin = jnp.where(is_ctx, ctx_ref[0], x_ref[0])
    shift = jnp.where(is_ctx, shc_ref[...], sh_ref[0])
    scale = jnp.where(is_ctx, scc_ref[...], sc_ref[0])
    h = _norm_mod(xin, shift, scale).astype(BF16)
    proj = jnp.dot(h, win_ref[...], preferred_element_type=F32)
    u_ref[0] = proj[:, 512:].astype(BF16)

    c_kv = _rms(proj[:, Q_LORA:Q_LORA + KV_LORA], kvn_ref[...], KV_LORA).astype(BF16)
    lane = lax.broadcasted_iota(jnp.int32, (1, MLA_HEADS * HEAD_PAD), 1)
    ones_lane = (lane % HEAD_PAD == V_HEAD).astype(F32)
    v_ref[0] = (jnp.dot(c_kv, wuv_ref[...], preferred_element_type=F32) + ones_lane).astype(BF16)
    kin = jnp.concatenate([c_kv, proj[:, 384:512].astype(BF16)], axis=1)
    kf = jnp.dot(kin, wk_ref[...], preferred_element_type=F32)
    _norm_rope_heads(kf, kg_ref, ka_ref[...], kb_ref[...], k_ref)

    @pl.when(j < A0_NT)
    def _():
        ql = _rms(proj[:, :Q_LORA], qn_ref[...], Q_LORA).astype(BF16)
        qf = jnp.dot(ql, wuq_ref[...], preferred_element_type=F32)
        _norm_rope_heads(qf, qg_ref, qa_ref[...], qb_ref[...], q_ref)


def _a0_call(x, ctx, sh, sc, shc, scc, w):
    nt = A0_NT
    lat = lambda b, j: (b, jnp.minimum(j, nt - 1), 0)
    full2 = lambda b, j: (0, 0)
    per_b = lambda b, j: (b, 0, 0)
    tab = pl.BlockSpec((A0_TL, HEAD_PAD), lambda b, j: (j, 0))
    in_specs = [
        pl.BlockSpec((1, A0_TL, D_MODEL), lat),
        pl.BlockSpec((1, CTX_LEN, D_MODEL), per_b),
        pl.BlockSpec((1, 1, D_MODEL), per_b),
        pl.BlockSpec((1, 1, D_MODEL), per_b),
        pl.BlockSpec((1, D_MODEL), full2),
        pl.BlockSpec((1, D_MODEL), full2),
        pl.BlockSpec((D_MODEL, 1024), full2),
        pl.BlockSpec((1, Q_LORA), full2),
        pl.BlockSpec((Q_LORA, 2 * MLA_HEADS * HEAD_PAD), full2),
        pl.BlockSpec((1, KV_LORA), full2),
        pl.BlockSpec((2 * KV_LORA, 2 * MLA_HEADS * HEAD_PAD), full2),
        pl.BlockSpec((KV_LORA, MLA_HEADS * HEAD_PAD), full2),
        pl.BlockSpec((1, MLA_HEADS * HEAD_PAD), full2),
        pl.BlockSpec((1, MLA_HEADS * HEAD_PAD), full2),
        tab, tab, tab, tab,
    ]
    out_specs = [
        pl.BlockSpec((1, A0_TL, MLA_HEADS * HEAD_PAD), lat),
        pl.BlockSpec((1, A0_TL, MLA_HEADS * HEAD_PAD), lambda b, j: (b, j, 0)),
        pl.BlockSpec((1, A0_TL, MLA_HEADS * HEAD_PAD), lambda b, j: (b, j, 0)),
        pl.BlockSpec((1, A0_TL, S5_WIDTH), lambda b, j: (b, j, 0)),
    ]
    out_shape = [
        jax.ShapeDtypeStruct((BATCH, SEQ, MLA_HEADS * HEAD_PAD), BF16),
        jax.ShapeDtypeStruct((BATCH, KV_LEN, MLA_HEADS * HEAD_PAD), BF16),
        jax.ShapeDtypeStruct((BATCH, KV_LEN, MLA_HEADS * HEAD_PAD), BF16),
        jax.ShapeDtypeStruct((BATCH, KV_LEN, S5_WIDTH), BF16),
    ]
    return pl.pallas_call(
        _a0_kernel, grid=(BATCH, nt + 1), in_specs=in_specs, out_specs=out_specs, out_shape=out_shape,
        compiler_params=_params(("parallel", "arbitrary")), name="even_in_proj",
    )(x, ctx, sh, sc, shc, scc, *w)


def _gain_swap(w, gain):
    half = QK_ROPE // 2
    wg = (w * gain).reshape(w.shape[0], MLA_HEADS, HEAD_PAD)
    re, im = wg[..., QK_NOPE:QK_NOPE + half], wg[..., QK_NOPE + half:QK_HEAD]
    out = jnp.concatenate([jnp.zeros_like(wg[..., :QK_NOPE]), im, re, jnp.zeros_like(wg[..., QK_HEAD:])], axis=-1)
    return out.reshape(w.shape)


def _a0_weights(w_in, q_norm, w_uq, kv_norm, w_ukv, q_qk, k_qk):
    perm = _rope_perm()
    kr0 = Q_LORA + KV_LORA
    w_cat = jnp.concatenate([
        w_in[:, :kr0], w_in[:, kr0:kr0 + QK_ROPE][:, perm],
        jnp.zeros((D_MODEL, HEAD_PAD - QK_ROPE), F32), w_in[:, kr0 + QK_ROPE:]], axis=1).astype(BF16)
    pad = HEAD_PAD - QK_HEAD

    def head_gain(g):
        gh = jnp.concatenate([g[:QK_NOPE], g[QK_NOPE:][perm], jnp.zeros((pad,), F32)])
        return jnp.tile(gh, MLA_HEADS)[None, :]

    uq = w_uq.reshape(Q_LORA, MLA_HEADS, QK_HEAD)
    uq = jnp.concatenate([uq[..., :QK_NOPE], uq[..., QK_NOPE:][..., perm],
                          jnp.zeros((Q_LORA, MLA_HEADS, pad), F32)], axis=-1)
    uq = uq.reshape(Q_LORA, MLA_HEADS * HEAD_PAD)
    uq = jnp.concatenate([uq, _gain_swap(uq, head_gain(q_qk))], axis=1).astype(BF16)
    ukv = w_ukv.reshape(KV_LORA, MLA_HEADS, QK_NOPE + V_HEAD)
    uk = jnp.concatenate([ukv[..., :QK_NOPE], jnp.zeros((KV_LORA, MLA_HEADS, HEAD_PAD - QK_NOPE), F32)], axis=-1)
    uk = uk.reshape(KV_LORA, MLA_HEADS * HEAD_PAD)
    place = np.zeros((KV_LORA, MLA_HEADS, HEAD_PAD), np.float32)
    for i in range(QK_ROPE):
        place[i, :, QK_NOPE + i] = 1.0
    wk = jnp.concatenate([uk, jnp.asarray(place.reshape(KV_LORA, MLA_HEADS * HEAD_PAD))], axis=0)
    wk = jnp.concatenate([wk, _gain_swap(wk, head_gain(k_qk))], axis=1).astype(BF16)
    wuv = jnp.concatenate([ukv[..., QK_NOPE:], jnp.zeros((KV_LORA, MLA_HEADS, HEAD_PAD - V_HEAD), F32)], axis=-1)
    wuv = wuv.reshape(KV_LORA, MLA_HEADS * HEAD_PAD).astype(BF16)
    a, b = _rope_tables()
    qs = QK_HEAD ** -0.5 * math.log2(math.e)
    tabs = [jnp.asarray(t, F32) for t in (a, b, a * qs, b * qs)]
    return [w_cat, q_norm[None, :], uq, kv_norm[None, :], wk, wuv, head_gain(q_qk), head_gain(k_qk)] + tabs


ATT_TQ = 256
HEADS_PER_STEP = 4


def _attn_kernel(q_ref, k_ref, v_ref, o_ref):
    for hh in range(HEADS_PER_STEP):
        sl = slice(hh * HEAD_PAD, (hh + 1) * HEAD_PAD)
        s = lax.dot_general(q_ref[0, :, sl], k_ref[0, :, sl], (((1,), (1,)), ((), ())),
                            preferred_element_type=F32)
        m = jnp.max(s, axis=-1, keepdims=True)
        p = jnp.exp2(s - m).astype(BF16)
        acc = jnp.dot(p, v_ref[0, :, sl], preferred_element_type=F32)
        o_ref[0, :, sl] = (acc * (1.0 / acc[:, V_HEAD:V_HEAD + 1])).astype(BF16)


def _attn_call(q, k, v):
    wq = HEADS_PER_STEP * HEAD_PAD
    return pl.pallas_call(
        _attn_kernel, grid=(BATCH, MLA_HEADS // HEADS_PER_STEP, SEQ // ATT_TQ),
        in_specs=[pl.BlockSpec((1, ATT_TQ, wq), lambda b, h, i: (b, i, h)),
                  pl.BlockSpec((1, KV_LEN, wq), lambda b, h, i: (b, 0, h)),
                  pl.BlockSpec((1, KV_LEN, wq), lambda b, h, i: (b, 0, h))],
        out_specs=pl.BlockSpec((1, ATT_TQ, wq), lambda b, h, i: (b, i, h)),
        out_shape=jax.ShapeDtypeStruct((BATCH, SEQ, MLA_HEADS * HEAD_PAD), BF16),
        compiler_params=_params(("parallel", "parallel", "arbitrary")), name="mla_attention",
    )(q, k, v)


S5_ROWS = S5_NCHUNK * BATCH
S5_ROWS_LAT = S5_NCHUNK_LAT * BATCH
S5_COLS = S5_CHUNK * S5_GROUP
S5_SW = 2 * S5_STATE


def _s5_kernel(u_ref, r_ref, mb_ref, mc_ref, coef_ref, y_ref, x_sc, sp_sc, t_sc):
    u = u_ref[0]
    lags = r_ref[0]
    for sg in range(S5_CHUNK):
        off = (S5_CHUNK - 1 - sg) * S5_GROUP
        t_sc[sg * S5_GROUP:(sg + 1) * S5_GROUP, :] = lags[:, off:off + S5_COLS].astype(BF16)
    x_sc[...] = jnp.dot(u, mb_ref[0], preferred_element_type=F32)
    cf = coef_ref[0]
    af, bfm, bfp, ab, bbm, bbp = [cf[i * 8:(i + 1) * 8] for i in range(6)]

    def body(i, carry):
        sf, sfw, sb, sbw = carry
        cfw = jnp.where(i < S5_NCHUNK_CTX, i + S5_NCHUNK_LAT, i - S5_NCHUNK_CTX)
        rf = pl.multiple_of(cfw * BATCH, BATCH)
        rb = pl.multiple_of((S5_NCHUNK - 1 - i) * BATCH, BATCH)
        sp_sc[pl.ds(rf, BATCH), 0:S5_SW] = sf
        sp_sc[pl.ds(rb, BATCH), S5_SW:2 * S5_SW] = sb
        xf = x_sc[pl.ds(rf, BATCH), 0:S5_SW]
        xfw = x_sc[pl.ds(rf, BATCH), S5_SW:2 * S5_SW]
        xb = x_sc[pl.ds(rb, BATCH), 2 * S5_SW:3 * S5_SW]
        xbw = x_sc[pl.ds(rb, BATCH), 3 * S5_SW:4 * S5_SW]
        return (sf * af + sfw * bfm + xf, sfw * af + sf * bfp + xfw,
                sb * ab + sbw * bbm + xb, sbw * ab + sb * bbp + xbw)

    z = jnp.zeros((BATCH, S5_SW), F32)
    lax.fori_loop(0, S5_NCHUNK, body, (z, z, z, z))
    y = jnp.dot(u[:S5_ROWS_LAT], t_sc[...], preferred_element_type=F32)
    y = y + jnp.dot(sp_sc[0:S5_ROWS_LAT, :].astype(BF16), mc_ref[0], preferred_element_type=F32)
    y_ref[0] = y.astype(BF16)


def _s5_call(ug, t, mb, mc, coef):
    g3 = lambda g: (g, 0, 0)
    return pl.pallas_call(
        _s5_kernel, grid=(S5_GROUPS,),
        in_specs=[pl.BlockSpec((1, S5_ROWS, S5_COLS), g3),
                  pl.BlockSpec((1, S5_GROUP, 2 * S5_COLS), g3),
                  pl.BlockSpec((1, S5_COLS, 4 * S5_SW), g3),
                  pl.BlockSpec((1, 2 * S5_SW, S5_COLS), g3),
                  pl.BlockSpec((1, 6 * 8, S5_SW), g3)],
        out_specs=pl.BlockSpec((1, S5_ROWS_LAT, S5_COLS), g3),
        out_shape=jax.ShapeDtypeStruct((S5_GROUPS, S5_ROWS_LAT, S5_COLS), BF16),
        scratch_shapes=[pltpu.VMEM((S5_ROWS, 4 * S5_SW), F32), pltpu.VMEM((S5_ROWS, 2 * S5_SW), F32),
                        pltpu.VMEM((S5_COLS, S5_COLS), BF16)],
        compiler_params=_params(("parallel",)), name="s5_chunked_scan",
    )(ug, t, mb, mc, coef)


def _s5_weights(lam_re, lam_im, log_step, b_re, b_im, c_re, c_im):
    q = S5_CHUNK
    hi = lax.Precision.HIGHEST
    t_blocks, mbs, mcs, coefs = [], [], [], []
    sig = jnp.arange(q)
    for d in range(2):
        lr = jnp.minimum(lam_re[d], -1e-4)
        li = lam_im[d]
        step = jnp.exp(log_step[d])[:, None]
        jj = jnp.arange(q + 1, dtype=F32)[:, None, None]
        mag = jnp.exp(lr * step * jj)
        ph = li * step * jj
        pr, pi = mag * jnp.cos(ph), mag * jnp.sin(ph)
        nr, ni = pr[1] - 1.0, pi[1]
        den = lr * lr + li * li
        fr, fi = (nr * lr + ni * li) / den, (ni * lr - nr * li) / den
        br = fr[..., None] * b_re[d] - fi[..., None] * b_im[d]
        bi = fr[..., None] * b_im[d] + fi[..., None] * b_re[d]
        cr, ci = c_re[d], c_im[d]
        cpr = cr[None] * pr[:, :, None, :] - ci[None] * pi[:, :, None, :]
        cpi = cr[None] * pi[:, :, None, :] + ci[None] * pr[:, :, None, :]
        kern = jnp.einsum('jghp,gpk->jghk', jnp.concatenate([cpr[:q], -cpi[:q]], axis=-1),
                          jnp.concatenate([br, bi], axis=1), precision=hi)
        kt = kern.transpose(1, 3, 0, 2)
        zero_slots = jnp.zeros((S5_GROUPS, S5_GROUP, q, S5_GROUP), F32)
        if d == 0:
            t_blocks.append(jnp.concatenate([zero_slots[:, :, :q - 1], kt, zero_slots[:, :, :1]], axis=2))
        else:
            t_blocks.append(jnp.concatenate([kt[:, :, ::-1], zero_slots], axis=2))
        pw = (q - 1 - sig) if d == 0 else sig
        xr = pr[pw][..., None] * br[None] - pi[pw][..., None] * bi[None]
        xi = pr[pw][..., None] * bi[None] + pi[pw][..., None] * br[None]
        xr = xr.transpose(1, 0, 3, 2).reshape(S5_GROUPS, S5_COLS, S5_STATE)
        xi = xi.transpose(1, 0, 3, 2).reshape(S5_GROUPS, S5_COLS, S5_STATE)
        mbs += [xr, xi, xi, xr]
        po = (sig + 1) if d == 0 else (q - sig)
        mr = cpr[po].transpose(1, 3, 0, 2).reshape(S5_GROUPS, S5_STATE, S5_COLS)
        mi = -cpi[po].transpose(1, 3, 0, 2).reshape(S5_GROUPS, S5_STATE, S5_COLS)
        mcs += [mr, mi]
        are, aim = pr[q], pi[q]
        rows = [jnp.concatenate([are, are], -1), jnp.concatenate([-aim, aim], -1), jnp.concatenate([aim, -aim], -1)]
        coefs += [jnp.broadcast_to(r[:, None, :], (S5_GROUPS, 8, S5_SW)) for r in rows]
    t = (t_blocks[0] + t_blocks[1]).reshape(S5_GROUPS, S5_GROUP, 2 * S5_COLS)
    mb = jnp.concatenate(mbs, axis=-1).astype(BF16)
    mc = jnp.concatenate(mcs, axis=1).astype(BF16)
    coef = jnp.concatenate(coefs, axis=1)
    return t, mb, mc, coef


A1_TL = 512


def _a1_kernel(att_ref, ys_ref, u_ref, d_ref, gw_ref, gb_ref, woa_ref, wos_ref, x_ref, g_ref, *route):
    o_ref = route[5]
    y = u_ref[0].astype(F32) * d_ref[...] + ys_ref[0].astype(F32)
    z = jax.nn.gelu(y)
    gate = jax.nn.sigmoid(jnp.dot(z.astype(BF16), gw_ref[...], preferred_element_type=F32) + gb_ref[...])
    s5 = (z * gate).astype(BF16)
    mix = jnp.dot(att_ref[0], woa_ref[...], preferred_element_type=F32)
    mix = mix + jnp.dot(s5, wos_ref[...], preferred_element_type=F32)
    x_new = x_ref[0] + g_ref[0] * mix
    o_ref[0] = x_new
    _route_tile(x_new, *route[:5], *route[6:])


def _a1_call(att, ys, u, d, gw, gb, woa, wos, x, g1, route):
    r_args, r_in, r_out, r_shape, r_scratch = route
    t3 = lambda b, j: (b, j, 0)
    full2 = lambda b, j: (0, 0)
    return pl.pallas_call(
        _a1_kernel, grid=(BATCH, SEQ // A1_TL),
        in_specs=[pl.BlockSpec((1, A1_TL, MLA_HEADS * HEAD_PAD), t3),
                  pl.BlockSpec((1, A1_TL, S5_WIDTH), t3),
                  pl.BlockSpec((1, A1_TL, S5_WIDTH), t3),
                  pl.BlockSpec((1, S5_WIDTH), full2),
                  pl.BlockSpec((S5_WIDTH, S5_WIDTH), full2),
                  pl.BlockSpec((1, S5_WIDTH), full2),
                  pl.BlockSpec((MLA_HEADS * HEAD_PAD, D_MODEL), full2),
                  pl.BlockSpec((S5_WIDTH, D_MODEL), full2),
                  pl.BlockSpec((1, A1_TL, D_MODEL), t3),
                  pl.BlockSpec((1, 1, D_MODEL), lambda b, j: (b, 0, 0))] + r_in,
        out_specs=[pl.BlockSpec((1, A1_TL, D_MODEL), t3)] + r_out,
        out_shape=[jax.ShapeDtypeStruct((BATCH, SEQ, D_MODEL), F32)] + r_shape,
        scratch_shapes=r_scratch,
        compiler_params=_params(("arbitrary", "arbitrary")), name="even_out_proj_route",
    )(att, ys, u, d, gw, gb, woa, wos, x, g1, *r_args)


MOE_TL = 512
SLOT_PAD = 8


def _pack_rows(v):
    halves = []
    for p in range(2):
        base = 2 * p * ROW_WORDS
        a = pltpu.bitcast(v[:, base:base + ROW_WORDS].astype(BF16).astype(F32), jnp.uint32)
        b = pltpu.bitcast(v[:, base + ROW_WORDS:base + 2 * ROW_WORDS].astype(BF16).astype(F32), jnp.uint32)
        halves.append((a >> 16) | b)
    return halves


def _unpack_rows(lo, hi):
    out = []
    for w in (lo, hi):
        out.append(pltpu.bitcast(w << 16, F32))
        out.append(pltpu.bitcast(w & jnp.uint32(0xFFFF0000), F32))
    return out


def _route_tile(x, sh_ref, sc_ref, rwt_ref, rb_ref, tri_ref,
                hlo_ref, hhi_ref, idx_ref, wt_ref, rank_ref, cnt_ref, run_sc):
    @pl.when((pl.program_id(0) == 0) & (pl.program_id(1) == 0))
    def _():
        run_sc[...] = jnp.zeros_like(run_sc)

    h = _norm_mod(x, sh_ref[0], sc_ref[0])
    hb = h.astype(BF16)
    hlo_ref[0], hhi_ref[0] = _pack_rows(h)
    h_lo = (h - hb.astype(F32)).astype(BF16)
    rwt = rwt_ref[...]
    rw_hi = rwt.astype(BF16)
    rw_lo = (rwt - rw_hi.astype(F32)).astype(BF16)
    logits = lax.dot_general(jnp.concatenate([rw_hi, rw_lo, rw_hi], axis=1), jnp.concatenate([hb, hb, h_lo], axis=1),
                             (((1,), (1,)), ((), ())), preferred_element_type=F32)
    scores = jax.nn.sigmoid(logits)

    work = scores + rb_ref[...]
    expert = lax.broadcasted_iota(jnp.int32, work.shape, 0).astype(F32)
    hits, ids = [], []
    for _ in range(TOP_K):
        m = jnp.max(work, axis=0, keepdims=True)
        ik = jnp.min(jnp.where(work == m, expert, float(N_EXPERTS)), axis=0, keepdims=True)
        hit = expert == ik
        hits.append(hit)
        ids.append(ik)
        work = jnp.where(hit, -jnp.inf, work)
    mask = hits[0]
    for hit in hits[1:]:
        mask = jnp.logical_or(mask, hit)
    maskf = mask.astype(F32)
    before = jnp.dot(maskf.astype(BF16), tri_ref[...], preferred_element_type=F32) + run_sc[:, 0:1]
    sel = [jnp.sum(jnp.where(hit, scores, 0.0), axis=0, keepdims=True) for hit in hits]
    denom = sel[0]
    for s in sel[1:]:
        denom = denom + s
    ranks = [jnp.sum(jnp.where(hit, before, 0.0), axis=0, keepdims=True) for hit in hits]
    pad = [jnp.zeros_like(denom)] * (SLOT_PAD - TOP_K)
    idx_ref[...] = jnp.concatenate(ids + pad, axis=0).astype(jnp.int32)
    wt_ref[...] = jnp.concatenate([s / denom * ROUTE_SCALE for s in sel] + pad, axis=0)
    rank_ref[...] = jnp.concatenate(ranks + pad, axis=0).astype(jnp.int32)
    run_sc[...] += jnp.sum(maskf, axis=1, keepdims=True)
    cnt_ref[...] = run_sc[...]


def _route_plumbing(sh, sc, rw, rb):
    t3 = lambda b, j: (b, j, 0)
    full2 = lambda b, j: (0, 0)
    per_b = lambda b, j: (b, 0, 0)
    nt = SEQ // MOE_TL
    slots = lambda b, j: (0, b * nt + j)
    tri = jnp.asarray(np.triu(np.ones((MOE_TL, MOE_TL), np.float32), 1), BF16)
    args = [sh, sc, rw.T, rb[:, None], tri]
    in_specs = [pl.BlockSpec((1, 1, D_MODEL), per_b),
                pl.BlockSpec((1, 1, D_MODEL), per_b),
                pl.BlockSpec((N_EXPERTS, D_MODEL), full2),
                pl.BlockSpec((N_EXPERTS, 1), full2),
                pl.BlockSpec((MOE_TL, MOE_TL), full2)]
    out_specs = [pl.BlockSpec((1, MOE_TL, ROW_WORDS), t3),
                 pl.BlockSpec((1, MOE_TL, ROW_WORDS), t3),
                 pl.BlockSpec((SLOT_PAD, MOE_TL), slots),
                 pl.BlockSpec((SLOT_PAD, MOE_TL), slots),
                 pl.BlockSpec((SLOT_PAD, MOE_TL), slots),
                 pl.BlockSpec((N_EXPERTS, V7X_LANES), full2)]
    slot_i = jax.ShapeDtypeStruct((SLOT_PAD, BATCH * SEQ), jnp.int32)
    out_shape = [jax.ShapeDtypeStruct((BATCH, SEQ, ROW_WORDS), jnp.uint32),
                 jax.ShapeDtypeStruct((BATCH, SEQ, ROW_WORDS), jnp.uint32),
                 slot_i,
                 jax.ShapeDtypeStruct((SLOT_PAD, BATCH * SEQ), F32),
                 slot_i,
                 jax.ShapeDtypeStruct((N_EXPERTS, V7X_LANES), F32)]
    scratch = [pltpu.VMEM((N_EXPERTS, V7X_LANES), F32)]
    return args, in_specs, out_specs, out_shape, scratch


def _sc_mesh():
    return plsc.VectorSubcoreMesh(core_axis_name="c", subcore_axis_name="s")


def _sc_dispatch(h_words, dest, n_rows):
    n_tok = h_words.shape[0]

    @pl.kernel(out_type=jax.ShapeDtypeStruct((n_rows, ROW_WORDS), jnp.uint32), mesh=_sc_mesh(), scratch_types=[])
    def scatter_rows(h_hbm, i_hbm, o_hbm):
        def body(h_vmem, i_vmem):
            for k in range(TOP_K):
                pltpu.sync_copy(h_vmem, o_hbm.at[i_vmem.at[k]])

        pltpu.emit_pipeline(
            body, grid=(n_tok // SC_WINDOW,),
            in_specs=[pl.BlockSpec((SC_WINDOW, ROW_WORDS), index_map=lambda i: (i, 0)),
                      pl.BlockSpec((SLOT_PAD, SC_WINDOW), index_map=lambda i: (0, i))],
            out_specs=[],
            core_axis_name=("c", "s"), dimension_semantics=(pltpu.PARALLEL,),
        )(h_hbm, i_hbm)

    return scatter_rows(h_words, dest)


def _sc_collect(y_words, dest):
    n_tok = dest.shape[1]

    @pl.kernel(out_type=jax.ShapeDtypeStruct((TOP_K, n_tok, ROW_WORDS), jnp.uint32), mesh=_sc_mesh(),
               scratch_types=[])
    def gather_rows(y_hbm, i_hbm, o_hbm):
        def body(i_vmem, o_vmem):
            pltpu.sync_copy(y_hbm.at[i_vmem.at[0]], o_vmem.at[0])

        pltpu.emit_pipeline(
            body, grid=(TOP_K, n_tok // SC_WINDOW),
            in_specs=[pl.BlockSpec((1, SC_WINDOW), index_map=lambda k, i: (k, i))],
            out_specs=[pl.BlockSpec((1, SC_WINDOW, ROW_WORDS), index_map=lambda k, i: (k, i, 0))],
            core_axis_name=("c", "s"), dimension_semantics=(pltpu.PARALLEL, pltpu.PARALLEL),
        )(i_hbm, o_hbm)

    return gather_rows(y_words, dest)


def _expert_kernel(be_ref, nv_ref, xlo_ref, xhi_ref, wg_ref, wu_ref, wd_ref, ylo_ref, yhi_ref,
                   wg_sc, wu_sc, wd_sc):
    i = pl.program_id(0)
    nv = nv_ref[i]

    @pl.when(jnp.logical_or(i == 0, be_ref[i] != be_ref[jnp.maximum(i - 1, 0)]))
    def _():
        wg_sc[...] = wg_ref[0, 0].astype(BF16)
        wu_sc[...] = wu_ref[0, 0].astype(BF16)
        wd_sc[...] = wd_ref[0, 0].astype(BF16)

    @pl.when(nv > 0)
    def _():
        parts = _unpack_rows(xlo_ref[...], xhi_ref[...])
        xb = jnp.concatenate([p.astype(BF16) for p in parts], axis=1)
        live = lax.broadcasted_iota(jnp.int32, xb.shape, 0) < nv
        xb = jnp.where(live, xb, jnp.zeros_like(xb))
        hid = jax.nn.silu(jnp.dot(xb, wg_sc[...], preferred_element_type=F32))
        hid = hid * jnp.dot(xb, wu_sc[...], preferred_element_type=F32)
        y = jnp.dot(hid.astype(BF16), wd_sc[...], preferred_element_type=F32)
        ylo_ref[...], yhi_ref[...] = _pack_rows(y)

    @pl.when(nv == 0)
    def _():
        ylo_ref[...] = jnp.zeros_like(ylo_ref)
        yhi_ref[...] = jnp.zeros_like(yhi_ref)


def _expert_call(block_e, n_valid, xlo, xhi, wg, wu, wd, li):
    n_rows = xlo.shape[0]
    n_blocks = n_rows // MOE_BLOCK
    rows = pl.BlockSpec((MOE_BLOCK, ROW_WORDS), lambda i, be, nv: (i, 0))
    grid_spec = pltpu.PrefetchScalarGridSpec(
        num_scalar_prefetch=2, grid=(n_blocks,),
        in_specs=[rows, rows,
                  pl.BlockSpec((1, 1, D_MODEL, EXPERT_FF), lambda i, be, nv: (li, be[i], 0, 0)),
                  pl.BlockSpec((1, 1, D_MODEL, EXPERT_FF), lambda i, be, nv: (li, be[i], 0, 0)),
                  pl.BlockSpec((1, 1, EXPERT_FF, D_MODEL), lambda i, be, nv: (li, be[i], 0, 0))],
        out_specs=[rows, rows],
        scratch_shapes=[pltpu.VMEM((D_MODEL, EXPERT_FF), BF16), pltpu.VMEM((D_MODEL, EXPERT_FF), BF16),
                        pltpu.VMEM((EXPERT_FF, D_MODEL), BF16)])
    out = jax.ShapeDtypeStruct((n_rows, ROW_WORDS), jnp.uint32)
    return pl.pallas_call(
        _expert_kernel, grid_spec=grid_spec, out_shape=[out, out],
        compiler_params=_params(("arbitrary",)), name="moe_experts",
    )(block_e, n_valid, xlo, xhi, wg, wu, wd)


def _combine_kernel(ylo_ref, yhi_ref, w_ref, x_ref, sh_ref, sc_ref, g_ref, sg_ref, su_ref, sd_ref, o_ref):
    hb = _norm_mod(x_ref[0], sh_ref[0], sc_ref[0]).astype(BF16)
    hid = jax.nn.silu(jnp.dot(hb, sg_ref[...].astype(BF16), preferred_element_type=F32))
    hid = hid * jnp.dot(hb, su_ref[...].astype(BF16), preferred_element_type=F32)
    shared = jnp.dot(hid.astype(BF16), sd_ref[...].astype(BF16), preferred_element_type=F32)
    w = w_ref[0]
    acc = [None] * 4
    for k in range(TOP_K):
        wk = w[:, k:k + 1]
        for c, part in enumerate(_unpack_rows(ylo_ref[k], yhi_ref[k])):
            acc[c] = wk * part if acc[c] is None else acc[c] + wk * part
    for c in range(4):
        sl = slice(c * ROW_WORDS, (c + 1) * ROW_WORDS)
        o_ref[0, :, sl] = x_ref[0, :, sl] + g_ref[0, :, sl] * (acc[c] + shared[:, sl])


def _combine_call(ylo, yhi, wts, x, sh, sc, g2, sg, su, sd):
    t3 = lambda b, j: (b, j, 0)
    per_b = lambda b, j: (b, 0, 0)
    full2 = lambda b, j: (0, 0)
    ff = sg.shape[1]
    nt = SEQ // MOE_TL
    rows = pl.BlockSpec((TOP_K, MOE_TL, ROW_WORDS), lambda b, j: (0, b * nt + j, 0))
    return pl.pallas_call(
        _combine_kernel, grid=(BATCH, nt),
        in_specs=[rows, rows,
                  pl.BlockSpec((1, MOE_TL, SLOT_PAD), t3),
                  pl.BlockSpec((1, MOE_TL, D_MODEL), t3),
                  pl.BlockSpec((1, 1, D_MODEL), per_b),
                  pl.BlockSpec((1, 1, D_MODEL), per_b),
                  pl.BlockSpec((1, 1, D_MODEL), per_b),
                  pl.BlockSpec((D_MODEL, ff), full2),
                  pl.BlockSpec((D_MODEL, ff), full2),
                  pl.BlockSpec((ff, D_MODEL), full2)],
        out_specs=pl.BlockSpec((1, MOE_TL, D_MODEL), t3),
        out_shape=jax.ShapeDtypeStruct((BATCH, SEQ, D_MODEL), F32),
        compiler_params=_params(("parallel", "arbitrary")), name="moe_combine_shared",
    )(ylo, yhi, wts, x, sh, sc, g2, sg, su, sd)


def _moe(x, routed, sh, sc, g2, w_gate, w_up, w_down, sh_gate, sh_up, sh_down, li):
    T = BATCH * SEQ
    TK = T * TOP_K
    hlo, hhi, idx, wts, rank, counts = routed
    wts = wts.T.reshape(BATCH, SEQ, SLOT_PAD)
    counts = counts[:, 0].astype(jnp.int32)
    padded = (counts + MOE_BLOCK - 1) // MOE_BLOCK * MOE_BLOCK
    pad_end = jnp.cumsum(padded)
    pad_start = pad_end - padded
    n_blocks = -(-TK // MOE_BLOCK) + N_EXPERTS
    n_rows = n_blocks * MOE_BLOCK
    block_start = jnp.arange(n_blocks, dtype=jnp.int32) * MOE_BLOCK
    owns = jnp.logical_and(block_start[:, None] >= pad_start[None, :], block_start[:, None] < pad_end[None, :])
    owns = owns.astype(jnp.int32)
    experts = jnp.arange(N_EXPERTS, dtype=jnp.int32)[None, :]
    block_e = jnp.sum(owns * experts, axis=1) + (N_EXPERTS - 1) * (1 - jnp.sum(owns, axis=1))
    n_valid = jnp.sum(owns * (counts[None, :] - (block_start[:, None] - pad_start[None, :])), axis=1)
    n_valid = jnp.clip(n_valid, 0, MOE_BLOCK).astype(jnp.int32)
    first_row = jnp.sum(jnp.where(idx[None] == experts.T[:, :, None], pad_start[:, None, None], 0), axis=0)
    dest = first_row + rank
    xlo = _sc_dispatch(hlo.reshape(T, ROW_WORDS), dest, n_rows)
    xhi = _sc_dispatch(hhi.reshape(T, ROW_WORDS), dest, n_rows)
    ylo, yhi = _expert_call(block_e, n_valid, xlo, xhi, w_gate, w_up, w_down, li)
    return _combine_call(_sc_collect(ylo, dest), _sc_collect(yhi, dest), wts, x, sh, sc, g2,
                         sh_gate, sh_up, sh_down)


HY_TL = 512
HALO = 8


def _hy_in_kernel(x_ref, xp_ref, xn_ref, sh_ref, sc_ref, w_ref, cw_ref, cb_ref, z_ref, x0_ref, h_sc):
    j = pl.program_id(1)
    shift, scale = sh_ref[0], sc_ref[0]
    keep_prev = (j > 0).astype(F32)
    keep_next = (j < SEQ // HY_TL - 1).astype(F32)
    h_sc[0:HALO, :] = _norm_mod(xp_ref[0], shift, scale) * keep_prev
    h_sc[HALO:HALO + HY_TL, :] = _norm_mod(x_ref[0], shift, scale)
    h_sc[HALO + HY_TL:, :] = _norm_mod(xn_ref[0], shift, scale) * keep_next
    hcat = h_sc[...].astype(BF16)
    outs = []
    for part in range(3):
        sl = slice(part * HY_WIDTH, (part + 1) * HY_WIDTH)
        p = jnp.dot(hcat, w_ref[:, sl], preferred_element_type=F32)
        o = (p[HALO - 1:HALO - 1 + HY_TL] * cw_ref[0:1, sl] + p[HALO:HALO + HY_TL] * cw_ref[1:2, sl]
             + p[HALO + 1:HALO + 1 + HY_TL] * cw_ref[2:3, sl] + cb_ref[:, sl])
        outs.append(o)
    x0_ref[0] = outs[0].astype(BF16)
    z_ref[0] = (outs[2] * outs[1]).astype(BF16)


def _hy_in_call(x, sh, sc, w, cw, cb):
    nb8 = HY_TL // HALO
    t3 = lambda b, j: (b, j, 0)
    full2 = lambda b, j: (0, 0)
    per_b = lambda b, j: (b, 0, 0)
    return pl.pallas_call(
        _hy_in_kernel, grid=(BATCH, SEQ // HY_TL),
        in_specs=[pl.BlockSpec((1, HY_TL, D_MODEL), t3),
                  pl.BlockSpec((1, HALO, D_MODEL), lambda b, j: (b, jnp.maximum(j * nb8 - 1, 0), 0)),
                  pl.BlockSpec((1, HALO, D_MODEL), lambda b, j: (b, jnp.minimum((j + 1) * nb8, SEQ // HALO - 1), 0)),
                  pl.BlockSpec((1, 1, D_MODEL), per_b),
                  pl.BlockSpec((1, 1, D_MODEL), per_b),
                  pl.BlockSpec((D_MODEL, 3 * HY_WIDTH), full2),
                  pl.BlockSpec((SHORT_CONV, 3 * HY_WIDTH), full2),
                  pl.BlockSpec((1, 3 * HY_WIDTH), full2)],
        out_specs=[pl.BlockSpec((1, HY_TL, HY_WIDTH), t3), pl.BlockSpec((1, HY_TL, HY_WIDTH), t3)],
        out_shape=[jax.ShapeDtypeStruct((BATCH, SEQ, HY_WIDTH), BF16),
                   jax.ShapeDtypeStruct((BATCH, SEQ, HY_WIDTH), BF16)],
        scratch_shapes=[pltpu.VMEM((HY_TL + 2 * HALO, D_MODEL), F32)],
        compiler_params=_params(("parallel", "arbitrary")), name="hyena_in_proj",
    )(x, x, x, sh, sc, w, cw, cb)


def _fft_tables():
    c = np.arange(FFT_N2, dtype=np.int64)
    ang = 2.0 * np.pi * ((c[:, None] * c[None, :]) % FFT_N2) / FFT_N2
    sr, si = np.cos(ang), -np.sin(ang)
    m = np.block([[sr, -si], [si, sr]])
    k1 = np.arange(FFT_NK, dtype=np.int64)
    ang_t = 2.0 * np.pi * (k1[:, None] * c[None, :]) / DFT_N
    lanes = np.ones((1, 1, V7X_LANES))
    tr = np.cos(ang_t)[:, :, None] * lanes
    ti = -np.sin(ang_t)[:, :, None] * lanes
    return jnp.asarray(m, BF16), jnp.asarray(tr, F32), jnp.asarray(ti, F32)


def _lin(acc, coef, val):
    if abs(coef) < 1e-12:
        return acc
    term = val if coef == 1.0 else (-val if coef == -1.0 else coef * val)
    return term if acc is None else acc + term


def _twiddle(tr_ref, ti_ref, k1, width):
    reps = width // V7X_LANES
    tr, ti = tr_ref[k1], ti_ref[k1]
    return jnp.concatenate([tr] * reps, axis=1), jnp.concatenate([ti] * reps, axis=1)


_COS_PI_4 = math.sqrt(0.5)


def _blocks_to_classes(z):
    assert FFT_N1 == 16
    sp = [z[a] + z[a + 4] for a in range(4)]
    sm = [z[a] - z[a + 4] for a in range(4)]
    q0p, q0m, q1p, q1m = sp[0] + sp[2], sp[0] - sp[2], sp[1] + sp[3], sp[1] - sp[3]
    d, e = _COS_PI_4 * (sm[1] - sm[3]), _COS_PI_4 * (sm[1] + sm[3])
    out = [None] * FFT_NK
    out[0] = (q0p + q1p, None)
    out[8] = (q0p - q1p, None)
    out[4] = (q0m, -q1m)
    out[2] = (sm[0] + d, -(sm[2] + e))
    out[6] = (sm[0] - d, sm[2] - e)
    for k in (1, 3):
        halves = []
        for parity in (0, 1):
            hr = hi = None
            for a in range(parity, FFT_NA, 2):
                th = 2.0 * math.pi * ((a * k) % FFT_N1) / FFT_N1
                hr = _lin(hr, round(math.cos(th), 15), z[a])
                hi = _lin(hi, round(-math.sin(th), 15), z[a])
            halves.append((hr, hi))
        (er, ei), (odr, odi) = halves
        out[k] = (er + odr, ei + odi)
        out[8 - k] = (er - odr, odi - ei)
    return out


def _classes_to_blocks(v):
    assert FFT_N1 == 16
    base = (v[0][0] + v[8][0], v[0][0] - v[8][0])
    pr = {k: (v[k][0] + v[8 - k][0], v[k][0] - v[8 - k][0]) for k in (1, 2, 3)}
    pi = {k: (v[k][1] - v[8 - k][1], v[k][1] + v[8 - k][1]) for k in (1, 2, 3)}
    out = []
    for a in range(FFT_NA):
        odd = a % 2
        mid = (v[4][0], 1.0 - (a % 4)) if not odd else (v[4][1], (a % 4) - 2.0)
        acc = _lin(base[odd], mid[1], mid[0])
        for k in (1, 2, 3):
            th = 2.0 * math.pi * ((a * k) % FFT_N1) / FFT_N1
            acc = _lin(acc, round(math.cos(th), 15), pr[k][odd])
            acc = _lin(acc, round(-math.sin(th), 15), pi[k][odd])
        out.append(acc)
    return out


def _class_forward(yr, yi, k1, m_ref, tr_ref, ti_ref, width):
    if k1 == 0:
        x = jnp.dot(m_ref[:, :FFT_N2], yr.astype(BF16), preferred_element_type=F32)
    else:
        tr, ti = _twiddle(tr_ref, ti_ref, k1, width)
        yr, yi = (yr * tr, yr * ti) if yi is None else (yr * tr - yi * ti, yr * ti + yi * tr)
        x = jnp.dot(m_ref[...], jnp.concatenate([yr, yi], axis=0).astype(BF16), preferred_element_type=F32)
    return x[:FFT_N2], x[FFT_N2:]


def _class_backward(pr, pi, k1, m_ref, tr_ref, ti_ref, width):
    v = jnp.dot(m_ref[...], jnp.concatenate([pr, -pi], axis=0).astype(BF16), preferred_element_type=F32)
    ur, ui = v[:FFT_N2], -v[FFT_N2:]
    if k1 > 0:
        tr, ti = _twiddle(tr_ref, ti_ref, k1, width)
        ur, ui = ur * tr + ui * ti, ui * tr - ur * ti
    return ur, ui


def _filter_features():
    t = np.linspace(0.0, 1.0, SEQ)[:, None]
    ang = 2.0 * np.pi * np.arange(SEQ)[:, None] / SEQ
    bands = np.linspace(1e-4, FILT_BANDS - 1, FILT_BANDS)
    z = np.concatenate([t, np.cos(bands * ang), -np.sin(bands * ang)], axis=-1)
    return jnp.asarray(np.pad(z, ((0, 0), (0, FILT_PAD - FILT_EMB))), F32)


def _spec_kernel(zf_ref, w1_ref, b1_ref, w2_ref, b2_ref, fq_ref, w3f_ref, w3b_ref, dl_ref, m_ref, tr_ref, ti_ref,
                 c_ref, hf_sc, hb_sc):
    hi = lax.Precision.HIGHEST
    zf, fq = zf_ref[...], fq_ref[...]
    hid = jnp.sin(fq * (jnp.dot(zf, w1_ref[...], preferred_element_type=F32, precision=hi) + b1_ref[...]))
    hid = jnp.sin(fq * (jnp.dot(hid, w2_ref[...], preferred_element_type=F32, precision=hi) + b2_ref[...]))
    decay = jnp.exp(-zf[:, 0:1] * dl_ref[...])
    hf = jnp.dot(hid, w3f_ref[...], preferred_element_type=F32, precision=hi) * decay
    hb = jnp.dot(hid, w3b_ref[...], preferred_element_type=F32, precision=hi) * decay
    ssq = jnp.sum(hf * hf, axis=0, keepdims=True) + jnp.sum(hb * hb, axis=0, keepdims=True)
    inv = lax.rsqrt(ssq + EPS)
    hf_sc[...] = hf * inv
    hb_sc[...] = hb * inv
    rows = lambda ref: [ref[a * FFT_N2:(a + 1) * FFT_N2, :] for a in range(FFT_NA)]
    fwd, bwd = _blocks_to_classes(rows(hf_sc)), _blocks_to_classes(rows(hb_sc))
    for k1 in range(FFT_NK):
        fr, fi = _class_forward(*fwd[k1], k1, m_ref, tr_ref, ti_ref, HY_CT)
        br, bi = _class_forward(*bwd[k1], k1, m_ref, tr_ref, ti_ref, HY_CT)
        scale = (1.0 if k1 in (0, FFT_N1 // 2) else 2.0) / DFT_N
        c_ref[k1, :FFT_N2, :] = ((fr + br) * scale).astype(BF16)
        c_ref[k1, FFT_N2:, :] = ((fi - bi) * scale).astype(BF16)


def _fft_table_specs(ngrid):
    z = (0,) * 2
    z3 = (0,) * 3
    if ngrid == 1:
        return [pl.BlockSpec((2 * FFT_N2, 2 * FFT_N2), lambda c: z),
                pl.BlockSpec((FFT_NK, FFT_N2, V7X_LANES), lambda c: z3),
                pl.BlockSpec((FFT_NK, FFT_N2, V7X_LANES), lambda c: z3)]
    return [pl.BlockSpec((2 * FFT_N2, 2 * FFT_N2), lambda b, c: z),
            pl.BlockSpec((FFT_NK, FFT_N2, V7X_LANES), lambda b, c: z3),
            pl.BlockSpec((FFT_NK, FFT_N2, V7X_LANES), lambda b, c: z3)]


def _spec_call(w1, b1, w2, b2, w3, freq, m, tr, ti):
    nct = HY_WIDTH // HY_CT

    def pad2(a, rows, cols):
        return jnp.pad(a, ((0, rows - a.shape[0]), (0, cols - a.shape[1])))

    full = lambda c: (0, 0)
    small = pl.BlockSpec((FILT_PAD, FILT_PAD), full)
    vec = pl.BlockSpec((1, FILT_PAD), full)
    deltas = jnp.asarray(np.linspace(HY_MIN_DECAY, HY_MAX_DECAY, HY_WIDTH)[None, :], F32)
    w3p = pad2(w3, FILT_PAD, 2 * HY_WIDTH)
    return pl.pallas_call(
        _spec_kernel, grid=(nct,),
        in_specs=[pl.BlockSpec((SEQ, FILT_PAD), full), small, vec, small, vec, vec,
                  pl.BlockSpec((FILT_PAD, HY_CT), lambda c: (0, c)),
                  pl.BlockSpec((FILT_PAD, HY_CT), lambda c: (0, c + nct)),
                  pl.BlockSpec((1, HY_CT), lambda c: (0, c))] + _fft_table_specs(1),
        out_specs=pl.BlockSpec((FFT_NK, 2 * FFT_N2, HY_CT), lambda c: (0, 0, c)),
        out_shape=jax.ShapeDtypeStruct((FFT_NK, 2 * FFT_N2, HY_WIDTH), BF16),
        scratch_shapes=[pltpu.VMEM((SEQ, HY_CT), F32), pltpu.VMEM((SEQ, HY_CT), F32)],
        compiler_params=_params(("arbitrary",)), name="hyena_filter_spectrum",
    )(_filter_features(), pad2(w1, FILT_PAD, FILT_PAD), pad2(b1[None, :], 1, FILT_PAD),
      pad2(w2, FILT_PAD, FILT_PAD), pad2(b2[None, :], 1, FILT_PAD), pad2(freq[None, :], 1, FILT_PAD),
      w3p, w3p, deltas, m, tr, ti)


def _conv_kernel(z_ref, c_ref, m_ref, tr_ref, ti_ref, y_ref, cls):
    for k1, (yr, yi) in enumerate(_blocks_to_classes(
            [z_ref[0, a * FFT_N2:(a + 1) * FFT_N2, :].astype(F32) for a in range(FFT_NA)])):
        cls[k1, :FFT_N2, :] = yr
        if yi is not None:
            cls[k1, FFT_N2:, :] = yi
    for k1 in range(FFT_NK):
        yi = None if k1 in (0, FFT_N1 // 2) else cls[k1, FFT_N2:, :]
        xr, xi = _class_forward(cls[k1, :FFT_N2, :], yi, k1, m_ref, tr_ref, ti_ref, HY_CT)
        cr = c_ref[k1, :FFT_N2, :].astype(F32)
        ci = c_ref[k1, FFT_N2:, :].astype(F32)
        ur, ui = _class_backward(xr * cr - xi * ci, xr * ci + xi * cr, k1, m_ref, tr_ref, ti_ref, HY_CT)
        cls[k1, :FFT_N2, :] = ur
        cls[k1, FFT_N2:, :] = ui
    blocks = _classes_to_blocks([(cls[k1, :FFT_N2, :], cls[k1, FFT_N2:, :]) for k1 in range(FFT_NK)])
    for a, ya in enumerate(blocks):
        y_ref[0, a * FFT_N2:(a + 1) * FFT_N2, :] = ya.astype(BF16)


def _conv_call(z, spec, m, tr, ti):
    return pl.pallas_call(
        _conv_kernel, grid=(HY_WIDTH // HY_CT, BATCH),
        in_specs=[pl.BlockSpec((1, SEQ, HY_CT), lambda c, b: (b, 0, c)),
                  pl.BlockSpec((FFT_NK, 2 * FFT_N2, HY_CT), lambda c, b: (0, 0, c))] + _fft_table_specs(2),
        out_specs=pl.BlockSpec((1, SEQ, HY_CT), lambda c, b: (b, 0, c)),
        out_shape=jax.ShapeDtypeStruct((BATCH, SEQ, HY_WIDTH), BF16),
        scratch_shapes=[pltpu.VMEM((FFT_NK, 2 * FFT_N2, HY_CT), F32)],
        compiler_params=_params(("parallel", "arbitrary")), name="hyena_long_conv",
    )(z, spec, m, tr, ti)


def _hy_out_kernel(y_ref, z_ref, x0_ref, b_ref, w_ref, x_ref, g_ref, *route):
    o_ref = route[5]
    z = z_ref[0].astype(F32)
    gated = x0_ref[0].astype(F32) * (y_ref[0].astype(F32) + b_ref[...] * z)
    mix = jnp.dot(gated.astype(BF16), w_ref[...], preferred_element_type=F32)
    x_new = x_ref[0] + g_ref[0] * mix
    o_ref[0] = x_new
    _route_tile(x_new, *route[:5], *route[6:])


def _hy_out_call(y, z, x0, bias, w, x, g1, route):
    r_args, r_in, r_out, r_shape, r_scratch = route
    t3 = lambda b, j: (b, j, 0)
    full2 = lambda b, j: (0, 0)
    return pl.pallas_call(
        _hy_out_kernel, grid=(BATCH, SEQ // HY_TL),
        in_specs=[pl.BlockSpec((1, HY_TL, HY_WIDTH), t3),
                  pl.BlockSpec((1, HY_TL, HY_WIDTH), t3),
                  pl.BlockSpec((1, HY_TL, HY_WIDTH), t3),
                  pl.BlockSpec((1, HY_WIDTH), full2),
                  pl.BlockSpec((HY_WIDTH, D_MODEL), full2),
                  pl.BlockSpec((1, HY_TL, D_MODEL), t3),
                  pl.BlockSpec((1, 1, D_MODEL), lambda b, j: (b, 0, 0))] + r_in,
        out_specs=[pl.BlockSpec((1, HY_TL, D_MODEL), t3)] + r_out,
        out_shape=[jax.ShapeDtypeStruct((BATCH, SEQ, D_MODEL), F32)] + r_shape,
        scratch_shapes=r_scratch,
        compiler_params=_params(("arbitrary", "arbitrary")), name="hyena_out_proj_route",
    )(y, z, x0, bias, w, x, g1, *r_args)


def kernel(x, c, ctx, c_ctx, ada_w, ada_b, ev_w_in, mla_q_norm, mla_w_uq, mla_kv_norm, mla_w_ukv, mla_q_qknorm, mla_k_qknorm, s5_lam_re, s5_lam_im, s5_log_step, s5_b_re, s5_b_im, s5_c_re, s5_c_im, s5_d, s5_glu_w, s5_glu_b, ev_w_out, hy_w_in, hy_conv_w, hy_conv_b, hy_f_w1, hy_f_b1, hy_f_w2, hy_f_b2, hy_f_w3, hy_f_freq, hy_bias, hy_w_out, moe_router_w, moe_router_b, moe_w_gate, moe_w_up, moe_w_down, moe_sh_gate, moe_sh_up, moe_sh_down):
    D = D_MODEL
    mod_all = _ada_call(c, c_ctx, ada_w, ada_b)

    def mods(li):
        return [m[:, None, :] for m in jnp.split(mod_all[li, :BATCH], 6, axis=-1)]

    sh1, sc1, g1, sh2, sc2, g2 = mods(0)
    mod_ctx = mod_all[0, BATCH, :2 * D]
    w0 = _a0_weights(ev_w_in[0], mla_q_norm[0], mla_w_uq[0], mla_kv_norm[0], mla_w_ukv[0],
                     mla_q_qknorm[0], mla_k_qknorm[0])
    q, k, v, u = _a0_call(x, ctx, sh1, sc1, mod_ctx[None, :D], mod_ctx[None, D:], w0)
    att = _attn_call(q, k, v)
    ug = u.reshape(BATCH, S5_NCHUNK, S5_CHUNK, S5_GROUPS, S5_GROUP)
    ug = ug.transpose(3, 1, 0, 2, 4).reshape(S5_GROUPS, S5_ROWS, S5_COLS)
    ys = _s5_call(ug, *_s5_weights(s5_lam_re[0], s5_lam_im[0], s5_log_step[0], s5_b_re[0], s5_b_im[0],
                                   s5_c_re[0], s5_c_im[0]))
    ys = ys.reshape(S5_GROUPS, S5_NCHUNK_LAT, BATCH, S5_CHUNK, S5_GROUP)
    ys = ys.transpose(2, 1, 3, 0, 4).reshape(BATCH, SEQ, S5_WIDTH)
    wo = ev_w_out[0].astype(BF16)
    wo_att = jnp.concatenate([wo[:MLA_WIDTH].reshape(MLA_HEADS, V_HEAD, D_MODEL),
                              jnp.zeros((MLA_HEADS, HEAD_PAD - V_HEAD, D_MODEL), BF16)], axis=1)
    x, *routed = _a1_call(att, ys, u, s5_d[0][None, :], s5_glu_w[0].astype(BF16), s5_glu_b[0][None, :],
                          wo_att.reshape(MLA_HEADS * HEAD_PAD, D_MODEL), wo[MLA_WIDTH:], x, g1,
                          _route_plumbing(sh2, sc2, moe_router_w[0], moe_router_b[0]))
    x = _moe(x, routed, sh2, sc2, g2, moe_w_gate, moe_w_up, moe_w_down,
             moe_sh_gate[0], moe_sh_up[0], moe_sh_down[0], 0)

    sh1, sc1, g1, sh2, sc2, g2 = mods(1)
    z, x0 = _hy_in_call(x, sh1, sc1, hy_w_in[0].astype(BF16), hy_conv_w[0], hy_conv_b[0][None, :])
    fft_tabs = _fft_tables()
    spectrum = _spec_call(hy_f_w1[0], hy_f_b1[0], hy_f_w2[0], hy_f_b2[0], hy_f_w3[0], hy_f_freq[0], *fft_tabs)
    y = _conv_call(z, spectrum, *fft_tabs)
    x, *routed = _hy_out_call(y, z, x0, hy_bias[0][None, :], hy_w_out[0].astype(BF16), x, g1,
                              _route_plumbing(sh2, sc2, moe_router_w[1], moe_router_b[1]))
    x = _moe(x, routed, sh2, sc2, g2, moe_w_gate, moe_w_up, moe_w_down,
             moe_sh_gate[1], moe_sh_up[1], moe_sh_down[1], 1)
    return x
```

```python
import functools
import math

import numpy as np
import jax
import jax.numpy as jnp
from jax import lax
from jax.experimental import pallas as pl
from jax.experimental.pallas import tpu as pltpu
from jax.experimental.pallas import tpu_sc as plsc

F32 = jnp.float32
BF16 = jnp.bfloat16

D_MODEL = 1024
BATCH = 8
SEQ = 4096
CTX_LEN = 256
KV_LEN = SEQ + CTX_LEN
GRID_W = 64
EPS = 1e-6

MLA_HEADS = 8
QK_NOPE = 64
QK_ROPE = 32
QK_HEAD = QK_NOPE + QK_ROPE
V_HEAD = 64
Q_LORA = 256
KV_LORA = 128
MLA_WIDTH = MLA_HEADS * V_HEAD
ROPE_BASE = 10000.0
HEAD_PAD = 128

S5_WIDTH = 512
S5_GROUP = 16
S5_GROUPS = S5_WIDTH // S5_GROUP
S5_STATE = 64
S5_CHUNK = 32
S5_NCHUNK = KV_LEN // S5_CHUNK
S5_NCHUNK_LAT = SEQ // S5_CHUNK
S5_NCHUNK_CTX = CTX_LEN // S5_CHUNK

HY_WIDTH = D_MODEL
FILT_EMB = 33
FILT_BANDS = (FILT_EMB - 1) // 2
FILT_PAD = 128
SHORT_CONV = 3
HY_MIN_DECAY = -math.log(1e-2) / 1.5
HY_MAX_DECAY = -math.log(1e-2) / 0.3
DFT_N = 2 * SEQ
FFT_N1 = 16
FFT_N2 = DFT_N // FFT_N1
FFT_NA = FFT_N1 // 2
FFT_NK = FFT_N1 // 2 + 1
HY_CT = 256

N_EXPERTS = 64
TOP_K = 6
EXPERT_FF = 256
ROUTE_SCALE = 2.5
MOE_BLOCK = 1024
ROW_WORDS = D_MODEL // 4
SC_WINDOW = 128

V7X_LANES = 128
V7X_VMEM_BYTES = 64 * 1024 * 1024
VMEM_LIMIT = V7X_VMEM_BYTES - 8 * 1024 * 1024


def _params(semantics):
    return pltpu.CompilerParams(dimension_semantics=semantics, vmem_limit_bytes=VMEM_LIMIT)


def _norm_mod(x, shift, scale):
    ms = jnp.mean(x * x, axis=-1, keepdims=True)
    return x * lax.rsqrt(ms + EPS) * (1.0 + scale) + shift


def _rms(x, gain, n):
    ms = jnp.sum(x * x, axis=-1, keepdims=True) * (1.0 / n)
    return x * lax.rsqrt(ms + EPS) * gain


ADA_ROWS = 16
ADA_TN = 1024


def _ada_kernel(c_ref, w_ref, b_ref, o_ref):
    sc = jax.nn.silu(c_ref[...])
    o_ref[0] = jnp.dot(sc, w_ref[0], preferred_element_type=F32, precision=lax.Precision.HIGHEST) + b_ref[0]


def _ada_call(c, c_ctx, ada_w, ada_b):
    depth, _, width = ada_w.shape
    rows = jnp.concatenate([c, c_ctx[None, :], jnp.zeros((ADA_ROWS - BATCH - 1, D_MODEL), F32)], axis=0)
    return pl.pallas_call(
        _ada_kernel, grid=(depth, width // ADA_TN),
        in_specs=[pl.BlockSpec((ADA_ROWS, D_MODEL), lambda l, n: (0, 0)),
                  pl.BlockSpec((1, D_MODEL, ADA_TN), lambda l, n: (l, 0, n)),
                  pl.BlockSpec((1, 1, ADA_TN), lambda l, n: (l, 0, n))],
        out_specs=pl.BlockSpec((1, ADA_ROWS, ADA_TN), lambda l, n: (l, 0, n)),
        out_shape=jax.ShapeDtypeStruct((depth, ADA_ROWS, width), F32),
        compiler_params=_params(("parallel", "arbitrary")), name="adaln_modulation",
    )(rows, ada_w, ada_b[:, None, :])


A0_TL = 256
A0_NT = SEQ // A0_TL


def _rope_perm():
    return np.concatenate([np.arange(0, QK_ROPE, 2), np.arange(1, QK_ROPE, 2)])


def _rope_tables():
    t = np.arange(SEQ)
    row = (t // GRID_W).astype(np.float64)
    col = (t % GRID_W).astype(np.float64)
    n_freq = QK_ROPE // 4
    inv = ROPE_BASE ** (-np.arange(n_freq, dtype=np.float64) / n_freq)
    ang = np.concatenate([row[:, None] * inv, col[:, None] * inv], axis=-1)
    cos, sin = np.cos(ang), np.sin(ang)
    half = QK_ROPE // 2
    a = np.zeros((KV_LEN, HEAD_PAD))
    b = np.zeros((KV_LEN, HEAD_PAD))
    a[:, :QK_HEAD] = 1.0
    a[:SEQ, QK_NOPE:QK_NOPE + half] = cos
    a[:SEQ, QK_NOPE + half:QK_HEAD] = cos
    b[:SEQ, QK_NOPE:QK_NOPE + half] = -sin
    b[:SEQ, QK_NOPE + half:QK_HEAD] = sin
    return a, b


def _norm_rope_heads(f, gain_ref, a, b, out_ref):
    width = MLA_HEADS * HEAD_PAD
    for hd in range(MLA_HEADS):
        sl = slice(hd * HEAD_PAD, (hd + 1) * HEAD_PAD)
        x = f[:, sl]
        r = lax.rsqrt(jnp.sum(x * x, axis=-1, keepdims=True) * (1.0 / QK_HEAD) + EPS)
        rot = x * (a * gain_ref[:, sl]) + f[:, width + hd * HEAD_PAD:width + (hd + 1) * HEAD_PAD] * b
        out_ref[0, :, sl] = (rot * r).astype(BF16)


def _a0_kernel(x_ref, ctx_ref, sh_ref, sc_ref, shc_ref, scc_ref, win_ref, qn_ref, wuq_ref, kvn_ref,
               wk_ref, wuv_ref, qg_ref, kg_ref, ka_ref, kb_ref, qa_ref, qb_ref,
               q_ref, k_ref, v_ref, u_ref):
    j = pl.program_id(1)
    is_ctx = j == A0_NT
    xin = jnp.where(is_ctx, ctx_ref[0], x_ref[0])
    shift = jnp.where(is_ctx, shc_ref[...], sh_ref[0])
    scale = jnp.where(is_ctx, scc_ref[...], sc_ref[0])
    h = _norm_mod(xin, shift, scale).astype(BF16)
    proj = jnp.dot(h, win_ref[...], preferred_element_type=F32)
    u_ref[0] = proj[:, 512:].astype(BF16)

    c_kv = _rms(proj[:, Q_LORA:Q_LORA + KV_LORA], kvn_ref[...], KV_LORA).astype(BF16)
    lane = lax.broadcasted_iota(jnp.int32, (1, MLA_HEADS * HEAD_PAD), 1)
    ones_lane = (lane % HEAD_PAD == V_HEAD).astype(F32)
    v_ref[0] = (jnp.dot(c_kv, wuv_ref[...], preferred_element_type=F32) + ones_lane).astype(BF16)
    kin = jnp.concatenate([c_kv, proj[:, 384:512].astype(BF16)], axis=1)
    kf = jnp.dot(kin, wk_ref[...], preferred_element_type=F32)
    _norm_rope_heads(kf, kg_ref, ka_ref[...], kb_ref[...], k_ref)

    @pl.when(j < A0_NT)
    def _():
        ql = _rms(proj[:, :Q_LORA], qn_ref[...], Q_LORA).astype(BF16)
        qf = jnp.dot(ql, wuq_ref[...], preferred_element_type=F32)
        _norm_rope_heads(qf, qg_ref, qa_ref[...], qb_ref[...], q_ref)


def _a0_call(x, ctx, sh, sc, shc, scc, w):
    nt = A0_NT
    lat = lambda b, j: (b, jnp.minimum(j, nt - 1), 0)
    full2 = lambda b, j: (0, 0)
    per_b = lambda b, j: (b, 0, 0)
    tab = pl.BlockSpec((A0_TL, HEAD_PAD), lambda b, j: (j, 0))
    in_specs = [
        pl.BlockSpec((1, A0_TL, D_MODEL), lat),
        pl.BlockSpec((1, CTX_LEN, D_MODEL), per_b),
        pl.BlockSpec((1, 1, D_MODEL), per_b),
        pl.BlockSpec((1, 1, D_MODEL), per_b),
        pl.BlockSpec((1, D_MODEL), full2),
        pl.BlockSpec((1, D_MODEL), full2),
        pl.BlockSpec((D_MODEL, 1024), full2),
        pl.BlockSpec((1, Q_LORA), full2),
        pl.BlockSpec((Q_LORA, 2 * MLA_HEADS * HEAD_PAD), full2),
        pl.BlockSpec((1, KV_LORA), full2),
        pl.BlockSpec((2 * KV_LORA, 2 * MLA_HEADS * HEAD_PAD), full2),
        pl.BlockSpec((KV_LORA, MLA_HEADS * HEAD_PAD), full2),
        pl.BlockSpec((1, MLA_HEADS * HEAD_PAD), full2),
        pl.BlockSpec((1, MLA_HEADS * HEAD_PAD), full2),
        tab, tab, tab, tab,
    ]
    out_specs = [
        pl.BlockSpec((1, A0_TL, MLA_HEADS * HEAD_PAD), lat),
        pl.BlockSpec((1, A0_TL, MLA_HEADS * HEAD_PAD), lambda b, j: (b, j, 0)),
        pl.BlockSpec((1, A0_TL, MLA_HEADS * HEAD_PAD), lambda b, j: (b, j, 0)),
        pl.BlockSpec((1, A0_TL, S5_WIDTH), lambda b, j: (b, j, 0)),
    ]
    out_shape = [
        jax.ShapeDtypeStruct((BATCH, SEQ, MLA_HEADS * HEAD_PAD), BF16),
        jax.ShapeDtypeStruct((BATCH, KV_LEN, MLA_HEADS * HEAD_PAD), BF16),
        jax.ShapeDtypeStruct((BATCH, KV_LEN, MLA_HEADS * HEAD_PAD), BF16),
        jax.ShapeDtypeStruct((BATCH, KV_LEN, S5_WIDTH), BF16),
    ]
    return pl.pallas_call(
        _a0_kernel, grid=(BATCH, nt + 1), in_specs=in_specs, out_specs=out_specs, out_shape=out_shape,
        compiler_params=_params(("parallel", "arbitrary")), name="even_in_proj",
    )(x, ctx, sh, sc, shc, scc, *w)


def _gain_swap(w, gain):
    half = QK_ROPE // 2
    wg = (w * gain).reshape(w.shape[0], MLA_HEADS, HEAD_PAD)
    re, im = wg[..., QK_NOPE:QK_NOPE + half], wg[..., QK_NOPE + half:QK_HEAD]
    out = jnp.concatenate([jnp.zeros_like(wg[..., :QK_NOPE]), im, re, jnp.zeros_like(wg[..., QK_HEAD:])], axis=-1)
    return out.reshape(w.shape)


def _a0_weights(w_in, q_norm, w_uq, kv_norm, w_ukv, q_qk, k_qk):
    perm = _rope_perm()
    kr0 = Q_LORA + KV_LORA
    w_cat = jnp.concatenate([
        w_in[:, :kr0], w_in[:, kr0:kr0 + QK_ROPE][:, perm],
        jnp.zeros((D_MODEL, HEAD_PAD - QK_ROPE), F32), w_in[:, kr0 + QK_ROPE:]], axis=1).astype(BF16)
    pad = HEAD_PAD - QK_HEAD

    def head_gain(g):
        gh = jnp.concatenate([g[:QK_NOPE], g[QK_NOPE:][perm], jnp.zeros((pad,), F32)])
        return jnp.tile(gh, MLA_HEADS)[None, :]

    uq = w_uq.reshape(Q_LORA, MLA_HEADS, QK_HEAD)
    uq = jnp.concatenate([uq[..., :QK_NOPE], uq[..., QK_NOPE:][..., perm],
                          jnp.zeros((Q_LORA, MLA_HEADS, pad), F32)], axis=-1)
    uq = uq.reshape(Q_LORA, MLA_HEADS * HEAD_PAD)
    uq = jnp.concatenate([uq, _gain_swap(uq, head_gain(q_qk))], axis=1).astype(BF16)
    ukv = w_ukv.reshape(KV_LORA, MLA_HEADS, QK_NOPE + V_HEAD)
    uk = jnp.concatenate([ukv[..., :QK_NOPE], jnp.zeros((KV_LORA, MLA_HEADS, HEAD_PAD - QK_NOPE), F32)], axis=-1)
    uk = uk.reshape(KV_LORA, MLA_HEADS * HEAD_PAD)
    place = np.zeros((KV_LORA, MLA_HEADS, HEAD_PAD), np.float32)
    for i in range(QK_ROPE):
        place[i, :, QK_NOPE + i] = 1.0
    wk = jnp.concatenate([uk, jnp.asarray(place.reshape(KV_LORA, MLA_HEADS * HEAD_PAD))], axis=0)
    wk = jnp.concatenate([wk, _gain_swap(wk, head_gain(k_qk))], axis=1).astype(BF16)
    wuv = jnp.concatenate([ukv[..., QK_NOPE:], jnp.zeros((KV_LORA, MLA_HEADS, HEAD_PAD - V_HEAD), F32)], axis=-1)
    wuv = wuv.reshape(KV_LORA, MLA_HEADS * HEAD_PAD).astype(BF16)
    a, b = _rope_tables()
    qs = QK_HEAD ** -0.5 * math.log2(math.e)
    tabs = [jnp.asarray(t, F32) for t in (a, b, a * qs, b * qs)]
    return [w_cat, q_norm[None, :], uq, kv_norm[None, :], wk, wuv, head_gain(q_qk), head_gain(k_qk)] + tabs


ATT_TQ = 256
HEADS_PER_STEP = 4


def _attn_kernel(q_ref, k_ref, v_ref, o_ref):
    for hh in range(HEADS_PER_STEP):
        sl = slice(hh * HEAD_PAD, (hh + 1) * HEAD_PAD)
        s = lax.dot_general(q_ref[0, :, sl], k_ref[0, :, sl], (((1,), (1,)), ((), ())),
                            preferred_element_type=F32)
        m = jnp.max(s, axis=-1, keepdims=True)
        p = jnp.exp2(s - m).astype(BF16)
        acc = jnp.dot(p, v_ref[0, :, sl], preferred_element_type=F32)
        o_ref[0, :, sl] = (acc * (1.0 / acc[:, V_HEAD:V_HEAD + 1])).astype(BF16)


def _attn_call(q, k, v):
    wq = HEADS_PER_STEP * HEAD_PAD
    return pl.pallas_call(
        _attn_kernel, grid=(BATCH, MLA_HEADS // HEADS_PER_STEP, SEQ // ATT_TQ),
        in_specs=[pl.BlockSpec((1, ATT_TQ, wq), lambda b, h, i: (b, i, h)),
                  pl.BlockSpec((1, KV_LEN, wq), lambda b, h, i: (b, 0, h)),
                  pl.BlockSpec((1, KV_LEN, wq), lambda b, h, i: (b, 0, h))],
        out_specs=pl.BlockSpec((1, ATT_TQ, wq), lambda b, h, i: (b, i, h)),
        out_shape=jax.ShapeDtypeStruct((BATCH, SEQ, MLA_HEADS * HEAD_PAD), BF16),
        compiler_params=_params(("parallel", "parallel", "arbitrary")), name="mla_attention",
    )(q, k, v)


S5_ROWS = S5_NCHUNK * BATCH
S5_ROWS_LAT = S5_NCHUNK_LAT * BATCH
S5_COLS = S5_CHUNK * S5_GROUP
S5_SW = 2 * S5_STATE


def _s5_kernel(u_ref, r_ref, mb_ref, mc_ref, coef_ref, y_ref, x_sc, sp_sc, t_sc):
    u = u_ref[0]
    lags = r_ref[0]
    for sg in range(S5_CHUNK):
        off = (S5_CHUNK - 1 - sg) * S5_GROUP
        t_sc[sg * S5_GROUP:(sg + 1) * S5_GROUP, :] = lags[:, off:off + S5_COLS].astype(BF16)
    x_sc[...] = jnp.dot(u, mb_ref[0], preferred_element_type=F32)
    cf = coef_ref[0]
    af, bfm, bfp, ab, bbm, bbp = [cf[i * 8:(i + 1) * 8] for i in range(6)]

    def body(i, carry):
        sf, sfw, sb, sbw = carry
        cfw = jnp.where(i < S5_NCHUNK_CTX, i + S5_NCHUNK_LAT, i - S5_NCHUNK_CTX)
        rf = pl.multiple_of(cfw * BATCH, BATCH)
        rb = pl.multiple_of((S5_NCHUNK - 1 - i) * BATCH, BATCH)
        sp_sc[pl.ds(rf, BATCH), 0:S5_SW] = sf
        sp_sc[pl.ds(rb, BATCH), S5_SW:2 * S5_SW] = sb
        xf = x_sc[pl.ds(rf, BATCH), 0:S5_SW]
        xfw = x_sc[pl.ds(rf, BATCH), S5_SW:2 * S5_SW]
        xb = x_sc[pl.ds(rb, BATCH), 2 * S5_SW:3 * S5_SW]
        xbw = x_sc[pl.ds(rb, BATCH), 3 * S5_SW:4 * S5_SW]
        return (sf * af + sfw * bfm + xf, sfw * af + sf * bfp + xfw,
                sb * ab + sbw * bbm + xb, sbw * ab + sb * bbp + xbw)

    z = jnp.zeros((BATCH, S5_SW), F32)
    lax.fori_loop(0, S5_NCHUNK, body, (z, z, z, z))
    y = jnp.dot(u[:S5_ROWS_LAT], t_sc[...], preferred_element_type=F32)
    y = y + jnp.dot(sp_sc[0:S5_ROWS_LAT, :].astype(BF16), mc_ref[0], preferred_element_type=F32)
    y_ref[0] = y.astype(BF16)


def _s5_call(ug, t, mb, mc, coef):
    g3 = lambda g: (g, 0, 0)
    return pl.pallas_call(
        _s5_kernel, grid=(S5_GROUPS,),
        in_specs=[pl.BlockSpec((1, S5_ROWS, S5_COLS), g3),
                  pl.BlockSpec((1, S5_GROUP, 2 * S5_COLS), g3),
                  pl.BlockSpec((1, S5_COLS, 4 * S5_SW), g3),
                  pl.BlockSpec((1, 2 * S5_SW, S5_COLS), g3),
                  pl.BlockSpec((1, 6 * 8, S5_SW), g3)],
        out_specs=pl.BlockSpec((1, S5_ROWS_LAT, S5_COLS), g3),
        out_shape=jax.ShapeDtypeStruct((S5_GROUPS, S5_ROWS_LAT, S5_COLS), BF16),
        scratch_shapes=[pltpu.VMEM((S5_ROWS, 4 * S5_SW), F32), pltpu.VMEM((S5_ROWS, 2 * S5_SW), F32),
                        pltpu.VMEM((S5_COLS, S5_COLS), BF16)],
        compiler_params=_params(("parallel",)), name="s5_chunked_scan",
    )(ug, t, mb, mc, coef)


def _s5_weights(lam_re, lam_im, log_step, b_re, b_im, c_re, c_im):
    q = S5_CHUNK
    hi = lax.Precision.HIGHEST
    t_blocks, mbs, mcs, coefs = [], [], [], []
    sig = jnp.arange(q)
    for d in range(2):
        lr = jnp.minimum(lam_re[d], -1e-4)
        li = lam_im[d]
        step = jnp.exp(log_step[d])[:, None]
        jj = jnp.arange(q + 1, dtype=F32)[:, None, None]
        mag = jnp.exp(lr * step * jj)
        ph = li * step * jj
        pr, pi = mag * jnp.cos(ph), mag * jnp.sin(ph)
        nr, ni = pr[1] - 1.0, pi[1]
        den = lr * lr + li * li
        fr, fi = (nr * lr + ni * li) / den, (ni * lr - nr * li) / den
        br = fr[..., None] * b_re[d] - fi[..., None] * b_im[d]
        bi = fr[..., None] * b_im[d] + fi[..., None] * b_re[d]
        cr, ci = c_re[d], c_im[d]
        cpr = cr[None] * pr[:, :, None, :] - ci[None] * pi[:, :, None, :]
        cpi = cr[None] * pi[:, :, None, :] + ci[None] * pr[:, :, None, :]
        kern = jnp.einsum('jghp,gpk->jghk', jnp.concatenate([cpr[:q], -cpi[:q]], axis=-1),
                          jnp.concatenate([br, bi], axis=1), precision=hi)
        kt = kern.transpose(1, 3, 0, 2)
        zero_slots = jnp.zeros((S5_GROUPS, S5_GROUP, q, S5_GROUP), F32)
        if d == 0:
            t_blocks.append(jnp.concatenate([zero_slots[:, :, :q - 1], kt, zero_slots[:, :, :1]], axis=2))
        else:
            t_blocks.append(jnp.concatenate([kt[:, :, ::-1], zero_slots], axis=2))
        pw = (q - 1 - sig) if d == 0 else sig
        xr = pr[pw][..., None] * br[None] - pi[pw][..., None] * bi[None]
        xi = pr[pw][..., None] * bi[None] + pi[pw][..., None] * br[None]
        xr = xr.transpose(1, 0, 3, 2).reshape(S5_GROUPS, S5_COLS, S5_STATE)
        xi = xi.transpose(1, 0, 3, 2).reshape(S5_GROUPS, S5_COLS, S5_STATE)
        mbs += [xr, xi, xi, xr]
        po = (sig + 1) if d == 0 else (q - sig)
        mr = cpr[po].transpose(1, 3, 0, 2).reshape(S5_GROUPS, S5_STATE, S5_COLS)
        mi = -cpi[po].transpose(1, 3, 0, 2).reshape(S5_GROUPS, S5_STATE, S5_COLS)
        mcs += [mr, mi]
        are, aim = pr[q], pi[q]
        rows = [jnp.concatenate([are, are], -1), jnp.concatenate([-aim, aim], -1), jnp.concatenate([aim, -aim], -1)]
        coefs += [jnp.broadcast_to(r[:, None, :], (S5_GROUPS, 8, S5_SW)) for r in rows]
    t = (t_blocks[0] + t_blocks[1]).reshape(S5_GROUPS, S5_GROUP, 2 * S5_COLS)
    mb = jnp.concatenate(mbs, axis=-1).astype(BF16)
    mc = jnp.concatenate(mcs, axis=1).astype(BF16)
    coef = jnp.concatenate(coefs, axis=1)
    return t, mb, mc, coef


A1_TL = 512


def _a1_kernel(att_ref, ys_ref, u_ref, d_ref, gw_ref, gb_ref, woa_ref, wos_ref, x_ref, g_ref, *route):
    o_ref = route[5]
    y = u_ref[0].astype(F32) * d_ref[...] + ys_ref[0].astype(F32)
    z = jax.nn.gelu(y)
    gate = jax.nn.sigmoid(jnp.dot(z.astype(BF16), gw_ref[...], preferred_element_type=F32) + gb_ref[...])
    s5 = (z * gate).astype(BF16)
    mix = jnp.dot(att_ref[0], woa_ref[...], preferred_element_type=F32)
    mix = mix + jnp.dot(s5, wos_ref[...], preferred_element_type=F32)
    x_new = x_ref[0] + g_ref[0] * mix
    o_ref[0] = x_new
    _route_tile(x_new, *route[:5], *route[6:])


def _a1_call(att, ys, u, d, gw, gb, woa, wos, x, g1, route):
    r_args, r_in, r_out, r_shape, r_scratch = route
    t3 = lambda b, j: (b, j, 0)
    full2 = lambda b, j: (0, 0)
    return pl.pallas_call(
        _a1_kernel, grid=(BATCH, SEQ // A1_TL),
        in_specs=[pl.BlockSpec((1, A1_TL, MLA_HEADS * HEAD_PAD), t3),
                  pl.BlockSpec((1, A1_TL, S5_WIDTH), t3),
                  pl.BlockSpec((1, A1_TL, S5_WIDTH), t3),
                  pl.BlockSpec((1, S5_WIDTH), full2),
                  pl.BlockSpec((S5_WIDTH, S5_WIDTH), full2),
                  pl.BlockSpec((1, S5_WIDTH), full2),
                  pl.BlockSpec((MLA_HEADS * HEAD_PAD, D_MODEL), full2),
                  pl.BlockSpec((S5_WIDTH, D_MODEL), full2),
                  pl.BlockSpec((1, A1_TL, D_MODEL), t3),
                  pl.BlockSpec((1, 1, D_MODEL), lambda b, j: (b, 0, 0))] + r_in,
        out_specs=[pl.BlockSpec((1, A1_TL, D_MODEL), t3)] + r_out,
        out_shape=[jax.ShapeDtypeStruct((BATCH, SEQ, D_MODEL), F32)] + r_shape,
        scratch_shapes=r_scratch,
        compiler_params=_params(("arbitrary", "arbitrary")), name="even_out_proj_route",
    )(att, ys, u, d, gw, gb, woa, wos, x, g1, *r_args)


MOE_TL = 512
SLOT_PAD = 8


def _pack_rows(v):
    halves = []
    for p in range(2):
        base = 2 * p * ROW_WORDS
        a = pltpu.bitcast(v[:, base:base + ROW_WORDS].astype(BF16).astype(F32), jnp.uint32)
        b = pltpu.bitcast(v[:, base + ROW_WORDS:base + 2 * ROW_WORDS].astype(BF16).astype(F32), jnp.uint32)
        halves.append((a >> 16) | b)
    return halves


def _unpack_rows(lo, hi):
    out = []
    for w in (lo, hi):
        out.append(pltpu.bitcast(w << 16, F32))
        out.append(pltpu.bitcast(w & jnp.uint32(0xFFFF0000), F32))
    return out


def _route_tile(x, sh_ref, sc_ref, rwt_ref, rb_ref, tri_ref,
                hlo_ref, hhi_ref, idx_ref, wt_ref, rank_ref, cnt_ref, run_sc):
    @pl.when((pl.program_id(0) == 0) & (pl.program_id(1) == 0))
    def _():
        run_sc[...] = jnp.zeros_like(run_sc)

    h = _norm_mod(x, sh_ref[0], sc_ref[0])
    hb = h.astype(BF16)
    hlo_ref[0], hhi_ref[0] = _pack_rows(h)
    h_lo = (h - hb.astype(F32)).astype(BF16)
    rwt = rwt_ref[...]
    rw_hi = rwt.astype(BF16)
    rw_lo = (rwt - rw_hi.astype(F32)).astype(BF16)
    logits = lax.dot_general(jnp.concatenate([rw_hi, rw_lo, rw_hi], axis=1), jnp.concatenate([hb, hb, h_lo], axis=1),
                             (((1,), (1,)), ((), ())), preferred_element_type=F32)
    scores = jax.nn.sigmoid(logits)

    work = scores + rb_ref[...]
    expert = lax.broadcasted_iota(jnp.int32, work.shape, 0).astype(F32)
    hits, ids = [], []
    for _ in range(TOP_K):
        m = jnp.max(work, axis=0, keepdims=True)
        ik = jnp.min(jnp.where(work == m, expert, float(N_EXPERTS)), axis=0, keepdims=True)
        hit = expert == ik
        hits.append(hit)
        ids.append(ik)
        work = jnp.where(hit, -jnp.inf, work)
    mask = hits[0]
    for hit in hits[1:]:
        mask = jnp.logical_or(mask, hit)
    maskf = mask.astype(F32)
    before = jnp.dot(maskf.astype(BF16), tri_ref[...], preferred_element_type=F32) + run_sc[:, 0:1]
    sel = [jnp.sum(jnp.where(hit, scores, 0.0), axis=0, keepdims=True) for hit in hits]
    denom = sel[0]
    for s in sel[1:]:
        denom = denom + s
    ranks = [jnp.sum(jnp.where(hit, before, 0.0), axis=0, keepdims=True) for hit in hits]
    pad = [jnp.zeros_like(denom)] * (SLOT_PAD - TOP_K)
    idx_ref[...] = jnp.concatenate(ids + pad, axis=0).astype(jnp.int32)
    wt_ref[...] = jnp.concatenate([s / denom * ROUTE_SCALE for s in sel] + pad, axis=0)
    rank_ref[...] = jnp.concatenate(ranks + pad, axis=0).astype(jnp.int32)
    run_sc[...] += jnp.sum(maskf, axis=1, keepdims=True)
    cnt_ref[...] = run_sc[...]


def _route_plumbing(sh, sc, rw, rb):
    t3 = lambda b, j: (b, j, 0)
    full2 = lambda b, j: (0, 0)
    per_b = lambda b, j: (b, 0, 0)
    nt = SEQ // MOE_TL
    slots = lambda b, j: (0, b * nt + j)
    tri = jnp.asarray(np.triu(np.ones((MOE_TL, MOE_TL), np.float32), 1), BF16)
    args = [sh, sc, rw.T, rb[:, None], tri]
    in_specs = [pl.BlockSpec((1, 1, D_MODEL), per_b),
                pl.BlockSpec((1, 1, D_MODEL), per_b),
                pl.BlockSpec((N_EXPERTS, D_MODEL), full2),
                pl.BlockSpec((N_EXPERTS, 1), full2),
                pl.BlockSpec((MOE_TL, MOE_TL), full2)]
    out_specs = [pl.BlockSpec((1, MOE_TL, ROW_WORDS), t3),
                 pl.BlockSpec((1, MOE_TL, ROW_WORDS), t3),
                 pl.BlockSpec((SLOT_PAD, MOE_TL), slots),
                 pl.BlockSpec((SLOT_PAD, MOE_TL), slots),
                 pl.BlockSpec((SLOT_PAD, MOE_TL), slots),
                 pl.BlockSpec((N_EXPERTS, V7X_LANES), full2)]
    slot_i = jax.ShapeDtypeStruct((SLOT_PAD, BATCH * SEQ), jnp.int32)
    out_shape = [jax.ShapeDtypeStruct((BATCH, SEQ, ROW_WORDS), jnp.uint32),
                 jax.ShapeDtypeStruct((BATCH, SEQ, ROW_WORDS), jnp.uint32),
                 slot_i,
                 jax.ShapeDtypeStruct((SLOT_PAD, BATCH * SEQ), F32),
                 slot_i,
                 jax.ShapeDtypeStruct((N_EXPERTS, V7X_LANES), F32)]
    scratch = [pltpu.VMEM((N_EXPERTS, V7X_LANES), F32)]
    return args, in_specs, out_specs, out_shape, scratch


def _sc_mesh():
    return plsc.VectorSubcoreMesh(core_axis_name="c", subcore_axis_name="s")


def _sc_dispatch(h_words, dest, n_rows):
    n_tok = h_words.shape[0]

    @pl.kernel(out_type=jax.ShapeDtypeStruct((n_rows, ROW_WORDS), jnp.uint32), mesh=_sc_mesh(),
               scratch_types=[pltpu.SemaphoreType.DMA((TOP_K,))])
    def scatter_rows(h_hbm, i_hbm, o_hbm, sems):
        def body(h_vmem, i_vmem):
            copies = [pltpu.async_copy(h_vmem, o_hbm.at[i_vmem.at[k]], sems.at[k]) for k in range(TOP_K)]
            for copy in copies:
                copy.wait()

        pltpu.emit_pipeline(
            body, grid=(n_tok // SC_WINDOW,),
            in_specs=[pl.BlockSpec((SC_WINDOW, ROW_WORDS), index_map=lambda i: (i, 0)),
                      pl.BlockSpec((SLOT_PAD, SC_WINDOW), index_map=lambda i: (0, i))],
            out_specs=[],
            core_axis_name=("c", "s"), dimension_semantics=(pltpu.PARALLEL,),
        )(h_hbm, i_hbm)

    return scatter_rows(h_words, dest)


def _sc_collect(y_words, dest):
    n_tok = dest.shape[1]

    @pl.kernel(out_type=jax.ShapeDtypeStruct((TOP_K, n_tok, ROW_WORDS), jnp.uint32), mesh=_sc_mesh(),
               scratch_types=[])
    def gather_rows(y_hbm, i_hbm, o_hbm):
        def body(i_vmem, o_vmem):
            pltpu.sync_copy(y_hbm.at[i_vmem.at[0]], o_vmem.at[0])

        pltpu.emit_pipeline(
            body, grid=(TOP_K, n_tok // SC_WINDOW),
            in_specs=[pl.BlockSpec((1, SC_WINDOW), index_map=lambda k, i: (k, i))],
            out_specs=[pl.BlockSpec((1, SC_WINDOW, ROW_WORDS), index_map=lambda k, i: (k, i, 0))],
            core_axis_name=("c", "s"), dimension_semantics=(pltpu.PARALLEL, pltpu.PARALLEL),
        )(i_hbm, o_hbm)

    return gather_rows(y_words, dest)


def _expert_kernel(be_ref, nv_ref, xlo_ref, xhi_ref, wg_ref, wu_ref, wd_ref, ylo_ref, yhi_ref,
                   wg_sc, wu_sc, wd_sc):
    i = pl.program_id(0)
    nv = nv_ref[i]

    @pl.when(jnp.logical_or(i == 0, be_ref[i] != be_ref[jnp.maximum(i - 1, 0)]))
    def _():
        wg_sc[...] = wg_ref[0, 0].astype(BF16)
        wu_sc[...] = wu_ref[0, 0].astype(BF16)
        wd_sc[...] = wd_ref[0, 0].astype(BF16)

    @pl.when(nv > 0)
    def _():
        parts = _unpack_rows(xlo_ref[...], xhi_ref[...])
        xb = jnp.concatenate([p.astype(BF16) for p in parts], axis=1)
        live = lax.broadcasted_iota(jnp.int32, xb.shape, 0) < nv
        xb = jnp.where(live, xb, jnp.zeros_like(xb))
        hid = jax.nn.silu(jnp.dot(xb, wg_sc[...], preferred_element_type=F32))
        hid = hid * jnp.dot(xb, wu_sc[...], preferred_element_type=F32)
        y = jnp.dot(hid.astype(BF16), wd_sc[...], preferred_element_type=F32)
        ylo_ref[...], yhi_ref[...] = _pack_rows(y)

    @pl.when(nv == 0)
    def _():
        ylo_ref[...] = jnp.zeros_like(ylo_ref)
        yhi_ref[...] = jnp.zeros_like(yhi_ref)


def _expert_call(block_e, n_valid, xlo, xhi, wg, wu, wd, li):
    n_rows = xlo.shape[0]
    n_blocks = n_rows // MOE_BLOCK
    rows = pl.BlockSpec((MOE_BLOCK, ROW_WORDS), lambda i, be, nv: (i, 0))
    grid_spec = pltpu.PrefetchScalarGridSpec(
        num_scalar_prefetch=2, grid=(n_blocks,),
        in_specs=[rows, rows,
                  pl.BlockSpec((1, 1, D_MODEL, EXPERT_FF), lambda i, be, nv: (li, be[i], 0, 0)),
                  pl.BlockSpec((1, 1, D_MODEL, EXPERT_FF), lambda i, be, nv: (li, be[i], 0, 0)),
                  pl.BlockSpec((1, 1, EXPERT_FF, D_MODEL), lambda i, be, nv: (li, be[i], 0, 0))],
        out_specs=[rows, rows],
        scratch_shapes=[pltpu.VMEM((D_MODEL, EXPERT_FF), BF16), pltpu.VMEM((D_MODEL, EXPERT_FF), BF16),
                        pltpu.VMEM((EXPERT_FF, D_MODEL), BF16)])
    out = jax.ShapeDtypeStruct((n_rows, ROW_WORDS), jnp.uint32)
    return pl.pallas_call(
        _expert_kernel, grid_spec=grid_spec, out_shape=[out, out],
        compiler_params=_params(("arbitrary",)), name="moe_experts",
    )(block_e, n_valid, xlo, xhi, wg, wu, wd)


def _combine_kernel(ylo_ref, yhi_ref, w_ref, x_ref, sh_ref, sc_ref, g_ref, sg_ref, su_ref, sd_ref, o_ref):
    hb = _norm_mod(x_ref[0], sh_ref[0], sc_ref[0]).astype(BF16)
    hid = jax.nn.silu(jnp.dot(hb, sg_ref[...].astype(BF16), preferred_element_type=F32))
    hid = hid * jnp.dot(hb, su_ref[...].astype(BF16), preferred_element_type=F32)
    shared = jnp.dot(hid.astype(BF16), sd_ref[...].astype(BF16), preferred_element_type=F32)
    w = w_ref[0]
    acc = [None] * 4
    for k in range(TOP_K):
        wk = w[:, k:k + 1]
        for c, part in enumerate(_unpack_rows(ylo_ref[k], yhi_ref[k])):
            acc[c] = wk * part if acc[c] is None else acc[c] + wk * part
    for c in range(4):
        sl = slice(c * ROW_WORDS, (c + 1) * ROW_WORDS)
        o_ref[0, :, sl] = x_ref[0, :, sl] + g_ref[0, :, sl] * (acc[c] + shared[:, sl])


def _combine_call(ylo, yhi, wts, x, sh, sc, g2, sg, su, sd):
    t3 = lambda b, j: (b, j, 0)
    per_b = lambda b, j: (b, 0, 0)
    full2 = lambda b, j: (0, 0)
    ff = sg.shape[1]
    nt = SEQ // MOE_TL
    rows = pl.BlockSpec((TOP_K, MOE_TL, ROW_WORDS), lambda b, j: (0, b * nt + j, 0))
    return pl.pallas_call(
        _combine_kernel, grid=(BATCH, nt),
        in_specs=[rows, rows,
                  pl.BlockSpec((1, MOE_TL, SLOT_PAD), t3),
                  pl.BlockSpec((1, MOE_TL, D_MODEL), t3),
                  pl.BlockSpec((1, 1, D_MODEL), per_b),
                  pl.BlockSpec((1, 1, D_MODEL), per_b),
                  pl.BlockSpec((1, 1, D_MODEL), per_b),
                  pl.BlockSpec((D_MODEL, ff), full2),
                  pl.BlockSpec((D_MODEL, ff), full2),
                  pl.BlockSpec((ff, D_MODEL), full2)],
        out_specs=pl.BlockSpec((1, MOE_TL, D_MODEL), t3),
        out_shape=jax.ShapeDtypeStruct((BATCH, SEQ, D_MODEL), F32),
        compiler_params=_params(("parallel", "arbitrary")), name="moe_combine_shared",
    )(ylo, yhi, wts, x, sh, sc, g2, sg, su, sd)


def _moe(x, routed, sh, sc, g2, w_gate, w_up, w_down, sh_gate, sh_up, sh_down, li):
    T = BATCH * SEQ
    TK = T * TOP_K
    hlo, hhi, idx, wts, rank, counts = routed
    wts = wts.T.reshape(BATCH, SEQ, SLOT_PAD)
    counts = counts[:, 0].astype(jnp.int32)
    padded = (counts + MOE_BLOCK - 1) // MOE_BLOCK * MOE_BLOCK
    pad_end = jnp.cumsum(padded)
    pad_start = pad_end - padded
    n_blocks = -(-TK // MOE_BLOCK) + N_EXPERTS
    n_rows = n_blocks * MOE_BLOCK
    block_start = jnp.arange(n_blocks, dtype=jnp.int32) * MOE_BLOCK
    owns = jnp.logical_and(block_start[:, None] >= pad_start[None, :], block_start[:, None] < pad_end[None, :])
    owns = owns.astype(jnp.int32)
    experts = jnp.arange(N_EXPERTS, dtype=jnp.int32)[None, :]
    block_e = jnp.sum(owns * experts, axis=1) + (N_EXPERTS - 1) * (1 - jnp.sum(owns, axis=1))
    n_valid = jnp.sum(owns * (counts[None, :] - (block_start[:, None] - pad_start[None, :])), axis=1)
    n_valid = jnp.clip(n_valid, 0, MOE_BLOCK).astype(jnp.int32)
    first_row = jnp.sum(jnp.where(idx[None] == experts.T[:, :, None], pad_start[:, None, None], 0), axis=0)
    dest = first_row + rank
    xlo = _sc_dispatch(hlo.reshape(T, ROW_WORDS), dest, n_rows)
    xhi = _sc_dispatch(hhi.reshape(T, ROW_WORDS), dest, n_rows)
    ylo, yhi = _expert_call(block_e, n_valid, xlo, xhi, w_gate, w_up, w_down, li)
    return _combine_call(_sc_collect(ylo, dest), _sc_collect(yhi, dest), wts, x, sh, sc, g2,
                         sh_gate, sh_up, sh_down)


HY_TL = 512
HALO = 8


def _hy_in_kernel(x_ref, xp_ref, xn_ref, sh_ref, sc_ref, w_ref, cw_ref, cb_ref, z_ref, x0_ref, h_sc):
    j = pl.program_id(1)
    shift, scale = sh_ref[0], sc_ref[0]
    keep_prev = (j > 0).astype(F32)
    keep_next = (j < SEQ // HY_TL - 1).astype(F32)
    h_sc[0:HALO, :] = _norm_mod(xp_ref[0], shift, scale) * keep_prev
    h_sc[HALO:HALO + HY_TL, :] = _norm_mod(x_ref[0], shift, scale)
    h_sc[HALO + HY_TL:, :] = _norm_mod(xn_ref[0], shift, scale) * keep_next
    hcat = h_sc[...].astype(BF16)
    outs = []
    for part in range(3):
        sl = slice(part * HY_WIDTH, (part + 1) * HY_WIDTH)
        p = jnp.dot(hcat, w_ref[:, sl], preferred_element_type=F32)
        o = (p[HALO - 1:HALO - 1 + HY_TL] * cw_ref[0:1, sl] + p[HALO:HALO + HY_TL] * cw_ref[1:2, sl]
             + p[HALO + 1:HALO + 1 + HY_TL] * cw_ref[2:3, sl] + cb_ref[:, sl])
        outs.append(o)
    x0_ref[0] = outs[0].astype(BF16)
    z_ref[0] = (outs[2] * outs[1]).astype(BF16)


def _hy_in_call(x, sh, sc, w, cw, cb):
    nb8 = HY_TL // HALO
    t3 = lambda b, j: (b, j, 0)
    full2 = lambda b, j: (0, 0)
    per_b = lambda b, j: (b, 0, 0)
    return pl.pallas_call(
        _hy_in_kernel, grid=(BATCH, SEQ // HY_TL),
        in_specs=[pl.BlockSpec((1, HY_TL, D_MODEL), t3),
                  pl.BlockSpec((1, HALO, D_MODEL), lambda b, j: (b, jnp.maximum(j * nb8 - 1, 0), 0)),
                  pl.BlockSpec((1, HALO, D_MODEL), lambda b, j: (b, jnp.minimum((j + 1) * nb8, SEQ // HALO - 1), 0)),
                  pl.BlockSpec((1, 1, D_MODEL), per_b),
                  pl.BlockSpec((1, 1, D_MODEL), per_b),
                  pl.BlockSpec((D_MODEL, 3 * HY_WIDTH), full2),
                  pl.BlockSpec((SHORT_CONV, 3 * HY_WIDTH), full2),
                  pl.BlockSpec((1, 3 * HY_WIDTH), full2)],
        out_specs=[pl.BlockSpec((1, HY_TL, HY_WIDTH), t3), pl.BlockSpec((1, HY_TL, HY_WIDTH), t3)],
        out_shape=[jax.ShapeDtypeStruct((BATCH, SEQ, HY_WIDTH), BF16),
                   jax.ShapeDtypeStruct((BATCH, SEQ, HY_WIDTH), BF16)],
        scratch_shapes=[pltpu.VMEM((HY_TL + 2 * HALO, D_MODEL), F32)],
        compiler_params=_params(("parallel", "arbitrary")), name="hyena_in_proj",
    )(x, x, x, sh, sc, w, cw, cb)


def _fft_tables():
    c = np.arange(FFT_N2, dtype=np.int64)
    ang = 2.0 * np.pi * ((c[:, None] * c[None, :]) % FFT_N2) / FFT_N2
    sr, si = np.cos(ang), -np.sin(ang)
    m = np.block([[sr, -si], [si, sr]])
    k1 = np.arange(FFT_NK, dtype=np.int64)
    ang_t = 2.0 * np.pi * (k1[:, None] * c[None, :]) / DFT_N
    lanes = np.ones((1, 1, V7X_LANES))
    tr = np.cos(ang_t)[:, :, None] * lanes
    ti = -np.sin(ang_t)[:, :, None] * lanes
    return jnp.asarray(m, BF16), jnp.asarray(tr, F32), jnp.asarray(ti, F32)


def _lin(acc, coef, val):
    if abs(coef) < 1e-12:
        return acc
    term = val if coef == 1.0 else (-val if coef == -1.0 else coef * val)
    return term if acc is None else acc + term


def _twiddle(tr_ref, ti_ref, k1, width):
    reps = width // V7X_LANES
    tr, ti = tr_ref[k1], ti_ref[k1]
    return jnp.concatenate([tr] * reps, axis=1), jnp.concatenate([ti] * reps, axis=1)


_COS_PI_4 = math.sqrt(0.5)


def _blocks_to_classes(z):
    assert FFT_N1 == 16
    sp = [z[a] + z[a + 4] for a in range(4)]
    sm = [z[a] - z[a + 4] for a in range(4)]
    q0p, q0m, q1p, q1m = sp[0] + sp[2], sp[0] - sp[2], sp[1] + sp[3], sp[1] - sp[3]
    d, e = _COS_PI_4 * (sm[1] - sm[3]), _COS_PI_4 * (sm[1] + sm[3])
    out = [None] * FFT_NK
    out[0] = (q0p + q1p, None)
    out[8] = (q0p - q1p, None)
    out[4] = (q0m, -q1m)
    out[2] = (sm[0] + d, -(sm[2] + e))
    out[6] = (sm[0] - d, sm[2] - e)
    for k in (1, 3):
        halves = []
        for parity in (0, 1):
            hr = hi = None
            for a in range(parity, FFT_NA, 2):
                th = 2.0 * math.pi * ((a * k) % FFT_N1) / FFT_N1
                hr = _lin(hr, round(math.cos(th), 15), z[a])
                hi = _lin(hi, round(-math.sin(th), 15), z[a])
            halves.append((hr, hi))
        (er, ei), (odr, odi) = halves
        out[k] = (er + odr, ei + odi)
        out[8 - k] = (er - odr, odi - ei)
    return out


def _classes_to_blocks(v):
    assert FFT_N1 == 16
    base = (v[0][0] + v[8][0], v[0][0] - v[8][0])
    pr = {k: (v[k][0] + v[8 - k][0], v[k][0] - v[8 - k][0]) for k in (1, 2, 3)}
    pi = {k: (v[k][1] - v[8 - k][1], v[k][1] + v[8 - k][1]) for k in (1, 2, 3)}
    out = []
    for a in range(FFT_NA):
        odd = a % 2
        mid = (v[4][0], 1.0 - (a % 4)) if not odd else (v[4][1], (a % 4) - 2.0)
        acc = _lin(base[odd], mid[1], mid[0])
        for k in (1, 2, 3):
            th = 2.0 * math.pi * ((a * k) % FFT_N1) / FFT_N1
            acc = _lin(acc, round(math.cos(th), 15), pr[k][odd])
            acc = _lin(acc, round(-math.sin(th), 15), pi[k][odd])
        out.append(acc)
    return out


def _class_forward(yr, yi, k1, m_ref, tr_ref, ti_ref, width):
    if k1 == 0:
        x = jnp.dot(m_ref[:, :FFT_N2], yr.astype(BF16), preferred_element_type=F32)
    else:
        tr, ti = _twiddle(tr_ref, ti_ref, k1, width)
        yr, yi = (yr * tr, yr * ti) if yi is None else (yr * tr - yi * ti, yr * ti + yi * tr)
        x = jnp.dot(m_ref[...], jnp.concatenate([yr, yi], axis=0).astype(BF16), preferred_element_type=F32)
    return x[:FFT_N2], x[FFT_N2:]


def _class_backward(pr, pi, k1, m_ref, tr_ref, ti_ref, width):
    v = jnp.dot(m_ref[...], jnp.concatenate([pr, -pi], axis=0).astype(BF16), preferred_element_type=F32)
    ur, ui = v[:FFT_N2], -v[FFT_N2:]
    if k1 > 0:
        tr, ti = _twiddle(tr_ref, ti_ref, k1, width)
        ur, ui = ur * tr + ui * ti, ui * tr - ur * ti
    return ur, ui


def _filter_features():
    t = np.linspace(0.0, 1.0, SEQ)[:, None]
    ang = 2.0 * np.pi * np.arange(SEQ)[:, None] / SEQ
    bands = np.linspace(1e-4, FILT_BANDS - 1, FILT_BANDS)
    z = np.concatenate([t, np.cos(bands * ang), -np.sin(bands * ang)], axis=-1)
    return jnp.asarray(np.pad(z, ((0, 0), (0, FILT_PAD - FILT_EMB))), F32)


def _spec_kernel(zf_ref, w1_ref, b1_ref, w2_ref, b2_ref, fq_ref, w3f_ref, w3b_ref, dl_ref, m_ref, tr_ref, ti_ref,
                 c_ref, hf_sc, hb_sc):
    hi = lax.Precision.HIGHEST
    zf, fq = zf_ref[...], fq_ref[...]
    hid = jnp.sin(fq * (jnp.dot(zf, w1_ref[...], preferred_element_type=F32, precision=hi) + b1_ref[...]))
    hid = jnp.sin(fq * (jnp.dot(hid, w2_ref[...], preferred_element_type=F32, precision=hi) + b2_ref[...]))
    decay = jnp.exp(-zf[:, 0:1] * dl_ref[...])
    hf = jnp.dot(hid, w3f_ref[...], preferred_element_type=F32, precision=hi) * decay
    hb = jnp.dot(hid, w3b_ref[...], preferred_element_type=F32, precision=hi) * decay
    ssq = jnp.sum(hf * hf, axis=0, keepdims=True) + jnp.sum(hb * hb, axis=0, keepdims=True)
    inv = lax.rsqrt(ssq + EPS)
    hf_sc[...] = hf * inv
    hb_sc[...] = hb * inv
    rows = lambda ref: [ref[a * FFT_N2:(a + 1) * FFT_N2, :] for a in range(FFT_NA)]
    fwd, bwd = _blocks_to_classes(rows(hf_sc)), _blocks_to_classes(rows(hb_sc))
    for k1 in range(FFT_NK):
        fr, fi = _class_forward(*fwd[k1], k1, m_ref, tr_ref, ti_ref, HY_CT)
        br, bi = _class_forward(*bwd[k1], k1, m_ref, tr_ref, ti_ref, HY_CT)
        scale = (1.0 if k1 in (0, FFT_N1 // 2) else 2.0) / DFT_N
        c_ref[k1, :FFT_N2, :] = ((fr + br) * scale).astype(BF16)
        c_ref[k1, FFT_N2:, :] = ((fi - bi) * scale).astype(BF16)


def _fft_table_specs(ngrid):
    z = (0,) * 2
    z3 = (0,) * 3
    if ngrid == 1:
        return [pl.BlockSpec((2 * FFT_N2, 2 * FFT_N2), lambda c: z),
                pl.BlockSpec((FFT_NK, FFT_N2, V7X_LANES), lambda c: z3),
                pl.BlockSpec((FFT_NK, FFT_N2, V7X_LANES), lambda c: z3)]
    return [pl.BlockSpec((2 * FFT_N2, 2 * FFT_N2), lambda b, c: z),
            pl.BlockSpec((FFT_NK, FFT_N2, V7X_LANES), lambda b, c: z3),
            pl.BlockSpec((FFT_NK, FFT_N2, V7X_LANES), lambda b, c: z3)]


def _spec_call(w1, b1, w2, b2, w3, freq, m, tr, ti):
    nct = HY_WIDTH // HY_CT

    def pad2(a, rows, cols):
        return jnp.pad(a, ((0, rows - a.shape[0]), (0, cols - a.shape[1])))

    full = lambda c: (0, 0)
    small = pl.BlockSpec((FILT_PAD, FILT_PAD), full)
    vec = pl.BlockSpec((1, FILT_PAD), full)
    deltas = jnp.asarray(np.linspace(HY_MIN_DECAY, HY_MAX_DECAY, HY_WIDTH)[None, :], F32)
    w3p = pad2(w3, FILT_PAD, 2 * HY_WIDTH)
    return pl.pallas_call(
        _spec_kernel, grid=(nct,),
        in_specs=[pl.BlockSpec((SEQ, FILT_PAD), full), small, vec, small, vec, vec,
                  pl.BlockSpec((FILT_PAD, HY_CT), lambda c: (0, c)),
                  pl.BlockSpec((FILT_PAD, HY_CT), lambda c: (0, c + nct)),
                  pl.BlockSpec((1, HY_CT), lambda c: (0, c))] + _fft_table_specs(1),
        out_specs=pl.BlockSpec((FFT_NK, 2 * FFT_N2, HY_CT), lambda c: (0, 0, c)),
        out_shape=jax.ShapeDtypeStruct((FFT_NK, 2 * FFT_N2, HY_WIDTH), BF16),
        scratch_shapes=[pltpu.VMEM((SEQ, HY_CT), F32), pltpu.VMEM((SEQ, HY_CT), F32)],
        compiler_params=_params(("arbitrary",)), name="hyena_filter_spectrum",
    )(_filter_features(), pad2(w1, FILT_PAD, FILT_PAD), pad2(b1[None, :], 1, FILT_PAD),
      pad2(w2, FILT_PAD, FILT_PAD), pad2(b2[None, :], 1, FILT_PAD), pad2(freq[None, :], 1, FILT_PAD),
      w3p, w3p, deltas, m, tr, ti)


def _conv_kernel(z_ref, c_ref, m_ref, tr_ref, ti_ref, y_ref, cls):
    for k1, (yr, yi) in enumerate(_blocks_to_classes(
            [z_ref[0, a * FFT_N2:(a + 1) * FFT_N2, :].astype(F32) for a in range(FFT_NA)])):
        cls[k1, :FFT_N2, :] = yr
        if yi is not None:
            cls[k1, FFT_N2:, :] = yi
    for k1 in range(FFT_NK):
        yi = None if k1 in (0, FFT_N1 // 2) else cls[k1, FFT_N2:, :]
        xr, xi = _class_forward(cls[k1, :FFT_N2, :], yi, k1, m_ref, tr_ref, ti_ref, HY_CT)
        cr = c_ref[k1, :FFT_N2, :].astype(F32)
        ci = c_ref[k1, FFT_N2:, :].astype(F32)
        ur, ui = _class_backward(xr * cr - xi * ci, xr * ci + xi * cr, k1, m_ref, tr_ref, ti_ref, HY_CT)
        cls[k1, :FFT_N2, :] = ur
        cls[k1, FFT_N2:, :] = ui
    blocks = _classes_to_blocks([(cls[k1, :FFT_N2, :], cls[k1, FFT_N2:, :]) for k1 in range(FFT_NK)])
    for a, ya in enumerate(blocks):
        y_ref[0, a * FFT_N2:(a + 1) * FFT_N2, :] = ya.astype(BF16)


def _conv_call(z, spec, m, tr, ti):
    return pl.pallas_call(
        _conv_kernel, grid=(HY_WIDTH // HY_CT, BATCH),
        in_specs=[pl.BlockSpec((1, SEQ, HY_CT), lambda c, b: (b, 0, c)),
                  pl.BlockSpec((FFT_NK, 2 * FFT_N2, HY_CT), lambda c, b: (0, 0, c))] + _fft_table_specs(2),
        out_specs=pl.BlockSpec((1, SEQ, HY_CT), lambda c, b: (b, 0, c)),
        out_shape=jax.ShapeDtypeStruct((BATCH, SEQ, HY_WIDTH), BF16),
        scratch_shapes=[pltpu.VMEM((FFT_NK, 2 * FFT_N2, HY_CT), F32)],
        compiler_params=_params(("parallel", "arbitrary")), name="hyena_long_conv",
    )(z, spec, m, tr, ti)


def _hy_out_kernel(y_ref, z_ref, x0_ref, b_ref, w_ref, x_ref, g_ref, *route):
    o_ref = route[5]
    z = z_ref[0].astype(F32)
    gated = x0_ref[0].astype(F32) * (y_ref[0].astype(F32) + b_ref[...] * z)
    mix = jnp.dot(gated.astype(BF16), w_ref[...], preferred_element_type=F32)
    x_new = x_ref[0] + g_ref[0] * mix
    o_ref[0] = x_new
    _route_tile(x_new, *route[:5], *route[6:])


def _hy_out_call(y, z, x0, bias, w, x, g1, route):
    r_args, r_in, r_out, r_shape, r_scratch = route
    t3 = lambda b, j: (b, j, 0)
    full2 = lambda b, j: (0, 0)
    return pl.pallas_call(
        _hy_out_kernel, grid=(BATCH, SEQ // HY_TL),
        in_specs=[pl.BlockSpec((1, HY_TL, HY_WIDTH), t3),
                  pl.BlockSpec((1, HY_TL, HY_WIDTH), t3),
                  pl.BlockSpec((1, HY_TL, HY_WIDTH), t3),
                  pl.BlockSpec((1, HY_WIDTH), full2),
                  pl.BlockSpec((HY_WIDTH, D_MODEL), full2),
                  pl.BlockSpec((1, HY_TL, D_MODEL), t3),
                  pl.BlockSpec((1, 1, D_MODEL), lambda b, j: (b, 0, 0))] + r_in,
        out_specs=[pl.BlockSpec((1, HY_TL, D_MODEL), t3)] + r_out,
        out_shape=[jax.ShapeDtypeStruct((BATCH, SEQ, D_MODEL), F32)] + r_shape,
        scratch_shapes=r_scratch,
        compiler_params=_params(("arbitrary", "arbitrary")), name="hyena_out_proj_route",
    )(y, z, x0, bias, w, x, g1, *r_args)


def kernel(x, c, ctx, c_ctx, ada_w, ada_b, ev_w_in, mla_q_norm, mla_w_uq, mla_kv_norm, mla_w_ukv, mla_q_qknorm, mla_k_qknorm, s5_lam_re, s5_lam_im, s5_log_step, s5_b_re, s5_b_im, s5_c_re, s5_c_im, s5_d, s5_glu_w, s5_glu_b, ev_w_out, hy_w_in, hy_conv_w, hy_conv_b, hy_f_w1, hy_f_b1, hy_f_w2, hy_f_b2, hy_f_w3, hy_f_freq, hy_bias, hy_w_out, moe_router_w, moe_router_b, moe_w_gate, moe_w_up, moe_w_down, moe_sh_gate, moe_sh_up, moe_sh_down):
    D = D_MODEL
    mod_all = _ada_call(c, c_ctx, ada_w, ada_b)

    def mods(li):
        return [m[:, None, :] for m in jnp.split(mod_all[li, :BATCH], 6, axis=-1)]

    sh1, sc1, g1, sh2, sc2, g2 = mods(0)
    mod_ctx = mod_all[0, BATCH, :2 * D]
    w0 = _a0_weights(ev_w_in[0], mla_q_norm[0], mla_w_uq[0], mla_kv_norm[0], mla_w_ukv[0],
                     mla_q_qknorm[0], mla_k_qknorm[0])
    q, k, v, u = _a0_call(x, ctx, sh1, sc1, mod_ctx[None, :D], mod_ctx[None, D:], w0)
    att = _attn_call(q, k, v)
    ug = u.reshape(BATCH, S5_NCHUNK, S5_CHUNK, S5_GROUPS, S5_GROUP)
    ug = ug.transpose(3, 1, 0, 2, 4).reshape(S5_GROUPS, S5_ROWS, S5_COLS)
    ys = _s5_call(ug, *_s5_weights(s5_lam_re[0], s5_lam_im[0], s5_log_step[0], s5_b_re[0], s5_b_im[0],
                                   s5_c_re[0], s5_c_im[0]))
    ys = ys.reshape(S5_GROUPS, S5_NCHUNK_LAT, BATCH, S5_CHUNK, S5_GROUP)
    ys = ys.transpose(2, 1, 3, 0, 4).reshape(BATCH, SEQ, S5_WIDTH)
    wo = ev_w_out[0].astype(BF16)
    wo_att = jnp.concatenate([wo[:MLA_WIDTH].reshape(MLA_HEADS, V_HEAD, D_MODEL),
                              jnp.zeros((MLA_HEADS, HEAD_PAD - V_HEAD, D_MODEL), BF16)], axis=1)
    x, *routed = _a1_call(att, ys, u, s5_d[0][None, :], s5_glu_w[0].astype(BF16), s5_glu_b[0][None, :],
                          wo_att.reshape(MLA_HEADS * HEAD_PAD, D_MODEL), wo[MLA_WIDTH:], x, g1,
                          _route_plumbing(sh2, sc2, moe_router_w[0], moe_router_b[0]))
    x = _moe(x, routed, sh2, sc2, g2, moe_w_gate, moe_w_up, moe_w_down,
             moe_sh_gate[0], moe_sh_up[0], moe_sh_down[0], 0)

    sh1, sc1, g1, sh2, sc2, g2 = mods(1)
    z, x0 = _hy_in_call(x, sh1, sc1, hy_w_in[0].astype(BF16), hy_conv_w[0], hy_conv_b[0][None, :])
    fft_tabs = _fft_tables()
    spectrum = _spec_call(hy_f_w1[0], hy_f_b1[0], hy_f_w2[0], hy_f_b2[0], hy_f_w3[0], hy_f_freq[0], *fft_tabs)
    y = _conv_call(z, spectrum, *fft_tabs)
    x, *routed = _hy_out_call(y, z, x0, hy_bias[0][None, :], hy_w_out[0].astype(BF16), x, g1,
                              _route_plumbing(sh2, sc2, moe_router_w[1], moe_router_b[1]))
    x = _moe(x, routed, sh2, sc2, g2, moe_w_gate, moe_w_up, moe_w_down,
             moe_sh_gate[1], moe_sh_up[1], moe_sh_down[1], 1)
    return x
```

```python
import math

import numpy as np
import jax
import jax.numpy as jnp
from jax import lax
from jax.experimental import pallas as pl
from jax.experimental.pallas import tpu as pltpu
from jax.experimental.pallas import tpu_sc as plsc

F32 = jnp.float32
BF16 = jnp.bfloat16

D_MODEL = 1024
BATCH = 8
SEQ = 4096
CTX_LEN = 256
KV_LEN = SEQ + CTX_LEN
GRID_W = 64
EPS = 1e-6

MLA_HEADS = 8
QK_NOPE = 64
QK_ROPE = 32
QK_HEAD = QK_NOPE + QK_ROPE
V_HEAD = 64
Q_LORA = 256
KV_LORA = 128
MLA_WIDTH = MLA_HEADS * V_HEAD
ROPE_BASE = 10000.0
HEAD_PAD = 128

S5_WIDTH = 512
S5_GROUP = 16
S5_GROUPS = S5_WIDTH // S5_GROUP
S5_STATE = 64
S5_CHUNK = 32
S5_NCHUNK = KV_LEN // S5_CHUNK
S5_NCHUNK_LAT = SEQ // S5_CHUNK
S5_NCHUNK_CTX = CTX_LEN // S5_CHUNK

HY_WIDTH = D_MODEL
FILT_EMB = 33
FILT_BANDS = (FILT_EMB - 1) // 2
FILT_PAD = 128
SHORT_CONV = 3
HY_MIN_DECAY = -math.log(1e-2) / 1.5
HY_MAX_DECAY = -math.log(1e-2) / 0.3
DFT_N = 2 * SEQ
FFT_N1 = 16
FFT_N2 = DFT_N // FFT_N1
FFT_NA = FFT_N1 // 2
FFT_NK = FFT_N1 // 2 + 1
HY_CT = 256

N_EXPERTS = 64
TOP_K = 6
EXPERT_FF = 256
ROUTE_SCALE = 2.5
MOE_BLOCK = 1024
MOE_TL = 512
ROW_WORDS = D_MODEL // 4
SC_WINDOW = 128

V7X_LANES = 128
V7X_VMEM_BYTES = 64 * 1024 * 1024
VMEM_LIMIT = V7X_VMEM_BYTES - 8 * 1024 * 1024


def _params(semantics):
    return pltpu.CompilerParams(dimension_semantics=semantics, vmem_limit_bytes=VMEM_LIMIT)


def _norm_mod(x, shift, scale):
    ms = jnp.mean(x * x, axis=-1, keepdims=True)
    return x * lax.rsqrt(ms + EPS) * (1.0 + scale) + shift


def _rms(x, gain, n):
    ms = jnp.sum(x * x, axis=-1, keepdims=True) * (1.0 / n)
    return x * lax.rsqrt(ms + EPS) * gain


ADA_ROWS = 16
ADA_TN = 1024


def _ada_kernel(c_ref, w_ref, b_ref, o_ref):
    sc = jax.nn.silu(c_ref[...])
    o_ref[0] = jnp.dot(sc, w_ref[0], preferred_element_type=F32, precision=lax.Precision.HIGHEST) + b_ref[0]


def _ada_call(c, c_ctx, ada_w, ada_b):
    depth, _, width = ada_w.shape
    rows = jnp.concatenate([c, c_ctx[None, :], jnp.zeros((ADA_ROWS - BATCH - 1, D_MODEL), F32)], axis=0)
    return pl.pallas_call(
        _ada_kernel, grid=(depth, width // ADA_TN),
        in_specs=[pl.BlockSpec((ADA_ROWS, D_MODEL), lambda l, n: (0, 0)),
                  pl.BlockSpec((1, D_MODEL, ADA_TN), lambda l, n: (l, 0, n)),
                  pl.BlockSpec((1, 1, ADA_TN), lambda l, n: (l, 0, n))],
        out_specs=pl.BlockSpec((1, ADA_ROWS, ADA_TN), lambda l, n: (l, 0, n)),
        out_shape=jax.ShapeDtypeStruct((depth, ADA_ROWS, width), F32),
        compiler_params=_params(("parallel", "arbitrary")), name="adaln_modulation",
    )(rows, ada_w, ada_b[:, None, :])


A0_TL = 256
A0_NT = SEQ // A0_TL


def _rope_perm():
    return np.concatenate([np.arange(0, QK_ROPE, 2), np.arange(1, QK_ROPE, 2)])


def _rope_tables():
    t = np.arange(SEQ)
    row = (t // GRID_W).astype(np.float64)
    col = (t % GRID_W).astype(np.float64)
    n_freq = QK_ROPE // 4
    inv = ROPE_BASE ** (-np.arange(n_freq, dtype=np.float64) / n_freq)
    ang = np.concatenate([row[:, None] * inv, col[:, None] * inv], axis=-1)
    cos, sin = np.cos(ang), np.sin(ang)
    half = QK_ROPE // 2
    a = np.zeros((KV_LEN, HEAD_PAD))
    b = np.zeros((KV_LEN, HEAD_PAD))
    a[:, :QK_HEAD] = 1.0
    a[:SEQ, QK_NOPE:QK_NOPE + half] = cos
    a[:SEQ, QK_NOPE + half:QK_HEAD] = cos
    b[:SEQ, QK_NOPE:QK_NOPE + half] = -sin
    b[:SEQ, QK_NOPE + half:QK_HEAD] = sin
    return a, b


def _norm_rope_heads(f, gain_ref, a, b, out_ref):
    width = MLA_HEADS * HEAD_PAD
    for hd in range(MLA_HEADS):
        sl = slice(hd * HEAD_PAD, (hd + 1) * HEAD_PAD)
        x = f[:, sl]
        r = lax.rsqrt(jnp.sum(x * x, axis=-1, keepdims=True) * (1.0 / QK_HEAD) + EPS)
        rot = x * (a * gain_ref[:, sl]) + f[:, width + hd * HEAD_PAD:width + (hd + 1) * HEAD_PAD] * b
        out_ref[0, :, sl] = (rot * r).astype(BF16)


def _a0_kernel(x_ref, ctx_ref, sh_ref, sc_ref, shc_ref, scc_ref, win_ref, qn_ref, wuq_ref, kvn_ref,
               wk_ref, wuv_ref, qg_ref, kg_ref, ka_ref, kb_ref, qa_ref, qb_ref,
               q_ref, k_ref, v_ref, u_ref):
    j = pl.program_id(1)
    is_ctx = j == A0_NT
    xin = jnp.where(is_ctx, ctx_ref[0], x_ref[0])
    shift = jnp.where(is_ctx, shc_ref[...], sh_ref[0])
    scale = jnp.where(is_ctx, scc_ref[...], sc_ref[0])
    h = _norm_mod(xin, shift, scale).astype(BF16)
    proj = jnp.dot(h, win_ref[...], preferred_element_type=F32)
    u_ref[0] = proj[:, 512:].astype(BF16)

    c_kv = _rms(proj[:, Q_LORA:Q_LORA + KV_LORA], kvn_ref[...], KV_LORA).astype(BF16)
    lane = lax.broadcasted_iota(jnp.int32, (1, MLA_HEADS * HEAD_PAD), 1)
    ones_lane = (lane % HEAD_PAD == V_HEAD).astype(F32)
    v_ref[0] = (jnp.dot(c_kv, wuv_ref[...], preferred_element_type=F32) + ones_lane).astype(BF16)
    kin = jnp.concatenate([c_kv, proj[:, 384:512].astype(BF16)], axis=1)
    kf = jnp.dot(kin, wk_ref[...], preferred_element_type=F32)
    _norm_rope_heads(kf, kg_ref, ka_ref[...], kb_ref[...], k_ref)

    @pl.when(j < A0_NT)
    def _():
        ql = _rms(proj[:, :Q_LORA], qn_ref[...], Q_LORA).astype(BF16)
        qf = jnp.dot(ql, wuq_ref[...], preferred_element_type=F32)
        _norm_rope_heads(qf, qg_ref, qa_ref[...], qb_ref[...], q_ref)


def _a0_call(x, ctx, sh, sc, shc, scc, w):
    nt = A0_NT
    lat = lambda b, j: (b, jnp.minimum(j, nt - 1), 0)
    full2 = lambda b, j: (0, 0)
    per_b = lambda b, j: (b, 0, 0)
    tab = pl.BlockSpec((A0_TL, HEAD_PAD), lambda b, j: (j, 0))
    in_specs = [
        pl.BlockSpec((1, A0_TL, D_MODEL), lat),
        pl.BlockSpec((1, CTX_LEN, D_MODEL), per_b),
        pl.BlockSpec((1, 1, D_MODEL), per_b),
        pl.BlockSpec((1, 1, D_MODEL), per_b),
        pl.BlockSpec((1, D_MODEL), full2),
        pl.BlockSpec((1, D_MODEL), full2),
        pl.BlockSpec((D_MODEL, 1024), full2),
        pl.BlockSpec((1, Q_LORA), full2),
        pl.BlockSpec((Q_LORA, 2 * MLA_HEADS * HEAD_PAD), full2),
        pl.BlockSpec((1, KV_LORA), full2),
        pl.BlockSpec((2 * KV_LORA, 2 * MLA_HEADS * HEAD_PAD), full2),
        pl.BlockSpec((KV_LORA, MLA_HEADS * HEAD_PAD), full2),
        pl.BlockSpec((1, MLA_HEADS * HEAD_PAD), full2),
        pl.BlockSpec((1, MLA_HEADS * HEAD_PAD), full2),
        tab, tab, tab, tab,
    ]
    out_specs = [
        pl.BlockSpec((1, A0_TL, MLA_HEADS * HEAD_PAD), lat),
        pl.BlockSpec((1, A0_TL, MLA_HEADS * HEAD_PAD), lambda b, j: (b, j, 0)),
        pl.BlockSpec((1, A0_TL, MLA_HEADS * HEAD_PAD), lambda b, j: (b, j, 0)),
        pl.BlockSpec((1, A0_TL, S5_WIDTH), lambda b, j: (b, j, 0)),
    ]
    out_shape = [
        jax.ShapeDtypeStruct((BATCH, SEQ, MLA_HEADS * HEAD_PAD), BF16),
        jax.ShapeDtypeStruct((BATCH, KV_LEN, MLA_HEADS * HEAD_PAD), BF16),
        jax.ShapeDtypeStruct((BATCH, KV_LEN, MLA_HEADS * HEAD_PAD), BF16),
        jax.ShapeDtypeStruct((BATCH, KV_LEN, S5_WIDTH), BF16),
    ]
    return pl.pallas_call(
        _a0_kernel, grid=(BATCH, nt + 1), in_specs=in_specs, out_specs=out_specs, out_shape=out_shape,
        compiler_params=_params(("parallel", "arbitrary")), name="even_in_proj",
    )(x, ctx, sh, sc, shc, scc, *w)


def _gain_swap(w, gain):
    half = QK_ROPE // 2
    wg = (w * gain).reshape(w.shape[0], MLA_HEADS, HEAD_PAD)
    re, im = wg[..., QK_NOPE:QK_NOPE + half], wg[..., QK_NOPE + half:QK_HEAD]
    out = jnp.concatenate([jnp.zeros_like(wg[..., :QK_NOPE]), im, re, jnp.zeros_like(wg[..., QK_HEAD:])], axis=-1)
    return out.reshape(w.shape)


def _a0_weights(w_in, q_norm, w_uq, kv_norm, w_ukv, q_qk, k_qk):
    perm = _rope_perm()
    kr0 = Q_LORA + KV_LORA
    w_cat = jnp.concatenate([
        w_in[:, :kr0], w_in[:, kr0:kr0 + QK_ROPE][:, perm],
        jnp.zeros((D_MODEL, HEAD_PAD - QK_ROPE), F32), w_in[:, kr0 + QK_ROPE:]], axis=1).astype(BF16)
    pad = HEAD_PAD - QK_HEAD

    def head_gain(g):
        gh = jnp.concatenate([g[:QK_NOPE], g[QK_NOPE:][perm], jnp.zeros((pad,), F32)])
        return jnp.tile(gh, MLA_HEADS)[None, :]

    uq = w_uq.reshape(Q_LORA, MLA_HEADS, QK_HEAD)
    uq = jnp.concatenate([uq[..., :QK_NOPE], uq[..., QK_NOPE:][..., perm],
                          jnp.zeros((Q_LORA, MLA_HEADS, pad), F32)], axis=-1)
    uq = uq.reshape(Q_LORA, MLA_HEADS * HEAD_PAD)
    uq = jnp.concatenate([uq, _gain_swap(uq, head_gain(q_qk))], axis=1).astype(BF16)
    ukv = w_ukv.reshape(KV_LORA, MLA_HEADS, QK_NOPE + V_HEAD)
    uk = jnp.concatenate([ukv[..., :QK_NOPE], jnp.zeros((KV_LORA, MLA_HEADS, HEAD_PAD - QK_NOPE), F32)], axis=-1)
    uk = uk.reshape(KV_LORA, MLA_HEADS * HEAD_PAD)
    place = np.zeros((KV_LORA, MLA_HEADS, HEAD_PAD), np.float32)
    for i in range(QK_ROPE):
        place[i, :, QK_NOPE + i] = 1.0
    wk = jnp.concatenate([uk, jnp.asarray(place.reshape(KV_LORA, MLA_HEADS * HEAD_PAD))], axis=0)
    wk = jnp.concatenate([wk, _gain_swap(wk, head_gain(k_qk))], axis=1).astype(BF16)
    wuv = jnp.concatenate([ukv[..., QK_NOPE:], jnp.zeros((KV_LORA, MLA_HEADS, HEAD_PAD - V_HEAD), F32)], axis=-1)
    wuv = wuv.reshape(KV_LORA, MLA_HEADS * HEAD_PAD).astype(BF16)
    a, b = _rope_tables()
    qs = QK_HEAD ** -0.5 * math.log2(math.e)
    tabs = [jnp.asarray(t, F32) for t in (a, b, a * qs, b * qs)]
    return [w_cat, q_norm[None, :], uq, kv_norm[None, :], wk, wuv, head_gain(q_qk), head_gain(k_qk)] + tabs


ATT_TQ = 256
HEADS_PER_STEP = 4


def _attn_kernel(q_ref, k_ref, v_ref, o_ref):
    for hh in range(HEADS_PER_STEP):
        sl = slice(hh * HEAD_PAD, (hh + 1) * HEAD_PAD)
        s = lax.dot_general(q_ref[0, :, sl], k_ref[0, :, sl], (((1,), (1,)), ((), ())),
                            preferred_element_type=F32)
        m = jnp.max(s, axis=-1, keepdims=True)
        p = jnp.exp2(s - m).astype(BF16)
        acc = jnp.dot(p, v_ref[0, :, sl], preferred_element_type=F32)
        o_ref[0, :, sl] = (acc * (1.0 / acc[:, V_HEAD:V_HEAD + 1])).astype(BF16)


def _attn_call(q, k, v):
    wq = HEADS_PER_STEP * HEAD_PAD
    return pl.pallas_call(
        _attn_kernel, grid=(BATCH, MLA_HEADS // HEADS_PER_STEP, SEQ // ATT_TQ),
        in_specs=[pl.BlockSpec((1, ATT_TQ, wq), lambda b, h, i: (b, i, h)),
                  pl.BlockSpec((1, KV_LEN, wq), lambda b, h, i: (b, 0, h)),
                  pl.BlockSpec((1, KV_LEN, wq), lambda b, h, i: (b, 0, h))],
        out_specs=pl.BlockSpec((1, ATT_TQ, wq), lambda b, h, i: (b, i, h)),
        out_shape=jax.ShapeDtypeStruct((BATCH, SEQ, MLA_HEADS * HEAD_PAD), BF16),
        compiler_params=_params(("parallel", "parallel", "arbitrary")), name="mla_attention",
    )(q, k, v)


S5_ROWS = S5_NCHUNK * BATCH
S5_ROWS_LAT = S5_NCHUNK_LAT * BATCH
S5_COLS = S5_CHUNK * S5_GROUP
S5_SW = 2 * S5_STATE


def _s5_kernel(u_ref, r_ref, mb_ref, mc_ref, coef_ref, y_ref, x_sc, sp_sc, t_sc):
    u = u_ref[0]
    lags = r_ref[0]
    for sg in range(S5_CHUNK):
        off = (S5_CHUNK - 1 - sg) * S5_GROUP
        t_sc[sg * S5_GROUP:(sg + 1) * S5_GROUP, :] = lags[:, off:off + S5_COLS].astype(BF16)
    x_sc[...] = jnp.dot(u, mb_ref[0], preferred_element_type=F32)
    cf = coef_ref[0]
    af, bfm, bfp, ab, bbm, bbp = [cf[i * 8:(i + 1) * 8] for i in range(6)]

    def body(i, carry):
        sf, sfw, sb, sbw = carry
        cfw = jnp.where(i < S5_NCHUNK_CTX, i + S5_NCHUNK_LAT, i - S5_NCHUNK_CTX)
        rf = pl.multiple_of(cfw * BATCH, BATCH)
        rb = pl.multiple_of((S5_NCHUNK - 1 - i) * BATCH, BATCH)
        sp_sc[pl.ds(rf, BATCH), 0:S5_SW] = sf
        sp_sc[pl.ds(rb, BATCH), S5_SW:2 * S5_SW] = sb
        xf = x_sc[pl.ds(rf, BATCH), 0:S5_SW]
        xfw = x_sc[pl.ds(rf, BATCH), S5_SW:2 * S5_SW]
        xb = x_sc[pl.ds(rb, BATCH), 2 * S5_SW:3 * S5_SW]
        xbw = x_sc[pl.ds(rb, BATCH), 3 * S5_SW:4 * S5_SW]
        return (sf * af + sfw * bfm + xf, sfw * af + sf * bfp + xfw,
                sb * ab + sbw * bbm + xb, sbw * ab + sb * bbp + xbw)

    z = jnp.zeros((BATCH, S5_SW), F32)
    lax.fori_loop(0, S5_NCHUNK, body, (z, z, z, z))
    y = jnp.dot(u[:S5_ROWS_LAT], t_sc[...], preferred_element_type=F32)
    y = y + jnp.dot(sp_sc[0:S5_ROWS_LAT, :].astype(BF16), mc_ref[0], preferred_element_type=F32)
    y_ref[0] = y.astype(BF16)


def _s5_call(ug, t, mb, mc, coef):
    g3 = lambda g: (g, 0, 0)
    return pl.pallas_call(
        _s5_kernel, grid=(S5_GROUPS,),
        in_specs=[pl.BlockSpec((1, S5_ROWS, S5_COLS), g3),
                  pl.BlockSpec((1, S5_GROUP, 2 * S5_COLS), g3),
                  pl.BlockSpec((1, S5_COLS, 4 * S5_SW), g3),
                  pl.BlockSpec((1, 2 * S5_SW, S5_COLS), g3),
                  pl.BlockSpec((1, 6 * 8, S5_SW), g3)],
        out_specs=pl.BlockSpec((1, S5_ROWS_LAT, S5_COLS), g3),
        out_shape=jax.ShapeDtypeStruct((S5_GROUPS, S5_ROWS_LAT, S5_COLS), BF16),
        scratch_shapes=[pltpu.VMEM((S5_ROWS, 4 * S5_SW), F32), pltpu.VMEM((S5_ROWS, 2 * S5_SW), F32),
                        pltpu.VMEM((S5_COLS, S5_COLS), BF16)],
        compiler_params=_params(("parallel",)), name="s5_chunked_scan",
    )(ug, t, mb, mc, coef)


def _s5_weights(lam_re, lam_im, log_step, b_re, b_im, c_re, c_im):
    q = S5_CHUNK
    hi = lax.Precision.HIGHEST
    sig = np.arange(q)
    dirs = np.arange(2)[None, :]
    lr = jnp.minimum(lam_re, -1e-4)
    li = lam_im
    step = jnp.exp(log_step)[..., None]
    jj = jnp.arange(q + 1, dtype=F32)[:, None, None, None]
    mag = jnp.exp(lr * step * jj)
    ph = li * step * jj
    pr, pi = mag * jnp.cos(ph), mag * jnp.sin(ph)
    nr, ni = pr[1] - 1.0, pi[1]
    den = lr * lr + li * li
    fr, fi = (nr * lr + ni * li) / den, (ni * lr - nr * li) / den
    br = fr[..., None] * b_re - fi[..., None] * b_im
    bi = fr[..., None] * b_im + fi[..., None] * b_re
    cpr = c_re[None] * pr[:, :, :, None, :] - c_im[None] * pi[:, :, :, None, :]
    cpi = c_re[None] * pi[:, :, :, None, :] + c_im[None] * pr[:, :, :, None, :]
    kern = jnp.einsum('jdghp,dgpk->jdghk', jnp.concatenate([cpr[:q], -cpi[:q]], axis=-1),
                      jnp.concatenate([br, bi], axis=2), precision=hi)
    kt = kern.transpose(1, 2, 4, 0, 3)
    zero_slots = jnp.zeros((S5_GROUPS, S5_GROUP, q, S5_GROUP), F32)
    t = (jnp.concatenate([zero_slots[:, :, :q - 1], kt[0], zero_slots[:, :, :1]], axis=2)
         + jnp.concatenate([kt[1, :, :, ::-1], zero_slots], axis=2)).reshape(S5_GROUPS, S5_GROUP, 2 * S5_COLS)
    pw = np.stack([q - 1 - sig, sig], axis=1)
    po = np.stack([sig + 1, q - sig], axis=1)
    ppr, ppi = pr[pw, dirs][..., None], pi[pw, dirs][..., None]
    xr = (ppr * br[None] - ppi * bi[None]).transpose(1, 2, 0, 4, 3).reshape(2, S5_GROUPS, S5_COLS, S5_STATE)
    xi = (ppr * bi[None] + ppi * br[None]).transpose(1, 2, 0, 4, 3).reshape(2, S5_GROUPS, S5_COLS, S5_STATE)
    mb = jnp.concatenate([xr[0], xi[0], xi[0], xr[0], xr[1], xi[1], xi[1], xr[1]], axis=-1).astype(BF16)
    mr = cpr[po, dirs].transpose(1, 2, 4, 0, 3).reshape(2, S5_GROUPS, S5_STATE, S5_COLS)
    mi = -cpi[po, dirs].transpose(1, 2, 4, 0, 3).reshape(2, S5_GROUPS, S5_STATE, S5_COLS)
    mc = jnp.concatenate([mr[0], mi[0], mr[1], mi[1]], axis=1).astype(BF16)
    are, aim = pr[q], pi[q]
    rows = jnp.stack([jnp.concatenate([are, are], -1), jnp.concatenate([-aim, aim], -1),
                      jnp.concatenate([aim, -aim], -1)], axis=1)
    coef = jnp.broadcast_to(rows.transpose(2, 0, 1, 3)[:, :, :, None, :], (S5_GROUPS, 2, 3, 8, S5_SW))
    return t, mb, mc, coef.reshape(S5_GROUPS, 6 * 8, S5_SW)


A1_TL = MOE_TL


def _a1_kernel(att_ref, ys_ref, u_ref, d_ref, gw_ref, gb_ref, woa_ref, wos_ref, x_ref, g_ref, *route):
    o_ref = route[5]
    y = u_ref[0].astype(F32) * d_ref[...] + ys_ref[0].astype(F32)
    z = jax.nn.gelu(y)
    gate = jax.nn.sigmoid(jnp.dot(z.astype(BF16), gw_ref[...], preferred_element_type=F32) + gb_ref[...])
    s5 = (z * gate).astype(BF16)
    mix = jnp.dot(att_ref[0], woa_ref[...], preferred_element_type=F32)
    mix = mix + jnp.dot(s5, wos_ref[...], preferred_element_type=F32)
    x_new = x_ref[0] + g_ref[0] * mix
    o_ref[0] = x_new
    _route_tile(x_new, *route[:5], *route[6:])


def _a1_call(att, ys, u, d, gw, gb, woa, wos, x, g1, route):
    r_args, r_in, r_out, r_shape, r_scratch = route
    t3 = lambda b, j: (b, j, 0)
    full2 = lambda b, j: (0, 0)
    return pl.pallas_call(
        _a1_kernel, grid=(BATCH, SEQ // A1_TL),
        in_specs=[pl.BlockSpec((1, A1_TL, MLA_HEADS * HEAD_PAD), t3),
                  pl.BlockSpec((1, A1_TL, S5_WIDTH), t3),
                  pl.BlockSpec((1, A1_TL, S5_WIDTH), t3),
                  pl.BlockSpec((1, S5_WIDTH), full2),
                  pl.BlockSpec((S5_WIDTH, S5_WIDTH), full2),
                  pl.BlockSpec((1, S5_WIDTH), full2),
                  pl.BlockSpec((MLA_HEADS * HEAD_PAD, D_MODEL), full2),
                  pl.BlockSpec((S5_WIDTH, D_MODEL), full2),
                  pl.BlockSpec((1, A1_TL, D_MODEL), t3),
                  pl.BlockSpec((1, 1, D_MODEL), lambda b, j: (b, 0, 0))] + r_in,
        out_specs=[pl.BlockSpec((1, A1_TL, D_MODEL), t3)] + r_out,
        out_shape=[jax.ShapeDtypeStruct((BATCH, SEQ, D_MODEL), F32)] + r_shape,
        scratch_shapes=r_scratch,
        compiler_params=_params(("arbitrary", "arbitrary")), name="even_out_proj_route",
    )(att, ys, u, d, gw, gb, woa, wos, x, g1, *r_args)


SLOT_PAD = 8


def _pack_rows(v):
    halves = []
    for p in range(2):
        base = 2 * p * ROW_WORDS
        a = pltpu.bitcast(v[:, base:base + ROW_WORDS].astype(BF16).astype(F32), jnp.uint32)
        b = pltpu.bitcast(v[:, base + ROW_WORDS:base + 2 * ROW_WORDS].astype(BF16).astype(F32), jnp.uint32)
        halves.append((a >> 16) | b)
    return halves


def _unpack_rows(lo, hi):
    out = []
    for w in (lo, hi):
        out.append(pltpu.bitcast(w << 16, F32))
        out.append(pltpu.bitcast(w & jnp.uint32(0xFFFF0000), F32))
    return out


def _route_tile(x, sh_ref, sc_ref, rwt_ref, rb_ref, tri_ref,
                hlo_ref, hhi_ref, idx_ref, wt_ref, rank_ref, cnt_ref, run_sc):
    @pl.when((pl.program_id(0) == 0) & (pl.program_id(1) == 0))
    def _():
        run_sc[...] = jnp.zeros_like(run_sc)

    h = _norm_mod(x, sh_ref[0], sc_ref[0])
    hb = h.astype(BF16)
    hlo_ref[0], hhi_ref[0] = _pack_rows(h)
    h_lo = (h - hb.astype(F32)).astype(BF16)
    rwt = rwt_ref[...]
    rw_hi = rwt.astype(BF16)
    rw_lo = (rwt - rw_hi.astype(F32)).astype(BF16)
    logits = lax.dot_general(jnp.concatenate([rw_hi, rw_lo, rw_hi], axis=1), jnp.concatenate([hb, hb, h_lo], axis=1),
                             (((1,), (1,)), ((), ())), preferred_element_type=F32)
    scores = jax.nn.sigmoid(logits)

    work = scores + rb_ref[...]
    expert = lax.broadcasted_iota(jnp.int32, work.shape, 0).astype(F32)
    hits, ids = [], []
    for _ in range(TOP_K):
        m = jnp.max(work, axis=0, keepdims=True)
        ik = jnp.min(jnp.where(work == m, expert, float(N_EXPERTS)), axis=0, keepdims=True)
        hit = expert == ik
        hits.append(hit)
        ids.append(ik)
        work = jnp.where(hit, -jnp.inf, work)
    mask = hits[0]
    for hit in hits[1:]:
        mask = jnp.logical_or(mask, hit)
    maskf = mask.astype(F32)
    before = jnp.dot(maskf.astype(BF16), tri_ref[...], preferred_element_type=F32) + run_sc[:, 0:1]
    sel = [jnp.sum(jnp.where(hit, scores, 0.0), axis=0, keepdims=True) for hit in hits]
    denom = sel[0]
    for s in sel[1:]:
        denom = denom + s
    ranks = [jnp.sum(jnp.where(hit, before, 0.0), axis=0, keepdims=True) for hit in hits]
    pad = [jnp.zeros_like(denom)] * (SLOT_PAD - TOP_K)
    idx_ref[...] = jnp.concatenate(ids + pad, axis=0).astype(jnp.int32)
    wt_ref[...] = jnp.concatenate([s / denom * ROUTE_SCALE for s in sel] + pad, axis=0)
    rank_ref[...] = jnp.concatenate(ranks + pad, axis=0).astype(jnp.int32)
    run_sc[...] += jnp.sum(maskf, axis=1, keepdims=True)
    cnt_ref[...] = run_sc[...]


def _route_plumbing(sh, sc, rw, rb):
    t3 = lambda b, j: (b, j, 0)
    full2 = lambda b, j: (0, 0)
    per_b = lambda b, j: (b, 0, 0)
    nt = SEQ // MOE_TL
    slots = lambda b, j: (0, b * nt + j)
    tri = jnp.asarray(np.triu(np.ones((MOE_TL, MOE_TL), np.float32), 1), BF16)
    args = [sh, sc, rw.T, rb[:, None], tri]
    in_specs = [pl.BlockSpec((1, 1, D_MODEL), per_b),
                pl.BlockSpec((1, 1, D_MODEL), per_b),
                pl.BlockSpec((N_EXPERTS, D_MODEL), full2),
                pl.BlockSpec((N_EXPERTS, 1), full2),
                pl.BlockSpec((MOE_TL, MOE_TL), full2)]
    out_specs = [pl.BlockSpec((1, MOE_TL, ROW_WORDS), t3),
                 pl.BlockSpec((1, MOE_TL, ROW_WORDS), t3),
                 pl.BlockSpec((SLOT_PAD, MOE_TL), slots),
                 pl.BlockSpec((SLOT_PAD, MOE_TL), slots),
                 pl.BlockSpec((SLOT_PAD, MOE_TL), slots),
                 pl.BlockSpec((N_EXPERTS, V7X_LANES), full2)]
    slot_i = jax.ShapeDtypeStruct((SLOT_PAD, BATCH * SEQ), jnp.int32)
    out_shape = [jax.ShapeDtypeStruct((BATCH, SEQ, ROW_WORDS), jnp.uint32),
                 jax.ShapeDtypeStruct((BATCH, SEQ, ROW_WORDS), jnp.uint32),
                 slot_i,
                 jax.ShapeDtypeStruct((SLOT_PAD, BATCH * SEQ), F32),
                 slot_i,
                 jax.ShapeDtypeStruct((N_EXPERTS, V7X_LANES), F32)]
    scratch = [pltpu.VMEM((N_EXPERTS, V7X_LANES), F32)]
    return args, in_specs, out_specs, out_shape, scratch


def _sc_mesh():
    return plsc.VectorSubcoreMesh(core_axis_name="c", subcore_axis_name="s")


def _sc_dispatch(h_words, dest, n_rows):
    n_tok = h_words.shape[0]

    @pl.kernel(out_type=jax.ShapeDtypeStruct((n_rows, ROW_WORDS), jnp.uint32), mesh=_sc_mesh(), scratch_types=[])
    def scatter_rows(h_hbm, i_hbm, o_hbm):
        def body(h_vmem, i_vmem):
            for k in range(TOP_K):
                pltpu.sync_copy(h_vmem, o_hbm.at[i_vmem.at[k]])

        pltpu.emit_pipeline(
            body, grid=(n_tok // SC_WINDOW,),
            in_specs=[pl.BlockSpec((SC_WINDOW, ROW_WORDS), index_map=lambda i: (i, 0)),
                      pl.BlockSpec((SLOT_PAD, SC_WINDOW), index_map=lambda i: (0, i))],
            out_specs=[],
            core_axis_name=("c", "s"), dimension_semantics=(pltpu.PARALLEL,),
        )(h_hbm, i_hbm)

    return scatter_rows(h_words, dest)


def _sc_collect(y_words, dest):
    n_tok = dest.shape[1]

    @pl.kernel(out_type=jax.ShapeDtypeStruct((TOP_K, n_tok, ROW_WORDS), jnp.uint32), mesh=_sc_mesh(),
               scratch_types=[])
    def gather_rows(y_hbm, i_hbm, o_hbm):
        def body(i_vmem, o_vmem):
            pltpu.sync_copy(y_hbm.at[i_vmem.at[0]], o_vmem.at[0])

        pltpu.emit_pipeline(
            body, grid=(TOP_K, n_tok // SC_WINDOW),
            in_specs=[pl.BlockSpec((1, SC_WINDOW), index_map=lambda k, i: (k, i))],
            out_specs=[pl.BlockSpec((1, SC_WINDOW, ROW_WORDS), index_map=lambda k, i: (k, i, 0))],
            core_axis_name=("c", "s"), dimension_semantics=(pltpu.PARALLEL, pltpu.PARALLEL),
        )(i_hbm, o_hbm)

    return gather_rows(y_words, dest)


def _expert_kernel(be_ref, nv_ref, xlo_ref, xhi_ref, wg_ref, wu_ref, wd_ref, ylo_ref, yhi_ref,
                   wg_sc, wu_sc, wd_sc):
    i = pl.program_id(0)
    nv = nv_ref[i]

    @pl.when(jnp.logical_or(i == 0, be_ref[i] != be_ref[jnp.maximum(i - 1, 0)]))
    def _():
        wg_sc[...] = wg_ref[0, 0].astype(BF16)
        wu_sc[...] = wu_ref[0, 0].astype(BF16)
        wd_sc[...] = wd_ref[0, 0].astype(BF16)

    @pl.when(nv > 0)
    def _():
        parts = _unpack_rows(xlo_ref[...], xhi_ref[...])
        xb = jnp.concatenate([p.astype(BF16) for p in parts], axis=1)
        live = lax.broadcasted_iota(jnp.int32, xb.shape, 0) < nv
        xb = jnp.where(live, xb, jnp.zeros_like(xb))
        hid = jax.nn.silu(jnp.dot(xb, wg_sc[...], preferred_element_type=F32))
        hid = hid * jnp.dot(xb, wu_sc[...], preferred_element_type=F32)
        y = jnp.dot(hid.astype(BF16), wd_sc[...], preferred_element_type=F32)
        ylo_ref[...], yhi_ref[...] = _pack_rows(y)

    @pl.when(nv == 0)
    def _():
        ylo_ref[...] = jnp.zeros_like(ylo_ref)
        yhi_ref[...] = jnp.zeros_like(yhi_ref)


def _expert_call(block_e, n_valid, xlo, xhi, wg, wu, wd, li):
    n_rows = xlo.shape[0]
    n_blocks = n_rows // MOE_BLOCK
    rows = pl.BlockSpec((MOE_BLOCK, ROW_WORDS), lambda i, be, nv: (i, 0))
    grid_spec = pltpu.PrefetchScalarGridSpec(
        num_scalar_prefetch=2, grid=(n_blocks,),
        in_specs=[rows, rows,
                  pl.BlockSpec((1, 1, D_MODEL, EXPERT_FF), lambda i, be, nv: (li, be[i], 0, 0)),
                  pl.BlockSpec((1, 1, D_MODEL, EXPERT_FF), lambda i, be, nv: (li, be[i], 0, 0)),
                  pl.BlockSpec((1, 1, EXPERT_FF, D_MODEL), lambda i, be, nv: (li, be[i], 0, 0))],
        out_specs=[rows, rows],
        scratch_shapes=[pltpu.VMEM((D_MODEL, EXPERT_FF), BF16), pltpu.VMEM((D_MODEL, EXPERT_FF), BF16),
                        pltpu.VMEM((EXPERT_FF, D_MODEL), BF16)])
    out = jax.ShapeDtypeStruct((n_rows, ROW_WORDS), jnp.uint32)
    return pl.pallas_call(
        _expert_kernel, grid_spec=grid_spec, out_shape=[out, out],
        compiler_params=_params(("arbitrary",)), name="moe_experts",
    )(block_e, n_valid, xlo, xhi, wg, wu, wd)


def _combine_kernel(ylo_ref, yhi_ref, w_ref, x_ref, sh_ref, sc_ref, g_ref, sg_ref, su_ref, sd_ref, o_ref):
    hb = _norm_mod(x_ref[0], sh_ref[0], sc_ref[0]).astype(BF16)
    hid = jax.nn.silu(jnp.dot(hb, sg_ref[...].astype(BF16), preferred_element_type=F32))
    hid = hid * jnp.dot(hb, su_ref[...].astype(BF16), preferred_element_type=F32)
    shared = jnp.dot(hid.astype(BF16), sd_ref[...].astype(BF16), preferred_element_type=F32)
    w = w_ref[0]
    acc = [None] * 4
    for k in range(TOP_K):
        wk = w[:, k:k + 1]
        for c, part in enumerate(_unpack_rows(ylo_ref[k], yhi_ref[k])):
            acc[c] = wk * part if acc[c] is None else acc[c] + wk * part
    for c in range(4):
        sl = slice(c * ROW_WORDS, (c + 1) * ROW_WORDS)
        o_ref[0, :, sl] = x_ref[0, :, sl] + g_ref[0, :, sl] * (acc[c] + shared[:, sl])


def _combine_call(ylo, yhi, wts, x, sh, sc, g2, sg, su, sd):
    t3 = lambda b, j: (b, j, 0)
    per_b = lambda b, j: (b, 0, 0)
    full2 = lambda b, j: (0, 0)
    ff = sg.shape[1]
    nt = SEQ // MOE_TL
    rows = pl.BlockSpec((TOP_K, MOE_TL, ROW_WORDS), lambda b, j: (0, b * nt + j, 0))
    return pl.pallas_call(
        _combine_kernel, grid=(BATCH, nt),
        in_specs=[rows, rows,
                  pl.BlockSpec((1, MOE_TL, SLOT_PAD), t3),
                  pl.BlockSpec((1, MOE_TL, D_MODEL), t3),
                  pl.BlockSpec((1, 1, D_MODEL), per_b),
                  pl.BlockSpec((1, 1, D_MODEL), per_b),
                  pl.BlockSpec((1, 1, D_MODEL), per_b),
                  pl.BlockSpec((D_MODEL, ff), full2),
                  pl.BlockSpec((D_MODEL, ff), full2),
                  pl.BlockSpec((ff, D_MODEL), full2)],
        out_specs=pl.BlockSpec((1, MOE_TL, D_MODEL), t3),
        out_shape=jax.ShapeDtypeStruct((BATCH, SEQ, D_MODEL), F32),
        compiler_params=_params(("parallel", "arbitrary")), name="moe_combine_shared",
    )(ylo, yhi, wts, x, sh, sc, g2, sg, su, sd)


def _moe(x, routed, sh, sc, g2, w_gate, w_up, w_down, sh_gate, sh_up, sh_down, li):
    T = BATCH * SEQ
    TK = T * TOP_K
    hlo, hhi, idx, wts, rank, counts = routed
    wts = wts.T.reshape(BATCH, SEQ, SLOT_PAD)
    counts = counts[:, 0].astype(jnp.int32)
    padded = (counts + MOE_BLOCK - 1) // MOE_BLOCK * MOE_BLOCK
    pad_end = jnp.cumsum(padded)
    pad_start = pad_end - padded
    n_blocks = -(-TK // MOE_BLOCK) + N_EXPERTS
    n_rows = n_blocks * MOE_BLOCK
    block_start = jnp.arange(n_blocks, dtype=jnp.int32) * MOE_BLOCK
    owns = jnp.logical_and(block_start[:, None] >= pad_start[None, :], block_start[:, None] < pad_end[None, :])
    owns = owns.astype(jnp.int32)
    experts = jnp.arange(N_EXPERTS, dtype=jnp.int32)[None, :]
    block_e = jnp.sum(owns * experts, axis=1) + (N_EXPERTS - 1) * (1 - jnp.sum(owns, axis=1))
    n_valid = jnp.sum(owns * (counts[None, :] - (block_start[:, None] - pad_start[None, :])), axis=1)
    n_valid = jnp.clip(n_valid, 0, MOE_BLOCK).astype(jnp.int32)
    first_row = jnp.sum(jnp.where(idx[None] == experts.T[:, :, None], pad_start[:, None, None], 0), axis=0)
    dest = first_row + rank
    xlo = _sc_dispatch(hlo.reshape(T, ROW_WORDS), dest, n_rows)
    xhi = _sc_dispatch(hhi.reshape(T, ROW_WORDS), dest, n_rows)
    ylo, yhi = _expert_call(block_e, n_valid, xlo, xhi, w_gate, w_up, w_down, li)
    return _combine_call(_sc_collect(ylo, dest), _sc_collect(yhi, dest), wts, x, sh, sc, g2,
                         sh_gate, sh_up, sh_down)


HY_TL = MOE_TL
HALO = 8


def _hy_in_kernel(x_ref, xp_ref, xn_ref, sh_ref, sc_ref, w_ref, cw_ref, cb_ref, z_ref, x0_ref, h_sc):
    j = pl.program_id(1)
    shift, scale = sh_ref[0], sc_ref[0]
    keep_prev = (j > 0).astype(F32)
    keep_next = (j < SEQ // HY_TL - 1).astype(F32)
    h_sc[0:HALO, :] = _norm_mod(xp_ref[0], shift, scale) * keep_prev
    h_sc[HALO:HALO + HY_TL, :] = _norm_mod(x_ref[0], shift, scale)
    h_sc[HALO + HY_TL:, :] = _norm_mod(xn_ref[0], shift, scale) * keep_next
    hcat = h_sc[...].astype(BF16)
    outs = []
    for part in range(3):
        sl = slice(part * HY_WIDTH, (part + 1) * HY_WIDTH)
        p = jnp.dot(hcat, w_ref[:, sl], preferred_element_type=F32)
        o = (p[HALO - 1:HALO - 1 + HY_TL] * cw_ref[0:1, sl] + p[HALO:HALO + HY_TL] * cw_ref[1:2, sl]
             + p[HALO + 1:HALO + 1 + HY_TL] * cw_ref[2:3, sl] + cb_ref[:, sl])
        outs.append(o)
    x0_ref[0] = outs[0].astype(BF16)
    z_ref[0] = (outs[2] * outs[1]).astype(BF16)


def _hy_in_call(x, sh, sc, w, cw, cb):
    nb8 = HY_TL // HALO
    t3 = lambda b, j: (b, j, 0)
    full2 = lambda b, j: (0, 0)
    per_b = lambda b, j: (b, 0, 0)
    return pl.pallas_call(
        _hy_in_kernel, grid=(BATCH, SEQ // HY_TL),
        in_specs=[pl.BlockSpec((1, HY_TL, D_MODEL), t3),
                  pl.BlockSpec((1, HALO, D_MODEL), lambda b, j: (b, jnp.maximum(j * nb8 - 1, 0), 0)),
                  pl.BlockSpec((1, HALO, D_MODEL), lambda b, j: (b, jnp.minimum((j + 1) * nb8, SEQ // HALO - 1), 0)),
                  pl.BlockSpec((1, 1, D_MODEL), per_b),
                  pl.BlockSpec((1, 1, D_MODEL), per_b),
                  pl.BlockSpec((D_MODEL, 3 * HY_WIDTH), full2),
                  pl.BlockSpec((SHORT_CONV, 3 * HY_WIDTH), full2),
                  pl.BlockSpec((1, 3 * HY_WIDTH), full2)],
        out_specs=[pl.BlockSpec((1, HY_TL, HY_WIDTH), t3), pl.BlockSpec((1, HY_TL, HY_WIDTH), t3)],
        out_shape=[jax.ShapeDtypeStruct((BATCH, SEQ, HY_WIDTH), BF16),
                   jax.ShapeDtypeStruct((BATCH, SEQ, HY_WIDTH), BF16)],
        scratch_shapes=[pltpu.VMEM((HY_TL + 2 * HALO, D_MODEL), F32)],
        compiler_params=_params(("parallel", "arbitrary")), name="hyena_in_proj",
    )(x, x, x, sh, sc, w, cw, cb)


def _fft_tables():
    c = np.arange(FFT_N2, dtype=np.int64)
    ang = 2.0 * np.pi * ((c[:, None] * c[None, :]) % FFT_N2) / FFT_N2
    sr, si = np.cos(ang), -np.sin(ang)
    m = np.block([[sr, -si], [si, sr]])
    k1 = np.arange(FFT_NK, dtype=np.int64)
    ang_t = 2.0 * np.pi * (k1[:, None] * c[None, :]) / DFT_N
    lanes = np.ones((1, 1, V7X_LANES))
    tr = np.cos(ang_t)[:, :, None] * lanes
    ti = -np.sin(ang_t)[:, :, None] * lanes
    return jnp.asarray(m, BF16), jnp.asarray(tr, F32), jnp.asarray(ti, F32)


def _lin(acc, coef, val):
    if abs(coef) < 1e-12:
        return acc
    term = val if coef == 1.0 else (-val if coef == -1.0 else coef * val)
    return term if acc is None else acc + term


def _twiddle(tr_ref, ti_ref, k1, width):
    reps = width // V7X_LANES
    tr, ti = tr_ref[k1], ti_ref[k1]
    return jnp.concatenate([tr] * reps, axis=1), jnp.concatenate([ti] * reps, axis=1)


_COS_PI_4 = math.sqrt(0.5)


def _blocks_to_classes(z):
    assert FFT_N1 == 16
    sp = [z[a] + z[a + 4] for a in range(4)]
    sm = [z[a] - z[a + 4] for a in range(4)]
    q0p, q0m, q1p, q1m = sp[0] + sp[2], sp[0] - sp[2], sp[1] + sp[3], sp[1] - sp[3]
    d, e = _COS_PI_4 * (sm[1] - sm[3]), _COS_PI_4 * (sm[1] + sm[3])
    out = [None] * FFT_NK
    out[0] = (q0p + q1p, None)
    out[8] = (q0p - q1p, None)
    out[4] = (q0m, -q1m)
    out[2] = (sm[0] + d, -(sm[2] + e))
    out[6] = (sm[0] - d, sm[2] - e)
    for k in (1, 3):
        halves = []
        for parity in (0, 1):
            hr = hi = None
            for a in range(parity, FFT_NA, 2):
                th = 2.0 * math.pi * ((a * k) % FFT_N1) / FFT_N1
                hr = _lin(hr, round(math.cos(th), 15), z[a])
                hi = _lin(hi, round(-math.sin(th), 15), z[a])
            halves.append((hr, hi))
        (er, ei), (odr, odi) = halves
        out[k] = (er + odr, ei + odi)
        out[8 - k] = (er - odr, odi - ei)
    return out


def _classes_to_blocks(v):
    assert FFT_N1 == 16
    base = (v[0][0] + v[8][0], v[0][0] - v[8][0])
    pr = {k: (v[k][0] + v[8 - k][0], v[k][0] - v[8 - k][0]) for k in (1, 2, 3)}
    pi = {k: (v[k][1] - v[8 - k][1], v[k][1] + v[8 - k][1]) for k in (1, 2, 3)}
    out = []
    for a in range(FFT_NA):
        odd = a % 2
        mid = (v[4][0], 1.0 - (a % 4)) if not odd else (v[4][1], (a % 4) - 2.0)
        acc = _lin(base[odd], mid[1], mid[0])
        for k in (1, 2, 3):
            th = 2.0 * math.pi * ((a * k) % FFT_N1) / FFT_N1
            acc = _lin(acc, round(math.cos(th), 15), pr[k][odd])
            acc = _lin(acc, round(-math.sin(th), 15), pi[k][odd])
        out.append(acc)
    return out


def _class_forward(yr, yi, k1, m_ref, tr_ref, ti_ref, width):
    if k1 == 0:
        x = jnp.dot(m_ref[:, :FFT_N2], yr.astype(BF16), preferred_element_type=F32)
    else:
        tr, ti = _twiddle(tr_ref, ti_ref, k1, width)
        yr, yi = (yr * tr, yr * ti) if yi is None else (yr * tr - yi * ti, yr * ti + yi * tr)
        x = jnp.dot(m_ref[...], jnp.concatenate([yr, yi], axis=0).astype(BF16), preferred_element_type=F32)
    return x[:FFT_N2], x[FFT_N2:]


def _class_backward(pr, pi, k1, m_ref, tr_ref, ti_ref, width):
    v = jnp.dot(m_ref[...], jnp.concatenate([pr, -pi], axis=0).astype(BF16), preferred_element_type=F32)
    ur, ui = v[:FFT_N2], -v[FFT_N2:]
    if k1 > 0:
        tr, ti = _twiddle(tr_ref, ti_ref, k1, width)
        ur, ui = ur * tr + ui * ti, ui * tr - ur * ti
    return ur, ui


def _filter_features():
    t = np.linspace(0.0, 1.0, SEQ)[:, None]
    ang = 2.0 * np.pi * np.arange(SEQ)[:, None] / SEQ
    bands = np.linspace(1e-4, FILT_BANDS - 1, FILT_BANDS)
    z = np.concatenate([t, np.cos(bands * ang), -np.sin(bands * ang)], axis=-1)
    return jnp.asarray(np.pad(z, ((0, 0), (0, FILT_PAD - FILT_EMB))), F32)


def _spec_kernel(zf_ref, w1_ref, b1_ref, w2_ref, b2_ref, fq_ref, w3f_ref, w3b_ref, dl_ref, m_ref, tr_ref, ti_ref,
                 c_ref, hf_sc, hb_sc):
    hi = lax.Precision.HIGHEST
    zf, fq = zf_ref[...], fq_ref[...]
    hid = jnp.sin(fq * (jnp.dot(zf, w1_ref[...], preferred_element_type=F32, precision=hi) + b1_ref[...]))
    hid = jnp.sin(fq * (jnp.dot(hid, w2_ref[...], preferred_element_type=F32, precision=hi) + b2_ref[...]))
    decay = jnp.exp(-zf[:, 0:1] * dl_ref[...])
    hf = jnp.dot(hid, w3f_ref[...], preferred_element_type=F32, precision=hi) * decay
    hb = jnp.dot(hid, w3b_ref[...], preferred_element_type=F32, precision=hi) * decay
    ssq = jnp.sum(hf * hf, axis=0, keepdims=True) + jnp.sum(hb * hb, axis=0, keepdims=True)
    inv = lax.rsqrt(ssq + EPS)
    hf_sc[...] = hf * inv
    hb_sc[...] = hb * inv
    rows = lambda ref: [ref[a * FFT_N2:(a + 1) * FFT_N2, :] for a in range(FFT_NA)]
    fwd, bwd = _blocks_to_classes(rows(hf_sc)), _blocks_to_classes(rows(hb_sc))
    for k1 in range(FFT_NK):
        fr, fi = _class_forward(*fwd[k1], k1, m_ref, tr_ref, ti_ref, HY_CT)
        br, bi = _class_forward(*bwd[k1], k1, m_ref, tr_ref, ti_ref, HY_CT)
        scale = (1.0 if k1 in (0, FFT_N1 // 2) else 2.0) / DFT_N
        c_ref[k1, :FFT_N2, :] = ((fr + br) * scale).astype(BF16)
        c_ref[k1, FFT_N2:, :] = ((fi - bi) * scale).astype(BF16)


def _fft_table_specs(ngrid):
    z = (0,) * 2
    z3 = (0,) * 3
    if ngrid == 1:
        return [pl.BlockSpec((2 * FFT_N2, 2 * FFT_N2), lambda c: z),
                pl.BlockSpec((FFT_NK, FFT_N2, V7X_LANES), lambda c: z3),
                pl.BlockSpec((FFT_NK, FFT_N2, V7X_LANES), lambda c: z3)]
    return [pl.BlockSpec((2 * FFT_N2, 2 * FFT_N2), lambda b, c: z),
            pl.BlockSpec((FFT_NK, FFT_N2, V7X_LANES), lambda b, c: z3),
            pl.BlockSpec((FFT_NK, FFT_N2, V7X_LANES), lambda b, c: z3)]


def _spec_call(w1, b1, w2, b2, w3, freq, m, tr, ti):
    nct = HY_WIDTH // HY_CT

    def pad2(a, rows, cols):
        return jnp.pad(a, ((0, rows - a.shape[0]), (0, cols - a.shape[1])))

    full = lambda c: (0, 0)
    small = pl.BlockSpec((FILT_PAD, FILT_PAD), full)
    vec = pl.BlockSpec((1, FILT_PAD), full)
    deltas = jnp.asarray(np.linspace(HY_MIN_DECAY, HY_MAX_DECAY, HY_WIDTH)[None, :], F32)
    w3p = pad2(w3, FILT_PAD, 2 * HY_WIDTH)
    return pl.pallas_call(
        _spec_kernel, grid=(nct,),
        in_specs=[pl.BlockSpec((SEQ, FILT_PAD), full), small, vec, small, vec, vec,
                  pl.BlockSpec((FILT_PAD, HY_CT), lambda c: (0, c)),
                  pl.BlockSpec((FILT_PAD, HY_CT), lambda c: (0, c + nct)),
                  pl.BlockSpec((1, HY_CT), lambda c: (0, c))] + _fft_table_specs(1),
        out_specs=pl.BlockSpec((FFT_NK, 2 * FFT_N2, HY_CT), lambda c: (0, 0, c)),
        out_shape=jax.ShapeDtypeStruct((FFT_NK, 2 * FFT_N2, HY_WIDTH), BF16),
        scratch_shapes=[pltpu.VMEM((SEQ, HY_CT), F32), pltpu.VMEM((SEQ, HY_CT), F32)],
        compiler_params=_params(("arbitrary",)), name="hyena_filter_spectrum",
    )(_filter_features(), pad2(w1, FILT_PAD, FILT_PAD), pad2(b1[None, :], 1, FILT_PAD),
      pad2(w2, FILT_PAD, FILT_PAD), pad2(b2[None, :], 1, FILT_PAD), pad2(freq[None, :], 1, FILT_PAD),
      w3p, w3p, deltas, m, tr, ti)


def _conv_kernel(z_ref, c_ref, m_ref, tr_ref, ti_ref, y_ref, cls):
    for k1, (yr, yi) in enumerate(_blocks_to_classes(
            [z_ref[0, a * FFT_N2:(a + 1) * FFT_N2, :].astype(F32) for a in range(FFT_NA)])):
        cls[k1, :FFT_N2, :] = yr
        if yi is not None:
            cls[k1, FFT_N2:, :] = yi
    for k1 in range(FFT_NK):
        yi = None if k1 in (0, FFT_N1 // 2) else cls[k1, FFT_N2:, :]
        xr, xi = _class_forward(cls[k1, :FFT_N2, :], yi, k1, m_ref, tr_ref, ti_ref, HY_CT)
        cr = c_ref[k1, :FFT_N2, :].astype(F32)
        ci = c_ref[k1, FFT_N2:, :].astype(F32)
        ur, ui = _class_backward(xr * cr - xi * ci, xr * ci + xi * cr, k1, m_ref, tr_ref, ti_ref, HY_CT)
        cls[k1, :FFT_N2, :] = ur
        cls[k1, FFT_N2:, :] = ui
    blocks = _classes_to_blocks([(cls[k1, :FFT_N2, :], cls[k1, FFT_N2:, :]) for k1 in range(FFT_NK)])
    for a, ya in enumerate(blocks):
        y_ref[0, a * FFT_N2:(a + 1) * FFT_N2, :] = ya.astype(BF16)


def _conv_call(z, spec, m, tr, ti):
    return pl.pallas_call(
        _conv_kernel, grid=(HY_WIDTH // HY_CT, BATCH),
        in_specs=[pl.BlockSpec((1, SEQ, HY_CT), lambda c, b: (b, 0, c)),
                  pl.BlockSpec((FFT_NK, 2 * FFT_N2, HY_CT), lambda c, b: (0, 0, c))] + _fft_table_specs(2),
        out_specs=pl.BlockSpec((1, SEQ, HY_CT), lambda c, b: (b, 0, c)),
        out_shape=jax.ShapeDtypeStruct((BATCH, SEQ, HY_WIDTH), BF16),
        scratch_shapes=[pltpu.VMEM((FFT_NK, 2 * FFT_N2, HY_CT), F32)],
        compiler_params=_params(("parallel", "arbitrary")), name="hyena_long_conv",
    )(z, spec, m, tr, ti)


def _hy_out_kernel(y_ref, z_ref, x0_ref, b_ref, w_ref, x_ref, g_ref, *route):
    o_ref = route[5]
    z = z_ref[0].astype(F32)
    gated = x0_ref[0].astype(F32) * (y_ref[0].astype(F32) + b_ref[...] * z)
    mix = jnp.dot(gated.astype(BF16), w_ref[...], preferred_element_type=F32)
    x_new = x_ref[0] + g_ref[0] * mix
    o_ref[0] = x_new
    _route_tile(x_new, *route[:5], *route[6:])


def _hy_out_call(y, z, x0, bias, w, x, g1, route):
    r_args, r_in, r_out, r_shape, r_scratch = route
    t3 = lambda b, j: (b, j, 0)
    full2 = lambda b, j: (0, 0)
    return pl.pallas_call(
        _hy_out_kernel, grid=(BATCH, SEQ // HY_TL),
        in_specs=[pl.BlockSpec((1, HY_TL, HY_WIDTH), t3),
                  pl.BlockSpec((1, HY_TL, HY_WIDTH), t3),
                  pl.BlockSpec((1, HY_TL, HY_WIDTH), t3),
                  pl.BlockSpec((1, HY_WIDTH), full2),
                  pl.BlockSpec((HY_WIDTH, D_MODEL), full2),
                  pl.BlockSpec((1, HY_TL, D_MODEL), t3),
                  pl.BlockSpec((1, 1, D_MODEL), lambda b, j: (b, 0, 0))] + r_in,
        out_specs=[pl.BlockSpec((1, HY_TL, D_MODEL), t3)] + r_out,
        out_shape=[jax.ShapeDtypeStruct((BATCH, SEQ, D_MODEL), F32)] + r_shape,
        scratch_shapes=r_scratch,
        compiler_params=_params(("arbitrary", "arbitrary")), name="hyena_out_proj_route",
    )(y, z, x0, bias, w, x, g1, *r_args)


def kernel(x, c, ctx, c_ctx, ada_w, ada_b, ev_w_in, mla_q_norm, mla_w_uq, mla_kv_norm, mla_w_ukv, mla_q_qknorm, mla_k_qknorm, s5_lam_re, s5_lam_im, s5_log_step, s5_b_re, s5_b_im, s5_c_re, s5_c_im, s5_d, s5_glu_w, s5_glu_b, ev_w_out, hy_w_in, hy_conv_w, hy_conv_b, hy_f_w1, hy_f_b1, hy_f_w2, hy_f_b2, hy_f_w3, hy_f_freq, hy_bias, hy_w_out, moe_router_w, moe_router_b, moe_w_gate, moe_w_up, moe_w_down, moe_sh_gate, moe_sh_up, moe_sh_down):
    D = D_MODEL
    mod_all = _ada_call(c, c_ctx, ada_w, ada_b)

    def mods(li):
        return [m[:, None, :] for m in jnp.split(mod_all[li, :BATCH], 6, axis=-1)]

    sh1, sc1, g1, sh2, sc2, g2 = mods(0)
    mod_ctx = mod_all[0, BATCH, :2 * D]
    w0 = _a0_weights(ev_w_in[0], mla_q_norm[0], mla_w_uq[0], mla_kv_norm[0], mla_w_ukv[0],
                     mla_q_qknorm[0], mla_k_qknorm[0])
    q, k, v, u = _a0_call(x, ctx, sh1, sc1, mod_ctx[None, :D], mod_ctx[None, D:], w0)
    att = _attn_call(q, k, v)
    ug = u.reshape(BATCH, S5_NCHUNK, S5_CHUNK, S5_GROUPS, S5_GROUP)
    ug = ug.transpose(3, 1, 0, 2, 4).reshape(S5_GROUPS, S5_ROWS, S5_COLS)
    ys = _s5_call(ug, *_s5_weights(s5_lam_re[0], s5_lam_im[0], s5_log_step[0], s5_b_re[0], s5_b_im[0],
                                   s5_c_re[0], s5_c_im[0]))
    ys = ys.reshape(S5_GROUPS, S5_NCHUNK_LAT, BATCH, S5_CHUNK, S5_GROUP)
    ys = ys.transpose(2, 1, 3, 0, 4).reshape(BATCH, SEQ, S5_WIDTH)
    wo = ev_w_out[0].astype(BF16)
    wo_att = jnp.concatenate([wo[:MLA_WIDTH].reshape(MLA_HEADS, V_HEAD, D_MODEL),
                              jnp.zeros((MLA_HEADS, HEAD_PAD - V_HEAD, D_MODEL), BF16)], axis=1)
    x, *routed = _a1_call(att, ys, u, s5_d[0][None, :], s5_glu_w[0].astype(BF16), s5_glu_b[0][None, :],
                          wo_att.reshape(MLA_HEADS * HEAD_PAD, D_MODEL), wo[MLA_WIDTH:], x, g1,
                          _route_plumbing(sh2, sc2, moe_router_w[0], moe_router_b[0]))
    x = _moe(x, routed, sh2, sc2, g2, moe_w_gate, moe_w_up, moe_w_down,
             moe_sh_gate[0], moe_sh_up[0], moe_sh_down[0], 0)

    sh1, sc1, g1, sh2, sc2, g2 = mods(1)
    z, x0 = _hy_in_call(x, sh1, sc1, hy_w_in[0].astype(BF16), hy_conv_w[0], hy_conv_b[0][None, :])
    fft_tabs = _fft_tables()
    spectrum = _spec_call(hy_f_w1[0], hy_f_b1[0], hy_f_w2[0], hy_f_b2[0], hy_f_w3[0], hy_f_freq[0], *fft_tabs)
    y = _conv_call(z, spectrum, *fft_tabs)
    x, *routed = _hy_out_call(y, z, x0, hy_bias[0][None, :], hy_w_out[0].astype(BF16), x, g1,
                              _route_plumbing(sh2, sc2, moe_router_w[1], moe_router_b[1]))
    x = _moe(x, routed, sh2, sc2, g2, moe_w_gate, moe_w_up, moe_w_down,
             moe_sh_gate[1], moe_sh_up[1], moe_sh_down[1], 1)
    return x
```

```python
import math

import numpy as np
import jax
import jax.numpy as jnp
from jax import lax
from jax.experimental import pallas as pl
from jax.experimental.pallas import tpu as pltpu
from jax.experimental.pallas import tpu_sc as plsc

F32 = jnp.float32
BF16 = jnp.bfloat16

D_MODEL = 1024
BATCH = 8
SEQ = 4096
CTX_LEN = 256
KV_LEN = SEQ + CTX_LEN
GRID_W = 64
EPS = 1e-6

MLA_HEADS = 8
QK_NOPE = 64
QK_ROPE = 32
QK_HEAD = QK_NOPE + QK_ROPE
V_HEAD = 64
Q_LORA = 256
KV_LORA = 128
MLA_WIDTH = MLA_HEADS * V_HEAD
ROPE_BASE = 10000.0
HEAD_PAD = 128

S5_WIDTH = 512
S5_GROUP = 16
S5_GROUPS = S5_WIDTH // S5_GROUP
S5_STATE = 64
S5_CHUNK = 32
S5_NCHUNK = KV_LEN // S5_CHUNK
S5_NCHUNK_LAT = SEQ // S5_CHUNK
S5_NCHUNK_CTX = CTX_LEN // S5_CHUNK

HY_WIDTH = D_MODEL
FILT_EMB = 33
FILT_BANDS = (FILT_EMB - 1) // 2
FILT_PAD = 128
SHORT_CONV = 3
HY_MIN_DECAY = -math.log(1e-2) / 1.5
HY_MAX_DECAY = -math.log(1e-2) / 0.3
DFT_N = 2 * SEQ
FFT_N1 = 16
FFT_N2 = DFT_N // FFT_N1
FFT_NA = FFT_N1 // 2
FFT_NK = FFT_N1 // 2 + 1
HY_CT = 256

N_EXPERTS = 64
TOP_K = 6
EXPERT_FF = 256
ROUTE_SCALE = 2.5
MOE_BLOCK = 1024
MOE_TL = 512
ROW_WORDS = D_MODEL // 4
SC_WINDOW = 128
COLLECT_PARTS = 2

V7X_LANES = 128
V7X_VMEM_BYTES = 64 * 1024 * 1024
VMEM_LIMIT = V7X_VMEM_BYTES - 8 * 1024 * 1024


def _params(semantics):
    return pltpu.CompilerParams(dimension_semantics=semantics, vmem_limit_bytes=VMEM_LIMIT)


def _norm_mod(x, shift, scale):
    ms = jnp.mean(x * x, axis=-1, keepdims=True)
    return x * lax.rsqrt(ms + EPS) * (1.0 + scale) + shift


def _rms(x, gain, n):
    ms = jnp.sum(x * x, axis=-1, keepdims=True) * (1.0 / n)
    return x * lax.rsqrt(ms + EPS) * gain


ADA_ROWS = 16
ADA_TN = 1024


def _ada_kernel(c_ref, w_ref, b_ref, o_ref):
    sc = jax.nn.silu(c_ref[...])
    o_ref[0] = jnp.dot(sc, w_ref[0], preferred_element_type=F32, precision=lax.Precision.HIGHEST) + b_ref[0]


def _ada_call(c, c_ctx, ada_w, ada_b):
    depth, _, width = ada_w.shape
    rows = jnp.concatenate([c, c_ctx[None, :], jnp.zeros((ADA_ROWS - BATCH - 1, D_MODEL), F32)], axis=0)
    return pl.pallas_call(
        _ada_kernel, grid=(depth, width // ADA_TN),
        in_specs=[pl.BlockSpec((ADA_ROWS, D_MODEL), lambda l, n: (0, 0)),
                  pl.BlockSpec((1, D_MODEL, ADA_TN), lambda l, n: (l, 0, n)),
                  pl.BlockSpec((1, 1, ADA_TN), lambda l, n: (l, 0, n))],
        out_specs=pl.BlockSpec((1, ADA_ROWS, ADA_TN), lambda l, n: (l, 0, n)),
        out_shape=jax.ShapeDtypeStruct((depth, ADA_ROWS, width), F32),
        compiler_params=_params(("parallel", "arbitrary")), name="adaln_modulation",
    )(rows, ada_w, ada_b[:, None, :])


A0_TL = 256
A0_NT = SEQ // A0_TL


def _rope_perm():
    return np.concatenate([np.arange(0, QK_ROPE, 2), np.arange(1, QK_ROPE, 2)])


def _rope_tables():
    t = np.arange(SEQ)
    row = (t // GRID_W).astype(np.float64)
    col = (t % GRID_W).astype(np.float64)
    n_freq = QK_ROPE // 4
    inv = ROPE_BASE ** (-np.arange(n_freq, dtype=np.float64) / n_freq)
    ang = np.concatenate([row[:, None] * inv, col[:, None] * inv], axis=-1)
    cos, sin = np.cos(ang), np.sin(ang)
    half = QK_ROPE // 2
    a = np.zeros((KV_LEN, HEAD_PAD))
    b = np.zeros((KV_LEN, HEAD_PAD))
    a[:, :QK_HEAD] = 1.0
    a[:SEQ, QK_NOPE:QK_NOPE + half] = cos
    a[:SEQ, QK_NOPE + half:QK_HEAD] = cos
    b[:SEQ, QK_NOPE:QK_NOPE + half] = -sin
    b[:SEQ, QK_NOPE + half:QK_HEAD] = sin
    return a, b


def _norm_rope_heads(f, gain_ref, a, b, out_ref):
    width = MLA_HEADS * HEAD_PAD
    for hd in range(MLA_HEADS):
        sl = slice(hd * HEAD_PAD, (hd + 1) * HEAD_PAD)
        x = f[:, sl]
        r = lax.rsqrt(jnp.sum(x * x, axis=-1, keepdims=True) * (1.0 / QK_HEAD) + EPS)
        rot = x * (a * gain_ref[:, sl]) + f[:, width + hd * HEAD_PAD:width + (hd + 1) * HEAD_PAD] * b
        out_ref[0, :, sl] = (rot * r).astype(BF16)


def _a0_kernel(x_ref, ctx_ref, sh_ref, sc_ref, shc_ref, scc_ref, win_ref, qn_ref, wuq_ref, kvn_ref,
               wk_ref, wuv_ref, qg_ref, kg_ref, ka_ref, kb_ref, qa_ref, qb_ref,
               q_ref, k_ref, v_ref, u_ref):
    j = pl.program_id(1)
    is_ctx = j == A0_NT
    xin = jnp.where(is_ctx, ctx_ref[0], x_ref[0])
    shift = jnp.where(is_ctx, shc_ref[...], sh_ref[0])
    scale = jnp.where(is_ctx, scc_ref[...], sc_ref[0])
    h = _norm_mod(xin, shift, scale).astype(BF16)
    proj = jnp.dot(h, win_ref[...], preferred_element_type=F32)
    u_ref[0] = proj[:, 512:].astype(BF16)

    c_kv = _rms(proj[:, Q_LORA:Q_LORA + KV_LORA], kvn_ref[...], KV_LORA).astype(BF16)
    lane = lax.broadcasted_iota(jnp.int32, (1, MLA_HEADS * HEAD_PAD), 1)
    ones_lane = (lane % HEAD_PAD == V_HEAD).astype(F32)
    v_ref[0] = (jnp.dot(c_kv, wuv_ref[...], preferred_element_type=F32) + ones_lane).astype(BF16)
    kin = jnp.concatenate([c_kv, proj[:, 384:512].astype(BF16)], axis=1)
    kf = jnp.dot(kin, wk_ref[...], preferred_element_type=F32)
    _norm_rope_heads(kf, kg_ref, ka_ref[...], kb_ref[...], k_ref)

    @pl.when(j < A0_NT)
    def _():
        ql = _rms(proj[:, :Q_LORA], qn_ref[...], Q_LORA).astype(BF16)
        qf = jnp.dot(ql, wuq_ref[...], preferred_element_type=F32)
        _norm_rope_heads(qf, qg_ref, qa_ref[...], qb_ref[...], q_ref)


def _a0_call(x, ctx, sh, sc, shc, scc, w):
    nt = A0_NT
    lat = lambda b, j: (b, jnp.minimum(j, nt - 1), 0)
    full2 = lambda b, j: (0, 0)
    per_b = lambda b, j: (b, 0, 0)
    tab = pl.BlockSpec((A0_TL, HEAD_PAD), lambda b, j: (j, 0))
    in_specs = [
        pl.BlockSpec((1, A0_TL, D_MODEL), lat),
        pl.BlockSpec((1, CTX_LEN, D_MODEL), per_b),
        pl.BlockSpec((1, 1, D_MODEL), per_b),
        pl.BlockSpec((1, 1, D_MODEL), per_b),
        pl.BlockSpec((1, D_MODEL), full2),
        pl.BlockSpec((1, D_MODEL), full2),
        pl.BlockSpec((D_MODEL, 1024), full2),
        pl.BlockSpec((1, Q_LORA), full2),
        pl.BlockSpec((Q_LORA, 2 * MLA_HEADS * HEAD_PAD), full2),
        pl.BlockSpec((1, KV_LORA), full2),
        pl.BlockSpec((2 * KV_LORA, 2 * MLA_HEADS * HEAD_PAD), full2),
        pl.BlockSpec((KV_LORA, MLA_HEADS * HEAD_PAD), full2),
        pl.BlockSpec((1, MLA_HEADS * HEAD_PAD), full2),
        pl.BlockSpec((1, MLA_HEADS * HEAD_PAD), full2),
        tab, tab, tab, tab,
    ]
    out_specs = [
        pl.BlockSpec((1, A0_TL, MLA_HEADS * HEAD_PAD), lat),
        pl.BlockSpec((1, A0_TL, MLA_HEADS * HEAD_PAD), lambda b, j: (b, j, 0)),
        pl.BlockSpec((1, A0_TL, MLA_HEADS * HEAD_PAD), lambda b, j: (b, j, 0)),
        pl.BlockSpec((1, A0_TL, S5_WIDTH), lambda b, j: (b, j, 0)),
    ]
    out_shape = [
        jax.ShapeDtypeStruct((BATCH, SEQ, MLA_HEADS * HEAD_PAD), BF16),
        jax.ShapeDtypeStruct((BATCH, KV_LEN, MLA_HEADS * HEAD_PAD), BF16),
        jax.ShapeDtypeStruct((BATCH, KV_LEN, MLA_HEADS * HEAD_PAD), BF16),
        jax.ShapeDtypeStruct((BATCH, KV_LEN, S5_WIDTH), BF16),
    ]
    return pl.pallas_call(
        _a0_kernel, grid=(BATCH, nt + 1), in_specs=in_specs, out_specs=out_specs, out_shape=out_shape,
        compiler_params=_params(("parallel", "arbitrary")), name="even_in_proj",
    )(x, ctx, sh, sc, shc, scc, *w)


def _gain_swap(w, gain):
    half = QK_ROPE // 2
    wg = (w * gain).reshape(w.shape[0], MLA_HEADS, HEAD_PAD)
    re, im = wg[..., QK_NOPE:QK_NOPE + half], wg[..., QK_NOPE + half:QK_HEAD]
    out = jnp.concatenate([jnp.zeros_like(wg[..., :QK_NOPE]), im, re, jnp.zeros_like(wg[..., QK_HEAD:])], axis=-1)
    return out.reshape(w.shape)


def _a0_weights(w_in, q_norm, w_uq, kv_norm, w_ukv, q_qk, k_qk):
    perm = _rope_perm()
    kr0 = Q_LORA + KV_LORA
    w_cat = jnp.concatenate([
        w_in[:, :kr0], w_in[:, kr0:kr0 + QK_ROPE][:, perm],
        jnp.zeros((D_MODEL, HEAD_PAD - QK_ROPE), F32), w_in[:, kr0 + QK_ROPE:]], axis=1).astype(BF16)
    pad = HEAD_PAD - QK_HEAD

    def head_gain(g):
        gh = jnp.concatenate([g[:QK_NOPE], g[QK_NOPE:][perm], jnp.zeros((pad,), F32)])
        return jnp.tile(gh, MLA_HEADS)[None, :]

    uq = w_uq.reshape(Q_LORA, MLA_HEADS, QK_HEAD)
    uq = jnp.concatenate([uq[..., :QK_NOPE], uq[..., QK_NOPE:][..., perm],
                          jnp.zeros((Q_LORA, MLA_HEADS, pad), F32)], axis=-1)
    uq = uq.reshape(Q_LORA, MLA_HEADS * HEAD_PAD)
    uq = jnp.concatenate([uq, _gain_swap(uq, head_gain(q_qk))], axis=1).astype(BF16)
    ukv = w_ukv.reshape(KV_LORA, MLA_HEADS, QK_NOPE + V_HEAD)
    uk = jnp.concatenate([ukv[..., :QK_NOPE], jnp.zeros((KV_LORA, MLA_HEADS, HEAD_PAD - QK_NOPE), F32)], axis=-1)
    uk = uk.reshape(KV_LORA, MLA_HEADS * HEAD_PAD)
    place = np.zeros((KV_LORA, MLA_HEADS, HEAD_PAD), np.float32)
    for i in range(QK_ROPE):
        place[i, :, QK_NOPE + i] = 1.0
    wk = jnp.concatenate([uk, jnp.asarray(place.reshape(KV_LORA, MLA_HEADS * HEAD_PAD))], axis=0)
    wk = jnp.concatenate([wk, _gain_swap(wk, head_gain(k_qk))], axis=1).astype(BF16)
    wuv = jnp.concatenate([ukv[..., QK_NOPE:], jnp.zeros((KV_LORA, MLA_HEADS, HEAD_PAD - V_HEAD), F32)], axis=-1)
    wuv = wuv.reshape(KV_LORA, MLA_HEADS * HEAD_PAD).astype(BF16)
    a, b = _rope_tables()
    qs = QK_HEAD ** -0.5 * math.log2(math.e)
    tabs = [jnp.asarray(t, F32) for t in (a, b, a * qs, b * qs)]
    return [w_cat, q_norm[None, :], uq, kv_norm[None, :], wk, wuv, head_gain(q_qk), head_gain(k_qk)] + tabs


ATT_TQ = 256
HEADS_PER_STEP = 4


def _attn_kernel(q_ref, k_ref, v_ref, o_ref):
    for hh in range(HEADS_PER_STEP):
        sl = slice(hh * HEAD_PAD, (hh + 1) * HEAD_PAD)
        s = lax.dot_general(q_ref[0, :, sl], k_ref[0, :, sl], (((1,), (1,)), ((), ())),
                            preferred_element_type=F32)
        m = jnp.max(s, axis=-1, keepdims=True)
        p = jnp.exp2(s - m).astype(BF16)
        acc = jnp.dot(p, v_ref[0, :, sl], preferred_element_type=F32)
        o_ref[0, :, sl] = (acc * (1.0 / acc[:, V_HEAD:V_HEAD + 1])).astype(BF16)


def _attn_call(q, k, v):
    wq = HEADS_PER_STEP * HEAD_PAD
    return pl.pallas_call(
        _attn_kernel, grid=(BATCH, MLA_HEADS // HEADS_PER_STEP, SEQ // ATT_TQ),
        in_specs=[pl.BlockSpec((1, ATT_TQ, wq), lambda b, h, i: (b, i, h)),
                  pl.BlockSpec((1, KV_LEN, wq), lambda b, h, i: (b, 0, h)),
                  pl.BlockSpec((1, KV_LEN, wq), lambda b, h, i: (b, 0, h))],
        out_specs=pl.BlockSpec((1, ATT_TQ, wq), lambda b, h, i: (b, i, h)),
        out_shape=jax.ShapeDtypeStruct((BATCH, SEQ, MLA_HEADS * HEAD_PAD), BF16),
        compiler_params=_params(("parallel", "parallel", "arbitrary")), name="mla_attention",
    )(q, k, v)


S5_ROWS = S5_NCHUNK * BATCH
S5_ROWS_LAT = S5_NCHUNK_LAT * BATCH
S5_COLS = S5_CHUNK * S5_GROUP
S5_SW = 2 * S5_STATE


def _s5_kernel(u_ref, r_ref, mb_ref, mc_ref, coef_ref, y_ref, x_sc, sp_sc, t_sc):
    u = u_ref[0]
    lags = r_ref[0]
    for sg in range(S5_CHUNK):
        off = (S5_CHUNK - 1 - sg) * S5_GROUP
        t_sc[sg * S5_GROUP:(sg + 1) * S5_GROUP, :] = lags[:, off:off + S5_COLS].astype(BF16)
    x_sc[...] = jnp.dot(u, mb_ref[0], preferred_element_type=F32)
    cf = coef_ref[0]
    af, bfm, bfp, ab, bbm, bbp = [cf[i * 8:(i + 1) * 8] for i in range(6)]

    def body(i, carry):
        sf, sfw, sb, sbw = carry
        cfw = jnp.where(i < S5_NCHUNK_CTX, i + S5_NCHUNK_LAT, i - S5_NCHUNK_CTX)
        rf = pl.multiple_of(cfw * BATCH, BATCH)
        rb = pl.multiple_of((S5_NCHUNK - 1 - i) * BATCH, BATCH)
        sp_sc[pl.ds(rf, BATCH), 0:S5_SW] = sf
        sp_sc[pl.ds(rb, BATCH), S5_SW:2 * S5_SW] = sb
        xf = x_sc[pl.ds(rf, BATCH), 0:S5_SW]
        xfw = x_sc[pl.ds(rf, BATCH), S5_SW:2 * S5_SW]
        xb = x_sc[pl.ds(rb, BATCH), 2 * S5_SW:3 * S5_SW]
        xbw = x_sc[pl.ds(rb, BATCH), 3 * S5_SW:4 * S5_SW]
        return (sf * af + sfw * bfm + xf, sfw * af + sf * bfp + xfw,
                sb * ab + sbw * bbm + xb, sbw * ab + sb * bbp + xbw)

    z = jnp.zeros((BATCH, S5_SW), F32)
    lax.fori_loop(0, S5_NCHUNK, body, (z, z, z, z))
    y = jnp.dot(u[:S5_ROWS_LAT], t_sc[...], preferred_element_type=F32)
    y = y + jnp.dot(sp_sc[0:S5_ROWS_LAT, :].astype(BF16), mc_ref[0], preferred_element_type=F32)
    y_ref[0] = y.astype(BF16)


def _s5_call(ug, t, mb, mc, coef):
    g3 = lambda g: (g, 0, 0)
    return pl.pallas_call(
        _s5_kernel, grid=(S5_GROUPS,),
        in_specs=[pl.BlockSpec((1, S5_ROWS, S5_COLS), g3),
                  pl.BlockSpec((1, S5_GROUP, 2 * S5_COLS), g3),
                  pl.BlockSpec((1, S5_COLS, 4 * S5_SW), g3),
                  pl.BlockSpec((1, 2 * S5_SW, S5_COLS), g3),
                  pl.BlockSpec((1, 6 * 8, S5_SW), g3)],
        out_specs=pl.BlockSpec((1, S5_ROWS_LAT, S5_COLS), g3),
        out_shape=jax.ShapeDtypeStruct((S5_GROUPS, S5_ROWS_LAT, S5_COLS), BF16),
        scratch_shapes=[pltpu.VMEM((S5_ROWS, 4 * S5_SW), F32), pltpu.VMEM((S5_ROWS, 2 * S5_SW), F32),
                        pltpu.VMEM((S5_COLS, S5_COLS), BF16)],
        compiler_params=_params(("parallel",)), name="s5_chunked_scan",
    )(ug, t, mb, mc, coef)


def _s5_weights(lam_re, lam_im, log_step, b_re, b_im, c_re, c_im):
    q = S5_CHUNK
    hi = lax.Precision.HIGHEST
    sig = np.arange(q)
    dirs = np.arange(2)[None, :]
    lr = jnp.minimum(lam_re, -1e-4)
    li = lam_im
    step = jnp.exp(log_step)[..., None]
    jj = jnp.arange(q + 1, dtype=F32)[:, None, None, None]
    mag = jnp.exp(lr * step * jj)
    ph = li * step * jj
    pr, pi = mag * jnp.cos(ph), mag * jnp.sin(ph)
    nr, ni = pr[1] - 1.0, pi[1]
    den = lr * lr + li * li
    fr, fi = (nr * lr + ni * li) / den, (ni * lr - nr * li) / den
    br = fr[..., None] * b_re - fi[..., None] * b_im
    bi = fr[..., None] * b_im + fi[..., None] * b_re
    cpr = c_re[None] * pr[:, :, :, None, :] - c_im[None] * pi[:, :, :, None, :]
    cpi = c_re[None] * pi[:, :, :, None, :] + c_im[None] * pr[:, :, :, None, :]
    kern = jnp.einsum('jdghp,dgpk->jdghk', jnp.concatenate([cpr[:q], -cpi[:q]], axis=-1),
                      jnp.concatenate([br, bi], axis=2), precision=hi)
    kt = kern.transpose(1, 2, 4, 0, 3)
    zero_slots = jnp.zeros((S5_GROUPS, S5_GROUP, q, S5_GROUP), F32)
    t = (jnp.concatenate([zero_slots[:, :, :q - 1], kt[0], zero_slots[:, :, :1]], axis=2)
         + jnp.concatenate([kt[1, :, :, ::-1], zero_slots], axis=2)).reshape(S5_GROUPS, S5_GROUP, 2 * S5_COLS)
    pw = np.stack([q - 1 - sig, sig], axis=1)
    po = np.stack([sig + 1, q - sig], axis=1)
    ppr, ppi = pr[pw, dirs][..., None], pi[pw, dirs][..., None]
    xr = (ppr * br[None] - ppi * bi[None]).transpose(1, 2, 0, 4, 3).reshape(2, S5_GROUPS, S5_COLS, S5_STATE)
    xi = (ppr * bi[None] + ppi * br[None]).transpose(1, 2, 0, 4, 3).reshape(2, S5_GROUPS, S5_COLS, S5_STATE)
    mb = jnp.concatenate([xr[0], xi[0], xi[0], xr[0], xr[1], xi[1], xi[1], xr[1]], axis=-1).astype(BF16)
    mr = cpr[po, dirs].transpose(1, 2, 4, 0, 3).reshape(2, S5_GROUPS, S5_STATE, S5_COLS)
    mi = -cpi[po, dirs].transpose(1, 2, 4, 0, 3).reshape(2, S5_GROUPS, S5_STATE, S5_COLS)
    mc = jnp.concatenate([mr[0], mi[0], mr[1], mi[1]], axis=1).astype(BF16)
    are, aim = pr[q], pi[q]
    rows = jnp.stack([jnp.concatenate([are, are], -1), jnp.concatenate([-aim, aim], -1),
                      jnp.concatenate([aim, -aim], -1)], axis=1)
    coef = jnp.broadcast_to(rows.transpose(2, 0, 1, 3)[:, :, :, None, :], (S5_GROUPS, 2, 3, 8, S5_SW))
    return t, mb, mc, coef.reshape(S5_GROUPS, 6 * 8, S5_SW)


A1_TL = MOE_TL


def _a1_kernel(att_ref, ys_ref, u_ref, d_ref, gw_ref, gb_ref, woa_ref, wos_ref, x_ref, g_ref, *route):
    o_ref = route[5]
    y = u_ref[0].astype(F32) * d_ref[...] + ys_ref[0].astype(F32)
    z = jax.nn.gelu(y)
    gate = jax.nn.sigmoid(jnp.dot(z.astype(BF16), gw_ref[...], preferred_element_type=F32) + gb_ref[...])
    s5 = (z * gate).astype(BF16)
    mix = jnp.dot(att_ref[0], woa_ref[...], preferred_element_type=F32)
    mix = mix + jnp.dot(s5, wos_ref[...], preferred_element_type=F32)
    x_new = x_ref[0] + g_ref[0] * mix
    o_ref[0] = x_new
    _route_tile(x_new, *route[:5], *route[6:])


def _a1_call(att, ys, u, d, gw, gb, woa, wos, x, g1, route):
    r_args, r_in, r_out, r_shape, r_scratch = route
    t3 = lambda b, j: (b, j, 0)
    full2 = lambda b, j: (0, 0)
    return pl.pallas_call(
        _a1_kernel, grid=(BATCH, SEQ // A1_TL),
        in_specs=[pl.BlockSpec((1, A1_TL, MLA_HEADS * HEAD_PAD), t3),
                  pl.BlockSpec((1, A1_TL, S5_WIDTH), t3),
                  pl.BlockSpec((1, A1_TL, S5_WIDTH), t3),
                  pl.BlockSpec((1, S5_WIDTH), full2),
                  pl.BlockSpec((S5_WIDTH, S5_WIDTH), full2),
                  pl.BlockSpec((1, S5_WIDTH), full2),
                  pl.BlockSpec((MLA_HEADS * HEAD_PAD, D_MODEL), full2),
                  pl.BlockSpec((S5_WIDTH, D_MODEL), full2),
                  pl.BlockSpec((1, A1_TL, D_MODEL), t3),
                  pl.BlockSpec((1, 1, D_MODEL), lambda b, j: (b, 0, 0))] + r_in,
        out_specs=[pl.BlockSpec((1, A1_TL, D_MODEL), t3)] + r_out,
        out_shape=[jax.ShapeDtypeStruct((BATCH, SEQ, D_MODEL), F32)] + r_shape,
        scratch_shapes=r_scratch,
        compiler_params=_params(("arbitrary", "arbitrary")), name="even_out_proj_route",
    )(att, ys, u, d, gw, gb, woa, wos, x, g1, *r_args)


SLOT_PAD = 8


def _pack_rows(v):
    halves = []
    for p in range(2):
        base = 2 * p * ROW_WORDS
        a = pltpu.bitcast(v[:, base:base + ROW_WORDS].astype(BF16).astype(F32), jnp.uint32)
        b = pltpu.bitcast(v[:, base + ROW_WORDS:base + 2 * ROW_WORDS].astype(BF16).astype(F32), jnp.uint32)
        halves.append((a >> 16) | b)
    return halves


def _unpack_rows(lo, hi):
    out = []
    for w in (lo, hi):
        out.append(pltpu.bitcast(w << 16, F32))
        out.append(pltpu.bitcast(w & jnp.uint32(0xFFFF0000), F32))
    return out


def _route_tile(x, sh_ref, sc_ref, rwt_ref, rb_ref, tri_ref,
                hlo_ref, hhi_ref, idx_ref, wt_ref, rank_ref, cnt_ref, run_sc):
    @pl.when((pl.program_id(0) == 0) & (pl.program_id(1) == 0))
    def _():
        run_sc[...] = jnp.zeros_like(run_sc)

    h = _norm_mod(x, sh_ref[0], sc_ref[0])
    hb = h.astype(BF16)
    hlo_ref[0], hhi_ref[0] = _pack_rows(h)
    h_lo = (h - hb.astype(F32)).astype(BF16)
    rwt = rwt_ref[...]
    rw_hi = rwt.astype(BF16)
    rw_lo = (rwt - rw_hi.astype(F32)).astype(BF16)
    logits = lax.dot_general(jnp.concatenate([rw_hi, rw_lo, rw_hi], axis=1), jnp.concatenate([hb, hb, h_lo], axis=1),
                             (((1,), (1,)), ((), ())), preferred_element_type=F32)
    scores = jax.nn.sigmoid(logits)

    work = scores + rb_ref[...]
    expert = lax.broadcasted_iota(jnp.int32, work.shape, 0).astype(F32)
    hits, ids = [], []
    for _ in range(TOP_K):
        m = jnp.max(work, axis=0, keepdims=True)
        ik = jnp.min(jnp.where(work == m, expert, float(N_EXPERTS)), axis=0, keepdims=True)
        hit = expert == ik
        hits.append(hit)
        ids.append(ik)
        work = jnp.where(hit, -jnp.inf, work)
    mask = hits[0]
    for hit in hits[1:]:
        mask = jnp.logical_or(mask, hit)
    maskf = mask.astype(F32)
    before = jnp.dot(maskf.astype(BF16), tri_ref[...], preferred_element_type=F32) + run_sc[:, 0:1]
    sel = [jnp.sum(jnp.where(hit, scores, 0.0), axis=0, keepdims=True) for hit in hits]
    denom = sel[0]
    for s in sel[1:]:
        denom = denom + s
    ranks = [jnp.sum(jnp.where(hit, before, 0.0), axis=0, keepdims=True) for hit in hits]
    pad = [jnp.zeros_like(denom)] * (SLOT_PAD - TOP_K)
    idx_ref[...] = jnp.concatenate(ids + pad, axis=0).astype(jnp.int32)
    wt_ref[...] = jnp.concatenate([s / denom * ROUTE_SCALE for s in sel] + pad, axis=0)
    rank_ref[...] = jnp.concatenate(ranks + pad, axis=0).astype(jnp.int32)
    run_sc[...] += jnp.sum(maskf, axis=1, keepdims=True)
    cnt_ref[...] = run_sc[...]


def _route_plumbing(sh, sc, rw, rb):
    t3 = lambda b, j: (b, j, 0)
    full2 = lambda b, j: (0, 0)
    per_b = lambda b, j: (b, 0, 0)
    nt = SEQ // MOE_TL
    slots = lambda b, j: (0, b * nt + j)
    tri = jnp.asarray(np.triu(np.ones((MOE_TL, MOE_TL), np.float32), 1), BF16)
    args = [sh, sc, rw.T, rb[:, None], tri]
    in_specs = [pl.BlockSpec((1, 1, D_MODEL), per_b),
                pl.BlockSpec((1, 1, D_MODEL), per_b),
                pl.BlockSpec((N_EXPERTS, D_MODEL), full2),
                pl.BlockSpec((N_EXPERTS, 1), full2),
                pl.BlockSpec((MOE_TL, MOE_TL), full2)]
    out_specs = [pl.BlockSpec((1, MOE_TL, ROW_WORDS), t3),
                 pl.BlockSpec((1, MOE_TL, ROW_WORDS), t3),
                 pl.BlockSpec((SLOT_PAD, MOE_TL), slots),
                 pl.BlockSpec((SLOT_PAD, MOE_TL), slots),
                 pl.BlockSpec((SLOT_PAD, MOE_TL), slots),
                 pl.BlockSpec((N_EXPERTS, V7X_LANES), full2)]
    slot_i = jax.ShapeDtypeStruct((SLOT_PAD, BATCH * SEQ), jnp.int32)
    out_shape = [jax.ShapeDtypeStruct((BATCH, SEQ, ROW_WORDS), jnp.uint32),
                 jax.ShapeDtypeStruct((BATCH, SEQ, ROW_WORDS), jnp.uint32),
                 slot_i,
                 jax.ShapeDtypeStruct((SLOT_PAD, BATCH * SEQ), F32),
                 slot_i,
                 jax.ShapeDtypeStruct((N_EXPERTS, V7X_LANES), F32)]
    scratch = [pltpu.VMEM((N_EXPERTS, V7X_LANES), F32)]
    return args, in_specs, out_specs, out_shape, scratch


def _sc_mesh():
    return plsc.VectorSubcoreMesh(core_axis_name="c", subcore_axis_name="s")


def _sc_dispatch(h_words, dest, n_rows):
    n_tok = h_words.shape[0]

    @pl.kernel(out_type=jax.ShapeDtypeStruct((n_rows, ROW_WORDS), jnp.uint32), mesh=_sc_mesh(), scratch_types=[])
    def scatter_rows(h_hbm, i_hbm, o_hbm):
        def body(h_vmem, i_vmem):
            for k in range(TOP_K):
                pltpu.sync_copy(h_vmem, o_hbm.at[i_vmem.at[k]])

        pltpu.emit_pipeline(
            body, grid=(n_tok // SC_WINDOW,),
            in_specs=[pl.BlockSpec((SC_WINDOW, ROW_WORDS), index_map=lambda i: (i, 0)),
                      pl.BlockSpec((SLOT_PAD, SC_WINDOW), index_map=lambda i: (0, i))],
            out_specs=[],
            core_axis_name=("c", "s"), dimension_semantics=(pltpu.PARALLEL,),
        )(h_hbm, i_hbm)

    return scatter_rows(h_words, dest)


def _sc_collect(y_words, dest, part):
    n_tok = dest.shape[1] // COLLECT_PARTS
    w0 = part * (n_tok // SC_WINDOW)

    @pl.kernel(out_type=jax.ShapeDtypeStruct((TOP_K, n_tok, ROW_WORDS), jnp.uint32), mesh=_sc_mesh(),
               scratch_types=[])
    def gather_rows(y_hbm, i_hbm, o_hbm):
        def body(i_vmem, o_vmem):
            pltpu.sync_copy(y_hbm.at[i_vmem.at[0]], o_vmem.at[0])

        pltpu.emit_pipeline(
            body, grid=(TOP_K, n_tok // SC_WINDOW),
            in_specs=[pl.BlockSpec((1, SC_WINDOW), index_map=lambda k, i: (k, i + w0))],
            out_specs=[pl.BlockSpec((1, SC_WINDOW, ROW_WORDS), index_map=lambda k, i: (k, i, 0))],
            core_axis_name=("c", "s"), dimension_semantics=(pltpu.PARALLEL, pltpu.PARALLEL),
        )(i_hbm, o_hbm)

    return gather_rows(y_words, dest)


def _expert_kernel(be_ref, nv_ref, xlo_ref, xhi_ref, wg_ref, wu_ref, wd_ref, ylo_ref, yhi_ref,
                   wg_sc, wu_sc, wd_sc):
    i = pl.program_id(0)
    nv = nv_ref[i]

    @pl.when(jnp.logical_or(i == 0, be_ref[i] != be_ref[jnp.maximum(i - 1, 0)]))
    def _():
        wg_sc[...] = wg_ref[0, 0].astype(BF16)
        wu_sc[...] = wu_ref[0, 0].astype(BF16)
        wd_sc[...] = wd_ref[0, 0].astype(BF16)

    @pl.when(nv > 0)
    def _():
        parts = _unpack_rows(xlo_ref[...], xhi_ref[...])
        xb = jnp.concatenate([p.astype(BF16) for p in parts], axis=1)
        live = lax.broadcasted_iota(jnp.int32, xb.shape, 0) < nv
        xb = jnp.where(live, xb, jnp.zeros_like(xb))
        hid = jax.nn.silu(jnp.dot(xb, wg_sc[...], preferred_element_type=F32))
        hid = hid * jnp.dot(xb, wu_sc[...], preferred_element_type=F32)
        y = jnp.dot(hid.astype(BF16), wd_sc[...], preferred_element_type=F32)
        ylo_ref[...], yhi_ref[...] = _pack_rows(y)

    @pl.when(nv == 0)
    def _():
        ylo_ref[...] = jnp.zeros_like(ylo_ref)
        yhi_ref[...] = jnp.zeros_like(yhi_ref)


def _expert_call(block_e, n_valid, xlo, xhi, wg, wu, wd, li):
    n_rows = xlo.shape[0]
    n_blocks = n_rows // MOE_BLOCK
    rows = pl.BlockSpec((MOE_BLOCK, ROW_WORDS), lambda i, be, nv: (i, 0))
    grid_spec = pltpu.PrefetchScalarGridSpec(
        num_scalar_prefetch=2, grid=(n_blocks,),
        in_specs=[rows, rows,
                  pl.BlockSpec((1, 1, D_MODEL, EXPERT_FF), lambda i, be, nv: (li, be[i], 0, 0)),
                  pl.BlockSpec((1, 1, D_MODEL, EXPERT_FF), lambda i, be, nv: (li, be[i], 0, 0)),
                  pl.BlockSpec((1, 1, EXPERT_FF, D_MODEL), lambda i, be, nv: (li, be[i], 0, 0))],
        out_specs=[rows, rows],
        scratch_shapes=[pltpu.VMEM((D_MODEL, EXPERT_FF), BF16), pltpu.VMEM((D_MODEL, EXPERT_FF), BF16),
                        pltpu.VMEM((EXPERT_FF, D_MODEL), BF16)])
    out = jax.ShapeDtypeStruct((n_rows, ROW_WORDS), jnp.uint32)
    return pl.pallas_call(
        _expert_kernel, grid_spec=grid_spec, out_shape=[out, out],
        compiler_params=_params(("arbitrary",)), name="moe_experts",
    )(block_e, n_valid, xlo, xhi, wg, wu, wd)


def _combine_kernel(ylo_ref, yhi_ref, w_ref, x_ref, sh_ref, sc_ref, g_ref, sg_ref, su_ref, sd_ref, *rest):
    o_ref = rest[-1]
    hb = _norm_mod(x_ref[0], sh_ref[0], sc_ref[0]).astype(BF16)
    hid = jax.nn.silu(jnp.dot(hb, sg_ref[...].astype(BF16), preferred_element_type=F32))
    hid = hid * jnp.dot(hb, su_ref[...].astype(BF16), preferred_element_type=F32)
    shared = jnp.dot(hid.astype(BF16), sd_ref[...].astype(BF16), preferred_element_type=F32)
    w = w_ref[0]
    acc = [None] * 4
    for k in range(TOP_K):
        wk = w[:, k:k + 1]
        for c, part in enumerate(_unpack_rows(ylo_ref[k], yhi_ref[k])):
            acc[c] = wk * part if acc[c] is None else acc[c] + wk * part
    for c in range(4):
        sl = slice(c * ROW_WORDS, (c + 1) * ROW_WORDS)
        o_ref[0, :, sl] = x_ref[0, :, sl] + g_ref[0, :, sl] * (acc[c] + shared[:, sl])


def _combine_call(ylo, yhi, wts, x, sh, sc, g2, sg, su, sd, part, prev):
    nb = BATCH // COLLECT_PARTS
    b0 = part * nb
    g3 = lambda b, j: (b + b0, j, 0)
    per_b = lambda b, j: (b + b0, 0, 0)
    full2 = lambda b, j: (0, 0)
    ff = sg.shape[1]
    nt = SEQ // MOE_TL
    rows = pl.BlockSpec((TOP_K, MOE_TL, ROW_WORDS), lambda b, j: (0, b * nt + j, 0))
    extra_specs = [] if prev is None else [pl.BlockSpec(memory_space=pl.ANY)]
    extra_args = [] if prev is None else [prev]
    return pl.pallas_call(
        _combine_kernel, grid=(nb, nt),
        in_specs=[rows, rows,
                  pl.BlockSpec((1, MOE_TL, SLOT_PAD), g3),
                  pl.BlockSpec((1, MOE_TL, D_MODEL), g3),
                  pl.BlockSpec((1, 1, D_MODEL), per_b),
                  pl.BlockSpec((1, 1, D_MODEL), per_b),
                  pl.BlockSpec((1, 1, D_MODEL), per_b),
                  pl.BlockSpec((D_MODEL, ff), full2),
                  pl.BlockSpec((D_MODEL, ff), full2),
                  pl.BlockSpec((ff, D_MODEL), full2)] + extra_specs,
        out_specs=pl.BlockSpec((1, MOE_TL, D_MODEL), g3),
        out_shape=jax.ShapeDtypeStruct((BATCH, SEQ, D_MODEL), F32),
        input_output_aliases={} if prev is None else {10: 0},
        compiler_params=_params(("parallel", "arbitrary")), name="moe_combine_shared",
    )(ylo, yhi, wts, x, sh, sc, g2, sg, su, sd, *extra_args)


def _moe(x, routed, sh, sc, g2, w_gate, w_up, w_down, sh_gate, sh_up, sh_down, li):
    T = BATCH * SEQ
    TK = T * TOP_K
    hlo, hhi, idx, wts, rank, counts = routed
    wts = wts.T.reshape(BATCH, SEQ, SLOT_PAD)
    counts = counts[:, 0].astype(jnp.int32)
    padded = (counts + MOE_BLOCK - 1) // MOE_BLOCK * MOE_BLOCK
    pad_end = jnp.cumsum(padded)
    pad_start = pad_end - padded
    n_blocks = -(-TK // MOE_BLOCK) + N_EXPERTS
    n_rows = n_blocks * MOE_BLOCK
    block_start = jnp.arange(n_blocks, dtype=jnp.int32) * MOE_BLOCK
    owns = jnp.logical_and(block_start[:, None] >= pad_start[None, :], block_start[:, None] < pad_end[None, :])
    owns = owns.astype(jnp.int32)
    experts = jnp.arange(N_EXPERTS, dtype=jnp.int32)[None, :]
    block_e = jnp.sum(owns * experts, axis=1) + (N_EXPERTS - 1) * (1 - jnp.sum(owns, axis=1))
    n_valid = jnp.sum(owns * (counts[None, :] - (block_start[:, None] - pad_start[None, :])), axis=1)
    n_valid = jnp.clip(n_valid, 0, MOE_BLOCK).astype(jnp.int32)
    first_row = jnp.sum(jnp.where(idx[None] == experts.T[:, :, None], pad_start[:, None, None], 0), axis=0)
    dest = first_row + rank
    xlo = _sc_dispatch(hlo.reshape(T, ROW_WORDS), dest, n_rows)
    xhi = _sc_dispatch(hhi.reshape(T, ROW_WORDS), dest, n_rows)
    ylo, yhi = _expert_call(block_e, n_valid, xlo, xhi, w_gate, w_up, w_down, li)
    out = None
    for part in range(COLLECT_PARTS):
        out = _combine_call(_sc_collect(ylo, dest, part), _sc_collect(yhi, dest, part), wts, x, sh, sc, g2,
                            sh_gate, sh_up, sh_down, part, out)
    return out


HY_TL = MOE_TL
HALO = 8


def _hy_in_kernel(x_ref, xp_ref, xn_ref, sh_ref, sc_ref, w_ref, cw_ref, cb_ref, z_ref, x0_ref, h_sc):
    j = pl.program_id(1)
    shift, scale = sh_ref[0], sc_ref[0]
    keep_prev = (j > 0).astype(F32)
    keep_next = (j < SEQ // HY_TL - 1).astype(F32)
    h_sc[0:HALO, :] = _norm_mod(xp_ref[0], shift, scale) * keep_prev
    h_sc[HALO:HALO + HY_TL, :] = _norm_mod(x_ref[0], shift, scale)
    h_sc[HALO + HY_TL:, :] = _norm_mod(xn_ref[0], shift, scale) * keep_next
    hcat = h_sc[...].astype(BF16)
    outs = []
    for part in range(3):
        sl = slice(part * HY_WIDTH, (part + 1) * HY_WIDTH)
        p = jnp.dot(hcat, w_ref[:, sl], preferred_element_type=F32)
        o = (p[HALO - 1:HALO - 1 + HY_TL] * cw_ref[0:1, sl] + p[HALO:HALO + HY_TL] * cw_ref[1:2, sl]
             + p[HALO + 1:HALO + 1 + HY_TL] * cw_ref[2:3, sl] + cb_ref[:, sl])
        outs.append(o)
    x0_ref[0] = outs[0].astype(BF16)
    z_ref[0] = (outs[2] * outs[1]).astype(BF16)


def _hy_in_call(x, sh, sc, w, cw, cb):
    nb8 = HY_TL // HALO
    t3 = lambda b, j: (b, j, 0)
    full2 = lambda b, j: (0, 0)
    per_b = lambda b, j: (b, 0, 0)
    return pl.pallas_call(
        _hy_in_kernel, grid=(BATCH, SEQ // HY_TL),
        in_specs=[pl.BlockSpec((1, HY_TL, D_MODEL), t3),
                  pl.BlockSpec((1, HALO, D_MODEL), lambda b, j: (b, jnp.maximum(j * nb8 - 1, 0), 0)),
                  pl.BlockSpec((1, HALO, D_MODEL), lambda b, j: (b, jnp.minimum((j + 1) * nb8, SEQ // HALO - 1), 0)),
                  pl.BlockSpec((1, 1, D_MODEL), per_b),
                  pl.BlockSpec((1, 1, D_MODEL), per_b),
                  pl.BlockSpec((D_MODEL, 3 * HY_WIDTH), full2),
                  pl.BlockSpec((SHORT_CONV, 3 * HY_WIDTH), full2),
                  pl.BlockSpec((1, 3 * HY_WIDTH), full2)],
        out_specs=[pl.BlockSpec((1, HY_TL, HY_WIDTH), t3), pl.BlockSpec((1, HY_TL, HY_WIDTH), t3)],
        out_shape=[jax.ShapeDtypeStruct((BATCH, SEQ, HY_WIDTH), BF16),
                   jax.ShapeDtypeStruct((BATCH, SEQ, HY_WIDTH), BF16)],
        scratch_shapes=[pltpu.VMEM((HY_TL + 2 * HALO, D_MODEL), F32)],
        compiler_params=_params(("parallel", "arbitrary")), name="hyena_in_proj",
    )(x, x, x, sh, sc, w, cw, cb)


def _fft_tables():
    c = np.arange(FFT_N2, dtype=np.int64)
    ang = 2.0 * np.pi * ((c[:, None] * c[None, :]) % FFT_N2) / FFT_N2
    sr, si = np.cos(ang), -np.sin(ang)
    m = np.block([[sr, -si], [si, sr]])
    k1 = np.arange(FFT_NK, dtype=np.int64)
    ang_t = 2.0 * np.pi * (k1[:, None] * c[None, :]) / DFT_N
    lanes = np.ones((1, 1, V7X_LANES))
    tr = np.cos(ang_t)[:, :, None] * lanes
    ti = -np.sin(ang_t)[:, :, None] * lanes
    return jnp.asarray(m, BF16), jnp.asarray(tr, F32), jnp.asarray(ti, F32)


def _lin(acc, coef, val):
    if abs(coef) < 1e-12:
        return acc
    term = val if coef == 1.0 else (-val if coef == -1.0 else coef * val)
    return term if acc is None else acc + term


def _twiddle(tr_ref, ti_ref, k1, width):
    reps = width // V7X_LANES
    tr, ti = tr_ref[k1], ti_ref[k1]
    return jnp.concatenate([tr] * reps, axis=1), jnp.concatenate([ti] * reps, axis=1)


_COS_PI_4 = math.sqrt(0.5)


def _blocks_to_classes(z):
    assert FFT_N1 == 16
    sp = [z[a] + z[a + 4] for a in range(4)]
    sm = [z[a] - z[a + 4] for a in range(4)]
    q0p, q0m, q1p, q1m = sp[0] + sp[2], sp[0] - sp[2], sp[1] + sp[3], sp[1] - sp[3]
    d, e = _COS_PI_4 * (sm[1] - sm[3]), _COS_PI_4 * (sm[1] + sm[3])
    out = [None] * FFT_NK
    out[0] = (q0p + q1p, None)
    out[8] = (q0p - q1p, None)
    out[4] = (q0m, -q1m)
    out[2] = (sm[0] + d, -(sm[2] + e))
    out[6] = (sm[0] - d, sm[2] - e)
    for k in (1, 3):
        halves = []
        for parity in (0, 1):
            hr = hi = None
            for a in range(parity, FFT_NA, 2):
                th = 2.0 * math.pi * ((a * k) % FFT_N1) / FFT_N1
                hr = _lin(hr, round(math.cos(th), 15), z[a])
                hi = _lin(hi, round(-math.sin(th), 15), z[a])
            halves.append((hr, hi))
        (er, ei), (odr, odi) = halves
        out[k] = (er + odr, ei + odi)
        out[8 - k] = (er - odr, odi - ei)
    return out


def _classes_to_blocks(v):
    assert FFT_N1 == 16
    base = (v[0][0] + v[8][0], v[0][0] - v[8][0])
    pr = {k: (v[k][0] + v[8 - k][0], v[k][0] - v[8 - k][0]) for k in (1, 2, 3)}
    pi = {k: (v[k][1] - v[8 - k][1], v[k][1] + v[8 - k][1]) for k in (1, 2, 3)}
    out = []
    for a in range(FFT_NA):
        odd = a % 2
        mid = (v[4][0], 1.0 - (a % 4)) if not odd else (v[4][1], (a % 4) - 2.0)
        acc = _lin(base[odd], mid[1], mid[0])
        for k in (1, 2, 3):
            th = 2.0 * math.pi * ((a * k) % FFT_N1) / FFT_N1
            acc = _lin(acc, round(math.cos(th), 15), pr[k][odd])
            acc = _lin(acc, round(-math.sin(th), 15), pi[k][odd])
        out.append(acc)
    return out


def _class_forward(yr, yi, k1, m_ref, tr_ref, ti_ref, width):
    if k1 == 0:
        x = jnp.dot(m_ref[:, :FFT_N2], yr.astype(BF16), preferred_element_type=F32)
    else:
        tr, ti = _twiddle(tr_ref, ti_ref, k1, width)
        yr, yi = (yr * tr, yr * ti) if yi is None else (yr * tr - yi * ti, yr * ti + yi * tr)
        x = jnp.dot(m_ref[...], jnp.concatenate([yr, yi], axis=0).astype(BF16), preferred_element_type=F32)
    return x[:FFT_N2], x[FFT_N2:]


def _class_backward(pr, pi, k1, m_ref, tr_ref, ti_ref, width):
    v = jnp.dot(m_ref[...], jnp.concatenate([pr, -pi], axis=0).astype(BF16), preferred_element_type=F32)
    ur, ui = v[:FFT_N2], -v[FFT_N2:]
    if k1 > 0:
        tr, ti = _twiddle(tr_ref, ti_ref, k1, width)
        ur, ui = ur * tr + ui * ti, ui * tr - ur * ti
    return ur, ui


def _filter_features():
    t = np.linspace(0.0, 1.0, SEQ)[:, None]
    ang = 2.0 * np.pi * np.arange(SEQ)[:, None] / SEQ
    bands = np.linspace(1e-4, FILT_BANDS - 1, FILT_BANDS)
    z = np.concatenate([t, np.cos(bands * ang), -np.sin(bands * ang)], axis=-1)
    return jnp.asarray(np.pad(z, ((0, 0), (0, FILT_PAD - FILT_EMB))), F32)


def _spec_kernel(zf_ref, w1_ref, b1_ref, w2_ref, b2_ref, fq_ref, w3f_ref, w3b_ref, dl_ref, m_ref, tr_ref, ti_ref,
                 c_ref, hf_sc, hb_sc):
    hi = lax.Precision.HIGHEST
    zf, fq = zf_ref[...], fq_ref[...]
    hid = jnp.sin(fq * (jnp.dot(zf, w1_ref[...], preferred_element_type=F32, precision=hi) + b1_ref[...]))
    hid = jnp.sin(fq * (jnp.dot(hid, w2_ref[...], preferred_element_type=F32, precision=hi) + b2_ref[...]))
    decay = jnp.exp(-zf[:, 0:1] * dl_ref[...])
    hf = jnp.dot(hid, w3f_ref[...], preferred_element_type=F32, precision=hi) * decay
    hb = jnp.dot(hid, w3b_ref[...], preferred_element_type=F32, precision=hi) * decay
    ssq = jnp.sum(hf * hf, axis=0, keepdims=True) + jnp.sum(hb * hb, axis=0, keepdims=True)
    inv = lax.rsqrt(ssq + EPS)
    hf_sc[...] = hf * inv
    hb_sc[...] = hb * inv
    rows = lambda ref: [ref[a * FFT_N2:(a + 1) * FFT_N2, :] for a in range(FFT_NA)]
    fwd, bwd = _blocks_to_classes(rows(hf_sc)), _blocks_to_classes(rows(hb_sc))
    for k1 in range(FFT_NK):
        fr, fi = _class_forward(*fwd[k1], k1, m_ref, tr_ref, ti_ref, HY_CT)
        br, bi = _class_forward(*bwd[k1], k1, m_ref, tr_ref, ti_ref, HY_CT)
        scale = (1.0 if k1 in (0, FFT_N1 // 2) else 2.0) / DFT_N
        c_ref[k1, :FFT_N2, :] = ((fr + br) * scale).astype(BF16)
        c_ref[k1, FFT_N2:, :] = ((fi - bi) * scale).astype(BF16)


def _fft_table_specs(ngrid):
    z = (0,) * 2
    z3 = (0,) * 3
    if ngrid == 1:
        return [pl.BlockSpec((2 * FFT_N2, 2 * FFT_N2), lambda c: z),
                pl.BlockSpec((FFT_NK, FFT_N2, V7X_LANES), lambda c: z3),
                pl.BlockSpec((FFT_NK, FFT_N2, V7X_LANES), lambda c: z3)]
    return [pl.BlockSpec((2 * FFT_N2, 2 * FFT_N2), lambda b, c: z),
            pl.BlockSpec((FFT_NK, FFT_N2, V7X_LANES), lambda b, c: z3),
            pl.BlockSpec((FFT_NK, FFT_N2, V7X_LANES), lambda b, c: z3)]


def _spec_call(w1, b1, w2, b2, w3, freq, m, tr, ti):
    nct = HY_WIDTH // HY_CT

    def pad2(a, rows, cols):
        return jnp.pad(a, ((0, rows - a.shape[0]), (0, cols - a.shape[1])))

    full = lambda c: (0, 0)
    small = pl.BlockSpec((FILT_PAD, FILT_PAD), full)
    vec = pl.BlockSpec((1, FILT_PAD), full)
    deltas = jnp.asarray(np.linspace(HY_MIN_DECAY, HY_MAX_DECAY, HY_WIDTH)[None, :], F32)
    w3p = pad2(w3, FILT_PAD, 2 * HY_WIDTH)
    return pl.pallas_call(
        _spec_kernel, grid=(nct,),
        in_specs=[pl.BlockSpec((SEQ, FILT_PAD), full), small, vec, small, vec, vec,
                  pl.BlockSpec((FILT_PAD, HY_CT), lambda c: (0, c)),
                  pl.BlockSpec((FILT_PAD, HY_CT), lambda c: (0, c + nct)),
                  pl.BlockSpec((1, HY_CT), lambda c: (0, c))] + _fft_table_specs(1),
        out_specs=pl.BlockSpec((FFT_NK, 2 * FFT_N2, HY_CT), lambda c: (0, 0, c)),
        out_shape=jax.ShapeDtypeStruct((FFT_NK, 2 * FFT_N2, HY_WIDTH), BF16),
        scratch_shapes=[pltpu.VMEM((SEQ, HY_CT), F32), pltpu.VMEM((SEQ, HY_CT), F32)],
        compiler_params=_params(("arbitrary",)), name="hyena_filter_spectrum",
    )(_filter_features(), pad2(w1, FILT_PAD, FILT_PAD), pad2(b1[None, :], 1, FILT_PAD),
      pad2(w2, FILT_PAD, FILT_PAD), pad2(b2[None, :], 1, FILT_PAD), pad2(freq[None, :], 1, FILT_PAD),
      w3p, w3p, deltas, m, tr, ti)


def _conv_kernel(z_ref, c_ref, m_ref, tr_ref, ti_ref, y_ref, cls):
    for k1, (yr, yi) in enumerate(_blocks_to_classes(
            [z_ref[0, a * FFT_N2:(a + 1) * FFT_N2, :].astype(F32) for a in range(FFT_NA)])):
        cls[k1, :FFT_N2, :] = yr
        if yi is not None:
            cls[k1, FFT_N2:, :] = yi
    for k1 in range(FFT_NK):
        yi = None if k1 in (0, FFT_N1 // 2) else cls[k1, FFT_N2:, :]
        xr, xi = _class_forward(cls[k1, :FFT_N2, :], yi, k1, m_ref, tr_ref, ti_ref, HY_CT)
        cr = c_ref[k1, :FFT_N2, :].astype(F32)
        ci = c_ref[k1, FFT_N2:, :].astype(F32)
        ur, ui = _class_backward(xr * cr - xi * ci, xr * ci + xi * cr, k1, m_ref, tr_ref, ti_ref, HY_CT)
        cls[k1, :FFT_N2, :] = ur
        cls[k1, FFT_N2:, :] = ui
    blocks = _classes_to_blocks([(cls[k1, :FFT_N2, :], cls[k1, FFT_N2:, :]) for k1 in range(FFT_NK)])
    for a, ya in enumerate(blocks):
        y_ref[0, a * FFT_N2:(a + 1) * FFT_N2, :] = ya.astype(BF16)


def _conv_call(z, spec, m, tr, ti):
    return pl.pallas_call(
        _conv_kernel, grid=(HY_WIDTH // HY_CT, BATCH),
        in_specs=[pl.BlockSpec((1, SEQ, HY_CT), lambda c, b: (b, 0, c)),
                  pl.BlockSpec((FFT_NK, 2 * FFT_N2, HY_CT), lambda c, b: (0, 0, c))] + _fft_table_specs(2),
        out_specs=pl.BlockSpec((1, SEQ, HY_CT), lambda c, b: (b, 0, c)),
        out_shape=jax.ShapeDtypeStruct((BATCH, SEQ, HY_WIDTH), BF16),
        scratch_shapes=[pltpu.VMEM((FFT_NK, 2 * FFT_N2, HY_CT), F32)],
        compiler_params=_params(("parallel", "arbitrary")), name="hyena_long_conv",
    )(z, spec, m, tr, ti)


def _hy_out_kernel(y_ref, z_ref, x0_ref, b_ref, w_ref, x_ref, g_ref, *route):
    o_ref = route[5]
    z = z_ref[0].astype(F32)
    gated = x0_ref[0].astype(F32) * (y_ref[0].astype(F32) + b_ref[...] * z)
    mix = jnp.dot(gated.astype(BF16), w_ref[...], preferred_element_type=F32)
    x_new = x_ref[0] + g_ref[0] * mix
    o_ref[0] = x_new
    _route_tile(x_new, *route[:5], *route[6:])


def _hy_out_call(y, z, x0, bias, w, x, g1, route):
    r_args, r_in, r_out, r_shape, r_scratch = route
    t3 = lambda b, j: (b, j, 0)
    full2 = lambda b, j: (0, 0)
    return pl.pallas_call(
        _hy_out_kernel, grid=(BATCH, SEQ // HY_TL),
        in_specs=[pl.BlockSpec((1, HY_TL, HY_WIDTH), t3),
                  pl.BlockSpec((1, HY_TL, HY_WIDTH), t3),
                  pl.BlockSpec((1, HY_TL, HY_WIDTH), t3),
                  pl.BlockSpec((1, HY_WIDTH), full2),
                  pl.BlockSpec((HY_WIDTH, D_MODEL), full2),
                  pl.BlockSpec((1, HY_TL, D_MODEL), t3),
                  pl.BlockSpec((1, 1, D_MODEL), lambda b, j: (b, 0, 0))] + r_in,
        out_specs=[pl.BlockSpec((1, HY_TL, D_MODEL), t3)] + r_out,
        out_shape=[jax.ShapeDtypeStruct((BATCH, SEQ, D_MODEL), F32)] + r_shape,
        scratch_shapes=r_scratch,
        compiler_params=_params(("arbitrary", "arbitrary")), name="hyena_out_proj_route",
    )(y, z, x0, bias, w, x, g1, *r_args)


def kernel(x, c, ctx, c_ctx, ada_w, ada_b, ev_w_in, mla_q_norm, mla_w_uq, mla_kv_norm, mla_w_ukv, mla_q_qknorm, mla_k_qknorm, s5_lam_re, s5_lam_im, s5_log_step, s5_b_re, s5_b_im, s5_c_re, s5_c_im, s5_d, s5_glu_w, s5_glu_b, ev_w_out, hy_w_in, hy_conv_w, hy_conv_b, hy_f_w1, hy_f_b1, hy_f_w2, hy_f_b2, hy_f_w3, hy_f_freq, hy_bias, hy_w_out, moe_router_w, moe_router_b, moe_w_gate, moe_w_up, moe_w_down, moe_sh_gate, moe_sh_up, moe_sh_down):
    D = D_MODEL
    mod_all = _ada_call(c, c_ctx, ada_w, ada_b)

    def mods(li):
        return [m[:, None, :] for m in jnp.split(mod_all[li, :BATCH], 6, axis=-1)]

    sh1, sc1, g1, sh2, sc2, g2 = mods(0)
    mod_ctx = mod_all[0, BATCH, :2 * D]
    w0 = _a0_weights(ev_w_in[0], mla_q_norm[0], mla_w_uq[0], mla_kv_norm[0], mla_w_ukv[0],
                     mla_q_qknorm[0], mla_k_qknorm[0])
    q, k, v, u = _a0_call(x, ctx, sh1, sc1, mod_ctx[None, :D], mod_ctx[None, D:], w0)
    att = _attn_call(q, k, v)
    ug = u.reshape(BATCH, S5_NCHUNK, S5_CHUNK, S5_GROUPS, S5_GROUP)
    ug = ug.transpose(3, 1, 0, 2, 4).reshape(S5_GROUPS, S5_ROWS, S5_COLS)
    ys = _s5_call(ug, *_s5_weights(s5_lam_re[0], s5_lam_im[0], s5_log_step[0], s5_b_re[0], s5_b_im[0],
                                   s5_c_re[0], s5_c_im[0]))
    ys = ys.reshape(S5_GROUPS, S5_NCHUNK_LAT, BATCH, S5_CHUNK, S5_GROUP)
    ys = ys.transpose(2, 1, 3, 0, 4).reshape(BATCH, SEQ, S5_WIDTH)
    wo = ev_w_out[0].astype(BF16)
    wo_att = jnp.concatenate([wo[:MLA_WIDTH].reshape(MLA_HEADS, V_HEAD, D_MODEL),
                              jnp.zeros((MLA_HEADS, HEAD_PAD - V_HEAD, D_MODEL), BF16)], axis=1)
    x, *routed = _a1_call(att, ys, u, s5_d[0][None, :], s5_glu_w[0].astype(BF16), s5_glu_b[0][None, :],
                          wo_att.reshape(MLA_HEADS * HEAD_PAD, D_MODEL), wo[MLA_WIDTH:], x, g1,
                          _route_plumbing(sh2, sc2, moe_router_w[0], moe_router_b[0]))
    x = _moe(x, routed, sh2, sc2, g2, moe_w_gate, moe_w_up, moe_w_down,
             moe_sh_gate[0], moe_sh_up[0], moe_sh_down[0], 0)

    sh1, sc1, g1, sh2, sc2, g2 = mods(1)
    z, x0 = _hy_in_call(x, sh1, sc1, hy_w_in[0].astype(BF16), hy_conv_w[0], hy_conv_b[0][None, :])
    fft_tabs = _fft_tables()
    spectrum = _spec_call(hy_f_w1[0], hy_f_b1[0], hy_f_w2[0], hy_f_b2[0], hy_f_w3[0], hy_f_freq[0], *fft_tabs)
    y = _conv_call(z, spectrum, *fft_tabs)
    x, *routed = _hy_out_call(y, z, x0, hy_bias[0][None, :], hy_w_out[0].astype(BF16), x, g1,
                              _route_plumbing(sh2, sc2, moe_router_w[1], moe_router_b[1]))
    x = _moe(x, routed, sh2, sc2, g2, moe_w_gate, moe_w_up, moe_w_down,
             moe_sh_gate[1], moe_sh_up[1], moe_sh_down[1], 1)
    return x
```

```python
import math

import numpy as np
import jax
import jax.numpy as jnp
from jax import lax
from jax.experimental import pallas as pl
from jax.experimental.pallas import tpu as pltpu
from jax.experimental.pallas import tpu_sc as plsc

F32 = jnp.float32
BF16 = jnp.bfloat16

D_MODEL = 1024
BATCH = 8
SEQ = 4096
CTX_LEN = 256
KV_LEN = SEQ + CTX_LEN
GRID_W = 64
EPS = 1e-6

MLA_HEADS = 8
QK_NOPE = 64
QK_ROPE = 32
QK_HEAD = QK_NOPE + QK_ROPE
V_HEAD = 64
Q_LORA = 256
KV_LORA = 128
MLA_WIDTH = MLA_HEADS * V_HEAD
ROPE_BASE = 10000.0
HEAD_PAD = 128

S5_WIDTH = 512
S5_GROUP = 16
S5_GROUPS = S5_WIDTH // S5_GROUP
S5_STATE = 64
S5_CHUNK = 32
S5_NCHUNK = KV_LEN // S5_CHUNK
S5_NCHUNK_LAT = SEQ // S5_CHUNK
S5_NCHUNK_CTX = CTX_LEN // S5_CHUNK

HY_WIDTH = D_MODEL
FILT_EMB = 33
FILT_BANDS = (FILT_EMB - 1) // 2
FILT_PAD = 128
SHORT_CONV = 3
HY_MIN_DECAY = -math.log(1e-2) / 1.5
HY_MAX_DECAY = -math.log(1e-2) / 0.3
DFT_N = 2 * SEQ
FFT_N1 = 16
FFT_N2 = DFT_N // FFT_N1
FFT_NA = FFT_N1 // 2
FFT_NK = FFT_N1 // 2 + 1
HY_CT = 256

N_EXPERTS = 64
TOP_K = 6
EXPERT_FF = 256
ROUTE_SCALE = 2.5
MOE_BLOCK = 1024
MOE_TL = 512
ROW_WORDS = D_MODEL // 4
SC_WINDOW = 128

V7X_LANES = 128
V7X_VMEM_BYTES = 64 * 1024 * 1024
VMEM_LIMIT = V7X_VMEM_BYTES - 8 * 1024 * 1024


def _params(semantics):
    return pltpu.CompilerParams(dimension_semantics=semantics, vmem_limit_bytes=VMEM_LIMIT)


def _norm_mod(x, shift, scale):
    ms = jnp.mean(x * x, axis=-1, keepdims=True)
    return x * lax.rsqrt(ms + EPS) * (1.0 + scale) + shift


def _rms(x, gain, n):
    ms = jnp.sum(x * x, axis=-1, keepdims=True) * (1.0 / n)
    return x * lax.rsqrt(ms + EPS) * gain


ADA_ROWS = 16
ADA_TN = 1024


def _ada_kernel(c_ref, w_ref, b_ref, o_ref):
    sc = jax.nn.silu(c_ref[...])
    o_ref[0] = jnp.dot(sc, w_ref[0], preferred_element_type=F32, precision=lax.Precision.HIGHEST) + b_ref[0]


def _ada_call(c, c_ctx, ada_w, ada_b):
    depth, _, width = ada_w.shape
    rows = jnp.concatenate([c, c_ctx[None, :], jnp.zeros((ADA_ROWS - BATCH - 1, D_MODEL), F32)], axis=0)
    return pl.pallas_call(
        _ada_kernel, grid=(depth, width // ADA_TN),
        in_specs=[pl.BlockSpec((ADA_ROWS, D_MODEL), lambda l, n: (0, 0)),
                  pl.BlockSpec((1, D_MODEL, ADA_TN), lambda l, n: (l, 0, n)),
                  pl.BlockSpec((1, 1, ADA_TN), lambda l, n: (l, 0, n))],
        out_specs=pl.BlockSpec((1, ADA_ROWS, ADA_TN), lambda l, n: (l, 0, n)),
        out_shape=jax.ShapeDtypeStruct((depth, ADA_ROWS, width), F32),
        compiler_params=_params(("parallel", "arbitrary")), name="adaln_modulation",
    )(rows, ada_w, ada_b[:, None, :])


A0_TL = 256
A0_NT = SEQ // A0_TL


def _rope_perm():
    return np.concatenate([np.arange(0, QK_ROPE, 2), np.arange(1, QK_ROPE, 2)])


def _rope_tables():
    t = np.arange(SEQ)
    row = (t // GRID_W).astype(np.float64)
    col = (t % GRID_W).astype(np.float64)
    n_freq = QK_ROPE // 4
    inv = ROPE_BASE ** (-np.arange(n_freq, dtype=np.float64) / n_freq)
    ang = np.concatenate([row[:, None] * inv, col[:, None] * inv], axis=-1)
    cos, sin = np.cos(ang), np.sin(ang)
    half = QK_ROPE // 2
    a = np.zeros((KV_LEN, HEAD_PAD))
    b = np.zeros((KV_LEN, HEAD_PAD))
    a[:, :QK_HEAD] = 1.0
    a[:SEQ, QK_NOPE:QK_NOPE + half] = cos
    a[:SEQ, QK_NOPE + half:QK_HEAD] = cos
    b[:SEQ, QK_NOPE:QK_NOPE + half] = -sin
    b[:SEQ, QK_NOPE + half:QK_HEAD] = sin
    return a, b


def _norm_rope_heads(f, gain_ref, a, b, out_ref):
    width = MLA_HEADS * HEAD_PAD
    for hd in range(MLA_HEADS):
        sl = slice(hd * HEAD_PAD, (hd + 1) * HEAD_PAD)
        x = f[:, sl]
        r = lax.rsqrt(jnp.sum(x * x, axis=-1, keepdims=True) * (1.0 / QK_HEAD) + EPS)
        rot = x * (a * gain_ref[:, sl]) + f[:, width + hd * HEAD_PAD:width + (hd + 1) * HEAD_PAD] * b
        out_ref[0, :, sl] = (rot * r).astype(BF16)


def _a0_kernel(x_ref, ctx_ref, sh_ref, sc_ref, shc_ref, scc_ref, win_ref, qn_ref, wuq_ref, kvn_ref,
               wk_ref, wuv_ref, qg_ref, kg_ref, ka_ref, kb_ref, qa_ref, qb_ref,
               q_ref, k_ref, v_ref, u_ref):
    j = pl.program_id(1)
    is_ctx = j == A0_NT
    xin = jnp.where(is_ctx, ctx_ref[0], x_ref[0])
    shift = jnp.where(is_ctx, shc_ref[...], sh_ref[0])
    scale = jnp.where(is_ctx, scc_ref[...], sc_ref[0])
    h = _norm_mod(xin, shift, scale).astype(BF16)
    proj = jnp.dot(h, win_ref[...], preferred_element_type=F32)
    u_ref[0] = proj[:, 512:].astype(BF16)

    c_kv = _rms(proj[:, Q_LORA:Q_LORA + KV_LORA], kvn_ref[...], KV_LORA).astype(BF16)
    lane = lax.broadcasted_iota(jnp.int32, (1, MLA_HEADS * HEAD_PAD), 1)
    ones_lane = (lane % HEAD_PAD == V_HEAD).astype(F32)
    v_ref[0] = (jnp.dot(c_kv, wuv_ref[...], preferred_element_type=F32) + ones_lane).astype(BF16)
    kin = jnp.concatenate([c_kv, proj[:, 384:512].astype(BF16)], axis=1)
    kf = jnp.dot(kin, wk_ref[...], preferred_element_type=F32)
    _norm_rope_heads(kf, kg_ref, ka_ref[...], kb_ref[...], k_ref)

    @pl.when(j < A0_NT)
    def _():
        ql = _rms(proj[:, :Q_LORA], qn_ref[...], Q_LORA).astype(BF16)
        qf = jnp.dot(ql, wuq_ref[...], preferred_element_type=F32)
        _norm_rope_heads(qf, qg_ref, qa_ref[...], qb_ref[...], q_ref)


def _a0_call(x, ctx, sh, sc, shc, scc, w):
    nt = A0_NT
    lat = lambda b, j: (b, jnp.minimum(j, nt - 1), 0)
    full2 = lambda b, j: (0, 0)
    per_b = lambda b, j: (b, 0, 0)
    tab = pl.BlockSpec((A0_TL, HEAD_PAD), lambda b, j: (j, 0))
    in_specs = [
        pl.BlockSpec((1, A0_TL, D_MODEL), lat),
        pl.BlockSpec((1, CTX_LEN, D_MODEL), per_b),
        pl.BlockSpec((1, 1, D_MODEL), per_b),
        pl.BlockSpec((1, 1, D_MODEL), per_b),
        pl.BlockSpec((1, D_MODEL), full2),
        pl.BlockSpec((1, D_MODEL), full2),
        pl.BlockSpec((D_MODEL, 1024), full2),
        pl.BlockSpec((1, Q_LORA), full2),
        pl.BlockSpec((Q_LORA, 2 * MLA_HEADS * HEAD_PAD), full2),
        pl.BlockSpec((1, KV_LORA), full2),
        pl.BlockSpec((2 * KV_LORA, 2 * MLA_HEADS * HEAD_PAD), full2),
        pl.BlockSpec((KV_LORA, MLA_HEADS * HEAD_PAD), full2),
        pl.BlockSpec((1, MLA_HEADS * HEAD_PAD), full2),
        pl.BlockSpec((1, MLA_HEADS * HEAD_PAD), full2),
        tab, tab, tab, tab,
    ]
    out_specs = [
        pl.BlockSpec((1, A0_TL, MLA_HEADS * HEAD_PAD), lat),
        pl.BlockSpec((1, A0_TL, MLA_HEADS * HEAD_PAD), lambda b, j: (b, j, 0)),
        pl.BlockSpec((1, A0_TL, MLA_HEADS * HEAD_PAD), lambda b, j: (b, j, 0)),
        pl.BlockSpec((1, A0_TL, S5_WIDTH), lambda b, j: (b, j, 0)),
    ]
    out_shape = [
        jax.ShapeDtypeStruct((BATCH, SEQ, MLA_HEADS * HEAD_PAD), BF16),
        jax.ShapeDtypeStruct((BATCH, KV_LEN, MLA_HEADS * HEAD_PAD), BF16),
        jax.ShapeDtypeStruct((BATCH, KV_LEN, MLA_HEADS * HEAD_PAD), BF16),
        jax.ShapeDtypeStruct((BATCH, KV_LEN, S5_WIDTH), BF16),
    ]
    return pl.pallas_call(
        _a0_kernel, grid=(BATCH, nt + 1), in_specs=in_specs, out_specs=out_specs, out_shape=out_shape,
        compiler_params=_params(("parallel", "arbitrary")), name="even_in_proj",
    )(x, ctx, sh, sc, shc, scc, *w)


def _gain_swap(w, gain):
    half = QK_ROPE // 2
    wg = (w * gain).reshape(w.shape[0], MLA_HEADS, HEAD_PAD)
    re, im = wg[..., QK_NOPE:QK_NOPE + half], wg[..., QK_NOPE + half:QK_HEAD]
    out = jnp.concatenate([jnp.zeros_like(wg[..., :QK_NOPE]), im, re, jnp.zeros_like(wg[..., QK_HEAD:])], axis=-1)
    return out.reshape(w.shape)


def _a0_weights(w_in, q_norm, w_uq, kv_norm, w_ukv, q_qk, k_qk):
    perm = _rope_perm()
    kr0 = Q_LORA + KV_LORA
    w_cat = jnp.concatenate([
        w_in[:, :kr0], w_in[:, kr0:kr0 + QK_ROPE][:, perm],
        jnp.zeros((D_MODEL, HEAD_PAD - QK_ROPE), F32), w_in[:, kr0 + QK_ROPE:]], axis=1).astype(BF16)
    pad = HEAD_PAD - QK_HEAD

    def head_gain(g):
        gh = jnp.concatenate([g[:QK_NOPE], g[QK_NOPE:][perm], jnp.zeros((pad,), F32)])
        return jnp.tile(gh, MLA_HEADS)[None, :]

    uq = w_uq.reshape(Q_LORA, MLA_HEADS, QK_HEAD)
    uq = jnp.concatenate([uq[..., :QK_NOPE], uq[..., QK_NOPE:][..., perm],
                          jnp.zeros((Q_LORA, MLA_HEADS, pad), F32)], axis=-1)
    uq = uq.reshape(Q_LORA, MLA_HEADS * HEAD_PAD)
    uq = jnp.concatenate([uq, _gain_swap(uq, head_gain(q_qk))], axis=1).astype(BF16)
    ukv = w_ukv.reshape(KV_LORA, MLA_HEADS, QK_NOPE + V_HEAD)
    uk = jnp.concatenate([ukv[..., :QK_NOPE], jnp.zeros((KV_LORA, MLA_HEADS, HEAD_PAD - QK_NOPE), F32)], axis=-1)
    uk = uk.reshape(KV_LORA, MLA_HEADS * HEAD_PAD)
    place = np.zeros((KV_LORA, MLA_HEADS, HEAD_PAD), np.float32)
    for i in range(QK_ROPE):
        place[i, :, QK_NOPE + i] = 1.0
    wk = jnp.concatenate([uk, jnp.asarray(place.reshape(KV_LORA, MLA_HEADS * HEAD_PAD))], axis=0)
    wk = jnp.concatenate([wk, _gain_swap(wk, head_gain(k_qk))], axis=1).astype(BF16)
    wuv = jnp.concatenate([ukv[..., QK_NOPE:], jnp.zeros((KV_LORA, MLA_HEADS, HEAD_PAD - V_HEAD), F32)], axis=-1)
    wuv = wuv.reshape(KV_LORA, MLA_HEADS * HEAD_PAD).astype(BF16)
    a, b = _rope_tables()
    qs = QK_HEAD ** -0.5 * math.log2(math.e)
    tabs = [jnp.asarray(t, F32) for t in (a, b, a * qs, b * qs)]
    return [w_cat, q_norm[None, :], uq, kv_norm[None, :], wk, wuv, head_gain(q_qk), head_gain(k_qk)] + tabs


ATT_TQ = 256
ATT_KC = 256
HEADS_PER_STEP = 4


def _attn_kernel(q_ref, k_ref, v_ref, o_ref):
    for hh in range(HEADS_PER_STEP):
        sl = slice(hh * HEAD_PAD, (hh + 1) * HEAD_PAD)
        q = q_ref[0, :, sl]
        m = jnp.full((ATT_TQ, 1), -jnp.inf, F32)
        acc = jnp.zeros((ATT_TQ, HEAD_PAD), F32)
        for c in range(KV_LEN // ATT_KC):
            rows = slice(c * ATT_KC, (c + 1) * ATT_KC)
            s = lax.dot_general(q, k_ref[0, rows, sl], (((1,), (1,)), ((), ())), preferred_element_type=F32)
            m_new = jnp.maximum(m, jnp.max(s, axis=-1, keepdims=True))
            p = jnp.exp2(s - m_new).astype(BF16)
            acc = acc * jnp.exp2(m - m_new) + jnp.dot(p, v_ref[0, rows, sl], preferred_element_type=F32)
            m = m_new
        o_ref[0, :, sl] = (acc * (1.0 / acc[:, V_HEAD:V_HEAD + 1])).astype(BF16)


def _attn_call(q, k, v):
    wq = HEADS_PER_STEP * HEAD_PAD
    return pl.pallas_call(
        _attn_kernel, grid=(BATCH, MLA_HEADS // HEADS_PER_STEP, SEQ // ATT_TQ),
        in_specs=[pl.BlockSpec((1, ATT_TQ, wq), lambda b, h, i: (b, i, h)),
                  pl.BlockSpec((1, KV_LEN, wq), lambda b, h, i: (b, 0, h)),
                  pl.BlockSpec((1, KV_LEN, wq), lambda b, h, i: (b, 0, h))],
        out_specs=pl.BlockSpec((1, ATT_TQ, wq), lambda b, h, i: (b, i, h)),
        out_shape=jax.ShapeDtypeStruct((BATCH, SEQ, MLA_HEADS * HEAD_PAD), BF16),
        compiler_params=_params(("parallel", "parallel", "arbitrary")), name="mla_attention",
    )(q, k, v)


S5_ROWS = S5_NCHUNK * BATCH
S5_ROWS_LAT = S5_NCHUNK_LAT * BATCH
S5_COLS = S5_CHUNK * S5_GROUP
S5_SW = 2 * S5_STATE


def _s5_kernel(u_ref, r_ref, mb_ref, mc_ref, coef_ref, y_ref, x_sc, sp_sc, t_sc):
    u = u_ref[0]
    lags = r_ref[0]
    for sg in range(S5_CHUNK):
        off = (S5_CHUNK - 1 - sg) * S5_GROUP
        t_sc[sg * S5_GROUP:(sg + 1) * S5_GROUP, :] = lags[:, off:off + S5_COLS].astype(BF16)
    x_sc[...] = jnp.dot(u, mb_ref[0], preferred_element_type=F32)
    cf = coef_ref[0]
    af, bfm, bfp, ab, bbm, bbp = [cf[i * 8:(i + 1) * 8] for i in range(6)]

    def body(i, carry):
        sf, sfw, sb, sbw = carry
        cfw = jnp.where(i < S5_NCHUNK_CTX, i + S5_NCHUNK_LAT, i - S5_NCHUNK_CTX)
        rf = pl.multiple_of(cfw * BATCH, BATCH)
        rb = pl.multiple_of((S5_NCHUNK - 1 - i) * BATCH, BATCH)
        sp_sc[pl.ds(rf, BATCH), 0:S5_SW] = sf
        sp_sc[pl.ds(rb, BATCH), S5_SW:2 * S5_SW] = sb
        xf = x_sc[pl.ds(rf, BATCH), 0:S5_SW]
        xfw = x_sc[pl.ds(rf, BATCH), S5_SW:2 * S5_SW]
        xb = x_sc[pl.ds(rb, BATCH), 2 * S5_SW:3 * S5_SW]
        xbw = x_sc[pl.ds(rb, BATCH), 3 * S5_SW:4 * S5_SW]
        return (sf * af + sfw * bfm + xf, sfw * af + sf * bfp + xfw,
                sb * ab + sbw * bbm + xb, sbw * ab + sb * bbp + xbw)

    z = jnp.zeros((BATCH, S5_SW), F32)
    lax.fori_loop(0, S5_NCHUNK, body, (z, z, z, z))
    y = jnp.dot(u[:S5_ROWS_LAT], t_sc[...], preferred_element_type=F32)
    y = y + jnp.dot(sp_sc[0:S5_ROWS_LAT, :].astype(BF16), mc_ref[0], preferred_element_type=F32)
    y_ref[0] = y.astype(BF16)


def _s5_call(ug, t, mb, mc, coef):
    g3 = lambda g: (g, 0, 0)
    return pl.pallas_call(
        _s5_kernel, grid=(S5_GROUPS,),
        in_specs=[pl.BlockSpec((1, S5_ROWS, S5_COLS), g3),
                  pl.BlockSpec((1, S5_GROUP, 2 * S5_COLS), g3),
                  pl.BlockSpec((1, S5_COLS, 4 * S5_SW), g3),
                  pl.BlockSpec((1, 2 * S5_SW, S5_COLS), g3),
                  pl.BlockSpec((1, 6 * 8, S5_SW), g3)],
        out_specs=pl.BlockSpec((1, S5_ROWS_LAT, S5_COLS), g3),
        out_shape=jax.ShapeDtypeStruct((S5_GROUPS, S5_ROWS_LAT, S5_COLS), BF16),
        scratch_shapes=[pltpu.VMEM((S5_ROWS, 4 * S5_SW), F32), pltpu.VMEM((S5_ROWS, 2 * S5_SW), F32),
                        pltpu.VMEM((S5_COLS, S5_COLS), BF16)],
        compiler_params=_params(("parallel",)), name="s5_chunked_scan",
    )(ug, t, mb, mc, coef)


def _s5_weights(lam_re, lam_im, log_step, b_re, b_im, c_re, c_im):
    q = S5_CHUNK
    hi = lax.Precision.HIGHEST
    sig = np.arange(q)
    dirs = np.arange(2)[None, :]
    lr = jnp.minimum(lam_re, -1e-4)
    li = lam_im
    step = jnp.exp(log_step)[..., None]
    jj = jnp.arange(q + 1, dtype=F32)[:, None, None, None]
    mag = jnp.exp(lr * step * jj)
    ph = li * step * jj
    pr, pi = mag * jnp.cos(ph), mag * jnp.sin(ph)
    nr, ni = pr[1] - 1.0, pi[1]
    den = lr * lr + li * li
    fr, fi = (nr * lr + ni * li) / den, (ni * lr - nr * li) / den
    br = fr[..., None] * b_re - fi[..., None] * b_im
    bi = fr[..., None] * b_im + fi[..., None] * b_re
    cpr = c_re[None] * pr[:, :, :, None, :] - c_im[None] * pi[:, :, :, None, :]
    cpi = c_re[None] * pi[:, :, :, None, :] + c_im[None] * pr[:, :, :, None, :]
    kern = jnp.einsum('jdghp,dgpk->jdghk', jnp.concatenate([cpr[:q], -cpi[:q]], axis=-1),
                      jnp.concatenate([br, bi], axis=2), precision=hi)
    kt = kern.transpose(1, 2, 4, 0, 3)
    zero_slots = jnp.zeros((S5_GROUPS, S5_GROUP, q, S5_GROUP), F32)
    t = (jnp.concatenate([zero_slots[:, :, :q - 1], kt[0], zero_slots[:, :, :1]], axis=2)
         + jnp.concatenate([kt[1, :, :, ::-1], zero_slots], axis=2)).reshape(S5_GROUPS, S5_GROUP, 2 * S5_COLS)
    pw = np.stack([q - 1 - sig, sig], axis=1)
    po = np.stack([sig + 1, q - sig], axis=1)
    ppr, ppi = pr[pw, dirs][..., None], pi[pw, dirs][..., None]
    xr = (ppr * br[None] - ppi * bi[None]).transpose(1, 2, 0, 4, 3).reshape(2, S5_GROUPS, S5_COLS, S5_STATE)
    xi = (ppr * bi[None] + ppi * br[None]).transpose(1, 2, 0, 4, 3).reshape(2, S5_GROUPS, S5_COLS, S5_STATE)
    mb = jnp.concatenate([xr[0], xi[0], xi[0], xr[0], xr[1], xi[1], xi[1], xr[1]], axis=-1).astype(BF16)
    mr = cpr[po, dirs].transpose(1, 2, 4, 0, 3).reshape(2, S5_GROUPS, S5_STATE, S5_COLS)
    mi = -cpi[po, dirs].transpose(1, 2, 4, 0, 3).reshape(2, S5_GROUPS, S5_STATE, S5_COLS)
    mc = jnp.concatenate([mr[0], mi[0], mr[1], mi[1]], axis=1).astype(BF16)
    are, aim = pr[q], pi[q]
    rows = jnp.stack([jnp.concatenate([are, are], -1), jnp.concatenate([-aim, aim], -1),
                      jnp.concatenate([aim, -aim], -1)], axis=1)
    coef = jnp.broadcast_to(rows.transpose(2, 0, 1, 3)[:, :, :, None, :], (S5_GROUPS, 2, 3, 8, S5_SW))
    return t, mb, mc, coef.reshape(S5_GROUPS, 6 * 8, S5_SW)


A1_TL = MOE_TL


def _a1_kernel(att_ref, ys_ref, u_ref, d_ref, gw_ref, gb_ref, woa_ref, wos_ref, x_ref, g_ref, *route):
    o_ref = route[5]
    y = u_ref[0].astype(F32) * d_ref[...] + ys_ref[0].astype(F32)
    z = jax.nn.gelu(y)
    gate = jax.nn.sigmoid(jnp.dot(z.astype(BF16), gw_ref[...], preferred_element_type=F32) + gb_ref[...])
    s5 = (z * gate).astype(BF16)
    mix = jnp.dot(att_ref[0], woa_ref[...], preferred_element_type=F32)
    mix = mix + jnp.dot(s5, wos_ref[...], preferred_element_type=F32)
    x_new = x_ref[0] + g_ref[0] * mix
    o_ref[0] = x_new
    _route_tile(x_new, *route[:5], *route[6:])


def _a1_call(att, ys, u, d, gw, gb, woa, wos, x, g1, route):
    r_args, r_in, r_out, r_shape, r_scratch = route
    t3 = lambda b, j: (b, j, 0)
    full2 = lambda b, j: (0, 0)
    return pl.pallas_call(
        _a1_kernel, grid=(BATCH, SEQ // A1_TL),
        in_specs=[pl.BlockSpec((1, A1_TL, MLA_HEADS * HEAD_PAD), t3),
                  pl.BlockSpec((1, A1_TL, S5_WIDTH), t3),
                  pl.BlockSpec((1, A1_TL, S5_WIDTH), t3),
                  pl.BlockSpec((1, S5_WIDTH), full2),
                  pl.BlockSpec((S5_WIDTH, S5_WIDTH), full2),
                  pl.BlockSpec((1, S5_WIDTH), full2),
                  pl.BlockSpec((MLA_HEADS * HEAD_PAD, D_MODEL), full2),
                  pl.BlockSpec((S5_WIDTH, D_MODEL), full2),
                  pl.BlockSpec((1, A1_TL, D_MODEL), t3),
                  pl.BlockSpec((1, 1, D_MODEL), lambda b, j: (b, 0, 0))] + r_in,
        out_specs=[pl.BlockSpec((1, A1_TL, D_MODEL), t3)] + r_out,
        out_shape=[jax.ShapeDtypeStruct((BATCH, SEQ, D_MODEL), F32)] + r_shape,
        scratch_shapes=r_scratch,
        compiler_params=_params(("arbitrary", "arbitrary")), name="even_out_proj_route",
    )(att, ys, u, d, gw, gb, woa, wos, x, g1, *r_args)


SLOT_PAD = 8


def _pack_rows(v):
    halves = []
    for p in range(2):
        base = 2 * p * ROW_WORDS
        a = pltpu.bitcast(v[:, base:base + ROW_WORDS].astype(BF16).astype(F32), jnp.uint32)
        b = pltpu.bitcast(v[:, base + ROW_WORDS:base + 2 * ROW_WORDS].astype(BF16).astype(F32), jnp.uint32)
        halves.append((a >> 16) | b)
    return halves


def _unpack_rows(lo, hi):
    out = []
    for w in (lo, hi):
        out.append(pltpu.bitcast(w << 16, F32))
        out.append(pltpu.bitcast(w & jnp.uint32(0xFFFF0000), F32))
    return out


def _route_tile(x, sh_ref, sc_ref, rwt_ref, rb_ref, tri_ref,
                hlo_ref, hhi_ref, idx_ref, wt_ref, rank_ref, cnt_ref, run_sc):
    @pl.when((pl.program_id(0) == 0) & (pl.program_id(1) == 0))
    def _():
        run_sc[...] = jnp.zeros_like(run_sc)

    h = _norm_mod(x, sh_ref[0], sc_ref[0])
    hb = h.astype(BF16)
    hlo_ref[0], hhi_ref[0] = _pack_rows(h)
    h_lo = (h - hb.astype(F32)).astype(BF16)
    rwt = rwt_ref[...]
    rw_hi = rwt.astype(BF16)
    rw_lo = (rwt - rw_hi.astype(F32)).astype(BF16)
    logits = lax.dot_general(jnp.concatenate([rw_hi, rw_lo, rw_hi], axis=1), jnp.concatenate([hb, hb, h_lo], axis=1),
                             (((1,), (1,)), ((), ())), preferred_element_type=F32)
    scores = jax.nn.sigmoid(logits)

    work = scores + rb_ref[...]
    expert = lax.broadcasted_iota(jnp.int32, work.shape, 0).astype(F32)
    hits, ids = [], []
    for _ in range(TOP_K):
        m = jnp.max(work, axis=0, keepdims=True)
        ik = jnp.min(jnp.where(work == m, expert, float(N_EXPERTS)), axis=0, keepdims=True)
        hit = expert == ik
        hits.append(hit)
        ids.append(ik)
        work = jnp.where(hit, -jnp.inf, work)
    mask = hits[0]
    for hit in hits[1:]:
        mask = jnp.logical_or(mask, hit)
    maskf = mask.astype(F32)
    before = jnp.dot(maskf.astype(BF16), tri_ref[...], preferred_element_type=F32) + run_sc[:, 0:1]
    sel = [jnp.sum(jnp.where(hit, scores, 0.0), axis=0, keepdims=True) for hit in hits]
    denom = sel[0]
    for s in sel[1:]:
        denom = denom + s
    ranks = [jnp.sum(jnp.where(hit, before, 0.0), axis=0, keepdims=True) for hit in hits]
    pad = [jnp.zeros_like(denom)] * (SLOT_PAD - TOP_K)
    idx_ref[...] = jnp.concatenate(ids + pad, axis=0).astype(jnp.int32)
    wt_ref[...] = jnp.concatenate([s / denom * ROUTE_SCALE for s in sel] + pad, axis=0)
    rank_ref[...] = jnp.concatenate(ranks + pad, axis=0).astype(jnp.int32)
    run_sc[...] += jnp.sum(maskf, axis=1, keepdims=True)
    cnt_ref[...] = run_sc[...]


def _route_plumbing(sh, sc, rw, rb):
    t3 = lambda b, j: (b, j, 0)
    full2 = lambda b, j: (0, 0)
    per_b = lambda b, j: (b, 0, 0)
    nt = SEQ // MOE_TL
    slots = lambda b, j: (0, b * nt + j)
    tri = jnp.asarray(np.triu(np.ones((MOE_TL, MOE_TL), np.float32), 1), BF16)
    args = [sh, sc, rw.T, rb[:, None], tri]
    in_specs = [pl.BlockSpec((1, 1, D_MODEL), per_b),
                pl.BlockSpec((1, 1, D_MODEL), per_b),
                pl.BlockSpec((N_EXPERTS, D_MODEL), full2),
                pl.BlockSpec((N_EXPERTS, 1), full2),
                pl.BlockSpec((MOE_TL, MOE_TL), full2)]
    out_specs = [pl.BlockSpec((1, MOE_TL, ROW_WORDS), t3),
                 pl.BlockSpec((1, MOE_TL, ROW_WORDS), t3),
                 pl.BlockSpec((SLOT_PAD, MOE_TL), slots),
                 pl.BlockSpec((SLOT_PAD, MOE_TL), slots),
                 pl.BlockSpec((SLOT_PAD, MOE_TL), slots),
                 pl.BlockSpec((N_EXPERTS, V7X_LANES), full2)]
    slot_i = jax.ShapeDtypeStruct((SLOT_PAD, BATCH * SEQ), jnp.int32)
    out_shape = [jax.ShapeDtypeStruct((BATCH, SEQ, ROW_WORDS), jnp.uint32),
                 jax.ShapeDtypeStruct((BATCH, SEQ, ROW_WORDS), jnp.uint32),
                 slot_i,
                 jax.ShapeDtypeStruct((SLOT_PAD, BATCH * SEQ), F32),
                 slot_i,
                 jax.ShapeDtypeStruct((N_EXPERTS, V7X_LANES), F32)]
    scratch = [pltpu.VMEM((N_EXPERTS, V7X_LANES), F32)]
    return args, in_specs, out_specs, out_shape, scratch


def _sc_mesh():
    return plsc.VectorSubcoreMesh(core_axis_name="c", subcore_axis_name="s")


def _sc_dispatch(h_words, dest, n_rows):
    n_tok = h_words.shape[0]

    @pl.kernel(out_type=jax.ShapeDtypeStruct((n_rows, ROW_WORDS), jnp.uint32), mesh=_sc_mesh(), scratch_types=[])
    def scatter_rows(h_hbm, i_hbm, o_hbm):
        def body(h_vmem, i_vmem):
            for k in range(TOP_K):
                pltpu.sync_copy(h_vmem, o_hbm.at[i_vmem.at[k]])

        pltpu.emit_pipeline(
            body, grid=(n_tok // SC_WINDOW,),
            in_specs=[pl.BlockSpec((SC_WINDOW, ROW_WORDS), index_map=lambda i: (i, 0)),
                      pl.BlockSpec((SLOT_PAD, SC_WINDOW), index_map=lambda i: (0, i))],
            out_specs=[],
            core_axis_name=("c", "s"), dimension_semantics=(pltpu.PARALLEL,),
        )(h_hbm, i_hbm)

    return scatter_rows(h_words, dest)


def _sc_collect(y_words, dest):
    n_tok = dest.shape[1]

    @pl.kernel(out_type=jax.ShapeDtypeStruct((TOP_K, n_tok, ROW_WORDS), jnp.uint32), mesh=_sc_mesh(),
               scratch_types=[])
    def gather_rows(y_hbm, i_hbm, o_hbm):
        def body(i_vmem, o_vmem):
            pltpu.sync_copy(y_hbm.at[i_vmem.at[0]], o_vmem.at[0])

        pltpu.emit_pipeline(
            body, grid=(TOP_K, n_tok // SC_WINDOW),
            in_specs=[pl.BlockSpec((1, SC_WINDOW), index_map=lambda k, i: (k, i))],
            out_specs=[pl.BlockSpec((1, SC_WINDOW, ROW_WORDS), index_map=lambda k, i: (k, i, 0))],
            core_axis_name=("c", "s"), dimension_semantics=(pltpu.PARALLEL, pltpu.PARALLEL),
        )(i_hbm, o_hbm)

    return gather_rows(y_words, dest)


def _expert_kernel(be_ref, nv_ref, xlo_ref, xhi_ref, wg_ref, wu_ref, wd_ref, ylo_ref, yhi_ref,
                   wg_sc, wu_sc, wd_sc):
    i = pl.program_id(0)
    nv = nv_ref[i]

    @pl.when(jnp.logical_or(i == 0, be_ref[i] != be_ref[jnp.maximum(i - 1, 0)]))
    def _():
        wg_sc[...] = wg_ref[0, 0].astype(BF16)
        wu_sc[...] = wu_ref[0, 0].astype(BF16)
        wd_sc[...] = wd_ref[0, 0].astype(BF16)

    @pl.when(nv > 0)
    def _():
        parts = _unpack_rows(xlo_ref[...], xhi_ref[...])
        xb = jnp.concatenate([p.astype(BF16) for p in parts], axis=1)
        live = lax.broadcasted_iota(jnp.int32, xb.shape, 0) < nv
        xb = jnp.where(live, xb, jnp.zeros_like(xb))
        hid = jax.nn.silu(jnp.dot(xb, wg_sc[...], preferred_element_type=F32))
        hid = hid * jnp.dot(xb, wu_sc[...], preferred_element_type=F32)
        y = jnp.dot(hid.astype(BF16), wd_sc[...], preferred_element_type=F32)
        ylo_ref[...], yhi_ref[...] = _pack_rows(y)

    @pl.when(nv == 0)
    def _():
        ylo_ref[...] = jnp.zeros_like(ylo_ref)
        yhi_ref[...] = jnp.zeros_like(yhi_ref)


def _expert_call(block_e, n_valid, xlo, xhi, wg, wu, wd, li):
    n_rows = xlo.shape[0]
    n_blocks = n_rows // MOE_BLOCK
    rows = pl.BlockSpec((MOE_BLOCK, ROW_WORDS), lambda i, be, nv: (i, 0))
    grid_spec = pltpu.PrefetchScalarGridSpec(
        num_scalar_prefetch=2, grid=(n_blocks,),
        in_specs=[rows, rows,
                  pl.BlockSpec((1, 1, D_MODEL, EXPERT_FF), lambda i, be, nv: (li, be[i], 0, 0)),
                  pl.BlockSpec((1, 1, D_MODEL, EXPERT_FF), lambda i, be, nv: (li, be[i], 0, 0)),
                  pl.BlockSpec((1, 1, EXPERT_FF, D_MODEL), lambda i, be, nv: (li, be[i], 0, 0))],
        out_specs=[rows, rows],
        scratch_shapes=[pltpu.VMEM((D_MODEL, EXPERT_FF), BF16), pltpu.VMEM((D_MODEL, EXPERT_FF), BF16),
                        pltpu.VMEM((EXPERT_FF, D_MODEL), BF16)])
    out = jax.ShapeDtypeStruct((n_rows, ROW_WORDS), jnp.uint32)
    return pl.pallas_call(
        _expert_kernel, grid_spec=grid_spec, out_shape=[out, out],
        compiler_params=_params(("arbitrary",)), name="moe_experts",
    )(block_e, n_valid, xlo, xhi, wg, wu, wd)


def _combine_kernel(ylo_ref, yhi_ref, w_ref, x_ref, sh_ref, sc_ref, g_ref, sg_ref, su_ref, sd_ref, o_ref):
    hb = _norm_mod(x_ref[0], sh_ref[0], sc_ref[0]).astype(BF16)
    hid = jax.nn.silu(jnp.dot(hb, sg_ref[...].astype(BF16), preferred_element_type=F32))
    hid = hid * jnp.dot(hb, su_ref[...].astype(BF16), preferred_element_type=F32)
    shared = jnp.dot(hid.astype(BF16), sd_ref[...].astype(BF16), preferred_element_type=F32)
    w = w_ref[0]
    acc = [None] * 4
    for k in range(TOP_K):
        wk = w[:, k:k + 1]
        for c, part in enumerate(_unpack_rows(ylo_ref[k], yhi_ref[k])):
            acc[c] = wk * part if acc[c] is None else acc[c] + wk * part
    for c in range(4):
        sl = slice(c * ROW_WORDS, (c + 1) * ROW_WORDS)
        o_ref[0, :, sl] = x_ref[0, :, sl] + g_ref[0, :, sl] * (acc[c] + shared[:, sl])


def _combine_call(ylo, yhi, wts, x, sh, sc, g2, sg, su, sd):
    t3 = lambda b, j: (b, j, 0)
    per_b = lambda b, j: (b, 0, 0)
    full2 = lambda b, j: (0, 0)
    ff = sg.shape[1]
    nt = SEQ // MOE_TL
    rows = pl.BlockSpec((TOP_K, MOE_TL, ROW_WORDS), lambda b, j: (0, b * nt + j, 0))
    return pl.pallas_call(
        _combine_kernel, grid=(BATCH, nt),
        in_specs=[rows, rows,
                  pl.BlockSpec((1, MOE_TL, SLOT_PAD), t3),
                  pl.BlockSpec((1, MOE_TL, D_MODEL), t3),
                  pl.BlockSpec((1, 1, D_MODEL), per_b),
                  pl.BlockSpec((1, 1, D_MODEL), per_b),
                  pl.BlockSpec((1, 1, D_MODEL), per_b),
                  pl.BlockSpec((D_MODEL, ff), full2),
                  pl.BlockSpec((D_MODEL, ff), full2),
                  pl.BlockSpec((ff, D_MODEL), full2)],
        out_specs=pl.BlockSpec((1, MOE_TL, D_MODEL), t3),
        out_shape=jax.ShapeDtypeStruct((BATCH, SEQ, D_MODEL), F32),
        compiler_params=_params(("parallel", "arbitrary")), name="moe_combine_shared",
    )(ylo, yhi, wts, x, sh, sc, g2, sg, su, sd)


def _moe(x, routed, sh, sc, g2, w_gate, w_up, w_down, sh_gate, sh_up, sh_down, li):
    T = BATCH * SEQ
    TK = T * TOP_K
    hlo, hhi, idx, wts, rank, counts = routed
    wts = wts.T.reshape(BATCH, SEQ, SLOT_PAD)
    counts = counts[:, 0].astype(jnp.int32)
    padded = (counts + MOE_BLOCK - 1) // MOE_BLOCK * MOE_BLOCK
    pad_end = jnp.cumsum(padded)
    pad_start = pad_end - padded
    n_blocks = -(-TK // MOE_BLOCK) + N_EXPERTS
    n_rows = n_blocks * MOE_BLOCK
    block_start = jnp.arange(n_blocks, dtype=jnp.int32) * MOE_BLOCK
    owns = jnp.logical_and(block_start[:, None] >= pad_start[None, :], block_start[:, None] < pad_end[None, :])
    owns = owns.astype(jnp.int32)
    experts = jnp.arange(N_EXPERTS, dtype=jnp.int32)[None, :]
    block_e = jnp.sum(owns * experts, axis=1) + (N_EXPERTS - 1) * (1 - jnp.sum(owns, axis=1))
    n_valid = jnp.sum(owns * (counts[None, :] - (block_start[:, None] - pad_start[None, :])), axis=1)
    n_valid = jnp.clip(n_valid, 0, MOE_BLOCK).astype(jnp.int32)
    first_row = jnp.sum(jnp.where(idx[None] == experts.T[:, :, None], pad_start[:, None, None], 0), axis=0)
    dest = first_row + rank
    xlo = _sc_dispatch(hlo.reshape(T, ROW_WORDS), dest, n_rows)
    xhi = _sc_dispatch(hhi.reshape(T, ROW_WORDS), dest, n_rows)
    ylo, yhi = _expert_call(block_e, n_valid, xlo, xhi, w_gate, w_up, w_down, li)
    return _combine_call(_sc_collect(ylo, dest), _sc_collect(yhi, dest), wts, x, sh, sc, g2,
                         sh_gate, sh_up, sh_down)


HY_TL = MOE_TL
HALO = 8


def _hy_in_kernel(x_ref, xp_ref, xn_ref, sh_ref, sc_ref, w_ref, cw_ref, cb_ref, z_ref, x0_ref, h_sc):
    j = pl.program_id(1)
    shift, scale = sh_ref[0], sc_ref[0]
    keep_prev = (j > 0).astype(F32)
    keep_next = (j < SEQ // HY_TL - 1).astype(F32)
    h_sc[0:HALO, :] = _norm_mod(xp_ref[0], shift, scale) * keep_prev
    h_sc[HALO:HALO + HY_TL, :] = _norm_mod(x_ref[0], shift, scale)
    h_sc[HALO + HY_TL:, :] = _norm_mod(xn_ref[0], shift, scale) * keep_next
    hcat = h_sc[...].astype(BF16)
    outs = []
    for part in range(3):
        sl = slice(part * HY_WIDTH, (part + 1) * HY_WIDTH)
        p = jnp.dot(hcat, w_ref[:, sl], preferred_element_type=F32)
        o = (p[HALO - 1:HALO - 1 + HY_TL] * cw_ref[0:1, sl] + p[HALO:HALO + HY_TL] * cw_ref[1:2, sl]
             + p[HALO + 1:HALO + 1 + HY_TL] * cw_ref[2:3, sl] + cb_ref[:, sl])
        outs.append(o)
    x0_ref[0] = outs[0].astype(BF16)
    z_ref[0] = (outs[2] * outs[1]).astype(BF16)


def _hy_in_call(x, sh, sc, w, cw, cb):
    nb8 = HY_TL // HALO
    t3 = lambda b, j: (b, j, 0)
    full2 = lambda b, j: (0, 0)
    per_b = lambda b, j: (b, 0, 0)
    return pl.pallas_call(
        _hy_in_kernel, grid=(BATCH, SEQ // HY_TL),
        in_specs=[pl.BlockSpec((1, HY_TL, D_MODEL), t3),
                  pl.BlockSpec((1, HALO, D_MODEL), lambda b, j: (b, jnp.maximum(j * nb8 - 1, 0), 0)),
                  pl.BlockSpec((1, HALO, D_MODEL), lambda b, j: (b, jnp.minimum((j + 1) * nb8, SEQ // HALO - 1), 0)),
                  pl.BlockSpec((1, 1, D_MODEL), per_b),
                  pl.BlockSpec((1, 1, D_MODEL), per_b),
                  pl.BlockSpec((D_MODEL, 3 * HY_WIDTH), full2),
                  pl.BlockSpec((SHORT_CONV, 3 * HY_WIDTH), full2),
                  pl.BlockSpec((1, 3 * HY_WIDTH), full2)],
        out_specs=[pl.BlockSpec((1, HY_TL, HY_WIDTH), t3), pl.BlockSpec((1, HY_TL, HY_WIDTH), t3)],
        out_shape=[jax.ShapeDtypeStruct((BATCH, SEQ, HY_WIDTH), BF16),
                   jax.ShapeDtypeStruct((BATCH, SEQ, HY_WIDTH), BF16)],
        scratch_shapes=[pltpu.VMEM((HY_TL + 2 * HALO, D_MODEL), F32)],
        compiler_params=_params(("parallel", "arbitrary")), name="hyena_in_proj",
    )(x, x, x, sh, sc, w, cw, cb)


def _fft_tables():
    c = np.arange(FFT_N2, dtype=np.int64)
    ang = 2.0 * np.pi * ((c[:, None] * c[None, :]) % FFT_N2) / FFT_N2
    sr, si = np.cos(ang), -np.sin(ang)
    m = np.block([[sr, -si], [si, sr]])
    k1 = np.arange(FFT_NK, dtype=np.int64)
    ang_t = 2.0 * np.pi * (k1[:, None] * c[None, :]) / DFT_N
    lanes = np.ones((1, 1, V7X_LANES))
    tr = np.cos(ang_t)[:, :, None] * lanes
    ti = -np.sin(ang_t)[:, :, None] * lanes
    return jnp.asarray(m, BF16), jnp.asarray(tr, F32), jnp.asarray(ti, F32)


def _lin(acc, coef, val):
    if abs(coef) < 1e-12:
        return acc
    term = val if coef == 1.0 else (-val if coef == -1.0 else coef * val)
    return term if acc is None else acc + term


def _twiddle(tr_ref, ti_ref, k1, width):
    reps = width // V7X_LANES
    tr, ti = tr_ref[k1], ti_ref[k1]
    return jnp.concatenate([tr] * reps, axis=1), jnp.concatenate([ti] * reps, axis=1)


_COS_PI_4 = math.sqrt(0.5)


def _blocks_to_classes(z):
    assert FFT_N1 == 16
    sp = [z[a] + z[a + 4] for a in range(4)]
    sm = [z[a] - z[a + 4] for a in range(4)]
    q0p, q0m, q1p, q1m = sp[0] + sp[2], sp[0] - sp[2], sp[1] + sp[3], sp[1] - sp[3]
    d, e = _COS_PI_4 * (sm[1] - sm[3]), _COS_PI_4 * (sm[1] + sm[3])
    out = [None] * FFT_NK
    out[0] = (q0p + q1p, None)
    out[8] = (q0p - q1p, None)
    out[4] = (q0m, -q1m)
    out[2] = (sm[0] + d, -(sm[2] + e))
    out[6] = (sm[0] - d, sm[2] - e)
    for k in (1, 3):
        halves = []
        for parity in (0, 1):
            hr = hi = None
            for a in range(parity, FFT_NA, 2):
                th = 2.0 * math.pi * ((a * k) % FFT_N1) / FFT_N1
                hr = _lin(hr, round(math.cos(th), 15), z[a])
                hi = _lin(hi, round(-math.sin(th), 15), z[a])
            halves.append((hr, hi))
        (er, ei), (odr, odi) = halves
        out[k] = (er + odr, ei + odi)
        out[8 - k] = (er - odr, odi - ei)
    return out


def _classes_to_blocks(v):
    assert FFT_N1 == 16
    base = (v[0][0] + v[8][0], v[0][0] - v[8][0])
    pr = {k: (v[k][0] + v[8 - k][0], v[k][0] - v[8 - k][0]) for k in (1, 2, 3)}
    pi = {k: (v[k][1] - v[8 - k][1], v[k][1] + v[8 - k][1]) for k in (1, 2, 3)}
    out = []
    for a in range(FFT_NA):
        odd = a % 2
        mid = (v[4][0], 1.0 - (a % 4)) if not odd else (v[4][1], (a % 4) - 2.0)
        acc = _lin(base[odd], mid[1], mid[0])
        for k in (1, 2, 3):
            th = 2.0 * math.pi * ((a * k) % FFT_N1) / FFT_N1
            acc = _lin(acc, round(math.cos(th), 15), pr[k][odd])
            acc = _lin(acc, round(-math.sin(th), 15), pi[k][odd])
        out.append(acc)
    return out


def _class_forward(yr, yi, k1, m_ref, tr_ref, ti_ref, width):
    if k1 == 0:
        x = jnp.dot(m_ref[:, :FFT_N2], yr.astype(BF16), preferred_element_type=F32)
    else:
        tr, ti = _twiddle(tr_ref, ti_ref, k1, width)
        yr, yi = (yr * tr, yr * ti) if yi is None else (yr * tr - yi * ti, yr * ti + yi * tr)
        x = jnp.dot(m_ref[...], jnp.concatenate([yr, yi], axis=0).astype(BF16), preferred_element_type=F32)
    return x[:FFT_N2], x[FFT_N2:]


def _class_backward(pr, pi, k1, m_ref, tr_ref, ti_ref, width):
    v = jnp.dot(m_ref[...], jnp.concatenate([pr, -pi], axis=0).astype(BF16), preferred_element_type=F32)
    ur, ui = v[:FFT_N2], -v[FFT_N2:]
    if k1 > 0:
        tr, ti = _twiddle(tr_ref, ti_ref, k1, width)
        ur, ui = ur * tr + ui * ti, ui * tr - ur * ti
    return ur, ui


def _filter_features():
    t = np.linspace(0.0, 1.0, SEQ)[:, None]
    ang = 2.0 * np.pi * np.arange(SEQ)[:, None] / SEQ
    bands = np.linspace(1e-4, FILT_BANDS - 1, FILT_BANDS)
    z = np.concatenate([t, np.cos(bands * ang), -np.sin(bands * ang)], axis=-1)
    return jnp.asarray(np.pad(z, ((0, 0), (0, FILT_PAD - FILT_EMB))), F32)


def _spec_kernel(zf_ref, w1_ref, b1_ref, w2_ref, b2_ref, fq_ref, w3f_ref, w3b_ref, dl_ref, m_ref, tr_ref, ti_ref,
                 c_ref, hf_sc, hb_sc):
    hi = lax.Precision.HIGHEST
    zf, fq = zf_ref[...], fq_ref[...]
    hid = jnp.sin(fq * (jnp.dot(zf, w1_ref[...], preferred_element_type=F32, precision=hi) + b1_ref[...]))
    hid = jnp.sin(fq * (jnp.dot(hid, w2_ref[...], preferred_element_type=F32, precision=hi) + b2_ref[...]))
    decay = jnp.exp(-zf[:, 0:1] * dl_ref[...])
    hf = jnp.dot(hid, w3f_ref[...], preferred_element_type=F32, precision=hi) * decay
    hb = jnp.dot(hid, w3b_ref[...], preferred_element_type=F32, precision=hi) * decay
    ssq = jnp.sum(hf * hf, axis=0, keepdims=True) + jnp.sum(hb * hb, axis=0, keepdims=True)
    inv = lax.rsqrt(ssq + EPS)
    hf_sc[...] = hf * inv
    hb_sc[...] = hb * inv
    rows = lambda ref: [ref[a * FFT_N2:(a + 1) * FFT_N2, :] for a in range(FFT_NA)]
    fwd, bwd = _blocks_to_classes(rows(hf_sc)), _blocks_to_classes(rows(hb_sc))
    for k1 in range(FFT_NK):
        fr, fi = _class_forward(*fwd[k1], k1, m_ref, tr_ref, ti_ref, HY_CT)
        br, bi = _class_forward(*bwd[k1], k1, m_ref, tr_ref, ti_ref, HY_CT)
        scale = (1.0 if k1 in (0, FFT_N1 // 2) else 2.0) / DFT_N
        c_ref[k1, :FFT_N2, :] = ((fr + br) * scale).astype(BF16)
        c_ref[k1, FFT_N2:, :] = ((fi - bi) * scale).astype(BF16)


def _fft_table_specs(ngrid):
    z = (0,) * 2
    z3 = (0,) * 3
    if ngrid == 1:
        return [pl.BlockSpec((2 * FFT_N2, 2 * FFT_N2), lambda c: z),
                pl.BlockSpec((FFT_NK, FFT_N2, V7X_LANES), lambda c: z3),
                pl.BlockSpec((FFT_NK, FFT_N2, V7X_LANES), lambda c: z3)]
    return [pl.BlockSpec((2 * FFT_N2, 2 * FFT_N2), lambda b, c: z),
            pl.BlockSpec((FFT_NK, FFT_N2, V7X_LANES), lambda b, c: z3),
            pl.BlockSpec((FFT_NK, FFT_N2, V7X_LANES), lambda b, c: z3)]


def _spec_call(w1, b1, w2, b2, w3, freq, m, tr, ti):
    nct = HY_WIDTH // HY_CT

    def pad2(a, rows, cols):
        return jnp.pad(a, ((0, rows - a.shape[0]), (0, cols - a.shape[1])))

    full = lambda c: (0, 0)
    small = pl.BlockSpec((FILT_PAD, FILT_PAD), full)
    vec = pl.BlockSpec((1, FILT_PAD), full)
    deltas = jnp.asarray(np.linspace(HY_MIN_DECAY, HY_MAX_DECAY, HY_WIDTH)[None, :], F32)
    w3p = pad2(w3, FILT_PAD, 2 * HY_WIDTH)
    return pl.pallas_call(
        _spec_kernel, grid=(nct,),
        in_specs=[pl.BlockSpec((SEQ, FILT_PAD), full), small, vec, small, vec, vec,
                  pl.BlockSpec((FILT_PAD, HY_CT), lambda c: (0, c)),
                  pl.BlockSpec((FILT_PAD, HY_CT), lambda c: (0, c + nct)),
                  pl.BlockSpec((1, HY_CT), lambda c: (0, c))] + _fft_table_specs(1),
        out_specs=pl.BlockSpec((FFT_NK, 2 * FFT_N2, HY_CT), lambda c: (0, 0, c)),
        out_shape=jax.ShapeDtypeStruct((FFT_NK, 2 * FFT_N2, HY_WIDTH), BF16),
        scratch_shapes=[pltpu.VMEM((SEQ, HY_CT), F32), pltpu.VMEM((SEQ, HY_CT), F32)],
        compiler_params=_params(("arbitrary",)), name="hyena_filter_spectrum",
    )(_filter_features(), pad2(w1, FILT_PAD, FILT_PAD), pad2(b1[None, :], 1, FILT_PAD),
      pad2(w2, FILT_PAD, FILT_PAD), pad2(b2[None, :], 1, FILT_PAD), pad2(freq[None, :], 1, FILT_PAD),
      w3p, w3p, deltas, m, tr, ti)


def _conv_kernel(z_ref, c_ref, m_ref, tr_ref, ti_ref, y_ref, cls):
    for k1, (yr, yi) in enumerate(_blocks_to_classes(
            [z_ref[0, a * FFT_N2:(a + 1) * FFT_N2, :].astype(F32) for a in range(FFT_NA)])):
        cls[k1, :FFT_N2, :] = yr
        if yi is not None:
            cls[k1, FFT_N2:, :] = yi
    for k1 in range(FFT_NK):
        yi = None if k1 in (0, FFT_N1 // 2) else cls[k1, FFT_N2:, :]
        xr, xi = _class_forward(cls[k1, :FFT_N2, :], yi, k1, m_ref, tr_ref, ti_ref, HY_CT)
        cr = c_ref[k1, :FFT_N2, :].astype(F32)
        ci = c_ref[k1, FFT_N2:, :].astype(F32)
        ur, ui = _class_backward(xr * cr - xi * ci, xr * ci + xi * cr, k1, m_ref, tr_ref, ti_ref, HY_CT)
        cls[k1, :FFT_N2, :] = ur
        cls[k1, FFT_N2:, :] = ui
    blocks = _classes_to_blocks([(cls[k1, :FFT_N2, :], cls[k1, FFT_N2:, :]) for k1 in range(FFT_NK)])
    for a, ya in enumerate(blocks):
        y_ref[0, a * FFT_N2:(a + 1) * FFT_N2, :] = ya.astype(BF16)


def _conv_call(z, spec, m, tr, ti):
    return pl.pallas_call(
        _conv_kernel, grid=(HY_WIDTH // HY_CT, BATCH),
        in_specs=[pl.BlockSpec((1, SEQ, HY_CT), lambda c, b: (b, 0, c)),
                  pl.BlockSpec((FFT_NK, 2 * FFT_N2, HY_CT), lambda c, b: (0, 0, c))] + _fft_table_specs(2),
        out_specs=pl.BlockSpec((1, SEQ, HY_CT), lambda c, b: (b, 0, c)),
        out_shape=jax.ShapeDtypeStruct((BATCH, SEQ, HY_WIDTH), BF16),
        scratch_shapes=[pltpu.VMEM((FFT_NK, 2 * FFT_N2, HY_CT), F32)],
        compiler_params=_params(("parallel", "arbitrary")), name="hyena_long_conv",
    )(z, spec, m, tr, ti)


def _hy_out_kernel(y_ref, z_ref, x0_ref, b_ref, w_ref, x_ref, g_ref, *route):
    o_ref = route[5]
    z = z_ref[0].astype(F32)
    gated = x0_ref[0].astype(F32) * (y_ref[0].astype(F32) + b_ref[...] * z)
    mix = jnp.dot(gated.astype(BF16), w_ref[...], preferred_element_type=F32)
    x_new = x_ref[0] + g_ref[0] * mix
    o_ref[0] = x_new
    _route_tile(x_new, *route[:5], *route[6:])


def _hy_out_call(y, z, x0, bias, w, x, g1, route):
    r_args, r_in, r_out, r_shape, r_scratch = route
    t3 = lambda b, j: (b, j, 0)
    full2 = lambda b, j: (0, 0)
    return pl.pallas_call(
        _hy_out_kernel, grid=(BATCH, SEQ // HY_TL),
        in_specs=[pl.BlockSpec((1, HY_TL, HY_WIDTH), t3),
                  pl.BlockSpec((1, HY_TL, HY_WIDTH), t3),
                  pl.BlockSpec((1, HY_TL, HY_WIDTH), t3),
                  pl.BlockSpec((1, HY_WIDTH), full2),
                  pl.BlockSpec((HY_WIDTH, D_MODEL), full2),
                  pl.BlockSpec((1, HY_TL, D_MODEL), t3),
                  pl.BlockSpec((1, 1, D_MODEL), lambda b, j: (b, 0, 0))] + r_in,
        out_specs=[pl.BlockSpec((1, HY_TL, D_MODEL), t3)] + r_out,
        out_shape=[jax.ShapeDtypeStruct((BATCH, SEQ, D_MODEL), F32)] + r_shape,
        scratch_shapes=r_scratch,
        compiler_params=_params(("arbitrary", "arbitrary")), name="hyena_out_proj_route",
    )(y, z, x0, bias, w, x, g1, *r_args)


def kernel(x, c, ctx, c_ctx, ada_w, ada_b, ev_w_in, mla_q_norm, mla_w_uq, mla_kv_norm, mla_w_ukv, mla_q_qknorm, mla_k_qknorm, s5_lam_re, s5_lam_im, s5_log_step, s5_b_re, s5_b_im, s5_c_re, s5_c_im, s5_d, s5_glu_w, s5_glu_b, ev_w_out, hy_w_in, hy_conv_w, hy_conv_b, hy_f_w1, hy_f_b1, hy_f_w2, hy_f_b2, hy_f_w3, hy_f_freq, hy_bias, hy_w_out, moe_router_w, moe_router_b, moe_w_gate, moe_w_up, moe_w_down, moe_sh_gate, moe_sh_up, moe_sh_down):
    D = D_MODEL
    mod_all = _ada_call(c, c_ctx, ada_w, ada_b)

    def mods(li):
        return [m[:, None, :] for m in jnp.split(mod_all[li, :BATCH], 6, axis=-1)]

    sh1, sc1, g1, sh2, sc2, g2 = mods(0)
    mod_ctx = mod_all[0, BATCH, :2 * D]
    w0 = _a0_weights(ev_w_in[0], mla_q_norm[0], mla_w_uq[0], mla_kv_norm[0], mla_w_ukv[0],
                     mla_q_qknorm[0], mla_k_qknorm[0])
    q, k, v, u = _a0_call(x, ctx, sh1, sc1, mod_ctx[None, :D], mod_ctx[None, D:], w0)
    att = _attn_call(q, k, v)
    ug = u.reshape(BATCH, S5_NCHUNK, S5_CHUNK, S5_GROUPS, S5_GROUP)
    ug = ug.transpose(3, 1, 0, 2, 4).reshape(S5_GROUPS, S5_ROWS, S5_COLS)
    ys = _s5_call(ug, *_s5_weights(s5_lam_re[0], s5_lam_im[0], s5_log_step[0], s5_b_re[0], s5_b_im[0],
                                   s5_c_re[0], s5_c_im[0]))
    ys = ys.reshape(S5_GROUPS, S5_NCHUNK_LAT, BATCH, S5_CHUNK, S5_GROUP)
    ys = ys.transpose(2, 1, 3, 0, 4).reshape(BATCH, SEQ, S5_WIDTH)
    wo = ev_w_out[0].astype(BF16)
    wo_att = jnp.concatenate([wo[:MLA_WIDTH].reshape(MLA_HEADS, V_HEAD, D_MODEL),
                              jnp.zeros((MLA_HEADS, HEAD_PAD - V_HEAD, D_MODEL), BF16)], axis=1)
    x, *routed = _a1_call(att, ys, u, s5_d[0][None, :], s5_glu_w[0].astype(BF16), s5_glu_b[0][None, :],
                          wo_att.reshape(MLA_HEADS * HEAD_PAD, D_MODEL), wo[MLA_WIDTH:], x, g1,
                          _route_plumbing(sh2, sc2, moe_router_w[0], moe_router_b[0]))
    x = _moe(x, routed, sh2, sc2, g2, moe_w_gate, moe_w_up, moe_w_down,
             moe_sh_gate[0], moe_sh_up[0], moe_sh_down[0], 0)

    sh1, sc1, g1, sh2, sc2, g2 = mods(1)
    z, x0 = _hy_in_call(x, sh1, sc1, hy_w_in[0].astype(BF16), hy_conv_w[0], hy_conv_b[0][None, :])
    fft_tabs = _fft_tables()
    spectrum = _spec_call(hy_f_w1[0], hy_f_b1[0], hy_f_w2[0], hy_f_b2[0], hy_f_w3[0], hy_f_freq[0], *fft_tabs)
    y = _conv_call(z, spectrum, *fft_tabs)
    x, *routed = _hy_out_call(y, z, x0, hy_bias[0][None, :], hy_w_out[0].astype(BF16), x, g1,
                              _route_plumbing(sh2, sc2, moe_router_w[1], moe_router_b[1]))
    x = _moe(x, routed, sh2, sc2, g2, moe_w_gate, moe_w_up, moe_w_down,
             moe_sh_gate[1], moe_sh_up[1], moe_sh_down[1], 1)
    return x
```

```python
import math

import numpy as np
import jax
import jax.numpy as jnp
from jax import lax
from jax.experimental import pallas as pl
from jax.experimental.pallas import tpu as pltpu
from jax.experimental.pallas import tpu_sc as plsc

F32 = jnp.float32
BF16 = jnp.bfloat16

D_MODEL = 1024
BATCH = 8
SEQ = 4096
CTX_LEN = 256
KV_LEN = SEQ + CTX_LEN
GRID_W = 64
EPS = 1e-6

MLA_HEADS = 8
QK_NOPE = 64
QK_ROPE = 32
QK_HEAD = QK_NOPE + QK_ROPE
V_HEAD = 64
Q_LORA = 256
KV_LORA = 128
MLA_WIDTH = MLA_HEADS * V_HEAD
ROPE_BASE = 10000.0
HEAD_PAD = 128

S5_WIDTH = 512
S5_GROUP = 16
S5_GROUPS = S5_WIDTH // S5_GROUP
S5_STATE = 64
S5_CHUNK = 32
S5_NCHUNK = KV_LEN // S5_CHUNK
S5_NCHUNK_LAT = SEQ // S5_CHUNK
S5_NCHUNK_CTX = CTX_LEN // S5_CHUNK

HY_WIDTH = D_MODEL
FILT_EMB = 33
FILT_BANDS = (FILT_EMB - 1) // 2
FILT_PAD = 128
SHORT_CONV = 3
HY_MIN_DECAY = -math.log(1e-2) / 1.5
HY_MAX_DECAY = -math.log(1e-2) / 0.3
DFT_N = 2 * SEQ
FFT_N1 = 16
FFT_N2 = DFT_N // FFT_N1
FFT_NA = FFT_N1 // 2
FFT_NK = FFT_N1 // 2 + 1
HY_CT = 256

N_EXPERTS = 64
TOP_K = 6
EXPERT_FF = 256
ROUTE_SCALE = 2.5
MOE_BLOCK = 1024
MOE_TL = 512
ROW_WORDS = D_MODEL // 4
SC_WINDOW = 128

V7X_LANES = 128
V7X_VMEM_BYTES = 64 * 1024 * 1024
VMEM_LIMIT = V7X_VMEM_BYTES - 8 * 1024 * 1024


def _params(semantics):
    return pltpu.CompilerParams(dimension_semantics=semantics, vmem_limit_bytes=VMEM_LIMIT)


def _norm_mod(x, shift, scale):
    ms = jnp.mean(x * x, axis=-1, keepdims=True)
    return x * lax.rsqrt(ms + EPS) * (1.0 + scale) + shift


def _rms(x, gain, n):
    ms = jnp.sum(x * x, axis=-1, keepdims=True) * (1.0 / n)
    return x * lax.rsqrt(ms + EPS) * gain


ADA_ROWS = 16
ADA_TN = 1024


def _ada_kernel(c_ref, w_ref, b_ref, o_ref):
    sc = jax.nn.silu(c_ref[...])
    o_ref[0] = jnp.dot(sc, w_ref[0], preferred_element_type=F32, precision=lax.Precision.HIGHEST) + b_ref[0]


def _ada_call(c, c_ctx, ada_w, ada_b):
    depth, _, width = ada_w.shape
    rows = jnp.concatenate([c, c_ctx[None, :], jnp.zeros((ADA_ROWS - BATCH - 1, D_MODEL), F32)], axis=0)
    return pl.pallas_call(
        _ada_kernel, grid=(depth, width // ADA_TN),
        in_specs=[pl.BlockSpec((ADA_ROWS, D_MODEL), lambda l, n: (0, 0)),
                  pl.BlockSpec((1, D_MODEL, ADA_TN), lambda l, n: (l, 0, n)),
                  pl.BlockSpec((1, 1, ADA_TN), lambda l, n: (l, 0, n))],
        out_specs=pl.BlockSpec((1, ADA_ROWS, ADA_TN), lambda l, n: (l, 0, n)),
        out_shape=jax.ShapeDtypeStruct((depth, ADA_ROWS, width), F32),
        compiler_params=_params(("parallel", "arbitrary")), name="adaln_modulation",
    )(rows, ada_w, ada_b[:, None, :])


A0_TL = 256
A0_NT = SEQ // A0_TL


def _rope_perm():
    return np.concatenate([np.arange(0, QK_ROPE, 2), np.arange(1, QK_ROPE, 2)])


def _rope_tables():
    t = np.arange(SEQ)
    row = (t // GRID_W).astype(np.float64)
    col = (t % GRID_W).astype(np.float64)
    n_freq = QK_ROPE // 4
    inv = ROPE_BASE ** (-np.arange(n_freq, dtype=np.float64) / n_freq)
    ang = np.concatenate([row[:, None] * inv, col[:, None] * inv], axis=-1)
    cos, sin = np.cos(ang), np.sin(ang)
    half = QK_ROPE // 2
    a = np.zeros((KV_LEN, HEAD_PAD))
    b = np.zeros((KV_LEN, HEAD_PAD))
    a[:, :QK_HEAD] = 1.0
    a[:SEQ, QK_NOPE:QK_NOPE + half] = cos
    a[:SEQ, QK_NOPE + half:QK_HEAD] = cos
    b[:SEQ, QK_NOPE:QK_NOPE + half] = -sin
    b[:SEQ, QK_NOPE + half:QK_HEAD] = sin
    return a, b


def _norm_rope_heads(f, gain_ref, a, b, out_ref):
    width = MLA_HEADS * HEAD_PAD
    for hd in range(MLA_HEADS):
        sl = slice(hd * HEAD_PAD, (hd + 1) * HEAD_PAD)
        x = f[:, sl]
        r = lax.rsqrt(jnp.sum(x * x, axis=-1, keepdims=True) * (1.0 / QK_HEAD) + EPS)
        rot = x * (a * gain_ref[:, sl]) + f[:, width + hd * HEAD_PAD:width + (hd + 1) * HEAD_PAD] * b
        out_ref[0, :, sl] = (rot * r).astype(BF16)


def _a0_kernel(x_ref, ctx_ref, sh_ref, sc_ref, shc_ref, scc_ref, win_ref, qn_ref, wuq_ref, kvn_ref,
               wk_ref, wuv_ref, qg_ref, kg_ref, ka_ref, kb_ref, qa_ref, qb_ref,
               q_ref, k_ref, v_ref, u_ref):
    j = pl.program_id(1)
    is_ctx = j == A0_NT
    xin = jnp.where(is_ctx, ctx_ref[0], x_ref[0])
    shift = jnp.where(is_ctx, shc_ref[...], sh_ref[0])
    scale = jnp.where(is_ctx, scc_ref[...], sc_ref[0])
    h = _norm_mod(xin, shift, scale).astype(BF16)
    proj = jnp.dot(h, win_ref[...], preferred_element_type=F32)
    u_ref[0] = proj[:, 512:].astype(BF16)

    c_kv = _rms(proj[:, Q_LORA:Q_LORA + KV_LORA], kvn_ref[...], KV_LORA).astype(BF16)
    lane = lax.broadcasted_iota(jnp.int32, (1, MLA_HEADS * HEAD_PAD), 1)
    ones_lane = (lane % HEAD_PAD == V_HEAD).astype(F32)
    v_ref[0] = (jnp.dot(c_kv, wuv_ref[...], preferred_element_type=F32) + ones_lane).astype(BF16)
    kin = jnp.concatenate([c_kv, proj[:, 384:512].astype(BF16)], axis=1)
    kf = jnp.dot(kin, wk_ref[...], preferred_element_type=F32)
    _norm_rope_heads(kf, kg_ref, ka_ref[...], kb_ref[...], k_ref)

    @pl.when(j < A0_NT)
    def _():
        ql = _rms(proj[:, :Q_LORA], qn_ref[...], Q_LORA).astype(BF16)
        qf = jnp.dot(ql, wuq_ref[...], preferred_element_type=F32)
        _norm_rope_heads(qf, qg_ref, qa_ref[...], qb_ref[...], q_ref)


def _a0_call(x, ctx, sh, sc, shc, scc, w):
    nt = A0_NT
    lat = lambda b, j: (b, jnp.minimum(j, nt - 1), 0)
    full2 = lambda b, j: (0, 0)
    per_b = lambda b, j: (b, 0, 0)
    tab = pl.BlockSpec((A0_TL, HEAD_PAD), lambda b, j: (j, 0))
    in_specs = [
        pl.BlockSpec((1, A0_TL, D_MODEL), lat),
        pl.BlockSpec((1, CTX_LEN, D_MODEL), per_b),
        pl.BlockSpec((1, 1, D_MODEL), per_b),
        pl.BlockSpec((1, 1, D_MODEL), per_b),
        pl.BlockSpec((1, D_MODEL), full2),
        pl.BlockSpec((1, D_MODEL), full2),
        pl.BlockSpec((D_MODEL, 1024), full2),
        pl.BlockSpec((1, Q_LORA), full2),
        pl.BlockSpec((Q_LORA, 2 * MLA_HEADS * HEAD_PAD), full2),
        pl.BlockSpec((1, KV_LORA), full2),
        pl.BlockSpec((2 * KV_LORA, 2 * MLA_HEADS * HEAD_PAD), full2),
        pl.BlockSpec((KV_LORA, MLA_HEADS * HEAD_PAD), full2),
        pl.BlockSpec((1, MLA_HEADS * HEAD_PAD), full2),
        pl.BlockSpec((1, MLA_HEADS * HEAD_PAD), full2),
        tab, tab, tab, tab,
    ]
    out_specs = [
        pl.BlockSpec((1, A0_TL, MLA_HEADS * HEAD_PAD), lat),
        pl.BlockSpec((1, A0_TL, MLA_HEADS * HEAD_PAD), lambda b, j: (b, j, 0)),
        pl.BlockSpec((1, A0_TL, MLA_HEADS * HEAD_PAD), lambda b, j: (b, j, 0)),
        pl.BlockSpec((1, A0_TL, S5_WIDTH), lambda b, j: (b, j, 0)),
    ]
    out_shape = [
        jax.ShapeDtypeStruct((BATCH, SEQ, MLA_HEADS * HEAD_PAD), BF16),
        jax.ShapeDtypeStruct((BATCH, KV_LEN, MLA_HEADS * HEAD_PAD), BF16),
        jax.ShapeDtypeStruct((BATCH, KV_LEN, MLA_HEADS * HEAD_PAD), BF16),
        jax.ShapeDtypeStruct((BATCH, KV_LEN, S5_WIDTH), BF16),
    ]
    return pl.pallas_call(
        _a0_kernel, grid=(BATCH, nt + 1), in_specs=in_specs, out_specs=out_specs, out_shape=out_shape,
        compiler_params=_params(("parallel", "arbitrary")), name="even_in_proj",
    )(x, ctx, sh, sc, shc, scc, *w)


def _gain_swap(w, gain):
    half = QK_ROPE // 2
    wg = (w * gain).reshape(w.shape[0], MLA_HEADS, HEAD_PAD)
    re, im = wg[..., QK_NOPE:QK_NOPE + half], wg[..., QK_NOPE + half:QK_HEAD]
    out = jnp.concatenate([jnp.zeros_like(wg[..., :QK_NOPE]), im, re, jnp.zeros_like(wg[..., QK_HEAD:])], axis=-1)
    return out.reshape(w.shape)


def _a0_weights(w_in, q_norm, w_uq, kv_norm, w_ukv, q_qk, k_qk):
    perm = _rope_perm()
    kr0 = Q_LORA + KV_LORA
    w_cat = jnp.concatenate([
        w_in[:, :kr0], w_in[:, kr0:kr0 + QK_ROPE][:, perm],
        jnp.zeros((D_MODEL, HEAD_PAD - QK_ROPE), F32), w_in[:, kr0 + QK_ROPE:]], axis=1).astype(BF16)
    pad = HEAD_PAD - QK_HEAD

    def head_gain(g):
        gh = jnp.concatenate([g[:QK_NOPE], g[QK_NOPE:][perm], jnp.zeros((pad,), F32)])
        return jnp.tile(gh, MLA_HEADS)[None, :]

    uq = w_uq.reshape(Q_LORA, MLA_HEADS, QK_HEAD)
    uq = jnp.concatenate([uq[..., :QK_NOPE], uq[..., QK_NOPE:][..., perm],
                          jnp.zeros((Q_LORA, MLA_HEADS, pad), F32)], axis=-1)
    uq = uq.reshape(Q_LORA, MLA_HEADS * HEAD_PAD)
    uq = jnp.concatenate([uq, _gain_swap(uq, head_gain(q_qk))], axis=1).astype(BF16)
    ukv = w_ukv.reshape(KV_LORA, MLA_HEADS, QK_NOPE + V_HEAD)
    uk = jnp.concatenate([ukv[..., :QK_NOPE], jnp.zeros((KV_LORA, MLA_HEADS, HEAD_PAD - QK_NOPE), F32)], axis=-1)
    uk = uk.reshape(KV_LORA, MLA_HEADS * HEAD_PAD)
    place = np.zeros((KV_LORA, MLA_HEADS, HEAD_PAD), np.float32)
    for i in range(QK_ROPE):
        place[i, :, QK_NOPE + i] = 1.0
    wk = jnp.concatenate([uk, jnp.asarray(place.reshape(KV_LORA, MLA_HEADS * HEAD_PAD))], axis=0)
    wk = jnp.concatenate([wk, _gain_swap(wk, head_gain(k_qk))], axis=1).astype(BF16)
    wuv = jnp.concatenate([ukv[..., QK_NOPE:], jnp.zeros((KV_LORA, MLA_HEADS, HEAD_PAD - V_HEAD), F32)], axis=-1)
    wuv = wuv.reshape(KV_LORA, MLA_HEADS * HEAD_PAD).astype(BF16)
    a, b = _rope_tables()
    qs = QK_HEAD ** -0.5 * math.log2(math.e)
    tabs = [jnp.asarray(t, F32) for t in (a, b, a * qs, b * qs)]
    return [w_cat, q_norm[None, :], uq, kv_norm[None, :], wk, wuv, head_gain(q_qk), head_gain(k_qk)] + tabs


ATT_TQ = 512
ATT_KC = 256
HEADS_PER_STEP = 4


def _attn_kernel(q_ref, k_ref, v_ref, o_ref):
    for hh in range(HEADS_PER_STEP):
        sl = slice(hh * HEAD_PAD, (hh + 1) * HEAD_PAD)
        q = q_ref[0, :, sl]
        m = jnp.full((ATT_TQ, 1), -jnp.inf, F32)
        acc = jnp.zeros((ATT_TQ, HEAD_PAD), F32)
        for c in range(KV_LEN // ATT_KC):
            rows = slice(c * ATT_KC, (c + 1) * ATT_KC)
            s = lax.dot_general(q, k_ref[0, rows, sl], (((1,), (1,)), ((), ())), preferred_element_type=F32)
            m_new = jnp.maximum(m, jnp.max(s, axis=-1, keepdims=True))
            p = jnp.exp2(s - m_new).astype(BF16)
            acc = acc * jnp.exp2(m - m_new) + jnp.dot(p, v_ref[0, rows, sl], preferred_element_type=F32)
            m = m_new
        o_ref[0, :, sl] = (acc * (1.0 / acc[:, V_HEAD:V_HEAD + 1])).astype(BF16)


def _attn_call(q, k, v):
    wq = HEADS_PER_STEP * HEAD_PAD
    return pl.pallas_call(
        _attn_kernel, grid=(BATCH, MLA_HEADS // HEADS_PER_STEP, SEQ // ATT_TQ),
        in_specs=[pl.BlockSpec((1, ATT_TQ, wq), lambda b, h, i: (b, i, h)),
                  pl.BlockSpec((1, KV_LEN, wq), lambda b, h, i: (b, 0, h)),
                  pl.BlockSpec((1, KV_LEN, wq), lambda b, h, i: (b, 0, h))],
        out_specs=pl.BlockSpec((1, ATT_TQ, wq), lambda b, h, i: (b, i, h)),
        out_shape=jax.ShapeDtypeStruct((BATCH, SEQ, MLA_HEADS * HEAD_PAD), BF16),
        compiler_params=_params(("parallel", "parallel", "arbitrary")), name="mla_attention",
    )(q, k, v)


S5_ROWS = S5_NCHUNK * BATCH
S5_ROWS_LAT = S5_NCHUNK_LAT * BATCH
S5_COLS = S5_CHUNK * S5_GROUP
S5_SW = 2 * S5_STATE


def _s5_kernel(u_ref, r_ref, mb_ref, mc_ref, coef_ref, y_ref, x_sc, sp_sc, t_sc):
    u = u_ref[0]
    lags = r_ref[0]
    for sg in range(S5_CHUNK):
        off = (S5_CHUNK - 1 - sg) * S5_GROUP
        t_sc[sg * S5_GROUP:(sg + 1) * S5_GROUP, :] = lags[:, off:off + S5_COLS].astype(BF16)
    x_sc[...] = jnp.dot(u, mb_ref[0], preferred_element_type=F32)
    cf = coef_ref[0]
    af, bfm, bfp, ab, bbm, bbp = [cf[i * 8:(i + 1) * 8] for i in range(6)]

    def body(i, carry):
        sf, sfw, sb, sbw = carry
        cfw = jnp.where(i < S5_NCHUNK_CTX, i + S5_NCHUNK_LAT, i - S5_NCHUNK_CTX)
        rf = pl.multiple_of(cfw * BATCH, BATCH)
        rb = pl.multiple_of((S5_NCHUNK - 1 - i) * BATCH, BATCH)
        sp_sc[pl.ds(rf, BATCH), 0:S5_SW] = sf
        sp_sc[pl.ds(rb, BATCH), S5_SW:2 * S5_SW] = sb
        xf = x_sc[pl.ds(rf, BATCH), 0:S5_SW]
        xfw = x_sc[pl.ds(rf, BATCH), S5_SW:2 * S5_SW]
        xb = x_sc[pl.ds(rb, BATCH), 2 * S5_SW:3 * S5_SW]
        xbw = x_sc[pl.ds(rb, BATCH), 3 * S5_SW:4 * S5_SW]
        return (sf * af + sfw * bfm + xf, sfw * af + sf * bfp + xfw,
                sb * ab + sbw * bbm + xb, sbw * ab + sb * bbp + xbw)

    z = jnp.zeros((BATCH, S5_SW), F32)
    lax.fori_loop(0, S5_NCHUNK, body, (z, z, z, z))
    y = jnp.dot(u[:S5_ROWS_LAT], t_sc[...], preferred_element_type=F32)
    y = y + jnp.dot(sp_sc[0:S5_ROWS_LAT, :].astype(BF16), mc_ref[0], preferred_element_type=F32)
    y_ref[0] = y.astype(BF16)


def _s5_call(ug, t, mb, mc, coef):
    g3 = lambda g: (g, 0, 0)
    return pl.pallas_call(
        _s5_kernel, grid=(S5_GROUPS,),
        in_specs=[pl.BlockSpec((1, S5_ROWS, S5_COLS), g3),
                  pl.BlockSpec((1, S5_GROUP, 2 * S5_COLS), g3),
                  pl.BlockSpec((1, S5_COLS, 4 * S5_SW), g3),
                  pl.BlockSpec((1, 2 * S5_SW, S5_COLS), g3),
                  pl.BlockSpec((1, 6 * 8, S5_SW), g3)],
        out_specs=pl.BlockSpec((1, S5_ROWS_LAT, S5_COLS), g3),
        out_shape=jax.ShapeDtypeStruct((S5_GROUPS, S5_ROWS_LAT, S5_COLS), BF16),
        scratch_shapes=[pltpu.VMEM((S5_ROWS, 4 * S5_SW), F32), pltpu.VMEM((S5_ROWS, 2 * S5_SW), F32),
                        pltpu.VMEM((S5_COLS, S5_COLS), BF16)],
        compiler_params=_params(("parallel",)), name="s5_chunked_scan",
    )(ug, t, mb, mc, coef)


def _s5_weights(lam_re, lam_im, log_step, b_re, b_im, c_re, c_im):
    q = S5_CHUNK
    hi = lax.Precision.HIGHEST
    sig = np.arange(q)
    dirs = np.arange(2)[None, :]
    lr = jnp.minimum(lam_re, -1e-4)
    li = lam_im
    step = jnp.exp(log_step)[..., None]
    jj = jnp.arange(q + 1, dtype=F32)[:, None, None, None]
    mag = jnp.exp(lr * step * jj)
    ph = li * step * jj
    pr, pi = mag * jnp.cos(ph), mag * jnp.sin(ph)
    nr, ni = pr[1] - 1.0, pi[1]
    den = lr * lr + li * li
    fr, fi = (nr * lr + ni * li) / den, (ni * lr - nr * li) / den
    br = fr[..., None] * b_re - fi[..., None] * b_im
    bi = fr[..., None] * b_im + fi[..., None] * b_re
    cpr = c_re[None] * pr[:, :, :, None, :] - c_im[None] * pi[:, :, :, None, :]
    cpi = c_re[None] * pi[:, :, :, None, :] + c_im[None] * pr[:, :, :, None, :]
    kern = jnp.einsum('jdghp,dgpk->jdghk', jnp.concatenate([cpr[:q], -cpi[:q]], axis=-1),
                      jnp.concatenate([br, bi], axis=2), precision=hi)
    kt = kern.transpose(1, 2, 4, 0, 3)
    zero_slots = jnp.zeros((S5_GROUPS, S5_GROUP, q, S5_GROUP), F32)
    t = (jnp.concatenate([zero_slots[:, :, :q - 1], kt[0], zero_slots[:, :, :1]], axis=2)
         + jnp.concatenate([kt[1, :, :, ::-1], zero_slots], axis=2)).reshape(S5_GROUPS, S5_GROUP, 2 * S5_COLS)
    pw = np.stack([q - 1 - sig, sig], axis=1)
    po = np.stack([sig + 1, q - sig], axis=1)
    ppr, ppi = pr[pw, dirs][..., None], pi[pw, dirs][..., None]
    xr = (ppr * br[None] - ppi * bi[None]).transpose(1, 2, 0, 4, 3).reshape(2, S5_GROUPS, S5_COLS, S5_STATE)
    xi = (ppr * bi[None] + ppi * br[None]).transpose(1, 2, 0, 4, 3).reshape(2, S5_GROUPS, S5_COLS, S5_STATE)
    mb = jnp.concatenate([xr[0], xi[0], xi[0], xr[0], xr[1], xi[1], xi[1], xr[1]], axis=-1).astype(BF16)
    mr = cpr[po, dirs].transpose(1, 2, 4, 0, 3).reshape(2, S5_GROUPS, S5_STATE, S5_COLS)
    mi = -cpi[po, dirs].transpose(1, 2, 4, 0, 3).reshape(2, S5_GROUPS, S5_STATE, S5_COLS)
    mc = jnp.concatenate([mr[0], mi[0], mr[1], mi[1]], axis=1).astype(BF16)
    are, aim = pr[q], pi[q]
    rows = jnp.stack([jnp.concatenate([are, are], -1), jnp.concatenate([-aim, aim], -1),
                      jnp.concatenate([aim, -aim], -1)], axis=1)
    coef = jnp.broadcast_to(rows.transpose(2, 0, 1, 3)[:, :, :, None, :], (S5_GROUPS, 2, 3, 8, S5_SW))
    return t, mb, mc, coef.reshape(S5_GROUPS, 6 * 8, S5_SW)


A1_TL = MOE_TL


def _a1_kernel(att_ref, ys_ref, u_ref, d_ref, gw_ref, gb_ref, woa_ref, wos_ref, x_ref, g_ref, *route):
    o_ref = route[5]
    y = u_ref[0].astype(F32) * d_ref[...] + ys_ref[0].astype(F32)
    z = jax.nn.gelu(y)
    gate = jax.nn.sigmoid(jnp.dot(z.astype(BF16), gw_ref[...], preferred_element_type=F32) + gb_ref[...])
    s5 = (z * gate).astype(BF16)
    mix = jnp.dot(att_ref[0], woa_ref[...], preferred_element_type=F32)
    mix = mix + jnp.dot(s5, wos_ref[...], preferred_element_type=F32)
    x_new = x_ref[0] + g_ref[0] * mix
    o_ref[0] = x_new
    _route_tile(x_new, *route[:5], *route[6:])


def _a1_call(att, ys, u, d, gw, gb, woa, wos, x, g1, route):
    r_args, r_in, r_out, r_shape, r_scratch = route
    t3 = lambda b, j: (b, j, 0)
    full2 = lambda b, j: (0, 0)
    return pl.pallas_call(
        _a1_kernel, grid=(BATCH, SEQ // A1_TL),
        in_specs=[pl.BlockSpec((1, A1_TL, MLA_HEADS * HEAD_PAD), t3),
                  pl.BlockSpec((1, A1_TL, S5_WIDTH), t3),
                  pl.BlockSpec((1, A1_TL, S5_WIDTH), t3),
                  pl.BlockSpec((1, S5_WIDTH), full2),
                  pl.BlockSpec((S5_WIDTH, S5_WIDTH), full2),
                  pl.BlockSpec((1, S5_WIDTH), full2),
                  pl.BlockSpec((MLA_HEADS * HEAD_PAD, D_MODEL), full2),
                  pl.BlockSpec((S5_WIDTH, D_MODEL), full2),
                  pl.BlockSpec((1, A1_TL, D_MODEL), t3),
                  pl.BlockSpec((1, 1, D_MODEL), lambda b, j: (b, 0, 0))] + r_in,
        out_specs=[pl.BlockSpec((1, A1_TL, D_MODEL), t3)] + r_out,
        out_shape=[jax.ShapeDtypeStruct((BATCH, SEQ, D_MODEL), F32)] + r_shape,
        scratch_shapes=r_scratch,
        compiler_params=_params(("arbitrary", "arbitrary")), name="even_out_proj_route",
    )(att, ys, u, d, gw, gb, woa, wos, x, g1, *r_args)


SLOT_PAD = 8


def _pack_rows(v):
    halves = []
    for p in range(2):
        base = 2 * p * ROW_WORDS
        a = pltpu.bitcast(v[:, base:base + ROW_WORDS].astype(BF16).astype(F32), jnp.uint32)
        b = pltpu.bitcast(v[:, base + ROW_WORDS:base + 2 * ROW_WORDS].astype(BF16).astype(F32), jnp.uint32)
        halves.append((a >> 16) | b)
    return halves


def _unpack_rows(lo, hi):
    out = []
    for w in (lo, hi):
        out.append(pltpu.bitcast(w << 16, F32))
        out.append(pltpu.bitcast(w & jnp.uint32(0xFFFF0000), F32))
    return out


def _route_tile(x, sh_ref, sc_ref, rwt_ref, rb_ref, tri_ref,
                hlo_ref, hhi_ref, idx_ref, wt_ref, rank_ref, cnt_ref, run_sc):
    @pl.when((pl.program_id(0) == 0) & (pl.program_id(1) == 0))
    def _():
        run_sc[...] = jnp.zeros_like(run_sc)

    h = _norm_mod(x, sh_ref[0], sc_ref[0])
    hb = h.astype(BF16)
    hlo_ref[0], hhi_ref[0] = _pack_rows(h)
    h_lo = (h - hb.astype(F32)).astype(BF16)
    rwt = rwt_ref[...]
    rw_hi = rwt.astype(BF16)
    rw_lo = (rwt - rw_hi.astype(F32)).astype(BF16)
    logits = lax.dot_general(jnp.concatenate([rw_hi, rw_lo, rw_hi], axis=1), jnp.concatenate([hb, hb, h_lo], axis=1),
                             (((1,), (1,)), ((), ())), preferred_element_type=F32)
    scores = jax.nn.sigmoid(logits)

    work = scores + rb_ref[...]
    expert = lax.broadcasted_iota(jnp.int32, work.shape, 0).astype(F32)
    hits, ids = [], []
    for _ in range(TOP_K):
        m = jnp.max(work, axis=0, keepdims=True)
        ik = jnp.min(jnp.where(work == m, expert, float(N_EXPERTS)), axis=0, keepdims=True)
        hit = expert == ik
        hits.append(hit)
        ids.append(ik)
        work = jnp.where(hit, -jnp.inf, work)
    mask = hits[0]
    for hit in hits[1:]:
        mask = jnp.logical_or(mask, hit)
    maskf = mask.astype(F32)
    before = jnp.dot(maskf.astype(BF16), tri_ref[...], preferred_element_type=F32) + run_sc[:, 0:1]
    sel = [jnp.sum(jnp.where(hit, scores, 0.0), axis=0, keepdims=True) for hit in hits]
    denom = sel[0]
    for s in sel[1:]:
        denom = denom + s
    ranks = [jnp.sum(jnp.where(hit, before, 0.0), axis=0, keepdims=True) for hit in hits]
    pad = [jnp.zeros_like(denom)] * (SLOT_PAD - TOP_K)
    idx_ref[...] = jnp.concatenate(ids + pad, axis=0).astype(jnp.int32)
    wt_ref[...] = jnp.concatenate([s / denom * ROUTE_SCALE for s in sel] + pad, axis=0)
    rank_ref[...] = jnp.concatenate(ranks + pad, axis=0).astype(jnp.int32)
    run_sc[...] += jnp.sum(maskf, axis=1, keepdims=True)
    cnt_ref[...] = run_sc[...]


def _route_plumbing(sh, sc, rw, rb):
    t3 = lambda b, j: (b, j, 0)
    full2 = lambda b, j: (0, 0)
    per_b = lambda b, j: (b, 0, 0)
    nt = SEQ // MOE_TL
    slots = lambda b, j: (0, b * nt + j)
    tri = jnp.asarray(np.triu(np.ones((MOE_TL, MOE_TL), np.float32), 1), BF16)
    args = [sh, sc, rw.T, rb[:, None], tri]
    in_specs = [pl.BlockSpec((1, 1, D_MODEL), per_b),
                pl.BlockSpec((1, 1, D_MODEL), per_b),
                pl.BlockSpec((N_EXPERTS, D_MODEL), full2),
                pl.BlockSpec((N_EXPERTS, 1), full2),
                pl.BlockSpec((MOE_TL, MOE_TL), full2)]
    out_specs = [pl.BlockSpec((1, MOE_TL, ROW_WORDS), t3),
                 pl.BlockSpec((1, MOE_TL, ROW_WORDS), t3),
                 pl.BlockSpec((SLOT_PAD, MOE_TL), slots),
                 pl.BlockSpec((SLOT_PAD, MOE_TL), slots),
                 pl.BlockSpec((SLOT_PAD, MOE_TL), slots),
                 pl.BlockSpec((N_EXPERTS, V7X_LANES), full2)]
    slot_i = jax.ShapeDtypeStruct((SLOT_PAD, BATCH * SEQ), jnp.int32)
    out_shape = [jax.ShapeDtypeStruct((BATCH, SEQ, ROW_WORDS), jnp.uint32),
                 jax.ShapeDtypeStruct((BATCH, SEQ, ROW_WORDS), jnp.uint32),
                 slot_i,
                 jax.ShapeDtypeStruct((SLOT_PAD, BATCH * SEQ), F32),
                 slot_i,
                 jax.ShapeDtypeStruct((N_EXPERTS, V7X_LANES), F32)]
    scratch = [pltpu.VMEM((N_EXPERTS, V7X_LANES), F32)]
    return args, in_specs, out_specs, out_shape, scratch


def _sc_mesh():
    return plsc.VectorSubcoreMesh(core_axis_name="c", subcore_axis_name="s")


def _sc_dispatch(h_words, dest, n_rows):
    n_tok = h_words.shape[0]

    @pl.kernel(out_type=jax.ShapeDtypeStruct((n_rows, ROW_WORDS), jnp.uint32), mesh=_sc_mesh(), scratch_types=[])
    def scatter_rows(h_hbm, i_hbm, o_hbm):
        def body(h_vmem, i_vmem):
            for k in range(TOP_K):
                pltpu.sync_copy(h_vmem, o_hbm.at[i_vmem.at[k]])

        pltpu.emit_pipeline(
            body, grid=(n_tok // SC_WINDOW,),
            in_specs=[pl.BlockSpec((SC_WINDOW, ROW_WORDS), index_map=lambda i: (i, 0)),
                      pl.BlockSpec((SLOT_PAD, SC_WINDOW), index_map=lambda i: (0, i))],
            out_specs=[],
            core_axis_name=("c", "s"), dimension_semantics=(pltpu.PARALLEL,),
        )(h_hbm, i_hbm)

    return scatter_rows(h_words, dest)


def _sc_collect(y_words, dest):
    n_tok = dest.shape[1]

    @pl.kernel(out_type=jax.ShapeDtypeStruct((TOP_K, n_tok, ROW_WORDS), jnp.uint32), mesh=_sc_mesh(),
               scratch_types=[])
    def gather_rows(y_hbm, i_hbm, o_hbm):
        def body(i_vmem, o_vmem):
            pltpu.sync_copy(y_hbm.at[i_vmem.at[0]], o_vmem.at[0])

        pltpu.emit_pipeline(
            body, grid=(TOP_K, n_tok // SC_WINDOW),
            in_specs=[pl.BlockSpec((1, SC_WINDOW), index_map=lambda k, i: (k, i))],
            out_specs=[pl.BlockSpec((1, SC_WINDOW, ROW_WORDS), index_map=lambda k, i: (k, i, 0))],
            core_axis_name=("c", "s"), dimension_semantics=(pltpu.PARALLEL, pltpu.PARALLEL),
        )(i_hbm, o_hbm)

    return gather_rows(y_words, dest)


def _expert_kernel(be_ref, nv_ref, xlo_ref, xhi_ref, wg_ref, wu_ref, wd_ref, ylo_ref, yhi_ref,
                   wg_sc, wu_sc, wd_sc):
    i = pl.program_id(0)
    nv = nv_ref[i]

    @pl.when(jnp.logical_or(i == 0, be_ref[i] != be_ref[jnp.maximum(i - 1, 0)]))
    def _():
        wg_sc[...] = wg_ref[0, 0].astype(BF16)
        wu_sc[...] = wu_ref[0, 0].astype(BF16)
        wd_sc[...] = wd_ref[0, 0].astype(BF16)

    @pl.when(nv > 0)
    def _():
        parts = _unpack_rows(xlo_ref[...], xhi_ref[...])
        xb = jnp.concatenate([p.astype(BF16) for p in parts], axis=1)
        live = lax.broadcasted_iota(jnp.int32, xb.shape, 0) < nv
        xb = jnp.where(live, xb, jnp.zeros_like(xb))
        hid = jax.nn.silu(jnp.dot(xb, wg_sc[...], preferred_element_type=F32))
        hid = hid * jnp.dot(xb, wu_sc[...], preferred_element_type=F32)
        y = jnp.dot(hid.astype(BF16), wd_sc[...], preferred_element_type=F32)
        ylo_ref[...], yhi_ref[...] = _pack_rows(y)

    @pl.when(nv == 0)
    def _():
        ylo_ref[...] = jnp.zeros_like(ylo_ref)
        yhi_ref[...] = jnp.zeros_like(yhi_ref)


def _expert_call(block_e, n_valid, xlo, xhi, wg, wu, wd, li):
    n_rows = xlo.shape[0]
    n_blocks = n_rows // MOE_BLOCK
    rows = pl.BlockSpec((MOE_BLOCK, ROW_WORDS), lambda i, be, nv: (i, 0))
    grid_spec = pltpu.PrefetchScalarGridSpec(
        num_scalar_prefetch=2, grid=(n_blocks,),
        in_specs=[rows, rows,
                  pl.BlockSpec((1, 1, D_MODEL, EXPERT_FF), lambda i, be, nv: (li, be[i], 0, 0)),
                  pl.BlockSpec((1, 1, D_MODEL, EXPERT_FF), lambda i, be, nv: (li, be[i], 0, 0)),
                  pl.BlockSpec((1, 1, EXPERT_FF, D_MODEL), lambda i, be, nv: (li, be[i], 0, 0))],
        out_specs=[rows, rows],
        scratch_shapes=[pltpu.VMEM((D_MODEL, EXPERT_FF), BF16), pltpu.VMEM((D_MODEL, EXPERT_FF), BF16),
                        pltpu.VMEM((EXPERT_FF, D_MODEL), BF16)])
    out = jax.ShapeDtypeStruct((n_rows, ROW_WORDS), jnp.uint32)
    return pl.pallas_call(
        _expert_kernel, grid_spec=grid_spec, out_shape=[out, out],
        compiler_params=_params(("arbitrary",)), name="moe_experts",
    )(block_e, n_valid, xlo, xhi, wg, wu, wd)


def _combine_kernel(ylo_ref, yhi_ref, w_ref, x_ref, sh_ref, sc_ref, g_ref, sg_ref, su_ref, sd_ref, o_ref):
    hb = _norm_mod(x_ref[0], sh_ref[0], sc_ref[0]).astype(BF16)
    hid = jax.nn.silu(jnp.dot(hb, sg_ref[...].astype(BF16), preferred_element_type=F32))
    hid = hid * jnp.dot(hb, su_ref[...].astype(BF16), preferred_element_type=F32)
    shared = jnp.dot(hid.astype(BF16), sd_ref[...].astype(BF16), preferred_element_type=F32)
    w = w_ref[0]
    acc = [None] * 4
    for k in range(TOP_K):
        wk = w[:, k:k + 1]
        for c, part in enumerate(_unpack_rows(ylo_ref[k], yhi_ref[k])):
            acc[c] = wk * part if acc[c] is None else acc[c] + wk * part
    for c in range(4):
        sl = slice(c * ROW_WORDS, (c + 1) * ROW_WORDS)
        o_ref[0, :, sl] = x_ref[0, :, sl] + g_ref[0, :, sl] * (acc[c] + shared[:, sl])


def _combine_call(ylo, yhi, wts, x, sh, sc, g2, sg, su, sd):
    t3 = lambda b, j: (b, j, 0)
    per_b = lambda b, j: (b, 0, 0)
    full2 = lambda b, j: (0, 0)
    ff = sg.shape[1]
    nt = SEQ // MOE_TL
    rows = pl.BlockSpec((TOP_K, MOE_TL, ROW_WORDS), lambda b, j: (0, b * nt + j, 0))
    return pl.pallas_call(
        _combine_kernel, grid=(BATCH, nt),
        in_specs=[rows, rows,
                  pl.BlockSpec((1, MOE_TL, SLOT_PAD), t3),
                  pl.BlockSpec((1, MOE_TL, D_MODEL), t3),
                  pl.BlockSpec((1, 1, D_MODEL), per_b),
                  pl.BlockSpec((1, 1, D_MODEL), per_b),
                  pl.BlockSpec((1, 1, D_MODEL), per_b),
                  pl.BlockSpec((D_MODEL, ff), full2),
                  pl.BlockSpec((D_MODEL, ff), full2),
                  pl.BlockSpec((ff, D_MODEL), full2)],
        out_specs=pl.BlockSpec((1, MOE_TL, D_MODEL), t3),
        out_shape=jax.ShapeDtypeStruct((BATCH, SEQ, D_MODEL), F32),
        compiler_params=_params(("parallel", "arbitrary")), name="moe_combine_shared",
    )(ylo, yhi, wts, x, sh, sc, g2, sg, su, sd)


def _moe(x, routed, sh, sc, g2, w_gate, w_up, w_down, sh_gate, sh_up, sh_down, li):
    T = BATCH * SEQ
    TK = T * TOP_K
    hlo, hhi, idx, wts, rank, counts = routed
    wts = wts.T.reshape(BATCH, SEQ, SLOT_PAD)
    counts = counts[:, 0].astype(jnp.int32)
    padded = (counts + MOE_BLOCK - 1) // MOE_BLOCK * MOE_BLOCK
    pad_end = jnp.cumsum(padded)
    pad_start = pad_end - padded
    n_blocks = -(-TK // MOE_BLOCK) + N_EXPERTS
    n_rows = n_blocks * MOE_BLOCK
    block_start = jnp.arange(n_blocks, dtype=jnp.int32) * MOE_BLOCK
    owns = jnp.logical_and(block_start[:, None] >= pad_start[None, :], block_start[:, None] < pad_end[None, :])
    owns = owns.astype(jnp.int32)
    experts = jnp.arange(N_EXPERTS, dtype=jnp.int32)[None, :]
    block_e = jnp.sum(owns * experts, axis=1) + (N_EXPERTS - 1) * (1 - jnp.sum(owns, axis=1))
    n_valid = jnp.sum(owns * (counts[None, :] - (block_start[:, None] - pad_start[None, :])), axis=1)
    n_valid = jnp.clip(n_valid, 0, MOE_BLOCK).astype(jnp.int32)
    first_row = jnp.sum(jnp.where(idx[None] == experts.T[:, :, None], pad_start[:, None, None], 0), axis=0)
    dest = first_row + rank
    xlo = _sc_dispatch(hlo.reshape(T, ROW_WORDS), dest, n_rows)
    xhi = _sc_dispatch(hhi.reshape(T, ROW_WORDS), dest, n_rows)
    ylo, yhi = _expert_call(block_e, n_valid, xlo, xhi, w_gate, w_up, w_down, li)
    return _combine_call(_sc_collect(ylo, dest), _sc_collect(yhi, dest), wts, x, sh, sc, g2,
                         sh_gate, sh_up, sh_down)


HY_TL = MOE_TL
HALO = 8


def _hy_in_kernel(x_ref, xp_ref, xn_ref, sh_ref, sc_ref, w_ref, cw_ref, cb_ref, z_ref, x0_ref, h_sc):
    j = pl.program_id(1)
    shift, scale = sh_ref[0], sc_ref[0]
    keep_prev = (j > 0).astype(F32)
    keep_next = (j < SEQ // HY_TL - 1).astype(F32)
    h_sc[0:HALO, :] = _norm_mod(xp_ref[0], shift, scale) * keep_prev
    h_sc[HALO:HALO + HY_TL, :] = _norm_mod(x_ref[0], shift, scale)
    h_sc[HALO + HY_TL:, :] = _norm_mod(xn_ref[0], shift, scale) * keep_next
    hcat = h_sc[...].astype(BF16)
    outs = []
    for part in range(3):
        sl = slice(part * HY_WIDTH, (part + 1) * HY_WIDTH)
        p = jnp.dot(hcat, w_ref[:, sl], preferred_element_type=F32)
        o = (p[HALO - 1:HALO - 1 + HY_TL] * cw_ref[0:1, sl] + p[HALO:HALO + HY_TL] * cw_ref[1:2, sl]
             + p[HALO + 1:HALO + 1 + HY_TL] * cw_ref[2:3, sl] + cb_ref[:, sl])
        outs.append(o)
    x0_ref[0] = outs[0].astype(BF16)
    z_ref[0] = (outs[2] * outs[1]).astype(BF16)


def _hy_in_call(x, sh, sc, w, cw, cb):
    nb8 = HY_TL // HALO
    t3 = lambda b, j: (b, j, 0)
    full2 = lambda b, j: (0, 0)
    per_b = lambda b, j: (b, 0, 0)
    return pl.pallas_call(
        _hy_in_kernel, grid=(BATCH, SEQ // HY_TL),
        in_specs=[pl.BlockSpec((1, HY_TL, D_MODEL), t3),
                  pl.BlockSpec((1, HALO, D_MODEL), lambda b, j: (b, jnp.maximum(j * nb8 - 1, 0), 0)),
                  pl.BlockSpec((1, HALO, D_MODEL), lambda b, j: (b, jnp.minimum((j + 1) * nb8, SEQ // HALO - 1), 0)),
                  pl.BlockSpec((1, 1, D_MODEL), per_b),
                  pl.BlockSpec((1, 1, D_MODEL), per_b),
                  pl.BlockSpec((D_MODEL, 3 * HY_WIDTH), full2),
                  pl.BlockSpec((SHORT_CONV, 3 * HY_WIDTH), full2),
                  pl.BlockSpec((1, 3 * HY_WIDTH), full2)],
        out_specs=[pl.BlockSpec((1, HY_TL, HY_WIDTH), t3), pl.BlockSpec((1, HY_TL, HY_WIDTH), t3)],
        out_shape=[jax.ShapeDtypeStruct((BATCH, SEQ, HY_WIDTH), BF16),
                   jax.ShapeDtypeStruct((BATCH, SEQ, HY_WIDTH), BF16)],
        scratch_shapes=[pltpu.VMEM((HY_TL + 2 * HALO, D_MODEL), F32)],
        compiler_params=_params(("parallel", "arbitrary")), name="hyena_in_proj",
    )(x, x, x, sh, sc, w, cw, cb)


def _fft_tables():
    c = np.arange(FFT_N2, dtype=np.int64)
    ang = 2.0 * np.pi * ((c[:, None] * c[None, :]) % FFT_N2) / FFT_N2
    sr, si = np.cos(ang), -np.sin(ang)
    m = np.block([[sr, -si], [si, sr]])
    k1 = np.arange(FFT_NK, dtype=np.int64)
    ang_t = 2.0 * np.pi * (k1[:, None] * c[None, :]) / DFT_N
    lanes = np.ones((1, 1, V7X_LANES))
    tr = np.cos(ang_t)[:, :, None] * lanes
    ti = -np.sin(ang_t)[:, :, None] * lanes
    return jnp.asarray(m, BF16), jnp.asarray(tr, F32), jnp.asarray(ti, F32)


def _lin(acc, coef, val):
    if abs(coef) < 1e-12:
        return acc
    term = val if coef == 1.0 else (-val if coef == -1.0 else coef * val)
    return term if acc is None else acc + term


def _twiddle(tr_ref, ti_ref, k1, width):
    reps = width // V7X_LANES
    tr, ti = tr_ref[k1], ti_ref[k1]
    return jnp.concatenate([tr] * reps, axis=1), jnp.concatenate([ti] * reps, axis=1)


_COS_PI_4 = math.sqrt(0.5)


def _blocks_to_classes(z):
    assert FFT_N1 == 16
    sp = [z[a] + z[a + 4] for a in range(4)]
    sm = [z[a] - z[a + 4] for a in range(4)]
    q0p, q0m, q1p, q1m = sp[0] + sp[2], sp[0] - sp[2], sp[1] + sp[3], sp[1] - sp[3]
    d, e = _COS_PI_4 * (sm[1] - sm[3]), _COS_PI_4 * (sm[1] + sm[3])
    out = [None] * FFT_NK
    out[0] = (q0p + q1p, None)
    out[8] = (q0p - q1p, None)
    out[4] = (q0m, -q1m)
    out[2] = (sm[0] + d, -(sm[2] + e))
    out[6] = (sm[0] - d, sm[2] - e)
    for k in (1, 3):
        halves = []
        for parity in (0, 1):
            hr = hi = None
            for a in range(parity, FFT_NA, 2):
                th = 2.0 * math.pi * ((a * k) % FFT_N1) / FFT_N1
                hr = _lin(hr, round(math.cos(th), 15), z[a])
                hi = _lin(hi, round(-math.sin(th), 15), z[a])
            halves.append((hr, hi))
        (er, ei), (odr, odi) = halves
        out[k] = (er + odr, ei + odi)
        out[8 - k] = (er - odr, odi - ei)
    return out


def _classes_to_blocks(v):
    assert FFT_N1 == 16
    base = (v[0][0] + v[8][0], v[0][0] - v[8][0])
    pr = {k: (v[k][0] + v[8 - k][0], v[k][0] - v[8 - k][0]) for k in (1, 2, 3)}
    pi = {k: (v[k][1] - v[8 - k][1], v[k][1] + v[8 - k][1]) for k in (1, 2, 3)}
    out = []
    for a in range(FFT_NA):
        odd = a % 2
        mid = (v[4][0], 1.0 - (a % 4)) if not odd else (v[4][1], (a % 4) - 2.0)
        acc = _lin(base[odd], mid[1], mid[0])
        for k in (1, 2, 3):
            th = 2.0 * math.pi * ((a * k) % FFT_N1) / FFT_N1
            acc = _lin(acc, round(math.cos(th), 15), pr[k][odd])
            acc = _lin(acc, round(-math.sin(th), 15), pi[k][odd])
        out.append(acc)
    return out


def _class_forward(yr, yi, k1, m_ref, tr_ref, ti_ref, width):
    if k1 == 0:
        x = jnp.dot(m_ref[:, :FFT_N2], yr.astype(BF16), preferred_element_type=F32)
    else:
        tr, ti = _twiddle(tr_ref, ti_ref, k1, width)
        yr, yi = (yr * tr, yr * ti) if yi is None else (yr * tr - yi * ti, yr * ti + yi * tr)
        x = jnp.dot(m_ref[...], jnp.concatenate([yr, yi], axis=0).astype(BF16), preferred_element_type=F32)
    return x[:FFT_N2], x[FFT_N2:]


def _class_backward(pr, pi, k1, m_ref, tr_ref, ti_ref, width):
    v = jnp.dot(m_ref[...], jnp.concatenate([pr, -pi], axis=0).astype(BF16), preferred_element_type=F32)
    ur, ui = v[:FFT_N2], -v[FFT_N2:]
    if k1 > 0:
        tr, ti = _twiddle(tr_ref, ti_ref, k1, width)
        ur, ui = ur * tr + ui * ti, ui * tr - ur * ti
    return ur, ui


def _filter_features():
    t = np.linspace(0.0, 1.0, SEQ)[:, None]
    ang = 2.0 * np.pi * np.arange(SEQ)[:, None] / SEQ
    bands = np.linspace(1e-4, FILT_BANDS - 1, FILT_BANDS)
    z = np.concatenate([t, np.cos(bands * ang), -np.sin(bands * ang)], axis=-1)
    return jnp.asarray(np.pad(z, ((0, 0), (0, FILT_PAD - FILT_EMB))), F32)


def _spec_kernel(zf_ref, w1_ref, b1_ref, w2_ref, b2_ref, fq_ref, w3f_ref, w3b_ref, dl_ref, m_ref, tr_ref, ti_ref,
                 c_ref, hf_sc, hb_sc):
    hi = lax.Precision.HIGHEST
    zf, fq = zf_ref[...], fq_ref[...]
    hid = jnp.sin(fq * (jnp.dot(zf, w1_ref[...], preferred_element_type=F32, precision=hi) + b1_ref[...]))
    hid = jnp.sin(fq * (jnp.dot(hid, w2_ref[...], preferred_element_type=F32, precision=hi) + b2_ref[...]))
    decay = jnp.exp(-zf[:, 0:1] * dl_ref[...])
    hf = jnp.dot(hid, w3f_ref[...], preferred_element_type=F32, precision=hi) * decay
    hb = jnp.dot(hid, w3b_ref[...], preferred_element_type=F32, precision=hi) * decay
    ssq = jnp.sum(hf * hf, axis=0, keepdims=True) + jnp.sum(hb * hb, axis=0, keepdims=True)
    inv = lax.rsqrt(ssq + EPS)
    hf_sc[...] = hf * inv
    hb_sc[...] = hb * inv
    rows = lambda ref: [ref[a * FFT_N2:(a + 1) * FFT_N2, :] for a in range(FFT_NA)]
    fwd, bwd = _blocks_to_classes(rows(hf_sc)), _blocks_to_classes(rows(hb_sc))
    for k1 in range(FFT_NK):
        fr, fi = _class_forward(*fwd[k1], k1, m_ref, tr_ref, ti_ref, HY_CT)
        br, bi = _class_forward(*bwd[k1], k1, m_ref, tr_ref, ti_ref, HY_CT)
        scale = (1.0 if k1 in (0, FFT_N1 // 2) else 2.0) / DFT_N
        c_ref[k1, :FFT_N2, :] = ((fr + br) * scale).astype(BF16)
        c_ref[k1, FFT_N2:, :] = ((fi - bi) * scale).astype(BF16)


def _fft_table_specs(ngrid):
    z = (0,) * 2
    z3 = (0,) * 3
    if ngrid == 1:
        return [pl.BlockSpec((2 * FFT_N2, 2 * FFT_N2), lambda c: z),
                pl.BlockSpec((FFT_NK, FFT_N2, V7X_LANES), lambda c: z3),
                pl.BlockSpec((FFT_NK, FFT_N2, V7X_LANES), lambda c: z3)]
    return [pl.BlockSpec((2 * FFT_N2, 2 * FFT_N2), lambda b, c: z),
            pl.BlockSpec((FFT_NK, FFT_N2, V7X_LANES), lambda b, c: z3),
            pl.BlockSpec((FFT_NK, FFT_N2, V7X_LANES), lambda b, c: z3)]


def _spec_call(w1, b1, w2, b2, w3, freq, m, tr, ti):
    nct = HY_WIDTH // HY_CT

    def pad2(a, rows, cols):
        return jnp.pad(a, ((0, rows - a.shape[0]), (0, cols - a.shape[1])))

    full = lambda c: (0, 0)
    small = pl.BlockSpec((FILT_PAD, FILT_PAD), full)
    vec = pl.BlockSpec((1, FILT_PAD), full)
    deltas = jnp.asarray(np.linspace(HY_MIN_DECAY, HY_MAX_DECAY, HY_WIDTH)[None, :], F32)
    w3p = pad2(w3, FILT_PAD, 2 * HY_WIDTH)
    return pl.pallas_call(
        _spec_kernel, grid=(nct,),
        in_specs=[pl.BlockSpec((SEQ, FILT_PAD), full), small, vec, small, vec, vec,
                  pl.BlockSpec((FILT_PAD, HY_CT), lambda c: (0, c)),
                  pl.BlockSpec((FILT_PAD, HY_CT), lambda c: (0, c + nct)),
                  pl.BlockSpec((1, HY_CT), lambda c: (0, c))] + _fft_table_specs(1),
        out_specs=pl.BlockSpec((FFT_NK, 2 * FFT_N2, HY_CT), lambda c: (0, 0, c)),
        out_shape=jax.ShapeDtypeStruct((FFT_NK, 2 * FFT_N2, HY_WIDTH), BF16),
        scratch_shapes=[pltpu.VMEM((SEQ, HY_CT), F32), pltpu.VMEM((SEQ, HY_CT), F32)],
        compiler_params=_params(("arbitrary",)), name="hyena_filter_spectrum",
    )(_filter_features(), pad2(w1, FILT_PAD, FILT_PAD), pad2(b1[None, :], 1, FILT_PAD),
      pad2(w2, FILT_PAD, FILT_PAD), pad2(b2[None, :], 1, FILT_PAD), pad2(freq[None, :], 1, FILT_PAD),
      w3p, w3p, deltas, m, tr, ti)


def _conv_kernel(z_ref, c_ref, m_ref, tr_ref, ti_ref, y_ref, cls):
    for k1, (yr, yi) in enumerate(_blocks_to_classes(
            [z_ref[0, a * FFT_N2:(a + 1) * FFT_N2, :].astype(F32) for a in range(FFT_NA)])):
        cls[k1, :FFT_N2, :] = yr
        if yi is not None:
            cls[k1, FFT_N2:, :] = yi
    for k1 in range(FFT_NK):
        yi = None if k1 in (0, FFT_N1 // 2) else cls[k1, FFT_N2:, :]
        xr, xi = _class_forward(cls[k1, :FFT_N2, :], yi, k1, m_ref, tr_ref, ti_ref, HY_CT)
        cr = c_ref[k1, :FFT_N2, :].astype(F32)
        ci = c_ref[k1, FFT_N2:, :].astype(F32)
        ur, ui = _class_backward(xr * cr - xi * ci, xr * ci + xi * cr, k1, m_ref, tr_ref, ti_ref, HY_CT)
        cls[k1, :FFT_N2, :] = ur
        cls[k1, FFT_N2:, :] = ui
    blocks = _classes_to_blocks([(cls[k1, :FFT_N2, :], cls[k1, FFT_N2:, :]) for k1 in range(FFT_NK)])
    for a, ya in enumerate(blocks):
        y_ref[0, a * FFT_N2:(a + 1) * FFT_N2, :] = ya.astype(BF16)


def _conv_call(z, spec, m, tr, ti):
    return pl.pallas_call(
        _conv_kernel, grid=(HY_WIDTH // HY_CT, BATCH),
        in_specs=[pl.BlockSpec((1, SEQ, HY_CT), lambda c, b: (b, 0, c)),
                  pl.BlockSpec((FFT_NK, 2 * FFT_N2, HY_CT), lambda c, b: (0, 0, c))] + _fft_table_specs(2),
        out_specs=pl.BlockSpec((1, SEQ, HY_CT), lambda c, b: (b, 0, c)),
        out_shape=jax.ShapeDtypeStruct((BATCH, SEQ, HY_WIDTH), BF16),
        scratch_shapes=[pltpu.VMEM((FFT_NK, 2 * FFT_N2, HY_CT), F32)],
        compiler_params=_params(("parallel", "arbitrary")), name="hyena_long_conv",
    )(z, spec, m, tr, ti)


def _hy_out_kernel(y_ref, z_ref, x0_ref, b_ref, w_ref, x_ref, g_ref, *route):
    o_ref = route[5]
    z = z_ref[0].astype(F32)
    gated = x0_ref[0].astype(F32) * (y_ref[0].astype(F32) + b_ref[...] * z)
    mix = jnp.dot(gated.astype(BF16), w_ref[...], preferred_element_type=F32)
    x_new = x_ref[0] + g_ref[0] * mix
    o_ref[0] = x_new
    _route_tile(x_new, *route[:5], *route[6:])


def _hy_out_call(y, z, x0, bias, w, x, g1, route):
    r_args, r_in, r_out, r_shape, r_scratch = route
    t3 = lambda b, j: (b, j, 0)
    full2 = lambda b, j: (0, 0)
    return pl.pallas_call(
        _hy_out_kernel, grid=(BATCH, SEQ // HY_TL),
        in_specs=[pl.BlockSpec((1, HY_TL, HY_WIDTH), t3),
                  pl.BlockSpec((1, HY_TL, HY_WIDTH), t3),
                  pl.BlockSpec((1, HY_TL, HY_WIDTH), t3),
                  pl.BlockSpec((1, HY_WIDTH), full2),
                  pl.BlockSpec((HY_WIDTH, D_MODEL), full2),
                  pl.BlockSpec((1, HY_TL, D_MODEL), t3),
                  pl.BlockSpec((1, 1, D_MODEL), lambda b, j: (b, 0, 0))] + r_in,
        out_specs=[pl.BlockSpec((1, HY_TL, D_MODEL), t3)] + r_out,
        out_shape=[jax.ShapeDtypeStruct((BATCH, SEQ, D_MODEL), F32)] + r_shape,
        scratch_shapes=r_scratch,
        compiler_params=_params(("arbitrary", "arbitrary")), name="hyena_out_proj_route",
    )(y, z, x0, bias, w, x, g1, *r_args)


def kernel(x, c, ctx, c_ctx, ada_w, ada_b, ev_w_in, mla_q_norm, mla_w_uq, mla_kv_norm, mla_w_ukv, mla_q_qknorm, mla_k_qknorm, s5_lam_re, s5_lam_im, s5_log_step, s5_b_re, s5_b_im, s5_c_re, s5_c_im, s5_d, s5_glu_w, s5_glu_b, ev_w_out, hy_w_in, hy_conv_w, hy_conv_b, hy_f_w1, hy_f_b1, hy_f_w2, hy_f_b2, hy_f_w3, hy_f_freq, hy_bias, hy_w_out, moe_router_w, moe_router_b, moe_w_gate, moe_w_up, moe_w_down, moe_sh_gate, moe_sh_up, moe_sh_down):
    D = D_MODEL
    mod_all = _ada_call(c, c_ctx, ada_w, ada_b)

    def mods(li):
        return [m[:, None, :] for m in jnp.split(mod_all[li, :BATCH], 6, axis=-1)]

    sh1, sc1, g1, sh2, sc2, g2 = mods(0)
    mod_ctx = mod_all[0, BATCH, :2 * D]
    w0 = _a0_weights(ev_w_in[0], mla_q_norm[0], mla_w_uq[0], mla_kv_norm[0], mla_w_ukv[0],
                     mla_q_qknorm[0], mla_k_qknorm[0])
    q, k, v, u = _a0_call(x, ctx, sh1, sc1, mod_ctx[None, :D], mod_ctx[None, D:], w0)
    att = _attn_call(q, k, v)
    ug = u.reshape(BATCH, S5_NCHUNK, S5_CHUNK, S5_GROUPS, S5_GROUP)
    ug = ug.transpose(3, 1, 0, 2, 4).reshape(S5_GROUPS, S5_ROWS, S5_COLS)
    ys = _s5_call(ug, *_s5_weights(s5_lam_re[0], s5_lam_im[0], s5_log_step[0], s5_b_re[0], s5_b_im[0],
                                   s5_c_re[0], s5_c_im[0]))
    ys = ys.reshape(S5_GROUPS, S5_NCHUNK_LAT, BATCH, S5_CHUNK, S5_GROUP)
    ys = ys.transpose(2, 1, 3, 0, 4).reshape(BATCH, SEQ, S5_WIDTH)
    wo = ev_w_out[0].astype(BF16)
    wo_att = jnp.concatenate([wo[:MLA_WIDTH].reshape(MLA_HEADS, V_HEAD, D_MODEL),
                              jnp.zeros((MLA_HEADS, HEAD_PAD - V_HEAD, D_MODEL), BF16)], axis=1)
    x, *routed = _a1_call(att, ys, u, s5_d[0][None, :], s5_glu_w[0].astype(BF16), s5_glu_b[0][None, :],
                          wo_att.reshape(MLA_HEADS * HEAD_PAD, D_MODEL), wo[MLA_WIDTH:], x, g1,
                          _route_plumbing(sh2, sc2, moe_router_w[0], moe_router_b[0]))
    x = _moe(x, routed, sh2, sc2, g2, moe_w_gate, moe_w_up, moe_w_down,
             moe_sh_gate[0], moe_sh_up[0], moe_sh_down[0], 0)

    sh1, sc1, g1, sh2, sc2, g2 = mods(1)
    z, x0 = _hy_in_call(x, sh1, sc1, hy_w_in[0].astype(BF16), hy_conv_w[0], hy_conv_b[0][None, :])
    fft_tabs = _fft_tables()
    spectrum = _spec_call(hy_f_w1[0], hy_f_b1[0], hy_f_w2[0], hy_f_b2[0], hy_f_w3[0], hy_f_freq[0], *fft_tabs)
    y = _conv_call(z, spectrum, *fft_tabs)
    x, *routed = _hy_out_call(y, z, x0, hy_bias[0][None, :], hy_w_out[0].astype(BF16), x, g1,
                              _route_plumbing(sh2, sc2, moe_router_w[1], moe_router_b[1]))
    x = _moe(x, routed, sh2, sc2, g2, moe_w_gate, moe_w_up, moe_w_down,
             moe_sh_gate[1], moe_sh_up[1], moe_sh_down[1], 1)
    return x
```
